```python
import math, functools
import jax, jax.numpy as jnp
from jax import lax
import numpy as np

D_MODEL = 1024
BATCH = 4
SEQ = 4096
DEPTH = 1
DEC_BATCH = 128
DEC_SEQ = 1
PAST_LEN = 2048
PAGE_SIZE = 128

HEAD_DIM = 128
N_HEADS = D_MODEL // HEAD_DIM
GDN_HEADS = N_HEADS // 2
ATT_HEADS = N_HEADS - GDN_HEADS
N_KV = 2
GDN_W = GDN_HEADS * HEAD_DIM
ATT_W = ATT_HEADS * HEAD_DIM
KV_W = N_KV * HEAD_DIM
CONV_W = 4
CHUNK = 64
IDX_HEADS = 8
IDX_DIM = 64
TOPK_MAX = 256
QBLOCK = 128
ROPE_THETA = 500000.0
D_FF = ((8 * D_MODEL + 2) // 3 + 255) // 256 * 256
EPS = 1e-6
IN_SIZES = (3 * GDN_W, GDN_W, GDN_HEADS, GDN_HEADS, ATT_W, KV_W, KV_W, IDX_HEADS * IDX_DIM, IDX_DIM, IDX_HEADS)
IN_COLS = 3 * GDN_W + GDN_W + 2 * GDN_HEADS + ATT_W + 2 * KV_W + IDX_HEADS * IDX_DIM + IDX_DIM + IDX_HEADS

kernel_name = 'hybrid_gdn_dsa_adaln_step'


def _rmsnorm(x, g):
    xf = x.astype(jnp.float32)
    y = xf * lax.rsqrt(jnp.mean(xf * xf, axis=-1, keepdims=True) + EPS)
    return (y * g.astype(jnp.float32)).astype(x.dtype)


def _l2norm(x):
    return x * lax.rsqrt(jnp.sum(x * x, axis=-1, keepdims=True) + EPS)


def _partial_rope(x, pos):
    d = x.shape[-1]
    half = d // 8
    rot = 2 * half
    inv = ROPE_THETA ** (-jnp.arange(half, dtype=jnp.float32) / half)
    ang = pos.astype(jnp.float32)[:, None] * inv[None, :]
    cos = jnp.cos(ang)[:, None, :]
    sin = jnp.sin(ang)[:, None, :]
    xf = x.astype(jnp.float32)
    x1, x2 = xf[..., :half], xf[..., half:rot]
    out = jnp.concatenate([x1 * cos - x2 * sin, x2 * cos + x1 * sin, xf[..., rot:]], axis=-1)
    return out.astype(x.dtype)


def _modulation(c, w_ada, b_ada):
    mod = jax.nn.silu(c) @ w_ada + b_ada
    return [m[:, None, :] for m in jnp.split(mod, 6, axis=-1)]


def _delta_rule_chunked(q, k, v, g, beta, s0):
    b, t, h, _ = q.shape
    dv = v.shape[-1]
    n = t // CHUNK

    def blocks(a):
        return jnp.swapaxes(a, 1, 2).reshape((b, h, n, CHUNK) + a.shape[3:])

    qc, kc, vc, gc, bc = blocks(q), blocks(k), blocks(v), blocks(g), blocks(beta)
    gc = jnp.cumsum(gc, axis=-1)
    causal = jnp.tril(jnp.ones((CHUNK, CHUNK), bool))
    strict = jnp.tril(jnp.ones((CHUNK, CHUNK), bool), -1)
    decay = jnp.exp(jnp.where(causal, gc[..., :, None] - gc[..., None, :], -jnp.inf))
    kb = kc * bc[..., None]
    lower = jnp.where(strict, jnp.einsum('bhncd,bhnsd->bhncs', kb, kc) * decay, 0.0)
    m = lower + jnp.eye(CHUNK, dtype=q.dtype)
    u = lax.linalg.triangular_solve(m, vc * bc[..., None], left_side=True, lower=True, unit_diagonal=True)
    w = lax.linalg.triangular_solve(m, kb * jnp.exp(gc)[..., None], left_side=True, lower=True, unit_diagonal=True)
    intra = jnp.einsum('bhncd,bhnsd->bhncs', qc, kc) * decay

    def step(s, xs):
        q_i, k_i, u_i, w_i, a_i, g_i = xs
        v_new = u_i - jnp.einsum('bhcd,bhde->bhce', w_i, s)
        o = (jnp.einsum('bhcd,bhde->bhce', q_i * jnp.exp(g_i)[..., None], s)
             + jnp.einsum('bhcs,bhse->bhce', a_i, v_new))
        g_last = g_i[..., -1]
        s = (s * jnp.exp(g_last)[..., None, None]
             + jnp.einsum('bhcd,bhce->bhde', k_i * jnp.exp(g_last[..., None] - g_i)[..., None], v_new))
        return s, o

    xs = tuple(jnp.moveaxis(a, 2, 0) for a in (qc, kc, u, w, intra, gc))
    s, o = lax.scan(step, s0, xs)
    o = jnp.transpose(o, (1, 0, 3, 2, 4)).reshape(b, t, h, dv)
    return o, s


def _delta_rule_recurrent(q, k, v, g, beta, s0):
    def step(s, xs):
        q_t, k_t, v_t, g_t, b_t = xs
        s = s * jnp.exp(g_t)[..., None, None]
        delta = (v_t - jnp.einsum('bhd,bhde->bhe', k_t, s)) * b_t[..., None]
        s = s + jnp.einsum('bhd,bhe->bhde', k_t, delta)
        return s, jnp.einsum('bhd,bhde->bhe', q_t, s)

    xs = tuple(jnp.moveaxis(a, 1, 0) for a in (q, k, v, g, beta))
    s, o = lax.scan(step, s0, xs)
    return jnp.moveaxis(o, 0, 1), s


def _index_scores(iq, ik, iw):
    dots = jnp.einsum('bqhd,bsd->bqhs', iq.astype(jnp.float32), ik.astype(jnp.float32))
    return jnp.einsum('bqh,bqhs->bqs', iw.astype(jnp.float32), jax.nn.relu(dots))


def _attend_selected(q, ks, vs, valid):
    b, nq, h, d = q.shape
    qg = q.reshape(b, nq, N_KV, h // N_KV, d)
    s = jnp.einsum('bqgrd,bqkgd->bqgrk', qg, ks).astype(jnp.float32) * d ** -0.5
    s = jnp.where(valid[:, :, None, None, :], s, -jnp.inf)
    p = jax.nn.softmax(s, axis=-1).astype(vs.dtype)
    return jnp.einsum('bqgrk,bqkgd->bqgrd', p, vs).reshape(b, nq, h * d)


def _sparse_attn_prompt(q, k, v, iq, ik, iw):
    b, t = q.shape[:2]
    topk = min(TOPK_MAX, t // 4)
    nblk = t // QBLOCK
    key_pos = jnp.arange(t)

    def to_blocks(a):
        return jnp.moveaxis(a.reshape((b, nblk, QBLOCK) + a.shape[2:]), 1, 0)

    def block(xs):
        q_b, iq_b, iw_b, t0 = xs
        qpos = t0 + jnp.arange(QBLOCK)
        score = _index_scores(iq_b, ik, iw_b)
        score = jnp.where(key_pos[None, None, :] <= qpos[None, :, None], score, -jnp.inf)
        _, idx = lax.top_k(score, topk)
        valid = idx <= qpos[None, :, None]
        ks = jax.vmap(lambda kk, ii: kk[ii])(k, idx)
        vs = jax.vmap(lambda vv, ii: vv[ii])(v, idx)
        return _attend_selected(q_b, ks, vs, valid)

    starts = jnp.arange(nblk, dtype=jnp.int32) * QBLOCK
    o = lax.map(block, (to_blocks(q), to_blocks(iq), to_blocks(iw), starts))
    return jnp.moveaxis(o, 0, 1).reshape(b, t, -1)


def _sparse_attn_sample(q, k, v, iq, ik, iw, cache_k, cache_v, cache_ik, page_table):
    b, tn = q.shape[:2]
    page = cache_k.shape[1]
    past = page_table.shape[1] * page
    n_keys = past + tn
    topk = min(TOPK_MAX, n_keys // 4)
    ik_past = cache_ik[page_table].reshape(b, past, IDX_DIM)
    ik_all = jnp.concatenate([ik_past.astype(ik.dtype), ik], axis=1)
    qpos = past + jnp.arange(tn)
    score = _index_scores(iq, ik_all, iw)
    score = jnp.where(jnp.arange(n_keys)[None, None, :] <= qpos[None, :, None], score, -jnp.inf)
    _, idx = lax.top_k(score, topk)
    valid = idx <= qpos[None, :, None]
    pidx = jnp.minimum(idx, past - 1)
    phys = jnp.take_along_axis(page_table, (pidx // page).reshape(b, -1), axis=1).reshape(pidx.shape)
    off = pidx % page
    nidx = jnp.clip(idx - past, 0, tn - 1)
    in_past = (idx < past)[..., None, None]
    ks = jnp.where(in_past, cache_k[phys, off].astype(k.dtype), jax.vmap(lambda kk, ii: kk[ii])(k, nidx))
    vs = jnp.where(in_past, cache_v[phys, off].astype(v.dtype), jax.vmap(lambda vv, ii: vv[ii])(v, nidx))
    return _attend_selected(q, ks, vs, valid)


def _layer(x, c, pos, conv_prefix, s0, sparse_attn, delta_rule,
           w_ada, b_ada, g_norm1, w_in, w_conv, a_log, dt_bias, g_gdn_norm, w_out, g_norm2, w_ffn_in, w_ffn_out):
    b, t, _ = x.shape
    sh1, sc1, ga1, sh2, sc2, ga2 = _modulation(c, w_ada, b_ada)
    h = _rmsnorm(x, g_norm1) * (1.0 + sc1) + sh1
    cuts = [int(n) for n in np.cumsum(IN_SIZES)[:-1]]
    qkv, z, beta_in, a_in, aq, ak, av, iq, ik, iw = jnp.split(h @ w_in, cuts, axis=-1)
    xcat = jnp.concatenate([conv_prefix.astype(qkv.dtype), qkv], axis=1)
    conv = sum(w_conv[i] * xcat[:, i:i + t] for i in range(CONV_W))
    conv = jax.nn.silu(conv).astype(jnp.float32).reshape(b, t, 3 * GDN_HEADS, HEAD_DIM)
    gq, gk, gv = jnp.split(conv, 3, axis=2)
    gq = _l2norm(gq) * HEAD_DIM ** -0.5
    gk = _l2norm(gk)
    beta = jax.nn.sigmoid(beta_in.astype(jnp.float32))
    g = -jnp.exp(a_log.astype(jnp.float32)) * jax.nn.softplus(a_in.astype(jnp.float32) + dt_bias.astype(jnp.float32))
    o_gdn, s_new = delta_rule(gq, gk, gv, g, beta, s0.astype(jnp.float32))
    o_gdn = _rmsnorm(o_gdn, g_gdn_norm) * jax.nn.silu(z.astype(jnp.float32).reshape(b, t, GDN_HEADS, HEAD_DIM))
    aq = _partial_rope(aq.reshape(b, t, ATT_HEADS, HEAD_DIM), pos)
    ak = _partial_rope(ak.reshape(b, t, N_KV, HEAD_DIM), pos)
    av = av.reshape(b, t, N_KV, HEAD_DIM)
    iq = _partial_rope(iq.reshape(b, t, IDX_HEADS, IDX_DIM), pos)
    ik = _partial_rope(ik.reshape(b, t, 1, IDX_DIM), pos)[:, :, 0]
    iw = iw * IDX_HEADS ** -0.5
    o_att = sparse_attn(aq, ak, av, iq, ik, iw)
    mixed = jnp.concatenate([o_gdn.reshape(b, t, GDN_W).astype(x.dtype), o_att.astype(x.dtype)], axis=-1) @ w_out
    x = x + ga1 * mixed
    h2 = _rmsnorm(x, g_norm2) * (1.0 + sc2) + sh2
    gate, up = jnp.split(h2 @ w_ffn_in, 2, axis=-1)
    x = x + ga2 * ((jax.nn.silu(gate) * up) @ w_ffn_out)
    return x, (ak, av, ik, xcat[:, t:], s_new)


def _stack(states, i):
    return jnp.stack([s[i] for s in states], axis=0)


def setup_inputs(seed: int = 0) -> dict:
    key = jax.random.key(seed)
    ks = jax.random.split(key, 24)
    n_pages = PAST_LEN // PAGE_SIZE
    n_used = DEC_BATCH * n_pages
    n_phys = n_used + max(1, n_used // 4)
    f32 = jnp.float32

    def nrm(k, shape, scale):
        return jax.random.normal(k, shape, f32) * scale

    dt = jnp.exp(jax.random.uniform(ks[14], (DEPTH, GDN_HEADS), f32, math.log(1e-3), math.log(1e-1)))
    return {
        'x_prompt': nrm(ks[0], (BATCH, SEQ, D_MODEL), 1.0),
        'x_sample': nrm(ks[1], (DEC_BATCH, DEC_SEQ, D_MODEL), 1.0),
        'c_prompt': nrm(ks[2], (BATCH, D_MODEL), 1.0),
        'c_sample': nrm(ks[3], (DEC_BATCH, D_MODEL), 1.0),
        'cache_k': nrm(ks[4], (DEPTH, n_phys, PAGE_SIZE, N_KV, HEAD_DIM), 1.0),
        'cache_v': nrm(ks[5], (DEPTH, n_phys, PAGE_SIZE, N_KV, HEAD_DIM), 1.0),
        'cache_idx_k': nrm(ks[6], (DEPTH, n_phys, PAGE_SIZE, IDX_DIM), 1.0),
        'page_table': jax.random.permutation(ks[7], n_phys)[:n_used].reshape(DEC_BATCH, n_pages).astype(jnp.int32),
        'state_conv': nrm(ks[8], (DEPTH, DEC_BATCH, CONV_W - 1, 3 * GDN_W), 1.0),
        'state_ssm': nrm(ks[9], (DEPTH, DEC_BATCH, GDN_HEADS, HEAD_DIM, HEAD_DIM), 0.1),
        'w_ada': nrm(ks[10], (DEPTH, D_MODEL, 6 * D_MODEL), 0.2 * D_MODEL ** -0.5),
        'b_ada': nrm(ks[11], (DEPTH, 6 * D_MODEL), 0.01),
        'g_norm1': 1.0 + nrm(ks[12], (DEPTH, D_MODEL), 0.01),
        'w_in': nrm(ks[15], (DEPTH, D_MODEL, IN_COLS), D_MODEL ** -0.5),
        'w_conv': nrm(ks[16], (DEPTH, CONV_W, 3 * GDN_W), CONV_W ** -0.5),
        'a_log': jnp.log(jax.random.uniform(ks[13], (DEPTH, GDN_HEADS), f32, 1.0, 16.0)),
        'dt_bias': dt + jnp.log(-jnp.expm1(-dt)),
        'g_gdn_norm': 1.0 + nrm(ks[17], (DEPTH, HEAD_DIM), 0.01),
        'w_out': nrm(ks[18], (DEPTH, D_MODEL, D_MODEL), D_MODEL ** -0.5),
        'g_norm2': 1.0 + nrm(ks[19], (DEPTH, D_MODEL), 0.01),
        'w_ffn_in': nrm(ks[20], (DEPTH, D_MODEL, 2 * D_FF), D_MODEL ** -0.5),
        'w_ffn_out': nrm(ks[21], (DEPTH, D_FF, D_MODEL), D_FF ** -0.5),
        'g_final': 1.0 + nrm(ks[22], (D_MODEL,), 0.01),
    }


def reference(x_prompt, x_sample, c_prompt, c_sample, cache_k, cache_v, cache_idx_k, page_table, state_conv, state_ssm,
              w_ada, b_ada, g_norm1, w_in, w_conv, a_log, dt_bias, g_gdn_norm, w_out, g_norm2, w_ffn_in, w_ffn_out, g_final):
    bp, tp, _ = x_prompt.shape
    ts = x_sample.shape[1]
    past = page_table.shape[1] * cache_k.shape[2]
    pos_p = jnp.arange(tp, dtype=jnp.int32)
    pos_s = past + jnp.arange(ts, dtype=jnp.int32)
    xp, xs = x_prompt, x_sample
    new_p, new_s = [], []
    for l in range(DEPTH):
        lw = (w_ada[l], b_ada[l], g_norm1[l], w_in[l], w_conv[l], a_log[l], dt_bias[l], g_gdn_norm[l],
              w_out[l], g_norm2[l], w_ffn_in[l], w_ffn_out[l])
        conv0 = jnp.zeros((bp, CONV_W - 1, 3 * GDN_W), xp.dtype)
        ssm0 = jnp.zeros((bp, GDN_HEADS, HEAD_DIM, HEAD_DIM), jnp.float32)
        xp, st_p = _layer(xp, c_prompt, pos_p, conv0, ssm0, _sparse_attn_prompt, _delta_rule_chunked, *lw)
        attn_s = functools.partial(_sparse_attn_sample, cache_k=cache_k[l], cache_v=cache_v[l],
                                   cache_ik=cache_idx_k[l], page_table=page_table)
        xs, st_s = _layer(xs, c_sample, pos_s, state_conv[l], state_ssm[l], attn_s, _delta_rule_recurrent, *lw)
        new_p.append(st_p)
        new_s.append(st_s)
    y_prompt = _rmsnorm(xp, g_final)
    y_sample = _rmsnorm(xs, g_final)
    return (y_prompt, y_sample,
            _stack(new_p, 0), _stack(new_p, 1), _stack(new_p, 2), _stack(new_p, 3), _stack(new_p, 4),
            _stack(new_s, 0), _stack(new_s, 1), _stack(new_s, 2), _stack(new_s, 3), _stack(new_s, 4))
```

```python
import functools
import math

import numpy as np
import jax
import jax.numpy as jnp
from jax import lax
from jax.experimental import pallas as pl
from jax.experimental.pallas import tpu as pltpu

F32 = jnp.float32
BF = jnp.bfloat16
I32 = jnp.int32

HEAD_DIM = 128
GDN_HEADS = 4
ATT_HEADS = 4
N_KV = 2
IDX_HEADS = 8
IDX_DIM = 64
CONV_W = 4
TOPK_MAX = 256
QBLOCK = 128
ROPE_THETA = 500000.0
EPS = 1e-6
GDN_W = GDN_HEADS * HEAD_DIM
ATT_W = ATT_HEADS * HEAD_DIM
KV_W = N_KV * HEAD_DIM
IDX_W = IDX_HEADS * IDX_DIM
QKV_W = 3 * GDN_W

GDN_CHUNK = 128
ATT_TK = 256
LANES = 128
NAT_W = QKV_W + GDN_W + KV_W + LANES
T_AQ, T_AK, T_IQ, T_IK, T_MISC, T_AV = 0, 512, 768, 1280, 1344, 1360
T_ROWS = T_AV + KV_W
VMEM_LIMIT = 56 * 1024 * 1024
INT_MIN = np.int32(-2 ** 31)
NEG_INF = float("-inf")


def _params(sem):
    return pltpu.CompilerParams(dimension_semantics=sem, vmem_limit_bytes=VMEM_LIMIT)


def _dot(a, b):
    return jnp.dot(a, b, preferred_element_type=F32)


def _dot_nt(a, b):
    return lax.dot_general(a, b, (((1,), (1,)), ((), ())), preferred_element_type=F32)


def _split3(x):
    hi = x.astype(BF)
    r1 = x - hi.astype(F32)
    mid = r1.astype(BF)
    lo = (r1 - mid.astype(F32)).astype(BF)
    return hi, mid, lo


def _mm_hi(a, b):
    ah = a.astype(BF)
    al = (a - ah.astype(F32)).astype(BF)
    bh = b.astype(BF)
    bl = (b - bh.astype(F32)).astype(BF)
    return _dot(ah, bh) + (_dot(ah, bl) + _dot(al, bh))


def _sigmoid(x):
    return 1.0 / (1.0 + jnp.exp(-x))


def _silu(x):
    return x * _sigmoid(x)


def _softplus(x):
    return jnp.maximum(x, 0.0) + jnp.log(1.0 + jnp.exp(-jnp.abs(x)))


def _rmsnorm(x, g):
    return x * lax.rsqrt(jnp.mean(x * x, axis=-1, keepdims=True) + EPS) * g


def _mod_kernel(c_ref, w_ref, b_ref, o_ref):
    s = _silu(c_ref[...]).astype(BF)
    o_ref[...] = _dot(s, w_ref[...].astype(BF)) + b_ref[...]


def _modulation(c_all, w_ada, b_ada):
    n, d = c_all.shape
    cols = w_ada.shape[1]
    tn = d
    return pl.pallas_call(
        _mod_kernel,
        grid=(cols // tn,),
        in_specs=[pl.BlockSpec((n, d), lambda j: (0, 0)),
                  pl.BlockSpec((d, tn), lambda j: (0, j)),
                  pl.BlockSpec((1, tn), lambda j: (0, j))],
        out_specs=pl.BlockSpec((n, tn), lambda j: (0, j)),
        out_shape=jax.ShapeDtypeStruct((n, cols), F32),
        compiler_params=_params(("arbitrary",)),
        name="modulation",
    )(c_all, w_ada, b_ada.reshape(1, cols))


def _rope_rows(rt_ref, base, half, cos, sin):
    x1 = rt_ref[base:base + half, :]
    x2 = rt_ref[base + half:base + 2 * half, :]
    rt_ref[base:base + half, :] = x1 * cos - x2 * sin
    rt_ref[base + half:base + 2 * half, :] = x2 * cos + x1 * sin


def _project(x_ref, sc_ref, sh_ref, g1_ref, wn_ref, wt_ref, cs128_ref, cs64_ref, pcol_ref, rt_ref):
    h = _rmsnorm(x_ref[...], g1_ref[...]) * (1.0 + sc_ref[...]) + sh_ref[...]
    hb = h.astype(BF)
    nat = _dot(hb, wn_ref[...])
    rt_ref[...] = _dot_nt(wt_ref[...], hb)
    half = HEAD_DIM // 8
    cos, sin = cs128_ref[0:half, :], cs128_ref[half:2 * half, :]
    for hd in range(ATT_HEADS):
        _rope_rows(rt_ref, T_AQ + hd * HEAD_DIM, half, cos, sin)
    for hd in range(N_KV):
        _rope_rows(rt_ref, T_AK + hd * HEAD_DIM, half, cos, sin)
    half = IDX_DIM // 8
    cos, sin = cs64_ref[0:half, :], cs64_ref[half:2 * half, :]
    for hd in range(IDX_HEADS):
        _rope_rows(rt_ref, T_IQ + hd * IDX_DIM, half, cos, sin)
    _rope_rows(rt_ref, T_IK, half, cos, sin)
    mt = rt_ref[T_MISC:T_MISC + 16, :]
    r = lax.broadcasted_iota(I32, mt.shape, 0)
    a_log, dt_b = pcol_ref[:, 0:1], pcol_ref[:, 1:2]
    gate = -jnp.exp(a_log) * _softplus(mt + dt_b)
    mt = jnp.where(r < IDX_HEADS, mt * IDX_HEADS ** -0.5, jnp.where(r < IDX_HEADS + GDN_HEADS, _sigmoid(mt), gate))
    rt_ref[T_MISC:T_MISC + 16, :] = mt
    return nat, mt


def _misc_natural(m, prow_ref):
    lane = lax.broadcasted_iota(I32, m.shape, 1)
    gate = -jnp.exp(prow_ref[0:1, :]) * _softplus(m + prow_ref[1:2, :])
    return jnp.where(lane < GDN_HEADS, _sigmoid(m), jnp.where(lane < 3 * GDN_HEADS, gate, 0.0))


def _qkv_post(conv, qkvn_ref):
    c = _silu(conv)
    for hb in range(2 * GDN_HEADS):
        xh = c[:, hb * HEAD_DIM:(hb + 1) * HEAD_DIM]
        n = xh * lax.rsqrt(jnp.sum(xh * xh, axis=-1, keepdims=True) + EPS)
        if hb < GDN_HEADS:
            n = n * HEAD_DIM ** -0.5
        qkvn_ref[:, hb * HEAD_DIM:(hb + 1) * HEAD_DIM] = n
    qkvn_ref[:, 2 * GDN_W:] = c[:, 2 * GDN_W:]


def _inproj_prompt_kernel(tm, x_ref, sc_ref, sh_ref, g1_ref, wn_ref, wt_ref, wconv_ref, cs128_ref, cs64_ref,
                          prow_ref, pcol_ref, tril_ref,
                          qkvn_ref, z_ref, vnat_ref, knat_ref, iknat_ref, tail_ref, misc_ref, gct_ref,
                          qT_ref, iqT_ref, iwT_ref, kbf_ref, vT_ref, ikbf_ref,
                          xs_ref, rt_ref):
    i = pl.program_id(1)
    nat, mt = _project(x_ref, sc_ref, sh_ref, g1_ref, wn_ref, wt_ref, cs128_ref, cs64_ref, pcol_ref, rt_ref)

    @pl.when(i == 0)
    def _():
        xs_ref[0:8, :] = jnp.zeros((8, QKV_W), F32)

    @pl.when(i > 0)
    def _():
        xs_ref[0:8, :] = xs_ref[tm:tm + 8, :]

    xs_ref[8:tm + 8, :] = nat[:, 0:QKV_W]
    conv = wconv_ref[0:1, :] * xs_ref[5:tm + 5, :]
    for t in range(1, CONV_W):
        conv = conv + wconv_ref[t:t + 1, :] * xs_ref[5 + t:tm + 5 + t, :]
    tail_ref[...] = xs_ref[tm:tm + 8, :]
    _qkv_post(conv, qkvn_ref)
    z_ref[...] = nat[:, QKV_W:QKV_W + GDN_W]
    vnat_ref[...] = nat[:, QKV_W + GDN_W:QKV_W + GDN_W + KV_W]

    gm = _misc_natural(nat[:, NAT_W - LANES:], prow_ref)
    tril = tril_ref[...]
    hi, mid, lo = _split3(gm)
    gc = _dot(tril, hi) + (_dot(tril, mid) + _dot(tril, lo))
    lane = lax.broadcasted_iota(I32, gm.shape, 1)
    misc_ref[...] = jnp.where(lane < 2 * GDN_HEADS, gm, gc)
    hi, mid, lo = _split3(mt)
    gct = _dot_nt(hi, tril) + (_dot_nt(mid, tril) + _dot_nt(lo, tril))
    r = lax.broadcasted_iota(I32, mt.shape, 0)
    bg = jnp.where(r < IDX_HEADS + GDN_HEADS, mt, gct)[8:16, :]
    for cc in range(tm // GDN_CHUNK):
        gct_ref[cc] = bg[:, cc * GDN_CHUNK:(cc + 1) * GDN_CHUNK]

    iwT_ref[...] = mt[0:IDX_HEADS, :]
    qT_ref[...] = rt_ref[T_AQ:T_AQ + ATT_W, :].astype(BF)
    iqT_ref[...] = rt_ref[T_IQ:T_IQ + IDX_W, :].astype(BF)
    kn = rt_ref[T_AK:T_AK + KV_W, :].T
    knat_ref[...] = kn
    kbf_ref[...] = kn.astype(BF)
    ikn = rt_ref[T_IK:T_IK + LANES, :].T[:, 0:IDX_DIM]
    iknat_ref[...] = ikn
    ikbf_ref[...] = ikn.astype(BF)
    for cc in range(tm // ATT_TK):
        vT_ref[cc] = rt_ref[T_AV:T_AV + KV_W, cc * ATT_TK:(cc + 1) * ATT_TK].astype(BF)


def _inproj_sample_kernel(x_ref, sc_ref, sh_ref, g1_ref, wn_ref, wt_ref, wconv_ref, cs128_ref, cs64_ref,
                          prow_ref, pcol_ref, s0_ref, s1_ref, s2_ref,
                          qkvn_ref, z_ref, vnat_ref, knat_ref, slab_ref, raw_ref, misc_ref, aq_ref, iq_ref,
                          rt_ref):
    nat, _ = _project(x_ref, sc_ref, sh_ref, g1_ref, wn_ref, wt_ref, cs128_ref, cs64_ref, pcol_ref, rt_ref)
    raw = nat[:, 0:QKV_W]
    raw_ref[...] = raw
    conv = wconv_ref[0:1, :] * s0_ref[...]
    conv = conv + wconv_ref[1:2, :] * s1_ref[...]
    conv = conv + wconv_ref[2:3, :] * s2_ref[...]
    conv = conv + wconv_ref[3:4, :] * raw
    _qkv_post(conv, qkvn_ref)
    z_ref[...] = nat[:, QKV_W:QKV_W + GDN_W]
    vnat_ref[...] = nat[:, QKV_W + GDN_W:QKV_W + GDN_W + KV_W]
    misc_ref[...] = _misc_natural(nat[:, NAT_W - LANES:], prow_ref)
    aq_ref[...] = rt_ref[T_AQ:T_AQ + ATT_W, :].T
    iq_ref[...] = rt_ref[T_IQ:T_IQ + IDX_W, :].T
    knat_ref[...] = rt_ref[T_AK:T_AK + KV_W, :].T
    slab_ref[...] = rt_ref[T_IK:T_IK + LANES, :].T


def _inproj_weights(w_in, a_log, dt_bias):
    offs = np.cumsum([0, QKV_W, GDN_W, GDN_HEADS, GDN_HEADS, ATT_W, KV_W, KV_W, IDX_W, IDX_DIM, IDX_HEADS])
    qkv, z, beta, a, aq, ak, av, iq, ik, iw = [w_in[:, offs[n]:offs[n + 1]] for n in range(10)]
    d = w_in.shape[0]
    pad = jnp.zeros((d, LANES - 3 * GDN_HEADS), w_in.dtype)
    w_nat = jnp.concatenate([qkv, z, av, beta, a, a, pad], axis=1).astype(BF)
    w_t = jnp.concatenate([aq, ak, iq, ik, iw, beta, a, av], axis=1).T.astype(BF)
    zrow = jnp.zeros((LANES - 3 * GDN_HEADS,), F32)
    prow = jnp.stack([jnp.concatenate([jnp.zeros((GDN_HEADS,), F32), a_log, a_log, zrow]),
                      jnp.concatenate([jnp.zeros((GDN_HEADS,), F32), dt_bias, dt_bias, zrow])])
    z12 = jnp.zeros((IDX_HEADS + GDN_HEADS,), F32)
    pcol = jnp.stack([jnp.concatenate([z12, a_log]), jnp.concatenate([z12, dt_bias])], axis=1)
    return w_nat, w_t, prow, pcol


def _rope_tables(pos, dim):
    half = dim // 8
    inv = ROPE_THETA ** (-jnp.arange(half, dtype=F32) / half)
    ang = pos.astype(F32)[:, None] * inv[None, :]
    return jnp.concatenate([jnp.cos(ang).T, jnp.sin(ang).T], axis=0)


def _inproj_prompt(x, mod, g1, w_nat, w_t, w_conv, prow, pcol, tm):
    bp, tp, d = x.shape
    nt = tp // tm
    rows = bp * tp
    pos = jnp.arange(tp, dtype=I32)
    cs128, cs64 = _rope_tables(pos, HEAD_DIM), _rope_tables(pos, IDX_DIM)
    ri = np.arange(tm)
    tril = jnp.asarray((ri[:, None] // GDN_CHUNK == ri[None, :] // GDN_CHUNK) & (ri[None, :] <= ri[:, None]), BF)
    const = lambda shape: pl.BlockSpec(shape, lambda b, i: (0,) * len(shape))
    rowblk = lambda w: pl.BlockSpec((tm, w), lambda b, i: (b * nt + i, 0))
    colblk = lambda h: pl.BlockSpec((h, tm), lambda b, i: (0, b * nt + i))
    in_specs = [
        pl.BlockSpec((None, tm, d), lambda b, i: (b, i, 0)),
        pl.BlockSpec((None, 1, d), lambda b, i: (b, 0, 1)),
        pl.BlockSpec((None, 1, d), lambda b, i: (b, 0, 0)),
        const((1, d)), const((d, NAT_W)), const((T_ROWS, d)), const((CONV_W, QKV_W)),
        pl.BlockSpec((HEAD_DIM // 4, tm), lambda b, i: (0, i)),
        pl.BlockSpec((IDX_DIM // 4, tm), lambda b, i: (0, i)),
        const((2, LANES)), const((16, 2)), const((tm, tm)),
    ]
    out_shape = [
        jax.ShapeDtypeStruct((rows, QKV_W), F32), jax.ShapeDtypeStruct((rows, GDN_W), F32),
        jax.ShapeDtypeStruct((rows, KV_W), F32), jax.ShapeDtypeStruct((rows, KV_W), F32),
        jax.ShapeDtypeStruct((rows, IDX_DIM), F32), jax.ShapeDtypeStruct((bp, 8, QKV_W), F32),
        jax.ShapeDtypeStruct((rows, LANES), F32), jax.ShapeDtypeStruct((rows // GDN_CHUNK, 8, GDN_CHUNK), F32),
        jax.ShapeDtypeStruct((ATT_W, rows), BF), jax.ShapeDtypeStruct((IDX_W, rows), BF),
        jax.ShapeDtypeStruct((IDX_HEADS, rows), F32), jax.ShapeDtypeStruct((rows, KV_W), BF),
        jax.ShapeDtypeStruct((rows // ATT_TK, KV_W, ATT_TK), BF), jax.ShapeDtypeStruct((rows, IDX_DIM), BF),
    ]
    out_specs = [
        rowblk(QKV_W), rowblk(GDN_W), rowblk(KV_W), rowblk(KV_W), rowblk(IDX_DIM),
        pl.BlockSpec((None, 8, QKV_W), lambda b, i: (b, 0, 0)),
        rowblk(LANES),
        pl.BlockSpec((tm // GDN_CHUNK, 8, GDN_CHUNK), lambda b, i: (b * nt + i, 0, 0)),
        colblk(ATT_W), colblk(IDX_W), colblk(IDX_HEADS), rowblk(KV_W),
        pl.BlockSpec((tm // ATT_TK, KV_W, ATT_TK), lambda b, i: (b * nt + i, 0, 0)),
        rowblk(IDX_DIM),
    ]
    return pl.pallas_call(
        functools.partial(_inproj_prompt_kernel, tm),
        grid=(bp, nt), in_specs=in_specs, out_specs=out_specs, out_shape=out_shape,
        scratch_shapes=[pltpu.VMEM((tm + 8, QKV_W), F32), pltpu.VMEM((T_ROWS, tm), F32)],
        compiler_params=_params(("arbitrary", "arbitrary")),
        name="inproj_prompt",
    )(x, mod, mod, g1.reshape(1, d), w_nat, w_t, w_conv, cs128, cs64, prow, pcol, tril)


def _inproj_sample(x, mod, g1, w_nat, w_t, w_conv, prow, pcol, conv_state, past):
    db, d = x.shape
    pos = jnp.full((db,), past, I32)
    cs128, cs64 = _rope_tables(pos, HEAD_DIM), _rope_tables(pos, IDX_DIM)
    full = lambda shape: pl.BlockSpec(shape, lambda i: (0,) * len(shape))
    in_specs = [
        full((db, d)),
        pl.BlockSpec((db, d), lambda i: (0, 1)), pl.BlockSpec((db, d), lambda i: (0, 0)),
        full((1, d)), full((d, NAT_W)), full((T_ROWS, d)), full((CONV_W, QKV_W)),
        full((HEAD_DIM // 4, db)), full((IDX_DIM // 4, db)), full((2, LANES)), full((16, 2)),
        full((db, QKV_W)), full((db, QKV_W)), full((db, QKV_W)),
    ]
    widths = [QKV_W, GDN_W, KV_W, KV_W, LANES, QKV_W, LANES, ATT_W, IDX_W]
    return pl.pallas_call(
        _inproj_sample_kernel,
        grid=(1,), in_specs=in_specs,
        out_specs=[full((db, w)) for w in widths],
        out_shape=[jax.ShapeDtypeStruct((db, w), F32) for w in widths],
        scratch_shapes=[pltpu.VMEM((T_ROWS, db), F32)],
        compiler_params=_params(("arbitrary",)),
        name="inproj_sample",
    )(x, mod, mod, g1.reshape(1, d), w_nat, w_t, w_conv, cs128, cs64, prow, pcol,
      conv_state[:, 0], conv_state[:, 1], conv_state[:, 2])


def _unit_lower_inverse(low):
    c = low.shape[0]
    ii = lax.broadcasted_iota(I32, (c, c), 0)
    jj = lax.broadcasted_iota(I32, (c, c), 1)
    n = -low
    t = jnp.where(ii == jj, 1.0, 0.0) + n
    p = n
    for _ in range(int(math.log2(c)) - 1):
        p = _mm_hi(p, p)
        t = t + _mm_hi(t, p)
    return t


def _gdn_chunk(q, k, v, beta_c, gc_c, gc_r, s):
    c = q.shape[0]
    ii = lax.broadcasted_iota(I32, (c, c), 0)
    jj = lax.broadcasted_iota(I32, (c, c), 1)
    decay = jnp.where(ii >= jj, jnp.exp(jnp.where(ii >= jj, gc_c - gc_r, 0.0)), 0.0)
    kb = k * beta_c
    vb = v * beta_c
    both = _dot_nt(jnp.concatenate([kb, q], axis=0).astype(BF), k.astype(BF))
    low = jnp.where(ii > jj, both[0:c] * decay, 0.0)
    intra = both[c:] * decay
    t = _unit_lower_inverse(low)
    eg = jnp.exp(gc_c)
    uw = _dot(t.astype(BF), jnp.concatenate([vb, kb * eg], axis=1).astype(BF))
    u, w = uw[:, 0:HEAD_DIM], uw[:, HEAD_DIM:]
    ws = _dot(jnp.concatenate([w, q * eg], axis=0).astype(BF), s.astype(BF))
    v_new = u - ws[0:c]
    o = ws[c:] + _dot(intra.astype(BF), v_new.astype(BF))
    g_last = gc_c[c - 1:c, :]
    kd = k * jnp.exp(g_last - gc_c)
    s_new = s * jnp.exp(g_last) + _dot(kd.T.astype(BF), v_new.astype(BF))
    return o, s_new


def _gdn_prompt_kernel(tg, qkvn_ref, z_ref, misc_ref, gct_ref, gn_ref, o_ref, ssm_ref, s_ref):
    i = pl.program_id(1)

    @pl.when(i == 0)
    def _():
        s_ref[...] = jnp.zeros(s_ref.shape, F32)

    c = GDN_CHUNK
    for cc in range(tg // c):
        r0 = cc * c
        for hd in range(GDN_HEADS):
            lo = hd * HEAD_DIM
            q = qkvn_ref[r0:r0 + c, lo:lo + HEAD_DIM]
            k = qkvn_ref[r0:r0 + c, GDN_W + lo:GDN_W + lo + HEAD_DIM]
            v = qkvn_ref[r0:r0 + c, 2 * GDN_W + lo:2 * GDN_W + lo + HEAD_DIM]
            beta_c = misc_ref[r0:r0 + c, hd:hd + 1]
            gc_c = misc_ref[r0:r0 + c, 2 * GDN_HEADS + hd:2 * GDN_HEADS + hd + 1]
            gc_r = gct_ref[cc, GDN_HEADS + hd:GDN_HEADS + hd + 1, :]
            o, s_new = _gdn_chunk(q, k, v, beta_c, gc_c, gc_r, s_ref[hd])
            s_ref[hd] = s_new
            o = _rmsnorm(o, gn_ref[...]) * _silu(z_ref[r0:r0 + c, lo:lo + HEAD_DIM])
            o_ref[r0:r0 + c, lo:lo + HEAD_DIM] = o.astype(BF)
    ssm_ref[...] = s_ref[...]


def _gdn_prompt(qkvn, z, misc, gct, g_norm, bp, tp, tg):
    nt = tp // tg
    rows = bp * tp
    rowblk = lambda w: pl.BlockSpec((tg, w), lambda b, i: (b * nt + i, 0))
    return pl.pallas_call(
        functools.partial(_gdn_prompt_kernel, tg),
        grid=(bp, nt),
        in_specs=[rowblk(QKV_W), rowblk(GDN_W), rowblk(LANES),
                  pl.BlockSpec((tg // GDN_CHUNK, 8, GDN_CHUNK), lambda b, i: (b * nt + i, 0, 0)),
                  pl.BlockSpec((1, HEAD_DIM), lambda b, i: (0, 0))],
        out_specs=[rowblk(GDN_W),
                   pl.BlockSpec((None, GDN_HEADS, HEAD_DIM, HEAD_DIM), lambda b, i: (b, 0, 0, 0))],
        out_shape=[jax.ShapeDtypeStruct((rows, GDN_W), BF),
                   jax.ShapeDtypeStruct((bp, GDN_HEADS, HEAD_DIM, HEAD_DIM), F32)],
        scratch_shapes=[pltpu.VMEM((GDN_HEADS, HEAD_DIM, HEAD_DIM), F32)],
        compiler_params=_params(("arbitrary", "arbitrary")),
        name="gdn_prompt",
    )(qkvn, z, misc, gct, g_norm.reshape(1, HEAD_DIM))


def _gdn_sample_kernel(nb, qkvn_ref, z_ref, misc_ref, gn_ref, s_ref, o_ref, ssm_ref):
    for bi in range(nb):
        for hd in range(GDN_HEADS):
            lo = hd * HEAD_DIM
            q = qkvn_ref[bi:bi + 1, lo:lo + HEAD_DIM]
            k = qkvn_ref[bi:bi + 1, GDN_W + lo:GDN_W + lo + HEAD_DIM]
            v = qkvn_ref[bi:bi + 1, 2 * GDN_W + lo:2 * GDN_W + lo + HEAD_DIM]
            beta = misc_ref[bi:bi + 1, hd:hd + 1]
            g = misc_ref[bi:bi + 1, GDN_HEADS + hd:GDN_HEADS + hd + 1]
            kcol = jnp.broadcast_to(k, (HEAD_DIM, HEAD_DIM)).T
            qcol = jnp.broadcast_to(q, (HEAD_DIM, HEAD_DIM)).T
            s = s_ref[bi, hd] * jnp.exp(g)
            ks = jnp.sum(kcol * s, axis=0, keepdims=True)
            delta = (v - ks) * beta
            s = s + kcol * delta
            ssm_ref[bi, hd] = s
            o = jnp.sum(qcol * s, axis=0, keepdims=True)
            o = _rmsnorm(o, gn_ref[...]) * _silu(z_ref[bi:bi + 1, lo:lo + HEAD_DIM])
            o_ref[bi:bi + 1, lo:lo + HEAD_DIM] = o.astype(BF)


def _gdn_sample(qkvn, z, misc, g_norm, state, nb):
    db = qkvn.shape[0]
    rowblk = lambda w: pl.BlockSpec((nb, w), lambda i: (i, 0))
    sblk = pl.BlockSpec((nb, GDN_HEADS, HEAD_DIM, HEAD_DIM), lambda i: (i, 0, 0, 0))
    return pl.pallas_call(
        functools.partial(_gdn_sample_kernel, nb),
        grid=(db // nb,),
        in_specs=[rowblk(QKV_W), rowblk(GDN_W), rowblk(LANES), pl.BlockSpec((1, HEAD_DIM), lambda i: (0, 0)), sblk],
        out_specs=[rowblk(GDN_W), sblk],
        out_shape=[jax.ShapeDtypeStruct((db, GDN_W), BF), jax.ShapeDtypeStruct(state.shape, F32)],
        compiler_params=_params(("arbitrary",)),
        name="gdn_sample",
    )(qkvn, z, misc, g_norm.reshape(1, HEAD_DIM), state)


def _ordered_word_to_float(u):
    s = u ^ INT_MIN
    return lax.bitcast_convert_type(s ^ ((s >> 31) & np.int32(0x7FFFFFFF)), F32)


def _select_topk_bias(i_ref, nch, tk, k_sel, idx_bits):
    nl = i_ref.shape[1]
    row = lax.broadcasted_iota(I32, (tk, 1), 0)

    def chunk(c):
        return pl.ds(pl.multiple_of(c * tk, tk), tk)

    def count(pred):
        def body(c, acc):
            m = pred(i_ref[chunk(c), :], c * tk + row)
            return acc + jnp.sum(jnp.where(m, 1, 0).astype(I32).reshape(tk // 8, 8, nl), axis=0)
        acc = lax.fori_loop(0, nch, body, jnp.zeros((8, nl), I32))
        return jnp.sum(acc, axis=0, keepdims=True)

    def bit_body(i, t_u):
        cand_u = t_u | jnp.left_shift(jnp.int32(1), 31 - i)
        cand = _ordered_word_to_float(cand_u)
        cnt = count(lambda sc, idx: sc >= cand)
        return jnp.where(cnt >= k_sel, cand_u, t_u)

    t_u = lax.fori_loop(0, 32, bit_body, jnp.zeros((1, nl), I32))
    thr = jnp.where((t_u >> 23) == 0, NEG_INF, _ordered_word_to_float(t_u))
    need = k_sel - count(lambda sc, idx: sc > thr)

    def idx_body(i, x):
        cand = x | jnp.left_shift(jnp.int32(1), idx_bits - 1 - i)
        cnt = count(lambda sc, idx: (sc == thr) & (idx < cand))
        return jnp.where(cnt < need, cand, x)

    x = lax.fori_loop(0, idx_bits, idx_body, jnp.zeros((1, nl), I32))

    def write(c, carry):
        sc = i_ref[chunk(c), :]
        sel = (sc > thr) | ((sc == thr) & (c * tk + row <= x))
        i_ref[chunk(c), :] = jnp.where(sel & (sc > NEG_INF), 0.0, NEG_INF)
        return carry

    lax.fori_loop(0, nch, write, 0)


def _attn_prompt_kernel(tq, tk, k_sel, idx_bits, qT_ref, iqT_ref, iwT_ref, kbf_ref, vT_ref, ikbf_ref, o_ref,
                        i_ref):
    t0 = pl.program_id(1) * tq
    nch = (t0 + tq + tk - 1) // tk
    qpos = t0 + lax.broadcasted_iota(I32, (1, tq), 1)
    row = lax.broadcasted_iota(I32, (tk, 1), 0)
    w = iwT_ref[...]
    pairs = IDX_HEADS // 2
    rhs = [jnp.concatenate([iqT_ref[(2 * p) * IDX_DIM:(2 * p + 1) * IDX_DIM, :],
                            iqT_ref[(2 * p + 1) * IDX_DIM:(2 * p + 2) * IDX_DIM, :]], axis=1) for p in range(pairs)]

    def chunk(c):
        return pl.ds(pl.multiple_of(c * tk, tk), tk)

    def idx_body(c, carry):
        ikc = ikbf_ref[chunk(c), :]
        acc = jnp.zeros((tk, tq), F32)
        for p in range(pairs):
            d = jnp.maximum(_dot(ikc, rhs[p]), 0.0)
            acc = acc + d[:, 0:tq] * w[2 * p:2 * p + 1, :] + d[:, tq:] * w[2 * p + 1:2 * p + 2, :]
        i_ref[chunk(c), :] = jnp.where(c * tk + row <= qpos, acc, NEG_INF)
        return carry

    lax.fori_loop(0, nch, idx_body, 0)
    _select_topk_bias(i_ref, nch, tk, k_sel, idx_bits)

    scale = HEAD_DIM ** -0.5
    rep = ATT_HEADS // N_KV
    for g in range(N_KV):
        qg = jnp.concatenate([qT_ref[(rep * g + r) * HEAD_DIM:(rep * g + r + 1) * HEAD_DIM, :] for r in range(rep)],
                             axis=1)

        def body(c, carry):
            m, l, acc = carry
            kc = kbf_ref[chunk(c), g * HEAD_DIM:(g + 1) * HEAD_DIM]
            bias = i_ref[chunk(c), :]
            s = _dot(kc, qg) * scale + jnp.concatenate([bias] * rep, axis=1)
            m_new = jnp.maximum(m, jnp.max(s, axis=0, keepdims=True))
            m_safe = jnp.where(m_new == NEG_INF, 0.0, m_new)
            p = jnp.exp(s - m_safe)
            alpha = jnp.exp(m - m_safe)
            l = alpha * l + jnp.sum(p, axis=0, keepdims=True)
            vt = vT_ref[c, g * HEAD_DIM:(g + 1) * HEAD_DIM, :]
            acc = acc * alpha + _dot(vt, p.astype(BF))
            return m_new, l, acc

        init = (jnp.full((1, rep * tq), NEG_INF, F32), jnp.zeros((1, rep * tq), F32),
                jnp.zeros((HEAD_DIM, rep * tq), F32))
        _, l, acc = lax.fori_loop(0, nch, body, init)
        o_t = acc / l
        for r in range(rep):
            hd = rep * g + r
            o_ref[:, hd * HEAD_DIM:(hd + 1) * HEAD_DIM] = o_t[:, r * tq:(r + 1) * tq].T.astype(BF)


def _attn_prompt(qT, iqT, iwT, kbf, vT, ikbf, bp, tp):
    tq, tk = QBLOCK, ATT_TK
    nq = tp // tq
    rows = bp * tp
    k_sel = min(TOPK_MAX, tp // 4)
    idx_bits = max(1, int(tp - 1).bit_length())
    colblk = lambda h: pl.BlockSpec((h, tq), lambda b, j: (0, b * nq + j))
    return pl.pallas_call(
        functools.partial(_attn_prompt_kernel, tq, tk, k_sel, idx_bits),
        grid=(bp, nq),
        in_specs=[colblk(ATT_W), colblk(IDX_W), colblk(IDX_HEADS),
                  pl.BlockSpec((tp, KV_W), lambda b, j: (b, 0)),
                  pl.BlockSpec((tp // tk, KV_W, tk), lambda b, j: (b, 0, 0)),
                  pl.BlockSpec((tp, IDX_DIM), lambda b, j: (b, 0))],
        out_specs=pl.BlockSpec((tq, ATT_W), lambda b, j: (b * nq + j, 0)),
        out_shape=jax.ShapeDtypeStruct((rows, ATT_W), BF),
        scratch_shapes=[pltpu.VMEM((tp, tq), F32)],
        compiler_params=_params(("arbitrary", "arbitrary")),
        name="attn_prompt",
    )(qT, iqT, iwT, kbf, vT, ikbf)


def _sidx_kernel(npg, pt_ref, iq_ref, iw_ref, ikn_ref, *refs):
    pages, out_ref = refs[:npg], refs[npg]
    iq = iq_ref[...].astype(BF)
    w = iw_ref[...]
    for p in range(npg):
        d = jnp.maximum(_dot_nt(iq, pages[p][...].astype(BF)), 0.0)
        out_ref[:, p * LANES:(p + 1) * LANES] = jnp.sum(d * w, axis=0, keepdims=True)
    dn = jnp.sum(iq.astype(F32) * ikn_ref[...].astype(BF).astype(F32), axis=1, keepdims=True)
    sn = jnp.sum(jnp.maximum(dn, 0.0) * w, axis=0, keepdims=True)
    lane = lax.broadcasted_iota(I32, (1, LANES), 1)
    out_ref[:, npg * LANES:(npg + 1) * LANES] = jnp.where(lane == 0, sn, NEG_INF)


def _sample_index_scores(iq, iw, ik_new, cache_ik, page_table):
    db, npg = page_table.shape
    page = cache_ik.shape[1]
    assert page == LANES
    width = (npg + 1) * LANES
    page_specs = [pl.BlockSpec((None, page, IDX_DIM), functools.partial(lambda p, b, pt: (pt[b, p], 0, 0), p))
                  for p in range(npg)]
    grid_spec = pltpu.PrefetchScalarGridSpec(
        num_scalar_prefetch=1, grid=(db,),
        in_specs=[pl.BlockSpec((None, IDX_HEADS, IDX_DIM), lambda b, pt: (b, 0, 0)),
                  pl.BlockSpec((None, IDX_HEADS, 1), lambda b, pt: (b, 0, 0)),
                  pl.BlockSpec((None, 1, IDX_DIM), lambda b, pt: (b, 0, 0))] + page_specs,
        out_specs=pl.BlockSpec((None, 1, width), lambda b, pt: (b, 0, 0)))
    out = pl.pallas_call(
        functools.partial(_sidx_kernel, npg), grid_spec=grid_spec,
        out_shape=jax.ShapeDtypeStruct((db, 1, width), F32),
        compiler_params=_params(("arbitrary",)),
        name="sample_index_scores",
    )(page_table, iq.reshape(db, IDX_HEADS, IDX_DIM), iw.reshape(db, IDX_HEADS, 1), ik_new.reshape(db, 1, IDX_DIM),
      *([cache_ik] * npg))
    return out.reshape(db, width)


def _ssel_kernel(nch, k_sel, idx_bits, s_ref, bias_ref, i_ref):
    for p in range(nch):
        i_ref[p * LANES:(p + 1) * LANES, :] = s_ref[:, p * LANES:(p + 1) * LANES].T
    _select_topk_bias(i_ref, nch, LANES, k_sel, idx_bits)
    for p in range(nch):
        bias_ref[:, p * LANES:(p + 1) * LANES] = i_ref[p * LANES:(p + 1) * LANES, :].T


def _sample_select(scores, n_keys):
    db, width = scores.shape
    assert db == LANES
    nch = width // LANES
    k_sel = min(TOPK_MAX, n_keys // 4)
    idx_bits = max(1, int(width - 1).bit_length())
    return pl.pallas_call(
        functools.partial(_ssel_kernel, nch, k_sel, idx_bits),
        grid=(1,),
        in_specs=[pl.BlockSpec((db, width), lambda i: (0, 0))],
        out_specs=pl.BlockSpec((db, width), lambda i: (0, 0)),
        out_shape=jax.ShapeDtypeStruct((db, width), F32),
        scratch_shapes=[pltpu.VMEM((width, db), F32)],
        compiler_params=_params(("arbitrary",)),
        name="sample_select",
    )(scores)


def _sattn_kernel(npg, pt_ref, q_ref, bias_ref, kn_ref, vn_ref, *refs):
    kpages, vpages, o_ref = refs[:npg], refs[npg:2 * npg], refs[2 * npg]
    rep = ATT_HEADS // N_KV
    q = q_ref[...]
    q8 = jnp.concatenate([q, jnp.zeros((8 - ATT_HEADS, HEAD_DIM), F32)], axis=0).astype(BF)
    hrow = lax.broadcasted_iota(I32, (8, 1), 0)
    lane = lax.broadcasted_iota(I32, (1, LANES), 1)
    parts = []
    for p in range(npg):
        sg = [_dot_nt(q8, kpages[p][:, g * HEAD_DIM:(g + 1) * HEAD_DIM].astype(BF)) for g in range(N_KV)]
        parts.append(jnp.where(hrow // rep == 0, sg[0], sg[1]))
    kn = kn_ref[...].astype(BF).astype(F32)
    kn8 = jnp.where(hrow // rep == 0, kn[0:1, :], kn[1:2, :])
    s_new = jnp.sum(q8.astype(F32) * kn8, axis=1, keepdims=True)
    parts.append(jnp.where(lane == 0, s_new, 0.0))
    s = jnp.concatenate(parts, axis=1) * HEAD_DIM ** -0.5 + bias_ref[...]
    m = jnp.max(s, axis=1, keepdims=True)
    e = jnp.exp(s - m)
    pr = (e / jnp.sum(e, axis=1, keepdims=True)).astype(BF)
    og = [jnp.zeros((8, HEAD_DIM), F32) for _ in range(N_KV)]
    for p in range(npg):
        pp = pr[:, p * LANES:(p + 1) * LANES]
        for g in range(N_KV):
            og[g] = og[g] + _dot(pp, vpages[p][:, g * HEAD_DIM:(g + 1) * HEAD_DIM].astype(BF))
    vn = vn_ref[...].astype(BF).astype(F32)
    vn8 = jnp.where(hrow // rep == 0, vn[0:1, :], vn[1:2, :])
    o8 = jnp.where(hrow // rep == 0, og[0], og[1]) + pr[:, npg * LANES:npg * LANES + 1].astype(F32) * vn8
    for hd in range(ATT_HEADS):
        o_ref[:, hd * HEAD_DIM:(hd + 1) * HEAD_DIM] = o8[hd:hd + 1, :].astype(BF)


def _sample_attention(q, bias, k_new, v_new, cache_k, cache_v, page_table):
    db, npg = page_table.shape
    n_phys, page = cache_k.shape[0], cache_k.shape[1]
    width = (npg + 1) * LANES
    ck = cache_k.reshape(n_phys, page, KV_W)
    cv = cache_v.reshape(n_phys, page, KV_W)
    page_specs = [pl.BlockSpec((None, page, KV_W), functools.partial(lambda p, b, pt: (pt[b, p], 0, 0), p))
                  for p in range(npg)]
    grid_spec = pltpu.PrefetchScalarGridSpec(
        num_scalar_prefetch=1, grid=(db,),
        in_specs=[pl.BlockSpec((None, ATT_HEADS, HEAD_DIM), lambda b, pt: (b, 0, 0)),
                  pl.BlockSpec((None, 1, width), lambda b, pt: (b, 0, 0)),
                  pl.BlockSpec((None, N_KV, HEAD_DIM), lambda b, pt: (b, 0, 0)),
                  pl.BlockSpec((None, N_KV, HEAD_DIM), lambda b, pt: (b, 0, 0))] + page_specs + page_specs,
        out_specs=pl.BlockSpec((None, 1, ATT_W), lambda b, pt: (b, 0, 0)))
    out = pl.pallas_call(
        functools.partial(_sattn_kernel, npg), grid_spec=grid_spec,
        out_shape=jax.ShapeDtypeStruct((db, 1, ATT_W), BF),
        compiler_params=_params(("arbitrary",)),
        name="sample_attention",
    )(page_table, q.reshape(db, ATT_HEADS, HEAD_DIM), bias.reshape(db, 1, width),
      k_new.reshape(db, N_KV, HEAD_DIM), v_new.reshape(db, N_KV, HEAD_DIM), *([ck] * npg), *([cv] * npg))
    return out.reshape(db, ATT_W)


def _post_kernel(final, nff, og_ref, oa_ref, x_ref, ga1_ref, sc2_ref, sh2_ref, ga2_ref, g2_ref, gf_ref,
                 wo_ref, wg_ref, wu_ref, wd_ref, y_ref, x1_ref, h2_ref, acc_ref):
    jf = pl.program_id(1)

    @pl.when(jf == 0)
    def _():
        mixed = _dot(og_ref[...], wo_ref[0:GDN_W, :]) + _dot(oa_ref[...], wo_ref[GDN_W:, :])
        x1 = x_ref[...] + ga1_ref[...] * mixed
        x1_ref[...] = x1
        h2_ref[...] = (_rmsnorm(x1, g2_ref[...]) * (1.0 + sc2_ref[...]) + sh2_ref[...]).astype(BF)
        acc_ref[...] = jnp.zeros(acc_ref.shape, F32)

    h2 = h2_ref[...]
    act = (_silu(_dot(h2, wg_ref[...])) * _dot(h2, wu_ref[...])).astype(BF)
    acc_ref[...] += _dot(act, wd_ref[...])

    @pl.when(jf == nff - 1)
    def _():
        x2 = x1_ref[...] + ga2_ref[...] * acc_ref[...]
        y_ref[...] = _rmsnorm(x2, gf_ref[...]) if final else x2


def _post(og, oa, x2d, mod, mod_spec, g2, gf, wo, wfi, wfo, tm, tf, final):
    rows, d = x2d.shape
    dff = wfo.shape[0]
    nff = dff // tf
    rowblk = lambda w: pl.BlockSpec((tm, w), lambda i, j: (i, 0))
    const = lambda shape: pl.BlockSpec(shape, lambda i, j: (0,) * len(shape))
    in_specs = [rowblk(GDN_W), rowblk(ATT_W), rowblk(d),
                mod_spec(2), mod_spec(4), mod_spec(3), mod_spec(5),
                const((1, d)), const((1, d)), const((d, d)),
                pl.BlockSpec((d, tf), lambda i, j: (0, j)),
                pl.BlockSpec((d, tf), lambda i, j: (0, nff + j)),
                pl.BlockSpec((tf, d), lambda i, j: (j, 0))]
    return pl.pallas_call(
        functools.partial(_post_kernel, final, nff),
        grid=(rows // tm, nff), in_specs=in_specs, out_specs=rowblk(d),
        out_shape=jax.ShapeDtypeStruct((rows, d), F32),
        scratch_shapes=[pltpu.VMEM((tm, d), F32), pltpu.VMEM((tm, d), BF), pltpu.VMEM((tm, d), F32)],
        compiler_params=_params(("arbitrary", "arbitrary")),
        name="post",
    )(og, oa, x2d, mod, mod, mod, mod, g2.reshape(1, d), gf.reshape(1, d), wo, wfi, wfi, wfo)


def _pick(n, prefs):
    for p in prefs:
        if n % p == 0:
            return p
    return n


def kernel(x_prompt, x_sample, c_prompt, c_sample, cache_k, cache_v, cache_idx_k, page_table, state_conv, state_ssm,
           w_ada, b_ada, g_norm1, w_in, w_conv, a_log, dt_bias, g_gdn_norm, w_out, g_norm2, w_ffn_in, w_ffn_out,
           g_final):
    bp, tp, d = x_prompt.shape
    db, ts, _ = x_sample.shape
    assert ts == 1 and d == GDN_W + ATT_W and tp % max(ATT_TK, GDN_CHUNK) == 0
    depth = w_in.shape[0]
    npg, page = page_table.shape[1], cache_k.shape[2]
    past = npg * page
    tm_in = _pick(tp, (256,))
    tg = _pick(tp, (256,))
    tm_post = _pick(tp, (512, 256))
    dff = w_ffn_out.shape[1]
    tf = _pick(dff, (256, 128))

    xp = x_prompt
    xs = x_sample.reshape(db, d)
    npad = (-(bp + db)) % 8
    c_all = jnp.concatenate([c_prompt, c_sample, jnp.zeros((npad, d), F32)], axis=0)
    new_p, new_s = [], []
    for l in range(depth):
        final = l == depth - 1
        mod = _modulation(c_all, w_ada[l], b_ada[l])
        mod_p = mod[:bp].reshape(bp, 1, 6 * d)
        mod_s = mod[bp:bp + db]
        w_nat, w_t, prow, pcol = _inproj_weights(w_in[l], a_log[l], dt_bias[l])
        wo = w_out[l].astype(BF)
        wfi = w_ffn_in[l].astype(BF)
        wfo = w_ffn_out[l].astype(BF)

        (qkvn, z, vnat, knat, iknat, tail, misc, gct, qT, iqT, iwT, kbf, vT, ikbf) = _inproj_prompt(
            xp, mod_p, g_norm1[l], w_nat, w_t, w_conv[l], prow, pcol, tm_in)
        og, ssm_p = _gdn_prompt(qkvn, z, misc, gct, g_gdn_norm[l], bp, tp, tg)
        oa = _attn_prompt(qT, iqT, iwT, kbf, vT, ikbf, bp, tp)
        tiles_b = tp // tm_post
        spec_p = lambda k: pl.BlockSpec((None, 1, d), lambda i, j: (i // tiles_b, 0, k))
        xp = _post(og, oa, xp.reshape(bp * tp, d), mod_p, spec_p, g_norm2[l], g_final, wo, wfi, wfo,
                   tm_post, tf, final).reshape(bp, tp, d)
        new_p.append((knat.reshape(bp, tp, N_KV, HEAD_DIM), vnat.reshape(bp, tp, N_KV, HEAD_DIM),
                      iknat.reshape(bp, tp, IDX_DIM), tail[:, 8 - (CONV_W - 1):, :], ssm_p))

        (qkvn_s, z_s, vnat_s, knat_s, slab_s, raw_s, misc_s, aq_s, iq_s) = _inproj_sample(
            xs, mod_s, g_norm1[l], w_nat, w_t, w_conv[l], prow, pcol, state_conv[l], past)
        og_s, ssm_s = _gdn_sample(qkvn_s, z_s, misc_s, g_gdn_norm[l], state_ssm[l], _pick(db, (8,)))
        ik_s = slab_s[:, 0:IDX_DIM]
        iw_s = slab_s[:, IDX_DIM:IDX_DIM + IDX_HEADS]
        scores = _sample_index_scores(iq_s, iw_s, ik_s, cache_idx_k[l], page_table)
        bias = _sample_select(scores, past + ts)
        oa_s = _sample_attention(aq_s, bias, knat_s, vnat_s, cache_k[l], cache_v[l], page_table)
        spec_s = lambda k: pl.BlockSpec((db, d), lambda i, j: (i, k))
        xs = _post(og_s, oa_s, xs, mod_s, spec_s, g_norm2[l], g_final, wo, wfi, wfo, db, tf, final)
        new_conv_s = jnp.concatenate([state_conv[l][:, 1:, :], raw_s[:, None, :]], axis=1)
        new_s.append((knat_s.reshape(db, ts, N_KV, HEAD_DIM), vnat_s.reshape(db, ts, N_KV, HEAD_DIM),
                      ik_s.reshape(db, ts, IDX_DIM), new_conv_s, ssm_s))

    stack = lambda states, n: jnp.stack([s[n] for s in states], axis=0)
    return (xp, xs.reshape(db, ts, d),
            stack(new_p, 0), stack(new_p, 1), stack(new_p, 2), stack(new_p, 3), stack(new_p, 4),
            stack(new_s, 0), stack(new_s, 1), stack(new_s, 2), stack(new_s, 3), stack(new_s, 4))
```

```python
import functools
import math

import numpy as np
import jax
import jax.numpy as jnp
from jax import lax
from jax.experimental import pallas as pl
from jax.experimental.pallas import tpu as pltpu

F32 = jnp.float32
BF = jnp.bfloat16
I32 = jnp.int32

HEAD_DIM = 128
GDN_HEADS = 4
ATT_HEADS = 4
N_KV = 2
IDX_HEADS = 8
IDX_DIM = 64
CONV_W = 4
TOPK_MAX = 256
QBLOCK = 128
ROPE_THETA = 500000.0
EPS = 1e-6
GDN_W = GDN_HEADS * HEAD_DIM
ATT_W = ATT_HEADS * HEAD_DIM
KV_W = N_KV * HEAD_DIM
IDX_W = IDX_HEADS * IDX_DIM
QKV_W = 3 * GDN_W

GDN_CHUNK = 128
ATT_TK = 256
IDX_TK = 512
ATT_BLK = 1024
LANES = 128
NAT_W = QKV_W + GDN_W + KV_W + LANES
T_AQ, T_AK, T_IQ, T_IK, T_MISC, T_AV = 0, 512, 768, 1280, 1344, 1360
T_ROWS = T_AV + KV_W
VMEM_LIMIT = 56 * 1024 * 1024
INT_MIN = np.int32(-2 ** 31)
NEG_INF = float("-inf")


def _params(sem):
    return pltpu.CompilerParams(dimension_semantics=sem, vmem_limit_bytes=VMEM_LIMIT)


def _dot(a, b):
    return jnp.dot(a, b, preferred_element_type=F32)


def _dot_nt(a, b):
    return lax.dot_general(a, b, (((1,), (1,)), ((), ())), preferred_element_type=F32)


def _split3(x):
    hi = x.astype(BF)
    r1 = x - hi.astype(F32)
    mid = r1.astype(BF)
    lo = (r1 - mid.astype(F32)).astype(BF)
    return hi, mid, lo


def _mm_hi(a, b):
    ah = a.astype(BF)
    al = (a - ah.astype(F32)).astype(BF)
    bh = b.astype(BF)
    bl = (b - bh.astype(F32)).astype(BF)
    return _dot(ah, bh) + (_dot(ah, bl) + _dot(al, bh))


def _sigmoid(x):
    return 1.0 / (1.0 + jnp.exp(-x))


def _silu(x):
    return x * _sigmoid(x)


def _softplus(x):
    return jnp.maximum(x, 0.0) + jnp.log(1.0 + jnp.exp(-jnp.abs(x)))


def _rmsnorm(x, g):
    return x * lax.rsqrt(jnp.mean(x * x, axis=-1, keepdims=True) + EPS) * g


def _mod_kernel(c_ref, w_ref, b_ref, o_ref):
    s = _silu(c_ref[...]).astype(BF)
    o_ref[...] = _dot(s, w_ref[...].astype(BF)) + b_ref[...]


def _modulation(c_all, w_ada, b_ada):
    n, d = c_all.shape
    cols = w_ada.shape[1]
    tn = d
    return pl.pallas_call(
        _mod_kernel,
        grid=(cols // tn,),
        in_specs=[pl.BlockSpec((n, d), lambda j: (0, 0)),
                  pl.BlockSpec((d, tn), lambda j: (0, j)),
                  pl.BlockSpec((1, tn), lambda j: (0, j))],
        out_specs=pl.BlockSpec((n, tn), lambda j: (0, j)),
        out_shape=jax.ShapeDtypeStruct((n, cols), F32),
        compiler_params=_params(("arbitrary",)),
        name="modulation",
    )(c_all, w_ada, b_ada.reshape(1, cols))


def _rope_rows(rt_ref, base, half, cos, sin):
    x1 = rt_ref[base:base + half, :]
    x2 = rt_ref[base + half:base + 2 * half, :]
    rt_ref[base:base + half, :] = x1 * cos - x2 * sin
    rt_ref[base + half:base + 2 * half, :] = x2 * cos + x1 * sin


def _project(x_ref, sc_ref, sh_ref, g1_ref, wn_ref, wt_ref, cs128_ref, cs64_ref, pcol_ref, rt_ref):
    h = _rmsnorm(x_ref[...], g1_ref[...]) * (1.0 + sc_ref[...]) + sh_ref[...]
    hb = h.astype(BF)
    nat = _dot(hb, wn_ref[...])
    rt_ref[...] = _dot_nt(wt_ref[...], hb)
    half = HEAD_DIM // 8
    cos, sin = cs128_ref[0:half, :], cs128_ref[half:2 * half, :]
    for hd in range(ATT_HEADS):
        _rope_rows(rt_ref, T_AQ + hd * HEAD_DIM, half, cos, sin)
    for hd in range(N_KV):
        _rope_rows(rt_ref, T_AK + hd * HEAD_DIM, half, cos, sin)
    half = IDX_DIM // 8
    cos, sin = cs64_ref[0:half, :], cs64_ref[half:2 * half, :]
    for hd in range(IDX_HEADS):
        _rope_rows(rt_ref, T_IQ + hd * IDX_DIM, half, cos, sin)
    _rope_rows(rt_ref, T_IK, half, cos, sin)
    mt = rt_ref[T_MISC:T_MISC + 16, :]
    r = lax.broadcasted_iota(I32, mt.shape, 0)
    a_log, dt_b = pcol_ref[:, 0:1], pcol_ref[:, 1:2]
    gate = -jnp.exp(a_log) * _softplus(mt + dt_b)
    mt = jnp.where(r < IDX_HEADS, mt * IDX_HEADS ** -0.5, jnp.where(r < IDX_HEADS + GDN_HEADS, _sigmoid(mt), gate))
    rt_ref[T_MISC:T_MISC + 16, :] = mt
    return nat, mt


def _misc_natural(m, prow_ref):
    lane = lax.broadcasted_iota(I32, m.shape, 1)
    gate = -jnp.exp(prow_ref[0:1, :]) * _softplus(m + prow_ref[1:2, :])
    return jnp.where(lane < GDN_HEADS, _sigmoid(m), jnp.where(lane < 3 * GDN_HEADS, gate, 0.0))


def _qkv_post(conv, qkvn_ref):
    c = _silu(conv)
    for hb in range(2 * GDN_HEADS):
        xh = c[:, hb * HEAD_DIM:(hb + 1) * HEAD_DIM]
        n = xh * lax.rsqrt(jnp.sum(xh * xh, axis=-1, keepdims=True) + EPS)
        if hb < GDN_HEADS:
            n = n * HEAD_DIM ** -0.5
        qkvn_ref[:, hb * HEAD_DIM:(hb + 1) * HEAD_DIM] = n
    qkvn_ref[:, 2 * GDN_W:] = c[:, 2 * GDN_W:]


def _inproj_prompt_kernel(tm, x_ref, sc_ref, sh_ref, g1_ref, wn_ref, wt_ref, wconv_ref, cs128_ref, cs64_ref,
                          prow_ref, pcol_ref, tril_ref,
                          qkvn_ref, z_ref, vnat_ref, knat_ref, ikT_ref, tail_ref, misc_ref, gct_ref,
                          qT_ref, iqT_ref, iwT_ref, kbf_ref, vT_ref, ikbf_ref,
                          xs_ref, rt_ref):
    i = pl.program_id(1)
    nat, mt = _project(x_ref, sc_ref, sh_ref, g1_ref, wn_ref, wt_ref, cs128_ref, cs64_ref, pcol_ref, rt_ref)

    @pl.when(i == 0)
    def _():
        xs_ref[0:8, :] = jnp.zeros((8, QKV_W), F32)

    @pl.when(i > 0)
    def _():
        xs_ref[0:8, :] = xs_ref[tm:tm + 8, :]

    xs_ref[8:tm + 8, :] = nat[:, 0:QKV_W]
    conv = wconv_ref[0:1, :] * xs_ref[5:tm + 5, :]
    for t in range(1, CONV_W):
        conv = conv + wconv_ref[t:t + 1, :] * xs_ref[5 + t:tm + 5 + t, :]
    tail_ref[...] = xs_ref[tm:tm + 8, :]
    _qkv_post(conv, qkvn_ref)
    z_ref[...] = nat[:, QKV_W:QKV_W + GDN_W]
    vnat_ref[...] = nat[:, QKV_W + GDN_W:QKV_W + GDN_W + KV_W]

    gm = _misc_natural(nat[:, NAT_W - LANES:], prow_ref)
    tril = tril_ref[...]
    hi, mid, lo = _split3(gm)
    gc = _dot(tril, hi) + (_dot(tril, mid) + _dot(tril, lo))
    lane = lax.broadcasted_iota(I32, gm.shape, 1)
    misc_ref[...] = jnp.where(lane < 2 * GDN_HEADS, gm, gc)
    hi, mid, lo = _split3(mt)
    gct = _dot_nt(hi, tril) + (_dot_nt(mid, tril) + _dot_nt(lo, tril))
    r = lax.broadcasted_iota(I32, mt.shape, 0)
    bg = jnp.where(r < IDX_HEADS + GDN_HEADS, mt, gct)[8:16, :]
    for cc in range(tm // GDN_CHUNK):
        gct_ref[cc] = bg[:, cc * GDN_CHUNK:(cc + 1) * GDN_CHUNK]

    iwT_ref[...] = mt[0:IDX_HEADS, :]
    qT_ref[...] = rt_ref[T_AQ:T_AQ + ATT_W, :].astype(BF)
    iqT_ref[...] = rt_ref[T_IQ:T_IQ + IDX_W, :].astype(BF)
    kn = rt_ref[T_AK:T_AK + KV_W, :].T
    knat_ref[...] = kn
    kbf_ref[...] = kn.astype(BF)
    ikT_ref[...] = rt_ref[T_IK:T_IK + IDX_DIM, :]
    ikbf_ref[...] = rt_ref[T_IK:T_IK + LANES, :].T[:, 0:IDX_DIM].astype(BF)
    for cc in range(tm // ATT_TK):
        vT_ref[cc] = rt_ref[T_AV:T_AV + KV_W, cc * ATT_TK:(cc + 1) * ATT_TK].astype(BF)


def _inproj_sample_kernel(x_ref, sc_ref, sh_ref, g1_ref, wn_ref, wt_ref, wconv_ref, cs128_ref, cs64_ref,
                          prow_ref, pcol_ref, s0_ref, s1_ref, s2_ref,
                          qkvn_ref, z_ref, vnat_ref, knat_ref, slab_ref, raw_ref, misc_ref, aq_ref, iq_ref,
                          rt_ref):
    nat, _ = _project(x_ref, sc_ref, sh_ref, g1_ref, wn_ref, wt_ref, cs128_ref, cs64_ref, pcol_ref, rt_ref)
    raw = nat[:, 0:QKV_W]
    raw_ref[...] = raw
    conv = wconv_ref[0:1, :] * s0_ref[...]
    conv = conv + wconv_ref[1:2, :] * s1_ref[...]
    conv = conv + wconv_ref[2:3, :] * s2_ref[...]
    conv = conv + wconv_ref[3:4, :] * raw
    _qkv_post(conv, qkvn_ref)
    z_ref[...] = nat[:, QKV_W:QKV_W + GDN_W]
    vnat_ref[...] = nat[:, QKV_W + GDN_W:QKV_W + GDN_W + KV_W]
    misc_ref[...] = _misc_natural(nat[:, NAT_W - LANES:], prow_ref)
    aq_ref[...] = rt_ref[T_AQ:T_AQ + ATT_W, :].T
    iq_ref[...] = rt_ref[T_IQ:T_IQ + IDX_W, :].T
    knat_ref[...] = rt_ref[T_AK:T_AK + KV_W, :].T
    slab_ref[...] = rt_ref[T_IK:T_IK + LANES, :].T


def _inproj_weights(w_in, a_log, dt_bias):
    offs = np.cumsum([0, QKV_W, GDN_W, GDN_HEADS, GDN_HEADS, ATT_W, KV_W, KV_W, IDX_W, IDX_DIM, IDX_HEADS])
    qkv, z, beta, a, aq, ak, av, iq, ik, iw = [w_in[:, offs[n]:offs[n + 1]] for n in range(10)]
    d = w_in.shape[0]
    pad = jnp.zeros((d, LANES - 3 * GDN_HEADS), w_in.dtype)
    w_nat = jnp.concatenate([qkv, z, av, beta, a, a, pad], axis=1).astype(BF)
    w_t = jnp.concatenate([aq, ak, iq, ik, iw, beta, a, av], axis=1).T.astype(BF)
    zrow = jnp.zeros((LANES - 3 * GDN_HEADS,), F32)
    prow = jnp.stack([jnp.concatenate([jnp.zeros((GDN_HEADS,), F32), a_log, a_log, zrow]),
                      jnp.concatenate([jnp.zeros((GDN_HEADS,), F32), dt_bias, dt_bias, zrow])])
    z12 = jnp.zeros((IDX_HEADS + GDN_HEADS,), F32)
    pcol = jnp.stack([jnp.concatenate([z12, a_log]), jnp.concatenate([z12, dt_bias])], axis=1)
    return w_nat, w_t, prow, pcol


def _rope_tables(pos, dim):
    half = dim // 8
    inv = ROPE_THETA ** (-jnp.arange(half, dtype=F32) / half)
    ang = pos.astype(F32)[:, None] * inv[None, :]
    return jnp.concatenate([jnp.cos(ang).T, jnp.sin(ang).T], axis=0)


def _inproj_prompt(x, mod, g1, w_nat, w_t, w_conv, prow, pcol, tm):
    bp, tp, d = x.shape
    nt = tp // tm
    rows = bp * tp
    pos = jnp.arange(tp, dtype=I32)
    cs128, cs64 = _rope_tables(pos, HEAD_DIM), _rope_tables(pos, IDX_DIM)
    ri = np.arange(tm)
    tril = jnp.asarray((ri[:, None] // GDN_CHUNK == ri[None, :] // GDN_CHUNK) & (ri[None, :] <= ri[:, None]), BF)
    const = lambda shape: pl.BlockSpec(shape, lambda b, i: (0,) * len(shape))
    rowblk = lambda w: pl.BlockSpec((tm, w), lambda b, i: (b * nt + i, 0))
    colblk = lambda h: pl.BlockSpec((h, tm), lambda b, i: (0, b * nt + i))
    in_specs = [
        pl.BlockSpec((None, tm, d), lambda b, i: (b, i, 0)),
        pl.BlockSpec((None, 1, d), lambda b, i: (b, 0, 1)),
        pl.BlockSpec((None, 1, d), lambda b, i: (b, 0, 0)),
        const((1, d)), const((d, NAT_W)), const((T_ROWS, d)), const((CONV_W, QKV_W)),
        pl.BlockSpec((HEAD_DIM // 4, tm), lambda b, i: (0, i)),
        pl.BlockSpec((IDX_DIM // 4, tm), lambda b, i: (0, i)),
        const((2, LANES)), const((16, 2)), const((tm, tm)),
    ]
    out_shape = [
        jax.ShapeDtypeStruct((rows, QKV_W), F32), jax.ShapeDtypeStruct((rows, GDN_W), F32),
        jax.ShapeDtypeStruct((rows, KV_W), F32), jax.ShapeDtypeStruct((rows, KV_W), F32),
        jax.ShapeDtypeStruct((bp, IDX_DIM, tp), F32), jax.ShapeDtypeStruct((bp, 8, QKV_W), F32),
        jax.ShapeDtypeStruct((rows, LANES), F32), jax.ShapeDtypeStruct((rows // GDN_CHUNK, 8, GDN_CHUNK), F32),
        jax.ShapeDtypeStruct((ATT_W, rows), BF), jax.ShapeDtypeStruct((IDX_W, rows), BF),
        jax.ShapeDtypeStruct((IDX_HEADS, rows), F32), jax.ShapeDtypeStruct((rows, KV_W), BF),
        jax.ShapeDtypeStruct((rows // ATT_TK, KV_W, ATT_TK), BF), jax.ShapeDtypeStruct((rows, IDX_DIM), BF),
    ]
    out_specs = [
        rowblk(QKV_W), rowblk(GDN_W), rowblk(KV_W), rowblk(KV_W),
        pl.BlockSpec((None, IDX_DIM, tm), lambda b, i: (b, 0, i)),
        pl.BlockSpec((None, 8, QKV_W), lambda b, i: (b, 0, 0)),
        rowblk(LANES),
        pl.BlockSpec((tm // GDN_CHUNK, 8, GDN_CHUNK), lambda b, i: (b * nt + i, 0, 0)),
        colblk(ATT_W), colblk(IDX_W), colblk(IDX_HEADS), rowblk(KV_W),
        pl.BlockSpec((tm // ATT_TK, KV_W, ATT_TK), lambda b, i: (b * nt + i, 0, 0)),
        rowblk(IDX_DIM),
    ]
    return pl.pallas_call(
        functools.partial(_inproj_prompt_kernel, tm),
        grid=(bp, nt), in_specs=in_specs, out_specs=out_specs, out_shape=out_shape,
        scratch_shapes=[pltpu.VMEM((tm + 8, QKV_W), F32), pltpu.VMEM((T_ROWS, tm), F32)],
        compiler_params=_params(("arbitrary", "arbitrary")),
        name="inproj_prompt",
    )(x, mod, mod, g1.reshape(1, d), w_nat, w_t, w_conv, cs128, cs64, prow, pcol, tril)


def _inproj_sample(x, mod, g1, w_nat, w_t, w_conv, prow, pcol, conv_state, past):
    db, d = x.shape
    pos = jnp.full((db,), past, I32)
    cs128, cs64 = _rope_tables(pos, HEAD_DIM), _rope_tables(pos, IDX_DIM)
    full = lambda shape: pl.BlockSpec(shape, lambda i: (0,) * len(shape))
    in_specs = [
        full((db, d)),
        pl.BlockSpec((db, d), lambda i: (0, 1)), pl.BlockSpec((db, d), lambda i: (0, 0)),
        full((1, d)), full((d, NAT_W)), full((T_ROWS, d)), full((CONV_W, QKV_W)),
        full((HEAD_DIM // 4, db)), full((IDX_DIM // 4, db)), full((2, LANES)), full((16, 2)),
        full((db, QKV_W)), full((db, QKV_W)), full((db, QKV_W)),
    ]
    widths = [QKV_W, GDN_W, KV_W, KV_W, LANES, QKV_W, LANES, ATT_W, IDX_W]
    return pl.pallas_call(
        _inproj_sample_kernel,
        grid=(1,), in_specs=in_specs,
        out_specs=[full((db, w)) for w in widths],
        out_shape=[jax.ShapeDtypeStruct((db, w), F32) for w in widths],
        scratch_shapes=[pltpu.VMEM((T_ROWS, db), F32)],
        compiler_params=_params(("arbitrary",)),
        name="inproj_sample",
    )(x, mod, mod, g1.reshape(1, d), w_nat, w_t, w_conv, cs128, cs64, prow, pcol,
      conv_state[:, 0], conv_state[:, 1], conv_state[:, 2])


def _unit_lower_inverses(lows):
    c = lows[0].shape[0]
    ii = lax.broadcasted_iota(I32, (c, c), 0)
    jj = lax.broadcasted_iota(I32, (c, c), 1)
    eye = jnp.where(ii == jj, 1.0, 0.0)
    levels = int(math.log2(c)) - 1
    ts = [eye - low for low in lows]
    ps = [_mm_hi(low, low) for low in lows]
    for lvl in range(levels):
        if lvl == levels - 1:
            ts = [t + _mm_hi(t, p) for t, p in zip(ts, ps)]
        else:
            both = [_mm_hi(jnp.concatenate([t, p], axis=0), p) for t, p in zip(ts, ps)]
            ts = [t + b[0:c] for t, b in zip(ts, both)]
            ps = [b[c:] for b in both]
    return ts


def _gdn_prompt_kernel(tg, qkvn_ref, z_ref, misc_ref, gct_ref, gn_ref, o_ref, ssm_ref, s_ref):
    i = pl.program_id(1)

    @pl.when(i == 0)
    def _():
        s_ref[...] = jnp.zeros(s_ref.shape, F32)

    c = GDN_CHUNK
    ii = lax.broadcasted_iota(I32, (c, c), 0)
    jj = lax.broadcasted_iota(I32, (c, c), 1)
    pairs = [(cc, hd) for cc in range(tg // c) for hd in range(GDN_HEADS)]

    qs, ks, gcs, rhs, lows, intras = {}, {}, {}, {}, [], {}
    for cc, hd in pairs:
        r0, lo = cc * c, hd * HEAD_DIM
        q = qkvn_ref[r0:r0 + c, lo:lo + HEAD_DIM]
        k = qkvn_ref[r0:r0 + c, GDN_W + lo:GDN_W + lo + HEAD_DIM]
        v = qkvn_ref[r0:r0 + c, 2 * GDN_W + lo:2 * GDN_W + lo + HEAD_DIM]
        beta_c = misc_ref[r0:r0 + c, hd:hd + 1]
        gc_c = misc_ref[r0:r0 + c, 2 * GDN_HEADS + hd:2 * GDN_HEADS + hd + 1]
        gc_r = gct_ref[cc, GDN_HEADS + hd:GDN_HEADS + hd + 1, :]
        decay = jnp.where(ii >= jj, jnp.exp(jnp.where(ii >= jj, gc_c - gc_r, 0.0)), 0.0)
        kb = k * beta_c
        both = _dot_nt(jnp.concatenate([kb, q], axis=0).astype(BF), k.astype(BF))
        lows.append(jnp.where(ii > jj, both[0:c] * decay, 0.0))
        intras[cc, hd] = (both[c:] * decay).astype(BF)
        rhs[cc, hd] = jnp.concatenate([v * beta_c, kb * jnp.exp(gc_c)], axis=1).astype(BF)
        qs[cc, hd], ks[cc, hd], gcs[cc, hd] = q, k, gc_c
    ts = _unit_lower_inverses(lows)
    uws = {p: _dot(t.astype(BF), rhs[p]) for p, t in zip(pairs, ts)}

    for cc, hd in pairs:
        r0, lo = cc * c, hd * HEAD_DIM
        q, k, gc_c, uw = qs[cc, hd], ks[cc, hd], gcs[cc, hd], uws[cc, hd]
        s = s_ref[hd]
        ws = _dot(jnp.concatenate([uw[:, HEAD_DIM:], q * jnp.exp(gc_c)], axis=0).astype(BF), s.astype(BF))
        v_new = (uw[:, 0:HEAD_DIM] - ws[0:c]).astype(BF)
        o = ws[c:] + _dot(intras[cc, hd], v_new)
        g_last = gc_c[c - 1:c, :]
        kd = k * jnp.exp(g_last - gc_c)
        s_ref[hd] = s * jnp.exp(g_last) + _dot(kd.T.astype(BF), v_new)
        o = _rmsnorm(o, gn_ref[...]) * _silu(z_ref[r0:r0 + c, lo:lo + HEAD_DIM])
        o_ref[r0:r0 + c, lo:lo + HEAD_DIM] = o.astype(BF)
    ssm_ref[...] = s_ref[...]


def _gdn_prompt(qkvn, z, misc, gct, g_norm, bp, tp, tg):
    nt = tp // tg
    rows = bp * tp
    rowblk = lambda w: pl.BlockSpec((tg, w), lambda b, i: (b * nt + i, 0))
    return pl.pallas_call(
        functools.partial(_gdn_prompt_kernel, tg),
        grid=(bp, nt),
        in_specs=[rowblk(QKV_W), rowblk(GDN_W), rowblk(LANES),
                  pl.BlockSpec((tg // GDN_CHUNK, 8, GDN_CHUNK), lambda b, i: (b * nt + i, 0, 0)),
                  pl.BlockSpec((1, HEAD_DIM), lambda b, i: (0, 0))],
        out_specs=[rowblk(GDN_W),
                   pl.BlockSpec((None, GDN_HEADS, HEAD_DIM, HEAD_DIM), lambda b, i: (b, 0, 0, 0))],
        out_shape=[jax.ShapeDtypeStruct((rows, GDN_W), BF),
                   jax.ShapeDtypeStruct((bp, GDN_HEADS, HEAD_DIM, HEAD_DIM), F32)],
        scratch_shapes=[pltpu.VMEM((GDN_HEADS, HEAD_DIM, HEAD_DIM), F32)],
        compiler_params=_params(("arbitrary", "arbitrary")),
        name="gdn_prompt",
    )(qkvn, z, misc, gct, g_norm.reshape(1, HEAD_DIM))


def _gdn_sample_kernel(nb, qkvn_ref, z_ref, misc_ref, gn_ref, s_ref, o_ref, ssm_ref):
    for bi in range(nb):
        for hd in range(GDN_HEADS):
            lo = hd * HEAD_DIM
            q = qkvn_ref[bi:bi + 1, lo:lo + HEAD_DIM]
            k = qkvn_ref[bi:bi + 1, GDN_W + lo:GDN_W + lo + HEAD_DIM]
            v = qkvn_ref[bi:bi + 1, 2 * GDN_W + lo:2 * GDN_W + lo + HEAD_DIM]
            beta = misc_ref[bi:bi + 1, hd:hd + 1]
            g = misc_ref[bi:bi + 1, GDN_HEADS + hd:GDN_HEADS + hd + 1]
            kcol = jnp.broadcast_to(k, (HEAD_DIM, HEAD_DIM)).T
            qcol = jnp.broadcast_to(q, (HEAD_DIM, HEAD_DIM)).T
            s = s_ref[bi, hd] * jnp.exp(g)
            ks = jnp.sum(kcol * s, axis=0, keepdims=True)
            delta = (v - ks) * beta
            s = s + kcol * delta
            ssm_ref[bi, hd] = s
            o = jnp.sum(qcol * s, axis=0, keepdims=True)
            o = _rmsnorm(o, gn_ref[...]) * _silu(z_ref[bi:bi + 1, lo:lo + HEAD_DIM])
            o_ref[bi:bi + 1, lo:lo + HEAD_DIM] = o.astype(BF)


def _gdn_sample(qkvn, z, misc, g_norm, state, nb):
    db = qkvn.shape[0]
    rowblk = lambda w: pl.BlockSpec((nb, w), lambda i: (i, 0))
    sblk = pl.BlockSpec((nb, GDN_HEADS, HEAD_DIM, HEAD_DIM), lambda i: (i, 0, 0, 0))
    return pl.pallas_call(
        functools.partial(_gdn_sample_kernel, nb),
        grid=(db // nb,),
        in_specs=[rowblk(QKV_W), rowblk(GDN_W), rowblk(LANES), pl.BlockSpec((1, HEAD_DIM), lambda i: (0, 0)), sblk],
        out_specs=[rowblk(GDN_W), sblk],
        out_shape=[jax.ShapeDtypeStruct((db, GDN_W), BF), jax.ShapeDtypeStruct(state.shape, F32)],
        compiler_params=_params(("arbitrary",)),
        name="gdn_sample",
    )(qkvn, z, misc, g_norm.reshape(1, HEAD_DIM), state)


def _ordered_word_to_float(u):
    s = u ^ INT_MIN
    return lax.bitcast_convert_type(s ^ ((s >> 31) & np.int32(0x7FFFFFFF)), F32)


def _select_topk_bias(i_ref, nch, tk, k_sel, idx_bits, taken=0.0, dropped=NEG_INF):
    nl = i_ref.shape[1]
    row = lax.broadcasted_iota(I32, (tk, 1), 0)

    def chunk(c):
        return pl.ds(pl.multiple_of(c * tk, tk), tk)

    def count(pred):
        def body(c, acc):
            m = pred(i_ref[chunk(c), :], c * tk + row)
            return acc + jnp.sum(jnp.where(m, 1, 0).astype(I32).reshape(tk // 8, 8, nl), axis=0)
        acc = lax.fori_loop(0, nch, body, jnp.zeros((8, nl), I32))
        return jnp.sum(acc, axis=0, keepdims=True)

    def bit_body(i, carry):
        t_u, n_ge = carry
        cand_u = t_u | jnp.left_shift(jnp.int32(1), 31 - i)
        cand = _ordered_word_to_float(cand_u)
        cnt = count(lambda sc, idx: sc >= cand)
        take = cnt >= k_sel
        return jnp.where(take, cand_u, t_u), jnp.where(take, cnt, n_ge)

    t_u, n_ge = lax.fori_loop(0, 32, bit_body, (jnp.zeros((1, nl), I32), jnp.full((1, nl), -1, I32)))
    thr = jnp.where((t_u >> 23) == 0, NEG_INF, _ordered_word_to_float(t_u))

    def tie_break(_):
        need = k_sel - count(lambda sc, idx: sc > thr)

        def idx_body(i, x):
            cand = x | jnp.left_shift(jnp.int32(1), idx_bits - 1 - i)
            cnt = count(lambda sc, idx: (sc == thr) & (idx < cand))
            return jnp.where(cnt < need, cand, x)

        return lax.fori_loop(0, idx_bits, idx_body, jnp.zeros((1, nl), I32))

    x = lax.cond(jnp.max(n_ge) > k_sel, tie_break, lambda _: jnp.full((1, nl), 2 ** idx_bits - 1, I32), 0)

    def write(c, carry):
        sc = i_ref[chunk(c), :]
        sel = (sc > thr) | ((sc == thr) & (c * tk + row <= x))
        i_ref[chunk(c), :] = jnp.where(sel & (sc > NEG_INF), taken, dropped)
        return carry

    lax.fori_loop(0, nch, write, 0)


def _attn_prompt_kernel(tq, tk, blk, k_sel, idx_bits, qT_ref, iqT_ref, iwT_ref, kbf_ref, vT_ref, ikbf_ref, o_ref,
                        i_ref):
    t0 = pl.program_id(1) * tq
    nch = (t0 + tq + tk - 1) // tk
    nblk = (t0 + tq + blk - 1) // blk
    qpos = t0 + lax.broadcasted_iota(I32, (1, tq), 1)
    row = lax.broadcasted_iota(I32, (tk, 1), 0)
    w = iwT_ref[...]
    pairs = IDX_HEADS // 2
    rhs = [jnp.concatenate([iqT_ref[(2 * p) * IDX_DIM:(2 * p + 1) * IDX_DIM, :],
                            iqT_ref[(2 * p + 1) * IDX_DIM:(2 * p + 2) * IDX_DIM, :]], axis=1) for p in range(pairs)]

    def chunk(c):
        return pl.ds(pl.multiple_of(c * tk, tk), tk)

    def idx_body(c, carry):
        ikc = ikbf_ref[chunk(c), :]
        acc = jnp.zeros((tk, tq), F32)
        for p in range(pairs):
            d = jnp.maximum(_dot(ikc, rhs[p]), 0.0)
            acc = acc + d[:, 0:tq] * w[2 * p:2 * p + 1, :] + d[:, tq:] * w[2 * p + 1:2 * p + 2, :]
        i_ref[chunk(c), :] = jnp.where(c * tk + row <= qpos, acc, NEG_INF)
        return carry

    lax.fori_loop(0, nch, idx_body, 0)

    def fill_body(c, carry):
        i_ref[chunk(c), :] = jnp.full((tk, tq), NEG_INF, F32)
        return carry

    lax.fori_loop(nch, nblk * (blk // tk), fill_body, 0)
    _select_topk_bias(i_ref, nch, tk, k_sel, idx_bits)

    scale = HEAD_DIM ** -0.5
    rep = ATT_HEADS // N_KV
    sub = ATT_TK
    nsub = blk // sub
    qg = [jnp.concatenate([qT_ref[(rep * g + r) * HEAD_DIM:(rep * g + r + 1) * HEAD_DIM, :] for r in range(rep)],
                          axis=1) for g in range(N_KV)]

    def body(c, carry):
        out = []
        for g in range(N_KV):
            m, l, acc = carry[g]
            ss = []
            for i in range(nsub):
                rows = pl.ds(pl.multiple_of(c * blk + i * sub, sub), sub)
                bias = i_ref[rows, :]
                ss.append(_dot(kbf_ref[rows, g * HEAD_DIM:(g + 1) * HEAD_DIM], qg[g]) * scale
                          + jnp.concatenate([bias] * rep, axis=1))
            m_new = m
            for s in ss:
                m_new = jnp.maximum(m_new, jnp.max(s, axis=0, keepdims=True))
            m_safe = jnp.where(m_new == NEG_INF, 0.0, m_new)
            alpha = jnp.exp(m - m_safe)
            l = alpha * l
            acc = acc * alpha
            for i, s in enumerate(ss):
                p = jnp.exp(s - m_safe)
                l = l + jnp.sum(p, axis=0, keepdims=True)
                acc = acc + _dot(vT_ref[c * nsub + i, g * HEAD_DIM:(g + 1) * HEAD_DIM, :], p.astype(BF))
            out.append((m_new, l, acc))
        return tuple(out)

    init = tuple((jnp.full((1, rep * tq), NEG_INF, F32), jnp.zeros((1, rep * tq), F32),
                  jnp.zeros((HEAD_DIM, rep * tq), F32)) for _ in range(N_KV))
    res = lax.fori_loop(0, nblk, body, init)
    for g in range(N_KV):
        _, l, acc = res[g]
        o_t = acc / l
        for r in range(rep):
            hd = rep * g + r
            o_ref[:, hd * HEAD_DIM:(hd + 1) * HEAD_DIM] = o_t[:, r * tq:(r + 1) * tq].T.astype(BF)


def _attn_prompt(qT, iqT, iwT, kbf, vT, ikbf, bp, tp):
    tq = QBLOCK
    tk = _pick(tp, (IDX_TK, ATT_TK))
    blk = _pick(tp, (ATT_BLK, IDX_TK, ATT_TK))
    assert blk % tk == 0 and blk % ATT_TK == 0
    nq = tp // tq
    rows = bp * tp
    k_sel = min(TOPK_MAX, tp // 4)
    idx_bits = max(1, int(tp - 1).bit_length())
    colblk = lambda h: pl.BlockSpec((h, tq), lambda b, j: (0, b * nq + j))
    return pl.pallas_call(
        functools.partial(_attn_prompt_kernel, tq, tk, blk, k_sel, idx_bits),
        grid=(bp, nq),
        in_specs=[colblk(ATT_W), colblk(IDX_W), colblk(IDX_HEADS),
                  pl.BlockSpec((tp, KV_W), lambda b, j: (b, 0)),
                  pl.BlockSpec((tp // ATT_TK, KV_W, ATT_TK), lambda b, j: (b, 0, 0)),
                  pl.BlockSpec((tp, IDX_DIM), lambda b, j: (b, 0))],
        out_specs=pl.BlockSpec((tq, ATT_W), lambda b, j: (b * nq + j, 0)),
        out_shape=jax.ShapeDtypeStruct((rows, ATT_W), BF),
        scratch_shapes=[pltpu.VMEM((tp, tq), F32)],
        compiler_params=_params(("arbitrary", "arbitrary")),
        name="attn_prompt",
    )(qT, iqT, iwT, kbf, vT, ikbf)


def _sidx_kernel(npg, pt_ref, iq_ref, iw_ref, ikn_ref, *refs):
    pages, out_ref = refs[:npg], refs[npg]
    iq = iq_ref[...].astype(BF)
    w = iw_ref[...]
    for p in range(npg):
        d = jnp.maximum(_dot(iq, pages[p][...].astype(BF)), 0.0)
        out_ref[:, p * LANES:(p + 1) * LANES] = jnp.sum(d * w, axis=0, keepdims=True)
    dn = jnp.sum(iq.astype(F32) * ikn_ref[...].astype(BF).astype(F32), axis=1, keepdims=True)
    sn = jnp.sum(jnp.maximum(dn, 0.0) * w, axis=0, keepdims=True)
    lane = lax.broadcasted_iota(I32, (1, LANES), 1)
    out_ref[:, npg * LANES:(npg + 1) * LANES] = jnp.where(lane == 0, sn, NEG_INF)


def _sample_index_scores(iq, iw, ik_new, cache_ik_t, page_table):
    db, npg = page_table.shape
    page = cache_ik_t.shape[2]
    assert page == LANES
    width = (npg + 1) * LANES
    page_specs = [pl.BlockSpec((None, IDX_DIM, page), functools.partial(lambda p, b, pt: (pt[b, p], 0, 0), p))
                  for p in range(npg)]
    grid_spec = pltpu.PrefetchScalarGridSpec(
        num_scalar_prefetch=1, grid=(db,),
        in_specs=[pl.BlockSpec((None, IDX_HEADS, IDX_DIM), lambda b, pt: (b, 0, 0)),
                  pl.BlockSpec((None, IDX_HEADS, 1), lambda b, pt: (b, 0, 0)),
                  pl.BlockSpec((None, 1, IDX_DIM), lambda b, pt: (b, 0, 0))] + page_specs,
        out_specs=pl.BlockSpec((None, 1, width), lambda b, pt: (b, 0, 0)))
    out = pl.pallas_call(
        functools.partial(_sidx_kernel, npg), grid_spec=grid_spec,
        out_shape=jax.ShapeDtypeStruct((db, 1, width), F32),
        compiler_params=_params(("arbitrary",)),
        name="sample_index_scores",
    )(page_table, iq.reshape(db, IDX_HEADS, IDX_DIM), iw.reshape(db, IDX_HEADS, 1), ik_new.reshape(db, 1, IDX_DIM),
      *([cache_ik_t] * npg))
    return out.reshape(db, width)


def _ssel_kernel(nch, k_sel, idx_bits, s_ref, keep_ref, i_ref):
    for p in range(nch):
        i_ref[p * LANES:(p + 1) * LANES, :] = s_ref[:, p * LANES:(p + 1) * LANES].T
    _select_topk_bias(i_ref, nch, LANES, k_sel, idx_bits, taken=1.0, dropped=0.0)
    for p in range(nch):
        keep_ref[:, p * LANES:(p + 1) * LANES] = i_ref[p * LANES:(p + 1) * LANES, :].T


def _sample_select(scores, n_keys):
    db, width = scores.shape
    assert db == LANES
    nch = width // LANES
    k_sel = min(TOPK_MAX, n_keys // 4)
    idx_bits = max(1, int(width - 1).bit_length())
    return pl.pallas_call(
        functools.partial(_ssel_kernel, nch, k_sel, idx_bits),
        grid=(1,),
        in_specs=[pl.BlockSpec((db, width), lambda i: (0, 0))],
        out_specs=pl.BlockSpec((db, width), lambda i: (0, 0)),
        out_shape=jax.ShapeDtypeStruct((db, width), F32),
        scratch_shapes=[pltpu.VMEM((width, db), F32)],
        compiler_params=_params(("arbitrary",)),
        name="sample_select",
    )(scores)


def _sattn_kernel(npg, pt_ref, q_ref, keep_ref, kn_ref, vn_ref, spread_ref, *refs):
    kpages, vpages, o_ref = refs[:npg], refs[npg:2 * npg], refs[2 * npg]
    rep = ATT_HEADS // N_KV
    rows = N_KV * LANES
    scale = HEAD_DIM ** -0.5
    q = q_ref[...]
    q8 = jnp.concatenate([q, jnp.zeros((8 - ATT_HEADS, HEAD_DIM), F32)], axis=0).astype(BF)
    hrow = lax.broadcasted_iota(I32, (8, 1), 0)
    lane = lax.broadcasted_iota(I32, (1, LANES), 1)
    own_kv = lax.broadcasted_iota(I32, (1, rows), 1) % N_KV == hrow // rep
    spread = spread_ref[...]
    parts = []
    for p in range(npg):
        s2 = _dot_nt(q8, kpages[p][...].astype(BF))
        keep = jnp.broadcast_to(keep_ref[:, p * LANES:(p + 1) * LANES], (8, LANES)).astype(BF)
        keep2 = _dot(keep, spread)
        parts.append(jnp.where(own_kv & (keep2 > 0.5), s2 * scale, NEG_INF))
    kn = kn_ref[...].astype(BF).astype(F32)
    kn8 = jnp.where(hrow // rep == 0, kn[0:1, :], kn[1:2, :])
    s_new = jnp.sum(q8.astype(F32) * kn8, axis=1, keepdims=True)
    keep_new = keep_ref[:, npg * LANES:npg * LANES + 1]
    parts.append(jnp.where((lane == 0) & (keep_new > 0.5), s_new * scale, NEG_INF))
    s = jnp.concatenate(parts, axis=1)
    m = jnp.max(s, axis=1, keepdims=True)
    e = jnp.exp(s - m)
    pr = (e / jnp.sum(e, axis=1, keepdims=True)).astype(BF)
    o8 = jnp.zeros((8, HEAD_DIM), F32)
    for p in range(npg):
        o8 = o8 + _dot(pr[:, p * rows:(p + 1) * rows], vpages[p][...].astype(BF))
    vn = vn_ref[...].astype(BF).astype(F32)
    vn8 = jnp.where(hrow // rep == 0, vn[0:1, :], vn[1:2, :])
    o8 = o8 + pr[:, npg * rows:npg * rows + 1].astype(F32) * vn8
    for hd in range(ATT_HEADS):
        o_ref[:, hd * HEAD_DIM:(hd + 1) * HEAD_DIM] = o8[hd:hd + 1, :].astype(BF)


def _sample_attention(q, keep, k_new, v_new, cache_k, cache_v, page_table):
    db, npg = page_table.shape
    n_phys, page = cache_k.shape[0], cache_k.shape[1]
    assert page == LANES and N_KV == 2
    width = (npg + 1) * LANES
    rows = page * N_KV
    ck = cache_k.reshape(n_phys, rows, HEAD_DIM)
    cv = cache_v.reshape(n_phys, rows, HEAD_DIM)
    spread = jnp.asarray(np.arange(rows)[None, :] // N_KV == np.arange(page)[:, None], BF)
    page_specs = [pl.BlockSpec((None, rows, HEAD_DIM), functools.partial(lambda p, b, pt: (pt[b, p], 0, 0), p))
                  for p in range(npg)]
    grid_spec = pltpu.PrefetchScalarGridSpec(
        num_scalar_prefetch=1, grid=(db,),
        in_specs=[pl.BlockSpec((None, ATT_HEADS, HEAD_DIM), lambda b, pt: (b, 0, 0)),
                  pl.BlockSpec((None, 1, width), lambda b, pt: (b, 0, 0)),
                  pl.BlockSpec((None, N_KV, HEAD_DIM), lambda b, pt: (b, 0, 0)),
                  pl.BlockSpec((None, N_KV, HEAD_DIM), lambda b, pt: (b, 0, 0)),
                  pl.BlockSpec((page, rows), lambda b, pt: (0, 0))] + page_specs + page_specs,
        out_specs=pl.BlockSpec((None, 1, ATT_W), lambda b, pt: (b, 0, 0)))
    out = pl.pallas_call(
        functools.partial(_sattn_kernel, npg), grid_spec=grid_spec,
        out_shape=jax.ShapeDtypeStruct((db, 1, ATT_W), BF),
        compiler_params=_params(("arbitrary",)),
        name="sample_attention",
    )(page_table, q.reshape(db, ATT_HEADS, HEAD_DIM), keep.reshape(db, 1, width),
      k_new.reshape(db, N_KV, HEAD_DIM), v_new.reshape(db, N_KV, HEAD_DIM), spread, *([ck] * npg), *([cv] * npg))
    return out.reshape(db, ATT_W)


def _post_kernel(final, nff, og_ref, oa_ref, x_ref, ga1_ref, sc2_ref, sh2_ref, ga2_ref, g2_ref, gf_ref,
                 wo_ref, wg_ref, wu_ref, wd_ref, y_ref, x1_ref, h2_ref, acc_ref):
    jf = pl.program_id(1)

    @pl.when(jf == 0)
    def _():
        mixed = _dot(og_ref[...], wo_ref[0:GDN_W, :]) + _dot(oa_ref[...], wo_ref[GDN_W:, :])
        x1 = x_ref[...] + ga1_ref[...] * mixed
        x1_ref[...] = x1
        h2_ref[...] = (_rmsnorm(x1, g2_ref[...]) * (1.0 + sc2_ref[...]) + sh2_ref[...]).astype(BF)
        acc_ref[...] = jnp.zeros(acc_ref.shape, F32)

    h2 = h2_ref[...]
    act = (_silu(_dot(h2, wg_ref[...])) * _dot(h2, wu_ref[...])).astype(BF)
    acc_ref[...] += _dot(act, wd_ref[...])

    @pl.when(jf == nff - 1)
    def _():
        x2 = x1_ref[...] + ga2_ref[...] * acc_ref[...]
        y_ref[...] = _rmsnorm(x2, gf_ref[...]) if final else x2


def _post(og, oa, x2d, mod, mod_spec, g2, gf, wo, wfi, wfo, tm, tf, final):
    rows, d = x2d.shape
    dff = wfo.shape[0]
    nff = dff // tf
    rowblk = lambda w: pl.BlockSpec((tm, w), lambda i, j: (i, 0))
    const = lambda shape: pl.BlockSpec(shape, lambda i, j: (0,) * len(shape))
    in_specs = [rowblk(GDN_W), rowblk(ATT_W), rowblk(d),
                mod_spec(2), mod_spec(4), mod_spec(3), mod_spec(5),
                const((1, d)), const((1, d)), const((d, d)),
                pl.BlockSpec((d, tf), lambda i, j: (0, j)),
                pl.BlockSpec((d, tf), lambda i, j: (0, nff + j)),
                pl.BlockSpec((tf, d), lambda i, j: (j, 0))]
    return pl.pallas_call(
        functools.partial(_post_kernel, final, nff),
        grid=(rows // tm, nff), in_specs=in_specs, out_specs=rowblk(d),
        out_shape=jax.ShapeDtypeStruct((rows, d), F32),
        scratch_shapes=[pltpu.VMEM((tm, d), F32), pltpu.VMEM((tm, d), BF), pltpu.VMEM((tm, d), F32)],
        compiler_params=_params(("arbitrary", "arbitrary")),
        name="post",
    )(og, oa, x2d, mod, mod, mod, mod, g2.reshape(1, d), gf.reshape(1, d), wo, wfi, wfi, wfo)


def _pick(n, prefs):
    for p in prefs:
        if n % p == 0:
            return p
    return n


def kernel(x_prompt, x_sample, c_prompt, c_sample, cache_k, cache_v, cache_idx_k, page_table, state_conv, state_ssm,
           w_ada, b_ada, g_norm1, w_in, w_conv, a_log, dt_bias, g_gdn_norm, w_out, g_norm2, w_ffn_in, w_ffn_out,
           g_final):
    bp, tp, d = x_prompt.shape
    db, ts, _ = x_sample.shape
    assert ts == 1 and d == GDN_W + ATT_W and tp % max(ATT_TK, GDN_CHUNK) == 0
    depth = w_in.shape[0]
    npg, page = page_table.shape[1], cache_k.shape[2]
    past = npg * page
    tm_in = _pick(tp, (256,))
    tg = _pick(tp, (256,))
    tm_post = _pick(tp, (512, 256))
    dff = w_ffn_out.shape[1]
    tf = _pick(dff, (256, 128))

    xp = x_prompt
    xs = x_sample.reshape(db, d)
    npad = (-(bp + db)) % 8
    c_all = jnp.concatenate([c_prompt, c_sample, jnp.zeros((npad, d), F32)], axis=0)
    new_p, new_s = [], []
    for l in range(depth):
        final = l == depth - 1
        mod = _modulation(c_all, w_ada[l], b_ada[l])
        mod_p = mod[:bp].reshape(bp, 1, 6 * d)
        mod_s = mod[bp:bp + db]
        w_nat, w_t, prow, pcol = _inproj_weights(w_in[l], a_log[l], dt_bias[l])
        wo = w_out[l].astype(BF)
        wfi = w_ffn_in[l].astype(BF)
        wfo = w_ffn_out[l].astype(BF)

        (qkvn, z, vnat, knat, ikT, tail, misc, gct, qT, iqT, iwT, kbf, vT, ikbf) = _inproj_prompt(
            xp, mod_p, g_norm1[l], w_nat, w_t, w_conv[l], prow, pcol, tm_in)
        og, ssm_p = _gdn_prompt(qkvn, z, misc, gct, g_gdn_norm[l], bp, tp, tg)
        oa = _attn_prompt(qT, iqT, iwT, kbf, vT, ikbf, bp, tp)
        tiles_b = tp // tm_post
        spec_p = lambda k: pl.BlockSpec((None, 1, d), lambda i, j: (i // tiles_b, 0, k))
        xp = _post(og, oa, xp.reshape(bp * tp, d), mod_p, spec_p, g_norm2[l], g_final, wo, wfi, wfo,
                   tm_post, tf, final).reshape(bp, tp, d)
        new_p.append((knat.reshape(bp, tp, N_KV, HEAD_DIM), vnat.reshape(bp, tp, N_KV, HEAD_DIM),
                      jnp.swapaxes(ikT, 1, 2), tail[:, 8 - (CONV_W - 1):, :], ssm_p))

        (qkvn_s, z_s, vnat_s, knat_s, slab_s, raw_s, misc_s, aq_s, iq_s) = _inproj_sample(
            xs, mod_s, g_norm1[l], w_nat, w_t, w_conv[l], prow, pcol, state_conv[l], past)
        og_s, ssm_s = _gdn_sample(qkvn_s, z_s, misc_s, g_gdn_norm[l], state_ssm[l], _pick(db, (8,)))
        ik_s = slab_s[:, 0:IDX_DIM]
        iw_s = slab_s[:, IDX_DIM:IDX_DIM + IDX_HEADS]
        scores = _sample_index_scores(iq_s, iw_s, ik_s, jnp.swapaxes(cache_idx_k[l], 1, 2), page_table)
        keep = _sample_select(scores, past + ts)
        oa_s = _sample_attention(aq_s, keep, knat_s, vnat_s, cache_k[l], cache_v[l], page_table)
        spec_s = lambda k: pl.BlockSpec((db, d), lambda i, j: (i, k))
        xs = _post(og_s, oa_s, xs, mod_s, spec_s, g_norm2[l], g_final, wo, wfi, wfo, db, tf, final)
        new_conv_s = jnp.concatenate([state_conv[l][:, 1:, :], raw_s[:, None, :]], axis=1)
        new_s.append((knat_s.reshape(db, ts, N_KV, HEAD_DIM), vnat_s.reshape(db, ts, N_KV, HEAD_DIM),
                      ik_s.reshape(db, ts, IDX_DIM), new_conv_s, ssm_s))

    stack = lambda states, n: jnp.stack([s[n] for s in states], axis=0)
    return (xp, xs.reshape(db, ts, d),
            stack(new_p, 0), stack(new_p, 1), stack(new_p, 2), stack(new_p, 3), stack(new_p, 4),
            stack(new_s, 0), stack(new_s, 1), stack(new_s, 2), stack(new_s, 3), stack(new_s, 4))
```

```python
import functools
import math

import numpy as np
import jax
import jax.numpy as jnp
from jax import lax
from jax.experimental import pallas as pl
from jax.experimental.pallas import tpu as pltpu

F32 = jnp.float32
BF = jnp.bfloat16
I32 = jnp.int32

HEAD_DIM = 128
GDN_HEADS = 4
ATT_HEADS = 4
N_KV = 2
IDX_HEADS = 8
IDX_DIM = 64
CONV_W = 4
TOPK_MAX = 256
QBLOCK = 128
ROPE_THETA = 500000.0
EPS = 1e-6
GDN_W = GDN_HEADS * HEAD_DIM
ATT_W = ATT_HEADS * HEAD_DIM
KV_W = N_KV * HEAD_DIM
IDX_W = IDX_HEADS * IDX_DIM
QKV_W = 3 * GDN_W

GDN_CHUNK = 128
ATT_TK = 256
IDX_TK = 512
ATT_BLK = 1024
LANES = 128
NAT_W = QKV_W + GDN_W + KV_W + LANES
T_AQ, T_AK, T_IQ, T_IK, T_MISC, T_AV = 0, 512, 768, 1280, 1344, 1360
T_ROWS = T_AV + KV_W
VMEM_LIMIT = 56 * 1024 * 1024
INT_MIN = np.int32(-2 ** 31)
NEG_INF = float("-inf")


def _params(sem):
    return pltpu.CompilerParams(dimension_semantics=sem, vmem_limit_bytes=VMEM_LIMIT)


def _dot(a, b):
    return jnp.dot(a, b, preferred_element_type=F32)


def _dot_nt(a, b):
    return lax.dot_general(a, b, (((1,), (1,)), ((), ())), preferred_element_type=F32)


def _split3(x):
    hi = x.astype(BF)
    r1 = x - hi.astype(F32)
    mid = r1.astype(BF)
    lo = (r1 - mid.astype(F32)).astype(BF)
    return hi, mid, lo


def _mm_hi(a, b):
    ah = a.astype(BF)
    al = (a - ah.astype(F32)).astype(BF)
    bh = b.astype(BF)
    bl = (b - bh.astype(F32)).astype(BF)
    return _dot(ah, bh) + (_dot(ah, bl) + _dot(al, bh))


def _sigmoid(x):
    return 1.0 / (1.0 + jnp.exp(-x))


def _silu(x):
    return x * _sigmoid(x)


def _softplus(x):
    return jnp.maximum(x, 0.0) + jnp.log(1.0 + jnp.exp(-jnp.abs(x)))


def _rmsnorm(x, g):
    return x * lax.rsqrt(jnp.mean(x * x, axis=-1, keepdims=True) + EPS) * g


def _mod_kernel(c_ref, w_ref, b_ref, o_ref):
    s = _silu(c_ref[...]).astype(BF)
    o_ref[...] = _dot(s, w_ref[...].astype(BF)) + b_ref[...]


def _modulation(c_all, w_ada, b_ada):
    n, d = c_all.shape
    cols = w_ada.shape[1]
    tn = d
    return pl.pallas_call(
        _mod_kernel,
        grid=(cols // tn,),
        in_specs=[pl.BlockSpec((n, d), lambda j: (0, 0)),
                  pl.BlockSpec((d, tn), lambda j: (0, j)),
                  pl.BlockSpec((1, tn), lambda j: (0, j))],
        out_specs=pl.BlockSpec((n, tn), lambda j: (0, j)),
        out_shape=jax.ShapeDtypeStruct((n, cols), F32),
        compiler_params=_params(("arbitrary",)),
        name="modulation",
    )(c_all, w_ada, b_ada.reshape(1, cols))


def _rope_rows(rt_ref, base, half, cos, sin):
    x1 = rt_ref[base:base + half, :]
    x2 = rt_ref[base + half:base + 2 * half, :]
    rt_ref[base:base + half, :] = x1 * cos - x2 * sin
    rt_ref[base + half:base + 2 * half, :] = x2 * cos + x1 * sin


def _project(x_ref, sc_ref, sh_ref, g1_ref, wn_ref, wt_ref, cs128_ref, cs64_ref, pcol_ref, rt_ref):
    h = _rmsnorm(x_ref[...], g1_ref[...]) * (1.0 + sc_ref[...]) + sh_ref[...]
    hb = h.astype(BF)
    nat = _dot(hb, wn_ref[...])
    rt_ref[...] = _dot_nt(wt_ref[...], hb)
    half = HEAD_DIM // 8
    cos, sin = cs128_ref[0:half, :], cs128_ref[half:2 * half, :]
    for hd in range(ATT_HEADS):
        _rope_rows(rt_ref, T_AQ + hd * HEAD_DIM, half, cos, sin)
    for hd in range(N_KV):
        _rope_rows(rt_ref, T_AK + hd * HEAD_DIM, half, cos, sin)
    half = IDX_DIM // 8
    cos, sin = cs64_ref[0:half, :], cs64_ref[half:2 * half, :]
    for hd in range(IDX_HEADS):
        _rope_rows(rt_ref, T_IQ + hd * IDX_DIM, half, cos, sin)
    _rope_rows(rt_ref, T_IK, half, cos, sin)
    mt = rt_ref[T_MISC:T_MISC + 16, :]
    r = lax.broadcasted_iota(I32, mt.shape, 0)
    a_log, dt_b = pcol_ref[:, 0:1], pcol_ref[:, 1:2]
    gate = -jnp.exp(a_log) * _softplus(mt + dt_b)
    mt = jnp.where(r < IDX_HEADS, mt * IDX_HEADS ** -0.5, jnp.where(r < IDX_HEADS + GDN_HEADS, _sigmoid(mt), gate))
    rt_ref[T_MISC:T_MISC + 16, :] = mt
    return nat, mt


def _misc_natural(m, prow_ref):
    lane = lax.broadcasted_iota(I32, m.shape, 1)
    gate = -jnp.exp(prow_ref[0:1, :]) * _softplus(m + prow_ref[1:2, :])
    return jnp.where(lane < GDN_HEADS, _sigmoid(m), jnp.where(lane < 3 * GDN_HEADS, gate, 0.0))


def _qkv_post(conv, qkvn_ref):
    c = _silu(conv)
    for hb in range(2 * GDN_HEADS):
        xh = c[:, hb * HEAD_DIM:(hb + 1) * HEAD_DIM]
        n = xh * lax.rsqrt(jnp.sum(xh * xh, axis=-1, keepdims=True) + EPS)
        if hb < GDN_HEADS:
            n = n * HEAD_DIM ** -0.5
        qkvn_ref[:, hb * HEAD_DIM:(hb + 1) * HEAD_DIM] = n
    qkvn_ref[:, 2 * GDN_W:] = c[:, 2 * GDN_W:]


def _inproj_prompt_kernel(tm, x_ref, sc_ref, sh_ref, g1_ref, wn_ref, wt_ref, wconv_ref, cs128_ref, cs64_ref,
                          prow_ref, pcol_ref, tril_ref,
                          qkvn_ref, z_ref, vnat_ref, knat_ref, ikT_ref, tail_ref, misc_ref, gct_ref,
                          qT_ref, iqT_ref, iwT_ref, kbf_ref, vT_ref, ikbf_ref,
                          xs_ref, rt_ref):
    i = pl.program_id(1)
    nat, mt = _project(x_ref, sc_ref, sh_ref, g1_ref, wn_ref, wt_ref, cs128_ref, cs64_ref, pcol_ref, rt_ref)

    @pl.when(i == 0)
    def _():
        xs_ref[0:8, :] = jnp.zeros((8, QKV_W), F32)

    @pl.when(i > 0)
    def _():
        xs_ref[0:8, :] = xs_ref[tm:tm + 8, :]

    xs_ref[8:tm + 8, :] = nat[:, 0:QKV_W]
    conv = wconv_ref[0:1, :] * xs_ref[5:tm + 5, :]
    for t in range(1, CONV_W):
        conv = conv + wconv_ref[t:t + 1, :] * xs_ref[5 + t:tm + 5 + t, :]
    tail_ref[...] = xs_ref[tm:tm + 8, :]
    _qkv_post(conv, qkvn_ref)
    z_ref[...] = nat[:, QKV_W:QKV_W + GDN_W]
    v0 = QKV_W + GDN_W
    for hd in range(N_KV):
        vnat_ref[pl.ds(hd, tm, stride=N_KV), :] = nat[:, v0 + hd * HEAD_DIM:v0 + (hd + 1) * HEAD_DIM]

    gm = _misc_natural(nat[:, NAT_W - LANES:], prow_ref)
    tril = tril_ref[...]
    hi, mid, lo = _split3(gm)
    gc = _dot(tril, hi) + (_dot(tril, mid) + _dot(tril, lo))
    lane = lax.broadcasted_iota(I32, gm.shape, 1)
    misc_ref[...] = jnp.where(lane < 2 * GDN_HEADS, gm, gc)
    hi, mid, lo = _split3(mt)
    gct = _dot_nt(hi, tril) + (_dot_nt(mid, tril) + _dot_nt(lo, tril))
    r = lax.broadcasted_iota(I32, mt.shape, 0)
    bg = jnp.where(r < IDX_HEADS + GDN_HEADS, mt, gct)[8:16, :]
    for cc in range(tm // GDN_CHUNK):
        gct_ref[cc] = bg[:, cc * GDN_CHUNK:(cc + 1) * GDN_CHUNK]

    iwT_ref[...] = mt[0:IDX_HEADS, :]
    qT_ref[...] = rt_ref[T_AQ:T_AQ + ATT_W, :].astype(BF)
    iqT_ref[...] = rt_ref[T_IQ:T_IQ + IDX_W, :].astype(BF)
    kn = rt_ref[T_AK:T_AK + KV_W, :].T
    for hd in range(N_KV):
        knat_ref[pl.ds(hd, tm, stride=N_KV), :] = kn[:, hd * HEAD_DIM:(hd + 1) * HEAD_DIM]
    kbf_ref[...] = kn.astype(BF)
    ikT_ref[...] = rt_ref[T_IK:T_IK + IDX_DIM, :]
    ikbf_ref[...] = rt_ref[T_IK:T_IK + LANES, :].T[:, 0:IDX_DIM].astype(BF)
    for cc in range(tm // ATT_TK):
        vT_ref[cc] = rt_ref[T_AV:T_AV + KV_W, cc * ATT_TK:(cc + 1) * ATT_TK].astype(BF)


def _inproj_sample_kernel(x_ref, sc_ref, sh_ref, g1_ref, wn_ref, wt_ref, wconv_ref, cs128_ref, cs64_ref,
                          prow_ref, pcol_ref, s0_ref, s1_ref, s2_ref,
                          qkvn_ref, z_ref, vnat_ref, knat_ref, slab_ref, raw_ref, misc_ref, aq_ref, iq_ref,
                          rt_ref):
    nat, _ = _project(x_ref, sc_ref, sh_ref, g1_ref, wn_ref, wt_ref, cs128_ref, cs64_ref, pcol_ref, rt_ref)
    raw = nat[:, 0:QKV_W]
    raw_ref[...] = raw
    conv = wconv_ref[0:1, :] * s0_ref[...]
    conv = conv + wconv_ref[1:2, :] * s1_ref[...]
    conv = conv + wconv_ref[2:3, :] * s2_ref[...]
    conv = conv + wconv_ref[3:4, :] * raw
    _qkv_post(conv, qkvn_ref)
    z_ref[...] = nat[:, QKV_W:QKV_W + GDN_W]
    vnat_ref[...] = nat[:, QKV_W + GDN_W:QKV_W + GDN_W + KV_W]
    misc_ref[...] = _misc_natural(nat[:, NAT_W - LANES:], prow_ref)
    aq_ref[...] = rt_ref[T_AQ:T_AQ + ATT_W, :].T
    iq_ref[...] = rt_ref[T_IQ:T_IQ + IDX_W, :].T
    knat_ref[...] = rt_ref[T_AK:T_AK + KV_W, :].T
    slab_ref[...] = rt_ref[T_IK:T_IK + LANES, :].T


def _inproj_weights(w_in, a_log, dt_bias):
    offs = np.cumsum([0, QKV_W, GDN_W, GDN_HEADS, GDN_HEADS, ATT_W, KV_W, KV_W, IDX_W, IDX_DIM, IDX_HEADS])
    qkv, z, beta, a, aq, ak, av, iq, ik, iw = [w_in[:, offs[n]:offs[n + 1]] for n in range(10)]
    d = w_in.shape[0]
    pad = jnp.zeros((d, LANES - 3 * GDN_HEADS), w_in.dtype)
    w_nat = jnp.concatenate([qkv, z, av, beta, a, a, pad], axis=1).astype(BF)
    w_t = jnp.concatenate([aq, ak, iq, ik, iw, beta, a, av], axis=1).T.astype(BF)
    zrow = jnp.zeros((LANES - 3 * GDN_HEADS,), F32)
    prow = jnp.stack([jnp.concatenate([jnp.zeros((GDN_HEADS,), F32), a_log, a_log, zrow]),
                      jnp.concatenate([jnp.zeros((GDN_HEADS,), F32), dt_bias, dt_bias, zrow])])
    z12 = jnp.zeros((IDX_HEADS + GDN_HEADS,), F32)
    pcol = jnp.stack([jnp.concatenate([z12, a_log]), jnp.concatenate([z12, dt_bias])], axis=1)
    return w_nat, w_t, prow, pcol


def _rope_tables(pos, dim):
    half = dim // 8
    inv = ROPE_THETA ** (-jnp.arange(half, dtype=F32) / half)
    ang = pos.astype(F32)[:, None] * inv[None, :]
    return jnp.concatenate([jnp.cos(ang).T, jnp.sin(ang).T], axis=0)


def _inproj_prompt(x, mod, g1, w_nat, w_t, w_conv, prow, pcol, tm):
    bp, tp, d = x.shape
    nt = tp // tm
    rows = bp * tp
    pos = jnp.arange(tp, dtype=I32)
    cs128, cs64 = _rope_tables(pos, HEAD_DIM), _rope_tables(pos, IDX_DIM)
    ri = np.arange(tm)
    tril = jnp.asarray((ri[:, None] // GDN_CHUNK == ri[None, :] // GDN_CHUNK) & (ri[None, :] <= ri[:, None]), BF)
    const = lambda shape: pl.BlockSpec(shape, lambda b, i: (0,) * len(shape))
    rowblk = lambda w: pl.BlockSpec((tm, w), lambda b, i: (b * nt + i, 0))
    colblk = lambda h: pl.BlockSpec((h, tm), lambda b, i: (0, b * nt + i))
    in_specs = [
        pl.BlockSpec((None, tm, d), lambda b, i: (b, i, 0)),
        pl.BlockSpec((None, 1, d), lambda b, i: (b, 0, 1)),
        pl.BlockSpec((None, 1, d), lambda b, i: (b, 0, 0)),
        const((1, d)), const((d, NAT_W)), const((T_ROWS, d)), const((CONV_W, QKV_W)),
        pl.BlockSpec((HEAD_DIM // 4, tm), lambda b, i: (0, i)),
        pl.BlockSpec((IDX_DIM // 4, tm), lambda b, i: (0, i)),
        const((2, LANES)), const((16, 2)), const((tm, tm)),
    ]
    out_shape = [
        jax.ShapeDtypeStruct((rows, QKV_W), F32), jax.ShapeDtypeStruct((rows, GDN_W), F32),
        jax.ShapeDtypeStruct((rows * N_KV, HEAD_DIM), F32), jax.ShapeDtypeStruct((rows * N_KV, HEAD_DIM), F32),
        jax.ShapeDtypeStruct((bp, IDX_DIM, tp), F32), jax.ShapeDtypeStruct((bp, 8, QKV_W), F32),
        jax.ShapeDtypeStruct((rows, LANES), F32), jax.ShapeDtypeStruct((rows // GDN_CHUNK, 8, GDN_CHUNK), F32),
        jax.ShapeDtypeStruct((ATT_W, rows), BF), jax.ShapeDtypeStruct((IDX_W, rows), BF),
        jax.ShapeDtypeStruct((IDX_HEADS, rows), F32), jax.ShapeDtypeStruct((rows, KV_W), BF),
        jax.ShapeDtypeStruct((rows // ATT_TK, KV_W, ATT_TK), BF), jax.ShapeDtypeStruct((rows, IDX_DIM), BF),
    ]
    out_specs = [
        rowblk(QKV_W), rowblk(GDN_W),
        pl.BlockSpec((tm * N_KV, HEAD_DIM), lambda b, i: (b * nt + i, 0)),
        pl.BlockSpec((tm * N_KV, HEAD_DIM), lambda b, i: (b * nt + i, 0)),
        pl.BlockSpec((None, IDX_DIM, tm), lambda b, i: (b, 0, i)),
        pl.BlockSpec((None, 8, QKV_W), lambda b, i: (b, 0, 0)),
        rowblk(LANES),
        pl.BlockSpec((tm // GDN_CHUNK, 8, GDN_CHUNK), lambda b, i: (b * nt + i, 0, 0)),
        colblk(ATT_W), colblk(IDX_W), colblk(IDX_HEADS), rowblk(KV_W),
        pl.BlockSpec((tm // ATT_TK, KV_W, ATT_TK), lambda b, i: (b * nt + i, 0, 0)),
        rowblk(IDX_DIM),
    ]
    return pl.pallas_call(
        functools.partial(_inproj_prompt_kernel, tm),
        grid=(bp, nt), in_specs=in_specs, out_specs=out_specs, out_shape=out_shape,
        scratch_shapes=[pltpu.VMEM((tm + 8, QKV_W), F32), pltpu.VMEM((T_ROWS, tm), F32)],
        compiler_params=_params(("arbitrary", "arbitrary")),
        name="inproj_prompt",
    )(x, mod, mod, g1.reshape(1, d), w_nat, w_t, w_conv, cs128, cs64, prow, pcol, tril)


def _inproj_sample(x, mod, g1, w_nat, w_t, w_conv, prow, pcol, conv_state, past):
    db, d = x.shape
    pos = jnp.full((db,), past, I32)
    cs128, cs64 = _rope_tables(pos, HEAD_DIM), _rope_tables(pos, IDX_DIM)
    full = lambda shape: pl.BlockSpec(shape, lambda i: (0,) * len(shape))
    in_specs = [
        full((db, d)),
        pl.BlockSpec((db, d), lambda i: (0, 1)), pl.BlockSpec((db, d), lambda i: (0, 0)),
        full((1, d)), full((d, NAT_W)), full((T_ROWS, d)), full((CONV_W, QKV_W)),
        full((HEAD_DIM // 4, db)), full((IDX_DIM // 4, db)), full((2, LANES)), full((16, 2)),
        full((db, QKV_W)), full((db, QKV_W)), full((db, QKV_W)),
    ]
    widths = [QKV_W, GDN_W, KV_W, KV_W, LANES, QKV_W, LANES, ATT_W, IDX_W]
    return pl.pallas_call(
        _inproj_sample_kernel,
        grid=(1,), in_specs=in_specs,
        out_specs=[full((db, w)) for w in widths],
        out_shape=[jax.ShapeDtypeStruct((db, w), F32) for w in widths],
        scratch_shapes=[pltpu.VMEM((T_ROWS, db), F32)],
        compiler_params=_params(("arbitrary",)),
        name="inproj_sample",
    )(x, mod, mod, g1.reshape(1, d), w_nat, w_t, w_conv, cs128, cs64, prow, pcol,
      conv_state[:, 0], conv_state[:, 1], conv_state[:, 2])


def _unit_lower_inverses(lows):
    c = lows[0].shape[0]
    ii = lax.broadcasted_iota(I32, (c, c), 0)
    jj = lax.broadcasted_iota(I32, (c, c), 1)
    eye = jnp.where(ii == jj, 1.0, 0.0)
    levels = int(math.log2(c)) - 1
    ts = [eye - low for low in lows]
    ps = [_mm_hi(low, low) for low in lows]
    for lvl in range(levels):
        if lvl == levels - 1:
            ts = [t + _mm_hi(t, p) for t, p in zip(ts, ps)]
        else:
            both = [_mm_hi(jnp.concatenate([t, p], axis=0), p) for t, p in zip(ts, ps)]
            ts = [t + b[0:c] for t, b in zip(ts, both)]
            ps = [b[c:] for b in both]
    return ts


def _gdn_prompt_kernel(tg, qkvn_ref, z_ref, misc_ref, gct_ref, gn_ref, o_ref, ssm_ref, s_ref):
    i = pl.program_id(1)

    @pl.when(i == 0)
    def _():
        s_ref[...] = jnp.zeros(s_ref.shape, F32)

    c = GDN_CHUNK
    ii = lax.broadcasted_iota(I32, (c, c), 0)
    jj = lax.broadcasted_iota(I32, (c, c), 1)
    pairs = [(cc, hd) for cc in range(tg // c) for hd in range(GDN_HEADS)]

    qs, ks, gcs, rhs, lows, intras = {}, {}, {}, {}, [], {}
    for cc, hd in pairs:
        r0, lo = cc * c, hd * HEAD_DIM
        q = qkvn_ref[r0:r0 + c, lo:lo + HEAD_DIM]
        k = qkvn_ref[r0:r0 + c, GDN_W + lo:GDN_W + lo + HEAD_DIM]
        v = qkvn_ref[r0:r0 + c, 2 * GDN_W + lo:2 * GDN_W + lo + HEAD_DIM]
        beta_c = misc_ref[r0:r0 + c, hd:hd + 1]
        gc_c = misc_ref[r0:r0 + c, 2 * GDN_HEADS + hd:2 * GDN_HEADS + hd + 1]
        gc_r = gct_ref[cc, GDN_HEADS + hd:GDN_HEADS + hd + 1, :]
        decay = jnp.where(ii >= jj, jnp.exp(jnp.where(ii >= jj, gc_c - gc_r, 0.0)), 0.0)
        kb = k * beta_c
        both = _dot_nt(jnp.concatenate([kb, q], axis=0).astype(BF), k.astype(BF))
        lows.append(jnp.where(ii > jj, both[0:c] * decay, 0.0))
        intras[cc, hd] = (both[c:] * decay).astype(BF)
        rhs[cc, hd] = jnp.concatenate([v * beta_c, kb * jnp.exp(gc_c)], axis=1).astype(BF)
        qs[cc, hd], ks[cc, hd], gcs[cc, hd] = q, k, gc_c
    ts = _unit_lower_inverses(lows)
    uws = {p: _dot(t.astype(BF), rhs[p]) for p, t in zip(pairs, ts)}

    for cc, hd in pairs:
        r0, lo = cc * c, hd * HEAD_DIM
        q, k, gc_c, uw = qs[cc, hd], ks[cc, hd], gcs[cc, hd], uws[cc, hd]
        s = s_ref[hd]
        ws = _dot(jnp.concatenate([uw[:, HEAD_DIM:], q * jnp.exp(gc_c)], axis=0).astype(BF), s.astype(BF))
        v_new = (uw[:, 0:HEAD_DIM] - ws[0:c]).astype(BF)
        o = ws[c:] + _dot(intras[cc, hd], v_new)
        g_last = gc_c[c - 1:c, :]
        kd = k * jnp.exp(g_last - gc_c)
        s_ref[hd] = s * jnp.exp(g_last) + _dot(kd.T.astype(BF), v_new)
        o = _rmsnorm(o, gn_ref[...]) * _silu(z_ref[r0:r0 + c, lo:lo + HEAD_DIM])
        o_ref[r0:r0 + c, lo:lo + HEAD_DIM] = o.astype(BF)
    ssm_ref[...] = s_ref[...]


def _gdn_prompt(qkvn, z, misc, gct, g_norm, bp, tp, tg):
    nt = tp // tg
    rows = bp * tp
    rowblk = lambda w: pl.BlockSpec((tg, w), lambda b, i: (b * nt + i, 0))
    return pl.pallas_call(
        functools.partial(_gdn_prompt_kernel, tg),
        grid=(bp, nt),
        in_specs=[rowblk(QKV_W), rowblk(GDN_W), rowblk(LANES),
                  pl.BlockSpec((tg // GDN_CHUNK, 8, GDN_CHUNK), lambda b, i: (b * nt + i, 0, 0)),
                  pl.BlockSpec((1, HEAD_DIM), lambda b, i: (0, 0))],
        out_specs=[rowblk(GDN_W),
                   pl.BlockSpec((None, GDN_HEADS, HEAD_DIM, HEAD_DIM), lambda b, i: (b, 0, 0, 0))],
        out_shape=[jax.ShapeDtypeStruct((rows, GDN_W), BF),
                   jax.ShapeDtypeStruct((bp, GDN_HEADS, HEAD_DIM, HEAD_DIM), F32)],
        scratch_shapes=[pltpu.VMEM((GDN_HEADS, HEAD_DIM, HEAD_DIM), F32)],
        compiler_params=_params(("arbitrary", "arbitrary")),
        name="gdn_prompt",
    )(qkvn, z, misc, gct, g_norm.reshape(1, HEAD_DIM))


def _gdn_sample_kernel(nb, qkvn_ref, z_ref, misc_ref, gn_ref, s_ref, o_ref, ssm_ref):
    for bi in range(nb):
        for hd in range(GDN_HEADS):
            lo = hd * HEAD_DIM
            q = qkvn_ref[bi:bi + 1, lo:lo + HEAD_DIM]
            k = qkvn_ref[bi:bi + 1, GDN_W + lo:GDN_W + lo + HEAD_DIM]
            v = qkvn_ref[bi:bi + 1, 2 * GDN_W + lo:2 * GDN_W + lo + HEAD_DIM]
            beta = misc_ref[bi:bi + 1, hd:hd + 1]
            g = misc_ref[bi:bi + 1, GDN_HEADS + hd:GDN_HEADS + hd + 1]
            kcol = jnp.broadcast_to(k, (HEAD_DIM, HEAD_DIM)).T
            qcol = jnp.broadcast_to(q, (HEAD_DIM, HEAD_DIM)).T
            s = s_ref[bi, hd] * jnp.exp(g)
            ks = jnp.sum(kcol * s, axis=0, keepdims=True)
            delta = (v - ks) * beta
            s = s + kcol * delta
            ssm_ref[bi, hd] = s
            o = jnp.sum(qcol * s, axis=0, keepdims=True)
            o = _rmsnorm(o, gn_ref[...]) * _silu(z_ref[bi:bi + 1, lo:lo + HEAD_DIM])
            o_ref[bi:bi + 1, lo:lo + HEAD_DIM] = o.astype(BF)


def _gdn_sample(qkvn, z, misc, g_norm, state, nb):
    db = qkvn.shape[0]
    rowblk = lambda w: pl.BlockSpec((nb, w), lambda i: (i, 0))
    sblk = pl.BlockSpec((nb, GDN_HEADS, HEAD_DIM, HEAD_DIM), lambda i: (i, 0, 0, 0))
    return pl.pallas_call(
        functools.partial(_gdn_sample_kernel, nb),
        grid=(db // nb,),
        in_specs=[rowblk(QKV_W), rowblk(GDN_W), rowblk(LANES), pl.BlockSpec((1, HEAD_DIM), lambda i: (0, 0)), sblk],
        out_specs=[rowblk(GDN_W), sblk],
        out_shape=[jax.ShapeDtypeStruct((db, GDN_W), BF), jax.ShapeDtypeStruct(state.shape, F32)],
        compiler_params=_params(("arbitrary",)),
        name="gdn_sample",
    )(qkvn, z, misc, g_norm.reshape(1, HEAD_DIM), state)


def _ordered_word_to_float(u):
    s = u ^ INT_MIN
    return lax.bitcast_convert_type(s ^ ((s >> 31) & np.int32(0x7FFFFFFF)), F32)


def _select_topk_bias(i_ref, nch, tk, k_sel, idx_bits, taken=0.0, dropped=NEG_INF):
    nl = i_ref.shape[1]
    row = lax.broadcasted_iota(I32, (tk, 1), 0)

    def chunk(c):
        return pl.ds(pl.multiple_of(c * tk, tk), tk)

    def count(pred):
        def body(c, acc):
            m = pred(i_ref[chunk(c), :], c * tk + row)
            return acc + jnp.sum(jnp.where(m, 1, 0).astype(I32).reshape(tk // 8, 8, nl), axis=0)
        acc = lax.fori_loop(0, nch, body, jnp.zeros((8, nl), I32))
        return jnp.sum(acc, axis=0, keepdims=True)

    def bit_body(i, carry):
        t_u, n_ge = carry
        cand_u = t_u | jnp.left_shift(jnp.int32(1), 31 - i)
        cand = _ordered_word_to_float(cand_u)
        cnt = count(lambda sc, idx: sc >= cand)
        take = cnt >= k_sel
        return jnp.where(take, cand_u, t_u), jnp.where(take, cnt, n_ge)

    t_u, n_ge = lax.fori_loop(0, 32, bit_body, (jnp.zeros((1, nl), I32), jnp.full((1, nl), -1, I32)))
    thr = jnp.where((t_u >> 23) == 0, NEG_INF, _ordered_word_to_float(t_u))

    def tie_break(_):
        need = k_sel - count(lambda sc, idx: sc > thr)

        def idx_body(i, x):
            cand = x | jnp.left_shift(jnp.int32(1), idx_bits - 1 - i)
            cnt = count(lambda sc, idx: (sc == thr) & (idx < cand))
            return jnp.where(cnt < need, cand, x)

        return lax.fori_loop(0, idx_bits, idx_body, jnp.zeros((1, nl), I32))

    x = lax.cond(jnp.max(n_ge) > k_sel, tie_break, lambda _: jnp.full((1, nl), 2 ** idx_bits - 1, I32), 0)

    def write(c, carry):
        sc = i_ref[chunk(c), :]
        sel = (sc > thr) | ((sc == thr) & (c * tk + row <= x))
        i_ref[chunk(c), :] = jnp.where(sel & (sc > NEG_INF), taken, dropped)
        return carry

    lax.fori_loop(0, nch, write, 0)


def _attn_prompt_kernel(tq, tk, blk, k_sel, idx_bits, qT_ref, iqT_ref, iwT_ref, kbf_ref, vT_ref, ikbf_ref, o_ref,
                        i_ref):
    t0 = pl.program_id(1) * tq
    nch = (t0 + tq + tk - 1) // tk
    nblk = (t0 + tq + blk - 1) // blk
    qpos = t0 + lax.broadcasted_iota(I32, (1, tq), 1)
    row = lax.broadcasted_iota(I32, (tk, 1), 0)
    w = iwT_ref[...]
    pairs = IDX_HEADS // 2
    rhs = [jnp.concatenate([iqT_ref[(2 * p) * IDX_DIM:(2 * p + 1) * IDX_DIM, :],
                            iqT_ref[(2 * p + 1) * IDX_DIM:(2 * p + 2) * IDX_DIM, :]], axis=1) for p in range(pairs)]

    def chunk(c):
        return pl.ds(pl.multiple_of(c * tk, tk), tk)

    def idx_body(c, carry):
        ikc = ikbf_ref[chunk(c), :]
        acc = jnp.zeros((tk, tq), F32)
        for p in range(pairs):
            d = jnp.maximum(_dot(ikc, rhs[p]), 0.0)
            acc = acc + d[:, 0:tq] * w[2 * p:2 * p + 1, :] + d[:, tq:] * w[2 * p + 1:2 * p + 2, :]
        i_ref[chunk(c), :] = jnp.where(c * tk + row <= qpos, acc, NEG_INF)
        return carry

    lax.fori_loop(0, nch, idx_body, 0)

    def fill_body(c, carry):
        i_ref[chunk(c), :] = jnp.full((tk, tq), NEG_INF, F32)
        return carry

    lax.fori_loop(nch, nblk * (blk // tk), fill_body, 0)
    _select_topk_bias(i_ref, nch, tk, k_sel, idx_bits)

    scale = HEAD_DIM ** -0.5
    rep = ATT_HEADS // N_KV
    sub = ATT_TK
    nsub = blk // sub
    qg = [jnp.concatenate([qT_ref[(rep * g + r) * HEAD_DIM:(rep * g + r + 1) * HEAD_DIM, :] for r in range(rep)],
                          axis=1) for g in range(N_KV)]

    def body(c, carry):
        out = []
        for g in range(N_KV):
            m, l, acc = carry[g]
            ss = []
            for i in range(nsub):
                rows = pl.ds(pl.multiple_of(c * blk + i * sub, sub), sub)
                bias = i_ref[rows, :]
                ss.append(_dot(kbf_ref[rows, g * HEAD_DIM:(g + 1) * HEAD_DIM], qg[g]) * scale
                          + jnp.concatenate([bias] * rep, axis=1))
            m_new = m
            for s in ss:
                m_new = jnp.maximum(m_new, jnp.max(s, axis=0, keepdims=True))
            m_safe = jnp.where(m_new == NEG_INF, 0.0, m_new)
            alpha = jnp.exp(m - m_safe)
            l = alpha * l
            acc = acc * alpha
            for i, s in enumerate(ss):
                p = jnp.exp(s - m_safe)
                l = l + jnp.sum(p, axis=0, keepdims=True)
                acc = acc + _dot(vT_ref[c * nsub + i, g * HEAD_DIM:(g + 1) * HEAD_DIM, :], p.astype(BF))
            out.append((m_new, l, acc))
        return tuple(out)

    init = tuple((jnp.full((1, rep * tq), NEG_INF, F32), jnp.zeros((1, rep * tq), F32),
                  jnp.zeros((HEAD_DIM, rep * tq), F32)) for _ in range(N_KV))
    res = lax.fori_loop(0, nblk, body, init)
    for g in range(N_KV):
        _, l, acc = res[g]
        o_t = acc / l
        for r in range(rep):
            hd = rep * g + r
            o_ref[:, hd * HEAD_DIM:(hd + 1) * HEAD_DIM] = o_t[:, r * tq:(r + 1) * tq].T.astype(BF)


def _attn_prompt(qT, iqT, iwT, kbf, vT, ikbf, bp, tp):
    tq = QBLOCK
    tk = _pick(tp, (IDX_TK, ATT_TK))
    blk = _pick(tp, (ATT_BLK, IDX_TK, ATT_TK))
    assert blk % tk == 0 and blk % ATT_TK == 0
    nq = tp // tq
    rows = bp * tp
    k_sel = min(TOPK_MAX, tp // 4)
    idx_bits = max(1, int(tp - 1).bit_length())
    colblk = lambda h: pl.BlockSpec((h, tq), lambda b, j: (0, b * nq + j))
    return pl.pallas_call(
        functools.partial(_attn_prompt_kernel, tq, tk, blk, k_sel, idx_bits),
        grid=(bp, nq),
        in_specs=[colblk(ATT_W), colblk(IDX_W), colblk(IDX_HEADS),
                  pl.BlockSpec((tp, KV_W), lambda b, j: (b, 0)),
                  pl.BlockSpec((tp // ATT_TK, KV_W, ATT_TK), lambda b, j: (b, 0, 0)),
                  pl.BlockSpec((tp, IDX_DIM), lambda b, j: (b, 0))],
        out_specs=pl.BlockSpec((tq, ATT_W), lambda b, j: (b * nq + j, 0)),
        out_shape=jax.ShapeDtypeStruct((rows, ATT_W), BF),
        scratch_shapes=[pltpu.VMEM((tp, tq), F32)],
        compiler_params=_params(("arbitrary", "arbitrary")),
        name="attn_prompt",
    )(qT, iqT, iwT, kbf, vT, ikbf)


def _sidx_kernel(npg, pt_ref, iq_ref, iw_ref, ikn_ref, *refs):
    pages, out_ref = refs[:npg], refs[npg]
    iq = iq_ref[...].astype(BF)
    w = iw_ref[...]
    for p in range(npg):
        d = jnp.maximum(_dot(iq, pages[p][...].astype(BF)), 0.0)
        out_ref[:, p * LANES:(p + 1) * LANES] = jnp.sum(d * w, axis=0, keepdims=True)
    dn = jnp.sum(iq.astype(F32) * ikn_ref[...].astype(BF).astype(F32), axis=1, keepdims=True)
    sn = jnp.sum(jnp.maximum(dn, 0.0) * w, axis=0, keepdims=True)
    lane = lax.broadcasted_iota(I32, (1, LANES), 1)
    out_ref[:, npg * LANES:(npg + 1) * LANES] = jnp.where(lane == 0, sn, NEG_INF)


def _sample_index_scores(iq, iw, ik_new, cache_ik_t, page_table):
    db, npg = page_table.shape
    page = cache_ik_t.shape[2]
    assert page == LANES
    width = (npg + 1) * LANES
    page_specs = [pl.BlockSpec((None, IDX_DIM, page), functools.partial(lambda p, b, pt: (pt[b, p], 0, 0), p))
                  for p in range(npg)]
    grid_spec = pltpu.PrefetchScalarGridSpec(
        num_scalar_prefetch=1, grid=(db,),
        in_specs=[pl.BlockSpec((None, IDX_HEADS, IDX_DIM), lambda b, pt: (b, 0, 0)),
                  pl.BlockSpec((None, IDX_HEADS, 1), lambda b, pt: (b, 0, 0)),
                  pl.BlockSpec((None, 1, IDX_DIM), lambda b, pt: (b, 0, 0))] + page_specs,
        out_specs=pl.BlockSpec((None, 1, width), lambda b, pt: (b, 0, 0)))
    out = pl.pallas_call(
        functools.partial(_sidx_kernel, npg), grid_spec=grid_spec,
        out_shape=jax.ShapeDtypeStruct((db, 1, width), F32),
        compiler_params=_params(("arbitrary",)),
        name="sample_index_scores",
    )(page_table, iq.reshape(db, IDX_HEADS, IDX_DIM), iw.reshape(db, IDX_HEADS, 1), ik_new.reshape(db, 1, IDX_DIM),
      *([cache_ik_t] * npg))
    return out.reshape(db, width)


def _ssel_kernel(nch, k_sel, idx_bits, s_ref, spread_ref, keep_ref, i_ref):
    for p in range(nch):
        i_ref[p * LANES:(p + 1) * LANES, :] = s_ref[:, p * LANES:(p + 1) * LANES].T
    _select_topk_bias(i_ref, nch, LANES, k_sel, idx_bits, taken=1.0, dropped=0.0)
    rows = N_KV * LANES
    for p in range(nch - 1):
        k2 = _dot(spread_ref[...], i_ref[p * LANES:(p + 1) * LANES, :].astype(BF))
        for h in range(N_KV):
            keep_ref[:, p * rows + h * LANES:p * rows + (h + 1) * LANES] = k2[h * LANES:(h + 1) * LANES, :].T
    keep_ref[:, (nch - 1) * rows:(nch - 1) * rows + LANES] = i_ref[(nch - 1) * LANES:nch * LANES, :].T


def _sample_select(scores, n_keys):
    db, width = scores.shape
    assert db == LANES
    nch = width // LANES
    k_sel = min(TOPK_MAX, n_keys // 4)
    idx_bits = max(1, int(width - 1).bit_length())
    rows = N_KV * LANES
    out_w = (nch - 1) * rows + LANES
    spread = jnp.asarray(np.arange(rows)[:, None] // N_KV == np.arange(LANES)[None, :], BF)
    return pl.pallas_call(
        functools.partial(_ssel_kernel, nch, k_sel, idx_bits),
        grid=(1,),
        in_specs=[pl.BlockSpec((db, width), lambda i: (0, 0)), pl.BlockSpec((rows, LANES), lambda i: (0, 0))],
        out_specs=pl.BlockSpec((db, out_w), lambda i: (0, 0)),
        out_shape=jax.ShapeDtypeStruct((db, out_w), F32),
        scratch_shapes=[pltpu.VMEM((width, db), F32)],
        compiler_params=_params(("arbitrary",)),
        name="sample_select",
    )(scores, spread)


def _sattn_kernel(npg, pt_ref, q_ref, keep_ref, kn_ref, vn_ref, *refs):
    kpages, vpages, o_ref = refs[:npg], refs[npg:2 * npg], refs[2 * npg]
    rep = ATT_HEADS // N_KV
    rows = N_KV * LANES
    scale = HEAD_DIM ** -0.5
    q = q_ref[...]
    q8 = jnp.concatenate([q, jnp.zeros((8 - ATT_HEADS, HEAD_DIM), F32)], axis=0).astype(BF)
    hrow = lax.broadcasted_iota(I32, (8, 1), 0)
    lane = lax.broadcasted_iota(I32, (1, LANES), 1)
    own_kv = lax.broadcasted_iota(I32, (1, rows), 1) % N_KV == hrow // rep
    parts = []
    for p in range(npg):
        s2 = _dot_nt(q8, kpages[p][...].astype(BF))
        keep2 = keep_ref[:, p * rows:(p + 1) * rows]
        parts.append(jnp.where(own_kv & (keep2 > 0.5), s2 * scale, NEG_INF))
    kn = kn_ref[...].astype(BF).astype(F32)
    kn8 = jnp.where(hrow // rep == 0, kn[0:1, :], kn[1:2, :])
    s_new = jnp.sum(q8.astype(F32) * kn8, axis=1, keepdims=True)
    keep_new = keep_ref[:, npg * rows:npg * rows + 1]
    parts.append(jnp.where((lane == 0) & (keep_new > 0.5), s_new * scale, NEG_INF))
    s = jnp.concatenate(parts, axis=1)
    m = jnp.max(s, axis=1, keepdims=True)
    e = jnp.exp(s - m)
    pr = (e / jnp.sum(e, axis=1, keepdims=True)).astype(BF)
    o8 = jnp.zeros((8, HEAD_DIM), F32)
    for p in range(npg):
        o8 = o8 + _dot(pr[:, p * rows:(p + 1) * rows], vpages[p][...].astype(BF))
    vn = vn_ref[...].astype(BF).astype(F32)
    vn8 = jnp.where(hrow // rep == 0, vn[0:1, :], vn[1:2, :])
    o8 = o8 + pr[:, npg * rows:npg * rows + 1].astype(F32) * vn8
    for hd in range(ATT_HEADS):
        o_ref[:, hd * HEAD_DIM:(hd + 1) * HEAD_DIM] = o8[hd:hd + 1, :].astype(BF)


def _sample_attention(q, keep, k_new, v_new, cache_k, cache_v, page_table):
    db, npg = page_table.shape
    n_phys, page = cache_k.shape[0], cache_k.shape[1]
    assert page == LANES and N_KV == 2
    rows = page * N_KV
    width = npg * rows + LANES
    ck = cache_k.reshape(n_phys, rows, HEAD_DIM)
    cv = cache_v.reshape(n_phys, rows, HEAD_DIM)
    page_specs = [pl.BlockSpec((None, rows, HEAD_DIM), functools.partial(lambda p, b, pt: (pt[b, p], 0, 0), p))
                  for p in range(npg)]
    grid_spec = pltpu.PrefetchScalarGridSpec(
        num_scalar_prefetch=1, grid=(db,),
        in_specs=[pl.BlockSpec((None, ATT_HEADS, HEAD_DIM), lambda b, pt: (b, 0, 0)),
                  pl.BlockSpec((None, 1, width), lambda b, pt: (b, 0, 0)),
                  pl.BlockSpec((None, N_KV, HEAD_DIM), lambda b, pt: (b, 0, 0)),
                  pl.BlockSpec((None, N_KV, HEAD_DIM), lambda b, pt: (b, 0, 0))] + page_specs + page_specs,
        out_specs=pl.BlockSpec((None, 1, ATT_W), lambda b, pt: (b, 0, 0)))
    out = pl.pallas_call(
        functools.partial(_sattn_kernel, npg), grid_spec=grid_spec,
        out_shape=jax.ShapeDtypeStruct((db, 1, ATT_W), BF),
        compiler_params=_params(("arbitrary",)),
        name="sample_attention",
    )(page_table, q.reshape(db, ATT_HEADS, HEAD_DIM), keep.reshape(db, 1, width),
      k_new.reshape(db, N_KV, HEAD_DIM), v_new.reshape(db, N_KV, HEAD_DIM), *([ck] * npg), *([cv] * npg))
    return out.reshape(db, ATT_W)


def _post_kernel(final, nff, og_ref, oa_ref, x_ref, ga1_ref, sc2_ref, sh2_ref, ga2_ref, g2_ref, gf_ref,
                 wo_ref, wg_ref, wu_ref, wd_ref, y_ref, x1_ref, h2_ref, acc_ref):
    jf = pl.program_id(1)

    @pl.when(jf == 0)
    def _():
        mixed = _dot(og_ref[...], wo_ref[0:GDN_W, :]) + _dot(oa_ref[...], wo_ref[GDN_W:, :])
        x1 = x_ref[...] + ga1_ref[...] * mixed
        x1_ref[...] = x1
        h2_ref[...] = (_rmsnorm(x1, g2_ref[...]) * (1.0 + sc2_ref[...]) + sh2_ref[...]).astype(BF)
        acc_ref[...] = jnp.zeros(acc_ref.shape, F32)

    h2 = h2_ref[...]
    act = (_silu(_dot(h2, wg_ref[...])) * _dot(h2, wu_ref[...])).astype(BF)
    acc_ref[...] += _dot(act, wd_ref[...])

    @pl.when(jf == nff - 1)
    def _():
        x2 = x1_ref[...] + ga2_ref[...] * acc_ref[...]
        y_ref[...] = _rmsnorm(x2, gf_ref[...]) if final else x2


def _post(og, oa, x2d, mod, mod_spec, g2, gf, wo, wfi, wfo, tm, tf, final):
    rows, d = x2d.shape
    dff = wfo.shape[0]
    nff = dff // tf
    rowblk = lambda w: pl.BlockSpec((tm, w), lambda i, j: (i, 0))
    const = lambda shape: pl.BlockSpec(shape, lambda i, j: (0,) * len(shape))
    in_specs = [rowblk(GDN_W), rowblk(ATT_W), rowblk(d),
                mod_spec(2), mod_spec(4), mod_spec(3), mod_spec(5),
                const((1, d)), const((1, d)), const((d, d)),
                pl.BlockSpec((d, tf), lambda i, j: (0, j)),
                pl.BlockSpec((d, tf), lambda i, j: (0, nff + j)),
                pl.BlockSpec((tf, d), lambda i, j: (j, 0))]
    return pl.pallas_call(
        functools.partial(_post_kernel, final, nff),
        grid=(rows // tm, nff), in_specs=in_specs, out_specs=rowblk(d),
        out_shape=jax.ShapeDtypeStruct((rows, d), F32),
        scratch_shapes=[pltpu.VMEM((tm, d), F32), pltpu.VMEM((tm, d), BF), pltpu.VMEM((tm, d), F32)],
        compiler_params=_params(("arbitrary", "arbitrary")),
        name="post",
    )(og, oa, x2d, mod, mod, mod, mod, g2.reshape(1, d), gf.reshape(1, d), wo, wfi, wfi, wfo)


def _pick(n, prefs):
    for p in prefs:
        if n % p == 0:
            return p
    return n


def kernel(x_prompt, x_sample, c_prompt, c_sample, cache_k, cache_v, cache_idx_k, page_table, state_conv, state_ssm,
           w_ada, b_ada, g_norm1, w_in, w_conv, a_log, dt_bias, g_gdn_norm, w_out, g_norm2, w_ffn_in, w_ffn_out,
           g_final):
    bp, tp, d = x_prompt.shape
    db, ts, _ = x_sample.shape
    assert ts == 1 and d == GDN_W + ATT_W and tp % max(ATT_TK, GDN_CHUNK) == 0
    depth = w_in.shape[0]
    npg, page = page_table.shape[1], cache_k.shape[2]
    past = npg * page
    tm_in = _pick(tp, (512, 256))
    tg = _pick(tp, (256,))
    tm_post = _pick(tp, (1024, 512, 256))
    dff = w_ffn_out.shape[1]
    tf = _pick(dff, (256, 128))

    xp = x_prompt
    xs = x_sample.reshape(db, d)
    npad = (-(bp + db)) % 8
    c_all = jnp.concatenate([c_prompt, c_sample, jnp.zeros((npad, d), F32)], axis=0)
    new_p, new_s = [], []
    for l in range(depth):
        final = l == depth - 1
        mod = _modulation(c_all, w_ada[l], b_ada[l])
        mod_p = mod[:bp].reshape(bp, 1, 6 * d)
        mod_s = mod[bp:bp + db]
        w_nat, w_t, prow, pcol = _inproj_weights(w_in[l], a_log[l], dt_bias[l])
        wo = w_out[l].astype(BF)
        wfi = w_ffn_in[l].astype(BF)
        wfo = w_ffn_out[l].astype(BF)

        (qkvn, z, vnat, knat, ikT, tail, misc, gct, qT, iqT, iwT, kbf, vT, ikbf) = _inproj_prompt(
            xp, mod_p, g_norm1[l], w_nat, w_t, w_conv[l], prow, pcol, tm_in)
        og, ssm_p = _gdn_prompt(qkvn, z, misc, gct, g_gdn_norm[l], bp, tp, tg)
        oa = _attn_prompt(qT, iqT, iwT, kbf, vT, ikbf, bp, tp)
        tiles_b = tp // tm_post
        spec_p = lambda k: pl.BlockSpec((None, 1, d), lambda i, j: (i // tiles_b, 0, k))
        xp = _post(og, oa, xp.reshape(bp * tp, d), mod_p, spec_p, g_norm2[l], g_final, wo, wfi, wfo,
                   tm_post, tf, final).reshape(bp, tp, d)
        new_p.append((knat.reshape(bp, tp, N_KV, HEAD_DIM), vnat.reshape(bp, tp, N_KV, HEAD_DIM),
                      jnp.swapaxes(ikT, 1, 2), tail[:, 8 - (CONV_W - 1):, :], ssm_p))

        (qkvn_s, z_s, vnat_s, knat_s, slab_s, raw_s, misc_s, aq_s, iq_s) = _inproj_sample(
            xs, mod_s, g_norm1[l], w_nat, w_t, w_conv[l], prow, pcol, state_conv[l], past)
        og_s, ssm_s = _gdn_sample(qkvn_s, z_s, misc_s, g_gdn_norm[l], state_ssm[l], _pick(db, (8,)))
        ik_s = slab_s[:, 0:IDX_DIM]
        iw_s = slab_s[:, IDX_DIM:IDX_DIM + IDX_HEADS]
        scores = _sample_index_scores(iq_s, iw_s, ik_s, jnp.swapaxes(cache_idx_k[l], 1, 2), page_table)
        keep = _sample_select(scores, past + ts)
        oa_s = _sample_attention(aq_s, keep, knat_s, vnat_s, cache_k[l], cache_v[l], page_table)
        spec_s = lambda k: pl.BlockSpec((db, d), lambda i, j: (i, k))
        xs = _post(og_s, oa_s, xs, mod_s, spec_s, g_norm2[l], g_final, wo, wfi, wfo, db, tf, final)
        new_conv_s = jnp.concatenate([state_conv[l][:, 1:, :], raw_s[:, None, :]], axis=1)
        new_s.append((knat_s.reshape(db, ts, N_KV, HEAD_DIM), vnat_s.reshape(db, ts, N_KV, HEAD_DIM),
                      ik_s.reshape(db, ts, IDX_DIM), new_conv_s, ssm_s))

    stack = lambda states, n: jnp.stack([s[n] for s in states], axis=0)
    return (xp, xs.reshape(db, ts, d),
            stack(new_p, 0), stack(new_p, 1), stack(new_p, 2), stack(new_p, 3), stack(new_p, 4),
            stack(new_s, 0), stack(new_s, 1), stack(new_s, 2), stack(new_s, 3), stack(new_s, 4))
```

```python
import functools
import math

import numpy as np
import jax
import jax.numpy as jnp
from jax import lax
from jax.experimental import pallas as pl
from jax.experimental.pallas import tpu as pltpu

F32 = jnp.float32
BF = jnp.bfloat16
I32 = jnp.int32

HEAD_DIM = 128
GDN_HEADS = 4
ATT_HEADS = 4
N_KV = 2
IDX_HEADS = 8
IDX_DIM = 64
CONV_W = 4
TOPK_MAX = 256
QBLOCK = 128
ROPE_THETA = 500000.0
EPS = 1e-6
GDN_W = GDN_HEADS * HEAD_DIM
ATT_W = ATT_HEADS * HEAD_DIM
KV_W = N_KV * HEAD_DIM
IDX_W = IDX_HEADS * IDX_DIM
QKV_W = 3 * GDN_W

GDN_CHUNK = 128
ATT_TK = 256
VT_ROWS = HEAD_DIM + 16
IDX_TK = 512
ATT_BLK = 1024
LANES = 128
NAT_W = QKV_W + GDN_W + KV_W + LANES
T_AQ, T_AK, T_IQ, T_IK, T_MISC, T_AV = 0, 512, 768, 1280, 1344, 1360
T_ROWS = T_AV + KV_W
VMEM_LIMIT = 56 * 1024 * 1024
INT_MIN = np.int32(-2 ** 31)
NEG_INF = float("-inf")


def _params(sem):
    return pltpu.CompilerParams(dimension_semantics=sem, vmem_limit_bytes=VMEM_LIMIT)


def _dot(a, b):
    return jnp.dot(a, b, preferred_element_type=F32)


def _dot_nt(a, b):
    return lax.dot_general(a, b, (((1,), (1,)), ((), ())), preferred_element_type=F32)


def _split3(x):
    hi = x.astype(BF)
    r1 = x - hi.astype(F32)
    mid = r1.astype(BF)
    lo = (r1 - mid.astype(F32)).astype(BF)
    return hi, mid, lo


def _mm_hi(a, b):
    ah = a.astype(BF)
    al = (a - ah.astype(F32)).astype(BF)
    bh = b.astype(BF)
    bl = (b - bh.astype(F32)).astype(BF)
    return _dot(ah, bh) + (_dot(ah, bl) + _dot(al, bh))


def _sigmoid(x):
    return 1.0 / (1.0 + jnp.exp(-x))


def _silu(x):
    return x * _sigmoid(x)


def _softplus(x):
    return jnp.maximum(x, 0.0) + jnp.log(1.0 + jnp.exp(-jnp.abs(x)))


def _rmsnorm(x, g):
    return x * lax.rsqrt(jnp.mean(x * x, axis=-1, keepdims=True) + EPS) * g


def _mod_kernel(c_ref, w_ref, b_ref, o_ref):
    s = _silu(c_ref[...]).astype(BF)
    o_ref[...] = _dot(s, w_ref[...].astype(BF)) + b_ref[...]


def _modulation(c_all, w_ada, b_ada):
    n, d = c_all.shape
    cols = w_ada.shape[1]
    tn = d
    return pl.pallas_call(
        _mod_kernel,
        grid=(cols // tn,),
        in_specs=[pl.BlockSpec((n, d), lambda j: (0, 0)),
                  pl.BlockSpec((d, tn), lambda j: (0, j)),
                  pl.BlockSpec((1, tn), lambda j: (0, j))],
        out_specs=pl.BlockSpec((n, tn), lambda j: (0, j)),
        out_shape=jax.ShapeDtypeStruct((n, cols), F32),
        compiler_params=_params(("arbitrary",)),
        name="modulation",
    )(c_all, w_ada, b_ada.reshape(1, cols))


def _rope_rows(rt_ref, base, half, cos, sin):
    x1 = rt_ref[base:base + half, :]
    x2 = rt_ref[base + half:base + 2 * half, :]
    rt_ref[base:base + half, :] = x1 * cos - x2 * sin
    rt_ref[base + half:base + 2 * half, :] = x2 * cos + x1 * sin


def _project(x_ref, sc_ref, sh_ref, g1_ref, wn_ref, wt_ref, cs128_ref, cs64_ref, pcol_ref, rt_ref):
    h = _rmsnorm(x_ref[...], g1_ref[...]) * (1.0 + sc_ref[...]) + sh_ref[...]
    hb = h.astype(BF)
    nat = _dot(hb, wn_ref[...])
    rt_ref[...] = _dot_nt(wt_ref[...], hb)
    half = HEAD_DIM // 8
    cos, sin = cs128_ref[0:half, :], cs128_ref[half:2 * half, :]
    for hd in range(ATT_HEADS):
        _rope_rows(rt_ref, T_AQ + hd * HEAD_DIM, half, cos, sin)
    for hd in range(N_KV):
        _rope_rows(rt_ref, T_AK + hd * HEAD_DIM, half, cos, sin)
    half = IDX_DIM // 8
    cos, sin = cs64_ref[0:half, :], cs64_ref[half:2 * half, :]
    for hd in range(IDX_HEADS):
        _rope_rows(rt_ref, T_IQ + hd * IDX_DIM, half, cos, sin)
    _rope_rows(rt_ref, T_IK, half, cos, sin)
    mt = rt_ref[T_MISC:T_MISC + 16, :]
    r = lax.broadcasted_iota(I32, mt.shape, 0)
    a_log, dt_b = pcol_ref[:, 0:1], pcol_ref[:, 1:2]
    gate = -jnp.exp(a_log) * _softplus(mt + dt_b)
    mt = jnp.where(r < IDX_HEADS, mt * IDX_HEADS ** -0.5, jnp.where(r < IDX_HEADS + GDN_HEADS, _sigmoid(mt), gate))
    rt_ref[T_MISC:T_MISC + 16, :] = mt
    return nat, mt


def _misc_natural(m, prow_ref):
    lane = lax.broadcasted_iota(I32, m.shape, 1)
    gate = -jnp.exp(prow_ref[0:1, :]) * _softplus(m + prow_ref[1:2, :])
    return jnp.where(lane < GDN_HEADS, _sigmoid(m), jnp.where(lane < 3 * GDN_HEADS, gate, 0.0))


def _qkv_post(conv, qkvn_ref):
    c = _silu(conv)
    for hb in range(2 * GDN_HEADS):
        xh = c[:, hb * HEAD_DIM:(hb + 1) * HEAD_DIM]
        n = xh * lax.rsqrt(jnp.sum(xh * xh, axis=-1, keepdims=True) + EPS)
        if hb < GDN_HEADS:
            n = n * HEAD_DIM ** -0.5
        qkvn_ref[:, hb * HEAD_DIM:(hb + 1) * HEAD_DIM] = n
    qkvn_ref[:, 2 * GDN_W:] = c[:, 2 * GDN_W:]


def _inproj_prompt_kernel(tm, x_ref, sc_ref, sh_ref, g1_ref, wn_ref, wt_ref, wconv_ref, cs128_ref, cs64_ref,
                          prow_ref, pcol_ref, tril_ref,
                          qkvn_ref, z_ref, vnat_ref, knat_ref, ikT_ref, tail_ref, misc_ref, gct_ref,
                          qT_ref, iqT_ref, iwT_ref, kbf_ref, vT_ref, ikbf_ref,
                          xs_ref, rt_ref):
    i = pl.program_id(1)
    nat, mt = _project(x_ref, sc_ref, sh_ref, g1_ref, wn_ref, wt_ref, cs128_ref, cs64_ref, pcol_ref, rt_ref)

    @pl.when(i == 0)
    def _():
        xs_ref[0:8, :] = jnp.zeros((8, QKV_W), F32)

    @pl.when(i > 0)
    def _():
        xs_ref[0:8, :] = xs_ref[tm:tm + 8, :]

    xs_ref[8:tm + 8, :] = nat[:, 0:QKV_W]
    conv = wconv_ref[0:1, :] * xs_ref[5:tm + 5, :]
    for t in range(1, CONV_W):
        conv = conv + wconv_ref[t:t + 1, :] * xs_ref[5 + t:tm + 5 + t, :]
    tail_ref[...] = xs_ref[tm:tm + 8, :]
    _qkv_post(conv, qkvn_ref)
    z_ref[...] = nat[:, QKV_W:QKV_W + GDN_W]
    v0 = QKV_W + GDN_W
    for hd in range(N_KV):
        vnat_ref[pl.ds(hd, tm, stride=N_KV), :] = nat[:, v0 + hd * HEAD_DIM:v0 + (hd + 1) * HEAD_DIM]

    gm = _misc_natural(nat[:, NAT_W - LANES:], prow_ref)
    tril = tril_ref[...]
    hi, mid, lo = _split3(gm)
    gc = _dot(tril, hi) + (_dot(tril, mid) + _dot(tril, lo))
    lane = lax.broadcasted_iota(I32, gm.shape, 1)
    misc_ref[...] = jnp.where(lane < 2 * GDN_HEADS, gm, gc)
    hi, mid, lo = _split3(mt)
    gct = _dot_nt(hi, tril) + (_dot_nt(mid, tril) + _dot_nt(lo, tril))
    r = lax.broadcasted_iota(I32, mt.shape, 0)
    bg = jnp.where(r < IDX_HEADS + GDN_HEADS, mt, gct)[8:16, :]
    for cc in range(tm // GDN_CHUNK):
        gct_ref[cc] = bg[:, cc * GDN_CHUNK:(cc + 1) * GDN_CHUNK]

    iwT_ref[...] = mt[0:IDX_HEADS, :]
    qT_ref[...] = rt_ref[T_AQ:T_AQ + ATT_W, :].astype(BF)
    iqT_ref[...] = rt_ref[T_IQ:T_IQ + IDX_W, :].astype(BF)
    kn = rt_ref[T_AK:T_AK + KV_W, :].T
    for hd in range(N_KV):
        knat_ref[pl.ds(hd, tm, stride=N_KV), :] = kn[:, hd * HEAD_DIM:(hd + 1) * HEAD_DIM]
    kbf_ref[...] = kn.astype(BF)
    ikT_ref[...] = rt_ref[T_IK:T_IK + IDX_DIM, :]
    ikbf_ref[...] = rt_ref[T_IK:T_IK + LANES, :].T[:, 0:IDX_DIM].astype(BF)
    for cc in range(tm // ATT_TK):
        for hd in range(N_KV):
            lo = hd * VT_ROWS
            vT_ref[cc, lo:lo + HEAD_DIM, :] = rt_ref[T_AV + hd * HEAD_DIM:T_AV + (hd + 1) * HEAD_DIM,
                                                     cc * ATT_TK:(cc + 1) * ATT_TK].astype(BF)
            vT_ref[cc, lo + HEAD_DIM:lo + VT_ROWS, :] = jnp.ones((VT_ROWS - HEAD_DIM, ATT_TK), BF)


def _inproj_sample_kernel(x_ref, sc_ref, sh_ref, g1_ref, wn_ref, wt_ref, wconv_ref, cs128_ref, cs64_ref,
                          prow_ref, pcol_ref, s0_ref, s1_ref, s2_ref,
                          qkvn_ref, z_ref, vnat_ref, knat_ref, slab_ref, raw_ref, misc_ref, aq_ref, iq_ref,
                          rt_ref):
    nat, _ = _project(x_ref, sc_ref, sh_ref, g1_ref, wn_ref, wt_ref, cs128_ref, cs64_ref, pcol_ref, rt_ref)
    raw = nat[:, 0:QKV_W]
    raw_ref[...] = raw
    conv = wconv_ref[0:1, :] * s0_ref[...]
    conv = conv + wconv_ref[1:2, :] * s1_ref[...]
    conv = conv + wconv_ref[2:3, :] * s2_ref[...]
    conv = conv + wconv_ref[3:4, :] * raw
    _qkv_post(conv, qkvn_ref)
    z_ref[...] = nat[:, QKV_W:QKV_W + GDN_W]
    vnat_ref[...] = nat[:, QKV_W + GDN_W:QKV_W + GDN_W + KV_W]
    misc_ref[...] = _misc_natural(nat[:, NAT_W - LANES:], prow_ref)
    aq_ref[...] = rt_ref[T_AQ:T_AQ + ATT_W, :].T
    iq_ref[...] = rt_ref[T_IQ:T_IQ + IDX_W, :].T
    knat_ref[...] = rt_ref[T_AK:T_AK + KV_W, :].T
    slab_ref[...] = rt_ref[T_IK:T_IK + LANES, :].T


def _inproj_weights(w_in, a_log, dt_bias):
    offs = np.cumsum([0, QKV_W, GDN_W, GDN_HEADS, GDN_HEADS, ATT_W, KV_W, KV_W, IDX_W, IDX_DIM, IDX_HEADS])
    qkv, z, beta, a, aq, ak, av, iq, ik, iw = [w_in[:, offs[n]:offs[n + 1]] for n in range(10)]
    d = w_in.shape[0]
    pad = jnp.zeros((d, LANES - 3 * GDN_HEADS), w_in.dtype)
    w_nat = jnp.concatenate([qkv, z, av, beta, a, a, pad], axis=1).astype(BF)
    w_t = jnp.concatenate([aq, ak, iq, ik, iw, beta, a, av], axis=1).T.astype(BF)
    zrow = jnp.zeros((LANES - 3 * GDN_HEADS,), F32)
    prow = jnp.stack([jnp.concatenate([jnp.zeros((GDN_HEADS,), F32), a_log, a_log, zrow]),
                      jnp.concatenate([jnp.zeros((GDN_HEADS,), F32), dt_bias, dt_bias, zrow])])
    z12 = jnp.zeros((IDX_HEADS + GDN_HEADS,), F32)
    pcol = jnp.stack([jnp.concatenate([z12, a_log]), jnp.concatenate([z12, dt_bias])], axis=1)
    return w_nat, w_t, prow, pcol


def _rope_tables(pos, dim):
    half = dim // 8
    inv = ROPE_THETA ** (-jnp.arange(half, dtype=F32) / half)
    ang = pos.astype(F32)[:, None] * inv[None, :]
    return jnp.concatenate([jnp.cos(ang).T, jnp.sin(ang).T], axis=0)


def _inproj_prompt(x, mod, g1, w_nat, w_t, w_conv, prow, pcol, tm):
    bp, tp, d = x.shape
    nt = tp // tm
    rows = bp * tp
    pos = jnp.arange(tp, dtype=I32)
    cs128, cs64 = _rope_tables(pos, HEAD_DIM), _rope_tables(pos, IDX_DIM)
    ri = np.arange(tm)
    tril = jnp.asarray((ri[:, None] // GDN_CHUNK == ri[None, :] // GDN_CHUNK) & (ri[None, :] <= ri[:, None]), BF)
    const = lambda shape: pl.BlockSpec(shape, lambda b, i: (0,) * len(shape))
    rowblk = lambda w: pl.BlockSpec((tm, w), lambda b, i: (b * nt + i, 0))
    colblk = lambda h: pl.BlockSpec((h, tm), lambda b, i: (0, b * nt + i))
    in_specs = [
        pl.BlockSpec((None, tm, d), lambda b, i: (b, i, 0)),
        pl.BlockSpec((None, 1, d), lambda b, i: (b, 0, 1)),
        pl.BlockSpec((None, 1, d), lambda b, i: (b, 0, 0)),
        const((1, d)), const((d, NAT_W)), const((T_ROWS, d)), const((CONV_W, QKV_W)),
        pl.BlockSpec((HEAD_DIM // 4, tm), lambda b, i: (0, i)),
        pl.BlockSpec((IDX_DIM // 4, tm), lambda b, i: (0, i)),
        const((2, LANES)), const((16, 2)), const((tm, tm)),
    ]
    out_shape = [
        jax.ShapeDtypeStruct((rows, QKV_W), F32), jax.ShapeDtypeStruct((rows, GDN_W), F32),
        jax.ShapeDtypeStruct((rows * N_KV, HEAD_DIM), F32), jax.ShapeDtypeStruct((rows * N_KV, HEAD_DIM), F32),
        jax.ShapeDtypeStruct((bp, IDX_DIM, tp), F32), jax.ShapeDtypeStruct((bp, 8, QKV_W), F32),
        jax.ShapeDtypeStruct((rows, LANES), F32), jax.ShapeDtypeStruct((rows // GDN_CHUNK, 8, GDN_CHUNK), F32),
        jax.ShapeDtypeStruct((ATT_W, rows), BF), jax.ShapeDtypeStruct((IDX_W, rows), BF),
        jax.ShapeDtypeStruct((IDX_HEADS, rows), F32), jax.ShapeDtypeStruct((rows, KV_W), BF),
        jax.ShapeDtypeStruct((rows // ATT_TK, N_KV * VT_ROWS, ATT_TK), BF), jax.ShapeDtypeStruct((rows, IDX_DIM), BF),
    ]
    out_specs = [
        rowblk(QKV_W), rowblk(GDN_W),
        pl.BlockSpec((tm * N_KV, HEAD_DIM), lambda b, i: (b * nt + i, 0)),
        pl.BlockSpec((tm * N_KV, HEAD_DIM), lambda b, i: (b * nt + i, 0)),
        pl.BlockSpec((None, IDX_DIM, tm), lambda b, i: (b, 0, i)),
        pl.BlockSpec((None, 8, QKV_W), lambda b, i: (b, 0, 0)),
        rowblk(LANES),
        pl.BlockSpec((tm // GDN_CHUNK, 8, GDN_CHUNK), lambda b, i: (b * nt + i, 0, 0)),
        colblk(ATT_W), colblk(IDX_W), colblk(IDX_HEADS), rowblk(KV_W),
        pl.BlockSpec((tm // ATT_TK, N_KV * VT_ROWS, ATT_TK), lambda b, i: (b * nt + i, 0, 0)),
        rowblk(IDX_DIM),
    ]
    return pl.pallas_call(
        functools.partial(_inproj_prompt_kernel, tm),
        grid=(bp, nt), in_specs=in_specs, out_specs=out_specs, out_shape=out_shape,
        scratch_shapes=[pltpu.VMEM((tm + 8, QKV_W), F32), pltpu.VMEM((T_ROWS, tm), F32)],
        compiler_params=_params(("arbitrary", "arbitrary")),
        name="inproj_prompt",
    )(x, mod, mod, g1.reshape(1, d), w_nat, w_t, w_conv, cs128, cs64, prow, pcol, tril)


def _inproj_sample(x, mod, g1, w_nat, w_t, w_conv, prow, pcol, conv_state, past):
    db, d = x.shape
    pos = jnp.full((db,), past, I32)
    cs128, cs64 = _rope_tables(pos, HEAD_DIM), _rope_tables(pos, IDX_DIM)
    full = lambda shape: pl.BlockSpec(shape, lambda i: (0,) * len(shape))
    in_specs = [
        full((db, d)),
        pl.BlockSpec((db, d), lambda i: (0, 1)), pl.BlockSpec((db, d), lambda i: (0, 0)),
        full((1, d)), full((d, NAT_W)), full((T_ROWS, d)), full((CONV_W, QKV_W)),
        full((HEAD_DIM // 4, db)), full((IDX_DIM // 4, db)), full((2, LANES)), full((16, 2)),
        full((db, QKV_W)), full((db, QKV_W)), full((db, QKV_W)),
    ]
    widths = [QKV_W, GDN_W, KV_W, KV_W, LANES, QKV_W, LANES, ATT_W, IDX_W]
    return pl.pallas_call(
        _inproj_sample_kernel,
        grid=(1,), in_specs=in_specs,
        out_specs=[full((db, w)) for w in widths],
        out_shape=[jax.ShapeDtypeStruct((db, w), F32) for w in widths],
        scratch_shapes=[pltpu.VMEM((T_ROWS, db), F32)],
        compiler_params=_params(("arbitrary",)),
        name="inproj_sample",
    )(x, mod, mod, g1.reshape(1, d), w_nat, w_t, w_conv, cs128, cs64, prow, pcol,
      conv_state[:, 0], conv_state[:, 1], conv_state[:, 2])


def _unit_lower_inverses(lows):
    c = lows[0].shape[0]
    ii = lax.broadcasted_iota(I32, (c, c), 0)
    jj = lax.broadcasted_iota(I32, (c, c), 1)
    eye = jnp.where(ii == jj, 1.0, 0.0)
    levels = int(math.log2(c)) - 1
    ts = [eye - low for low in lows]
    ps = [_mm_hi(low, low) for low in lows]
    for lvl in range(levels):
        if lvl == levels - 1:
            ts = [t + _mm_hi(t, p) for t, p in zip(ts, ps)]
        else:
            both = [_mm_hi(jnp.concatenate([t, p], axis=0), p) for t, p in zip(ts, ps)]
            ts = [t + b[0:c] for t, b in zip(ts, both)]
            ps = [b[c:] for b in both]
    return ts


def _gdn_prompt_kernel(tg, qkvn_ref, z_ref, misc_ref, gct_ref, gn_ref, o_ref, ssm_ref, s_ref):
    i = pl.program_id(1)

    @pl.when(i == 0)
    def _():
        s_ref[...] = jnp.zeros(s_ref.shape, F32)

    c = GDN_CHUNK
    ii = lax.broadcasted_iota(I32, (c, c), 0)
    jj = lax.broadcasted_iota(I32, (c, c), 1)
    pairs = [(cc, hd) for cc in range(tg // c) for hd in range(GDN_HEADS)]

    qs, ks, gcs, rhs, lows, intras = {}, {}, {}, {}, [], {}
    for cc, hd in pairs:
        r0, lo = cc * c, hd * HEAD_DIM
        q = qkvn_ref[r0:r0 + c, lo:lo + HEAD_DIM]
        k = qkvn_ref[r0:r0 + c, GDN_W + lo:GDN_W + lo + HEAD_DIM]
        v = qkvn_ref[r0:r0 + c, 2 * GDN_W + lo:2 * GDN_W + lo + HEAD_DIM]
        beta_c = misc_ref[r0:r0 + c, hd:hd + 1]
        gc_c = misc_ref[r0:r0 + c, 2 * GDN_HEADS + hd:2 * GDN_HEADS + hd + 1]
        gc_r = gct_ref[cc, GDN_HEADS + hd:GDN_HEADS + hd + 1, :]
        decay = jnp.where(ii >= jj, jnp.exp(jnp.where(ii >= jj, gc_c - gc_r, 0.0)), 0.0)
        kb = k * beta_c
        both = _dot_nt(jnp.concatenate([kb, q], axis=0).astype(BF), k.astype(BF))
        lows.append(jnp.where(ii > jj, both[0:c] * decay, 0.0))
        intras[cc, hd] = (both[c:] * decay).astype(BF)
        rhs[cc, hd] = jnp.concatenate([v * beta_c, kb * jnp.exp(gc_c)], axis=1).astype(BF)
        qs[cc, hd], ks[cc, hd], gcs[cc, hd] = q, k, gc_c
    ts = _unit_lower_inverses(lows)
    uws = {p: _dot(t.astype(BF), rhs[p]) for p, t in zip(pairs, ts)}

    for cc, hd in pairs:
        r0, lo = cc * c, hd * HEAD_DIM
        q, k, gc_c, uw = qs[cc, hd], ks[cc, hd], gcs[cc, hd], uws[cc, hd]
        s = s_ref[hd]
        ws = _dot(jnp.concatenate([uw[:, HEAD_DIM:], q * jnp.exp(gc_c)], axis=0).astype(BF), s.astype(BF))
        v_new = (uw[:, 0:HEAD_DIM] - ws[0:c]).astype(BF)
        o = ws[c:] + _dot(intras[cc, hd], v_new)
        g_last = gc_c[c - 1:c, :]
        kd = k * jnp.exp(g_last - gc_c)
        s_ref[hd] = s * jnp.exp(g_last) + _dot(kd.T.astype(BF), v_new)
        o = _rmsnorm(o, gn_ref[...]) * _silu(z_ref[r0:r0 + c, lo:lo + HEAD_DIM])
        o_ref[r0:r0 + c, lo:lo + HEAD_DIM] = o.astype(BF)
    ssm_ref[...] = s_ref[...]


def _gdn_prompt(qkvn, z, misc, gct, g_norm, bp, tp, tg):
    nt = tp // tg
    rows = bp * tp
    rowblk = lambda w: pl.BlockSpec((tg, w), lambda b, i: (b * nt + i, 0))
    return pl.pallas_call(
        functools.partial(_gdn_prompt_kernel, tg),
        grid=(bp, nt),
        in_specs=[rowblk(QKV_W), rowblk(GDN_W), rowblk(LANES),
                  pl.BlockSpec((tg // GDN_CHUNK, 8, GDN_CHUNK), lambda b, i: (b * nt + i, 0, 0)),
                  pl.BlockSpec((1, HEAD_DIM), lambda b, i: (0, 0))],
        out_specs=[rowblk(GDN_W),
                   pl.BlockSpec((None, GDN_HEADS, HEAD_DIM, HEAD_DIM), lambda b, i: (b, 0, 0, 0))],
        out_shape=[jax.ShapeDtypeStruct((rows, GDN_W), BF),
                   jax.ShapeDtypeStruct((bp, GDN_HEADS, HEAD_DIM, HEAD_DIM), F32)],
        scratch_shapes=[pltpu.VMEM((GDN_HEADS, HEAD_DIM, HEAD_DIM), F32)],
        compiler_params=_params(("arbitrary", "arbitrary")),
        name="gdn_prompt",
    )(qkvn, z, misc, gct, g_norm.reshape(1, HEAD_DIM))


def _gdn_sample_kernel(nb, qkvn_ref, z_ref, misc_ref, gn_ref, s_ref, o_ref, ssm_ref):
    for bi in range(nb):
        for hd in range(GDN_HEADS):
            lo = hd * HEAD_DIM
            q = qkvn_ref[bi:bi + 1, lo:lo + HEAD_DIM]
            k = qkvn_ref[bi:bi + 1, GDN_W + lo:GDN_W + lo + HEAD_DIM]
            v = qkvn_ref[bi:bi + 1, 2 * GDN_W + lo:2 * GDN_W + lo + HEAD_DIM]
            beta = misc_ref[bi:bi + 1, hd:hd + 1]
            g = misc_ref[bi:bi + 1, GDN_HEADS + hd:GDN_HEADS + hd + 1]
            kcol = jnp.broadcast_to(k, (HEAD_DIM, HEAD_DIM)).T
            qcol = jnp.broadcast_to(q, (HEAD_DIM, HEAD_DIM)).T
            s = s_ref[bi, hd] * jnp.exp(g)
            ks = jnp.sum(kcol * s, axis=0, keepdims=True)
            delta = (v - ks) * beta
            s = s + kcol * delta
            ssm_ref[bi, hd] = s
            o = jnp.sum(qcol * s, axis=0, keepdims=True)
            o = _rmsnorm(o, gn_ref[...]) * _silu(z_ref[bi:bi + 1, lo:lo + HEAD_DIM])
            o_ref[bi:bi + 1, lo:lo + HEAD_DIM] = o.astype(BF)


def _gdn_sample(qkvn, z, misc, g_norm, state, nb):
    db = qkvn.shape[0]
    rowblk = lambda w: pl.BlockSpec((nb, w), lambda i: (i, 0))
    sblk = pl.BlockSpec((nb, GDN_HEADS, HEAD_DIM, HEAD_DIM), lambda i: (i, 0, 0, 0))
    return pl.pallas_call(
        functools.partial(_gdn_sample_kernel, nb),
        grid=(db // nb,),
        in_specs=[rowblk(QKV_W), rowblk(GDN_W), rowblk(LANES), pl.BlockSpec((1, HEAD_DIM), lambda i: (0, 0)), sblk],
        out_specs=[rowblk(GDN_W), sblk],
        out_shape=[jax.ShapeDtypeStruct((db, GDN_W), BF), jax.ShapeDtypeStruct(state.shape, F32)],
        compiler_params=_params(("arbitrary",)),
        name="gdn_sample",
    )(qkvn, z, misc, g_norm.reshape(1, HEAD_DIM), state)


def _ordered_word_to_float(u):
    s = u ^ INT_MIN
    return lax.bitcast_convert_type(s ^ ((s >> 31) & np.int32(0x7FFFFFFF)), F32)


def _select_topk_bias(i_ref, t_ref, nch, tk, k_sel, idx_bits, taken=0.0, dropped=NEG_INF):
    nl = i_ref.shape[1]
    row = lax.broadcasted_iota(I32, (tk, 1), 0)

    def chunk(c):
        return pl.ds(pl.multiple_of(c * tk, tk), tk)

    def count(ref, pred):
        def body(c, acc):
            m = pred(ref[chunk(c), :])
            return acc + jnp.sum(jnp.where(m, 1, 0).astype(I32).reshape(tk // 8, 8, nl), axis=0)
        acc = lax.fori_loop(0, nch, body, jnp.zeros((8, nl), I32))
        return jnp.sum(acc, axis=0, keepdims=True)

    def bit_body(i, carry):
        t_u, n_ge = carry
        cand_u = t_u | jnp.left_shift(jnp.int32(1), 31 - i)
        cand = _ordered_word_to_float(cand_u)
        cnt = count(i_ref, lambda sc: sc >= cand)
        take = cnt >= k_sel
        return jnp.where(take, cand_u, t_u), jnp.where(take, cnt, n_ge)

    t_u, n_ge = lax.fori_loop(0, 32, bit_body, (jnp.zeros((1, nl), I32), jnp.full((1, nl), -1, I32)))
    thr = jnp.where((t_u >> 23) == 0, NEG_INF, _ordered_word_to_float(t_u))

    def tie_break(_):
        need = k_sel - count(i_ref, lambda sc: sc > thr)

        def mark(c, carry):
            t_ref[chunk(c), :] = jnp.where(i_ref[chunk(c), :] == thr, c * tk + row, 2 ** idx_bits)
            return carry

        lax.fori_loop(0, nch, mark, 0)

        def idx_body(i, x):
            cand = x | jnp.left_shift(jnp.int32(1), idx_bits - 1 - i)
            cnt = count(t_ref, lambda idx: idx < cand)
            return jnp.where(cnt < need, cand, x)

        return lax.fori_loop(0, idx_bits, idx_body, jnp.zeros((1, nl), I32))

    x = lax.cond(jnp.max(n_ge) > k_sel, tie_break, lambda _: jnp.full((1, nl), 2 ** idx_bits - 1, I32), 0)

    def write(c, carry):
        sc = i_ref[chunk(c), :]
        sel = (sc > thr) | ((sc == thr) & (c * tk + row <= x))
        i_ref[chunk(c), :] = jnp.where(sel & (sc > NEG_INF), taken, dropped)
        return carry

    lax.fori_loop(0, nch, write, 0)


def _attn_prompt_kernel(tq, tk, blk, k_sel, idx_bits, qT_ref, iqT_ref, iwT_ref, kbf_ref, vT_ref, ikbf_ref, o_ref,
                        i_ref, t_ref):
    t0 = pl.program_id(1) * tq
    nch = (t0 + tq + tk - 1) // tk
    nblk = (t0 + tq + blk - 1) // blk
    qpos = t0 + lax.broadcasted_iota(I32, (1, tq), 1)
    row = lax.broadcasted_iota(I32, (tk, 1), 0)
    w = iwT_ref[...]
    pairs = IDX_HEADS // 2
    rhs = [jnp.concatenate([iqT_ref[(2 * p) * IDX_DIM:(2 * p + 1) * IDX_DIM, :],
                            iqT_ref[(2 * p + 1) * IDX_DIM:(2 * p + 2) * IDX_DIM, :]], axis=1) for p in range(pairs)]

    def chunk(c):
        return pl.ds(pl.multiple_of(c * tk, tk), tk)

    def idx_body(c, carry):
        ikc = ikbf_ref[chunk(c), :]
        acc = jnp.zeros((tk, tq), F32)
        for p in range(pairs):
            d = jnp.maximum(_dot(ikc, rhs[p]), 0.0)
            acc = acc + d[:, 0:tq] * w[2 * p:2 * p + 1, :] + d[:, tq:] * w[2 * p + 1:2 * p + 2, :]
        i_ref[chunk(c), :] = jnp.where(c * tk + row <= qpos, acc, NEG_INF)
        return carry

    lax.fori_loop(0, nch, idx_body, 0)

    def fill_body(c, carry):
        i_ref[chunk(c), :] = jnp.full((tk, tq), NEG_INF, F32)
        return carry

    lax.fori_loop(nch, nblk * (blk // tk), fill_body, 0)
    _select_topk_bias(i_ref, t_ref, nch, tk, k_sel, idx_bits)

    scale2 = HEAD_DIM ** -0.5 * math.log2(math.e)
    rep = ATT_HEADS // N_KV
    sub = ATT_TK
    nsub = blk // sub
    qg = [jnp.concatenate([qT_ref[(rep * g + r) * HEAD_DIM:(rep * g + r + 1) * HEAD_DIM, :] for r in range(rep)],
                          axis=1) for g in range(N_KV)]

    def body(c, carry):
        out = []
        for g in range(N_KV):
            m, acc = carry[g]
            ss = []
            for i in range(nsub):
                rows = pl.ds(pl.multiple_of(c * blk + i * sub, sub), sub)
                bias = i_ref[rows, :]
                ss.append(_dot(kbf_ref[rows, g * HEAD_DIM:(g + 1) * HEAD_DIM], qg[g]) * scale2
                          + jnp.concatenate([bias] * rep, axis=1))
            m_new = m
            for s in ss:
                m_new = jnp.maximum(m_new, jnp.max(s, axis=0, keepdims=True))
            m_safe = jnp.where(m_new == NEG_INF, 0.0, m_new)
            acc = acc * jnp.exp2(m - m_safe)
            for i, s in enumerate(ss):
                p = jnp.exp2(s - m_safe).astype(BF)
                acc = acc + _dot(vT_ref[c * nsub + i, g * VT_ROWS:(g + 1) * VT_ROWS, :], p)
            out.append((m_new, acc))
        return tuple(out)

    init = tuple((jnp.full((1, rep * tq), NEG_INF, F32), jnp.zeros((VT_ROWS, rep * tq), F32)) for _ in range(N_KV))
    res = lax.fori_loop(0, nblk, body, init)
    for g in range(N_KV):
        acc = res[g][1]
        o_t = acc[0:HEAD_DIM] / acc[HEAD_DIM:HEAD_DIM + 1]
        for r in range(rep):
            hd = rep * g + r
            o_ref[:, hd * HEAD_DIM:(hd + 1) * HEAD_DIM] = o_t[:, r * tq:(r + 1) * tq].T.astype(BF)


def _attn_prompt(qT, iqT, iwT, kbf, vT, ikbf, bp, tp):
    tq = QBLOCK
    tk = _pick(tp, (IDX_TK, ATT_TK))
    blk = _pick(tp, (ATT_BLK, IDX_TK, ATT_TK))
    assert blk % tk == 0 and blk % ATT_TK == 0
    nq = tp // tq
    rows = bp * tp
    k_sel = min(TOPK_MAX, tp // 4)
    idx_bits = max(1, int(tp - 1).bit_length())
    colblk = lambda h: pl.BlockSpec((h, tq), lambda b, j: (0, b * nq + j))
    return pl.pallas_call(
        functools.partial(_attn_prompt_kernel, tq, tk, blk, k_sel, idx_bits),
        grid=(bp, nq),
        in_specs=[colblk(ATT_W), colblk(IDX_W), colblk(IDX_HEADS),
                  pl.BlockSpec((tp, KV_W), lambda b, j: (b, 0)),
                  pl.BlockSpec((tp // ATT_TK, N_KV * VT_ROWS, ATT_TK), lambda b, j: (b, 0, 0)),
                  pl.BlockSpec((tp, IDX_DIM), lambda b, j: (b, 0))],
        out_specs=pl.BlockSpec((tq, ATT_W), lambda b, j: (b * nq + j, 0)),
        out_shape=jax.ShapeDtypeStruct((rows, ATT_W), BF),
        scratch_shapes=[pltpu.VMEM((tp, tq), F32), pltpu.VMEM((tp, tq), I32)],
        compiler_params=_params(("arbitrary", "arbitrary")),
        name="attn_prompt",
    )(qT, iqT, iwT, kbf, vT, ikbf)


def _sidx_kernel(npg, pt_ref, iq_ref, iw_ref, ikn_ref, *refs):
    pages, out_ref = refs[:npg], refs[npg]
    iq = iq_ref[...].astype(BF)
    w = iw_ref[...]
    for p in range(npg):
        d = jnp.maximum(_dot(iq, pages[p][...].astype(BF)), 0.0)
        out_ref[:, p * LANES:(p + 1) * LANES] = jnp.sum(d * w, axis=0, keepdims=True)
    dn = jnp.sum(iq.astype(F32) * ikn_ref[...].astype(BF).astype(F32), axis=1, keepdims=True)
    sn = jnp.sum(jnp.maximum(dn, 0.0) * w, axis=0, keepdims=True)
    lane = lax.broadcasted_iota(I32, (1, LANES), 1)
    out_ref[:, npg * LANES:(npg + 1) * LANES] = jnp.where(lane == 0, sn, NEG_INF)


def _sample_index_scores(iq, iw, ik_new, cache_ik_t, page_table):
    db, npg = page_table.shape
    page = cache_ik_t.shape[2]
    assert page == LANES
    width = (npg + 1) * LANES
    page_specs = [pl.BlockSpec((None, IDX_DIM, page), functools.partial(lambda p, b, pt: (pt[b, p], 0, 0), p))
                  for p in range(npg)]
    grid_spec = pltpu.PrefetchScalarGridSpec(
        num_scalar_prefetch=1, grid=(db,),
        in_specs=[pl.BlockSpec((None, IDX_HEADS, IDX_DIM), lambda b, pt: (b, 0, 0)),
                  pl.BlockSpec((None, IDX_HEADS, 1), lambda b, pt: (b, 0, 0)),
                  pl.BlockSpec((None, 1, IDX_DIM), lambda b, pt: (b, 0, 0))] + page_specs,
        out_specs=pl.BlockSpec((None, 1, width), lambda b, pt: (b, 0, 0)))
    out = pl.pallas_call(
        functools.partial(_sidx_kernel, npg), grid_spec=grid_spec,
        out_shape=jax.ShapeDtypeStruct((db, 1, width), F32),
        compiler_params=_params(("arbitrary",)),
        name="sample_index_scores",
    )(page_table, iq.reshape(db, IDX_HEADS, IDX_DIM), iw.reshape(db, IDX_HEADS, 1), ik_new.reshape(db, 1, IDX_DIM),
      *([cache_ik_t] * npg))
    return out.reshape(db, width)


def _ssel_kernel(nch, k_sel, idx_bits, s_ref, spread_ref, keep_ref, i_ref, t_ref):
    for p in range(nch):
        i_ref[p * LANES:(p + 1) * LANES, :] = s_ref[:, p * LANES:(p + 1) * LANES].T
    _select_topk_bias(i_ref, t_ref, nch, LANES, k_sel, idx_bits, taken=1.0, dropped=0.0)
    rows = N_KV * LANES
    for p in range(nch - 1):
        k2 = _dot(spread_ref[...], i_ref[p * LANES:(p + 1) * LANES, :].astype(BF))
        for h in range(N_KV):
            keep_ref[:, p * rows + h * LANES:p * rows + (h + 1) * LANES] = k2[h * LANES:(h + 1) * LANES, :].T
    keep_ref[:, (nch - 1) * rows:(nch - 1) * rows + LANES] = i_ref[(nch - 1) * LANES:nch * LANES, :].T


def _sample_select(scores, n_keys):
    db, width = scores.shape
    assert db == LANES
    nch = width // LANES
    k_sel = min(TOPK_MAX, n_keys // 4)
    idx_bits = max(1, int(width - 1).bit_length())
    rows = N_KV * LANES
    out_w = (nch - 1) * rows + LANES
    spread = jnp.asarray(np.arange(rows)[:, None] // N_KV == np.arange(LANES)[None, :], BF)
    return pl.pallas_call(
        functools.partial(_ssel_kernel, nch, k_sel, idx_bits),
        grid=(1,),
        in_specs=[pl.BlockSpec((db, width), lambda i: (0, 0)), pl.BlockSpec((rows, LANES), lambda i: (0, 0))],
        out_specs=pl.BlockSpec((db, out_w), lambda i: (0, 0)),
        out_shape=jax.ShapeDtypeStruct((db, out_w), F32),
        scratch_shapes=[pltpu.VMEM((width, db), F32), pltpu.VMEM((width, db), I32)],
        compiler_params=_params(("arbitrary",)),
        name="sample_select",
    )(scores, spread)


def _sattn_kernel(npg, pt_ref, q_ref, keep_ref, kn_ref, vn_ref, *refs):
    kpages, vpages, o_ref = refs[:npg], refs[npg:2 * npg], refs[2 * npg]
    rep = ATT_HEADS // N_KV
    rows = N_KV * LANES
    scale = HEAD_DIM ** -0.5
    q = q_ref[...]
    q8 = jnp.concatenate([q, jnp.zeros((8 - ATT_HEADS, HEAD_DIM), F32)], axis=0).astype(BF)
    hrow = lax.broadcasted_iota(I32, (8, 1), 0)
    lane = lax.broadcasted_iota(I32, (1, LANES), 1)
    own_kv = lax.broadcasted_iota(I32, (1, rows), 1) % N_KV == hrow // rep
    parts = []
    for p in range(npg):
        s2 = _dot_nt(q8, kpages[p][...].astype(BF))
        keep2 = keep_ref[:, p * rows:(p + 1) * rows]
        parts.append(jnp.where(own_kv & (keep2 > 0.5), s2 * scale, NEG_INF))
    kn = kn_ref[...].astype(BF).astype(F32)
    kn8 = jnp.where(hrow // rep == 0, kn[0:1, :], kn[1:2, :])
    s_new = jnp.sum(q8.astype(F32) * kn8, axis=1, keepdims=True)
    keep_new = keep_ref[:, npg * rows:npg * rows + 1]
    parts.append(jnp.where((lane == 0) & (keep_new > 0.5), s_new * scale, NEG_INF))
    s = jnp.concatenate(parts, axis=1)
    m = jnp.max(s, axis=1, keepdims=True)
    e = jnp.exp(s - m)
    pr = (e / jnp.sum(e, axis=1, keepdims=True)).astype(BF)
    o8 = jnp.zeros((8, HEAD_DIM), F32)
    for p in range(npg):
        o8 = o8 + _dot(pr[:, p * rows:(p + 1) * rows], vpages[p][...].astype(BF))
    vn = vn_ref[...].astype(BF).astype(F32)
    vn8 = jnp.where(hrow // rep == 0, vn[0:1, :], vn[1:2, :])
    o8 = o8 + pr[:, npg * rows:npg * rows + 1].astype(F32) * vn8
    for hd in range(ATT_HEADS):
        o_ref[:, hd * HEAD_DIM:(hd + 1) * HEAD_DIM] = o8[hd:hd + 1, :].astype(BF)


def _sample_attention(q, keep, k_new, v_new, cache_k, cache_v, page_table):
    db, npg = page_table.shape
    n_phys, page = cache_k.shape[0], cache_k.shape[1]
    assert page == LANES and N_KV == 2
    rows = page * N_KV
    width = npg * rows + LANES
    ck = cache_k.reshape(n_phys, rows, HEAD_DIM)
    cv = cache_v.reshape(n_phys, rows, HEAD_DIM)
    page_specs = [pl.BlockSpec((None, rows, HEAD_DIM), functools.partial(lambda p, b, pt: (pt[b, p], 0, 0), p))
                  for p in range(npg)]
    grid_spec = pltpu.PrefetchScalarGridSpec(
        num_scalar_prefetch=1, grid=(db,),
        in_specs=[pl.BlockSpec((None, ATT_HEADS, HEAD_DIM), lambda b, pt: (b, 0, 0)),
                  pl.BlockSpec((None, 1, width), lambda b, pt: (b, 0, 0)),
                  pl.BlockSpec((None, N_KV, HEAD_DIM), lambda b, pt: (b, 0, 0)),
                  pl.BlockSpec((None, N_KV, HEAD_DIM), lambda b, pt: (b, 0, 0))] + page_specs + page_specs,
        out_specs=pl.BlockSpec((None, 1, ATT_W), lambda b, pt: (b, 0, 0)))
    out = pl.pallas_call(
        functools.partial(_sattn_kernel, npg), grid_spec=grid_spec,
        out_shape=jax.ShapeDtypeStruct((db, 1, ATT_W), BF),
        compiler_params=_params(("arbitrary",)),
        name="sample_attention",
    )(page_table, q.reshape(db, ATT_HEADS, HEAD_DIM), keep.reshape(db, 1, width),
      k_new.reshape(db, N_KV, HEAD_DIM), v_new.reshape(db, N_KV, HEAD_DIM), *([ck] * npg), *([cv] * npg))
    return out.reshape(db, ATT_W)


def _post_kernel(final, nff, og_ref, oa_ref, x_ref, ga1_ref, sc2_ref, sh2_ref, ga2_ref, g2_ref, gf_ref,
                 wo_ref, wg_ref, wu_ref, wd_ref, y_ref, x1_ref, h2_ref, acc_ref):
    jf = pl.program_id(1)

    @pl.when(jf == 0)
    def _():
        mixed = _dot(og_ref[...], wo_ref[0:GDN_W, :]) + _dot(oa_ref[...], wo_ref[GDN_W:, :])
        x1 = x_ref[...] + ga1_ref[...] * mixed
        x1_ref[...] = x1
        h2_ref[...] = (_rmsnorm(x1, g2_ref[...]) * (1.0 + sc2_ref[...]) + sh2_ref[...]).astype(BF)
        acc_ref[...] = jnp.zeros(acc_ref.shape, F32)

    h2 = h2_ref[...]
    act = (_silu(_dot(h2, wg_ref[...])) * _dot(h2, wu_ref[...])).astype(BF)
    acc_ref[...] += _dot(act, wd_ref[...])

    @pl.when(jf == nff - 1)
    def _():
        x2 = x1_ref[...] + ga2_ref[...] * acc_ref[...]
        y_ref[...] = _rmsnorm(x2, gf_ref[...]) if final else x2


def _post(og, oa, x2d, mod, mod_spec, g2, gf, wo, wfi, wfo, tm, tf, final):
    rows, d = x2d.shape
    dff = wfo.shape[0]
    nff = dff // tf
    rowblk = lambda w: pl.BlockSpec((tm, w), lambda i, j: (i, 0))
    const = lambda shape: pl.BlockSpec(shape, lambda i, j: (0,) * len(shape))
    in_specs = [rowblk(GDN_W), rowblk(ATT_W), rowblk(d),
                mod_spec(2), mod_spec(4), mod_spec(3), mod_spec(5),
                const((1, d)), const((1, d)), const((d, d)),
                pl.BlockSpec((d, tf), lambda i, j: (0, j)),
                pl.BlockSpec((d, tf), lambda i, j: (0, nff + j)),
                pl.BlockSpec((tf, d), lambda i, j: (j, 0))]
    return pl.pallas_call(
        functools.partial(_post_kernel, final, nff),
        grid=(rows // tm, nff), in_specs=in_specs, out_specs=rowblk(d),
        out_shape=jax.ShapeDtypeStruct((rows, d), F32),
        scratch_shapes=[pltpu.VMEM((tm, d), F32), pltpu.VMEM((tm, d), BF), pltpu.VMEM((tm, d), F32)],
        compiler_params=_params(("arbitrary", "arbitrary")),
        name="post",
    )(og, oa, x2d, mod, mod, mod, mod, g2.reshape(1, d), gf.reshape(1, d), wo, wfi, wfi, wfo)


def _pick(n, prefs):
    for p in prefs:
        if n % p == 0:
            return p
    return n


def kernel(x_prompt, x_sample, c_prompt, c_sample, cache_k, cache_v, cache_idx_k, page_table, state_conv, state_ssm,
           w_ada, b_ada, g_norm1, w_in, w_conv, a_log, dt_bias, g_gdn_norm, w_out, g_norm2, w_ffn_in, w_ffn_out,
           g_final):
    bp, tp, d = x_prompt.shape
    db, ts, _ = x_sample.shape
    assert ts == 1 and d == GDN_W + ATT_W and tp % max(ATT_TK, GDN_CHUNK) == 0
    depth = w_in.shape[0]
    npg, page = page_table.shape[1], cache_k.shape[2]
    past = npg * page
    tm_in = _pick(tp, (512, 256))
    tg = _pick(tp, (256,))
    tm_post = _pick(tp, (1024, 512, 256))
    dff = w_ffn_out.shape[1]
    tf = _pick(dff, (256, 128))

    xp = x_prompt
    xs = x_sample.reshape(db, d)
    npad = (-(bp + db)) % 8
    c_all = jnp.concatenate([c_prompt, c_sample, jnp.zeros((npad, d), F32)], axis=0)
    new_p, new_s = [], []
    for l in range(depth):
        final = l == depth - 1
        mod = _modulation(c_all, w_ada[l], b_ada[l])
        mod_p = mod[:bp].reshape(bp, 1, 6 * d)
        mod_s = mod[bp:bp + db]
        w_nat, w_t, prow, pcol = _inproj_weights(w_in[l], a_log[l], dt_bias[l])
        wo = w_out[l].astype(BF)
        wfi = w_ffn_in[l].astype(BF)
        wfo = w_ffn_out[l].astype(BF)

        (qkvn, z, vnat, knat, ikT, tail, misc, gct, qT, iqT, iwT, kbf, vT, ikbf) = _inproj_prompt(
            xp, mod_p, g_norm1[l], w_nat, w_t, w_conv[l], prow, pcol, tm_in)
        og, ssm_p = _gdn_prompt(qkvn, z, misc, gct, g_gdn_norm[l], bp, tp, tg)
        oa = _attn_prompt(qT, iqT, iwT, kbf, vT, ikbf, bp, tp)
        tiles_b = tp // tm_post
        spec_p = lambda k: pl.BlockSpec((None, 1, d), lambda i, j: (i // tiles_b, 0, k))
        xp = _post(og, oa, xp.reshape(bp * tp, d), mod_p, spec_p, g_norm2[l], g_final, wo, wfi, wfo,
                   tm_post, tf, final).reshape(bp, tp, d)
        new_p.append((knat.reshape(bp, tp, N_KV, HEAD_DIM), vnat.reshape(bp, tp, N_KV, HEAD_DIM),
                      jnp.swapaxes(ikT, 1, 2), tail[:, 8 - (CONV_W - 1):, :], ssm_p))

        (qkvn_s, z_s, vnat_s, knat_s, slab_s, raw_s, misc_s, aq_s, iq_s) = _inproj_sample(
            xs, mod_s, g_norm1[l], w_nat, w_t, w_conv[l], prow, pcol, state_conv[l], past)
        og_s, ssm_s = _gdn_sample(qkvn_s, z_s, misc_s, g_gdn_norm[l], state_ssm[l], _pick(db, (8,)))
        ik_s = slab_s[:, 0:IDX_DIM]
        iw_s = slab_s[:, IDX_DIM:IDX_DIM + IDX_HEADS]
        scores = _sample_index_scores(iq_s, iw_s, ik_s, jnp.swapaxes(cache_idx_k[l], 1, 2), page_table)
        keep = _sample_select(scores, past + ts)
        oa_s = _sample_attention(aq_s, keep, knat_s, vnat_s, cache_k[l], cache_v[l], page_table)
        spec_s = lambda k: pl.BlockSpec((db, d), lambda i, j: (i, k))
        xs = _post(og_s, oa_s, xs, mod_s, spec_s, g_norm2[l], g_final, wo, wfi, wfo, db, tf, final)
        new_conv_s = jnp.concatenate([state_conv[l][:, 1:, :], raw_s[:, None, :]], axis=1)
        new_s.append((knat_s.reshape(db, ts, N_KV, HEAD_DIM), vnat_s.reshape(db, ts, N_KV, HEAD_DIM),
                      ik_s.reshape(db, ts, IDX_DIM), new_conv_s, ssm_s))

    stack = lambda states, n: jnp.stack([s[n] for s in states], axis=0)
    return (xp, xs.reshape(db, ts, d),
            stack(new_p, 0), stack(new_p, 1), stack(new_p, 2), stack(new_p, 3), stack(new_p, 4),
            stack(new_s, 0), stack(new_s, 1), stack(new_s, 2), stack(new_s, 3), stack(new_s, 4))
```

```python
import functools
import math

import numpy as np
import jax
import jax.numpy as jnp
from jax import lax
from jax.experimental import pallas as pl
from jax.experimental.pallas import tpu as pltpu

F32 = jnp.float32
BF = jnp.bfloat16
I32 = jnp.int32

HEAD_DIM = 128
GDN_HEADS = 4
ATT_HEADS = 4
N_KV = 2
IDX_HEADS = 8
IDX_DIM = 64
CONV_W = 4
TOPK_MAX = 256
QBLOCK = 128
ROPE_THETA = 500000.0
EPS = 1e-6
GDN_W = GDN_HEADS * HEAD_DIM
ATT_W = ATT_HEADS * HEAD_DIM
KV_W = N_KV * HEAD_DIM
IDX_W = IDX_HEADS * IDX_DIM
QKV_W = 3 * GDN_W

GDN_CHUNK = 128
ATT_TK = 256
VT_ROWS = HEAD_DIM + 16
IDX_TK = 512
ATT_BLK = 1024
LANES = 128
NAT_W = QKV_W + GDN_W + KV_W + LANES
T_AQ, T_AK, T_IQ, T_IK, T_MISC, T_AV = 0, 512, 768, 1280, 1344, 1360
T_ROWS = T_AV + KV_W
VMEM_LIMIT = 56 * 1024 * 1024
INT_MIN = np.int32(-2 ** 31)
NEG_INF = float("-inf")


def _params(sem):
    return pltpu.CompilerParams(dimension_semantics=sem, vmem_limit_bytes=VMEM_LIMIT)


def _dot(a, b):
    return jnp.dot(a, b, preferred_element_type=F32)


def _dot_nt(a, b):
    return lax.dot_general(a, b, (((1,), (1,)), ((), ())), preferred_element_type=F32)


def _split3(x):
    hi = x.astype(BF)
    r1 = x - hi.astype(F32)
    mid = r1.astype(BF)
    lo = (r1 - mid.astype(F32)).astype(BF)
    return hi, mid, lo


def _mm_hi(a, b):
    ah = a.astype(BF)
    al = (a - ah.astype(F32)).astype(BF)
    bh = b.astype(BF)
    bl = (b - bh.astype(F32)).astype(BF)
    return _dot(ah, bh) + (_dot(ah, bl) + _dot(al, bh))


def _sigmoid(x):
    return 1.0 / (1.0 + jnp.exp(-x))


def _silu(x):
    return x * _sigmoid(x)


def _softplus(x):
    return jnp.maximum(x, 0.0) + jnp.log(1.0 + jnp.exp(-jnp.abs(x)))


def _rmsnorm(x, g):
    return x * lax.rsqrt(jnp.mean(x * x, axis=-1, keepdims=True) + EPS) * g


def _mod_kernel(c_ref, w_ref, b_ref, o_ref):
    s = _silu(c_ref[...]).astype(BF)
    o_ref[...] = _dot(s, w_ref[...].astype(BF)) + b_ref[...]


def _modulation(c_all, w_ada, b_ada):
    n, d = c_all.shape
    cols = w_ada.shape[1]
    tn = d
    return pl.pallas_call(
        _mod_kernel,
        grid=(cols // tn,),
        in_specs=[pl.BlockSpec((n, d), lambda j: (0, 0)),
                  pl.BlockSpec((d, tn), lambda j: (0, j)),
                  pl.BlockSpec((1, tn), lambda j: (0, j))],
        out_specs=pl.BlockSpec((n, tn), lambda j: (0, j)),
        out_shape=jax.ShapeDtypeStruct((n, cols), F32),
        compiler_params=_params(("arbitrary",)),
        name="modulation",
    )(c_all, w_ada, b_ada.reshape(1, cols))


def _rope_rows(rt_ref, base, half, cos, sin):
    x1 = rt_ref[base:base + half, :]
    x2 = rt_ref[base + half:base + 2 * half, :]
    rt_ref[base:base + half, :] = x1 * cos - x2 * sin
    rt_ref[base + half:base + 2 * half, :] = x2 * cos + x1 * sin


def _project(x_ref, sc_ref, sh_ref, g1_ref, wn_ref, wt_ref, cs128_ref, cs64_ref, pcol_ref, rt_ref):
    h = _rmsnorm(x_ref[...], g1_ref[...]) * (1.0 + sc_ref[...]) + sh_ref[...]
    hb = h.astype(BF)
    nat = _dot(hb, wn_ref[...])
    rt_ref[...] = _dot_nt(wt_ref[...], hb)
    half = HEAD_DIM // 8
    cos, sin = cs128_ref[0:half, :], cs128_ref[half:2 * half, :]
    for hd in range(ATT_HEADS):
        _rope_rows(rt_ref, T_AQ + hd * HEAD_DIM, half, cos, sin)
    for hd in range(N_KV):
        _rope_rows(rt_ref, T_AK + hd * HEAD_DIM, half, cos, sin)
    half = IDX_DIM // 8
    cos, sin = cs64_ref[0:half, :], cs64_ref[half:2 * half, :]
    for hd in range(IDX_HEADS):
        _rope_rows(rt_ref, T_IQ + hd * IDX_DIM, half, cos, sin)
    _rope_rows(rt_ref, T_IK, half, cos, sin)
    mt = rt_ref[T_MISC:T_MISC + 16, :]
    r = lax.broadcasted_iota(I32, mt.shape, 0)
    a_log, dt_b = pcol_ref[:, 0:1], pcol_ref[:, 1:2]
    gate = -jnp.exp(a_log) * _softplus(mt + dt_b)
    mt = jnp.where(r < IDX_HEADS, mt * IDX_HEADS ** -0.5, jnp.where(r < IDX_HEADS + GDN_HEADS, _sigmoid(mt), gate))
    rt_ref[T_MISC:T_MISC + 16, :] = mt
    return nat, mt


def _misc_natural(m, prow_ref):
    lane = lax.broadcasted_iota(I32, m.shape, 1)
    gate = -jnp.exp(prow_ref[0:1, :]) * _softplus(m + prow_ref[1:2, :])
    return jnp.where(lane < GDN_HEADS, _sigmoid(m), jnp.where(lane < 3 * GDN_HEADS, gate, 0.0))


def _qkv_post(conv, qkvn_ref):
    c = _silu(conv)
    for hb in range(2 * GDN_HEADS):
        xh = c[:, hb * HEAD_DIM:(hb + 1) * HEAD_DIM]
        n = xh * lax.rsqrt(jnp.sum(xh * xh, axis=-1, keepdims=True) + EPS)
        if hb < GDN_HEADS:
            n = n * HEAD_DIM ** -0.5
        qkvn_ref[:, hb * HEAD_DIM:(hb + 1) * HEAD_DIM] = n
    qkvn_ref[:, 2 * GDN_W:] = c[:, 2 * GDN_W:]


def _inproj_prompt_kernel(tm, parts, x_ref, sc_ref, sh_ref, g1_ref, wn_ref, wt_ref, wconv_ref, cs128_ref, cs64_ref,
                          prow_ref, pcol_ref, tril_ref,
                          qkvn_ref, z_ref, vnat_ref, knat_ref, ikT_ref, tail_ref, misc_ref, gct_ref,
                          qT_ref, iqT_ref, iwT_ref, kbf_ref, vT_ref, ikbf_ref,
                          xs_ref, rt_ref):
    i = pl.program_id(1)

    @pl.when(i == 0)
    def _():
        xs_ref[0:8, :] = jnp.zeros((8, QKV_W), F32)

    @pl.when(i > 0)
    def _():
        xs_ref[0:8, :] = xs_ref[tm:tm + 8, :]

    th = tm // parts
    for n in range(parts):
        rows, cols = pl.ds(n * th, th), pl.ds(n * th, th)
        _inproj_prompt_rows(
            th, x_ref.at[rows, :], sc_ref, sh_ref, g1_ref, wn_ref, wt_ref, wconv_ref,
            cs128_ref.at[:, cols], cs64_ref.at[:, cols], prow_ref, pcol_ref, tril_ref.at[0:th, 0:th],
            qkvn_ref.at[rows, :], z_ref.at[rows, :], vnat_ref.at[pl.ds(n * th * N_KV, th * N_KV), :],
            knat_ref.at[pl.ds(n * th * N_KV, th * N_KV), :], ikT_ref.at[:, cols], tail_ref, misc_ref.at[rows, :],
            gct_ref.at[pl.ds(n * (th // GDN_CHUNK), th // GDN_CHUNK)],
            qT_ref.at[:, cols], iqT_ref.at[:, cols], iwT_ref.at[:, cols], kbf_ref.at[rows, :],
            vT_ref.at[pl.ds(n * (th // ATT_TK), th // ATT_TK)], ikbf_ref.at[rows, :],
            xs_ref.at[pl.ds(n * th, th + 8), :], rt_ref.at[:, cols])


def _inproj_prompt_rows(tm, x_ref, sc_ref, sh_ref, g1_ref, wn_ref, wt_ref, wconv_ref, cs128_ref, cs64_ref,
                        prow_ref, pcol_ref, tril_ref,
                        qkvn_ref, z_ref, vnat_ref, knat_ref, ikT_ref, tail_ref, misc_ref, gct_ref,
                        qT_ref, iqT_ref, iwT_ref, kbf_ref, vT_ref, ikbf_ref,
                        xs_ref, rt_ref):
    nat, mt = _project(x_ref, sc_ref, sh_ref, g1_ref, wn_ref, wt_ref, cs128_ref, cs64_ref, pcol_ref, rt_ref)
    xs_ref[8:tm + 8, :] = nat[:, 0:QKV_W]
    conv = wconv_ref[0:1, :] * xs_ref[5:tm + 5, :]
    for t in range(1, CONV_W):
        conv = conv + wconv_ref[t:t + 1, :] * xs_ref[5 + t:tm + 5 + t, :]
    tail_ref[...] = xs_ref[tm:tm + 8, :]
    _qkv_post(conv, qkvn_ref)
    z_ref[...] = nat[:, QKV_W:QKV_W + GDN_W]
    v0 = QKV_W + GDN_W
    for hd in range(N_KV):
        vnat_ref[pl.ds(hd, tm, stride=N_KV), :] = nat[:, v0 + hd * HEAD_DIM:v0 + (hd + 1) * HEAD_DIM]

    gm = _misc_natural(nat[:, NAT_W - LANES:], prow_ref)
    tril = tril_ref[...]
    hi, mid, lo = _split3(gm)
    gc = _dot(tril, hi) + (_dot(tril, mid) + _dot(tril, lo))
    lane = lax.broadcasted_iota(I32, gm.shape, 1)
    misc_ref[...] = jnp.where(lane < 2 * GDN_HEADS, gm, gc)
    hi, mid, lo = _split3(mt)
    gct = _dot_nt(hi, tril) + (_dot_nt(mid, tril) + _dot_nt(lo, tril))
    r = lax.broadcasted_iota(I32, mt.shape, 0)
    bg = jnp.where(r < IDX_HEADS + GDN_HEADS, mt, gct)[8:16, :]
    for cc in range(tm // GDN_CHUNK):
        gct_ref[cc] = bg[:, cc * GDN_CHUNK:(cc + 1) * GDN_CHUNK]

    iwT_ref[...] = mt[0:IDX_HEADS, :]
    qT_ref[...] = rt_ref[T_AQ:T_AQ + ATT_W, :].astype(BF)
    iqT_ref[...] = rt_ref[T_IQ:T_IQ + IDX_W, :].astype(BF)
    kn = rt_ref[T_AK:T_AK + KV_W, :].T
    for hd in range(N_KV):
        knat_ref[pl.ds(hd, tm, stride=N_KV), :] = kn[:, hd * HEAD_DIM:(hd + 1) * HEAD_DIM]
    kbf_ref[...] = kn.astype(BF)
    ikT_ref[...] = rt_ref[T_IK:T_IK + IDX_DIM, :]
    ikbf_ref[...] = rt_ref[T_IK:T_IK + LANES, :].T[:, 0:IDX_DIM].astype(BF)
    for cc in range(tm // ATT_TK):
        for hd in range(N_KV):
            lo = hd * VT_ROWS
            vT_ref[cc, lo:lo + HEAD_DIM, :] = rt_ref[T_AV + hd * HEAD_DIM:T_AV + (hd + 1) * HEAD_DIM,
                                                     cc * ATT_TK:(cc + 1) * ATT_TK].astype(BF)
            vT_ref[cc, lo + HEAD_DIM:lo + VT_ROWS, :] = jnp.ones((VT_ROWS - HEAD_DIM, ATT_TK), BF)


def _inproj_sample_kernel(x_ref, sc_ref, sh_ref, g1_ref, wn_ref, wt_ref, wconv_ref, cs128_ref, cs64_ref,
                          prow_ref, pcol_ref, s0_ref, s1_ref, s2_ref,
                          qkvn_ref, z_ref, vnat_ref, knat_ref, slab_ref, raw_ref, misc_ref, aq_ref, iq_ref,
                          rt_ref):
    nat, _ = _project(x_ref, sc_ref, sh_ref, g1_ref, wn_ref, wt_ref, cs128_ref, cs64_ref, pcol_ref, rt_ref)
    raw = nat[:, 0:QKV_W]
    raw_ref[...] = raw
    conv = wconv_ref[0:1, :] * s0_ref[...]
    conv = conv + wconv_ref[1:2, :] * s1_ref[...]
    conv = conv + wconv_ref[2:3, :] * s2_ref[...]
    conv = conv + wconv_ref[3:4, :] * raw
    _qkv_post(conv, qkvn_ref)
    z_ref[...] = nat[:, QKV_W:QKV_W + GDN_W]
    vnat_ref[...] = nat[:, QKV_W + GDN_W:QKV_W + GDN_W + KV_W]
    misc_ref[...] = _misc_natural(nat[:, NAT_W - LANES:], prow_ref)
    aq_ref[...] = rt_ref[T_AQ:T_AQ + ATT_W, :].T
    iq_ref[...] = rt_ref[T_IQ:T_IQ + IDX_W, :].T
    knat_ref[...] = rt_ref[T_AK:T_AK + KV_W, :].T
    slab_ref[...] = rt_ref[T_IK:T_IK + LANES, :].T


def _inproj_weights(w_in, a_log, dt_bias):
    offs = np.cumsum([0, QKV_W, GDN_W, GDN_HEADS, GDN_HEADS, ATT_W, KV_W, KV_W, IDX_W, IDX_DIM, IDX_HEADS])
    qkv, z, beta, a, aq, ak, av, iq, ik, iw = [w_in[:, offs[n]:offs[n + 1]] for n in range(10)]
    d = w_in.shape[0]
    pad = jnp.zeros((d, LANES - 3 * GDN_HEADS), w_in.dtype)
    w_nat = jnp.concatenate([qkv, z, av, beta, a, a, pad], axis=1).astype(BF)
    w_t = jnp.concatenate([aq, ak, iq, ik, iw, beta, a, av], axis=1).T.astype(BF)
    zrow = jnp.zeros((LANES - 3 * GDN_HEADS,), F32)
    prow = jnp.stack([jnp.concatenate([jnp.zeros((GDN_HEADS,), F32), a_log, a_log, zrow]),
                      jnp.concatenate([jnp.zeros((GDN_HEADS,), F32), dt_bias, dt_bias, zrow])])
    z12 = jnp.zeros((IDX_HEADS + GDN_HEADS,), F32)
    pcol = jnp.stack([jnp.concatenate([z12, a_log]), jnp.concatenate([z12, dt_bias])], axis=1)
    return w_nat, w_t, prow, pcol


def _rope_tables(pos, dim):
    half = dim // 8
    inv = ROPE_THETA ** (-jnp.arange(half, dtype=F32) / half)
    ang = pos.astype(F32)[:, None] * inv[None, :]
    return jnp.concatenate([jnp.cos(ang).T, jnp.sin(ang).T], axis=0)


def _inproj_prompt(x, mod, g1, w_nat, w_t, w_conv, prow, pcol, tm):
    bp, tp, d = x.shape
    nt = tp // tm
    rows = bp * tp
    pos = jnp.arange(tp, dtype=I32)
    cs128, cs64 = _rope_tables(pos, HEAD_DIM), _rope_tables(pos, IDX_DIM)
    ri = np.arange(tm)
    tril = jnp.asarray((ri[:, None] // GDN_CHUNK == ri[None, :] // GDN_CHUNK) & (ri[None, :] <= ri[:, None]), BF)
    const = lambda shape: pl.BlockSpec(shape, lambda b, i: (0,) * len(shape))
    rowblk = lambda w: pl.BlockSpec((tm, w), lambda b, i: (b * nt + i, 0))
    colblk = lambda h: pl.BlockSpec((h, tm), lambda b, i: (0, b * nt + i))
    in_specs = [
        pl.BlockSpec((None, tm, d), lambda b, i: (b, i, 0)),
        pl.BlockSpec((None, 1, d), lambda b, i: (b, 0, 1)),
        pl.BlockSpec((None, 1, d), lambda b, i: (b, 0, 0)),
        const((1, d)), const((d, NAT_W)), const((T_ROWS, d)), const((CONV_W, QKV_W)),
        pl.BlockSpec((HEAD_DIM // 4, tm), lambda b, i: (0, i)),
        pl.BlockSpec((IDX_DIM // 4, tm), lambda b, i: (0, i)),
        const((2, LANES)), const((16, 2)), const((tm, tm)),
    ]
    out_shape = [
        jax.ShapeDtypeStruct((rows, QKV_W), F32), jax.ShapeDtypeStruct((rows, GDN_W), F32),
        jax.ShapeDtypeStruct((rows * N_KV, HEAD_DIM), F32), jax.ShapeDtypeStruct((rows * N_KV, HEAD_DIM), F32),
        jax.ShapeDtypeStruct((bp, IDX_DIM, tp), F32), jax.ShapeDtypeStruct((bp, 8, QKV_W), F32),
        jax.ShapeDtypeStruct((rows, LANES), F32), jax.ShapeDtypeStruct((rows // GDN_CHUNK, 8, GDN_CHUNK), F32),
        jax.ShapeDtypeStruct((ATT_W, rows), BF), jax.ShapeDtypeStruct((IDX_W, rows), BF),
        jax.ShapeDtypeStruct((IDX_HEADS, rows), F32), jax.ShapeDtypeStruct((rows, KV_W), BF),
        jax.ShapeDtypeStruct((rows // ATT_TK, N_KV * VT_ROWS, ATT_TK), BF), jax.ShapeDtypeStruct((rows, IDX_DIM), BF),
    ]
    out_specs = [
        rowblk(QKV_W), rowblk(GDN_W),
        pl.BlockSpec((tm * N_KV, HEAD_DIM), lambda b, i: (b * nt + i, 0)),
        pl.BlockSpec((tm * N_KV, HEAD_DIM), lambda b, i: (b * nt + i, 0)),
        pl.BlockSpec((None, IDX_DIM, tm), lambda b, i: (b, 0, i)),
        pl.BlockSpec((None, 8, QKV_W), lambda b, i: (b, 0, 0)),
        rowblk(LANES),
        pl.BlockSpec((tm // GDN_CHUNK, 8, GDN_CHUNK), lambda b, i: (b * nt + i, 0, 0)),
        colblk(ATT_W), colblk(IDX_W), colblk(IDX_HEADS), rowblk(KV_W),
        pl.BlockSpec((tm // ATT_TK, N_KV * VT_ROWS, ATT_TK), lambda b, i: (b * nt + i, 0, 0)),
        rowblk(IDX_DIM),
    ]
    return pl.pallas_call(
        functools.partial(_inproj_prompt_kernel, tm, tm // _pick(tm, (ATT_TK,))),
        grid=(bp, nt), in_specs=in_specs, out_specs=out_specs, out_shape=out_shape,
        scratch_shapes=[pltpu.VMEM((tm + 8, QKV_W), F32), pltpu.VMEM((T_ROWS, tm), F32)],
        compiler_params=_params(("arbitrary", "arbitrary")),
        name="inproj_prompt",
    )(x, mod, mod, g1.reshape(1, d), w_nat, w_t, w_conv, cs128, cs64, prow, pcol, tril)


def _inproj_sample(x, mod, g1, w_nat, w_t, w_conv, prow, pcol, conv_state, past):
    db, d = x.shape
    pos = jnp.full((db,), past, I32)
    cs128, cs64 = _rope_tables(pos, HEAD_DIM), _rope_tables(pos, IDX_DIM)
    full = lambda shape: pl.BlockSpec(shape, lambda i: (0,) * len(shape))
    in_specs = [
        full((db, d)),
        pl.BlockSpec((db, d), lambda i: (0, 1)), pl.BlockSpec((db, d), lambda i: (0, 0)),
        full((1, d)), full((d, NAT_W)), full((T_ROWS, d)), full((CONV_W, QKV_W)),
        full((HEAD_DIM // 4, db)), full((IDX_DIM // 4, db)), full((2, LANES)), full((16, 2)),
        full((db, QKV_W)), full((db, QKV_W)), full((db, QKV_W)),
    ]
    widths = [QKV_W, GDN_W, KV_W, KV_W, LANES, QKV_W, LANES, ATT_W, IDX_W]
    return pl.pallas_call(
        _inproj_sample_kernel,
        grid=(1,), in_specs=in_specs,
        out_specs=[full((db, w)) for w in widths],
        out_shape=[jax.ShapeDtypeStruct((db, w), F32) for w in widths],
        scratch_shapes=[pltpu.VMEM((T_ROWS, db), F32)],
        compiler_params=_params(("arbitrary",)),
        name="inproj_sample",
    )(x, mod, mod, g1.reshape(1, d), w_nat, w_t, w_conv, cs128, cs64, prow, pcol,
      conv_state[:, 0], conv_state[:, 1], conv_state[:, 2])


def _unit_lower_inverses(lows):
    c = lows[0].shape[0]
    ii = lax.broadcasted_iota(I32, (c, c), 0)
    jj = lax.broadcasted_iota(I32, (c, c), 1)
    eye = jnp.where(ii == jj, 1.0, 0.0)
    levels = int(math.log2(c)) - 1
    ts = [eye - low for low in lows]
    ps = [_mm_hi(low, low) for low in lows]
    for lvl in range(levels):
        if lvl == levels - 1:
            ts = [t + _mm_hi(t, p) for t, p in zip(ts, ps)]
        else:
            both = [_mm_hi(jnp.concatenate([t, p], axis=0), p) for t, p in zip(ts, ps)]
            ts = [t + b[0:c] for t, b in zip(ts, both)]
            ps = [b[c:] for b in both]
    return ts


def _gdn_prompt_kernel(tg, qkvn_ref, z_ref, misc_ref, gct_ref, gn_ref, o_ref, ssm_ref, s_ref):
    i = pl.program_id(1)

    @pl.when(i == 0)
    def _():
        s_ref[...] = jnp.zeros(s_ref.shape, F32)

    c = GDN_CHUNK
    ii = lax.broadcasted_iota(I32, (c, c), 0)
    jj = lax.broadcasted_iota(I32, (c, c), 1)
    pairs = [(cc, hd) for cc in range(tg // c) for hd in range(GDN_HEADS)]

    qs, ks, gcs, rhs, lows, intras = {}, {}, {}, {}, [], {}
    for cc, hd in pairs:
        r0, lo = cc * c, hd * HEAD_DIM
        q = qkvn_ref[r0:r0 + c, lo:lo + HEAD_DIM]
        k = qkvn_ref[r0:r0 + c, GDN_W + lo:GDN_W + lo + HEAD_DIM]
        v = qkvn_ref[r0:r0 + c, 2 * GDN_W + lo:2 * GDN_W + lo + HEAD_DIM]
        beta_c = misc_ref[r0:r0 + c, hd:hd + 1]
        gc_c = misc_ref[r0:r0 + c, 2 * GDN_HEADS + hd:2 * GDN_HEADS + hd + 1]
        gc_r = gct_ref[cc, GDN_HEADS + hd:GDN_HEADS + hd + 1, :]
        decay = jnp.where(ii >= jj, jnp.exp(jnp.where(ii >= jj, gc_c - gc_r, 0.0)), 0.0)
        kb = k * beta_c
        both = _dot_nt(jnp.concatenate([kb, q], axis=0).astype(BF), k.astype(BF))
        lows.append(jnp.where(ii > jj, both[0:c] * decay, 0.0))
        intras[cc, hd] = (both[c:] * decay).astype(BF)
        rhs[cc, hd] = jnp.concatenate([v * beta_c, kb * jnp.exp(gc_c)], axis=1).astype(BF)
        qs[cc, hd], ks[cc, hd], gcs[cc, hd] = q, k, gc_c
    ts = _unit_lower_inverses(lows)
    uws = {p: _dot(t.astype(BF), rhs[p]) for p, t in zip(pairs, ts)}

    for cc, hd in pairs:
        r0, lo = cc * c, hd * HEAD_DIM
        q, k, gc_c, uw = qs[cc, hd], ks[cc, hd], gcs[cc, hd], uws[cc, hd]
        s = s_ref[hd]
        ws = _dot(jnp.concatenate([uw[:, HEAD_DIM:], q * jnp.exp(gc_c)], axis=0).astype(BF), s.astype(BF))
        v_new = (uw[:, 0:HEAD_DIM] - ws[0:c]).astype(BF)
        o = ws[c:] + _dot(intras[cc, hd], v_new)
        g_last = gc_c[c - 1:c, :]
        kd = k * jnp.exp(g_last - gc_c)
        s_ref[hd] = s * jnp.exp(g_last) + _dot(kd.T.astype(BF), v_new)
        o = _rmsnorm(o, gn_ref[...]) * _silu(z_ref[r0:r0 + c, lo:lo + HEAD_DIM])
        o_ref[r0:r0 + c, lo:lo + HEAD_DIM] = o.astype(BF)
    ssm_ref[...] = s_ref[...]


def _gdn_prompt(qkvn, z, misc, gct, g_norm, bp, tp, tg):
    nt = tp // tg
    rows = bp * tp
    rowblk = lambda w: pl.BlockSpec((tg, w), lambda b, i: (b * nt + i, 0))
    return pl.pallas_call(
        functools.partial(_gdn_prompt_kernel, tg),
        grid=(bp, nt),
        in_specs=[rowblk(QKV_W), rowblk(GDN_W), rowblk(LANES),
                  pl.BlockSpec((tg // GDN_CHUNK, 8, GDN_CHUNK), lambda b, i: (b * nt + i, 0, 0)),
                  pl.BlockSpec((1, HEAD_DIM), lambda b, i: (0, 0))],
        out_specs=[rowblk(GDN_W),
                   pl.BlockSpec((None, GDN_HEADS, HEAD_DIM, HEAD_DIM), lambda b, i: (b, 0, 0, 0))],
        out_shape=[jax.ShapeDtypeStruct((rows, GDN_W), BF),
                   jax.ShapeDtypeStruct((bp, GDN_HEADS, HEAD_DIM, HEAD_DIM), F32)],
        scratch_shapes=[pltpu.VMEM((GDN_HEADS, HEAD_DIM, HEAD_DIM), F32)],
        compiler_params=_params(("arbitrary", "arbitrary")),
        name="gdn_prompt",
    )(qkvn, z, misc, gct, g_norm.reshape(1, HEAD_DIM))


def _gdn_sample_kernel(nb, qkvn_ref, z_ref, misc_ref, gn_ref, s_ref, o_ref, ssm_ref):
    for bi in range(nb):
        for hd in range(GDN_HEADS):
            lo = hd * HEAD_DIM
            q = qkvn_ref[bi:bi + 1, lo:lo + HEAD_DIM]
            k = qkvn_ref[bi:bi + 1, GDN_W + lo:GDN_W + lo + HEAD_DIM]
            v = qkvn_ref[bi:bi + 1, 2 * GDN_W + lo:2 * GDN_W + lo + HEAD_DIM]
            beta = misc_ref[bi:bi + 1, hd:hd + 1]
            g = misc_ref[bi:bi + 1, GDN_HEADS + hd:GDN_HEADS + hd + 1]
            kcol = jnp.broadcast_to(k, (HEAD_DIM, HEAD_DIM)).T
            qcol = jnp.broadcast_to(q, (HEAD_DIM, HEAD_DIM)).T
            s = s_ref[bi, hd] * jnp.exp(g)
            ks = jnp.sum(kcol * s, axis=0, keepdims=True)
            delta = (v - ks) * beta
            s = s + kcol * delta
            ssm_ref[bi, hd] = s
            o = jnp.sum(qcol * s, axis=0, keepdims=True)
            o = _rmsnorm(o, gn_ref[...]) * _silu(z_ref[bi:bi + 1, lo:lo + HEAD_DIM])
            o_ref[bi:bi + 1, lo:lo + HEAD_DIM] = o.astype(BF)


def _gdn_sample(qkvn, z, misc, g_norm, state, nb):
    db = qkvn.shape[0]
    rowblk = lambda w: pl.BlockSpec((nb, w), lambda i: (i, 0))
    sblk = pl.BlockSpec((nb, GDN_HEADS, HEAD_DIM, HEAD_DIM), lambda i: (i, 0, 0, 0))
    return pl.pallas_call(
        functools.partial(_gdn_sample_kernel, nb),
        grid=(db // nb,),
        in_specs=[rowblk(QKV_W), rowblk(GDN_W), rowblk(LANES), pl.BlockSpec((1, HEAD_DIM), lambda i: (0, 0)), sblk],
        out_specs=[rowblk(GDN_W), sblk],
        out_shape=[jax.ShapeDtypeStruct((db, GDN_W), BF), jax.ShapeDtypeStruct(state.shape, F32)],
        compiler_params=_params(("arbitrary",)),
        name="gdn_sample",
    )(qkvn, z, misc, g_norm.reshape(1, HEAD_DIM), state)


def _ordered_word_to_float(u):
    s = u ^ INT_MIN
    return lax.bitcast_convert_type(s ^ ((s >> 31) & np.int32(0x7FFFFFFF)), F32)


def _select_topk_bias(i_ref, t_ref, nch, tk, k_sel, idx_bits, taken=0.0, dropped=NEG_INF):
    nl = i_ref.shape[1]
    row = lax.broadcasted_iota(I32, (tk, 1), 0)

    def chunk(c):
        return pl.ds(pl.multiple_of(c * tk, tk), tk)

    def count(ref, pred):
        def body(c, acc):
            m = pred(ref[chunk(c), :])
            return acc + jnp.sum(jnp.where(m, 1, 0).astype(I32).reshape(tk // 8, 8, nl), axis=0)
        acc = lax.fori_loop(0, nch, body, jnp.zeros((8, nl), I32))
        return jnp.sum(acc, axis=0, keepdims=True)

    def bit_body(i, carry):
        t_u, n_ge = carry
        cand_u = t_u | jnp.left_shift(jnp.int32(1), 31 - i)
        cand = _ordered_word_to_float(cand_u)
        cnt = count(i_ref, lambda sc: sc >= cand)
        take = cnt >= k_sel
        return jnp.where(take, cand_u, t_u), jnp.where(take, cnt, n_ge)

    t_u, n_ge = lax.fori_loop(0, 32, bit_body, (jnp.zeros((1, nl), I32), jnp.full((1, nl), -1, I32)))
    thr = jnp.where((t_u >> 23) == 0, NEG_INF, _ordered_word_to_float(t_u))

    def tie_break(_):
        need = k_sel - count(i_ref, lambda sc: sc > thr)

        def mark(c, carry):
            t_ref[chunk(c), :] = jnp.where(i_ref[chunk(c), :] == thr, c * tk + row, 2 ** idx_bits)
            return carry

        lax.fori_loop(0, nch, mark, 0)

        def idx_body(i, x):
            cand = x | jnp.left_shift(jnp.int32(1), idx_bits - 1 - i)
            cnt = count(t_ref, lambda idx: idx < cand)
            return jnp.where(cnt < need, cand, x)

        return lax.fori_loop(0, idx_bits, idx_body, jnp.zeros((1, nl), I32))

    x = lax.cond(jnp.max(n_ge) > k_sel, tie_break, lambda _: jnp.full((1, nl), 2 ** idx_bits - 1, I32), 0)

    def write(c, carry):
        sc = i_ref[chunk(c), :]
        sel = (sc > thr) | ((sc == thr) & (c * tk + row <= x))
        i_ref[chunk(c), :] = jnp.where(sel & (sc > NEG_INF), taken, dropped)
        return carry

    lax.fori_loop(0, nch, write, 0)


def _attn_prompt_kernel(tq, tk, blk, k_sel, idx_bits, qT_ref, iqT_ref, iwT_ref, kbf_ref, vT_ref, ikbf_ref, o_ref,
                        i_ref, t_ref):
    t0 = pl.program_id(1) * tq
    nch = (t0 + tq + tk - 1) // tk
    nblk = (t0 + tq + blk - 1) // blk
    qpos = t0 + lax.broadcasted_iota(I32, (1, tq), 1)
    row = lax.broadcasted_iota(I32, (tk, 1), 0)
    w = iwT_ref[...]
    pairs = IDX_HEADS // 2
    rhs = [jnp.concatenate([iqT_ref[(2 * p) * IDX_DIM:(2 * p + 1) * IDX_DIM, :],
                            iqT_ref[(2 * p + 1) * IDX_DIM:(2 * p + 2) * IDX_DIM, :]], axis=1) for p in range(pairs)]

    def chunk(c):
        return pl.ds(pl.multiple_of(c * tk, tk), tk)

    def idx_body(c, carry):
        ikc = ikbf_ref[chunk(c), :]
        acc = jnp.zeros((tk, tq), F32)
        for p in range(pairs):
            d = jnp.maximum(_dot(ikc, rhs[p]), 0.0)
            acc = acc + d[:, 0:tq] * w[2 * p:2 * p + 1, :] + d[:, tq:] * w[2 * p + 1:2 * p + 2, :]
        i_ref[chunk(c), :] = jnp.where(c * tk + row <= qpos, acc, NEG_INF)
        return carry

    lax.fori_loop(0, nch, idx_body, 0)

    def fill_body(c, carry):
        i_ref[chunk(c), :] = jnp.full((tk, tq), NEG_INF, F32)
        return carry

    lax.fori_loop(nch, nblk * (blk // tk), fill_body, 0)
    _select_topk_bias(i_ref, t_ref, nch, tk, k_sel, idx_bits)

    scale2 = HEAD_DIM ** -0.5 * math.log2(math.e)
    rep = ATT_HEADS // N_KV
    sub = ATT_TK
    nsub = blk // sub
    qg = [jnp.concatenate([qT_ref[(rep * g + r) * HEAD_DIM:(rep * g + r + 1) * HEAD_DIM, :] for r in range(rep)],
                          axis=1) for g in range(N_KV)]

    def body(c, carry):
        out = []
        for g in range(N_KV):
            m, acc = carry[g]
            ss = []
            for i in range(nsub):
                rows = pl.ds(pl.multiple_of(c * blk + i * sub, sub), sub)
                bias = i_ref[rows, :]
                ss.append(_dot(kbf_ref[rows, g * HEAD_DIM:(g + 1) * HEAD_DIM], qg[g]) * scale2
                          + jnp.concatenate([bias] * rep, axis=1))
            m_new = m
            for s in ss:
                m_new = jnp.maximum(m_new, jnp.max(s, axis=0, keepdims=True))
            m_safe = jnp.where(m_new == NEG_INF, 0.0, m_new)
            acc = acc * jnp.exp2(m - m_safe)
            for i, s in enumerate(ss):
                p = jnp.exp2(s - m_safe).astype(BF)
                acc = acc + _dot(vT_ref[c * nsub + i, g * VT_ROWS:(g + 1) * VT_ROWS, :], p)
            out.append((m_new, acc))
        return tuple(out)

    init = tuple((jnp.full((1, rep * tq), NEG_INF, F32), jnp.zeros((VT_ROWS, rep * tq), F32)) for _ in range(N_KV))
    res = lax.fori_loop(0, nblk, body, init)
    for g in range(N_KV):
        acc = res[g][1]
        o_t = acc[0:HEAD_DIM] / acc[HEAD_DIM:HEAD_DIM + 1]
        for r in range(rep):
            hd = rep * g + r
            o_ref[:, hd * HEAD_DIM:(hd + 1) * HEAD_DIM] = o_t[:, r * tq:(r + 1) * tq].T.astype(BF)


def _attn_prompt(qT, iqT, iwT, kbf, vT, ikbf, bp, tp):
    tq = QBLOCK
    tk = _pick(tp, (IDX_TK, ATT_TK))
    blk = _pick(tp, (ATT_BLK, IDX_TK, ATT_TK))
    assert blk % tk == 0 and blk % ATT_TK == 0
    nq = tp // tq
    rows = bp * tp
    k_sel = min(TOPK_MAX, tp // 4)
    idx_bits = max(1, int(tp - 1).bit_length())
    colblk = lambda h: pl.BlockSpec((h, tq), lambda b, j: (0, b * nq + j))
    return pl.pallas_call(
        functools.partial(_attn_prompt_kernel, tq, tk, blk, k_sel, idx_bits),
        grid=(bp, nq),
        in_specs=[colblk(ATT_W), colblk(IDX_W), colblk(IDX_HEADS),
                  pl.BlockSpec((tp, KV_W), lambda b, j: (b, 0)),
                  pl.BlockSpec((tp // ATT_TK, N_KV * VT_ROWS, ATT_TK), lambda b, j: (b, 0, 0)),
                  pl.BlockSpec((tp, IDX_DIM), lambda b, j: (b, 0))],
        out_specs=pl.BlockSpec((tq, ATT_W), lambda b, j: (b * nq + j, 0)),
        out_shape=jax.ShapeDtypeStruct((rows, ATT_W), BF),
        scratch_shapes=[pltpu.VMEM((tp, tq), F32), pltpu.VMEM((tp, tq), I32)],
        compiler_params=_params(("arbitrary", "arbitrary")),
        name="attn_prompt",
    )(qT, iqT, iwT, kbf, vT, ikbf)


def _sidx_kernel(npg, pt_ref, iq_ref, iw_ref, ikn_ref, *refs):
    pages, out_ref = refs[:npg], refs[npg]
    iq = iq_ref[...].astype(BF)
    w = iw_ref[...]
    for p in range(npg):
        d = jnp.maximum(_dot(iq, pages[p][...].astype(BF)), 0.0)
        out_ref[:, p * LANES:(p + 1) * LANES] = jnp.sum(d * w, axis=0, keepdims=True)
    dn = jnp.sum(iq.astype(F32) * ikn_ref[...].astype(BF).astype(F32), axis=1, keepdims=True)
    sn = jnp.sum(jnp.maximum(dn, 0.0) * w, axis=0, keepdims=True)
    lane = lax.broadcasted_iota(I32, (1, LANES), 1)
    out_ref[:, npg * LANES:(npg + 1) * LANES] = jnp.where(lane == 0, sn, NEG_INF)


def _sample_index_scores(iq, iw, ik_new, cache_ik_t, page_table):
    db, npg = page_table.shape
    page = cache_ik_t.shape[2]
    assert page == LANES
    width = (npg + 1) * LANES
    page_specs = [pl.BlockSpec((None, IDX_DIM, page), functools.partial(lambda p, b, pt: (pt[b, p], 0, 0), p))
                  for p in range(npg)]
    grid_spec = pltpu.PrefetchScalarGridSpec(
        num_scalar_prefetch=1, grid=(db,),
        in_specs=[pl.BlockSpec((None, IDX_HEADS, IDX_DIM), lambda b, pt: (b, 0, 0)),
                  pl.BlockSpec((None, IDX_HEADS, 1), lambda b, pt: (b, 0, 0)),
                  pl.BlockSpec((None, 1, IDX_DIM), lambda b, pt: (b, 0, 0))] + page_specs,
        out_specs=pl.BlockSpec((None, 1, width), lambda b, pt: (b, 0, 0)))
    out = pl.pallas_call(
        functools.partial(_sidx_kernel, npg), grid_spec=grid_spec,
        out_shape=jax.ShapeDtypeStruct((db, 1, width), F32),
        compiler_params=_params(("arbitrary",)),
        name="sample_index_scores",
    )(page_table, iq.reshape(db, IDX_HEADS, IDX_DIM), iw.reshape(db, IDX_HEADS, 1), ik_new.reshape(db, 1, IDX_DIM),
      *([cache_ik_t] * npg))
    return out.reshape(db, width)


def _ssel_kernel(nch, k_sel, idx_bits, s_ref, spread_ref, keep_ref, i_ref, t_ref):
    for p in range(nch):
        i_ref[p * LANES:(p + 1) * LANES, :] = s_ref[:, p * LANES:(p + 1) * LANES].T
    _select_topk_bias(i_ref, t_ref, nch, LANES, k_sel, idx_bits, taken=1.0, dropped=0.0)
    rows = N_KV * LANES
    for p in range(nch - 1):
        k2 = _dot(spread_ref[...], i_ref[p * LANES:(p + 1) * LANES, :].astype(BF))
        for h in range(N_KV):
            keep_ref[:, p * rows + h * LANES:p * rows + (h + 1) * LANES] = k2[h * LANES:(h + 1) * LANES, :].T
    keep_ref[:, (nch - 1) * rows:(nch - 1) * rows + LANES] = i_ref[(nch - 1) * LANES:nch * LANES, :].T


def _sample_select(scores, n_keys):
    db, width = scores.shape
    assert db == LANES
    nch = width // LANES
    k_sel = min(TOPK_MAX, n_keys // 4)
    idx_bits = max(1, int(width - 1).bit_length())
    rows = N_KV * LANES
    out_w = (nch - 1) * rows + LANES
    spread = jnp.asarray(np.arange(rows)[:, None] // N_KV == np.arange(LANES)[None, :], BF)
    return pl.pallas_call(
        functools.partial(_ssel_kernel, nch, k_sel, idx_bits),
        grid=(1,),
        in_specs=[pl.BlockSpec((db, width), lambda i: (0, 0)), pl.BlockSpec((rows, LANES), lambda i: (0, 0))],
        out_specs=pl.BlockSpec((db, out_w), lambda i: (0, 0)),
        out_shape=jax.ShapeDtypeStruct((db, out_w), F32),
        scratch_shapes=[pltpu.VMEM((width, db), F32), pltpu.VMEM((width, db), I32)],
        compiler_params=_params(("arbitrary",)),
        name="sample_select",
    )(scores, spread)


def _sattn_kernel(npg, pt_ref, q_ref, keep_ref, kn_ref, vn_ref, *refs):
    kpages, vpages, o_ref = refs[:npg], refs[npg:2 * npg], refs[2 * npg]
    rep = ATT_HEADS // N_KV
    rows = N_KV * LANES
    scale = HEAD_DIM ** -0.5
    q = q_ref[...]
    q8 = jnp.concatenate([q, jnp.zeros((8 - ATT_HEADS, HEAD_DIM), F32)], axis=0).astype(BF)
    hrow = lax.broadcasted_iota(I32, (8, 1), 0)
    lane = lax.broadcasted_iota(I32, (1, LANES), 1)
    own_kv = lax.broadcasted_iota(I32, (1, rows), 1) % N_KV == hrow // rep
    parts = []
    for p in range(npg):
        s2 = _dot_nt(q8, kpages[p][...].astype(BF))
        keep2 = keep_ref[:, p * rows:(p + 1) * rows]
        parts.append(jnp.where(own_kv & (keep2 > 0.5), s2 * scale, NEG_INF))
    kn = kn_ref[...].astype(BF).astype(F32)
    kn8 = jnp.where(hrow // rep == 0, kn[0:1, :], kn[1:2, :])
    s_new = jnp.sum(q8.astype(F32) * kn8, axis=1, keepdims=True)
    keep_new = keep_ref[:, npg * rows:npg * rows + 1]
    parts.append(jnp.where((lane == 0) & (keep_new > 0.5), s_new * scale, NEG_INF))
    s = jnp.concatenate(parts, axis=1)
    m = jnp.max(s, axis=1, keepdims=True)
    e = jnp.exp(s - m)
    pr = (e / jnp.sum(e, axis=1, keepdims=True)).astype(BF)
    o8 = jnp.zeros((8, HEAD_DIM), F32)
    for p in range(npg):
        o8 = o8 + _dot(pr[:, p * rows:(p + 1) * rows], vpages[p][...].astype(BF))
    vn = vn_ref[...].astype(BF).astype(F32)
    vn8 = jnp.where(hrow // rep == 0, vn[0:1, :], vn[1:2, :])
    o8 = o8 + pr[:, npg * rows:npg * rows + 1].astype(F32) * vn8
    for hd in range(ATT_HEADS):
        o_ref[:, hd * HEAD_DIM:(hd + 1) * HEAD_DIM] = o8[hd:hd + 1, :].astype(BF)


def _sample_attention(q, keep, k_new, v_new, cache_k, cache_v, page_table):
    db, npg = page_table.shape
    n_phys, page = cache_k.shape[0], cache_k.shape[1]
    assert page == LANES and N_KV == 2
    rows = page * N_KV
    width = npg * rows + LANES
    ck = cache_k.reshape(n_phys, rows, HEAD_DIM)
    cv = cache_v.reshape(n_phys, rows, HEAD_DIM)
    page_specs = [pl.BlockSpec((None, rows, HEAD_DIM), functools.partial(lambda p, b, pt: (pt[b, p], 0, 0), p))
                  for p in range(npg)]
    grid_spec = pltpu.PrefetchScalarGridSpec(
        num_scalar_prefetch=1, grid=(db,),
        in_specs=[pl.BlockSpec((None, ATT_HEADS, HEAD_DIM), lambda b, pt: (b, 0, 0)),
                  pl.BlockSpec((None, 1, width), lambda b, pt: (b, 0, 0)),
                  pl.BlockSpec((None, N_KV, HEAD_DIM), lambda b, pt: (b, 0, 0)),
                  pl.BlockSpec((None, N_KV, HEAD_DIM), lambda b, pt: (b, 0, 0))] + page_specs + page_specs,
        out_specs=pl.BlockSpec((None, 1, ATT_W), lambda b, pt: (b, 0, 0)))
    out = pl.pallas_call(
        functools.partial(_sattn_kernel, npg), grid_spec=grid_spec,
        out_shape=jax.ShapeDtypeStruct((db, 1, ATT_W), BF),
        compiler_params=_params(("arbitrary",)),
        name="sample_attention",
    )(page_table, q.reshape(db, ATT_HEADS, HEAD_DIM), keep.reshape(db, 1, width),
      k_new.reshape(db, N_KV, HEAD_DIM), v_new.reshape(db, N_KV, HEAD_DIM), *([ck] * npg), *([cv] * npg))
    return out.reshape(db, ATT_W)


def _post_kernel(final, nff, parts, og_ref, oa_ref, x_ref, ga1_ref, sc2_ref, sh2_ref, ga2_ref, g2_ref, gf_ref,
                 wo_ref, wg_ref, wu_ref, wd_ref, y_ref, x1_ref, h2_ref, acc_ref):
    jf = pl.program_id(1)
    th = x_ref.shape[0] // parts
    groups = [pl.ds(n * th, th) for n in range(parts)]

    def mod(ref, r):
        return ref[...] if ref.shape[0] == 1 else ref[r, :]

    @pl.when(jf == 0)
    def _():
        for r in groups:
            mixed = _dot(og_ref[r, :], wo_ref[0:GDN_W, :]) + _dot(oa_ref[r, :], wo_ref[GDN_W:, :])
            x1 = x_ref[r, :] + mod(ga1_ref, r) * mixed
            x1_ref[r, :] = x1
            h2_ref[r, :] = (_rmsnorm(x1, g2_ref[...]) * (1.0 + mod(sc2_ref, r)) + mod(sh2_ref, r)).astype(BF)
            acc_ref[r, :] = jnp.zeros((th, acc_ref.shape[1]), F32)

    for r in groups:
        h2 = h2_ref[r, :]
        act = (_silu(_dot(h2, wg_ref[...])) * _dot(h2, wu_ref[...])).astype(BF)
        acc_ref[r, :] += _dot(act, wd_ref[...])

    @pl.when(jf == nff - 1)
    def _():
        for r in groups:
            x2 = x1_ref[r, :] + mod(ga2_ref, r) * acc_ref[r, :]
            y_ref[r, :] = _rmsnorm(x2, gf_ref[...]) if final else x2


def _post(og, oa, x2d, mod, mod_spec, g2, gf, wo, wfi, wfo, tm, tf, final):
    rows, d = x2d.shape
    dff = wfo.shape[0]
    nff = dff // tf
    rowblk = lambda w: pl.BlockSpec((tm, w), lambda i, j: (i, 0))
    const = lambda shape: pl.BlockSpec(shape, lambda i, j: (0,) * len(shape))
    in_specs = [rowblk(GDN_W), rowblk(ATT_W), rowblk(d),
                mod_spec(2), mod_spec(4), mod_spec(3), mod_spec(5),
                const((1, d)), const((1, d)), const((d, d)),
                pl.BlockSpec((d, tf), lambda i, j: (0, j)),
                pl.BlockSpec((d, tf), lambda i, j: (0, nff + j)),
                pl.BlockSpec((tf, d), lambda i, j: (j, 0))]
    return pl.pallas_call(
        functools.partial(_post_kernel, final, nff, tm // _pick(tm, (256,))),
        grid=(rows // tm, nff), in_specs=in_specs, out_specs=rowblk(d),
        out_shape=jax.ShapeDtypeStruct((rows, d), F32),
        scratch_shapes=[pltpu.VMEM((tm, d), F32), pltpu.VMEM((tm, d), BF), pltpu.VMEM((tm, d), F32)],
        compiler_params=_params(("arbitrary", "arbitrary")),
        name="post",
    )(og, oa, x2d, mod, mod, mod, mod, g2.reshape(1, d), gf.reshape(1, d), wo, wfi, wfi, wfo)


def _pick(n, prefs):
    for p in prefs:
        if n % p == 0:
            return p
    return n


def kernel(x_prompt, x_sample, c_prompt, c_sample, cache_k, cache_v, cache_idx_k, page_table, state_conv, state_ssm,
           w_ada, b_ada, g_norm1, w_in, w_conv, a_log, dt_bias, g_gdn_norm, w_out, g_norm2, w_ffn_in, w_ffn_out,
           g_final):
    bp, tp, d = x_prompt.shape
    db, ts, _ = x_sample.shape
    assert ts == 1 and d == GDN_W + ATT_W and tp % max(ATT_TK, GDN_CHUNK) == 0
    depth = w_in.shape[0]
    npg, page = page_table.shape[1], cache_k.shape[2]
    past = npg * page
    tm_in = _pick(tp, (512, 256))
    tg = _pick(tp, (256,))
    tm_post = _pick(tp, (1024, 512, 256))
    dff = w_ffn_out.shape[1]
    tf = _pick(dff, (256, 128))

    xp = x_prompt
    xs = x_sample.reshape(db, d)
    npad = (-(bp + db)) % 8
    c_all = jnp.concatenate([c_prompt, c_sample, jnp.zeros((npad, d), F32)], axis=0)
    new_p, new_s = [], []
    for l in range(depth):
        final = l == depth - 1
        mod = _modulation(c_all, w_ada[l], b_ada[l])
        mod_p = mod[:bp].reshape(bp, 1, 6 * d)
        mod_s = mod[bp:bp + db]
        w_nat, w_t, prow, pcol = _inproj_weights(w_in[l], a_log[l], dt_bias[l])
        wo = w_out[l].astype(BF)
        wfi = w_ffn_in[l].astype(BF)
        wfo = w_ffn_out[l].astype(BF)

        (qkvn, z, vnat, knat, ikT, tail, misc, gct, qT, iqT, iwT, kbf, vT, ikbf) = _inproj_prompt(
            xp, mod_p, g_norm1[l], w_nat, w_t, w_conv[l], prow, pcol, tm_in)
        og, ssm_p = _gdn_prompt(qkvn, z, misc, gct, g_gdn_norm[l], bp, tp, tg)
        oa = _attn_prompt(qT, iqT, iwT, kbf, vT, ikbf, bp, tp)
        tiles_b = tp // tm_post
        spec_p = lambda k: pl.BlockSpec((None, 1, d), lambda i, j: (i // tiles_b, 0, k))
        xp = _post(og, oa, xp.reshape(bp * tp, d), mod_p, spec_p, g_norm2[l], g_final, wo, wfi, wfo,
                   tm_post, tf, final).reshape(bp, tp, d)
        new_p.append((knat.reshape(bp, tp, N_KV, HEAD_DIM), vnat.reshape(bp, tp, N_KV, HEAD_DIM),
                      jnp.swapaxes(ikT, 1, 2), tail[:, 8 - (CONV_W - 1):, :], ssm_p))

        (qkvn_s, z_s, vnat_s, knat_s, slab_s, raw_s, misc_s, aq_s, iq_s) = _inproj_sample(
            xs, mod_s, g_norm1[l], w_nat, w_t, w_conv[l], prow, pcol, state_conv[l], past)
        og_s, ssm_s = _gdn_sample(qkvn_s, z_s, misc_s, g_gdn_norm[l], state_ssm[l], _pick(db, (8,)))
        ik_s = slab_s[:, 0:IDX_DIM]
        iw_s = slab_s[:, IDX_DIM:IDX_DIM + IDX_HEADS]
        scores = _sample_index_scores(iq_s, iw_s, ik_s, jnp.swapaxes(cache_idx_k[l], 1, 2), page_table)
        keep = _sample_select(scores, past + ts)
        oa_s = _sample_attention(aq_s, keep, knat_s, vnat_s, cache_k[l], cache_v[l], page_table)
        spec_s = lambda k: pl.BlockSpec((db, d), lambda i, j: (i, k))
        xs = _post(og_s, oa_s, xs, mod_s, spec_s, g_norm2[l], g_final, wo, wfi, wfo, db, tf, final)
        new_conv_s = jnp.concatenate([state_conv[l][:, 1:, :], raw_s[:, None, :]], axis=1)
        new_s.append((knat_s.reshape(db, ts, N_KV, HEAD_DIM), vnat_s.reshape(db, ts, N_KV, HEAD_DIM),
                      ik_s.reshape(db, ts, IDX_DIM), new_conv_s, ssm_s))

    stack = lambda states, n: jnp.stack([s[n] for s in states], axis=0)
    return (xp, xs.reshape(db, ts, d),
            stack(new_p, 0), stack(new_p, 1), stack(new_p, 2), stack(new_p, 3), stack(new_p, 4),
            stack(new_s, 0), stack(new_s, 1), stack(new_s, 2), stack(new_s, 3), stack(new_s, 4))
```

```python
import functools
import math

import numpy as np
import jax
import jax.numpy as jnp
from jax import lax
from jax.experimental import pallas as pl
from jax.experimental.pallas import tpu as pltpu

F32 = jnp.float32
BF = jnp.bfloat16
I32 = jnp.int32

HEAD_DIM = 128
GDN_HEADS = 4
ATT_HEADS = 4
N_KV = 2
IDX_HEADS = 8
IDX_DIM = 64
CONV_W = 4
TOPK_MAX = 256
QBLOCK = 128
ROPE_THETA = 500000.0
EPS = 1e-6
GDN_W = GDN_HEADS * HEAD_DIM
ATT_W = ATT_HEADS * HEAD_DIM
KV_W = N_KV * HEAD_DIM
IDX_W = IDX_HEADS * IDX_DIM
QKV_W = 3 * GDN_W

GDN_CHUNK = 128
ATT_TK = 256
VT_ROWS = HEAD_DIM + 16
IDX_TK = 512
ATT_BLK = 1024
LANES = 128
NAT_W = QKV_W + GDN_W + KV_W + LANES
T_AQ, T_AK, T_IQ, T_IK, T_MISC, T_AV = 0, 512, 768, 1280, 1344, 1360
T_ROWS = T_AV + KV_W
VMEM_LIMIT = 56 * 1024 * 1024
INT_MIN = np.int32(-2 ** 31)
NEG_INF = float("-inf")


def _params(sem):
    return pltpu.CompilerParams(dimension_semantics=sem, vmem_limit_bytes=VMEM_LIMIT)


def _dot(a, b):
    return jnp.dot(a, b, preferred_element_type=F32)


def _dot_nt(a, b):
    return lax.dot_general(a, b, (((1,), (1,)), ((), ())), preferred_element_type=F32)


def _split3(x):
    hi = x.astype(BF)
    r1 = x - hi.astype(F32)
    mid = r1.astype(BF)
    lo = (r1 - mid.astype(F32)).astype(BF)
    return hi, mid, lo


def _mm_hi(a, b):
    ah = a.astype(BF)
    al = (a - ah.astype(F32)).astype(BF)
    bh = b.astype(BF)
    bl = (b - bh.astype(F32)).astype(BF)
    return _dot(ah, bh) + (_dot(ah, bl) + _dot(al, bh))


def _sigmoid(x):
    return 1.0 / (1.0 + jnp.exp(-x))


def _silu(x):
    return x * _sigmoid(x)


def _softplus(x):
    return jnp.maximum(x, 0.0) + jnp.log(1.0 + jnp.exp(-jnp.abs(x)))


def _rmsnorm(x, g):
    return x * lax.rsqrt(jnp.mean(x * x, axis=-1, keepdims=True) + EPS) * g


def _mod_kernel(c_ref, w_ref, b_ref, o_ref):
    s = _silu(c_ref[...]).astype(BF)
    o_ref[...] = _dot(s, w_ref[...].astype(BF)) + b_ref[...]


def _modulation(c_all, w_ada, b_ada):
    n, d = c_all.shape
    cols = w_ada.shape[1]
    tn = d
    return pl.pallas_call(
        _mod_kernel,
        grid=(cols // tn,),
        in_specs=[pl.BlockSpec((n, d), lambda j: (0, 0)),
                  pl.BlockSpec((d, tn), lambda j: (0, j)),
                  pl.BlockSpec((1, tn), lambda j: (0, j))],
        out_specs=pl.BlockSpec((n, tn), lambda j: (0, j)),
        out_shape=jax.ShapeDtypeStruct((n, cols), F32),
        compiler_params=_params(("arbitrary",)),
        name="modulation",
    )(c_all, w_ada, b_ada.reshape(1, cols))


def _rope_rows(rt_ref, base, half, cos, sin):
    x1 = rt_ref[base:base + half, :]
    x2 = rt_ref[base + half:base + 2 * half, :]
    rt_ref[base:base + half, :] = x1 * cos - x2 * sin
    rt_ref[base + half:base + 2 * half, :] = x2 * cos + x1 * sin


def _project(x_ref, sc_ref, sh_ref, g1_ref, wn_ref, wt_ref, cs128_ref, cs64_ref, pcol_ref, rt_ref):
    h = _rmsnorm(x_ref[...], g1_ref[...]) * (1.0 + sc_ref[...]) + sh_ref[...]
    hb = h.astype(BF)
    nat = _dot(hb, wn_ref[...])
    rt_ref[...] = _dot_nt(wt_ref[...], hb)
    half = HEAD_DIM // 8
    cos, sin = cs128_ref[0:half, :], cs128_ref[half:2 * half, :]
    for hd in range(ATT_HEADS):
        _rope_rows(rt_ref, T_AQ + hd * HEAD_DIM, half, cos, sin)
    for hd in range(N_KV):
        _rope_rows(rt_ref, T_AK + hd * HEAD_DIM, half, cos, sin)
    half = IDX_DIM // 8
    cos, sin = cs64_ref[0:half, :], cs64_ref[half:2 * half, :]
    for hd in range(IDX_HEADS):
        _rope_rows(rt_ref, T_IQ + hd * IDX_DIM, half, cos, sin)
    _rope_rows(rt_ref, T_IK, half, cos, sin)
    mt = rt_ref[T_MISC:T_MISC + 16, :]
    r = lax.broadcasted_iota(I32, mt.shape, 0)
    a_log, dt_b = pcol_ref[:, 0:1], pcol_ref[:, 1:2]
    gate = -jnp.exp(a_log) * _softplus(mt + dt_b)
    mt = jnp.where(r < IDX_HEADS, mt * IDX_HEADS ** -0.5, jnp.where(r < IDX_HEADS + GDN_HEADS, _sigmoid(mt), gate))
    rt_ref[T_MISC:T_MISC + 16, :] = mt
    return nat, mt


def _misc_natural(m, prow_ref):
    lane = lax.broadcasted_iota(I32, m.shape, 1)
    gate = -jnp.exp(prow_ref[0:1, :]) * _softplus(m + prow_ref[1:2, :])
    return jnp.where(lane < GDN_HEADS, _sigmoid(m), jnp.where(lane < 3 * GDN_HEADS, gate, 0.0))


def _qkv_post(conv, qkvn_ref):
    c = _silu(conv)
    for hb in range(2 * GDN_HEADS):
        xh = c[:, hb * HEAD_DIM:(hb + 1) * HEAD_DIM]
        n = xh * lax.rsqrt(jnp.sum(xh * xh, axis=-1, keepdims=True) + EPS)
        if hb < GDN_HEADS:
            n = n * HEAD_DIM ** -0.5
        qkvn_ref[:, hb * HEAD_DIM:(hb + 1) * HEAD_DIM] = n
    qkvn_ref[:, 2 * GDN_W:] = c[:, 2 * GDN_W:]


def _inproj_prompt_kernel(tm, parts, x_ref, sc_ref, sh_ref, g1_ref, wn_ref, wt_ref, wconv_ref, cs128_ref, cs64_ref,
                          prow_ref, pcol_ref, tril_ref,
                          qkvn_ref, z_ref, vnat_ref, knat_ref, ikT_ref, tail_ref, misc_ref, gct_ref,
                          qT_ref, iqT_ref, iwT_ref, kbf_ref, vT_ref, ikbf_ref,
                          xs_ref, rt_ref):
    i = pl.program_id(1)

    @pl.when(i == 0)
    def _():
        xs_ref[0:8, :] = jnp.zeros((8, QKV_W), F32)

    @pl.when(i > 0)
    def _():
        xs_ref[0:8, :] = xs_ref[tm:tm + 8, :]

    th = tm // parts
    for n in range(parts):
        rows, cols = pl.ds(n * th, th), pl.ds(n * th, th)
        _inproj_prompt_rows(
            th, x_ref.at[rows, :], sc_ref, sh_ref, g1_ref, wn_ref, wt_ref, wconv_ref,
            cs128_ref.at[:, cols], cs64_ref.at[:, cols], prow_ref, pcol_ref, tril_ref.at[0:th, 0:th],
            qkvn_ref.at[rows, :], z_ref.at[rows, :], vnat_ref.at[pl.ds(n * th * N_KV, th * N_KV), :],
            knat_ref.at[pl.ds(n * th * N_KV, th * N_KV), :], ikT_ref.at[:, cols], tail_ref, misc_ref.at[rows, :],
            gct_ref.at[pl.ds(n * (th // GDN_CHUNK), th // GDN_CHUNK)],
            qT_ref.at[:, cols], iqT_ref.at[:, cols], iwT_ref.at[:, cols], kbf_ref.at[rows, :],
            vT_ref.at[pl.ds(n * (th // ATT_TK), th // ATT_TK)], ikbf_ref.at[rows, :],
            xs_ref.at[pl.ds(n * th, th + 8), :], rt_ref.at[:, cols])


def _inproj_prompt_rows(tm, x_ref, sc_ref, sh_ref, g1_ref, wn_ref, wt_ref, wconv_ref, cs128_ref, cs64_ref,
                        prow_ref, pcol_ref, tril_ref,
                        qkvn_ref, z_ref, vnat_ref, knat_ref, ikT_ref, tail_ref, misc_ref, gct_ref,
                        qT_ref, iqT_ref, iwT_ref, kbf_ref, vT_ref, ikbf_ref,
                        xs_ref, rt_ref):
    nat, mt = _project(x_ref, sc_ref, sh_ref, g1_ref, wn_ref, wt_ref, cs128_ref, cs64_ref, pcol_ref, rt_ref)
    xs_ref[8:tm + 8, :] = nat[:, 0:QKV_W]
    conv = wconv_ref[0:1, :] * xs_ref[5:tm + 5, :]
    for t in range(1, CONV_W):
        conv = conv + wconv_ref[t:t + 1, :] * xs_ref[5 + t:tm + 5 + t, :]
    tail_ref[...] = xs_ref[tm:tm + 8, :]
    _qkv_post(conv, qkvn_ref)
    z_ref[...] = nat[:, QKV_W:QKV_W + GDN_W]
    v0 = QKV_W + GDN_W
    for hd in range(N_KV):
        vnat_ref[pl.ds(hd, tm, stride=N_KV), :] = nat[:, v0 + hd * HEAD_DIM:v0 + (hd + 1) * HEAD_DIM]

    gm = _misc_natural(nat[:, NAT_W - LANES:], prow_ref)
    tril = tril_ref[...]
    hi, mid, lo = _split3(gm)
    gc = _dot(tril, hi) + (_dot(tril, mid) + _dot(tril, lo))
    lane = lax.broadcasted_iota(I32, gm.shape, 1)
    misc_ref[...] = jnp.where(lane < 2 * GDN_HEADS, gm, gc)
    hi, mid, lo = _split3(mt)
    gct = _dot_nt(hi, tril) + (_dot_nt(mid, tril) + _dot_nt(lo, tril))
    r = lax.broadcasted_iota(I32, mt.shape, 0)
    bg = jnp.where(r < IDX_HEADS + GDN_HEADS, mt, gct)[8:16, :]
    for cc in range(tm // GDN_CHUNK):
        gct_ref[cc] = bg[:, cc * GDN_CHUNK:(cc + 1) * GDN_CHUNK]

    iwT_ref[...] = mt[0:IDX_HEADS, :]
    qT_ref[...] = rt_ref[T_AQ:T_AQ + ATT_W, :].astype(BF)
    iqT_ref[...] = rt_ref[T_IQ:T_IQ + IDX_W, :].astype(BF)
    kn = rt_ref[T_AK:T_AK + KV_W, :].T
    for hd in range(N_KV):
        knat_ref[pl.ds(hd, tm, stride=N_KV), :] = kn[:, hd * HEAD_DIM:(hd + 1) * HEAD_DIM]
    kbf_ref[...] = kn.astype(BF)
    ikT_ref[...] = rt_ref[T_IK:T_IK + IDX_DIM, :]
    ikbf_ref[...] = rt_ref[T_IK:T_IK + LANES, :].T[:, 0:IDX_DIM].astype(BF)
    for cc in range(tm // ATT_TK):
        for hd in range(N_KV):
            lo = hd * VT_ROWS
            vT_ref[cc, lo:lo + HEAD_DIM, :] = rt_ref[T_AV + hd * HEAD_DIM:T_AV + (hd + 1) * HEAD_DIM,
                                                     cc * ATT_TK:(cc + 1) * ATT_TK].astype(BF)
            vT_ref[cc, lo + HEAD_DIM:lo + VT_ROWS, :] = jnp.ones((VT_ROWS - HEAD_DIM, ATT_TK), BF)


def _inproj_sample_kernel(x_ref, sc_ref, sh_ref, g1_ref, wn_ref, wt_ref, wconv_ref, cs128_ref, cs64_ref,
                          prow_ref, pcol_ref, s0_ref, s1_ref, s2_ref,
                          qkvn_ref, z_ref, vnat_ref, knat_ref, slab_ref, raw_ref, misc_ref, aq_ref, iq_ref,
                          rt_ref):
    nat, _ = _project(x_ref, sc_ref, sh_ref, g1_ref, wn_ref, wt_ref, cs128_ref, cs64_ref, pcol_ref, rt_ref)
    raw = nat[:, 0:QKV_W]
    raw_ref[...] = raw
    conv = wconv_ref[0:1, :] * s0_ref[...]
    conv = conv + wconv_ref[1:2, :] * s1_ref[...]
    conv = conv + wconv_ref[2:3, :] * s2_ref[...]
    conv = conv + wconv_ref[3:4, :] * raw
    _qkv_post(conv, qkvn_ref)
    z_ref[...] = nat[:, QKV_W:QKV_W + GDN_W]
    vnat_ref[...] = nat[:, QKV_W + GDN_W:QKV_W + GDN_W + KV_W]
    misc_ref[...] = _misc_natural(nat[:, NAT_W - LANES:], prow_ref)
    aq_ref[...] = rt_ref[T_AQ:T_AQ + ATT_W, :].T
    iq_ref[...] = rt_ref[T_IQ:T_IQ + IDX_W, :].T
    knat_ref[...] = rt_ref[T_AK:T_AK + KV_W, :].T
    slab_ref[...] = rt_ref[T_IK:T_IK + LANES, :].T


def _inproj_weights(w_in, a_log, dt_bias):
    offs = np.cumsum([0, QKV_W, GDN_W, GDN_HEADS, GDN_HEADS, ATT_W, KV_W, KV_W, IDX_W, IDX_DIM, IDX_HEADS])
    qkv, z, beta, a, aq, ak, av, iq, ik, iw = [w_in[:, offs[n]:offs[n + 1]] for n in range(10)]
    d = w_in.shape[0]
    pad = jnp.zeros((d, LANES - 3 * GDN_HEADS), w_in.dtype)
    w_nat = jnp.concatenate([qkv, z, av, beta, a, a, pad], axis=1).astype(BF)
    w_t = jnp.concatenate([aq, ak, iq, ik, iw, beta, a, av], axis=1).T.astype(BF)
    zrow = jnp.zeros((LANES - 3 * GDN_HEADS,), F32)
    prow = jnp.stack([jnp.concatenate([jnp.zeros((GDN_HEADS,), F32), a_log, a_log, zrow]),
                      jnp.concatenate([jnp.zeros((GDN_HEADS,), F32), dt_bias, dt_bias, zrow])])
    z12 = jnp.zeros((IDX_HEADS + GDN_HEADS,), F32)
    pcol = jnp.stack([jnp.concatenate([z12, a_log]), jnp.concatenate([z12, dt_bias])], axis=1)
    return w_nat, w_t, prow, pcol


def _rope_tables(pos, dim):
    half = dim // 8
    inv = ROPE_THETA ** (-jnp.arange(half, dtype=F32) / half)
    ang = pos.astype(F32)[:, None] * inv[None, :]
    return jnp.concatenate([jnp.cos(ang).T, jnp.sin(ang).T], axis=0)


def _inproj_prompt(x, mod, g1, w_nat, w_t, w_conv, prow, pcol, tm):
    bp, tp, d = x.shape
    nt = tp // tm
    rows = bp * tp
    pos = jnp.arange(tp, dtype=I32)
    cs128, cs64 = _rope_tables(pos, HEAD_DIM), _rope_tables(pos, IDX_DIM)
    ri = np.arange(tm)
    tril = jnp.asarray((ri[:, None] // GDN_CHUNK == ri[None, :] // GDN_CHUNK) & (ri[None, :] <= ri[:, None]), BF)
    const = lambda shape: pl.BlockSpec(shape, lambda b, i: (0,) * len(shape))
    rowblk = lambda w: pl.BlockSpec((tm, w), lambda b, i: (b * nt + i, 0))
    colblk = lambda h: pl.BlockSpec((h, tm), lambda b, i: (0, b * nt + i))
    in_specs = [
        pl.BlockSpec((None, tm, d), lambda b, i: (b, i, 0)),
        pl.BlockSpec((None, 1, d), lambda b, i: (b, 0, 1)),
        pl.BlockSpec((None, 1, d), lambda b, i: (b, 0, 0)),
        const((1, d)), const((d, NAT_W)), const((T_ROWS, d)), const((CONV_W, QKV_W)),
        pl.BlockSpec((HEAD_DIM // 4, tm), lambda b, i: (0, i)),
        pl.BlockSpec((IDX_DIM // 4, tm), lambda b, i: (0, i)),
        const((2, LANES)), const((16, 2)), const((tm, tm)),
    ]
    out_shape = [
        jax.ShapeDtypeStruct((rows, QKV_W), F32), jax.ShapeDtypeStruct((rows, GDN_W), F32),
        jax.ShapeDtypeStruct((rows * N_KV, HEAD_DIM), F32), jax.ShapeDtypeStruct((rows * N_KV, HEAD_DIM), F32),
        jax.ShapeDtypeStruct((bp, IDX_DIM, tp), F32), jax.ShapeDtypeStruct((bp, 8, QKV_W), F32),
        jax.ShapeDtypeStruct((rows, LANES), F32), jax.ShapeDtypeStruct((rows // GDN_CHUNK, 8, GDN_CHUNK), F32),
        jax.ShapeDtypeStruct((ATT_W, rows), BF), jax.ShapeDtypeStruct((IDX_W, rows), BF),
        jax.ShapeDtypeStruct((IDX_HEADS, rows), F32), jax.ShapeDtypeStruct((rows, KV_W), BF),
        jax.ShapeDtypeStruct((rows // ATT_TK, N_KV * VT_ROWS, ATT_TK), BF), jax.ShapeDtypeStruct((rows, IDX_DIM), BF),
    ]
    out_specs = [
        rowblk(QKV_W), rowblk(GDN_W),
        pl.BlockSpec((tm * N_KV, HEAD_DIM), lambda b, i: (b * nt + i, 0)),
        pl.BlockSpec((tm * N_KV, HEAD_DIM), lambda b, i: (b * nt + i, 0)),
        pl.BlockSpec((None, IDX_DIM, tm), lambda b, i: (b, 0, i)),
        pl.BlockSpec((None, 8, QKV_W), lambda b, i: (b, 0, 0)),
        rowblk(LANES),
        pl.BlockSpec((tm // GDN_CHUNK, 8, GDN_CHUNK), lambda b, i: (b * nt + i, 0, 0)),
        colblk(ATT_W), colblk(IDX_W), colblk(IDX_HEADS), rowblk(KV_W),
        pl.BlockSpec((tm // ATT_TK, N_KV * VT_ROWS, ATT_TK), lambda b, i: (b * nt + i, 0, 0)),
        rowblk(IDX_DIM),
    ]
    return pl.pallas_call(
        functools.partial(_inproj_prompt_kernel, tm, tm // _pick(tm, (ATT_TK,))),
        grid=(bp, nt), in_specs=in_specs, out_specs=out_specs, out_shape=out_shape,
        scratch_shapes=[pltpu.VMEM((tm + 8, QKV_W), F32), pltpu.VMEM((T_ROWS, tm), F32)],
        compiler_params=_params(("arbitrary", "arbitrary")),
        name="inproj_prompt",
    )(x, mod, mod, g1.reshape(1, d), w_nat, w_t, w_conv, cs128, cs64, prow, pcol, tril)


def _inproj_sample(x, mod, g1, w_nat, w_t, w_conv, prow, pcol, conv_state, past):
    db, d = x.shape
    pos = jnp.full((db,), past, I32)
    cs128, cs64 = _rope_tables(pos, HEAD_DIM), _rope_tables(pos, IDX_DIM)
    full = lambda shape: pl.BlockSpec(shape, lambda i: (0,) * len(shape))
    in_specs = [
        full((db, d)),
        pl.BlockSpec((db, d), lambda i: (0, 1)), pl.BlockSpec((db, d), lambda i: (0, 0)),
        full((1, d)), full((d, NAT_W)), full((T_ROWS, d)), full((CONV_W, QKV_W)),
        full((HEAD_DIM // 4, db)), full((IDX_DIM // 4, db)), full((2, LANES)), full((16, 2)),
        full((db, QKV_W)), full((db, QKV_W)), full((db, QKV_W)),
    ]
    widths = [QKV_W, GDN_W, KV_W, KV_W, LANES, QKV_W, LANES, ATT_W, IDX_W]
    return pl.pallas_call(
        _inproj_sample_kernel,
        grid=(1,), in_specs=in_specs,
        out_specs=[full((db, w)) for w in widths],
        out_shape=[jax.ShapeDtypeStruct((db, w), F32) for w in widths],
        scratch_shapes=[pltpu.VMEM((T_ROWS, db), F32)],
        compiler_params=_params(("arbitrary",)),
        name="inproj_sample",
    )(x, mod, mod, g1.reshape(1, d), w_nat, w_t, w_conv, cs128, cs64, prow, pcol,
      conv_state[:, 0], conv_state[:, 1], conv_state[:, 2])


def _unit_lower_inverses(lows):
    c = lows[0].shape[0]
    ii = lax.broadcasted_iota(I32, (c, c), 0)
    jj = lax.broadcasted_iota(I32, (c, c), 1)
    eye = jnp.where(ii == jj, 1.0, 0.0)
    levels = int(math.log2(c)) - 1
    ts = [eye - low for low in lows]
    ps = [_mm_hi(low, low) for low in lows]
    for lvl in range(levels):
        if lvl == levels - 1:
            ts = [t + _mm_hi(t, p) for t, p in zip(ts, ps)]
        else:
            both = [_mm_hi(jnp.concatenate([t, p], axis=0), p) for t, p in zip(ts, ps)]
            ts = [t + b[0:c] for t, b in zip(ts, both)]
            ps = [b[c:] for b in both]
    return ts


def _gdn_prompt_kernel(tg, qkvn_ref, z_ref, misc_ref, gct_ref, gn_ref, o_ref, ssm_ref, s_ref):
    i = pl.program_id(1)

    @pl.when(i == 0)
    def _():
        s_ref[...] = jnp.zeros(s_ref.shape, F32)

    c = GDN_CHUNK
    ii = lax.broadcasted_iota(I32, (c, c), 0)
    jj = lax.broadcasted_iota(I32, (c, c), 1)
    pairs = [(cc, hd) for cc in range(tg // c) for hd in range(GDN_HEADS)]

    qs, ks, gcs, rhs, lows, intras = {}, {}, {}, {}, [], {}
    for cc, hd in pairs:
        r0, lo = cc * c, hd * HEAD_DIM
        q = qkvn_ref[r0:r0 + c, lo:lo + HEAD_DIM]
        k = qkvn_ref[r0:r0 + c, GDN_W + lo:GDN_W + lo + HEAD_DIM]
        v = qkvn_ref[r0:r0 + c, 2 * GDN_W + lo:2 * GDN_W + lo + HEAD_DIM]
        beta_c = misc_ref[r0:r0 + c, hd:hd + 1]
        gc_c = misc_ref[r0:r0 + c, 2 * GDN_HEADS + hd:2 * GDN_HEADS + hd + 1]
        gc_r = gct_ref[cc, GDN_HEADS + hd:GDN_HEADS + hd + 1, :]
        decay = jnp.where(ii >= jj, jnp.exp(jnp.where(ii >= jj, gc_c - gc_r, 0.0)), 0.0)
        kb = k * beta_c
        both = _dot_nt(jnp.concatenate([kb, q], axis=0).astype(BF), k.astype(BF))
        lows.append(jnp.where(ii > jj, both[0:c] * decay, 0.0))
        intras[cc, hd] = (both[c:] * decay).astype(BF)
        rhs[cc, hd] = jnp.concatenate([v * beta_c, kb * jnp.exp(gc_c)], axis=1).astype(BF)
        qs[cc, hd], ks[cc, hd], gcs[cc, hd] = q, k, gc_c
    ts = _unit_lower_inverses(lows)
    uws = {p: _dot(t.astype(BF), rhs[p]) for p, t in zip(pairs, ts)}

    for cc, hd in pairs:
        r0, lo = cc * c, hd * HEAD_DIM
        q, k, gc_c, uw = qs[cc, hd], ks[cc, hd], gcs[cc, hd], uws[cc, hd]
        s = s_ref[hd]
        ws = _dot(jnp.concatenate([uw[:, HEAD_DIM:], q * jnp.exp(gc_c)], axis=0).astype(BF), s.astype(BF))
        v_new = (uw[:, 0:HEAD_DIM] - ws[0:c]).astype(BF)
        o = ws[c:] + _dot(intras[cc, hd], v_new)
        g_last = gc_c[c - 1:c, :]
        kd = k * jnp.exp(g_last - gc_c)
        s_ref[hd] = s * jnp.exp(g_last) + _dot(kd.T.astype(BF), v_new)
        o = _rmsnorm(o, gn_ref[...]) * _silu(z_ref[r0:r0 + c, lo:lo + HEAD_DIM])
        o_ref[r0:r0 + c, lo:lo + HEAD_DIM] = o.astype(BF)
    ssm_ref[...] = s_ref[...]


def _gdn_prompt(qkvn, z, misc, gct, g_norm, bp, tp, tg):
    nt = tp // tg
    rows = bp * tp
    rowblk = lambda w: pl.BlockSpec((tg, w), lambda b, i: (b * nt + i, 0))
    return pl.pallas_call(
        functools.partial(_gdn_prompt_kernel, tg),
        grid=(bp, nt),
        in_specs=[rowblk(QKV_W), rowblk(GDN_W), rowblk(LANES),
                  pl.BlockSpec((tg // GDN_CHUNK, 8, GDN_CHUNK), lambda b, i: (b * nt + i, 0, 0)),
                  pl.BlockSpec((1, HEAD_DIM), lambda b, i: (0, 0))],
        out_specs=[rowblk(GDN_W),
                   pl.BlockSpec((None, GDN_HEADS, HEAD_DIM, HEAD_DIM), lambda b, i: (b, 0, 0, 0))],
        out_shape=[jax.ShapeDtypeStruct((rows, GDN_W), BF),
                   jax.ShapeDtypeStruct((bp, GDN_HEADS, HEAD_DIM, HEAD_DIM), F32)],
        scratch_shapes=[pltpu.VMEM((GDN_HEADS, HEAD_DIM, HEAD_DIM), F32)],
        compiler_params=_params(("arbitrary", "arbitrary")),
        name="gdn_prompt",
    )(qkvn, z, misc, gct, g_norm.reshape(1, HEAD_DIM))


def _gdn_sample_kernel(nb, qkvn_ref, z_ref, misc_ref, gn_ref, s_ref, o_ref, ssm_ref):
    for bi in range(nb):
        for hd in range(GDN_HEADS):
            lo = hd * HEAD_DIM
            q = qkvn_ref[bi:bi + 1, lo:lo + HEAD_DIM]
            k = qkvn_ref[bi:bi + 1, GDN_W + lo:GDN_W + lo + HEAD_DIM]
            v = qkvn_ref[bi:bi + 1, 2 * GDN_W + lo:2 * GDN_W + lo + HEAD_DIM]
            beta = misc_ref[bi:bi + 1, hd:hd + 1]
            g = misc_ref[bi:bi + 1, GDN_HEADS + hd:GDN_HEADS + hd + 1]
            kcol = jnp.broadcast_to(k, (HEAD_DIM, HEAD_DIM)).T
            qcol = jnp.broadcast_to(q, (HEAD_DIM, HEAD_DIM)).T
            s = s_ref[bi, hd] * jnp.exp(g)
            ks = jnp.sum(kcol * s, axis=0, keepdims=True)
            delta = (v - ks) * beta
            s = s + kcol * delta
            ssm_ref[bi, hd] = s
            o = jnp.sum(qcol * s, axis=0, keepdims=True)
            o = _rmsnorm(o, gn_ref[...]) * _silu(z_ref[bi:bi + 1, lo:lo + HEAD_DIM])
            o_ref[bi:bi + 1, lo:lo + HEAD_DIM] = o.astype(BF)


def _gdn_sample(qkvn, z, misc, g_norm, state, nb):
    db = qkvn.shape[0]
    rowblk = lambda w: pl.BlockSpec((nb, w), lambda i: (i, 0))
    sblk = pl.BlockSpec((nb, GDN_HEADS, HEAD_DIM, HEAD_DIM), lambda i: (i, 0, 0, 0))
    return pl.pallas_call(
        functools.partial(_gdn_sample_kernel, nb),
        grid=(db // nb,),
        in_specs=[rowblk(QKV_W), rowblk(GDN_W), rowblk(LANES), pl.BlockSpec((1, HEAD_DIM), lambda i: (0, 0)), sblk],
        out_specs=[rowblk(GDN_W), sblk],
        out_shape=[jax.ShapeDtypeStruct((db, GDN_W), BF), jax.ShapeDtypeStruct(state.shape, F32)],
        compiler_params=_params(("arbitrary",)),
        name="gdn_sample",
    )(qkvn, z, misc, g_norm.reshape(1, HEAD_DIM), state)


def _ordered_word_to_float(u):
    s = u ^ INT_MIN
    return lax.bitcast_convert_type(s ^ ((s >> 31) & np.int32(0x7FFFFFFF)), F32)


def _select_topk_bias(i_ref, t_ref, nch, tk, k_sel, idx_bits, taken=0.0, dropped=NEG_INF):
    nl = i_ref.shape[1]
    row = lax.broadcasted_iota(I32, (tk, 1), 0)

    def chunk(c):
        return pl.ds(pl.multiple_of(c * tk, tk), tk)

    def count(ref, pred):
        def body(c, acc):
            m = pred(ref[chunk(c), :])
            return acc + jnp.sum(jnp.where(m, 1, 0).astype(I32).reshape(tk // 8, 8, nl), axis=0)
        acc = lax.fori_loop(0, nch, body, jnp.zeros((8, nl), I32))
        return jnp.sum(acc, axis=0, keepdims=True)

    def bit_body(i, carry):
        t_u, n_ge = carry
        cand_u = t_u | jnp.left_shift(jnp.int32(1), 31 - i)
        cand = _ordered_word_to_float(cand_u)
        cnt = count(i_ref, lambda sc: sc >= cand)
        take = cnt >= k_sel
        return jnp.where(take, cand_u, t_u), jnp.where(take, cnt, n_ge)

    t_u, n_ge = lax.fori_loop(0, 32, bit_body, (jnp.zeros((1, nl), I32), jnp.full((1, nl), -1, I32)))
    thr = jnp.where((t_u >> 23) == 0, NEG_INF, _ordered_word_to_float(t_u))

    def tie_break(_):
        need = k_sel - count(i_ref, lambda sc: sc > thr)

        def mark(c, carry):
            t_ref[chunk(c), :] = jnp.where(i_ref[chunk(c), :] == thr, c * tk + row, 2 ** idx_bits)
            return carry

        lax.fori_loop(0, nch, mark, 0)

        def idx_body(i, x):
            cand = x | jnp.left_shift(jnp.int32(1), idx_bits - 1 - i)
            cnt = count(t_ref, lambda idx: idx < cand)
            return jnp.where(cnt < need, cand, x)

        return lax.fori_loop(0, idx_bits, idx_body, jnp.zeros((1, nl), I32))

    x = lax.cond(jnp.max(n_ge) > k_sel, tie_break, lambda _: jnp.full((1, nl), 2 ** idx_bits - 1, I32), 0)

    def write(c, carry):
        sc = i_ref[chunk(c), :]
        sel = (sc > thr) | ((sc == thr) & (c * tk + row <= x))
        i_ref[chunk(c), :] = jnp.where(sel & (sc > NEG_INF), taken, dropped)
        return carry

    lax.fori_loop(0, nch, write, 0)


def _attn_prompt_kernel(tq, tk, blk, k_sel, idx_bits, qT_ref, iqT_ref, iwT_ref, kbf_ref, vT_ref, ikbf_ref, o_ref,
                        i_ref, t_ref):
    t0 = pl.program_id(1) * tq
    nch = (t0 + tq + tk - 1) // tk
    nblk = (t0 + tq + blk - 1) // blk
    qpos = t0 + lax.broadcasted_iota(I32, (1, tq), 1)
    row = lax.broadcasted_iota(I32, (tk, 1), 0)
    w = iwT_ref[...]
    pairs = IDX_HEADS // 2
    rhs = [jnp.concatenate([iqT_ref[(2 * p) * IDX_DIM:(2 * p + 1) * IDX_DIM, :],
                            iqT_ref[(2 * p + 1) * IDX_DIM:(2 * p + 2) * IDX_DIM, :]], axis=1) for p in range(pairs)]

    def chunk(c):
        return pl.ds(pl.multiple_of(c * tk, tk), tk)

    def idx_body(c, carry):
        ikc = ikbf_ref[chunk(c), :]
        acc = jnp.zeros((tk, tq), F32)
        for p in range(pairs):
            d = jnp.maximum(_dot(ikc, rhs[p]), 0.0)
            acc = acc + d[:, 0:tq] * w[2 * p:2 * p + 1, :] + d[:, tq:] * w[2 * p + 1:2 * p + 2, :]
        i_ref[chunk(c), :] = jnp.where(c * tk + row <= qpos, acc, NEG_INF)
        return carry

    lax.fori_loop(0, nch, idx_body, 0)

    def fill_body(c, carry):
        i_ref[chunk(c), :] = jnp.full((tk, tq), NEG_INF, F32)
        return carry

    lax.fori_loop(nch, nblk * (blk // tk), fill_body, 0)
    _select_topk_bias(i_ref, t_ref, nch, tk, k_sel, idx_bits)

    scale2 = HEAD_DIM ** -0.5 * math.log2(math.e)
    rep = ATT_HEADS // N_KV
    sub = ATT_TK
    nsub = blk // sub
    qg = [jnp.concatenate([qT_ref[(rep * g + r) * HEAD_DIM:(rep * g + r + 1) * HEAD_DIM, :] for r in range(rep)],
                          axis=1) for g in range(N_KV)]

    def body(c, carry):
        out = []
        for g in range(N_KV):
            m, acc = carry[g]
            rows = pl.ds(pl.multiple_of(c * blk, blk), blk)
            qk = _dot(kbf_ref[rows, g * HEAD_DIM:(g + 1) * HEAD_DIM], qg[g])
            ss = []
            for i in range(nsub):
                bias = i_ref[pl.ds(pl.multiple_of(c * blk + i * sub, sub), sub), :]
                ss.append(qk[i * sub:(i + 1) * sub] * scale2 + jnp.concatenate([bias] * rep, axis=1))
            m_new = m
            for s in ss:
                m_new = jnp.maximum(m_new, jnp.max(s, axis=0, keepdims=True))
            m_safe = jnp.where(m_new == NEG_INF, 0.0, m_new)
            acc = acc * jnp.exp2(m - m_safe)
            for i, s in enumerate(ss):
                p = jnp.exp2(s - m_safe).astype(BF)
                acc = acc + _dot(vT_ref[c * nsub + i, g * VT_ROWS:(g + 1) * VT_ROWS, :], p)
            out.append((m_new, acc))
        return tuple(out)

    init = tuple((jnp.full((1, rep * tq), NEG_INF, F32), jnp.zeros((VT_ROWS, rep * tq), F32)) for _ in range(N_KV))
    res = lax.fori_loop(0, nblk, body, init)
    for g in range(N_KV):
        acc = res[g][1]
        o_t = acc[0:HEAD_DIM] / acc[HEAD_DIM:HEAD_DIM + 1]
        for r in range(rep):
            hd = rep * g + r
            o_ref[:, hd * HEAD_DIM:(hd + 1) * HEAD_DIM] = o_t[:, r * tq:(r + 1) * tq].T.astype(BF)


def _attn_prompt(qT, iqT, iwT, kbf, vT, ikbf, bp, tp):
    tq = QBLOCK
    tk = _pick(tp, (IDX_TK, ATT_TK))
    blk = _pick(tp, (ATT_BLK, IDX_TK, ATT_TK))
    assert blk % tk == 0 and blk % ATT_TK == 0
    nq = tp // tq
    rows = bp * tp
    k_sel = min(TOPK_MAX, tp // 4)
    idx_bits = max(1, int(tp - 1).bit_length())
    colblk = lambda h: pl.BlockSpec((h, tq), lambda b, j: (0, b * nq + j))
    return pl.pallas_call(
        functools.partial(_attn_prompt_kernel, tq, tk, blk, k_sel, idx_bits),
        grid=(bp, nq),
        in_specs=[colblk(ATT_W), colblk(IDX_W), colblk(IDX_HEADS),
                  pl.BlockSpec((tp, KV_W), lambda b, j: (b, 0)),
                  pl.BlockSpec((tp // ATT_TK, N_KV * VT_ROWS, ATT_TK), lambda b, j: (b, 0, 0)),
                  pl.BlockSpec((tp, IDX_DIM), lambda b, j: (b, 0))],
        out_specs=pl.BlockSpec((tq, ATT_W), lambda b, j: (b * nq + j, 0)),
        out_shape=jax.ShapeDtypeStruct((rows, ATT_W), BF),
        scratch_shapes=[pltpu.VMEM((tp, tq), F32), pltpu.VMEM((tp, tq), I32)],
        compiler_params=_params(("arbitrary", "arbitrary")),
        name="attn_prompt",
    )(qT, iqT, iwT, kbf, vT, ikbf)


def _sidx_kernel(npg, pt_ref, iq_ref, iw_ref, ikn_ref, *refs):
    pages, out_ref = refs[:npg], refs[npg]
    iq = iq_ref[...].astype(BF)
    w = iw_ref[...]
    for p in range(npg):
        d = jnp.maximum(_dot(iq, pages[p][...].astype(BF)), 0.0)
        out_ref[:, p * LANES:(p + 1) * LANES] = jnp.sum(d * w, axis=0, keepdims=True)
    dn = jnp.sum(iq.astype(F32) * ikn_ref[...].astype(BF).astype(F32), axis=1, keepdims=True)
    sn = jnp.sum(jnp.maximum(dn, 0.0) * w, axis=0, keepdims=True)
    lane = lax.broadcasted_iota(I32, (1, LANES), 1)
    out_ref[:, npg * LANES:(npg + 1) * LANES] = jnp.where(lane == 0, sn, NEG_INF)


def _sample_index_scores(iq, iw, ik_new, cache_ik_t, page_table):
    db, npg = page_table.shape
    page = cache_ik_t.shape[2]
    assert page == LANES
    width = (npg + 1) * LANES
    page_specs = [pl.BlockSpec((None, IDX_DIM, page), functools.partial(lambda p, b, pt: (pt[b, p], 0, 0), p))
                  for p in range(npg)]
    grid_spec = pltpu.PrefetchScalarGridSpec(
        num_scalar_prefetch=1, grid=(db,),
        in_specs=[pl.BlockSpec((None, IDX_HEADS, IDX_DIM), lambda b, pt: (b, 0, 0)),
                  pl.BlockSpec((None, IDX_HEADS, 1), lambda b, pt: (b, 0, 0)),
                  pl.BlockSpec((None, 1, IDX_DIM), lambda b, pt: (b, 0, 0))] + page_specs,
        out_specs=pl.BlockSpec((None, 1, width), lambda b, pt: (b, 0, 0)))
    out = pl.pallas_call(
        functools.partial(_sidx_kernel, npg), grid_spec=grid_spec,
        out_shape=jax.ShapeDtypeStruct((db, 1, width), F32),
        compiler_params=_params(("arbitrary",)),
        name="sample_index_scores",
    )(page_table, iq.reshape(db, IDX_HEADS, IDX_DIM), iw.reshape(db, IDX_HEADS, 1), ik_new.reshape(db, 1, IDX_DIM),
      *([cache_ik_t] * npg))
    return out.reshape(db, width)


def _ssel_kernel(nch, k_sel, idx_bits, s_ref, spread_ref, keep_ref, i_ref, t_ref):
    for p in range(nch):
        i_ref[p * LANES:(p + 1) * LANES, :] = s_ref[:, p * LANES:(p + 1) * LANES].T
    _select_topk_bias(i_ref, t_ref, nch, LANES, k_sel, idx_bits, taken=1.0, dropped=0.0)
    rows = N_KV * LANES
    for p in range(nch - 1):
        k2 = _dot(spread_ref[...], i_ref[p * LANES:(p + 1) * LANES, :].astype(BF))
        for h in range(N_KV):
            keep_ref[:, p * rows + h * LANES:p * rows + (h + 1) * LANES] = k2[h * LANES:(h + 1) * LANES, :].T
    keep_ref[:, (nch - 1) * rows:(nch - 1) * rows + LANES] = i_ref[(nch - 1) * LANES:nch * LANES, :].T


def _sample_select(scores, n_keys):
    db, width = scores.shape
    assert db == LANES
    nch = width // LANES
    k_sel = min(TOPK_MAX, n_keys // 4)
    idx_bits = max(1, int(width - 1).bit_length())
    rows = N_KV * LANES
    out_w = (nch - 1) * rows + LANES
    spread = jnp.asarray(np.arange(rows)[:, None] // N_KV == np.arange(LANES)[None, :], BF)
    return pl.pallas_call(
        functools.partial(_ssel_kernel, nch, k_sel, idx_bits),
        grid=(1,),
        in_specs=[pl.BlockSpec((db, width), lambda i: (0, 0)), pl.BlockSpec((rows, LANES), lambda i: (0, 0))],
        out_specs=pl.BlockSpec((db, out_w), lambda i: (0, 0)),
        out_shape=jax.ShapeDtypeStruct((db, out_w), F32),
        scratch_shapes=[pltpu.VMEM((width, db), F32), pltpu.VMEM((width, db), I32)],
        compiler_params=_params(("arbitrary",)),
        name="sample_select",
    )(scores, spread)


def _sattn_kernel(npg, pt_ref, q_ref, keep_ref, kn_ref, vn_ref, *refs):
    kpages, vpages, o_ref = refs[:npg], refs[npg:2 * npg], refs[2 * npg]
    rep = ATT_HEADS // N_KV
    rows = N_KV * LANES
    scale = HEAD_DIM ** -0.5
    q = q_ref[...]
    q8 = jnp.concatenate([q, jnp.zeros((8 - ATT_HEADS, HEAD_DIM), F32)], axis=0).astype(BF)
    hrow = lax.broadcasted_iota(I32, (8, 1), 0)
    lane = lax.broadcasted_iota(I32, (1, LANES), 1)
    own_kv = lax.broadcasted_iota(I32, (1, rows), 1) % N_KV == hrow // rep
    parts = []
    for p in range(npg):
        s2 = _dot_nt(q8, kpages[p][...].astype(BF))
        keep2 = keep_ref[:, p * rows:(p + 1) * rows]
        parts.append(jnp.where(own_kv & (keep2 > 0.5), s2 * scale, NEG_INF))
    kn = kn_ref[...].astype(BF).astype(F32)
    kn8 = jnp.where(hrow // rep == 0, kn[0:1, :], kn[1:2, :])
    s_new = jnp.sum(q8.astype(F32) * kn8, axis=1, keepdims=True)
    keep_new = keep_ref[:, npg * rows:npg * rows + 1]
    parts.append(jnp.where((lane == 0) & (keep_new > 0.5), s_new * scale, NEG_INF))
    s = jnp.concatenate(parts, axis=1)
    m = jnp.max(s, axis=1, keepdims=True)
    e = jnp.exp(s - m)
    pr = (e / jnp.sum(e, axis=1, keepdims=True)).astype(BF)
    o8 = jnp.zeros((8, HEAD_DIM), F32)
    for p in range(npg):
        o8 = o8 + _dot(pr[:, p * rows:(p + 1) * rows], vpages[p][...].astype(BF))
    vn = vn_ref[...].astype(BF).astype(F32)
    vn8 = jnp.where(hrow // rep == 0, vn[0:1, :], vn[1:2, :])
    o8 = o8 + pr[:, npg * rows:npg * rows + 1].astype(F32) * vn8
    for hd in range(ATT_HEADS):
        o_ref[:, hd * HEAD_DIM:(hd + 1) * HEAD_DIM] = o8[hd:hd + 1, :].astype(BF)


def _sample_attention(q, keep, k_new, v_new, cache_k, cache_v, page_table):
    db, npg = page_table.shape
    n_phys, page = cache_k.shape[0], cache_k.shape[1]
    assert page == LANES and N_KV == 2
    rows = page * N_KV
    width = npg * rows + LANES
    ck = cache_k.reshape(n_phys, rows, HEAD_DIM)
    cv = cache_v.reshape(n_phys, rows, HEAD_DIM)
    page_specs = [pl.BlockSpec((None, rows, HEAD_DIM), functools.partial(lambda p, b, pt: (pt[b, p], 0, 0), p))
                  for p in range(npg)]
    grid_spec = pltpu.PrefetchScalarGridSpec(
        num_scalar_prefetch=1, grid=(db,),
        in_specs=[pl.BlockSpec((None, ATT_HEADS, HEAD_DIM), lambda b, pt: (b, 0, 0)),
                  pl.BlockSpec((None, 1, width), lambda b, pt: (b, 0, 0)),
                  pl.BlockSpec((None, N_KV, HEAD_DIM), lambda b, pt: (b, 0, 0)),
                  pl.BlockSpec((None, N_KV, HEAD_DIM), lambda b, pt: (b, 0, 0))] + page_specs + page_specs,
        out_specs=pl.BlockSpec((None, 1, ATT_W), lambda b, pt: (b, 0, 0)))
    out = pl.pallas_call(
        functools.partial(_sattn_kernel, npg), grid_spec=grid_spec,
        out_shape=jax.ShapeDtypeStruct((db, 1, ATT_W), BF),
        compiler_params=_params(("arbitrary",)),
        name="sample_attention",
    )(page_table, q.reshape(db, ATT_HEADS, HEAD_DIM), keep.reshape(db, 1, width),
      k_new.reshape(db, N_KV, HEAD_DIM), v_new.reshape(db, N_KV, HEAD_DIM), *([ck] * npg), *([cv] * npg))
    return out.reshape(db, ATT_W)


def _post_kernel(final, nff, og_ref, oa_ref, x_ref, ga1_ref, sc2_ref, sh2_ref, ga2_ref, g2_ref, gf_ref,
                 wo_ref, wg_ref, wu_ref, wd_ref, y_ref, x1_ref, h2_ref, acc_ref):
    jf = pl.program_id(1)

    @pl.when(jf == 0)
    def _():
        mixed = _dot(og_ref[...], wo_ref[0:GDN_W, :]) + _dot(oa_ref[...], wo_ref[GDN_W:, :])
        x1 = x_ref[...] + ga1_ref[...] * mixed
        x1_ref[...] = x1
        h2_ref[...] = (_rmsnorm(x1, g2_ref[...]) * (1.0 + sc2_ref[...]) + sh2_ref[...]).astype(BF)
        acc_ref[...] = jnp.zeros(acc_ref.shape, F32)

    h2 = h2_ref[...]
    act = (_silu(_dot(h2, wg_ref[...])) * _dot(h2, wu_ref[...])).astype(BF)
    acc_ref[...] += _dot(act, wd_ref[...])

    @pl.when(jf == nff - 1)
    def _():
        x2 = x1_ref[...] + ga2_ref[...] * acc_ref[...]
        y_ref[...] = _rmsnorm(x2, gf_ref[...]) if final else x2


def _post(og, oa, x2d, mod, mod_spec, g2, gf, wo, wfi, wfo, tm, tf, final):
    rows, d = x2d.shape
    dff = wfo.shape[0]
    nff = dff // tf
    rowblk = lambda w: pl.BlockSpec((tm, w), lambda i, j: (i, 0))
    const = lambda shape: pl.BlockSpec(shape, lambda i, j: (0,) * len(shape))
    in_specs = [rowblk(GDN_W), rowblk(ATT_W), rowblk(d),
                mod_spec(2), mod_spec(4), mod_spec(3), mod_spec(5),
                const((1, d)), const((1, d)), const((d, d)),
                pl.BlockSpec((d, tf), lambda i, j: (0, j)),
                pl.BlockSpec((d, tf), lambda i, j: (0, nff + j)),
                pl.BlockSpec((tf, d), lambda i, j: (j, 0))]
    return pl.pallas_call(
        functools.partial(_post_kernel, final, nff),
        grid=(rows // tm, nff), in_specs=in_specs, out_specs=rowblk(d),
        out_shape=jax.ShapeDtypeStruct((rows, d), F32),
        scratch_shapes=[pltpu.VMEM((tm, d), F32), pltpu.VMEM((tm, d), BF), pltpu.VMEM((tm, d), F32)],
        compiler_params=_params(("arbitrary", "arbitrary")),
        name="post",
    )(og, oa, x2d, mod, mod, mod, mod, g2.reshape(1, d), gf.reshape(1, d), wo, wfi, wfi, wfo)


def _pick(n, prefs):
    for p in prefs:
        if n % p == 0:
            return p
    return n


def kernel(x_prompt, x_sample, c_prompt, c_sample, cache_k, cache_v, cache_idx_k, page_table, state_conv, state_ssm,
           w_ada, b_ada, g_norm1, w_in, w_conv, a_log, dt_bias, g_gdn_norm, w_out, g_norm2, w_ffn_in, w_ffn_out,
           g_final):
    bp, tp, d = x_prompt.shape
    db, ts, _ = x_sample.shape
    assert ts == 1 and d == GDN_W + ATT_W and tp % max(ATT_TK, GDN_CHUNK) == 0
    depth = w_in.shape[0]
    npg, page = page_table.shape[1], cache_k.shape[2]
    past = npg * page
    tm_in = _pick(tp, (512, 256))
    tg = _pick(tp, (256,))
    tm_post = _pick(tp, (1024, 512, 256))
    dff = w_ffn_out.shape[1]
    tf = _pick(dff, (256, 128))

    xp = x_prompt
    xs = x_sample.reshape(db, d)
    npad = (-(bp + db)) % 8
    c_all = jnp.concatenate([c_prompt, c_sample, jnp.zeros((npad, d), F32)], axis=0)
    new_p, new_s = [], []
    for l in range(depth):
        final = l == depth - 1
        mod = _modulation(c_all, w_ada[l], b_ada[l])
        mod_p = mod[:bp].reshape(bp, 1, 6 * d)
        mod_s = mod[bp:bp + db]
        w_nat, w_t, prow, pcol = _inproj_weights(w_in[l], a_log[l], dt_bias[l])
        wo = w_out[l].astype(BF)
        wfi = w_ffn_in[l].astype(BF)
        wfo = w_ffn_out[l].astype(BF)

        (qkvn, z, vnat, knat, ikT, tail, misc, gct, qT, iqT, iwT, kbf, vT, ikbf) = _inproj_prompt(
            xp, mod_p, g_norm1[l], w_nat, w_t, w_conv[l], prow, pcol, tm_in)
        og, ssm_p = _gdn_prompt(qkvn, z, misc, gct, g_gdn_norm[l], bp, tp, tg)
        oa = _attn_prompt(qT, iqT, iwT, kbf, vT, ikbf, bp, tp)
        tiles_b = tp // tm_post
        spec_p = lambda k: pl.BlockSpec((None, 1, d), lambda i, j: (i // tiles_b, 0, k))
        xp = _post(og, oa, xp.reshape(bp * tp, d), mod_p, spec_p, g_norm2[l], g_final, wo, wfi, wfo,
                   tm_post, tf, final).reshape(bp, tp, d)
        new_p.append((knat.reshape(bp, tp, N_KV, HEAD_DIM), vnat.reshape(bp, tp, N_KV, HEAD_DIM),
                      jnp.swapaxes(ikT, 1, 2), tail[:, 8 - (CONV_W - 1):, :], ssm_p))

        (qkvn_s, z_s, vnat_s, knat_s, slab_s, raw_s, misc_s, aq_s, iq_s) = _inproj_sample(
            xs, mod_s, g_norm1[l], w_nat, w_t, w_conv[l], prow, pcol, state_conv[l], past)
        og_s, ssm_s = _gdn_sample(qkvn_s, z_s, misc_s, g_gdn_norm[l], state_ssm[l], _pick(db, (8,)))
        ik_s = slab_s[:, 0:IDX_DIM]
        iw_s = slab_s[:, IDX_DIM:IDX_DIM + IDX_HEADS]
        scores = _sample_index_scores(iq_s, iw_s, ik_s, jnp.swapaxes(cache_idx_k[l], 1, 2), page_table)
        keep = _sample_select(scores, past + ts)
        oa_s = _sample_attention(aq_s, keep, knat_s, vnat_s, cache_k[l], cache_v[l], page_table)
        spec_s = lambda k: pl.BlockSpec((db, d), lambda i, j: (i, k))
        xs = _post(og_s, oa_s, xs, mod_s, spec_s, g_norm2[l], g_final, wo, wfi, wfo, db, tf, final)
        new_conv_s = jnp.concatenate([state_conv[l][:, 1:, :], raw_s[:, None, :]], axis=1)
        new_s.append((knat_s.reshape(db, ts, N_KV, HEAD_DIM), vnat_s.reshape(db, ts, N_KV, HEAD_DIM),
                      ik_s.reshape(db, ts, IDX_DIM), new_conv_s, ssm_s))

    stack = lambda states, n: jnp.stack([s[n] for s in states], axis=0)
    return (xp, xs.reshape(db, ts, d),
            stack(new_p, 0), stack(new_p, 1), stack(new_p, 2), stack(new_p, 3), stack(new_p, 4),
            stack(new_s, 0), stack(new_s, 1), stack(new_s, 2), stack(new_s, 3), stack(new_s, 4))
```

```python
import functools
import math

import numpy as np
import jax
import jax.numpy as jnp
from jax import lax
from jax.experimental import pallas as pl
from jax.experimental.pallas import tpu as pltpu

F32 = jnp.float32
BF = jnp.bfloat16
I32 = jnp.int32

HEAD_DIM = 128
GDN_HEADS = 4
ATT_HEADS = 4
N_KV = 2
IDX_HEADS = 8
IDX_DIM = 64
CONV_W = 4
TOPK_MAX = 256
QBLOCK = 128
ROPE_THETA = 500000.0
EPS = 1e-6
GDN_W = GDN_HEADS * HEAD_DIM
ATT_W = ATT_HEADS * HEAD_DIM
KV_W = N_KV * HEAD_DIM
IDX_W = IDX_HEADS * IDX_DIM
QKV_W = 3 * GDN_W

GDN_CHUNK = 128
ATT_TK = 256
VT_ROWS = HEAD_DIM + 16
IDX_TK = 512
ATT_BLK = 1024
LANES = 128
NAT_W = QKV_W + GDN_W + KV_W + LANES
T_AQ, T_AK, T_IQ, T_IK, T_MISC, T_AV = 0, 512, 768, 1280, 1344, 1360
T_ROWS = T_AV + KV_W
VMEM_LIMIT = 56 * 1024 * 1024
INT_MIN = np.int32(-2 ** 31)
NEG_INF = float("-inf")


def _params(sem):
    return pltpu.CompilerParams(dimension_semantics=sem, vmem_limit_bytes=VMEM_LIMIT)


def _dot(a, b):
    return jnp.dot(a, b, preferred_element_type=F32)


def _dot_nt(a, b):
    return lax.dot_general(a, b, (((1,), (1,)), ((), ())), preferred_element_type=F32)


def _split3(x):
    hi = x.astype(BF)
    r1 = x - hi.astype(F32)
    mid = r1.astype(BF)
    lo = (r1 - mid.astype(F32)).astype(BF)
    return hi, mid, lo


def _mm_hi(a, b):
    ah = a.astype(BF)
    al = (a - ah.astype(F32)).astype(BF)
    bh = b.astype(BF)
    bl = (b - bh.astype(F32)).astype(BF)
    m = a.shape[0]
    hi = _dot(jnp.concatenate([ah, al], axis=0), bh)
    return hi[0:m] + (hi[m:] + _dot(ah, bl))


def _sigmoid(x):
    return 1.0 / (1.0 + jnp.exp(-x))


def _silu(x):
    return x * _sigmoid(x)


def _softplus(x):
    return jnp.maximum(x, 0.0) + jnp.log(1.0 + jnp.exp(-jnp.abs(x)))


def _rmsnorm(x, g):
    return x * lax.rsqrt(jnp.mean(x * x, axis=-1, keepdims=True) + EPS) * g


def _mod_kernel(c_ref, w_ref, b_ref, o_ref):
    s = _silu(c_ref[...]).astype(BF)
    o_ref[...] = _dot(s, w_ref[...].astype(BF)) + b_ref[...]


def _modulation(c_all, w_ada, b_ada):
    n, d = c_all.shape
    cols = w_ada.shape[1]
    tn = d
    return pl.pallas_call(
        _mod_kernel,
        grid=(cols // tn,),
        in_specs=[pl.BlockSpec((n, d), lambda j: (0, 0)),
                  pl.BlockSpec((d, tn), lambda j: (0, j)),
                  pl.BlockSpec((1, tn), lambda j: (0, j))],
        out_specs=pl.BlockSpec((n, tn), lambda j: (0, j)),
        out_shape=jax.ShapeDtypeStruct((n, cols), F32),
        compiler_params=_params(("arbitrary",)),
        name="modulation",
    )(c_all, w_ada, b_ada.reshape(1, cols))


def _rope_rows(rt_ref, base, half, cos, sin):
    x1 = rt_ref[base:base + half, :]
    x2 = rt_ref[base + half:base + 2 * half, :]
    rt_ref[base:base + half, :] = x1 * cos - x2 * sin
    rt_ref[base + half:base + 2 * half, :] = x2 * cos + x1 * sin


def _project(x_ref, sc_ref, sh_ref, g1_ref, wn_ref, wt_ref, cs128_ref, cs64_ref, pcol_ref, rt_ref):
    h = _rmsnorm(x_ref[...], g1_ref[...]) * (1.0 + sc_ref[...]) + sh_ref[...]
    hb = h.astype(BF)
    nat = _dot(hb, wn_ref[...])
    rt_ref[...] = _dot_nt(wt_ref[...], hb)
    half = HEAD_DIM // 8
    cos, sin = cs128_ref[0:half, :], cs128_ref[half:2 * half, :]
    for hd in range(ATT_HEADS):
        _rope_rows(rt_ref, T_AQ + hd * HEAD_DIM, half, cos, sin)
    for hd in range(N_KV):
        _rope_rows(rt_ref, T_AK + hd * HEAD_DIM, half, cos, sin)
    half = IDX_DIM // 8
    cos, sin = cs64_ref[0:half, :], cs64_ref[half:2 * half, :]
    for hd in range(IDX_HEADS):
        _rope_rows(rt_ref, T_IQ + hd * IDX_DIM, half, cos, sin)
    _rope_rows(rt_ref, T_IK, half, cos, sin)
    mt = rt_ref[T_MISC:T_MISC + 16, :]
    r = lax.broadcasted_iota(I32, mt.shape, 0)
    a_log, dt_b = pcol_ref[:, 0:1], pcol_ref[:, 1:2]
    gate = -jnp.exp(a_log) * _softplus(mt + dt_b)
    mt = jnp.where(r < IDX_HEADS, mt * IDX_HEADS ** -0.5, jnp.where(r < IDX_HEADS + GDN_HEADS, _sigmoid(mt), gate))
    rt_ref[T_MISC:T_MISC + 16, :] = mt
    return nat, mt


def _misc_natural(m, prow_ref):
    lane = lax.broadcasted_iota(I32, m.shape, 1)
    gate = -jnp.exp(prow_ref[0:1, :]) * _softplus(m + prow_ref[1:2, :])
    return jnp.where(lane < GDN_HEADS, _sigmoid(m), jnp.where(lane < 3 * GDN_HEADS, gate, 0.0))


def _qkv_post(conv, qkvn_ref):
    c = _silu(conv)
    for hb in range(2 * GDN_HEADS):
        xh = c[:, hb * HEAD_DIM:(hb + 1) * HEAD_DIM]
        n = xh * lax.rsqrt(jnp.sum(xh * xh, axis=-1, keepdims=True) + EPS)
        if hb < GDN_HEADS:
            n = n * HEAD_DIM ** -0.5
        qkvn_ref[:, hb * HEAD_DIM:(hb + 1) * HEAD_DIM] = n
    qkvn_ref[:, 2 * GDN_W:] = c[:, 2 * GDN_W:]


def _inproj_prompt_kernel(tm, parts, x_ref, sc_ref, sh_ref, g1_ref, wn_ref, wt_ref, wconv_ref, cs128_ref, cs64_ref,
                          prow_ref, pcol_ref, tril_ref,
                          qkvn_ref, z_ref, vnat_ref, knat_ref, ikT_ref, tail_ref, misc_ref, gct_ref,
                          qT_ref, iqT_ref, iwT_ref, kbf_ref, vT_ref, ikbf_ref,
                          xs_ref, rt_ref):
    i = pl.program_id(1)

    @pl.when(i == 0)
    def _():
        xs_ref[0:8, :] = jnp.zeros((8, QKV_W), F32)

    @pl.when(i > 0)
    def _():
        xs_ref[0:8, :] = xs_ref[tm:tm + 8, :]

    th = tm // parts
    for n in range(parts):
        rows, cols = pl.ds(n * th, th), pl.ds(n * th, th)
        _inproj_prompt_rows(
            th, x_ref.at[rows, :], sc_ref, sh_ref, g1_ref, wn_ref, wt_ref, wconv_ref,
            cs128_ref.at[:, cols], cs64_ref.at[:, cols], prow_ref, pcol_ref, tril_ref.at[0:th, 0:th],
            qkvn_ref.at[rows, :], z_ref.at[rows, :], vnat_ref.at[pl.ds(n * th * N_KV, th * N_KV), :],
            knat_ref.at[pl.ds(n * th * N_KV, th * N_KV), :], ikT_ref.at[:, cols], tail_ref, misc_ref.at[rows, :],
            gct_ref.at[pl.ds(n * (th // GDN_CHUNK), th // GDN_CHUNK)],
            qT_ref.at[:, cols], iqT_ref.at[:, cols], iwT_ref.at[:, cols], kbf_ref.at[rows, :],
            vT_ref.at[pl.ds(n * (th // ATT_TK), th // ATT_TK)], ikbf_ref.at[rows, :],
            xs_ref.at[pl.ds(n * th, th + 8), :], rt_ref.at[:, cols])


def _inproj_prompt_rows(tm, x_ref, sc_ref, sh_ref, g1_ref, wn_ref, wt_ref, wconv_ref, cs128_ref, cs64_ref,
                        prow_ref, pcol_ref, tril_ref,
                        qkvn_ref, z_ref, vnat_ref, knat_ref, ikT_ref, tail_ref, misc_ref, gct_ref,
                        qT_ref, iqT_ref, iwT_ref, kbf_ref, vT_ref, ikbf_ref,
                        xs_ref, rt_ref):
    nat, mt = _project(x_ref, sc_ref, sh_ref, g1_ref, wn_ref, wt_ref, cs128_ref, cs64_ref, pcol_ref, rt_ref)
    xs_ref[8:tm + 8, :] = nat[:, 0:QKV_W]
    conv = wconv_ref[0:1, :] * xs_ref[5:tm + 5, :]
    for t in range(1, CONV_W):
        conv = conv + wconv_ref[t:t + 1, :] * xs_ref[5 + t:tm + 5 + t, :]
    tail_ref[...] = xs_ref[tm:tm + 8, :]
    _qkv_post(conv, qkvn_ref)
    z_ref[...] = nat[:, QKV_W:QKV_W + GDN_W]
    v0 = QKV_W + GDN_W
    for hd in range(N_KV):
        vnat_ref[pl.ds(hd, tm, stride=N_KV), :] = nat[:, v0 + hd * HEAD_DIM:v0 + (hd + 1) * HEAD_DIM]

    gm = _misc_natural(nat[:, NAT_W - LANES:], prow_ref)
    tril = tril_ref[...]
    hi, mid, lo = _split3(gm)
    gc = _dot(tril, hi) + (_dot(tril, mid) + _dot(tril, lo))
    lane = lax.broadcasted_iota(I32, gm.shape, 1)
    misc_ref[...] = jnp.where(lane < 2 * GDN_HEADS, gm, gc)
    hi, mid, lo = _split3(mt)
    gct = _dot_nt(hi, tril) + (_dot_nt(mid, tril) + _dot_nt(lo, tril))
    r = lax.broadcasted_iota(I32, mt.shape, 0)
    bg = jnp.where(r < IDX_HEADS + GDN_HEADS, mt, gct)[8:16, :]
    for cc in range(tm // GDN_CHUNK):
        gct_ref[cc] = bg[:, cc * GDN_CHUNK:(cc + 1) * GDN_CHUNK]

    iwT_ref[...] = mt[0:IDX_HEADS, :]
    qT_ref[...] = rt_ref[T_AQ:T_AQ + ATT_W, :].astype(BF)
    iqT_ref[...] = rt_ref[T_IQ:T_IQ + IDX_W, :].astype(BF)
    kn = rt_ref[T_AK:T_AK + KV_W, :].T
    for hd in range(N_KV):
        knat_ref[pl.ds(hd, tm, stride=N_KV), :] = kn[:, hd * HEAD_DIM:(hd + 1) * HEAD_DIM]
    kbf_ref[...] = kn.astype(BF)
    ikT_ref[...] = rt_ref[T_IK:T_IK + IDX_DIM, :]
    ikbf_ref[...] = rt_ref[T_IK:T_IK + LANES, :].T[:, 0:IDX_DIM].astype(BF)
    for cc in range(tm // ATT_TK):
        for hd in range(N_KV):
            lo = hd * VT_ROWS
            vT_ref[cc, lo:lo + HEAD_DIM, :] = rt_ref[T_AV + hd * HEAD_DIM:T_AV + (hd + 1) * HEAD_DIM,
                                                     cc * ATT_TK:(cc + 1) * ATT_TK].astype(BF)
            vT_ref[cc, lo + HEAD_DIM:lo + VT_ROWS, :] = jnp.ones((VT_ROWS - HEAD_DIM, ATT_TK), BF)


def _inproj_sample_kernel(x_ref, sc_ref, sh_ref, g1_ref, wn_ref, wt_ref, wconv_ref, cs128_ref, cs64_ref,
                          prow_ref, pcol_ref, s0_ref, s1_ref, s2_ref,
                          qkvn_ref, z_ref, vnat_ref, knat_ref, slab_ref, raw_ref, misc_ref, aq_ref, iq_ref,
                          rt_ref):
    nat, _ = _project(x_ref, sc_ref, sh_ref, g1_ref, wn_ref, wt_ref, cs128_ref, cs64_ref, pcol_ref, rt_ref)
    raw = nat[:, 0:QKV_W]
    raw_ref[...] = raw
    conv = wconv_ref[0:1, :] * s0_ref[...]
    conv = conv + wconv_ref[1:2, :] * s1_ref[...]
    conv = conv + wconv_ref[2:3, :] * s2_ref[...]
    conv = conv + wconv_ref[3:4, :] * raw
    _qkv_post(conv, qkvn_ref)
    z_ref[...] = nat[:, QKV_W:QKV_W + GDN_W]
    vnat_ref[...] = nat[:, QKV_W + GDN_W:QKV_W + GDN_W + KV_W]
    misc_ref[...] = _misc_natural(nat[:, NAT_W - LANES:], prow_ref)
    aq_ref[...] = rt_ref[T_AQ:T_AQ + ATT_W, :].T
    iq_ref[...] = rt_ref[T_IQ:T_IQ + IDX_W, :].T
    knat_ref[...] = rt_ref[T_AK:T_AK + KV_W, :].T
    slab_ref[...] = rt_ref[T_IK:T_IK + LANES, :].T


def _inproj_weights(w_in, a_log, dt_bias):
    offs = np.cumsum([0, QKV_W, GDN_W, GDN_HEADS, GDN_HEADS, ATT_W, KV_W, KV_W, IDX_W, IDX_DIM, IDX_HEADS])
    qkv, z, beta, a, aq, ak, av, iq, ik, iw = [w_in[:, offs[n]:offs[n + 1]] for n in range(10)]
    d = w_in.shape[0]
    pad = jnp.zeros((d, LANES - 3 * GDN_HEADS), w_in.dtype)
    w_nat = jnp.concatenate([qkv, z, av, beta, a, a, pad], axis=1).astype(BF)
    w_t = jnp.concatenate([aq, ak, iq, ik, iw, beta, a, av], axis=1).T.astype(BF)
    zrow = jnp.zeros((LANES - 3 * GDN_HEADS,), F32)
    prow = jnp.stack([jnp.concatenate([jnp.zeros((GDN_HEADS,), F32), a_log, a_log, zrow]),
                      jnp.concatenate([jnp.zeros((GDN_HEADS,), F32), dt_bias, dt_bias, zrow])])
    z12 = jnp.zeros((IDX_HEADS + GDN_HEADS,), F32)
    pcol = jnp.stack([jnp.concatenate([z12, a_log]), jnp.concatenate([z12, dt_bias])], axis=1)
    return w_nat, w_t, prow, pcol


def _rope_tables(pos, dim):
    half = dim // 8
    inv = ROPE_THETA ** (-jnp.arange(half, dtype=F32) / half)
    ang = pos.astype(F32)[:, None] * inv[None, :]
    return jnp.concatenate([jnp.cos(ang).T, jnp.sin(ang).T], axis=0)


def _inproj_prompt(x, mod, g1, w_nat, w_t, w_conv, prow, pcol, tm):
    bp, tp, d = x.shape
    nt = tp // tm
    rows = bp * tp
    pos = jnp.arange(tp, dtype=I32)
    cs128, cs64 = _rope_tables(pos, HEAD_DIM), _rope_tables(pos, IDX_DIM)
    ri = np.arange(tm)
    tril = jnp.asarray((ri[:, None] // GDN_CHUNK == ri[None, :] // GDN_CHUNK) & (ri[None, :] <= ri[:, None]), BF)
    const = lambda shape: pl.BlockSpec(shape, lambda b, i: (0,) * len(shape))
    rowblk = lambda w: pl.BlockSpec((tm, w), lambda b, i: (b * nt + i, 0))
    colblk = lambda h: pl.BlockSpec((h, tm), lambda b, i: (0, b * nt + i))
    in_specs = [
        pl.BlockSpec((None, tm, d), lambda b, i: (b, i, 0)),
        pl.BlockSpec((None, 1, d), lambda b, i: (b, 0, 1)),
        pl.BlockSpec((None, 1, d), lambda b, i: (b, 0, 0)),
        const((1, d)), const((d, NAT_W)), const((T_ROWS, d)), const((CONV_W, QKV_W)),
        pl.BlockSpec((HEAD_DIM // 4, tm), lambda b, i: (0, i)),
        pl.BlockSpec((IDX_DIM // 4, tm), lambda b, i: (0, i)),
        const((2, LANES)), const((16, 2)), const((tm, tm)),
    ]
    out_shape = [
        jax.ShapeDtypeStruct((rows, QKV_W), F32), jax.ShapeDtypeStruct((rows, GDN_W), F32),
        jax.ShapeDtypeStruct((rows * N_KV, HEAD_DIM), F32), jax.ShapeDtypeStruct((rows * N_KV, HEAD_DIM), F32),
        jax.ShapeDtypeStruct((bp, IDX_DIM, tp), F32), jax.ShapeDtypeStruct((bp, 8, QKV_W), F32),
        jax.ShapeDtypeStruct((rows, LANES), F32), jax.ShapeDtypeStruct((rows // GDN_CHUNK, 8, GDN_CHUNK), F32),
        jax.ShapeDtypeStruct((ATT_W, rows), BF), jax.ShapeDtypeStruct((IDX_W, rows), BF),
        jax.ShapeDtypeStruct((IDX_HEADS, rows), F32), jax.ShapeDtypeStruct((rows, KV_W), BF),
        jax.ShapeDtypeStruct((rows // ATT_TK, N_KV * VT_ROWS, ATT_TK), BF), jax.ShapeDtypeStruct((rows, IDX_DIM), BF),
    ]
    out_specs = [
        rowblk(QKV_W), rowblk(GDN_W),
        pl.BlockSpec((tm * N_KV, HEAD_DIM), lambda b, i: (b * nt + i, 0)),
        pl.BlockSpec((tm * N_KV, HEAD_DIM), lambda b, i: (b * nt + i, 0)),
        pl.BlockSpec((None, IDX_DIM, tm), lambda b, i: (b, 0, i)),
        pl.BlockSpec((None, 8, QKV_W), lambda b, i: (b, 0, 0)),
        rowblk(LANES),
        pl.BlockSpec((tm // GDN_CHUNK, 8, GDN_CHUNK), lambda b, i: (b * nt + i, 0, 0)),
        colblk(ATT_W), colblk(IDX_W), colblk(IDX_HEADS), rowblk(KV_W),
        pl.BlockSpec((tm // ATT_TK, N_KV * VT_ROWS, ATT_TK), lambda b, i: (b * nt + i, 0, 0)),
        rowblk(IDX_DIM),
    ]
    return pl.pallas_call(
        functools.partial(_inproj_prompt_kernel, tm, tm // _pick(tm, (ATT_TK,))),
        grid=(bp, nt), in_specs=in_specs, out_specs=out_specs, out_shape=out_shape,
        scratch_shapes=[pltpu.VMEM((tm + 8, QKV_W), F32), pltpu.VMEM((T_ROWS, tm), F32)],
        compiler_params=_params(("arbitrary", "arbitrary")),
        name="inproj_prompt",
    )(x, mod, mod, g1.reshape(1, d), w_nat, w_t, w_conv, cs128, cs64, prow, pcol, tril)


def _inproj_sample(x, mod, g1, w_nat, w_t, w_conv, prow, pcol, conv_state, past):
    db, d = x.shape
    pos = jnp.full((db,), past, I32)
    cs128, cs64 = _rope_tables(pos, HEAD_DIM), _rope_tables(pos, IDX_DIM)
    full = lambda shape: pl.BlockSpec(shape, lambda i: (0,) * len(shape))
    in_specs = [
        full((db, d)),
        pl.BlockSpec((db, d), lambda i: (0, 1)), pl.BlockSpec((db, d), lambda i: (0, 0)),
        full((1, d)), full((d, NAT_W)), full((T_ROWS, d)), full((CONV_W, QKV_W)),
        full((HEAD_DIM // 4, db)), full((IDX_DIM // 4, db)), full((2, LANES)), full((16, 2)),
        full((db, QKV_W)), full((db, QKV_W)), full((db, QKV_W)),
    ]
    widths = [QKV_W, GDN_W, KV_W, KV_W, LANES, QKV_W, LANES, ATT_W, IDX_W]
    return pl.pallas_call(
        _inproj_sample_kernel,
        grid=(1,), in_specs=in_specs,
        out_specs=[full((db, w)) for w in widths],
        out_shape=[jax.ShapeDtypeStruct((db, w), F32) for w in widths],
        scratch_shapes=[pltpu.VMEM((T_ROWS, db), F32)],
        compiler_params=_params(("arbitrary",)),
        name="inproj_sample",
    )(x, mod, mod, g1.reshape(1, d), w_nat, w_t, w_conv, cs128, cs64, prow, pcol,
      conv_state[:, 0], conv_state[:, 1], conv_state[:, 2])


def _unit_lower_inverses(lows):
    c = lows[0].shape[0]
    ii = lax.broadcasted_iota(I32, (c, c), 0)
    jj = lax.broadcasted_iota(I32, (c, c), 1)
    eye = jnp.where(ii == jj, 1.0, 0.0)
    levels = int(math.log2(c)) - 1
    ts = [eye - low for low in lows]
    ps = [_mm_hi(low, low) for low in lows]
    for lvl in range(levels):
        if lvl == levels - 1:
            ts = [t + _mm_hi(t, p) for t, p in zip(ts, ps)]
        else:
            both = [_mm_hi(jnp.concatenate([t, p], axis=0), p) for t, p in zip(ts, ps)]
            ts = [t + b[0:c] for t, b in zip(ts, both)]
            ps = [b[c:] for b in both]
    return ts


def _gdn_prompt_kernel(tg, qkvn_ref, z_ref, misc_ref, gct_ref, gn_ref, o_ref, ssm_ref, s_ref):
    i = pl.program_id(1)

    @pl.when(i == 0)
    def _():
        s_ref[...] = jnp.zeros(s_ref.shape, F32)

    c = GDN_CHUNK
    ii = lax.broadcasted_iota(I32, (c, c), 0)
    jj = lax.broadcasted_iota(I32, (c, c), 1)
    pairs = [(cc, hd) for cc in range(tg // c) for hd in range(GDN_HEADS)]

    qs, ks, gcs, rhs, lows, intras = {}, {}, {}, {}, [], {}
    for cc, hd in pairs:
        r0, lo = cc * c, hd * HEAD_DIM
        q = qkvn_ref[r0:r0 + c, lo:lo + HEAD_DIM]
        k = qkvn_ref[r0:r0 + c, GDN_W + lo:GDN_W + lo + HEAD_DIM]
        v = qkvn_ref[r0:r0 + c, 2 * GDN_W + lo:2 * GDN_W + lo + HEAD_DIM]
        beta_c = misc_ref[r0:r0 + c, hd:hd + 1]
        gc_c = misc_ref[r0:r0 + c, 2 * GDN_HEADS + hd:2 * GDN_HEADS + hd + 1]
        gc_r = gct_ref[cc, GDN_HEADS + hd:GDN_HEADS + hd + 1, :]
        decay = jnp.where(ii >= jj, jnp.exp(jnp.where(ii >= jj, gc_c - gc_r, 0.0)), 0.0)
        kb = k * beta_c
        both = _dot_nt(jnp.concatenate([kb, q], axis=0).astype(BF), k.astype(BF))
        lows.append(jnp.where(ii > jj, both[0:c] * decay, 0.0))
        intras[cc, hd] = (both[c:] * decay).astype(BF)
        rhs[cc, hd] = jnp.concatenate([v * beta_c, kb * jnp.exp(gc_c)], axis=1).astype(BF)
        qs[cc, hd], ks[cc, hd], gcs[cc, hd] = q, k, gc_c
    ts = _unit_lower_inverses(lows)
    uws = {p: _dot(t.astype(BF), rhs[p]) for p, t in zip(pairs, ts)}

    for cc, hd in pairs:
        r0, lo = cc * c, hd * HEAD_DIM
        q, k, gc_c, uw = qs[cc, hd], ks[cc, hd], gcs[cc, hd], uws[cc, hd]
        s = s_ref[hd]
        ws = _dot(jnp.concatenate([uw[:, HEAD_DIM:], q * jnp.exp(gc_c)], axis=0).astype(BF), s.astype(BF))
        v_new = (uw[:, 0:HEAD_DIM] - ws[0:c]).astype(BF)
        o = ws[c:] + _dot(intras[cc, hd], v_new)
        g_last = gc_c[c - 1:c, :]
        kd = k * jnp.exp(g_last - gc_c)
        s_ref[hd] = s * jnp.exp(g_last) + _dot(kd.T.astype(BF), v_new)
        o = _rmsnorm(o, gn_ref[...]) * _silu(z_ref[r0:r0 + c, lo:lo + HEAD_DIM])
        o_ref[r0:r0 + c, lo:lo + HEAD_DIM] = o.astype(BF)
    ssm_ref[...] = s_ref[...]


def _gdn_prompt(qkvn, z, misc, gct, g_norm, bp, tp, tg):
    nt = tp // tg
    rows = bp * tp
    rowblk = lambda w: pl.BlockSpec((tg, w), lambda b, i: (b * nt + i, 0))
    return pl.pallas_call(
        functools.partial(_gdn_prompt_kernel, tg),
        grid=(bp, nt),
        in_specs=[rowblk(QKV_W), rowblk(GDN_W), rowblk(LANES),
                  pl.BlockSpec((tg // GDN_CHUNK, 8, GDN_CHUNK), lambda b, i: (b * nt + i, 0, 0)),
                  pl.BlockSpec((1, HEAD_DIM), lambda b, i: (0, 0))],
        out_specs=[rowblk(GDN_W),
                   pl.BlockSpec((None, GDN_HEADS, HEAD_DIM, HEAD_DIM), lambda b, i: (b, 0, 0, 0))],
        out_shape=[jax.ShapeDtypeStruct((rows, GDN_W), BF),
                   jax.ShapeDtypeStruct((bp, GDN_HEADS, HEAD_DIM, HEAD_DIM), F32)],
        scratch_shapes=[pltpu.VMEM((GDN_HEADS, HEAD_DIM, HEAD_DIM), F32)],
        compiler_params=_params(("arbitrary", "arbitrary")),
        name="gdn_prompt",
    )(qkvn, z, misc, gct, g_norm.reshape(1, HEAD_DIM))


def _gdn_sample_kernel(nb, qkvn_ref, z_ref, misc_ref, gn_ref, s_ref, o_ref, ssm_ref):
    for bi in range(nb):
        for hd in range(GDN_HEADS):
            lo = hd * HEAD_DIM
            q = qkvn_ref[bi:bi + 1, lo:lo + HEAD_DIM]
            k = qkvn_ref[bi:bi + 1, GDN_W + lo:GDN_W + lo + HEAD_DIM]
            v = qkvn_ref[bi:bi + 1, 2 * GDN_W + lo:2 * GDN_W + lo + HEAD_DIM]
            beta = misc_ref[bi:bi + 1, hd:hd + 1]
            g = misc_ref[bi:bi + 1, GDN_HEADS + hd:GDN_HEADS + hd + 1]
            kcol = jnp.broadcast_to(k, (HEAD_DIM, HEAD_DIM)).T
            qcol = jnp.broadcast_to(q, (HEAD_DIM, HEAD_DIM)).T
            s = s_ref[bi, hd] * jnp.exp(g)
            ks = jnp.sum(kcol * s, axis=0, keepdims=True)
            delta = (v - ks) * beta
            s = s + kcol * delta
            ssm_ref[bi, hd] = s
            o = jnp.sum(qcol * s, axis=0, keepdims=True)
            o = _rmsnorm(o, gn_ref[...]) * _silu(z_ref[bi:bi + 1, lo:lo + HEAD_DIM])
            o_ref[bi:bi + 1, lo:lo + HEAD_DIM] = o.astype(BF)


def _gdn_sample(qkvn, z, misc, g_norm, state, nb):
    db = qkvn.shape[0]
    rowblk = lambda w: pl.BlockSpec((nb, w), lambda i: (i, 0))
    sblk = pl.BlockSpec((nb, GDN_HEADS, HEAD_DIM, HEAD_DIM), lambda i: (i, 0, 0, 0))
    return pl.pallas_call(
        functools.partial(_gdn_sample_kernel, nb),
        grid=(db // nb,),
        in_specs=[rowblk(QKV_W), rowblk(GDN_W), rowblk(LANES), pl.BlockSpec((1, HEAD_DIM), lambda i: (0, 0)), sblk],
        out_specs=[rowblk(GDN_W), sblk],
        out_shape=[jax.ShapeDtypeStruct((db, GDN_W), BF), jax.ShapeDtypeStruct(state.shape, F32)],
        compiler_params=_params(("arbitrary",)),
        name="gdn_sample",
    )(qkvn, z, misc, g_norm.reshape(1, HEAD_DIM), state)


def _ordered_word_to_float(u):
    s = u ^ INT_MIN
    return lax.bitcast_convert_type(s ^ ((s >> 31) & np.int32(0x7FFFFFFF)), F32)


def _select_topk_bias(i_ref, t_ref, nch, tk, k_sel, idx_bits, taken=0.0, dropped=NEG_INF):
    nl = i_ref.shape[1]
    row = lax.broadcasted_iota(I32, (tk, 1), 0)

    def chunk(c):
        return pl.ds(pl.multiple_of(c * tk, tk), tk)

    def count(ref, pred):
        def body(c, acc):
            m = pred(ref[chunk(c), :])
            return acc + jnp.sum(jnp.where(m, 1, 0).astype(I32).reshape(tk // 8, 8, nl), axis=0)
        acc = lax.fori_loop(0, nch, body, jnp.zeros((8, nl), I32))
        return jnp.sum(acc, axis=0, keepdims=True)

    def bit_body(i, carry):
        t_u, n_ge = carry
        cand_u = t_u | jnp.left_shift(jnp.int32(1), 31 - i)
        cand = _ordered_word_to_float(cand_u)
        cnt = count(i_ref, lambda sc: sc >= cand)
        take = cnt >= k_sel
        return jnp.where(take, cand_u, t_u), jnp.where(take, cnt, n_ge)

    t_u, n_ge = lax.fori_loop(0, 32, bit_body, (jnp.zeros((1, nl), I32), jnp.full((1, nl), -1, I32)))
    thr = jnp.where((t_u >> 23) == 0, NEG_INF, _ordered_word_to_float(t_u))

    def tie_break(_):
        need = k_sel - count(i_ref, lambda sc: sc > thr)

        def mark(c, carry):
            t_ref[chunk(c), :] = jnp.where(i_ref[chunk(c), :] == thr, c * tk + row, 2 ** idx_bits)
            return carry

        lax.fori_loop(0, nch, mark, 0)

        def idx_body(i, x):
            cand = x | jnp.left_shift(jnp.int32(1), idx_bits - 1 - i)
            cnt = count(t_ref, lambda idx: idx < cand)
            return jnp.where(cnt < need, cand, x)

        return lax.fori_loop(0, idx_bits, idx_body, jnp.zeros((1, nl), I32))

    x = lax.cond(jnp.max(n_ge) > k_sel, tie_break, lambda _: jnp.full((1, nl), 2 ** idx_bits - 1, I32), 0)

    def write(c, carry):
        sc = i_ref[chunk(c), :]
        sel = (sc > thr) | ((sc == thr) & (c * tk + row <= x))
        i_ref[chunk(c), :] = jnp.where(sel & (sc > NEG_INF), taken, dropped)
        return carry

    lax.fori_loop(0, nch, write, 0)


def _attn_prompt_kernel(tq, tk, blk, k_sel, idx_bits, qT_ref, iqT_ref, iwT_ref, kbf_ref, vT_ref, ikbf_ref, o_ref,
                        i_ref, t_ref):
    t0 = pl.program_id(1) * tq
    nch = (t0 + tq + tk - 1) // tk
    nblk = (t0 + tq + blk - 1) // blk
    qpos = t0 + lax.broadcasted_iota(I32, (1, tq), 1)
    row = lax.broadcasted_iota(I32, (tk, 1), 0)
    w = iwT_ref[...]
    pairs = IDX_HEADS // 2
    rhs = [jnp.concatenate([iqT_ref[(2 * p) * IDX_DIM:(2 * p + 1) * IDX_DIM, :],
                            iqT_ref[(2 * p + 1) * IDX_DIM:(2 * p + 2) * IDX_DIM, :]], axis=1) for p in range(pairs)]

    def chunk(c):
        return pl.ds(pl.multiple_of(c * tk, tk), tk)

    def idx_body(c, carry):
        ikc = ikbf_ref[chunk(c), :]
        acc = jnp.zeros((tk, tq), F32)
        for p in range(pairs):
            d = jnp.maximum(_dot(ikc, rhs[p]), 0.0)
            acc = acc + d[:, 0:tq] * w[2 * p:2 * p + 1, :] + d[:, tq:] * w[2 * p + 1:2 * p + 2, :]
        i_ref[chunk(c), :] = jnp.where(c * tk + row <= qpos, acc, NEG_INF)
        return carry

    lax.fori_loop(0, nch, idx_body, 0)

    def fill_body(c, carry):
        i_ref[chunk(c), :] = jnp.full((tk, tq), NEG_INF, F32)
        return carry

    lax.fori_loop(nch, nblk * (blk // tk), fill_body, 0)
    _select_topk_bias(i_ref, t_ref, nch, tk, k_sel, idx_bits)

    scale2 = HEAD_DIM ** -0.5 * math.log2(math.e)
    rep = ATT_HEADS // N_KV
    sub = ATT_TK
    nsub = blk // sub
    qg = [jnp.concatenate([qT_ref[(rep * g + r) * HEAD_DIM:(rep * g + r + 1) * HEAD_DIM, :] for r in range(rep)],
                          axis=1) for g in range(N_KV)]

    def body(c, carry):
        out = []
        for g in range(N_KV):
            m, acc = carry[g]
            ss = []
            for i in range(nsub):
                rows = pl.ds(pl.multiple_of(c * blk + i * sub, sub), sub)
                bias = i_ref[rows, :]
                ss.append(_dot(kbf_ref[rows, g * HEAD_DIM:(g + 1) * HEAD_DIM], qg[g]) * scale2
                          + jnp.concatenate([bias] * rep, axis=1))
            m_new = m
            for s in ss:
                m_new = jnp.maximum(m_new, jnp.max(s, axis=0, keepdims=True))
            m_safe = jnp.where(m_new == NEG_INF, 0.0, m_new)
            acc = acc * jnp.exp2(m - m_safe)
            for i, s in enumerate(ss):
                p = jnp.exp2(s - m_safe).astype(BF)
                acc = acc + _dot(vT_ref[c * nsub + i, g * VT_ROWS:(g + 1) * VT_ROWS, :], p)
            out.append((m_new, acc))
        return tuple(out)

    init = tuple((jnp.full((1, rep * tq), NEG_INF, F32), jnp.zeros((VT_ROWS, rep * tq), F32)) for _ in range(N_KV))
    res = lax.fori_loop(0, nblk, body, init)
    for g in range(N_KV):
        acc = res[g][1]
        o_t = acc[0:HEAD_DIM] / acc[HEAD_DIM:HEAD_DIM + 1]
        for r in range(rep):
            hd = rep * g + r
            o_ref[:, hd * HEAD_DIM:(hd + 1) * HEAD_DIM] = o_t[:, r * tq:(r + 1) * tq].T.astype(BF)


def _attn_prompt(qT, iqT, iwT, kbf, vT, ikbf, bp, tp):
    tq = QBLOCK
    tk = _pick(tp, (IDX_TK, ATT_TK))
    blk = _pick(tp, (ATT_BLK, IDX_TK, ATT_TK))
    assert blk % tk == 0 and blk % ATT_TK == 0
    nq = tp // tq
    rows = bp * tp
    k_sel = min(TOPK_MAX, tp // 4)
    idx_bits = max(1, int(tp - 1).bit_length())
    colblk = lambda h: pl.BlockSpec((h, tq), lambda b, j: (0, b * nq + j))
    return pl.pallas_call(
        functools.partial(_attn_prompt_kernel, tq, tk, blk, k_sel, idx_bits),
        grid=(bp, nq),
        in_specs=[colblk(ATT_W), colblk(IDX_W), colblk(IDX_HEADS),
                  pl.BlockSpec((tp, KV_W), lambda b, j: (b, 0)),
                  pl.BlockSpec((tp // ATT_TK, N_KV * VT_ROWS, ATT_TK), lambda b, j: (b, 0, 0)),
                  pl.BlockSpec((tp, IDX_DIM), lambda b, j: (b, 0))],
        out_specs=pl.BlockSpec((tq, ATT_W), lambda b, j: (b * nq + j, 0)),
        out_shape=jax.ShapeDtypeStruct((rows, ATT_W), BF),
        scratch_shapes=[pltpu.VMEM((tp, tq), F32), pltpu.VMEM((tp, tq), I32)],
        compiler_params=_params(("arbitrary", "arbitrary")),
        name="attn_prompt",
    )(qT, iqT, iwT, kbf, vT, ikbf)


def _sidx_kernel(npg, pt_ref, iq_ref, iw_ref, ikn_ref, *refs):
    pages, out_ref = refs[:npg], refs[npg]
    iq = iq_ref[...].astype(BF)
    w = iw_ref[...]
    for p in range(npg):
        d = jnp.maximum(_dot(iq, pages[p][...].astype(BF)), 0.0)
        out_ref[:, p * LANES:(p + 1) * LANES] = jnp.sum(d * w, axis=0, keepdims=True)
    dn = jnp.sum(iq.astype(F32) * ikn_ref[...].astype(BF).astype(F32), axis=1, keepdims=True)
    sn = jnp.sum(jnp.maximum(dn, 0.0) * w, axis=0, keepdims=True)
    lane = lax.broadcasted_iota(I32, (1, LANES), 1)
    out_ref[:, npg * LANES:(npg + 1) * LANES] = jnp.where(lane == 0, sn, NEG_INF)


def _sample_index_scores(iq, iw, ik_new, cache_ik_t, page_table):
    db, npg = page_table.shape
    page = cache_ik_t.shape[2]
    assert page == LANES
    width = (npg + 1) * LANES
    page_specs = [pl.BlockSpec((None, IDX_DIM, page), functools.partial(lambda p, b, pt: (pt[b, p], 0, 0), p))
                  for p in range(npg)]
    grid_spec = pltpu.PrefetchScalarGridSpec(
        num_scalar_prefetch=1, grid=(db,),
        in_specs=[pl.BlockSpec((None, IDX_HEADS, IDX_DIM), lambda b, pt: (b, 0, 0)),
                  pl.BlockSpec((None, IDX_HEADS, 1), lambda b, pt: (b, 0, 0)),
                  pl.BlockSpec((None, 1, IDX_DIM), lambda b, pt: (b, 0, 0))] + page_specs,
        out_specs=pl.BlockSpec((None, 1, width), lambda b, pt: (b, 0, 0)))
    out = pl.pallas_call(
        functools.partial(_sidx_kernel, npg), grid_spec=grid_spec,
        out_shape=jax.ShapeDtypeStruct((db, 1, width), F32),
        compiler_params=_params(("arbitrary",)),
        name="sample_index_scores",
    )(page_table, iq.reshape(db, IDX_HEADS, IDX_DIM), iw.reshape(db, IDX_HEADS, 1), ik_new.reshape(db, 1, IDX_DIM),
      *([cache_ik_t] * npg))
    return out.reshape(db, width)


def _ssel_kernel(nch, k_sel, idx_bits, s_ref, spread_ref, keep_ref, i_ref, t_ref):
    for p in range(nch):
        i_ref[p * LANES:(p + 1) * LANES, :] = s_ref[:, p * LANES:(p + 1) * LANES].T
    _select_topk_bias(i_ref, t_ref, nch, LANES, k_sel, idx_bits, taken=1.0, dropped=0.0)
    rows = N_KV * LANES
    for p in range(nch - 1):
        k2 = _dot(spread_ref[...], i_ref[p * LANES:(p + 1) * LANES, :].astype(BF))
        for h in range(N_KV):
            keep_ref[:, p * rows + h * LANES:p * rows + (h + 1) * LANES] = k2[h * LANES:(h + 1) * LANES, :].T
    keep_ref[:, (nch - 1) * rows:(nch - 1) * rows + LANES] = i_ref[(nch - 1) * LANES:nch * LANES, :].T


def _sample_select(scores, n_keys):
    db, width = scores.shape
    assert db == LANES
    nch = width // LANES
    k_sel = min(TOPK_MAX, n_keys // 4)
    idx_bits = max(1, int(width - 1).bit_length())
    rows = N_KV * LANES
    out_w = (nch - 1) * rows + LANES
    spread = jnp.asarray(np.arange(rows)[:, None] // N_KV == np.arange(LANES)[None, :], BF)
    return pl.pallas_call(
        functools.partial(_ssel_kernel, nch, k_sel, idx_bits),
        grid=(1,),
        in_specs=[pl.BlockSpec((db, width), lambda i: (0, 0)), pl.BlockSpec((rows, LANES), lambda i: (0, 0))],
        out_specs=pl.BlockSpec((db, out_w), lambda i: (0, 0)),
        out_shape=jax.ShapeDtypeStruct((db, out_w), F32),
        scratch_shapes=[pltpu.VMEM((width, db), F32), pltpu.VMEM((width, db), I32)],
        compiler_params=_params(("arbitrary",)),
        name="sample_select",
    )(scores, spread)


def _sattn_kernel(npg, pt_ref, q_ref, keep_ref, kn_ref, vn_ref, *refs):
    kpages, vpages, o_ref = refs[:npg], refs[npg:2 * npg], refs[2 * npg]
    rep = ATT_HEADS // N_KV
    rows = N_KV * LANES
    scale = HEAD_DIM ** -0.5
    q = q_ref[...]
    q8 = jnp.concatenate([q, jnp.zeros((8 - ATT_HEADS, HEAD_DIM), F32)], axis=0).astype(BF)
    hrow = lax.broadcasted_iota(I32, (8, 1), 0)
    lane = lax.broadcasted_iota(I32, (1, LANES), 1)
    own_kv = lax.broadcasted_iota(I32, (1, rows), 1) % N_KV == hrow // rep
    parts = []
    for p in range(npg):
        s2 = _dot_nt(q8, kpages[p][...].astype(BF))
        keep2 = keep_ref[:, p * rows:(p + 1) * rows]
        parts.append(jnp.where(own_kv & (keep2 > 0.5), s2 * scale, NEG_INF))
    kn = kn_ref[...].astype(BF).astype(F32)
    kn8 = jnp.where(hrow // rep == 0, kn[0:1, :], kn[1:2, :])
    s_new = jnp.sum(q8.astype(F32) * kn8, axis=1, keepdims=True)
    keep_new = keep_ref[:, npg * rows:npg * rows + 1]
    parts.append(jnp.where((lane == 0) & (keep_new > 0.5), s_new * scale, NEG_INF))
    s = jnp.concatenate(parts, axis=1)
    m = jnp.max(s, axis=1, keepdims=True)
    e = jnp.exp(s - m)
    pr = (e / jnp.sum(e, axis=1, keepdims=True)).astype(BF)
    o8 = jnp.zeros((8, HEAD_DIM), F32)
    for p in range(npg):
        o8 = o8 + _dot(pr[:, p * rows:(p + 1) * rows], vpages[p][...].astype(BF))
    vn = vn_ref[...].astype(BF).astype(F32)
    vn8 = jnp.where(hrow // rep == 0, vn[0:1, :], vn[1:2, :])
    o8 = o8 + pr[:, npg * rows:npg * rows + 1].astype(F32) * vn8
    for hd in range(ATT_HEADS):
        o_ref[:, hd * HEAD_DIM:(hd + 1) * HEAD_DIM] = o8[hd:hd + 1, :].astype(BF)


def _sample_attention(q, keep, k_new, v_new, cache_k, cache_v, page_table):
    db, npg = page_table.shape
    n_phys, page = cache_k.shape[0], cache_k.shape[1]
    assert page == LANES and N_KV == 2
    rows = page * N_KV
    width = npg * rows + LANES
    ck = cache_k.reshape(n_phys, rows, HEAD_DIM)
    cv = cache_v.reshape(n_phys, rows, HEAD_DIM)
    page_specs = [pl.BlockSpec((None, rows, HEAD_DIM), functools.partial(lambda p, b, pt: (pt[b, p], 0, 0), p))
                  for p in range(npg)]
    grid_spec = pltpu.PrefetchScalarGridSpec(
        num_scalar_prefetch=1, grid=(db,),
        in_specs=[pl.BlockSpec((None, ATT_HEADS, HEAD_DIM), lambda b, pt: (b, 0, 0)),
                  pl.BlockSpec((None, 1, width), lambda b, pt: (b, 0, 0)),
                  pl.BlockSpec((None, N_KV, HEAD_DIM), lambda b, pt: (b, 0, 0)),
                  pl.BlockSpec((None, N_KV, HEAD_DIM), lambda b, pt: (b, 0, 0))] + page_specs + page_specs,
        out_specs=pl.BlockSpec((None, 1, ATT_W), lambda b, pt: (b, 0, 0)))
    out = pl.pallas_call(
        functools.partial(_sattn_kernel, npg), grid_spec=grid_spec,
        out_shape=jax.ShapeDtypeStruct((db, 1, ATT_W), BF),
        compiler_params=_params(("arbitrary",)),
        name="sample_attention",
    )(page_table, q.reshape(db, ATT_HEADS, HEAD_DIM), keep.reshape(db, 1, width),
      k_new.reshape(db, N_KV, HEAD_DIM), v_new.reshape(db, N_KV, HEAD_DIM), *([ck] * npg), *([cv] * npg))
    return out.reshape(db, ATT_W)


def _post_kernel(final, nff, og_ref, oa_ref, x_ref, ga1_ref, sc2_ref, sh2_ref, ga2_ref, g2_ref, gf_ref,
                 wo_ref, wg_ref, wu_ref, wd_ref, y_ref, x1_ref, h2_ref, acc_ref):
    jf = pl.program_id(1)

    @pl.when(jf == 0)
    def _():
        mixed = _dot(og_ref[...], wo_ref[0:GDN_W, :]) + _dot(oa_ref[...], wo_ref[GDN_W:, :])
        x1 = x_ref[...] + ga1_ref[...] * mixed
        x1_ref[...] = x1
        h2_ref[...] = (_rmsnorm(x1, g2_ref[...]) * (1.0 + sc2_ref[...]) + sh2_ref[...]).astype(BF)
        acc_ref[...] = jnp.zeros(acc_ref.shape, F32)

    h2 = h2_ref[...]
    act = (_silu(_dot(h2, wg_ref[...])) * _dot(h2, wu_ref[...])).astype(BF)
    acc_ref[...] += _dot(act, wd_ref[...])

    @pl.when(jf == nff - 1)
    def _():
        x2 = x1_ref[...] + ga2_ref[...] * acc_ref[...]
        y_ref[...] = _rmsnorm(x2, gf_ref[...]) if final else x2


def _post(og, oa, x2d, mod, mod_spec, g2, gf, wo, wfi, wfo, tm, tf, final):
    rows, d = x2d.shape
    dff = wfo.shape[0]
    nff = dff // tf
    rowblk = lambda w: pl.BlockSpec((tm, w), lambda i, j: (i, 0))
    const = lambda shape: pl.BlockSpec(shape, lambda i, j: (0,) * len(shape))
    in_specs = [rowblk(GDN_W), rowblk(ATT_W), rowblk(d),
                mod_spec(2), mod_spec(4), mod_spec(3), mod_spec(5),
                const((1, d)), const((1, d)), const((d, d)),
                pl.BlockSpec((d, tf), lambda i, j: (0, j)),
                pl.BlockSpec((d, tf), lambda i, j: (0, nff + j)),
                pl.BlockSpec((tf, d), lambda i, j: (j, 0))]
    return pl.pallas_call(
        functools.partial(_post_kernel, final, nff),
        grid=(rows // tm, nff), in_specs=in_specs, out_specs=rowblk(d),
        out_shape=jax.ShapeDtypeStruct((rows, d), F32),
        scratch_shapes=[pltpu.VMEM((tm, d), F32), pltpu.VMEM((tm, d), BF), pltpu.VMEM((tm, d), F32)],
        compiler_params=_params(("arbitrary", "arbitrary")),
        name="post",
    )(og, oa, x2d, mod, mod, mod, mod, g2.reshape(1, d), gf.reshape(1, d), wo, wfi, wfi, wfo)


def _pick(n, prefs):
    for p in prefs:
        if n % p == 0:
            return p
    return n


def kernel(x_prompt, x_sample, c_prompt, c_sample, cache_k, cache_v, cache_idx_k, page_table, state_conv, state_ssm,
           w_ada, b_ada, g_norm1, w_in, w_conv, a_log, dt_bias, g_gdn_norm, w_out, g_norm2, w_ffn_in, w_ffn_out,
           g_final):
    bp, tp, d = x_prompt.shape
    db, ts, _ = x_sample.shape
    assert ts == 1 and d == GDN_W + ATT_W and tp % max(ATT_TK, GDN_CHUNK) == 0
    depth = w_in.shape[0]
    npg, page = page_table.shape[1], cache_k.shape[2]
    past = npg * page
    tm_in = _pick(tp, (512, 256))
    tg = _pick(tp, (512, 256))
    tm_post = _pick(tp, (1024, 512, 256))
    dff = w_ffn_out.shape[1]
    tf = _pick(dff, (256, 128))

    xp = x_prompt
    xs = x_sample.reshape(db, d)
    npad = (-(bp + db)) % 8
    c_all = jnp.concatenate([c_prompt, c_sample, jnp.zeros((npad, d), F32)], axis=0)
    new_p, new_s = [], []
    for l in range(depth):
        final = l == depth - 1
        mod = _modulation(c_all, w_ada[l], b_ada[l])
        mod_p = mod[:bp].reshape(bp, 1, 6 * d)
        mod_s = mod[bp:bp + db]
        w_nat, w_t, prow, pcol = _inproj_weights(w_in[l], a_log[l], dt_bias[l])
        wo = w_out[l].astype(BF)
        wfi = w_ffn_in[l].astype(BF)
        wfo = w_ffn_out[l].astype(BF)

        (qkvn, z, vnat, knat, ikT, tail, misc, gct, qT, iqT, iwT, kbf, vT, ikbf) = _inproj_prompt(
            xp, mod_p, g_norm1[l], w_nat, w_t, w_conv[l], prow, pcol, tm_in)
        og, ssm_p = _gdn_prompt(qkvn, z, misc, gct, g_gdn_norm[l], bp, tp, tg)
        oa = _attn_prompt(qT, iqT, iwT, kbf, vT, ikbf, bp, tp)
        tiles_b = tp // tm_post
        spec_p = lambda k: pl.BlockSpec((None, 1, d), lambda i, j: (i // tiles_b, 0, k))
        xp = _post(og, oa, xp.reshape(bp * tp, d), mod_p, spec_p, g_norm2[l], g_final, wo, wfi, wfo,
                   tm_post, tf, final).reshape(bp, tp, d)
        new_p.append((knat.reshape(bp, tp, N_KV, HEAD_DIM), vnat.reshape(bp, tp, N_KV, HEAD_DIM),
                      jnp.swapaxes(ikT, 1, 2), tail[:, 8 - (CONV_W - 1):, :], ssm_p))

        (qkvn_s, z_s, vnat_s, knat_s, slab_s, raw_s, misc_s, aq_s, iq_s) = _inproj_sample(
            xs, mod_s, g_norm1[l], w_nat, w_t, w_conv[l], prow, pcol, state_conv[l], past)
        og_s, ssm_s = _gdn_sample(qkvn_s, z_s, misc_s, g_gdn_norm[l], state_ssm[l], _pick(db, (8,)))
        ik_s = slab_s[:, 0:IDX_DIM]
        iw_s = slab_s[:, IDX_DIM:IDX_DIM + IDX_HEADS]
        scores = _sample_index_scores(iq_s, iw_s, ik_s, jnp.swapaxes(cache_idx_k[l], 1, 2), page_table)
        keep = _sample_select(scores, past + ts)
        oa_s = _sample_attention(aq_s, keep, knat_s, vnat_s, cache_k[l], cache_v[l], page_table)
        spec_s = lambda k: pl.BlockSpec((db, d), lambda i, j: (i, k))
        xs = _post(og_s, oa_s, xs, mod_s, spec_s, g_norm2[l], g_final, wo, wfi, wfo, db, tf, final)
        new_conv_s = jnp.concatenate([state_conv[l][:, 1:, :], raw_s[:, None, :]], axis=1)
        new_s.append((knat_s.reshape(db, ts, N_KV, HEAD_DIM), vnat_s.reshape(db, ts, N_KV, HEAD_DIM),
                      ik_s.reshape(db, ts, IDX_DIM), new_conv_s, ssm_s))

    stack = lambda states, n: jnp.stack([s[n] for s in states], axis=0)
    return (xp, xs.reshape(db, ts, d),
            stack(new_p, 0), stack(new_p, 1), stack(new_p, 2), stack(new_p, 3), stack(new_p, 4),
            stack(new_s, 0), stack(new_s, 1), stack(new_s, 2), stack(new_s, 3), stack(new_s, 4))
```

```python
import functools
import math

import numpy as np
import jax
import jax.numpy as jnp
from jax import lax
from jax.experimental import pallas as pl
from jax.experimental.pallas import tpu as pltpu

F32 = jnp.float32
BF = jnp.bfloat16
I32 = jnp.int32

HEAD_DIM = 128
GDN_HEADS = 4
ATT_HEADS = 4
N_KV = 2
IDX_HEADS = 8
IDX_DIM = 64
CONV_W = 4
TOPK_MAX = 256
QBLOCK = 128
ROPE_THETA = 500000.0
EPS = 1e-6
GDN_W = GDN_HEADS * HEAD_DIM
ATT_W = ATT_HEADS * HEAD_DIM
KV_W = N_KV * HEAD_DIM
IDX_W = IDX_HEADS * IDX_DIM
QKV_W = 3 * GDN_W

GDN_CHUNK = 128
ATT_TK = 256
VT_ROWS = HEAD_DIM + 16
IDX_TK = 512
ATT_BLK = 1024
LANES = 128
NAT_W = QKV_W + GDN_W + KV_W + LANES
T_AQ, T_AK, T_IQ, T_IK, T_MISC, T_AV = 0, 512, 768, 1280, 1344, 1360
T_ROWS = T_AV + KV_W
VMEM_LIMIT = 56 * 1024 * 1024
INT_MIN = np.int32(-2 ** 31)
NEG_INF = float("-inf")


def _params(sem):
    return pltpu.CompilerParams(dimension_semantics=sem, vmem_limit_bytes=VMEM_LIMIT)


def _dot(a, b):
    return jnp.dot(a, b, preferred_element_type=F32)


def _dot_nt(a, b):
    return lax.dot_general(a, b, (((1,), (1,)), ((), ())), preferred_element_type=F32)


def _split3(x):
    hi = x.astype(BF)
    r1 = x - hi.astype(F32)
    mid = r1.astype(BF)
    lo = (r1 - mid.astype(F32)).astype(BF)
    return hi, mid, lo


def _mm_hi(a, b):
    ah = a.astype(BF)
    al = (a - ah.astype(F32)).astype(BF)
    bh = b.astype(BF)
    bl = (b - bh.astype(F32)).astype(BF)
    m = a.shape[0]
    hi = _dot(jnp.concatenate([ah, al], axis=0), bh)
    return hi[0:m] + (hi[m:] + _dot(ah, bl))


def _sigmoid(x):
    return 1.0 / (1.0 + jnp.exp(-x))


def _silu(x):
    return x * _sigmoid(x)


def _softplus(x):
    return jnp.maximum(x, 0.0) + jnp.log(1.0 + jnp.exp(-jnp.abs(x)))


def _rmsnorm(x, g):
    return x * lax.rsqrt(jnp.mean(x * x, axis=-1, keepdims=True) + EPS) * g


def _mod_kernel(c_ref, w_ref, b_ref, o_ref):
    s = _silu(c_ref[...]).astype(BF)
    o_ref[...] = _dot(s, w_ref[...].astype(BF)) + b_ref[...]


def _modulation(c_all, w_ada, b_ada):
    n, d = c_all.shape
    cols = w_ada.shape[1]
    tn = d
    return pl.pallas_call(
        _mod_kernel,
        grid=(cols // tn,),
        in_specs=[pl.BlockSpec((n, d), lambda j: (0, 0)),
                  pl.BlockSpec((d, tn), lambda j: (0, j)),
                  pl.BlockSpec((1, tn), lambda j: (0, j))],
        out_specs=pl.BlockSpec((n, tn), lambda j: (0, j)),
        out_shape=jax.ShapeDtypeStruct((n, cols), F32),
        compiler_params=_params(("arbitrary",)),
        name="modulation",
    )(c_all, w_ada, b_ada.reshape(1, cols))


def _rope_rows(rt_ref, base, half, cos, sin):
    x1 = rt_ref[base:base + half, :]
    x2 = rt_ref[base + half:base + 2 * half, :]
    rt_ref[base:base + half, :] = x1 * cos - x2 * sin
    rt_ref[base + half:base + 2 * half, :] = x2 * cos + x1 * sin


def _project(x_ref, sc_ref, sh_ref, g1_ref, wn_ref, wt_ref, cs128_ref, cs64_ref, pcol_ref, rt_ref):
    h = _rmsnorm(x_ref[...], g1_ref[...]) * (1.0 + sc_ref[...]) + sh_ref[...]
    hb = h.astype(BF)
    nat = _dot(hb, wn_ref[...])
    rt_ref[...] = _dot_nt(wt_ref[...], hb)
    half = HEAD_DIM // 8
    cos, sin = cs128_ref[0:half, :], cs128_ref[half:2 * half, :]
    for hd in range(ATT_HEADS):
        _rope_rows(rt_ref, T_AQ + hd * HEAD_DIM, half, cos, sin)
    for hd in range(N_KV):
        _rope_rows(rt_ref, T_AK + hd * HEAD_DIM, half, cos, sin)
    half = IDX_DIM // 8
    cos, sin = cs64_ref[0:half, :], cs64_ref[half:2 * half, :]
    for hd in range(IDX_HEADS):
        _rope_rows(rt_ref, T_IQ + hd * IDX_DIM, half, cos, sin)
    _rope_rows(rt_ref, T_IK, half, cos, sin)
    mt = rt_ref[T_MISC:T_MISC + 16, :]
    r = lax.broadcasted_iota(I32, mt.shape, 0)
    a_log, dt_b = pcol_ref[:, 0:1], pcol_ref[:, 1:2]
    gate = -jnp.exp(a_log) * _softplus(mt + dt_b)
    mt = jnp.where(r < IDX_HEADS, mt * IDX_HEADS ** -0.5, jnp.where(r < IDX_HEADS + GDN_HEADS, _sigmoid(mt), gate))
    rt_ref[T_MISC:T_MISC + 16, :] = mt
    return nat, mt


def _misc_natural(m, prow_ref):
    lane = lax.broadcasted_iota(I32, m.shape, 1)
    gate = -jnp.exp(prow_ref[0:1, :]) * _softplus(m + prow_ref[1:2, :])
    return jnp.where(lane < GDN_HEADS, _sigmoid(m), jnp.where(lane < 3 * GDN_HEADS, gate, 0.0))


def _qkv_post(conv, qkvn_ref):
    c = _silu(conv)
    for hb in range(2 * GDN_HEADS):
        xh = c[:, hb * HEAD_DIM:(hb + 1) * HEAD_DIM]
        n = xh * lax.rsqrt(jnp.sum(xh * xh, axis=-1, keepdims=True) + EPS)
        if hb < GDN_HEADS:
            n = n * HEAD_DIM ** -0.5
        qkvn_ref[:, hb * HEAD_DIM:(hb + 1) * HEAD_DIM] = n
    qkvn_ref[:, 2 * GDN_W:] = c[:, 2 * GDN_W:]


def _inproj_prompt_kernel(tm, parts, x_ref, sc_ref, sh_ref, g1_ref, wn_ref, wt_ref, wconv_ref, cs128_ref, cs64_ref,
                          prow_ref, pcol_ref, tril_ref,
                          qkvn_ref, z_ref, vnat_ref, knat_ref, ikT_ref, tail_ref, misc_ref, gct_ref,
                          qT_ref, iqT_ref, iwT_ref, kbf_ref, vT_ref, ikbf_ref,
                          xs_ref, rt_ref):
    i = pl.program_id(1)

    @pl.when(i == 0)
    def _():
        xs_ref[0:8, :] = jnp.zeros((8, QKV_W), F32)

    @pl.when(i > 0)
    def _():
        xs_ref[0:8, :] = xs_ref[tm:tm + 8, :]

    th = tm // parts
    for n in range(parts):
        rows, cols = pl.ds(n * th, th), pl.ds(n * th, th)
        _inproj_prompt_rows(
            th, x_ref.at[rows, :], sc_ref, sh_ref, g1_ref, wn_ref, wt_ref, wconv_ref,
            cs128_ref.at[:, cols], cs64_ref.at[:, cols], prow_ref, pcol_ref, tril_ref.at[0:th, 0:th],
            qkvn_ref.at[rows, :], z_ref.at[rows, :], vnat_ref.at[pl.ds(n * th * N_KV, th * N_KV), :],
            knat_ref.at[pl.ds(n * th * N_KV, th * N_KV), :], ikT_ref.at[:, cols], tail_ref, misc_ref.at[rows, :],
            gct_ref.at[pl.ds(n * (th // GDN_CHUNK), th // GDN_CHUNK)],
            qT_ref.at[:, cols], iqT_ref.at[:, cols], iwT_ref.at[:, cols], kbf_ref.at[rows, :],
            vT_ref.at[pl.ds(n * (th // ATT_TK), th // ATT_TK)], ikbf_ref.at[rows, :],
            xs_ref.at[pl.ds(n * th, th + 8), :], rt_ref.at[:, cols])


def _inproj_prompt_rows(tm, x_ref, sc_ref, sh_ref, g1_ref, wn_ref, wt_ref, wconv_ref, cs128_ref, cs64_ref,
                        prow_ref, pcol_ref, tril_ref,
                        qkvn_ref, z_ref, vnat_ref, knat_ref, ikT_ref, tail_ref, misc_ref, gct_ref,
                        qT_ref, iqT_ref, iwT_ref, kbf_ref, vT_ref, ikbf_ref,
                        xs_ref, rt_ref):
    nat, mt = _project(x_ref, sc_ref, sh_ref, g1_ref, wn_ref, wt_ref, cs128_ref, cs64_ref, pcol_ref, rt_ref)
    xs_ref[8:tm + 8, :] = nat[:, 0:QKV_W]
    conv = wconv_ref[0:1, :] * xs_ref[5:tm + 5, :]
    for t in range(1, CONV_W):
        conv = conv + wconv_ref[t:t + 1, :] * xs_ref[5 + t:tm + 5 + t, :]
    tail_ref[...] = xs_ref[tm:tm + 8, :]
    _qkv_post(conv, qkvn_ref)
    z_ref[...] = nat[:, QKV_W:QKV_W + GDN_W]
    v0 = QKV_W + GDN_W
    for hd in range(N_KV):
        vnat_ref[pl.ds(hd, tm, stride=N_KV), :] = nat[:, v0 + hd * HEAD_DIM:v0 + (hd + 1) * HEAD_DIM]

    gm = _misc_natural(nat[:, NAT_W - LANES:], prow_ref)
    tril = tril_ref[...]
    hi, mid, lo = _split3(gm)
    gc = _dot(tril, hi) + (_dot(tril, mid) + _dot(tril, lo))
    lane = lax.broadcasted_iota(I32, gm.shape, 1)
    misc_ref[...] = jnp.where(lane < 2 * GDN_HEADS, gm, gc)
    hi, mid, lo = _split3(mt)
    gct = _dot_nt(hi, tril) + (_dot_nt(mid, tril) + _dot_nt(lo, tril))
    r = lax.broadcasted_iota(I32, mt.shape, 0)
    bg = jnp.where(r < IDX_HEADS + GDN_HEADS, mt, gct)[8:16, :]
    for cc in range(tm // GDN_CHUNK):
        gct_ref[cc] = bg[:, cc * GDN_CHUNK:(cc + 1) * GDN_CHUNK]

    iwT_ref[...] = mt[0:IDX_HEADS, :]
    qT_ref[...] = rt_ref[T_AQ:T_AQ + ATT_W, :].astype(BF)
    iqT_ref[...] = rt_ref[T_IQ:T_IQ + IDX_W, :].astype(BF)
    kn = rt_ref[T_AK:T_AK + KV_W, :].T
    for hd in range(N_KV):
        knat_ref[pl.ds(hd, tm, stride=N_KV), :] = kn[:, hd * HEAD_DIM:(hd + 1) * HEAD_DIM]
    kbf_ref[...] = kn.astype(BF)
    ikT_ref[...] = rt_ref[T_IK:T_IK + IDX_DIM, :]
    ikbf_ref[...] = rt_ref[T_IK:T_IK + LANES, :].T[:, 0:IDX_DIM].astype(BF)
    for cc in range(tm // ATT_TK):
        for hd in range(N_KV):
            lo = hd * VT_ROWS
            vT_ref[cc, lo:lo + HEAD_DIM, :] = rt_ref[T_AV + hd * HEAD_DIM:T_AV + (hd + 1) * HEAD_DIM,
                                                     cc * ATT_TK:(cc + 1) * ATT_TK].astype(BF)
            vT_ref[cc, lo + HEAD_DIM:lo + VT_ROWS, :] = jnp.ones((VT_ROWS - HEAD_DIM, ATT_TK), BF)


def _inproj_sample_kernel(x_ref, sc_ref, sh_ref, g1_ref, wn_ref, wt_ref, wconv_ref, cs128_ref, cs64_ref,
                          prow_ref, pcol_ref, s0_ref, s1_ref, s2_ref,
                          qkvn_ref, z_ref, vnat_ref, knat_ref, slab_ref, raw_ref, misc_ref, aq_ref, iq_ref,
                          rt_ref):
    nat, _ = _project(x_ref, sc_ref, sh_ref, g1_ref, wn_ref, wt_ref, cs128_ref, cs64_ref, pcol_ref, rt_ref)
    raw = nat[:, 0:QKV_W]
    raw_ref[...] = raw
    conv = wconv_ref[0:1, :] * s0_ref[...]
    conv = conv + wconv_ref[1:2, :] * s1_ref[...]
    conv = conv + wconv_ref[2:3, :] * s2_ref[...]
    conv = conv + wconv_ref[3:4, :] * raw
    _qkv_post(conv, qkvn_ref)
    z_ref[...] = nat[:, QKV_W:QKV_W + GDN_W]
    vnat_ref[...] = nat[:, QKV_W + GDN_W:QKV_W + GDN_W + KV_W]
    misc_ref[...] = _misc_natural(nat[:, NAT_W - LANES:], prow_ref)
    aq_ref[...] = rt_ref[T_AQ:T_AQ + ATT_W, :].T
    iq_ref[...] = rt_ref[T_IQ:T_IQ + IDX_W, :].T
    knat_ref[...] = rt_ref[T_AK:T_AK + KV_W, :].T
    slab_ref[...] = rt_ref[T_IK:T_IK + LANES, :].T


def _inproj_weights(w_in, a_log, dt_bias):
    offs = np.cumsum([0, QKV_W, GDN_W, GDN_HEADS, GDN_HEADS, ATT_W, KV_W, KV_W, IDX_W, IDX_DIM, IDX_HEADS])
    qkv, z, beta, a, aq, ak, av, iq, ik, iw = [w_in[:, offs[n]:offs[n + 1]] for n in range(10)]
    d = w_in.shape[0]
    pad = jnp.zeros((d, LANES - 3 * GDN_HEADS), w_in.dtype)
    w_nat = jnp.concatenate([qkv, z, av, beta, a, a, pad], axis=1).astype(BF)
    w_t = jnp.concatenate([aq, ak, iq, ik, iw, beta, a, av], axis=1).T.astype(BF)
    zrow = jnp.zeros((LANES - 3 * GDN_HEADS,), F32)
    prow = jnp.stack([jnp.concatenate([jnp.zeros((GDN_HEADS,), F32), a_log, a_log, zrow]),
                      jnp.concatenate([jnp.zeros((GDN_HEADS,), F32), dt_bias, dt_bias, zrow])])
    z12 = jnp.zeros((IDX_HEADS + GDN_HEADS,), F32)
    pcol = jnp.stack([jnp.concatenate([z12, a_log]), jnp.concatenate([z12, dt_bias])], axis=1)
    return w_nat, w_t, prow, pcol


def _rope_tables(pos, dim):
    half = dim // 8
    inv = ROPE_THETA ** (-jnp.arange(half, dtype=F32) / half)
    ang = pos.astype(F32)[:, None] * inv[None, :]
    return jnp.concatenate([jnp.cos(ang).T, jnp.sin(ang).T], axis=0)


def _inproj_prompt(x, mod, g1, w_nat, w_t, w_conv, prow, pcol, tm):
    bp, tp, d = x.shape
    nt = tp // tm
    rows = bp * tp
    pos = jnp.arange(tp, dtype=I32)
    cs128, cs64 = _rope_tables(pos, HEAD_DIM), _rope_tables(pos, IDX_DIM)
    ri = np.arange(tm)
    tril = jnp.asarray((ri[:, None] // GDN_CHUNK == ri[None, :] // GDN_CHUNK) & (ri[None, :] <= ri[:, None]), BF)
    const = lambda shape: pl.BlockSpec(shape, lambda b, i: (0,) * len(shape))
    rowblk = lambda w: pl.BlockSpec((tm, w), lambda b, i: (b * nt + i, 0))
    colblk = lambda h: pl.BlockSpec((h, tm), lambda b, i: (0, b * nt + i))
    in_specs = [
        pl.BlockSpec((None, tm, d), lambda b, i: (b, i, 0)),
        pl.BlockSpec((None, 1, d), lambda b, i: (b, 0, 1)),
        pl.BlockSpec((None, 1, d), lambda b, i: (b, 0, 0)),
        const((1, d)), const((d, NAT_W)), const((T_ROWS, d)), const((CONV_W, QKV_W)),
        pl.BlockSpec((HEAD_DIM // 4, tm), lambda b, i: (0, i)),
        pl.BlockSpec((IDX_DIM // 4, tm), lambda b, i: (0, i)),
        const((2, LANES)), const((16, 2)), const((tm, tm)),
    ]
    out_shape = [
        jax.ShapeDtypeStruct((rows, QKV_W), F32), jax.ShapeDtypeStruct((rows, GDN_W), F32),
        jax.ShapeDtypeStruct((rows * N_KV, HEAD_DIM), F32), jax.ShapeDtypeStruct((rows * N_KV, HEAD_DIM), F32),
        jax.ShapeDtypeStruct((bp, IDX_DIM, tp), F32), jax.ShapeDtypeStruct((bp, 8, QKV_W), F32),
        jax.ShapeDtypeStruct((rows, LANES), F32), jax.ShapeDtypeStruct((rows // GDN_CHUNK, 8, GDN_CHUNK), F32),
        jax.ShapeDtypeStruct((ATT_W, rows), BF), jax.ShapeDtypeStruct((IDX_W, rows), BF),
        jax.ShapeDtypeStruct((IDX_HEADS, rows), F32), jax.ShapeDtypeStruct((rows, KV_W), BF),
        jax.ShapeDtypeStruct((rows // ATT_TK, N_KV * VT_ROWS, ATT_TK), BF), jax.ShapeDtypeStruct((rows, IDX_DIM), BF),
    ]
    out_specs = [
        rowblk(QKV_W), rowblk(GDN_W),
        pl.BlockSpec((tm * N_KV, HEAD_DIM), lambda b, i: (b * nt + i, 0)),
        pl.BlockSpec((tm * N_KV, HEAD_DIM), lambda b, i: (b * nt + i, 0)),
        pl.BlockSpec((None, IDX_DIM, tm), lambda b, i: (b, 0, i)),
        pl.BlockSpec((None, 8, QKV_W), lambda b, i: (b, 0, 0)),
        rowblk(LANES),
        pl.BlockSpec((tm // GDN_CHUNK, 8, GDN_CHUNK), lambda b, i: (b * nt + i, 0, 0)),
        colblk(ATT_W), colblk(IDX_W), colblk(IDX_HEADS), rowblk(KV_W),
        pl.BlockSpec((tm // ATT_TK, N_KV * VT_ROWS, ATT_TK), lambda b, i: (b * nt + i, 0, 0)),
        rowblk(IDX_DIM),
    ]
    return pl.pallas_call(
        functools.partial(_inproj_prompt_kernel, tm, tm // _pick(tm, (ATT_TK,))),
        grid=(bp, nt), in_specs=in_specs, out_specs=out_specs, out_shape=out_shape,
        scratch_shapes=[pltpu.VMEM((tm + 8, QKV_W), F32), pltpu.VMEM((T_ROWS, tm), F32)],
        compiler_params=_params(("arbitrary", "arbitrary")),
        name="inproj_prompt",
    )(x, mod, mod, g1.reshape(1, d), w_nat, w_t, w_conv, cs128, cs64, prow, pcol, tril)


def _inproj_sample(x, mod, g1, w_nat, w_t, w_conv, prow, pcol, conv_state, past):
    db, d = x.shape
    pos = jnp.full((db,), past, I32)
    cs128, cs64 = _rope_tables(pos, HEAD_DIM), _rope_tables(pos, IDX_DIM)
    full = lambda shape: pl.BlockSpec(shape, lambda i: (0,) * len(shape))
    in_specs = [
        full((db, d)),
        pl.BlockSpec((db, d), lambda i: (0, 1)), pl.BlockSpec((db, d), lambda i: (0, 0)),
        full((1, d)), full((d, NAT_W)), full((T_ROWS, d)), full((CONV_W, QKV_W)),
        full((HEAD_DIM // 4, db)), full((IDX_DIM // 4, db)), full((2, LANES)), full((16, 2)),
        full((db, QKV_W)), full((db, QKV_W)), full((db, QKV_W)),
    ]
    widths = [QKV_W, GDN_W, KV_W, KV_W, LANES, QKV_W, LANES, ATT_W, IDX_W]
    return pl.pallas_call(
        _inproj_sample_kernel,
        grid=(1,), in_specs=in_specs,
        out_specs=[full((db, w)) for w in widths],
        out_shape=[jax.ShapeDtypeStruct((db, w), F32) for w in widths],
        scratch_shapes=[pltpu.VMEM((T_ROWS, db), F32)],
        compiler_params=_params(("arbitrary",)),
        name="inproj_sample",
    )(x, mod, mod, g1.reshape(1, d), w_nat, w_t, w_conv, cs128, cs64, prow, pcol,
      conv_state[:, 0], conv_state[:, 1], conv_state[:, 2])


def _unit_lower_inverses(lows):
    c = lows[0].shape[0]
    ii = lax.broadcasted_iota(I32, (c, c), 0)
    jj = lax.broadcasted_iota(I32, (c, c), 1)
    eye = jnp.where(ii == jj, 1.0, 0.0)
    levels = int(math.log2(c)) - 1
    ts = [eye - low for low in lows]
    ps = [_mm_hi(low, low) for low in lows]
    for lvl in range(levels):
        if lvl == levels - 1:
            ts = [t + _mm_hi(t, p) for t, p in zip(ts, ps)]
        else:
            both = [_mm_hi(jnp.concatenate([t, p], axis=0), p) for t, p in zip(ts, ps)]
            ts = [t + b[0:c] for t, b in zip(ts, both)]
            ps = [b[c:] for b in both]
    return ts


def _gdn_prompt_kernel(tg, qkvn_ref, z_ref, misc_ref, gct_ref, gn_ref, o_ref, ssm_ref, s_ref):
    i = pl.program_id(1)

    @pl.when(i == 0)
    def _():
        s_ref[...] = jnp.zeros(s_ref.shape, F32)

    c = GDN_CHUNK
    ii = lax.broadcasted_iota(I32, (c, c), 0)
    jj = lax.broadcasted_iota(I32, (c, c), 1)
    pairs = [(cc, hd) for cc in range(tg // c) for hd in range(GDN_HEADS)]

    qs, ks, gcs, rhs, lows, intras = {}, {}, {}, {}, [], {}
    for cc, hd in pairs:
        r0, lo = cc * c, hd * HEAD_DIM
        q = qkvn_ref[r0:r0 + c, lo:lo + HEAD_DIM]
        k = qkvn_ref[r0:r0 + c, GDN_W + lo:GDN_W + lo + HEAD_DIM]
        v = qkvn_ref[r0:r0 + c, 2 * GDN_W + lo:2 * GDN_W + lo + HEAD_DIM]
        beta_c = misc_ref[r0:r0 + c, hd:hd + 1]
        gc_c = misc_ref[r0:r0 + c, 2 * GDN_HEADS + hd:2 * GDN_HEADS + hd + 1]
        gc_r = gct_ref[cc, GDN_HEADS + hd:GDN_HEADS + hd + 1, :]
        decay = jnp.where(ii >= jj, jnp.exp(jnp.where(ii >= jj, gc_c - gc_r, 0.0)), 0.0)
        kb = k * beta_c
        both = _dot_nt(jnp.concatenate([kb, q], axis=0).astype(BF), k.astype(BF))
        lows.append(jnp.where(ii > jj, both[0:c] * decay, 0.0))
        intras[cc, hd] = (both[c:] * decay).astype(BF)
        rhs[cc, hd] = jnp.concatenate([v * beta_c, kb * jnp.exp(gc_c)], axis=1).astype(BF)
        qs[cc, hd], ks[cc, hd], gcs[cc, hd] = q, k, gc_c
    ts = _unit_lower_inverses(lows)
    uws = {p: _dot(t.astype(BF), rhs[p]) for p, t in zip(pairs, ts)}

    for cc, hd in pairs:
        r0, lo = cc * c, hd * HEAD_DIM
        q, k, gc_c, uw = qs[cc, hd], ks[cc, hd], gcs[cc, hd], uws[cc, hd]
        s = s_ref[hd]
        ws = _dot(jnp.concatenate([uw[:, HEAD_DIM:], q * jnp.exp(gc_c)], axis=0).astype(BF), s.astype(BF))
        v_new = (uw[:, 0:HEAD_DIM] - ws[0:c]).astype(BF)
        o = ws[c:] + _dot(intras[cc, hd], v_new)
        g_last = gc_c[c - 1:c, :]
        kd = k * jnp.exp(g_last - gc_c)
        s_ref[hd] = s * jnp.exp(g_last) + _dot(kd.T.astype(BF), v_new)
        o = _rmsnorm(o, gn_ref[...]) * _silu(z_ref[r0:r0 + c, lo:lo + HEAD_DIM])
        o_ref[r0:r0 + c, lo:lo + HEAD_DIM] = o.astype(BF)
    ssm_ref[...] = s_ref[...]


def _gdn_prompt(qkvn, z, misc, gct, g_norm, bp, tp, tg):
    nt = tp // tg
    rows = bp * tp
    rowblk = lambda w: pl.BlockSpec((tg, w), lambda b, i: (b * nt + i, 0))
    return pl.pallas_call(
        functools.partial(_gdn_prompt_kernel, tg),
        grid=(bp, nt),
        in_specs=[rowblk(QKV_W), rowblk(GDN_W), rowblk(LANES),
                  pl.BlockSpec((tg // GDN_CHUNK, 8, GDN_CHUNK), lambda b, i: (b * nt + i, 0, 0)),
                  pl.BlockSpec((1, HEAD_DIM), lambda b, i: (0, 0))],
        out_specs=[rowblk(GDN_W),
                   pl.BlockSpec((None, GDN_HEADS, HEAD_DIM, HEAD_DIM), lambda b, i: (b, 0, 0, 0))],
        out_shape=[jax.ShapeDtypeStruct((rows, GDN_W), BF),
                   jax.ShapeDtypeStruct((bp, GDN_HEADS, HEAD_DIM, HEAD_DIM), F32)],
        scratch_shapes=[pltpu.VMEM((GDN_HEADS, HEAD_DIM, HEAD_DIM), F32)],
        compiler_params=_params(("arbitrary", "arbitrary")),
        name="gdn_prompt",
    )(qkvn, z, misc, gct, g_norm.reshape(1, HEAD_DIM))


def _gdn_sample_kernel(nb, qkvn_ref, z_ref, misc_ref, gn_ref, s_ref, o_ref, ssm_ref):
    for bi in range(nb):
        for hd in range(GDN_HEADS):
            lo = hd * HEAD_DIM
            q = qkvn_ref[bi:bi + 1, lo:lo + HEAD_DIM]
            k = qkvn_ref[bi:bi + 1, GDN_W + lo:GDN_W + lo + HEAD_DIM]
            v = qkvn_ref[bi:bi + 1, 2 * GDN_W + lo:2 * GDN_W + lo + HEAD_DIM]
            beta = misc_ref[bi:bi + 1, hd:hd + 1]
            g = misc_ref[bi:bi + 1, GDN_HEADS + hd:GDN_HEADS + hd + 1]
            kcol = jnp.broadcast_to(k, (HEAD_DIM, HEAD_DIM)).T
            qcol = jnp.broadcast_to(q, (HEAD_DIM, HEAD_DIM)).T
            s = s_ref[bi, hd] * jnp.exp(g)
            ks = jnp.sum(kcol * s, axis=0, keepdims=True)
            delta = (v - ks) * beta
            s = s + kcol * delta
            ssm_ref[bi, hd] = s
            o = jnp.sum(qcol * s, axis=0, keepdims=True)
            o = _rmsnorm(o, gn_ref[...]) * _silu(z_ref[bi:bi + 1, lo:lo + HEAD_DIM])
            o_ref[bi:bi + 1, lo:lo + HEAD_DIM] = o.astype(BF)


def _gdn_sample(qkvn, z, misc, g_norm, state, nb):
    db = qkvn.shape[0]
    rowblk = lambda w: pl.BlockSpec((nb, w), lambda i: (i, 0))
    sblk = pl.BlockSpec((nb, GDN_HEADS, HEAD_DIM, HEAD_DIM), lambda i: (i, 0, 0, 0))
    return pl.pallas_call(
        functools.partial(_gdn_sample_kernel, nb),
        grid=(db // nb,),
        in_specs=[rowblk(QKV_W), rowblk(GDN_W), rowblk(LANES), pl.BlockSpec((1, HEAD_DIM), lambda i: (0, 0)), sblk],
        out_specs=[rowblk(GDN_W), sblk],
        out_shape=[jax.ShapeDtypeStruct((db, GDN_W), BF), jax.ShapeDtypeStruct(state.shape, F32)],
        compiler_params=_params(("arbitrary",)),
        name="gdn_sample",
    )(qkvn, z, misc, g_norm.reshape(1, HEAD_DIM), state)


def _ordered_word_to_float(u):
    s = u ^ INT_MIN
    return lax.bitcast_convert_type(s ^ ((s >> 31) & np.int32(0x7FFFFFFF)), F32)


def _select_topk_bias(i_ref, h_ref, t_ref, nch, tk, k_sel, idx_bits, taken=0.0, dropped=NEG_INF, coarse=None):
    nl = i_ref.shape[1]
    row = lax.broadcasted_iota(I32, (tk, 1), 0)

    def chunk(c):
        return pl.ds(pl.multiple_of(c * tk, tk), tk)

    def count(ref, pred):
        def body(c, acc):
            m = pred(ref[chunk(c), :])
            return acc + jnp.sum(jnp.where(m, 1, 0).astype(I32).reshape(tk // 8, 8, nl), axis=0)
        acc = lax.fori_loop(0, nch, body, jnp.zeros((8, nl), I32))
        return jnp.sum(acc, axis=0, keepdims=True)

    nchc, tkc = coarse or (nch, tk)

    def chunk_c(c):
        return pl.ds(pl.multiple_of(c * tkc, tkc), tkc)

    def to_coarse(c, carry):
        h_ref[chunk_c(c), :] = i_ref[chunk_c(c), :].astype(BF)
        return carry

    lax.fori_loop(0, nchc, to_coarse, 0)

    def coarse_word(p):
        return (p << 16) | jnp.where(p < 0x8000, 0xFFFF, 0)

    def count_coarse(cand):
        def body(c, acc):
            ones = jnp.where(h_ref[chunk_c(c), :] >= cand, jnp.ones((), BF), jnp.zeros((), BF)).reshape(tkc // 16, 16, nl)
            parts = [ones[n] for n in range(tkc // 16)]
            while len(parts) > 1:
                parts = [a + b for a, b in zip(parts[0::2], parts[1::2])]
            return acc + parts[0].astype(F32)
        acc = lax.fori_loop(0, nchc, body, jnp.zeros((16, nl), F32))
        return jnp.sum(acc, axis=0, keepdims=True)

    def coarse_body(i, p):
        cand_p = p | jnp.left_shift(jnp.int32(1), 15 - i)
        cnt = count_coarse(_ordered_word_to_float(coarse_word(cand_p)).astype(BF))
        return jnp.where(cnt >= k_sel, cand_p, p)

    p16 = lax.fori_loop(0, 16, coarse_body, jnp.zeros((1, nl), I32))
    lo_w = coarse_word(p16) - 0x8000

    def fine_body(i, off):
        cand_off = off | jnp.left_shift(jnp.int32(1), 16 - i)
        cand = _ordered_word_to_float(lo_w + cand_off)
        cnt = count(i_ref, lambda sc: sc >= cand)
        return jnp.where(cnt >= k_sel, cand_off, off)

    off = lax.fori_loop(0, 17, fine_body, jnp.zeros((1, nl), I32))
    thr = jnp.where((p16 >> 7) == 0, NEG_INF, _ordered_word_to_float(lo_w + off))
    n_ge = count(i_ref, lambda sc: sc >= thr)

    def tie_break(_):
        need = k_sel - count(i_ref, lambda sc: sc > thr)

        def mark(c, carry):
            t_ref[chunk(c), :] = jnp.where(i_ref[chunk(c), :] == thr, c * tk + row, 2 ** idx_bits)
            return carry

        lax.fori_loop(0, nch, mark, 0)

        def idx_body(i, x):
            cand = x | jnp.left_shift(jnp.int32(1), idx_bits - 1 - i)
            cnt = count(t_ref, lambda idx: idx < cand)
            return jnp.where(cnt < need, cand, x)

        return lax.fori_loop(0, idx_bits, idx_body, jnp.zeros((1, nl), I32))

    has_ties = jnp.max(jnp.where(thr > NEG_INF, n_ge, 0)) > k_sel
    x = lax.cond(has_ties, tie_break, lambda _: jnp.full((1, nl), 2 ** idx_bits - 1, I32), 0)

    def write(c, carry):
        sc = i_ref[chunk(c), :]
        sel = (sc > thr) | ((sc == thr) & (c * tk + row <= x))
        i_ref[chunk(c), :] = jnp.where(sel & (sc > NEG_INF), taken, dropped)
        return carry

    lax.fori_loop(0, nch, write, 0)


def _attn_prompt_kernel(tq, tk, blk, k_sel, idx_bits, qT_ref, iqT_ref, iwT_ref, kbf_ref, vT_ref, ikbf_ref, o_ref,
                        i_ref, h_ref, t_ref):
    t0 = pl.program_id(1) * tq
    nch = (t0 + tq + tk - 1) // tk
    nblk = (t0 + tq + blk - 1) // blk
    qpos = t0 + lax.broadcasted_iota(I32, (1, tq), 1)
    row = lax.broadcasted_iota(I32, (tk, 1), 0)
    w = iwT_ref[...]
    pairs = IDX_HEADS // 2
    rhs = [jnp.concatenate([iqT_ref[(2 * p) * IDX_DIM:(2 * p + 1) * IDX_DIM, :],
                            iqT_ref[(2 * p + 1) * IDX_DIM:(2 * p + 2) * IDX_DIM, :]], axis=1) for p in range(pairs)]

    def chunk(c):
        return pl.ds(pl.multiple_of(c * tk, tk), tk)

    def idx_body(c, carry):
        ikc = ikbf_ref[chunk(c), :]
        acc = jnp.zeros((tk, tq), F32)
        for p in range(pairs):
            d = jnp.maximum(_dot(ikc, rhs[p]), 0.0)
            acc = acc + d[:, 0:tq] * w[2 * p:2 * p + 1, :] + d[:, tq:] * w[2 * p + 1:2 * p + 2, :]
        i_ref[chunk(c), :] = jnp.where(c * tk + row <= qpos, acc, NEG_INF)
        return carry

    lax.fori_loop(0, nch, idx_body, 0)

    def fill_body(c, carry):
        i_ref[chunk(c), :] = jnp.full((tk, tq), NEG_INF, F32)
        return carry

    lax.fori_loop(nch, nblk * (blk // tk), fill_body, 0)
    _select_topk_bias(i_ref, h_ref, t_ref, nch, tk, k_sel, idx_bits, coarse=(nblk, blk))

    scale2 = HEAD_DIM ** -0.5 * math.log2(math.e)
    rep = ATT_HEADS // N_KV
    sub = ATT_TK
    nsub = blk // sub
    qg = [jnp.concatenate([qT_ref[(rep * g + r) * HEAD_DIM:(rep * g + r + 1) * HEAD_DIM, :] for r in range(rep)],
                          axis=1) for g in range(N_KV)]

    def body(c, carry):
        out = []
        for g in range(N_KV):
            m, acc = carry[g]
            ss = []
            for i in range(nsub):
                rows = pl.ds(pl.multiple_of(c * blk + i * sub, sub), sub)
                bias = i_ref[rows, :]
                ss.append(_dot(kbf_ref[rows, g * HEAD_DIM:(g + 1) * HEAD_DIM], qg[g]) * scale2
                          + jnp.concatenate([bias] * rep, axis=1))
            m_new = m
            for s in ss:
                m_new = jnp.maximum(m_new, jnp.max(s, axis=0, keepdims=True))
            m_safe = jnp.where(m_new == NEG_INF, 0.0, m_new)
            acc = acc * jnp.exp2(m - m_safe)
            for i, s in enumerate(ss):
                p = jnp.exp2(s - m_safe).astype(BF)
                acc = acc + _dot(vT_ref[c * nsub + i, g * VT_ROWS:(g + 1) * VT_ROWS, :], p)
            out.append((m_new, acc))
        return tuple(out)

    init = tuple((jnp.full((1, rep * tq), NEG_INF, F32), jnp.zeros((VT_ROWS, rep * tq), F32)) for _ in range(N_KV))
    res = lax.fori_loop(0, nblk, body, init)
    for g in range(N_KV):
        acc = res[g][1]
        o_t = acc[0:HEAD_DIM] / acc[HEAD_DIM:HEAD_DIM + 1]
        for r in range(rep):
            hd = rep * g + r
            o_ref[:, hd * HEAD_DIM:(hd + 1) * HEAD_DIM] = o_t[:, r * tq:(r + 1) * tq].T.astype(BF)


def _attn_prompt(qT, iqT, iwT, kbf, vT, ikbf, bp, tp):
    tq = QBLOCK
    tk = _pick(tp, (IDX_TK, ATT_TK))
    blk = _pick(tp, (ATT_BLK, IDX_TK, ATT_TK))
    assert blk % tk == 0 and blk % ATT_TK == 0
    nq = tp // tq
    rows = bp * tp
    k_sel = min(TOPK_MAX, tp // 4)
    idx_bits = max(1, int(tp - 1).bit_length())
    colblk = lambda h: pl.BlockSpec((h, tq), lambda b, j: (0, b * nq + j))
    return pl.pallas_call(
        functools.partial(_attn_prompt_kernel, tq, tk, blk, k_sel, idx_bits),
        grid=(bp, nq),
        in_specs=[colblk(ATT_W), colblk(IDX_W), colblk(IDX_HEADS),
                  pl.BlockSpec((tp, KV_W), lambda b, j: (b, 0)),
                  pl.BlockSpec((tp // ATT_TK, N_KV * VT_ROWS, ATT_TK), lambda b, j: (b, 0, 0)),
                  pl.BlockSpec((tp, IDX_DIM), lambda b, j: (b, 0))],
        out_specs=pl.BlockSpec((tq, ATT_W), lambda b, j: (b * nq + j, 0)),
        out_shape=jax.ShapeDtypeStruct((rows, ATT_W), BF),
        scratch_shapes=[pltpu.VMEM((tp, tq), F32), pltpu.VMEM((tp, tq), BF), pltpu.VMEM((tp, tq), I32)],
        compiler_params=_params(("arbitrary", "arbitrary")),
        name="attn_prompt",
    )(qT, iqT, iwT, kbf, vT, ikbf)


def _sidx_kernel(npg, pt_ref, iq_ref, iw_ref, ikn_ref, *refs):
    pages, out_ref = refs[:npg], refs[npg]
    iq = iq_ref[...].astype(BF)
    w = iw_ref[...]
    for p in range(npg):
        d = jnp.maximum(_dot(iq, pages[p][...].astype(BF)), 0.0)
        out_ref[:, p * LANES:(p + 1) * LANES] = jnp.sum(d * w, axis=0, keepdims=True)
    dn = jnp.sum(iq.astype(F32) * ikn_ref[...].astype(BF).astype(F32), axis=1, keepdims=True)
    sn = jnp.sum(jnp.maximum(dn, 0.0) * w, axis=0, keepdims=True)
    lane = lax.broadcasted_iota(I32, (1, LANES), 1)
    out_ref[:, npg * LANES:(npg + 1) * LANES] = jnp.where(lane == 0, sn, NEG_INF)


def _sample_index_scores(iq, iw, ik_new, cache_ik_t, page_table):
    db, npg = page_table.shape
    page = cache_ik_t.shape[2]
    assert page == LANES
    width = (npg + 1) * LANES
    page_specs = [pl.BlockSpec((None, IDX_DIM, page), functools.partial(lambda p, b, pt: (pt[b, p], 0, 0), p))
                  for p in range(npg)]
    grid_spec = pltpu.PrefetchScalarGridSpec(
        num_scalar_prefetch=1, grid=(db,),
        in_specs=[pl.BlockSpec((None, IDX_HEADS, IDX_DIM), lambda b, pt: (b, 0, 0)),
                  pl.BlockSpec((None, IDX_HEADS, 1), lambda b, pt: (b, 0, 0)),
                  pl.BlockSpec((None, 1, IDX_DIM), lambda b, pt: (b, 0, 0))] + page_specs,
        out_specs=pl.BlockSpec((None, 1, width), lambda b, pt: (b, 0, 0)))
    out = pl.pallas_call(
        functools.partial(_sidx_kernel, npg), grid_spec=grid_spec,
        out_shape=jax.ShapeDtypeStruct((db, 1, width), F32),
        compiler_params=_params(("arbitrary",)),
        name="sample_index_scores",
    )(page_table, iq.reshape(db, IDX_HEADS, IDX_DIM), iw.reshape(db, IDX_HEADS, 1), ik_new.reshape(db, 1, IDX_DIM),
      *([cache_ik_t] * npg))
    return out.reshape(db, width)


def _ssel_kernel(nch, k_sel, idx_bits, s_ref, spread_ref, keep_ref, i_ref, h_ref, t_ref):
    for p in range(nch):
        i_ref[p * LANES:(p + 1) * LANES, :] = s_ref[:, p * LANES:(p + 1) * LANES].T
    _select_topk_bias(i_ref, h_ref, t_ref, nch, LANES, k_sel, idx_bits, taken=1.0, dropped=0.0)
    rows = N_KV * LANES
    for p in range(nch - 1):
        k2 = _dot(spread_ref[...], i_ref[p * LANES:(p + 1) * LANES, :].astype(BF))
        for h in range(N_KV):
            keep_ref[:, p * rows + h * LANES:p * rows + (h + 1) * LANES] = k2[h * LANES:(h + 1) * LANES, :].T
    keep_ref[:, (nch - 1) * rows:(nch - 1) * rows + LANES] = i_ref[(nch - 1) * LANES:nch * LANES, :].T


def _sample_select(scores, n_keys):
    db, width = scores.shape
    assert db == LANES
    nch = width // LANES
    k_sel = min(TOPK_MAX, n_keys // 4)
    idx_bits = max(1, int(width - 1).bit_length())
    rows = N_KV * LANES
    out_w = (nch - 1) * rows + LANES
    spread = jnp.asarray(np.arange(rows)[:, None] // N_KV == np.arange(LANES)[None, :], BF)
    return pl.pallas_call(
        functools.partial(_ssel_kernel, nch, k_sel, idx_bits),
        grid=(1,),
        in_specs=[pl.BlockSpec((db, width), lambda i: (0, 0)), pl.BlockSpec((rows, LANES), lambda i: (0, 0))],
        out_specs=pl.BlockSpec((db, out_w), lambda i: (0, 0)),
        out_shape=jax.ShapeDtypeStruct((db, out_w), F32),
        scratch_shapes=[pltpu.VMEM((width, db), F32), pltpu.VMEM((width, db), BF), pltpu.VMEM((width, db), I32)],
        compiler_params=_params(("arbitrary",)),
        name="sample_select",
    )(scores, spread)


def _sattn_kernel(npg, pt_ref, q_ref, keep_ref, kn_ref, vn_ref, *refs):
    kpages, vpages, o_ref = refs[:npg], refs[npg:2 * npg], refs[2 * npg]
    rep = ATT_HEADS // N_KV
    rows = N_KV * LANES
    scale = HEAD_DIM ** -0.5
    q = q_ref[...]
    q8 = jnp.concatenate([q, jnp.zeros((8 - ATT_HEADS, HEAD_DIM), F32)], axis=0).astype(BF)
    hrow = lax.broadcasted_iota(I32, (8, 1), 0)
    lane = lax.broadcasted_iota(I32, (1, LANES), 1)
    own_kv = lax.broadcasted_iota(I32, (1, rows), 1) % N_KV == hrow // rep
    parts = []
    for p in range(npg):
        s2 = _dot_nt(q8, kpages[p][...].astype(BF))
        keep2 = keep_ref[:, p * rows:(p + 1) * rows]
        parts.append(jnp.where(own_kv & (keep2 > 0.5), s2 * scale, NEG_INF))
    kn = kn_ref[...].astype(BF).astype(F32)
    kn8 = jnp.where(hrow // rep == 0, kn[0:1, :], kn[1:2, :])
    s_new = jnp.sum(q8.astype(F32) * kn8, axis=1, keepdims=True)
    keep_new = keep_ref[:, npg * rows:npg * rows + 1]
    parts.append(jnp.where((lane == 0) & (keep_new > 0.5), s_new * scale, NEG_INF))
    s = jnp.concatenate(parts, axis=1)
    m = jnp.max(s, axis=1, keepdims=True)
    e = jnp.exp(s - m)
    pr = (e / jnp.sum(e, axis=1, keepdims=True)).astype(BF)
    o8 = jnp.zeros((8, HEAD_DIM), F32)
    for p in range(npg):
        o8 = o8 + _dot(pr[:, p * rows:(p + 1) * rows], vpages[p][...].astype(BF))
    vn = vn_ref[...].astype(BF).astype(F32)
    vn8 = jnp.where(hrow // rep == 0, vn[0:1, :], vn[1:2, :])
    o8 = o8 + pr[:, npg * rows:npg * rows + 1].astype(F32) * vn8
    for hd in range(ATT_HEADS):
        o_ref[:, hd * HEAD_DIM:(hd + 1) * HEAD_DIM] = o8[hd:hd + 1, :].astype(BF)


def _sample_attention(q, keep, k_new, v_new, cache_k, cache_v, page_table):
    db, npg = page_table.shape
    n_phys, page = cache_k.shape[0], cache_k.shape[1]
    assert page == LANES and N_KV == 2
    rows = page * N_KV
    width = npg * rows + LANES
    ck = cache_k.reshape(n_phys, rows, HEAD_DIM)
    cv = cache_v.reshape(n_phys, rows, HEAD_DIM)
    page_specs = [pl.BlockSpec((None, rows, HEAD_DIM), functools.partial(lambda p, b, pt: (pt[b, p], 0, 0), p))
                  for p in range(npg)]
    grid_spec = pltpu.PrefetchScalarGridSpec(
        num_scalar_prefetch=1, grid=(db,),
        in_specs=[pl.BlockSpec((None, ATT_HEADS, HEAD_DIM), lambda b, pt: (b, 0, 0)),
                  pl.BlockSpec((None, 1, width), lambda b, pt: (b, 0, 0)),
                  pl.BlockSpec((None, N_KV, HEAD_DIM), lambda b, pt: (b, 0, 0)),
                  pl.BlockSpec((None, N_KV, HEAD_DIM), lambda b, pt: (b, 0, 0))] + page_specs + page_specs,
        out_specs=pl.BlockSpec((None, 1, ATT_W), lambda b, pt: (b, 0, 0)))
    out = pl.pallas_call(
        functools.partial(_sattn_kernel, npg), grid_spec=grid_spec,
        out_shape=jax.ShapeDtypeStruct((db, 1, ATT_W), BF),
        compiler_params=_params(("arbitrary",)),
        name="sample_attention",
    )(page_table, q.reshape(db, ATT_HEADS, HEAD_DIM), keep.reshape(db, 1, width),
      k_new.reshape(db, N_KV, HEAD_DIM), v_new.reshape(db, N_KV, HEAD_DIM), *([ck] * npg), *([cv] * npg))
    return out.reshape(db, ATT_W)


def _post_kernel(final, nff, og_ref, oa_ref, x_ref, ga1_ref, sc2_ref, sh2_ref, ga2_ref, g2_ref, gf_ref,
                 wo_ref, wg_ref, wu_ref, wd_ref, y_ref, x1_ref, h2_ref, acc_ref):
    jf = pl.program_id(1)

    @pl.when(jf == 0)
    def _():
        mixed = _dot(og_ref[...], wo_ref[0:GDN_W, :]) + _dot(oa_ref[...], wo_ref[GDN_W:, :])
        x1 = x_ref[...] + ga1_ref[...] * mixed
        x1_ref[...] = x1
        h2_ref[...] = (_rmsnorm(x1, g2_ref[...]) * (1.0 + sc2_ref[...]) + sh2_ref[...]).astype(BF)
        acc_ref[...] = jnp.zeros(acc_ref.shape, F32)

    h2 = h2_ref[...]
    act = (_silu(_dot(h2, wg_ref[...])) * _dot(h2, wu_ref[...])).astype(BF)
    acc_ref[...] += _dot(act, wd_ref[...])

    @pl.when(jf == nff - 1)
    def _():
        x2 = x1_ref[...] + ga2_ref[...] * acc_ref[...]
        y_ref[...] = _rmsnorm(x2, gf_ref[...]) if final else x2


def _post(og, oa, x2d, mod, mod_spec, g2, gf, wo, wfi, wfo, tm, tf, final):
    rows, d = x2d.shape
    dff = wfo.shape[0]
    nff = dff // tf
    rowblk = lambda w: pl.BlockSpec((tm, w), lambda i, j: (i, 0))
    const = lambda shape: pl.BlockSpec(shape, lambda i, j: (0,) * len(shape))
    in_specs = [rowblk(GDN_W), rowblk(ATT_W), rowblk(d),
                mod_spec(2), mod_spec(4), mod_spec(3), mod_spec(5),
                const((1, d)), const((1, d)), const((d, d)),
                pl.BlockSpec((d, tf), lambda i, j: (0, j)),
                pl.BlockSpec((d, tf), lambda i, j: (0, nff + j)),
                pl.BlockSpec((tf, d), lambda i, j: (j, 0))]
    return pl.pallas_call(
        functools.partial(_post_kernel, final, nff),
        grid=(rows // tm, nff), in_specs=in_specs, out_specs=rowblk(d),
        out_shape=jax.ShapeDtypeStruct((rows, d), F32),
        scratch_shapes=[pltpu.VMEM((tm, d), F32), pltpu.VMEM((tm, d), BF), pltpu.VMEM((tm, d), F32)],
        compiler_params=_params(("arbitrary", "arbitrary")),
        name="post",
    )(og, oa, x2d, mod, mod, mod, mod, g2.reshape(1, d), gf.reshape(1, d), wo, wfi, wfi, wfo)


def _pick(n, prefs):
    for p in prefs:
        if n % p == 0:
            return p
    return n


def kernel(x_prompt, x_sample, c_prompt, c_sample, cache_k, cache_v, cache_idx_k, page_table, state_conv, state_ssm,
           w_ada, b_ada, g_norm1, w_in, w_conv, a_log, dt_bias, g_gdn_norm, w_out, g_norm2, w_ffn_in, w_ffn_out,
           g_final):
    bp, tp, d = x_prompt.shape
    db, ts, _ = x_sample.shape
    assert ts == 1 and d == GDN_W + ATT_W and tp % max(ATT_TK, GDN_CHUNK) == 0
    depth = w_in.shape[0]
    npg, page = page_table.shape[1], cache_k.shape[2]
    past = npg * page
    tm_in = _pick(tp, (512, 256))
    tg = _pick(tp, (512, 256))
    tm_post = _pick(tp, (1024, 512, 256))
    dff = w_ffn_out.shape[1]
    tf = _pick(dff, (256, 128))

    xp = x_prompt
    xs = x_sample.reshape(db, d)
    npad = (-(bp + db)) % 8
    c_all = jnp.concatenate([c_prompt, c_sample, jnp.zeros((npad, d), F32)], axis=0)
    new_p, new_s = [], []
    for l in range(depth):
        final = l == depth - 1
        mod = _modulation(c_all, w_ada[l], b_ada[l])
        mod_p = mod[:bp].reshape(bp, 1, 6 * d)
        mod_s = mod[bp:bp + db]
        w_nat, w_t, prow, pcol = _inproj_weights(w_in[l], a_log[l], dt_bias[l])
        wo = w_out[l].astype(BF)
        wfi = w_ffn_in[l].astype(BF)
        wfo = w_ffn_out[l].astype(BF)

        (qkvn, z, vnat, knat, ikT, tail, misc, gct, qT, iqT, iwT, kbf, vT, ikbf) = _inproj_prompt(
            xp, mod_p, g_norm1[l], w_nat, w_t, w_conv[l], prow, pcol, tm_in)
        og, ssm_p = _gdn_prompt(qkvn, z, misc, gct, g_gdn_norm[l], bp, tp, tg)
        oa = _attn_prompt(qT, iqT, iwT, kbf, vT, ikbf, bp, tp)
        tiles_b = tp // tm_post
        spec_p = lambda k: pl.BlockSpec((None, 1, d), lambda i, j: (i // tiles_b, 0, k))
        xp = _post(og, oa, xp.reshape(bp * tp, d), mod_p, spec_p, g_norm2[l], g_final, wo, wfi, wfo,
                   tm_post, tf, final).reshape(bp, tp, d)
        new_p.append((knat.reshape(bp, tp, N_KV, HEAD_DIM), vnat.reshape(bp, tp, N_KV, HEAD_DIM),
                      jnp.swapaxes(ikT, 1, 2), tail[:, 8 - (CONV_W - 1):, :], ssm_p))

        (qkvn_s, z_s, vnat_s, knat_s, slab_s, raw_s, misc_s, aq_s, iq_s) = _inproj_sample(
            xs, mod_s, g_norm1[l], w_nat, w_t, w_conv[l], prow, pcol, state_conv[l], past)
        og_s, ssm_s = _gdn_sample(qkvn_s, z_s, misc_s, g_gdn_norm[l], state_ssm[l], _pick(db, (8,)))
        ik_s = slab_s[:, 0:IDX_DIM]
        iw_s = slab_s[:, IDX_DIM:IDX_DIM + IDX_HEADS]
        scores = _sample_index_scores(iq_s, iw_s, ik_s, jnp.swapaxes(cache_idx_k[l], 1, 2), page_table)
        keep = _sample_select(scores, past + ts)
        oa_s = _sample_attention(aq_s, keep, knat_s, vnat_s, cache_k[l], cache_v[l], page_table)
        spec_s = lambda k: pl.BlockSpec((db, d), lambda i, j: (i, k))
        xs = _post(og_s, oa_s, xs, mod_s, spec_s, g_norm2[l], g_final, wo, wfi, wfo, db, tf, final)
        new_conv_s = jnp.concatenate([state_conv[l][:, 1:, :], raw_s[:, None, :]], axis=1)
        new_s.append((knat_s.reshape(db, ts, N_KV, HEAD_DIM), vnat_s.reshape(db, ts, N_KV, HEAD_DIM),
                      ik_s.reshape(db, ts, IDX_DIM), new_conv_s, ssm_s))

    stack = lambda states, n: jnp.stack([s[n] for s in states], axis=0)
    return (xp, xs.reshape(db, ts, d),
            stack(new_p, 0), stack(new_p, 1), stack(new_p, 2), stack(new_p, 3), stack(new_p, 4),
            stack(new_s, 0), stack(new_s, 1), stack(new_s, 2), stack(new_s, 3), stack(new_s, 4))
```

```python
import functools
import math

import numpy as np
import jax
import jax.numpy as jnp
from jax import lax
from jax.experimental import pallas as pl
from jax.experimental.pallas import tpu as pltpu

F32 = jnp.float32
BF = jnp.bfloat16
I32 = jnp.int32

HEAD_DIM = 128
GDN_HEADS = 4
ATT_HEADS = 4
N_KV = 2
IDX_HEADS = 8
IDX_DIM = 64
CONV_W = 4
TOPK_MAX = 256
QBLOCK = 128
ROPE_THETA = 500000.0
EPS = 1e-6
GDN_W = GDN_HEADS * HEAD_DIM
ATT_W = ATT_HEADS * HEAD_DIM
KV_W = N_KV * HEAD_DIM
IDX_W = IDX_HEADS * IDX_DIM
QKV_W = 3 * GDN_W

GDN_CHUNK = 128
ATT_TK = 256
VT_ROWS = HEAD_DIM + 16
IDX_TK = 512
ATT_BLK = 1024
ATT_TQ = 256
LANES = 128
NAT_W = QKV_W + GDN_W + KV_W + LANES
T_AQ, T_AK, T_IQ, T_IK, T_MISC, T_AV = 0, 512, 768, 1280, 1344, 1360
T_ROWS = T_AV + KV_W
VMEM_LIMIT = 56 * 1024 * 1024
INT_MIN = np.int32(-2 ** 31)
NEG_INF = float("-inf")


def _params(sem):
    return pltpu.CompilerParams(dimension_semantics=sem, vmem_limit_bytes=VMEM_LIMIT)


def _dot(a, b):
    return jnp.dot(a, b, preferred_element_type=F32)


def _dot_nt(a, b):
    return lax.dot_general(a, b, (((1,), (1,)), ((), ())), preferred_element_type=F32)


def _split3(x):
    hi = x.astype(BF)
    r1 = x - hi.astype(F32)
    mid = r1.astype(BF)
    lo = (r1 - mid.astype(F32)).astype(BF)
    return hi, mid, lo


def _mm_hi(a, b):
    ah = a.astype(BF)
    al = (a - ah.astype(F32)).astype(BF)
    bh = b.astype(BF)
    bl = (b - bh.astype(F32)).astype(BF)
    m = a.shape[0]
    hi = _dot(jnp.concatenate([ah, al], axis=0), bh)
    return hi[0:m] + (hi[m:] + _dot(ah, bl))


def _sigmoid(x):
    return 1.0 / (1.0 + jnp.exp(-x))


def _silu(x):
    return x * _sigmoid(x)


def _softplus(x):
    return jnp.maximum(x, 0.0) + jnp.log(1.0 + jnp.exp(-jnp.abs(x)))


def _rmsnorm(x, g):
    return x * lax.rsqrt(jnp.mean(x * x, axis=-1, keepdims=True) + EPS) * g


def _mod_kernel(c_ref, w_ref, b_ref, o_ref):
    s = _silu(c_ref[...]).astype(BF)
    o_ref[...] = _dot(s, w_ref[...].astype(BF)) + b_ref[...]


def _modulation(c_all, w_ada, b_ada):
    n, d = c_all.shape
    cols = w_ada.shape[1]
    tn = d
    return pl.pallas_call(
        _mod_kernel,
        grid=(cols // tn,),
        in_specs=[pl.BlockSpec((n, d), lambda j: (0, 0)),
                  pl.BlockSpec((d, tn), lambda j: (0, j)),
                  pl.BlockSpec((1, tn), lambda j: (0, j))],
        out_specs=pl.BlockSpec((n, tn), lambda j: (0, j)),
        out_shape=jax.ShapeDtypeStruct((n, cols), F32),
        compiler_params=_params(("arbitrary",)),
        name="modulation",
    )(c_all, w_ada, b_ada.reshape(1, cols))


def _rope_rows(rt_ref, base, half, cos, sin):
    x1 = rt_ref[base:base + half, :]
    x2 = rt_ref[base + half:base + 2 * half, :]
    rt_ref[base:base + half, :] = x1 * cos - x2 * sin
    rt_ref[base + half:base + 2 * half, :] = x2 * cos + x1 * sin


def _project(x_ref, sc_ref, sh_ref, g1_ref, wn_ref, wt_ref, cs128_ref, cs64_ref, pcol_ref, rt_ref):
    h = _rmsnorm(x_ref[...], g1_ref[...]) * (1.0 + sc_ref[...]) + sh_ref[...]
    hb = h.astype(BF)
    nat = _dot(hb, wn_ref[...])
    rt_ref[...] = _dot_nt(wt_ref[...], hb)
    half = HEAD_DIM // 8
    cos, sin = cs128_ref[0:half, :], cs128_ref[half:2 * half, :]
    for hd in range(ATT_HEADS):
        _rope_rows(rt_ref, T_AQ + hd * HEAD_DIM, half, cos, sin)
    for hd in range(N_KV):
        _rope_rows(rt_ref, T_AK + hd * HEAD_DIM, half, cos, sin)
    half = IDX_DIM // 8
    cos, sin = cs64_ref[0:half, :], cs64_ref[half:2 * half, :]
    for hd in range(IDX_HEADS):
        _rope_rows(rt_ref, T_IQ + hd * IDX_DIM, half, cos, sin)
    _rope_rows(rt_ref, T_IK, half, cos, sin)
    mt = rt_ref[T_MISC:T_MISC + 16, :]
    r = lax.broadcasted_iota(I32, mt.shape, 0)
    a_log, dt_b = pcol_ref[:, 0:1], pcol_ref[:, 1:2]
    gate = -jnp.exp(a_log) * _softplus(mt + dt_b)
    mt = jnp.where(r < IDX_HEADS, mt * IDX_HEADS ** -0.5, jnp.where(r < IDX_HEADS + GDN_HEADS, _sigmoid(mt), gate))
    rt_ref[T_MISC:T_MISC + 16, :] = mt
    return nat, mt


def _misc_natural(m, prow_ref):
    lane = lax.broadcasted_iota(I32, m.shape, 1)
    gate = -jnp.exp(prow_ref[0:1, :]) * _softplus(m + prow_ref[1:2, :])
    return jnp.where(lane < GDN_HEADS, _sigmoid(m), jnp.where(lane < 3 * GDN_HEADS, gate, 0.0))


def _qkv_post(conv, qkvn_ref):
    c = _silu(conv)
    for hb in range(2 * GDN_HEADS):
        xh = c[:, hb * HEAD_DIM:(hb + 1) * HEAD_DIM]
        n = xh * lax.rsqrt(jnp.sum(xh * xh, axis=-1, keepdims=True) + EPS)
        if hb < GDN_HEADS:
            n = n * HEAD_DIM ** -0.5
        qkvn_ref[:, hb * HEAD_DIM:(hb + 1) * HEAD_DIM] = n
    qkvn_ref[:, 2 * GDN_W:] = c[:, 2 * GDN_W:]


def _inproj_prompt_kernel(tm, parts, x_ref, sc_ref, sh_ref, g1_ref, wn_ref, wt_ref, wconv_ref, cs128_ref, cs64_ref,
                          prow_ref, pcol_ref, tril_ref,
                          qkvn_ref, z_ref, vnat_ref, knat_ref, ikT_ref, tail_ref, misc_ref, gct_ref,
                          qT_ref, iqT_ref, iwT_ref, kbf_ref, vT_ref, ikbf_ref,
                          xs_ref, rt_ref):
    i = pl.program_id(1)

    @pl.when(i == 0)
    def _():
        xs_ref[0:8, :] = jnp.zeros((8, QKV_W), F32)

    @pl.when(i > 0)
    def _():
        xs_ref[0:8, :] = xs_ref[tm:tm + 8, :]

    th = tm // parts
    for n in range(parts):
        rows, cols = pl.ds(n * th, th), pl.ds(n * th, th)
        _inproj_prompt_rows(
            th, x_ref.at[rows, :], sc_ref, sh_ref, g1_ref, wn_ref, wt_ref, wconv_ref,
            cs128_ref.at[:, cols], cs64_ref.at[:, cols], prow_ref, pcol_ref, tril_ref.at[0:th, 0:th],
            qkvn_ref.at[rows, :], z_ref.at[rows, :], vnat_ref.at[pl.ds(n * th * N_KV, th * N_KV), :],
            knat_ref.at[pl.ds(n * th * N_KV, th * N_KV), :], ikT_ref.at[:, cols], tail_ref, misc_ref.at[rows, :],
            gct_ref.at[pl.ds(n * (th // GDN_CHUNK), th // GDN_CHUNK)],
            qT_ref.at[:, cols], iqT_ref.at[:, cols], iwT_ref.at[:, cols], kbf_ref.at[rows, :],
            vT_ref.at[pl.ds(n * (th // ATT_TK), th // ATT_TK)], ikbf_ref.at[rows, :],
            xs_ref.at[pl.ds(n * th, th + 8), :], rt_ref.at[:, cols])


def _inproj_prompt_rows(tm, x_ref, sc_ref, sh_ref, g1_ref, wn_ref, wt_ref, wconv_ref, cs128_ref, cs64_ref,
                        prow_ref, pcol_ref, tril_ref,
                        qkvn_ref, z_ref, vnat_ref, knat_ref, ikT_ref, tail_ref, misc_ref, gct_ref,
                        qT_ref, iqT_ref, iwT_ref, kbf_ref, vT_ref, ikbf_ref,
                        xs_ref, rt_ref):
    nat, mt = _project(x_ref, sc_ref, sh_ref, g1_ref, wn_ref, wt_ref, cs128_ref, cs64_ref, pcol_ref, rt_ref)
    xs_ref[8:tm + 8, :] = nat[:, 0:QKV_W]
    conv = wconv_ref[0:1, :] * xs_ref[5:tm + 5, :]
    for t in range(1, CONV_W):
        conv = conv + wconv_ref[t:t + 1, :] * xs_ref[5 + t:tm + 5 + t, :]
    tail_ref[...] = xs_ref[tm:tm + 8, :]
    _qkv_post(conv, qkvn_ref)
    z_ref[...] = nat[:, QKV_W:QKV_W + GDN_W]
    v0 = QKV_W + GDN_W
    for hd in range(N_KV):
        vnat_ref[pl.ds(hd, tm, stride=N_KV), :] = nat[:, v0 + hd * HEAD_DIM:v0 + (hd + 1) * HEAD_DIM]

    gm = _misc_natural(nat[:, NAT_W - LANES:], prow_ref)
    tril = tril_ref[...]
    hi, mid, lo = _split3(gm)
    gc = _dot(tril, hi) + (_dot(tril, mid) + _dot(tril, lo))
    lane = lax.broadcasted_iota(I32, gm.shape, 1)
    misc_ref[...] = jnp.where(lane < 2 * GDN_HEADS, gm, gc)
    hi, mid, lo = _split3(mt)
    gct = _dot_nt(hi, tril) + (_dot_nt(mid, tril) + _dot_nt(lo, tril))
    r = lax.broadcasted_iota(I32, mt.shape, 0)
    bg = jnp.where(r < IDX_HEADS + GDN_HEADS, mt, gct)[8:16, :]
    for cc in range(tm // GDN_CHUNK):
        gct_ref[cc] = bg[:, cc * GDN_CHUNK:(cc + 1) * GDN_CHUNK]

    iwT_ref[...] = mt[0:IDX_HEADS, :]
    qT_ref[...] = rt_ref[T_AQ:T_AQ + ATT_W, :].astype(BF)
    iqT_ref[...] = rt_ref[T_IQ:T_IQ + IDX_W, :].astype(BF)
    kn = rt_ref[T_AK:T_AK + KV_W, :].T
    for hd in range(N_KV):
        knat_ref[pl.ds(hd, tm, stride=N_KV), :] = kn[:, hd * HEAD_DIM:(hd + 1) * HEAD_DIM]
    kbf_ref[...] = kn.astype(BF)
    ikT_ref[...] = rt_ref[T_IK:T_IK + IDX_DIM, :]
    ikbf_ref[...] = rt_ref[T_IK:T_IK + LANES, :].T[:, 0:IDX_DIM].astype(BF)
    for cc in range(tm // ATT_TK):
        for hd in range(N_KV):
            lo = hd * VT_ROWS
            vT_ref[cc, lo:lo + HEAD_DIM, :] = rt_ref[T_AV + hd * HEAD_DIM:T_AV + (hd + 1) * HEAD_DIM,
                                                     cc * ATT_TK:(cc + 1) * ATT_TK].astype(BF)
            vT_ref[cc, lo + HEAD_DIM:lo + VT_ROWS, :] = jnp.ones((VT_ROWS - HEAD_DIM, ATT_TK), BF)


def _inproj_sample_kernel(x_ref, sc_ref, sh_ref, g1_ref, wn_ref, wt_ref, wconv_ref, cs128_ref, cs64_ref,
                          prow_ref, pcol_ref, s0_ref, s1_ref, s2_ref,
                          qkvn_ref, z_ref, vnat_ref, knat_ref, slab_ref, raw_ref, misc_ref, aq_ref, iq_ref,
                          rt_ref):
    nat, _ = _project(x_ref, sc_ref, sh_ref, g1_ref, wn_ref, wt_ref, cs128_ref, cs64_ref, pcol_ref, rt_ref)
    raw = nat[:, 0:QKV_W]
    raw_ref[...] = raw
    conv = wconv_ref[0:1, :] * s0_ref[...]
    conv = conv + wconv_ref[1:2, :] * s1_ref[...]
    conv = conv + wconv_ref[2:3, :] * s2_ref[...]
    conv = conv + wconv_ref[3:4, :] * raw
    _qkv_post(conv, qkvn_ref)
    z_ref[...] = nat[:, QKV_W:QKV_W + GDN_W]
    vnat_ref[...] = nat[:, QKV_W + GDN_W:QKV_W + GDN_W + KV_W]
    misc_ref[...] = _misc_natural(nat[:, NAT_W - LANES:], prow_ref)
    aq_ref[...] = rt_ref[T_AQ:T_AQ + ATT_W, :].T
    iq_ref[...] = rt_ref[T_IQ:T_IQ + IDX_W, :].T
    knat_ref[...] = rt_ref[T_AK:T_AK + KV_W, :].T
    slab_ref[...] = rt_ref[T_IK:T_IK + LANES, :].T


def _inproj_weights(w_in, a_log, dt_bias):
    offs = np.cumsum([0, QKV_W, GDN_W, GDN_HEADS, GDN_HEADS, ATT_W, KV_W, KV_W, IDX_W, IDX_DIM, IDX_HEADS])
    qkv, z, beta, a, aq, ak, av, iq, ik, iw = [w_in[:, offs[n]:offs[n + 1]] for n in range(10)]
    d = w_in.shape[0]
    pad = jnp.zeros((d, LANES - 3 * GDN_HEADS), w_in.dtype)
    w_nat = jnp.concatenate([qkv, z, av, beta, a, a, pad], axis=1).astype(BF)
    w_t = jnp.concatenate([aq, ak, iq, ik, iw, beta, a, av], axis=1).T.astype(BF)
    zrow = jnp.zeros((LANES - 3 * GDN_HEADS,), F32)
    prow = jnp.stack([jnp.concatenate([jnp.zeros((GDN_HEADS,), F32), a_log, a_log, zrow]),
                      jnp.concatenate([jnp.zeros((GDN_HEADS,), F32), dt_bias, dt_bias, zrow])])
    z12 = jnp.zeros((IDX_HEADS + GDN_HEADS,), F32)
    pcol = jnp.stack([jnp.concatenate([z12, a_log]), jnp.concatenate([z12, dt_bias])], axis=1)
    return w_nat, w_t, prow, pcol


def _rope_tables(pos, dim):
    half = dim // 8
    inv = ROPE_THETA ** (-jnp.arange(half, dtype=F32) / half)
    ang = pos.astype(F32)[:, None] * inv[None, :]
    return jnp.concatenate([jnp.cos(ang).T, jnp.sin(ang).T], axis=0)


def _inproj_prompt(x, mod, g1, w_nat, w_t, w_conv, prow, pcol, tm):
    bp, tp, d = x.shape
    nt = tp // tm
    rows = bp * tp
    pos = jnp.arange(tp, dtype=I32)
    cs128, cs64 = _rope_tables(pos, HEAD_DIM), _rope_tables(pos, IDX_DIM)
    ri = np.arange(tm)
    tril = jnp.asarray((ri[:, None] // GDN_CHUNK == ri[None, :] // GDN_CHUNK) & (ri[None, :] <= ri[:, None]), BF)
    const = lambda shape: pl.BlockSpec(shape, lambda b, i: (0,) * len(shape))
    rowblk = lambda w: pl.BlockSpec((tm, w), lambda b, i: (b * nt + i, 0))
    colblk = lambda h: pl.BlockSpec((h, tm), lambda b, i: (0, b * nt + i))
    in_specs = [
        pl.BlockSpec((None, tm, d), lambda b, i: (b, i, 0)),
        pl.BlockSpec((None, 1, d), lambda b, i: (b, 0, 1)),
        pl.BlockSpec((None, 1, d), lambda b, i: (b, 0, 0)),
        const((1, d)), const((d, NAT_W)), const((T_ROWS, d)), const((CONV_W, QKV_W)),
        pl.BlockSpec((HEAD_DIM // 4, tm), lambda b, i: (0, i)),
        pl.BlockSpec((IDX_DIM // 4, tm), lambda b, i: (0, i)),
        const((2, LANES)), const((16, 2)), const((tm, tm)),
    ]
    out_shape = [
        jax.ShapeDtypeStruct((rows, QKV_W), F32), jax.ShapeDtypeStruct((rows, GDN_W), F32),
        jax.ShapeDtypeStruct((rows * N_KV, HEAD_DIM), F32), jax.ShapeDtypeStruct((rows * N_KV, HEAD_DIM), F32),
        jax.ShapeDtypeStruct((bp, IDX_DIM, tp), F32), jax.ShapeDtypeStruct((bp, 8, QKV_W), F32),
        jax.ShapeDtypeStruct((rows, LANES), F32), jax.ShapeDtypeStruct((rows // GDN_CHUNK, 8, GDN_CHUNK), F32),
        jax.ShapeDtypeStruct((ATT_W, rows), BF), jax.ShapeDtypeStruct((IDX_W, rows), BF),
        jax.ShapeDtypeStruct((IDX_HEADS, rows), F32), jax.ShapeDtypeStruct((rows, KV_W), BF),
        jax.ShapeDtypeStruct((rows // ATT_TK, N_KV * VT_ROWS, ATT_TK), BF), jax.ShapeDtypeStruct((rows, IDX_DIM), BF),
    ]
    out_specs = [
        rowblk(QKV_W), rowblk(GDN_W),
        pl.BlockSpec((tm * N_KV, HEAD_DIM), lambda b, i: (b * nt + i, 0)),
        pl.BlockSpec((tm * N_KV, HEAD_DIM), lambda b, i: (b * nt + i, 0)),
        pl.BlockSpec((None, IDX_DIM, tm), lambda b, i: (b, 0, i)),
        pl.BlockSpec((None, 8, QKV_W), lambda b, i: (b, 0, 0)),
        rowblk(LANES),
        pl.BlockSpec((tm // GDN_CHUNK, 8, GDN_CHUNK), lambda b, i: (b * nt + i, 0, 0)),
        colblk(ATT_W), colblk(IDX_W), colblk(IDX_HEADS), rowblk(KV_W),
        pl.BlockSpec((tm // ATT_TK, N_KV * VT_ROWS, ATT_TK), lambda b, i: (b * nt + i, 0, 0)),
        rowblk(IDX_DIM),
    ]
    return pl.pallas_call(
        functools.partial(_inproj_prompt_kernel, tm, tm // _pick(tm, (ATT_TK,))),
        grid=(bp, nt), in_specs=in_specs, out_specs=out_specs, out_shape=out_shape,
        scratch_shapes=[pltpu.VMEM((tm + 8, QKV_W), F32), pltpu.VMEM((T_ROWS, tm), F32)],
        compiler_params=_params(("arbitrary", "arbitrary")),
        name="inproj_prompt",
    )(x, mod, mod, g1.reshape(1, d), w_nat, w_t, w_conv, cs128, cs64, prow, pcol, tril)


def _inproj_sample(x, mod, g1, w_nat, w_t, w_conv, prow, pcol, conv_state, past):
    db, d = x.shape
    pos = jnp.full((db,), past, I32)
    cs128, cs64 = _rope_tables(pos, HEAD_DIM), _rope_tables(pos, IDX_DIM)
    full = lambda shape: pl.BlockSpec(shape, lambda i: (0,) * len(shape))
    in_specs = [
        full((db, d)),
        pl.BlockSpec((db, d), lambda i: (0, 1)), pl.BlockSpec((db, d), lambda i: (0, 0)),
        full((1, d)), full((d, NAT_W)), full((T_ROWS, d)), full((CONV_W, QKV_W)),
        full((HEAD_DIM // 4, db)), full((IDX_DIM // 4, db)), full((2, LANES)), full((16, 2)),
        full((db, QKV_W)), full((db, QKV_W)), full((db, QKV_W)),
    ]
    widths = [QKV_W, GDN_W, KV_W, KV_W, LANES, QKV_W, LANES, ATT_W, IDX_W]
    return pl.pallas_call(
        _inproj_sample_kernel,
        grid=(1,), in_specs=in_specs,
        out_specs=[full((db, w)) for w in widths],
        out_shape=[jax.ShapeDtypeStruct((db, w), F32) for w in widths],
        scratch_shapes=[pltpu.VMEM((T_ROWS, db), F32)],
        compiler_params=_params(("arbitrary",)),
        name="inproj_sample",
    )(x, mod, mod, g1.reshape(1, d), w_nat, w_t, w_conv, cs128, cs64, prow, pcol,
      conv_state[:, 0], conv_state[:, 1], conv_state[:, 2])


def _unit_lower_inverses(lows):
    c = lows[0].shape[0]
    ii = lax.broadcasted_iota(I32, (c, c), 0)
    jj = lax.broadcasted_iota(I32, (c, c), 1)
    eye = jnp.where(ii == jj, 1.0, 0.0)
    levels = int(math.log2(c)) - 1
    ts = [eye - low for low in lows]
    ps = [_mm_hi(low, low) for low in lows]
    for lvl in range(levels):
        if lvl == levels - 1:
            ts = [t + _mm_hi(t, p) for t, p in zip(ts, ps)]
        else:
            both = [_mm_hi(jnp.concatenate([t, p], axis=0), p) for t, p in zip(ts, ps)]
            ts = [t + b[0:c] for t, b in zip(ts, both)]
            ps = [b[c:] for b in both]
    return ts


def _gdn_prompt_kernel(tg, qkvn_ref, z_ref, misc_ref, gct_ref, gn_ref, o_ref, ssm_ref, s_ref):
    i = pl.program_id(1)

    @pl.when(i == 0)
    def _():
        s_ref[...] = jnp.zeros(s_ref.shape, F32)

    c = GDN_CHUNK
    ii = lax.broadcasted_iota(I32, (c, c), 0)
    jj = lax.broadcasted_iota(I32, (c, c), 1)
    pairs = [(cc, hd) for cc in range(tg // c) for hd in range(GDN_HEADS)]

    qs, ks, gcs, rhs, lows, intras = {}, {}, {}, {}, [], {}
    for cc, hd in pairs:
        r0, lo = cc * c, hd * HEAD_DIM
        q = qkvn_ref[r0:r0 + c, lo:lo + HEAD_DIM]
        k = qkvn_ref[r0:r0 + c, GDN_W + lo:GDN_W + lo + HEAD_DIM]
        v = qkvn_ref[r0:r0 + c, 2 * GDN_W + lo:2 * GDN_W + lo + HEAD_DIM]
        beta_c = misc_ref[r0:r0 + c, hd:hd + 1]
        gc_c = misc_ref[r0:r0 + c, 2 * GDN_HEADS + hd:2 * GDN_HEADS + hd + 1]
        gc_r = gct_ref[cc, GDN_HEADS + hd:GDN_HEADS + hd + 1, :]
        decay = jnp.where(ii >= jj, jnp.exp(jnp.where(ii >= jj, gc_c - gc_r, 0.0)), 0.0)
        kb = k * beta_c
        both = _dot_nt(jnp.concatenate([kb, q], axis=0).astype(BF), k.astype(BF))
        lows.append(jnp.where(ii > jj, both[0:c] * decay, 0.0))
        intras[cc, hd] = (both[c:] * decay).astype(BF)
        rhs[cc, hd] = jnp.concatenate([v * beta_c, kb * jnp.exp(gc_c)], axis=1).astype(BF)
        qs[cc, hd], ks[cc, hd], gcs[cc, hd] = q, k, gc_c
    ts = _unit_lower_inverses(lows)
    uws = {p: _dot(t.astype(BF), rhs[p]) for p, t in zip(pairs, ts)}

    for cc, hd in pairs:
        r0, lo = cc * c, hd * HEAD_DIM
        q, k, gc_c, uw = qs[cc, hd], ks[cc, hd], gcs[cc, hd], uws[cc, hd]
        s = s_ref[hd]
        ws = _dot(jnp.concatenate([uw[:, HEAD_DIM:], q * jnp.exp(gc_c)], axis=0).astype(BF), s.astype(BF))
        v_new = (uw[:, 0:HEAD_DIM] - ws[0:c]).astype(BF)
        o = ws[c:] + _dot(intras[cc, hd], v_new)
        g_last = gc_c[c - 1:c, :]
        kd = k * jnp.exp(g_last - gc_c)
        s_ref[hd] = s * jnp.exp(g_last) + _dot(kd.T.astype(BF), v_new)
        o = _rmsnorm(o, gn_ref[...]) * _silu(z_ref[r0:r0 + c, lo:lo + HEAD_DIM])
        o_ref[r0:r0 + c, lo:lo + HEAD_DIM] = o.astype(BF)
    ssm_ref[...] = s_ref[...]


def _gdn_prompt(qkvn, z, misc, gct, g_norm, bp, tp, tg):
    nt = tp // tg
    rows = bp * tp
    rowblk = lambda w: pl.BlockSpec((tg, w), lambda b, i: (b * nt + i, 0))
    return pl.pallas_call(
        functools.partial(_gdn_prompt_kernel, tg),
        grid=(bp, nt),
        in_specs=[rowblk(QKV_W), rowblk(GDN_W), rowblk(LANES),
                  pl.BlockSpec((tg // GDN_CHUNK, 8, GDN_CHUNK), lambda b, i: (b * nt + i, 0, 0)),
                  pl.BlockSpec((1, HEAD_DIM), lambda b, i: (0, 0))],
        out_specs=[rowblk(GDN_W),
                   pl.BlockSpec((None, GDN_HEADS, HEAD_DIM, HEAD_DIM), lambda b, i: (b, 0, 0, 0))],
        out_shape=[jax.ShapeDtypeStruct((rows, GDN_W), BF),
                   jax.ShapeDtypeStruct((bp, GDN_HEADS, HEAD_DIM, HEAD_DIM), F32)],
        scratch_shapes=[pltpu.VMEM((GDN_HEADS, HEAD_DIM, HEAD_DIM), F32)],
        compiler_params=_params(("arbitrary", "arbitrary")),
        name="gdn_prompt",
    )(qkvn, z, misc, gct, g_norm.reshape(1, HEAD_DIM))


def _gdn_sample_kernel(nb, qkvn_ref, z_ref, misc_ref, gn_ref, s_ref, o_ref, ssm_ref):
    for bi in range(nb):
        for hd in range(GDN_HEADS):
            lo = hd * HEAD_DIM
            q = qkvn_ref[bi:bi + 1, lo:lo + HEAD_DIM]
            k = qkvn_ref[bi:bi + 1, GDN_W + lo:GDN_W + lo + HEAD_DIM]
            v = qkvn_ref[bi:bi + 1, 2 * GDN_W + lo:2 * GDN_W + lo + HEAD_DIM]
            beta = misc_ref[bi:bi + 1, hd:hd + 1]
            g = misc_ref[bi:bi + 1, GDN_HEADS + hd:GDN_HEADS + hd + 1]
            kcol = jnp.broadcast_to(k, (HEAD_DIM, HEAD_DIM)).T
            qcol = jnp.broadcast_to(q, (HEAD_DIM, HEAD_DIM)).T
            s = s_ref[bi, hd] * jnp.exp(g)
            ks = jnp.sum(kcol * s, axis=0, keepdims=True)
            delta = (v - ks) * beta
            s = s + kcol * delta
            ssm_ref[bi, hd] = s
            o = jnp.sum(qcol * s, axis=0, keepdims=True)
            o = _rmsnorm(o, gn_ref[...]) * _silu(z_ref[bi:bi + 1, lo:lo + HEAD_DIM])
            o_ref[bi:bi + 1, lo:lo + HEAD_DIM] = o.astype(BF)


def _gdn_sample(qkvn, z, misc, g_norm, state, nb):
    db = qkvn.shape[0]
    rowblk = lambda w: pl.BlockSpec((nb, w), lambda i: (i, 0))
    sblk = pl.BlockSpec((nb, GDN_HEADS, HEAD_DIM, HEAD_DIM), lambda i: (i, 0, 0, 0))
    return pl.pallas_call(
        functools.partial(_gdn_sample_kernel, nb),
        grid=(db // nb,),
        in_specs=[rowblk(QKV_W), rowblk(GDN_W), rowblk(LANES), pl.BlockSpec((1, HEAD_DIM), lambda i: (0, 0)), sblk],
        out_specs=[rowblk(GDN_W), sblk],
        out_shape=[jax.ShapeDtypeStruct((db, GDN_W), BF), jax.ShapeDtypeStruct(state.shape, F32)],
        compiler_params=_params(("arbitrary",)),
        name="gdn_sample",
    )(qkvn, z, misc, g_norm.reshape(1, HEAD_DIM), state)


def _ordered_word_to_float(u):
    s = u ^ INT_MIN
    return lax.bitcast_convert_type(s ^ ((s >> 31) & np.int32(0x7FFFFFFF)), F32)


def _select_topk_bias(i_ref, t_ref, nch, tk, k_sel, idx_bits, taken=0.0, dropped=NEG_INF):
    nl = i_ref.shape[1]
    row = lax.broadcasted_iota(I32, (tk, 1), 0)

    def chunk(c):
        return pl.ds(pl.multiple_of(c * tk, tk), tk)

    def count(ref, pred):
        def body(c, acc):
            m = pred(ref[chunk(c), :])
            return acc + jnp.sum(jnp.where(m, 1, 0).astype(I32).reshape(tk // 8, 8, nl), axis=0)
        acc = lax.fori_loop(0, nch, body, jnp.zeros((8, nl), I32))
        return jnp.sum(acc, axis=0, keepdims=True)

    def bit_body(i, carry):
        t_u, n_ge = carry
        cand_u = t_u | jnp.left_shift(jnp.int32(1), 31 - i)
        cand = _ordered_word_to_float(cand_u)
        cnt = count(i_ref, lambda sc: sc >= cand)
        take = cnt >= k_sel
        return jnp.where(take, cand_u, t_u), jnp.where(take, cnt, n_ge)

    t_u, n_ge = lax.fori_loop(0, 32, bit_body, (jnp.zeros((1, nl), I32), jnp.full((1, nl), -1, I32)))
    thr = jnp.where((t_u >> 23) == 0, NEG_INF, _ordered_word_to_float(t_u))

    def tie_break(_):
        need = k_sel - count(i_ref, lambda sc: sc > thr)

        def mark(c, carry):
            t_ref[chunk(c), :] = jnp.where(i_ref[chunk(c), :] == thr, c * tk + row, 2 ** idx_bits)
            return carry

        lax.fori_loop(0, nch, mark, 0)

        def idx_body(i, x):
            cand = x | jnp.left_shift(jnp.int32(1), idx_bits - 1 - i)
            cnt = count(t_ref, lambda idx: idx < cand)
            return jnp.where(cnt < need, cand, x)

        return lax.fori_loop(0, idx_bits, idx_body, jnp.zeros((1, nl), I32))

    x = lax.cond(jnp.max(n_ge) > k_sel, tie_break, lambda _: jnp.full((1, nl), 2 ** idx_bits - 1, I32), 0)

    def write(c, carry):
        sc = i_ref[chunk(c), :]
        sel = (sc > thr) | ((sc == thr) & (c * tk + row <= x))
        i_ref[chunk(c), :] = jnp.where(sel & (sc > NEG_INF), taken, dropped)
        return carry

    lax.fori_loop(0, nch, write, 0)


def _attn_prompt_kernel(tq, tk, blk, k_sel, idx_bits, qT_ref, iqT_ref, iwT_ref, kbf_ref, vT_ref, ikbf_ref, o_ref,
                        i_ref, t_ref):
    t0 = pl.program_id(1) * tq
    nch = (t0 + tq + tk - 1) // tk
    nblk = (t0 + tq + blk - 1) // blk
    qpos = t0 + lax.broadcasted_iota(I32, (1, tq), 1)
    row = lax.broadcasted_iota(I32, (tk, 1), 0)
    w = iwT_ref[...]
    pairs = IDX_HEADS // 2
    rhs = [jnp.concatenate([iqT_ref[(2 * p) * IDX_DIM:(2 * p + 1) * IDX_DIM, :],
                            iqT_ref[(2 * p + 1) * IDX_DIM:(2 * p + 2) * IDX_DIM, :]], axis=1) for p in range(pairs)]

    def chunk(c):
        return pl.ds(pl.multiple_of(c * tk, tk), tk)

    def idx_body(c, carry):
        ikc = ikbf_ref[chunk(c), :]
        acc = jnp.zeros((tk, tq), F32)
        for p in range(pairs):
            d = jnp.maximum(_dot(ikc, rhs[p]), 0.0)
            acc = acc + d[:, 0:tq] * w[2 * p:2 * p + 1, :] + d[:, tq:] * w[2 * p + 1:2 * p + 2, :]
        i_ref[chunk(c), :] = jnp.where(c * tk + row <= qpos, acc, NEG_INF)
        return carry

    lax.fori_loop(0, nch, idx_body, 0)

    def fill_body(c, carry):
        i_ref[chunk(c), :] = jnp.full((tk, tq), NEG_INF, F32)
        return carry

    lax.fori_loop(nch, nblk * (blk // tk), fill_body, 0)
    tks = max(tk * LANES // tq, 8)
    _select_topk_bias(i_ref, t_ref, (t0 + tq + tks - 1) // tks, tks, k_sel, idx_bits)

    scale2 = HEAD_DIM ** -0.5 * math.log2(math.e)
    rep = ATT_HEADS // N_KV
    sub = ATT_TK
    nsub = blk // sub
    qg = [jnp.concatenate([qT_ref[(rep * g + r) * HEAD_DIM:(rep * g + r + 1) * HEAD_DIM, :] for r in range(rep)],
                          axis=1) for g in range(N_KV)]

    def body(c, carry):
        out = []
        for g in range(N_KV):
            m, acc = carry[g]
            ss = []
            for i in range(nsub):
                rows = pl.ds(pl.multiple_of(c * blk + i * sub, sub), sub)
                bias = i_ref[rows, :]
                ss.append(_dot(kbf_ref[rows, g * HEAD_DIM:(g + 1) * HEAD_DIM], qg[g]) * scale2
                          + jnp.concatenate([bias] * rep, axis=1))
            m_new = m
            for s in ss:
                m_new = jnp.maximum(m_new, jnp.max(s, axis=0, keepdims=True))
            m_safe = jnp.where(m_new == NEG_INF, 0.0, m_new)
            acc = acc * jnp.exp2(m - m_safe)
            for i, s in enumerate(ss):
                p = jnp.exp2(s - m_safe).astype(BF)
                acc = acc + _dot(vT_ref[c * nsub + i, g * VT_ROWS:(g + 1) * VT_ROWS, :], p)
            out.append((m_new, acc))
        return tuple(out)

    init = tuple((jnp.full((1, rep * tq), NEG_INF, F32), jnp.zeros((VT_ROWS, rep * tq), F32)) for _ in range(N_KV))
    res = lax.fori_loop(0, nblk, body, init)
    for g in range(N_KV):
        acc = res[g][1]
        o_t = acc[0:HEAD_DIM] / acc[HEAD_DIM:HEAD_DIM + 1]
        for r in range(rep):
            hd = rep * g + r
            o_ref[:, hd * HEAD_DIM:(hd + 1) * HEAD_DIM] = o_t[:, r * tq:(r + 1) * tq].T.astype(BF)


def _attn_prompt(qT, iqT, iwT, kbf, vT, ikbf, bp, tp):
    tq = _pick(tp, (ATT_TQ, QBLOCK))
    tk = _pick(tp, (IDX_TK, ATT_TK))
    blk = _pick(tp, (ATT_BLK, IDX_TK, ATT_TK))
    assert blk % tk == 0 and blk % ATT_TK == 0
    nq = tp // tq
    rows = bp * tp
    k_sel = min(TOPK_MAX, tp // 4)
    idx_bits = max(1, int(tp - 1).bit_length())
    colblk = lambda h: pl.BlockSpec((h, tq), lambda b, j: (0, b * nq + j))
    return pl.pallas_call(
        functools.partial(_attn_prompt_kernel, tq, tk, blk, k_sel, idx_bits),
        grid=(bp, nq),
        in_specs=[colblk(ATT_W), colblk(IDX_W), colblk(IDX_HEADS),
                  pl.BlockSpec((tp, KV_W), lambda b, j: (b, 0)),
                  pl.BlockSpec((tp // ATT_TK, N_KV * VT_ROWS, ATT_TK), lambda b, j: (b, 0, 0)),
                  pl.BlockSpec((tp, IDX_DIM), lambda b, j: (b, 0))],
        out_specs=pl.BlockSpec((tq, ATT_W), lambda b, j: (b * nq + j, 0)),
        out_shape=jax.ShapeDtypeStruct((rows, ATT_W), BF),
        scratch_shapes=[pltpu.VMEM((tp, tq), F32), pltpu.VMEM((tp, tq), I32)],
        compiler_params=_params(("arbitrary", "arbitrary")),
        name="attn_prompt",
    )(qT, iqT, iwT, kbf, vT, ikbf)


def _sidx_kernel(npg, pt_ref, iq_ref, iw_ref, ikn_ref, *refs):
    pages, out_ref = refs[:npg], refs[npg]
    iq = iq_ref[...].astype(BF)
    w = iw_ref[...]
    for p in range(npg):
        d = jnp.maximum(_dot(iq, pages[p][...].astype(BF)), 0.0)
        out_ref[:, p * LANES:(p + 1) * LANES] = jnp.sum(d * w, axis=0, keepdims=True)
    dn = jnp.sum(iq.astype(F32) * ikn_ref[...].astype(BF).astype(F32), axis=1, keepdims=True)
    sn = jnp.sum(jnp.maximum(dn, 0.0) * w, axis=0, keepdims=True)
    lane = lax.broadcasted_iota(I32, (1, LANES), 1)
    out_ref[:, npg * LANES:(npg + 1) * LANES] = jnp.where(lane == 0, sn, NEG_INF)


def _sample_index_scores(iq, iw, ik_new, cache_ik_t, page_table):
    db, npg = page_table.shape
    page = cache_ik_t.shape[2]
    assert page == LANES
    width = (npg + 1) * LANES
    page_specs = [pl.BlockSpec((None, IDX_DIM, page), functools.partial(lambda p, b, pt: (pt[b, p], 0, 0), p))
                  for p in range(npg)]
    grid_spec = pltpu.PrefetchScalarGridSpec(
        num_scalar_prefetch=1, grid=(db,),
        in_specs=[pl.BlockSpec((None, IDX_HEADS, IDX_DIM), lambda b, pt: (b, 0, 0)),
                  pl.BlockSpec((None, IDX_HEADS, 1), lambda b, pt: (b, 0, 0)),
                  pl.BlockSpec((None, 1, IDX_DIM), lambda b, pt: (b, 0, 0))] + page_specs,
        out_specs=pl.BlockSpec((None, 1, width), lambda b, pt: (b, 0, 0)))
    out = pl.pallas_call(
        functools.partial(_sidx_kernel, npg), grid_spec=grid_spec,
        out_shape=jax.ShapeDtypeStruct((db, 1, width), F32),
        compiler_params=_params(("arbitrary",)),
        name="sample_index_scores",
    )(page_table, iq.reshape(db, IDX_HEADS, IDX_DIM), iw.reshape(db, IDX_HEADS, 1), ik_new.reshape(db, 1, IDX_DIM),
      *([cache_ik_t] * npg))
    return out.reshape(db, width)


def _ssel_kernel(nch, k_sel, idx_bits, s_ref, spread_ref, keep_ref, i_ref, t_ref):
    for p in range(nch):
        i_ref[p * LANES:(p + 1) * LANES, :] = s_ref[:, p * LANES:(p + 1) * LANES].T
    _select_topk_bias(i_ref, t_ref, nch, LANES, k_sel, idx_bits, taken=1.0, dropped=0.0)
    rows = N_KV * LANES
    for p in range(nch - 1):
        k2 = _dot(spread_ref[...], i_ref[p * LANES:(p + 1) * LANES, :].astype(BF))
        for h in range(N_KV):
            keep_ref[:, p * rows + h * LANES:p * rows + (h + 1) * LANES] = k2[h * LANES:(h + 1) * LANES, :].T
    keep_ref[:, (nch - 1) * rows:(nch - 1) * rows + LANES] = i_ref[(nch - 1) * LANES:nch * LANES, :].T


def _sample_select(scores, n_keys):
    db, width = scores.shape
    assert db == LANES
    nch = width // LANES
    k_sel = min(TOPK_MAX, n_keys // 4)
    idx_bits = max(1, int(width - 1).bit_length())
    rows = N_KV * LANES
    out_w = (nch - 1) * rows + LANES
    spread = jnp.asarray(np.arange(rows)[:, None] // N_KV == np.arange(LANES)[None, :], BF)
    return pl.pallas_call(
        functools.partial(_ssel_kernel, nch, k_sel, idx_bits),
        grid=(1,),
        in_specs=[pl.BlockSpec((db, width), lambda i: (0, 0)), pl.BlockSpec((rows, LANES), lambda i: (0, 0))],
        out_specs=pl.BlockSpec((db, out_w), lambda i: (0, 0)),
        out_shape=jax.ShapeDtypeStruct((db, out_w), F32),
        scratch_shapes=[pltpu.VMEM((width, db), F32), pltpu.VMEM((width, db), I32)],
        compiler_params=_params(("arbitrary",)),
        name="sample_select",
    )(scores, spread)


def _sattn_kernel(npg, pt_ref, q_ref, keep_ref, kn_ref, vn_ref, *refs):
    kpages, vpages, o_ref = refs[:npg], refs[npg:2 * npg], refs[2 * npg]
    rep = ATT_HEADS // N_KV
    rows = N_KV * LANES
    scale = HEAD_DIM ** -0.5
    q = q_ref[...]
    q8 = jnp.concatenate([q, jnp.zeros((8 - ATT_HEADS, HEAD_DIM), F32)], axis=0).astype(BF)
    hrow = lax.broadcasted_iota(I32, (8, 1), 0)
    lane = lax.broadcasted_iota(I32, (1, LANES), 1)
    own_kv = lax.broadcasted_iota(I32, (1, rows), 1) % N_KV == hrow // rep
    parts = []
    for p in range(npg):
        s2 = _dot_nt(q8, kpages[p][...].astype(BF))
        keep2 = keep_ref[:, p * rows:(p + 1) * rows]
        parts.append(jnp.where(own_kv & (keep2 > 0.5), s2 * scale, NEG_INF))
    kn = kn_ref[...].astype(BF).astype(F32)
    kn8 = jnp.where(hrow // rep == 0, kn[0:1, :], kn[1:2, :])
    s_new = jnp.sum(q8.astype(F32) * kn8, axis=1, keepdims=True)
    keep_new = keep_ref[:, npg * rows:npg * rows + 1]
    parts.append(jnp.where((lane == 0) & (keep_new > 0.5), s_new * scale, NEG_INF))
    s = jnp.concatenate(parts, axis=1)
    m = jnp.max(s, axis=1, keepdims=True)
    e = jnp.exp(s - m)
    pr = (e / jnp.sum(e, axis=1, keepdims=True)).astype(BF)
    o8 = jnp.zeros((8, HEAD_DIM), F32)
    for p in range(npg):
        o8 = o8 + _dot(pr[:, p * rows:(p + 1) * rows], vpages[p][...].astype(BF))
    vn = vn_ref[...].astype(BF).astype(F32)
    vn8 = jnp.where(hrow // rep == 0, vn[0:1, :], vn[1:2, :])
    o8 = o8 + pr[:, npg * rows:npg * rows + 1].astype(F32) * vn8
    for hd in range(ATT_HEADS):
        o_ref[:, hd * HEAD_DIM:(hd + 1) * HEAD_DIM] = o8[hd:hd + 1, :].astype(BF)


def _sample_attention(q, keep, k_new, v_new, cache_k, cache_v, page_table):
    db, npg = page_table.shape
    n_phys, page = cache_k.shape[0], cache_k.shape[1]
    assert page == LANES and N_KV == 2
    rows = page * N_KV
    width = npg * rows + LANES
    ck = cache_k.reshape(n_phys, rows, HEAD_DIM)
    cv = cache_v.reshape(n_phys, rows, HEAD_DIM)
    page_specs = [pl.BlockSpec((None, rows, HEAD_DIM), functools.partial(lambda p, b, pt: (pt[b, p], 0, 0), p))
                  for p in range(npg)]
    grid_spec = pltpu.PrefetchScalarGridSpec(
        num_scalar_prefetch=1, grid=(db,),
        in_specs=[pl.BlockSpec((None, ATT_HEADS, HEAD_DIM), lambda b, pt: (b, 0, 0)),
                  pl.BlockSpec((None, 1, width), lambda b, pt: (b, 0, 0)),
                  pl.BlockSpec((None, N_KV, HEAD_DIM), lambda b, pt: (b, 0, 0)),
                  pl.BlockSpec((None, N_KV, HEAD_DIM), lambda b, pt: (b, 0, 0))] + page_specs + page_specs,
        out_specs=pl.BlockSpec((None, 1, ATT_W), lambda b, pt: (b, 0, 0)))
    out = pl.pallas_call(
        functools.partial(_sattn_kernel, npg), grid_spec=grid_spec,
        out_shape=jax.ShapeDtypeStruct((db, 1, ATT_W), BF),
        compiler_params=_params(("arbitrary",)),
        name="sample_attention",
    )(page_table, q.reshape(db, ATT_HEADS, HEAD_DIM), keep.reshape(db, 1, width),
      k_new.reshape(db, N_KV, HEAD_DIM), v_new.reshape(db, N_KV, HEAD_DIM), *([ck] * npg), *([cv] * npg))
    return out.reshape(db, ATT_W)


def _post_kernel(final, nff, og_ref, oa_ref, x_ref, ga1_ref, sc2_ref, sh2_ref, ga2_ref, g2_ref, gf_ref,
                 wo_ref, wg_ref, wu_ref, wd_ref, y_ref, x1_ref, h2_ref, acc_ref):
    jf = pl.program_id(1)

    @pl.when(jf == 0)
    def _():
        mixed = _dot(og_ref[...], wo_ref[0:GDN_W, :]) + _dot(oa_ref[...], wo_ref[GDN_W:, :])
        x1 = x_ref[...] + ga1_ref[...] * mixed
        x1_ref[...] = x1
        h2_ref[...] = (_rmsnorm(x1, g2_ref[...]) * (1.0 + sc2_ref[...]) + sh2_ref[...]).astype(BF)
        acc_ref[...] = jnp.zeros(acc_ref.shape, F32)

    h2 = h2_ref[...]
    act = (_silu(_dot(h2, wg_ref[...])) * _dot(h2, wu_ref[...])).astype(BF)
    acc_ref[...] += _dot(act, wd_ref[...])

    @pl.when(jf == nff - 1)
    def _():
        x2 = x1_ref[...] + ga2_ref[...] * acc_ref[...]
        y_ref[...] = _rmsnorm(x2, gf_ref[...]) if final else x2


def _post(og, oa, x2d, mod, mod_spec, g2, gf, wo, wfi, wfo, tm, tf, final):
    rows, d = x2d.shape
    dff = wfo.shape[0]
    nff = dff // tf
    rowblk = lambda w: pl.BlockSpec((tm, w), lambda i, j: (i, 0))
    const = lambda shape: pl.BlockSpec(shape, lambda i, j: (0,) * len(shape))
    in_specs = [rowblk(GDN_W), rowblk(ATT_W), rowblk(d),
                mod_spec(2), mod_spec(4), mod_spec(3), mod_spec(5),
                const((1, d)), const((1, d)), const((d, d)),
                pl.BlockSpec((d, tf), lambda i, j: (0, j)),
                pl.BlockSpec((d, tf), lambda i, j: (0, nff + j)),
                pl.BlockSpec((tf, d), lambda i, j: (j, 0))]
    return pl.pallas_call(
        functools.partial(_post_kernel, final, nff),
        grid=(rows // tm, nff), in_specs=in_specs, out_specs=rowblk(d),
        out_shape=jax.ShapeDtypeStruct((rows, d), F32),
        scratch_shapes=[pltpu.VMEM((tm, d), F32), pltpu.VMEM((tm, d), BF), pltpu.VMEM((tm, d), F32)],
        compiler_params=_params(("arbitrary", "arbitrary")),
        name="post",
    )(og, oa, x2d, mod, mod, mod, mod, g2.reshape(1, d), gf.reshape(1, d), wo, wfi, wfi, wfo)


def _pick(n, prefs):
    for p in prefs:
        if n % p == 0:
            return p
    return n


def kernel(x_prompt, x_sample, c_prompt, c_sample, cache_k, cache_v, cache_idx_k, page_table, state_conv, state_ssm,
           w_ada, b_ada, g_norm1, w_in, w_conv, a_log, dt_bias, g_gdn_norm, w_out, g_norm2, w_ffn_in, w_ffn_out,
           g_final):
    bp, tp, d = x_prompt.shape
    db, ts, _ = x_sample.shape
    assert ts == 1 and d == GDN_W + ATT_W and tp % max(ATT_TK, GDN_CHUNK) == 0
    depth = w_in.shape[0]
    npg, page = page_table.shape[1], cache_k.shape[2]
    past = npg * page
    tm_in = _pick(tp, (512, 256))
    tg = _pick(tp, (512, 256))
    tm_post = _pick(tp, (1024, 512, 256))
    dff = w_ffn_out.shape[1]
    tf = _pick(dff, (256, 128))

    xp = x_prompt
    xs = x_sample.reshape(db, d)
    npad = (-(bp + db)) % 8
    c_all = jnp.concatenate([c_prompt, c_sample, jnp.zeros((npad, d), F32)], axis=0)
    new_p, new_s = [], []
    for l in range(depth):
        final = l == depth - 1
        mod = _modulation(c_all, w_ada[l], b_ada[l])
        mod_p = mod[:bp].reshape(bp, 1, 6 * d)
        mod_s = mod[bp:bp + db]
        w_nat, w_t, prow, pcol = _inproj_weights(w_in[l], a_log[l], dt_bias[l])
        wo = w_out[l].astype(BF)
        wfi = w_ffn_in[l].astype(BF)
        wfo = w_ffn_out[l].astype(BF)

        (qkvn, z, vnat, knat, ikT, tail, misc, gct, qT, iqT, iwT, kbf, vT, ikbf) = _inproj_prompt(
            xp, mod_p, g_norm1[l], w_nat, w_t, w_conv[l], prow, pcol, tm_in)
        og, ssm_p = _gdn_prompt(qkvn, z, misc, gct, g_gdn_norm[l], bp, tp, tg)
        oa = _attn_prompt(qT, iqT, iwT, kbf, vT, ikbf, bp, tp)
        tiles_b = tp // tm_post
        spec_p = lambda k: pl.BlockSpec((None, 1, d), lambda i, j: (i // tiles_b, 0, k))
        xp = _post(og, oa, xp.reshape(bp * tp, d), mod_p, spec_p, g_norm2[l], g_final, wo, wfi, wfo,
                   tm_post, tf, final).reshape(bp, tp, d)
        new_p.append((knat.reshape(bp, tp, N_KV, HEAD_DIM), vnat.reshape(bp, tp, N_KV, HEAD_DIM),
                      jnp.swapaxes(ikT, 1, 2), tail[:, 8 - (CONV_W - 1):, :], ssm_p))

        (qkvn_s, z_s, vnat_s, knat_s, slab_s, raw_s, misc_s, aq_s, iq_s) = _inproj_sample(
            xs, mod_s, g_norm1[l], w_nat, w_t, w_conv[l], prow, pcol, state_conv[l], past)
        og_s, ssm_s = _gdn_sample(qkvn_s, z_s, misc_s, g_gdn_norm[l], state_ssm[l], _pick(db, (8,)))
        ik_s = slab_s[:, 0:IDX_DIM]
        iw_s = slab_s[:, IDX_DIM:IDX_DIM + IDX_HEADS]
        scores = _sample_index_scores(iq_s, iw_s, ik_s, jnp.swapaxes(cache_idx_k[l], 1, 2), page_table)
        keep = _sample_select(scores, past + ts)
        oa_s = _sample_attention(aq_s, keep, knat_s, vnat_s, cache_k[l], cache_v[l], page_table)
        spec_s = lambda k: pl.BlockSpec((db, d), lambda i, j: (i, k))
        xs = _post(og_s, oa_s, xs, mod_s, spec_s, g_norm2[l], g_final, wo, wfi, wfo, db, tf, final)
        new_conv_s = jnp.concatenate([state_conv[l][:, 1:, :], raw_s[:, None, :]], axis=1)
        new_s.append((knat_s.reshape(db, ts, N_KV, HEAD_DIM), vnat_s.reshape(db, ts, N_KV, HEAD_DIM),
                      ik_s.reshape(db, ts, IDX_DIM), new_conv_s, ssm_s))

    stack = lambda states, n: jnp.stack([s[n] for s in states], axis=0)
    return (xp, xs.reshape(db, ts, d),
            stack(new_p, 0), stack(new_p, 1), stack(new_p, 2), stack(new_p, 3), stack(new_p, 4),
            stack(new_s, 0), stack(new_s, 1), stack(new_s, 2), stack(new_s, 3), stack(new_s, 4))
```

```python
import functools
import math

import numpy as np
import jax
import jax.numpy as jnp
from jax import lax
from jax.experimental import pallas as pl
from jax.experimental.pallas import tpu as pltpu

F32 = jnp.float32
BF = jnp.bfloat16
I32 = jnp.int32

HEAD_DIM = 128
GDN_HEADS = 4
ATT_HEADS = 4
N_KV = 2
IDX_HEADS = 8
IDX_DIM = 64
CONV_W = 4
TOPK_MAX = 256
QBLOCK = 128
ROPE_THETA = 500000.0
EPS = 1e-6
GDN_W = GDN_HEADS * HEAD_DIM
ATT_W = ATT_HEADS * HEAD_DIM
KV_W = N_KV * HEAD_DIM
IDX_W = IDX_HEADS * IDX_DIM
QKV_W = 3 * GDN_W

GDN_CHUNK = 128
ATT_TK = 256
VT_ROWS = HEAD_DIM + 16
IDX_TK = 512
ATT_BLK = 1024
ATT_TQ = 512
LANES = 128
NAT_W = QKV_W + GDN_W + KV_W + LANES
T_AQ, T_AK, T_IQ, T_IK, T_MISC, T_AV = 0, 512, 768, 1280, 1344, 1360
T_ROWS = T_AV + KV_W
VMEM_LIMIT = 56 * 1024 * 1024
INT_MIN = np.int32(-2 ** 31)
NEG_INF = float("-inf")


def _params(sem):
    return pltpu.CompilerParams(dimension_semantics=sem, vmem_limit_bytes=VMEM_LIMIT)


def _dot(a, b):
    return jnp.dot(a, b, preferred_element_type=F32)


def _dot_nt(a, b):
    return lax.dot_general(a, b, (((1,), (1,)), ((), ())), preferred_element_type=F32)


def _split3(x):
    hi = x.astype(BF)
    r1 = x - hi.astype(F32)
    mid = r1.astype(BF)
    lo = (r1 - mid.astype(F32)).astype(BF)
    return hi, mid, lo


def _mm_hi(a, b):
    ah = a.astype(BF)
    al = (a - ah.astype(F32)).astype(BF)
    bh = b.astype(BF)
    bl = (b - bh.astype(F32)).astype(BF)
    m = a.shape[0]
    hi = _dot(jnp.concatenate([ah, al], axis=0), bh)
    return hi[0:m] + (hi[m:] + _dot(ah, bl))


def _sigmoid(x):
    return 1.0 / (1.0 + jnp.exp(-x))


def _silu(x):
    return x * _sigmoid(x)


def _softplus(x):
    return jnp.maximum(x, 0.0) + jnp.log(1.0 + jnp.exp(-jnp.abs(x)))


def _rmsnorm(x, g):
    return x * lax.rsqrt(jnp.mean(x * x, axis=-1, keepdims=True) + EPS) * g


def _mod_kernel(c_ref, w_ref, b_ref, o_ref):
    s = _silu(c_ref[...]).astype(BF)
    o_ref[...] = _dot(s, w_ref[...].astype(BF)) + b_ref[...]


def _modulation(c_all, w_ada, b_ada):
    n, d = c_all.shape
    cols = w_ada.shape[1]
    tn = d
    return pl.pallas_call(
        _mod_kernel,
        grid=(cols // tn,),
        in_specs=[pl.BlockSpec((n, d), lambda j: (0, 0)),
                  pl.BlockSpec((d, tn), lambda j: (0, j)),
                  pl.BlockSpec((1, tn), lambda j: (0, j))],
        out_specs=pl.BlockSpec((n, tn), lambda j: (0, j)),
        out_shape=jax.ShapeDtypeStruct((n, cols), F32),
        compiler_params=_params(("arbitrary",)),
        name="modulation",
    )(c_all, w_ada, b_ada.reshape(1, cols))


def _rope_rows(rt_ref, base, half, cos, sin):
    x1 = rt_ref[base:base + half, :]
    x2 = rt_ref[base + half:base + 2 * half, :]
    rt_ref[base:base + half, :] = x1 * cos - x2 * sin
    rt_ref[base + half:base + 2 * half, :] = x2 * cos + x1 * sin


def _project(x_ref, sc_ref, sh_ref, g1_ref, wn_ref, wt_ref, cs128_ref, cs64_ref, pcol_ref, rt_ref):
    h = _rmsnorm(x_ref[...], g1_ref[...]) * (1.0 + sc_ref[...]) + sh_ref[...]
    hb = h.astype(BF)
    nat = _dot(hb, wn_ref[...])
    rt_ref[...] = _dot_nt(wt_ref[...], hb)
    half = HEAD_DIM // 8
    cos, sin = cs128_ref[0:half, :], cs128_ref[half:2 * half, :]
    for hd in range(ATT_HEADS):
        _rope_rows(rt_ref, T_AQ + hd * HEAD_DIM, half, cos, sin)
    for hd in range(N_KV):
        _rope_rows(rt_ref, T_AK + hd * HEAD_DIM, half, cos, sin)
    half = IDX_DIM // 8
    cos, sin = cs64_ref[0:half, :], cs64_ref[half:2 * half, :]
    for hd in range(IDX_HEADS):
        _rope_rows(rt_ref, T_IQ + hd * IDX_DIM, half, cos, sin)
    _rope_rows(rt_ref, T_IK, half, cos, sin)
    mt = rt_ref[T_MISC:T_MISC + 16, :]
    r = lax.broadcasted_iota(I32, mt.shape, 0)
    a_log, dt_b = pcol_ref[:, 0:1], pcol_ref[:, 1:2]
    gate = -jnp.exp(a_log) * _softplus(mt + dt_b)
    mt = jnp.where(r < IDX_HEADS, mt * IDX_HEADS ** -0.5, jnp.where(r < IDX_HEADS + GDN_HEADS, _sigmoid(mt), gate))
    rt_ref[T_MISC:T_MISC + 16, :] = mt
    return nat, mt


def _misc_natural(m, prow_ref):
    lane = lax.broadcasted_iota(I32, m.shape, 1)
    gate = -jnp.exp(prow_ref[0:1, :]) * _softplus(m + prow_ref[1:2, :])
    return jnp.where(lane < GDN_HEADS, _sigmoid(m), jnp.where(lane < 3 * GDN_HEADS, gate, 0.0))


def _qkv_post(conv, qkvn_ref):
    c = _silu(conv)
    for hb in range(2 * GDN_HEADS):
        xh = c[:, hb * HEAD_DIM:(hb + 1) * HEAD_DIM]
        n = xh * lax.rsqrt(jnp.sum(xh * xh, axis=-1, keepdims=True) + EPS)
        if hb < GDN_HEADS:
            n = n * HEAD_DIM ** -0.5
        qkvn_ref[:, hb * HEAD_DIM:(hb + 1) * HEAD_DIM] = n
    qkvn_ref[:, 2 * GDN_W:] = c[:, 2 * GDN_W:]


def _inproj_prompt_kernel(tm, parts, x_ref, sc_ref, sh_ref, g1_ref, wn_ref, wt_ref, wconv_ref, cs128_ref, cs64_ref,
                          prow_ref, pcol_ref, tril_ref,
                          qkvn_ref, z_ref, vnat_ref, knat_ref, ikT_ref, tail_ref, misc_ref, gct_ref,
                          qT_ref, iqT_ref, iwT_ref, kbf_ref, vT_ref, ikbf_ref,
                          xs_ref, rt_ref):
    i = pl.program_id(1)

    @pl.when(i == 0)
    def _():
        xs_ref[0:8, :] = jnp.zeros((8, QKV_W), F32)

    @pl.when(i > 0)
    def _():
        xs_ref[0:8, :] = xs_ref[tm:tm + 8, :]

    th = tm // parts
    for n in range(parts):
        rows, cols = pl.ds(n * th, th), pl.ds(n * th, th)
        _inproj_prompt_rows(
            th, x_ref.at[rows, :], sc_ref, sh_ref, g1_ref, wn_ref, wt_ref, wconv_ref,
            cs128_ref.at[:, cols], cs64_ref.at[:, cols], prow_ref, pcol_ref, tril_ref.at[0:th, 0:th],
            qkvn_ref.at[rows, :], z_ref.at[rows, :], vnat_ref.at[pl.ds(n * th * N_KV, th * N_KV), :],
            knat_ref.at[pl.ds(n * th * N_KV, th * N_KV), :], ikT_ref.at[:, cols], tail_ref, misc_ref.at[rows, :],
            gct_ref.at[pl.ds(n * (th // GDN_CHUNK), th // GDN_CHUNK)],
            qT_ref.at[:, cols], iqT_ref.at[:, cols], iwT_ref.at[:, cols], kbf_ref.at[rows, :],
            vT_ref.at[pl.ds(n * (th // ATT_TK), th // ATT_TK)], ikbf_ref.at[rows, :],
            xs_ref.at[pl.ds(n * th, th + 8), :], rt_ref.at[:, cols])


def _inproj_prompt_rows(tm, x_ref, sc_ref, sh_ref, g1_ref, wn_ref, wt_ref, wconv_ref, cs128_ref, cs64_ref,
                        prow_ref, pcol_ref, tril_ref,
                        qkvn_ref, z_ref, vnat_ref, knat_ref, ikT_ref, tail_ref, misc_ref, gct_ref,
                        qT_ref, iqT_ref, iwT_ref, kbf_ref, vT_ref, ikbf_ref,
                        xs_ref, rt_ref):
    nat, mt = _project(x_ref, sc_ref, sh_ref, g1_ref, wn_ref, wt_ref, cs128_ref, cs64_ref, pcol_ref, rt_ref)
    xs_ref[8:tm + 8, :] = nat[:, 0:QKV_W]
    conv = wconv_ref[0:1, :] * xs_ref[5:tm + 5, :]
    for t in range(1, CONV_W):
        conv = conv + wconv_ref[t:t + 1, :] * xs_ref[5 + t:tm + 5 + t, :]
    tail_ref[...] = xs_ref[tm:tm + 8, :]
    _qkv_post(conv, qkvn_ref)
    z_ref[...] = nat[:, QKV_W:QKV_W + GDN_W]
    v0 = QKV_W + GDN_W
    for hd in range(N_KV):
        vnat_ref[pl.ds(hd, tm, stride=N_KV), :] = nat[:, v0 + hd * HEAD_DIM:v0 + (hd + 1) * HEAD_DIM]

    gm = _misc_natural(nat[:, NAT_W - LANES:], prow_ref)
    tril = tril_ref[...]
    hi, mid, lo = _split3(gm)
    gc = _dot(tril, hi) + (_dot(tril, mid) + _dot(tril, lo))
    lane = lax.broadcasted_iota(I32, gm.shape, 1)
    misc_ref[...] = jnp.where(lane < 2 * GDN_HEADS, gm, gc)
    hi, mid, lo = _split3(mt)
    gct = _dot_nt(hi, tril) + (_dot_nt(mid, tril) + _dot_nt(lo, tril))
    r = lax.broadcasted_iota(I32, mt.shape, 0)
    bg = jnp.where(r < IDX_HEADS + GDN_HEADS, mt, gct)[8:16, :]
    for cc in range(tm // GDN_CHUNK):
        gct_ref[cc] = bg[:, cc * GDN_CHUNK:(cc + 1) * GDN_CHUNK]

    iwT_ref[...] = mt[0:IDX_HEADS, :]
    qT_ref[...] = rt_ref[T_AQ:T_AQ + ATT_W, :].astype(BF)
    iqT_ref[...] = rt_ref[T_IQ:T_IQ + IDX_W, :].astype(BF)
    kn = rt_ref[T_AK:T_AK + KV_W, :].T
    for hd in range(N_KV):
        knat_ref[pl.ds(hd, tm, stride=N_KV), :] = kn[:, hd * HEAD_DIM:(hd + 1) * HEAD_DIM]
    kbf_ref[...] = kn.astype(BF)
    ikT_ref[...] = rt_ref[T_IK:T_IK + IDX_DIM, :]
    ikbf_ref[...] = rt_ref[T_IK:T_IK + LANES, :].T[:, 0:IDX_DIM].astype(BF)
    for cc in range(tm // ATT_TK):
        for hd in range(N_KV):
            lo = hd * VT_ROWS
            vT_ref[cc, lo:lo + HEAD_DIM, :] = rt_ref[T_AV + hd * HEAD_DIM:T_AV + (hd + 1) * HEAD_DIM,
                                                     cc * ATT_TK:(cc + 1) * ATT_TK].astype(BF)
            vT_ref[cc, lo + HEAD_DIM:lo + VT_ROWS, :] = jnp.ones((VT_ROWS - HEAD_DIM, ATT_TK), BF)


def _inproj_sample_kernel(x_ref, sc_ref, sh_ref, g1_ref, wn_ref, wt_ref, wconv_ref, cs128_ref, cs64_ref,
                          prow_ref, pcol_ref, s0_ref, s1_ref, s2_ref,
                          qkvn_ref, z_ref, vnat_ref, knat_ref, slab_ref, raw_ref, misc_ref, aq_ref, iq_ref,
                          rt_ref):
    nat, _ = _project(x_ref, sc_ref, sh_ref, g1_ref, wn_ref, wt_ref, cs128_ref, cs64_ref, pcol_ref, rt_ref)
    raw = nat[:, 0:QKV_W]
    raw_ref[...] = raw
    conv = wconv_ref[0:1, :] * s0_ref[...]
    conv = conv + wconv_ref[1:2, :] * s1_ref[...]
    conv = conv + wconv_ref[2:3, :] * s2_ref[...]
    conv = conv + wconv_ref[3:4, :] * raw
    _qkv_post(conv, qkvn_ref)
    z_ref[...] = nat[:, QKV_W:QKV_W + GDN_W]
    vnat_ref[...] = nat[:, QKV_W + GDN_W:QKV_W + GDN_W + KV_W]
    misc_ref[...] = _misc_natural(nat[:, NAT_W - LANES:], prow_ref)
    aq_ref[...] = rt_ref[T_AQ:T_AQ + ATT_W, :].T
    iq_ref[...] = rt_ref[T_IQ:T_IQ + IDX_W, :].T
    knat_ref[...] = rt_ref[T_AK:T_AK + KV_W, :].T
    slab_ref[...] = rt_ref[T_IK:T_IK + LANES, :].T


def _inproj_weights(w_in, a_log, dt_bias):
    offs = np.cumsum([0, QKV_W, GDN_W, GDN_HEADS, GDN_HEADS, ATT_W, KV_W, KV_W, IDX_W, IDX_DIM, IDX_HEADS])
    qkv, z, beta, a, aq, ak, av, iq, ik, iw = [w_in[:, offs[n]:offs[n + 1]] for n in range(10)]
    d = w_in.shape[0]
    pad = jnp.zeros((d, LANES - 3 * GDN_HEADS), w_in.dtype)
    w_nat = jnp.concatenate([qkv, z, av, beta, a, a, pad], axis=1).astype(BF)
    w_t = jnp.concatenate([aq, ak, iq, ik, iw, beta, a, av], axis=1).T.astype(BF)
    zrow = jnp.zeros((LANES - 3 * GDN_HEADS,), F32)
    prow = jnp.stack([jnp.concatenate([jnp.zeros((GDN_HEADS,), F32), a_log, a_log, zrow]),
                      jnp.concatenate([jnp.zeros((GDN_HEADS,), F32), dt_bias, dt_bias, zrow])])
    z12 = jnp.zeros((IDX_HEADS + GDN_HEADS,), F32)
    pcol = jnp.stack([jnp.concatenate([z12, a_log]), jnp.concatenate([z12, dt_bias])], axis=1)
    return w_nat, w_t, prow, pcol


def _rope_tables(pos, dim):
    half = dim // 8
    inv = ROPE_THETA ** (-jnp.arange(half, dtype=F32) / half)
    ang = pos.astype(F32)[:, None] * inv[None, :]
    return jnp.concatenate([jnp.cos(ang).T, jnp.sin(ang).T], axis=0)


def _inproj_prompt(x, mod, g1, w_nat, w_t, w_conv, prow, pcol, tm):
    bp, tp, d = x.shape
    nt = tp // tm
    rows = bp * tp
    pos = jnp.arange(tp, dtype=I32)
    cs128, cs64 = _rope_tables(pos, HEAD_DIM), _rope_tables(pos, IDX_DIM)
    ri = np.arange(tm)
    tril = jnp.asarray((ri[:, None] // GDN_CHUNK == ri[None, :] // GDN_CHUNK) & (ri[None, :] <= ri[:, None]), BF)
    const = lambda shape: pl.BlockSpec(shape, lambda b, i: (0,) * len(shape))
    rowblk = lambda w: pl.BlockSpec((tm, w), lambda b, i: (b * nt + i, 0))
    colblk = lambda h: pl.BlockSpec((h, tm), lambda b, i: (0, b * nt + i))
    in_specs = [
        pl.BlockSpec((None, tm, d), lambda b, i: (b, i, 0)),
        pl.BlockSpec((None, 1, d), lambda b, i: (b, 0, 1)),
        pl.BlockSpec((None, 1, d), lambda b, i: (b, 0, 0)),
        const((1, d)), const((d, NAT_W)), const((T_ROWS, d)), const((CONV_W, QKV_W)),
        pl.BlockSpec((HEAD_DIM // 4, tm), lambda b, i: (0, i)),
        pl.BlockSpec((IDX_DIM // 4, tm), lambda b, i: (0, i)),
        const((2, LANES)), const((16, 2)), const((tm, tm)),
    ]
    out_shape = [
        jax.ShapeDtypeStruct((rows, QKV_W), F32), jax.ShapeDtypeStruct((rows, GDN_W), F32),
        jax.ShapeDtypeStruct((rows * N_KV, HEAD_DIM), F32), jax.ShapeDtypeStruct((rows * N_KV, HEAD_DIM), F32),
        jax.ShapeDtypeStruct((bp, IDX_DIM, tp), F32), jax.ShapeDtypeStruct((bp, 8, QKV_W), F32),
        jax.ShapeDtypeStruct((rows, LANES), F32), jax.ShapeDtypeStruct((rows // GDN_CHUNK, 8, GDN_CHUNK), F32),
        jax.ShapeDtypeStruct((ATT_W, rows), BF), jax.ShapeDtypeStruct((IDX_W, rows), BF),
        jax.ShapeDtypeStruct((IDX_HEADS, rows), F32), jax.ShapeDtypeStruct((rows, KV_W), BF),
        jax.ShapeDtypeStruct((rows // ATT_TK, N_KV * VT_ROWS, ATT_TK), BF), jax.ShapeDtypeStruct((rows, IDX_DIM), BF),
    ]
    out_specs = [
        rowblk(QKV_W), rowblk(GDN_W),
        pl.BlockSpec((tm * N_KV, HEAD_DIM), lambda b, i: (b * nt + i, 0)),
        pl.BlockSpec((tm * N_KV, HEAD_DIM), lambda b, i: (b * nt + i, 0)),
        pl.BlockSpec((None, IDX_DIM, tm), lambda b, i: (b, 0, i)),
        pl.BlockSpec((None, 8, QKV_W), lambda b, i: (b, 0, 0)),
        rowblk(LANES),
        pl.BlockSpec((tm // GDN_CHUNK, 8, GDN_CHUNK), lambda b, i: (b * nt + i, 0, 0)),
        colblk(ATT_W), colblk(IDX_W), colblk(IDX_HEADS), rowblk(KV_W),
        pl.BlockSpec((tm // ATT_TK, N_KV * VT_ROWS, ATT_TK), lambda b, i: (b * nt + i, 0, 0)),
        rowblk(IDX_DIM),
    ]
    return pl.pallas_call(
        functools.partial(_inproj_prompt_kernel, tm, tm // _pick(tm, (ATT_TK,))),
        grid=(bp, nt), in_specs=in_specs, out_specs=out_specs, out_shape=out_shape,
        scratch_shapes=[pltpu.VMEM((tm + 8, QKV_W), F32), pltpu.VMEM((T_ROWS, tm), F32)],
        compiler_params=_params(("arbitrary", "arbitrary")),
        name="inproj_prompt",
    )(x, mod, mod, g1.reshape(1, d), w_nat, w_t, w_conv, cs128, cs64, prow, pcol, tril)


def _inproj_sample(x, mod, g1, w_nat, w_t, w_conv, prow, pcol, conv_state, past):
    db, d = x.shape
    pos = jnp.full((db,), past, I32)
    cs128, cs64 = _rope_tables(pos, HEAD_DIM), _rope_tables(pos, IDX_DIM)
    full = lambda shape: pl.BlockSpec(shape, lambda i: (0,) * len(shape))
    in_specs = [
        full((db, d)),
        pl.BlockSpec((db, d), lambda i: (0, 1)), pl.BlockSpec((db, d), lambda i: (0, 0)),
        full((1, d)), full((d, NAT_W)), full((T_ROWS, d)), full((CONV_W, QKV_W)),
        full((HEAD_DIM // 4, db)), full((IDX_DIM // 4, db)), full((2, LANES)), full((16, 2)),
        full((db, QKV_W)), full((db, QKV_W)), full((db, QKV_W)),
    ]
    widths = [QKV_W, GDN_W, KV_W, KV_W, LANES, QKV_W, LANES, ATT_W, IDX_W]
    return pl.pallas_call(
        _inproj_sample_kernel,
        grid=(1,), in_specs=in_specs,
        out_specs=[full((db, w)) for w in widths],
        out_shape=[jax.ShapeDtypeStruct((db, w), F32) for w in widths],
        scratch_shapes=[pltpu.VMEM((T_ROWS, db), F32)],
        compiler_params=_params(("arbitrary",)),
        name="inproj_sample",
    )(x, mod, mod, g1.reshape(1, d), w_nat, w_t, w_conv, cs128, cs64, prow, pcol,
      conv_state[:, 0], conv_state[:, 1], conv_state[:, 2])


def _unit_lower_inverses(lows):
    c = lows[0].shape[0]
    ii = lax.broadcasted_iota(I32, (c, c), 0)
    jj = lax.broadcasted_iota(I32, (c, c), 1)
    eye = jnp.where(ii == jj, 1.0, 0.0)
    levels = int(math.log2(c)) - 1
    ts = [eye - low for low in lows]
    ps = [_mm_hi(low, low) for low in lows]
    for lvl in range(levels):
        if lvl == levels - 1:
            ts = [t + _mm_hi(t, p) for t, p in zip(ts, ps)]
        else:
            both = [_mm_hi(jnp.concatenate([t, p], axis=0), p) for t, p in zip(ts, ps)]
            ts = [t + b[0:c] for t, b in zip(ts, both)]
            ps = [b[c:] for b in both]
    return ts


def _gdn_prompt_kernel(tg, qkvn_ref, z_ref, misc_ref, gct_ref, gn_ref, o_ref, ssm_ref, s_ref):
    i = pl.program_id(1)

    @pl.when(i == 0)
    def _():
        s_ref[...] = jnp.zeros(s_ref.shape, F32)

    c = GDN_CHUNK
    ii = lax.broadcasted_iota(I32, (c, c), 0)
    jj = lax.broadcasted_iota(I32, (c, c), 1)
    pairs = [(cc, hd) for cc in range(tg // c) for hd in range(GDN_HEADS)]

    qs, ks, gcs, rhs, lows, intras = {}, {}, {}, {}, [], {}
    for cc, hd in pairs:
        r0, lo = cc * c, hd * HEAD_DIM
        q = qkvn_ref[r0:r0 + c, lo:lo + HEAD_DIM]
        k = qkvn_ref[r0:r0 + c, GDN_W + lo:GDN_W + lo + HEAD_DIM]
        v = qkvn_ref[r0:r0 + c, 2 * GDN_W + lo:2 * GDN_W + lo + HEAD_DIM]
        beta_c = misc_ref[r0:r0 + c, hd:hd + 1]
        gc_c = misc_ref[r0:r0 + c, 2 * GDN_HEADS + hd:2 * GDN_HEADS + hd + 1]
        gc_r = gct_ref[cc, GDN_HEADS + hd:GDN_HEADS + hd + 1, :]
        decay = jnp.where(ii >= jj, jnp.exp(jnp.where(ii >= jj, gc_c - gc_r, 0.0)), 0.0)
        kb = k * beta_c
        both = _dot_nt(jnp.concatenate([kb, q], axis=0).astype(BF), k.astype(BF))
        lows.append(jnp.where(ii > jj, both[0:c] * decay, 0.0))
        intras[cc, hd] = (both[c:] * decay).astype(BF)
        rhs[cc, hd] = jnp.concatenate([v * beta_c, kb * jnp.exp(gc_c)], axis=1).astype(BF)
        qs[cc, hd], ks[cc, hd], gcs[cc, hd] = q, k, gc_c
    ts = _unit_lower_inverses(lows)
    uws = {p: _dot(t.astype(BF), rhs[p]) for p, t in zip(pairs, ts)}

    for cc, hd in pairs:
        r0, lo = cc * c, hd * HEAD_DIM
        q, k, gc_c, uw = qs[cc, hd], ks[cc, hd], gcs[cc, hd], uws[cc, hd]
        s = s_ref[hd]
        ws = _dot(jnp.concatenate([uw[:, HEAD_DIM:], q * jnp.exp(gc_c)], axis=0).astype(BF), s.astype(BF))
        v_new = (uw[:, 0:HEAD_DIM] - ws[0:c]).astype(BF)
        o = ws[c:] + _dot(intras[cc, hd], v_new)
        g_last = gc_c[c - 1:c, :]
        kd = k * jnp.exp(g_last - gc_c)
        s_ref[hd] = s * jnp.exp(g_last) + _dot(kd.T.astype(BF), v_new)
        o = _rmsnorm(o, gn_ref[...]) * _silu(z_ref[r0:r0 + c, lo:lo + HEAD_DIM])
        o_ref[r0:r0 + c, lo:lo + HEAD_DIM] = o.astype(BF)
    ssm_ref[...] = s_ref[...]


def _gdn_prompt(qkvn, z, misc, gct, g_norm, bp, tp, tg):
    nt = tp // tg
    rows = bp * tp
    rowblk = lambda w: pl.BlockSpec((tg, w), lambda b, i: (b * nt + i, 0))
    return pl.pallas_call(
        functools.partial(_gdn_prompt_kernel, tg),
        grid=(bp, nt),
        in_specs=[rowblk(QKV_W), rowblk(GDN_W), rowblk(LANES),
                  pl.BlockSpec((tg // GDN_CHUNK, 8, GDN_CHUNK), lambda b, i: (b * nt + i, 0, 0)),
                  pl.BlockSpec((1, HEAD_DIM), lambda b, i: (0, 0))],
        out_specs=[rowblk(GDN_W),
                   pl.BlockSpec((None, GDN_HEADS, HEAD_DIM, HEAD_DIM), lambda b, i: (b, 0, 0, 0))],
        out_shape=[jax.ShapeDtypeStruct((rows, GDN_W), BF),
                   jax.ShapeDtypeStruct((bp, GDN_HEADS, HEAD_DIM, HEAD_DIM), F32)],
        scratch_shapes=[pltpu.VMEM((GDN_HEADS, HEAD_DIM, HEAD_DIM), F32)],
        compiler_params=_params(("arbitrary", "arbitrary")),
        name="gdn_prompt",
    )(qkvn, z, misc, gct, g_norm.reshape(1, HEAD_DIM))


def _gdn_sample_kernel(nb, qkvn_ref, z_ref, misc_ref, gn_ref, s_ref, o_ref, ssm_ref):
    for bi in range(nb):
        for hd in range(GDN_HEADS):
            lo = hd * HEAD_DIM
            q = qkvn_ref[bi:bi + 1, lo:lo + HEAD_DIM]
            k = qkvn_ref[bi:bi + 1, GDN_W + lo:GDN_W + lo + HEAD_DIM]
            v = qkvn_ref[bi:bi + 1, 2 * GDN_W + lo:2 * GDN_W + lo + HEAD_DIM]
            beta = misc_ref[bi:bi + 1, hd:hd + 1]
            g = misc_ref[bi:bi + 1, GDN_HEADS + hd:GDN_HEADS + hd + 1]
            kcol = jnp.broadcast_to(k, (HEAD_DIM, HEAD_DIM)).T
            qcol = jnp.broadcast_to(q, (HEAD_DIM, HEAD_DIM)).T
            s = s_ref[bi, hd] * jnp.exp(g)
            ks = jnp.sum(kcol * s, axis=0, keepdims=True)
            delta = (v - ks) * beta
            s = s + kcol * delta
            ssm_ref[bi, hd] = s
            o = jnp.sum(qcol * s, axis=0, keepdims=True)
            o = _rmsnorm(o, gn_ref[...]) * _silu(z_ref[bi:bi + 1, lo:lo + HEAD_DIM])
            o_ref[bi:bi + 1, lo:lo + HEAD_DIM] = o.astype(BF)


def _gdn_sample(qkvn, z, misc, g_norm, state, nb):
    db = qkvn.shape[0]
    rowblk = lambda w: pl.BlockSpec((nb, w), lambda i: (i, 0))
    sblk = pl.BlockSpec((nb, GDN_HEADS, HEAD_DIM, HEAD_DIM), lambda i: (i, 0, 0, 0))
    return pl.pallas_call(
        functools.partial(_gdn_sample_kernel, nb),
        grid=(db // nb,),
        in_specs=[rowblk(QKV_W), rowblk(GDN_W), rowblk(LANES), pl.BlockSpec((1, HEAD_DIM), lambda i: (0, 0)), sblk],
        out_specs=[rowblk(GDN_W), sblk],
        out_shape=[jax.ShapeDtypeStruct((db, GDN_W), BF), jax.ShapeDtypeStruct(state.shape, F32)],
        compiler_params=_params(("arbitrary",)),
        name="gdn_sample",
    )(qkvn, z, misc, g_norm.reshape(1, HEAD_DIM), state)


def _ordered_word_to_float(u):
    s = u ^ INT_MIN
    return lax.bitcast_convert_type(s ^ ((s >> 31) & np.int32(0x7FFFFFFF)), F32)


def _select_topk_bias(i_ref, t_ref, nch, tk, k_sel, idx_bits, taken=0.0, dropped=NEG_INF):
    nl = i_ref.shape[1]
    row = lax.broadcasted_iota(I32, (tk, 1), 0)

    def chunk(c):
        return pl.ds(pl.multiple_of(c * tk, tk), tk)

    def count(ref, pred):
        def body(c, acc):
            m = pred(ref[chunk(c), :])
            return acc + jnp.sum(jnp.where(m, 1, 0).astype(I32).reshape(tk // 8, 8, nl), axis=0)
        acc = lax.fori_loop(0, nch, body, jnp.zeros((8, nl), I32))
        return jnp.sum(acc, axis=0, keepdims=True)

    def bit_body(i, carry):
        t_u, n_ge = carry
        cand_u = t_u | jnp.left_shift(jnp.int32(1), 31 - i)
        cand = _ordered_word_to_float(cand_u)
        cnt = count(i_ref, lambda sc: sc >= cand)
        take = cnt >= k_sel
        return jnp.where(take, cand_u, t_u), jnp.where(take, cnt, n_ge)

    t_u, n_ge = lax.fori_loop(0, 32, bit_body, (jnp.zeros((1, nl), I32), jnp.full((1, nl), -1, I32)))
    thr = jnp.where((t_u >> 23) == 0, NEG_INF, _ordered_word_to_float(t_u))

    def tie_break(_):
        need = k_sel - count(i_ref, lambda sc: sc > thr)

        def mark(c, carry):
            t_ref[chunk(c), :] = jnp.where(i_ref[chunk(c), :] == thr, c * tk + row, 2 ** idx_bits)
            return carry

        lax.fori_loop(0, nch, mark, 0)

        def idx_body(i, x):
            cand = x | jnp.left_shift(jnp.int32(1), idx_bits - 1 - i)
            cnt = count(t_ref, lambda idx: idx < cand)
            return jnp.where(cnt < need, cand, x)

        return lax.fori_loop(0, idx_bits, idx_body, jnp.zeros((1, nl), I32))

    x = lax.cond(jnp.max(n_ge) > k_sel, tie_break, lambda _: jnp.full((1, nl), 2 ** idx_bits - 1, I32), 0)

    def write(c, carry):
        sc = i_ref[chunk(c), :]
        sel = (sc > thr) | ((sc == thr) & (c * tk + row <= x))
        i_ref[chunk(c), :] = jnp.where(sel & (sc > NEG_INF), taken, dropped)
        return carry

    lax.fori_loop(0, nch, write, 0)


def _attn_prompt_kernel(tq, tk, blk, k_sel, idx_bits, qT_ref, iqT_ref, iwT_ref, kbf_ref, vT_ref, ikbf_ref, o_ref,
                        i_ref, t_ref):
    t0 = pl.program_id(1) * tq
    nch = (t0 + tq + tk - 1) // tk
    nblk = (t0 + tq + blk - 1) // blk
    qpos = t0 + lax.broadcasted_iota(I32, (1, tq), 1)
    row = lax.broadcasted_iota(I32, (tk, 1), 0)
    w = iwT_ref[...]
    pairs = IDX_HEADS // 2
    rhs = [jnp.concatenate([iqT_ref[(2 * p) * IDX_DIM:(2 * p + 1) * IDX_DIM, :],
                            iqT_ref[(2 * p + 1) * IDX_DIM:(2 * p + 2) * IDX_DIM, :]], axis=1) for p in range(pairs)]

    def chunk(c):
        return pl.ds(pl.multiple_of(c * tk, tk), tk)

    def idx_body(c, carry):
        ikc = ikbf_ref[chunk(c), :]
        acc = jnp.zeros((tk, tq), F32)
        for p in range(pairs):
            d = jnp.maximum(_dot(ikc, rhs[p]), 0.0)
            acc = acc + d[:, 0:tq] * w[2 * p:2 * p + 1, :] + d[:, tq:] * w[2 * p + 1:2 * p + 2, :]
        i_ref[chunk(c), :] = jnp.where(c * tk + row <= qpos, acc, NEG_INF)
        return carry

    lax.fori_loop(0, nch, idx_body, 0)

    def fill_body(c, carry):
        i_ref[chunk(c), :] = jnp.full((tk, tq), NEG_INF, F32)
        return carry

    lax.fori_loop(nch, nblk * (blk // tk), fill_body, 0)
    tks = max(tk * LANES // tq, 8)
    _select_topk_bias(i_ref, t_ref, (t0 + tq + tks - 1) // tks, tks, k_sel, idx_bits)

    scale2 = HEAD_DIM ** -0.5 * math.log2(math.e)
    rep = ATT_HEADS // N_KV
    sub = ATT_TK
    nsub = blk // sub
    qg = [jnp.concatenate([qT_ref[(rep * g + r) * HEAD_DIM:(rep * g + r + 1) * HEAD_DIM, :] for r in range(rep)],
                          axis=1) for g in range(N_KV)]

    def body(c, carry):
        out = []
        for g in range(N_KV):
            m, acc = carry[g]
            ss = []
            for i in range(nsub):
                rows = pl.ds(pl.multiple_of(c * blk + i * sub, sub), sub)
                bias = i_ref[rows, :]
                ss.append(_dot(kbf_ref[rows, g * HEAD_DIM:(g + 1) * HEAD_DIM], qg[g]) * scale2
                          + jnp.concatenate([bias] * rep, axis=1))
            m_new = m
            for s in ss:
                m_new = jnp.maximum(m_new, jnp.max(s, axis=0, keepdims=True))
            m_safe = jnp.where(m_new == NEG_INF, 0.0, m_new)
            acc = acc * jnp.exp2(m - m_safe)
            for i, s in enumerate(ss):
                p = jnp.exp2(s - m_safe).astype(BF)
                acc = acc + _dot(vT_ref[c * nsub + i, g * VT_ROWS:(g + 1) * VT_ROWS, :], p)
            out.append((m_new, acc))
        return tuple(out)

    init = tuple((jnp.full((1, rep * tq), NEG_INF, F32), jnp.zeros((VT_ROWS, rep * tq), F32)) for _ in range(N_KV))
    res = lax.fori_loop(0, nblk, body, init)
    for g in range(N_KV):
        acc = res[g][1]
        o_t = acc[0:HEAD_DIM] / acc[HEAD_DIM:HEAD_DIM + 1]
        for r in range(rep):
            hd = rep * g + r
            o_ref[:, hd * HEAD_DIM:(hd + 1) * HEAD_DIM] = o_t[:, r * tq:(r + 1) * tq].T.astype(BF)


def _attn_prompt(qT, iqT, iwT, kbf, vT, ikbf, bp, tp):
    tq = _pick(tp, (ATT_TQ, QBLOCK))
    tk = _pick(tp, (IDX_TK, ATT_TK))
    blk = _pick(tp, (ATT_BLK, IDX_TK, ATT_TK))
    assert blk % tk == 0 and blk % ATT_TK == 0
    nq = tp // tq
    rows = bp * tp
    k_sel = min(TOPK_MAX, tp // 4)
    idx_bits = max(1, int(tp - 1).bit_length())
    colblk = lambda h: pl.BlockSpec((h, tq), lambda b, j: (0, b * nq + j))
    return pl.pallas_call(
        functools.partial(_attn_prompt_kernel, tq, tk, blk, k_sel, idx_bits),
        grid=(bp, nq),
        in_specs=[colblk(ATT_W), colblk(IDX_W), colblk(IDX_HEADS),
                  pl.BlockSpec((tp, KV_W), lambda b, j: (b, 0)),
                  pl.BlockSpec((tp // ATT_TK, N_KV * VT_ROWS, ATT_TK), lambda b, j: (b, 0, 0)),
                  pl.BlockSpec((tp, IDX_DIM), lambda b, j: (b, 0))],
        out_specs=pl.BlockSpec((tq, ATT_W), lambda b, j: (b * nq + j, 0)),
        out_shape=jax.ShapeDtypeStruct((rows, ATT_W), BF),
        scratch_shapes=[pltpu.VMEM((tp, tq), F32), pltpu.VMEM((tp, tq), I32)],
        compiler_params=_params(("arbitrary", "arbitrary")),
        name="attn_prompt",
    )(qT, iqT, iwT, kbf, vT, ikbf)


def _sidx_kernel(npg, pt_ref, iq_ref, iw_ref, ikn_ref, *refs):
    pages, out_ref = refs[:npg], refs[npg]
    iq = iq_ref[...].astype(BF)
    w = iw_ref[...]
    for p in range(npg):
        d = jnp.maximum(_dot(iq, pages[p][...].astype(BF)), 0.0)
        out_ref[:, p * LANES:(p + 1) * LANES] = jnp.sum(d * w, axis=0, keepdims=True)
    dn = jnp.sum(iq.astype(F32) * ikn_ref[...].astype(BF).astype(F32), axis=1, keepdims=True)
    sn = jnp.sum(jnp.maximum(dn, 0.0) * w, axis=0, keepdims=True)
    lane = lax.broadcasted_iota(I32, (1, LANES), 1)
    out_ref[:, npg * LANES:(npg + 1) * LANES] = jnp.where(lane == 0, sn, NEG_INF)


def _sample_index_scores(iq, iw, ik_new, cache_ik_t, page_table):
    db, npg = page_table.shape
    page = cache_ik_t.shape[2]
    assert page == LANES
    width = (npg + 1) * LANES
    page_specs = [pl.BlockSpec((None, IDX_DIM, page), functools.partial(lambda p, b, pt: (pt[b, p], 0, 0), p))
                  for p in range(npg)]
    grid_spec = pltpu.PrefetchScalarGridSpec(
        num_scalar_prefetch=1, grid=(db,),
        in_specs=[pl.BlockSpec((None, IDX_HEADS, IDX_DIM), lambda b, pt: (b, 0, 0)),
                  pl.BlockSpec((None, IDX_HEADS, 1), lambda b, pt: (b, 0, 0)),
                  pl.BlockSpec((None, 1, IDX_DIM), lambda b, pt: (b, 0, 0))] + page_specs,
        out_specs=pl.BlockSpec((None, 1, width), lambda b, pt: (b, 0, 0)))
    out = pl.pallas_call(
        functools.partial(_sidx_kernel, npg), grid_spec=grid_spec,
        out_shape=jax.ShapeDtypeStruct((db, 1, width), F32),
        compiler_params=_params(("arbitrary",)),
        name="sample_index_scores",
    )(page_table, iq.reshape(db, IDX_HEADS, IDX_DIM), iw.reshape(db, IDX_HEADS, 1), ik_new.reshape(db, 1, IDX_DIM),
      *([cache_ik_t] * npg))
    return out.reshape(db, width)


def _ssel_kernel(nch, k_sel, idx_bits, s_ref, spread_ref, keep_ref, i_ref, t_ref):
    for p in range(nch):
        i_ref[p * LANES:(p + 1) * LANES, :] = s_ref[:, p * LANES:(p + 1) * LANES].T
    _select_topk_bias(i_ref, t_ref, nch, LANES, k_sel, idx_bits, taken=1.0, dropped=0.0)
    rows = N_KV * LANES
    for p in range(nch - 1):
        k2 = _dot(spread_ref[...], i_ref[p * LANES:(p + 1) * LANES, :].astype(BF))
        for h in range(N_KV):
            keep_ref[:, p * rows + h * LANES:p * rows + (h + 1) * LANES] = k2[h * LANES:(h + 1) * LANES, :].T
    keep_ref[:, (nch - 1) * rows:(nch - 1) * rows + LANES] = i_ref[(nch - 1) * LANES:nch * LANES, :].T


def _sample_select(scores, n_keys):
    db, width = scores.shape
    assert db == LANES
    nch = width // LANES
    k_sel = min(TOPK_MAX, n_keys // 4)
    idx_bits = max(1, int(width - 1).bit_length())
    rows = N_KV * LANES
    out_w = (nch - 1) * rows + LANES
    spread = jnp.asarray(np.arange(rows)[:, None] // N_KV == np.arange(LANES)[None, :], BF)
    return pl.pallas_call(
        functools.partial(_ssel_kernel, nch, k_sel, idx_bits),
        grid=(1,),
        in_specs=[pl.BlockSpec((db, width), lambda i: (0, 0)), pl.BlockSpec((rows, LANES), lambda i: (0, 0))],
        out_specs=pl.BlockSpec((db, out_w), lambda i: (0, 0)),
        out_shape=jax.ShapeDtypeStruct((db, out_w), F32),
        scratch_shapes=[pltpu.VMEM((width, db), F32), pltpu.VMEM((width, db), I32)],
        compiler_params=_params(("arbitrary",)),
        name="sample_select",
    )(scores, spread)


def _sattn_kernel(npg, pt_ref, q_ref, keep_ref, kn_ref, vn_ref, *refs):
    kpages, vpages, o_ref = refs[:npg], refs[npg:2 * npg], refs[2 * npg]
    rep = ATT_HEADS // N_KV
    rows = N_KV * LANES
    scale = HEAD_DIM ** -0.5
    q = q_ref[...]
    q8 = jnp.concatenate([q, jnp.zeros((8 - ATT_HEADS, HEAD_DIM), F32)], axis=0).astype(BF)
    hrow = lax.broadcasted_iota(I32, (8, 1), 0)
    lane = lax.broadcasted_iota(I32, (1, LANES), 1)
    own_kv = lax.broadcasted_iota(I32, (1, rows), 1) % N_KV == hrow // rep
    parts = []
    for p in range(npg):
        s2 = _dot_nt(q8, kpages[p][...].astype(BF))
        keep2 = keep_ref[:, p * rows:(p + 1) * rows]
        parts.append(jnp.where(own_kv & (keep2 > 0.5), s2 * scale, NEG_INF))
    kn = kn_ref[...].astype(BF).astype(F32)
    kn8 = jnp.where(hrow // rep == 0, kn[0:1, :], kn[1:2, :])
    s_new = jnp.sum(q8.astype(F32) * kn8, axis=1, keepdims=True)
    keep_new = keep_ref[:, npg * rows:npg * rows + 1]
    parts.append(jnp.where((lane == 0) & (keep_new > 0.5), s_new * scale, NEG_INF))
    s = jnp.concatenate(parts, axis=1)
    m = jnp.max(s, axis=1, keepdims=True)
    e = jnp.exp(s - m)
    pr = (e / jnp.sum(e, axis=1, keepdims=True)).astype(BF)
    o8 = jnp.zeros((8, HEAD_DIM), F32)
    for p in range(npg):
        o8 = o8 + _dot(pr[:, p * rows:(p + 1) * rows], vpages[p][...].astype(BF))
    vn = vn_ref[...].astype(BF).astype(F32)
    vn8 = jnp.where(hrow // rep == 0, vn[0:1, :], vn[1:2, :])
    o8 = o8 + pr[:, npg * rows:npg * rows + 1].astype(F32) * vn8
    for hd in range(ATT_HEADS):
        o_ref[:, hd * HEAD_DIM:(hd + 1) * HEAD_DIM] = o8[hd:hd + 1, :].astype(BF)


def _sample_attention(q, keep, k_new, v_new, cache_k, cache_v, page_table):
    db, npg = page_table.shape
    n_phys, page = cache_k.shape[0], cache_k.shape[1]
    assert page == LANES and N_KV == 2
    rows = page * N_KV
    width = npg * rows + LANES
    ck = cache_k.reshape(n_phys, rows, HEAD_DIM)
    cv = cache_v.reshape(n_phys, rows, HEAD_DIM)
    page_specs = [pl.BlockSpec((None, rows, HEAD_DIM), functools.partial(lambda p, b, pt: (pt[b, p], 0, 0), p))
                  for p in range(npg)]
    grid_spec = pltpu.PrefetchScalarGridSpec(
        num_scalar_prefetch=1, grid=(db,),
        in_specs=[pl.BlockSpec((None, ATT_HEADS, HEAD_DIM), lambda b, pt: (b, 0, 0)),
                  pl.BlockSpec((None, 1, width), lambda b, pt: (b, 0, 0)),
                  pl.BlockSpec((None, N_KV, HEAD_DIM), lambda b, pt: (b, 0, 0)),
                  pl.BlockSpec((None, N_KV, HEAD_DIM), lambda b, pt: (b, 0, 0))] + page_specs + page_specs,
        out_specs=pl.BlockSpec((None, 1, ATT_W), lambda b, pt: (b, 0, 0)))
    out = pl.pallas_call(
        functools.partial(_sattn_kernel, npg), grid_spec=grid_spec,
        out_shape=jax.ShapeDtypeStruct((db, 1, ATT_W), BF),
        compiler_params=_params(("arbitrary",)),
        name="sample_attention",
    )(page_table, q.reshape(db, ATT_HEADS, HEAD_DIM), keep.reshape(db, 1, width),
      k_new.reshape(db, N_KV, HEAD_DIM), v_new.reshape(db, N_KV, HEAD_DIM), *([ck] * npg), *([cv] * npg))
    return out.reshape(db, ATT_W)


def _post_kernel(final, nff, og_ref, oa_ref, x_ref, ga1_ref, sc2_ref, sh2_ref, ga2_ref, g2_ref, gf_ref,
                 wo_ref, wg_ref, wu_ref, wd_ref, y_ref, x1_ref, h2_ref, acc_ref):
    jf = pl.program_id(1)

    @pl.when(jf == 0)
    def _():
        mixed = _dot(og_ref[...], wo_ref[0:GDN_W, :]) + _dot(oa_ref[...], wo_ref[GDN_W:, :])
        x1 = x_ref[...] + ga1_ref[...] * mixed
        x1_ref[...] = x1
        h2_ref[...] = (_rmsnorm(x1, g2_ref[...]) * (1.0 + sc2_ref[...]) + sh2_ref[...]).astype(BF)
        acc_ref[...] = jnp.zeros(acc_ref.shape, F32)

    h2 = h2_ref[...]
    act = (_silu(_dot(h2, wg_ref[...])) * _dot(h2, wu_ref[...])).astype(BF)
    acc_ref[...] += _dot(act, wd_ref[...])

    @pl.when(jf == nff - 1)
    def _():
        x2 = x1_ref[...] + ga2_ref[...] * acc_ref[...]
        y_ref[...] = _rmsnorm(x2, gf_ref[...]) if final else x2


def _post(og, oa, x2d, mod, mod_spec, g2, gf, wo, wfi, wfo, tm, tf, final):
    rows, d = x2d.shape
    dff = wfo.shape[0]
    nff = dff // tf
    rowblk = lambda w: pl.BlockSpec((tm, w), lambda i, j: (i, 0))
    const = lambda shape: pl.BlockSpec(shape, lambda i, j: (0,) * len(shape))
    in_specs = [rowblk(GDN_W), rowblk(ATT_W), rowblk(d),
                mod_spec(2), mod_spec(4), mod_spec(3), mod_spec(5),
                const((1, d)), const((1, d)), const((d, d)),
                pl.BlockSpec((d, tf), lambda i, j: (0, j)),
                pl.BlockSpec((d, tf), lambda i, j: (0, nff + j)),
                pl.BlockSpec((tf, d), lambda i, j: (j, 0))]
    return pl.pallas_call(
        functools.partial(_post_kernel, final, nff),
        grid=(rows // tm, nff), in_specs=in_specs, out_specs=rowblk(d),
        out_shape=jax.ShapeDtypeStruct((rows, d), F32),
        scratch_shapes=[pltpu.VMEM((tm, d), F32), pltpu.VMEM((tm, d), BF), pltpu.VMEM((tm, d), F32)],
        compiler_params=_params(("arbitrary", "arbitrary")),
        name="post",
    )(og, oa, x2d, mod, mod, mod, mod, g2.reshape(1, d), gf.reshape(1, d), wo, wfi, wfi, wfo)


def _pick(n, prefs):
    for p in prefs:
        if n % p == 0:
            return p
    return n


def kernel(x_prompt, x_sample, c_prompt, c_sample, cache_k, cache_v, cache_idx_k, page_table, state_conv, state_ssm,
           w_ada, b_ada, g_norm1, w_in, w_conv, a_log, dt_bias, g_gdn_norm, w_out, g_norm2, w_ffn_in, w_ffn_out,
           g_final):
    bp, tp, d = x_prompt.shape
    db, ts, _ = x_sample.shape
    assert ts == 1 and d == GDN_W + ATT_W and tp % max(ATT_TK, GDN_CHUNK) == 0
    depth = w_in.shape[0]
    npg, page = page_table.shape[1], cache_k.shape[2]
    past = npg * page
    tm_in = _pick(tp, (512, 256))
    tg = _pick(tp, (512, 256))
    tm_post = _pick(tp, (1024, 512, 256))
    dff = w_ffn_out.shape[1]
    tf = _pick(dff, (256, 128))

    xp = x_prompt
    xs = x_sample.reshape(db, d)
    npad = (-(bp + db)) % 8
    c_all = jnp.concatenate([c_prompt, c_sample, jnp.zeros((npad, d), F32)], axis=0)
    new_p, new_s = [], []
    for l in range(depth):
        final = l == depth - 1
        mod = _modulation(c_all, w_ada[l], b_ada[l])
        mod_p = mod[:bp].reshape(bp, 1, 6 * d)
        mod_s = mod[bp:bp + db]
        w_nat, w_t, prow, pcol = _inproj_weights(w_in[l], a_log[l], dt_bias[l])
        wo = w_out[l].astype(BF)
        wfi = w_ffn_in[l].astype(BF)
        wfo = w_ffn_out[l].astype(BF)

        (qkvn, z, vnat, knat, ikT, tail, misc, gct, qT, iqT, iwT, kbf, vT, ikbf) = _inproj_prompt(
            xp, mod_p, g_norm1[l], w_nat, w_t, w_conv[l], prow, pcol, tm_in)
        og, ssm_p = _gdn_prompt(qkvn, z, misc, gct, g_gdn_norm[l], bp, tp, tg)
        oa = _attn_prompt(qT, iqT, iwT, kbf, vT, ikbf, bp, tp)
        tiles_b = tp // tm_post
        spec_p = lambda k: pl.BlockSpec((None, 1, d), lambda i, j: (i // tiles_b, 0, k))
        xp = _post(og, oa, xp.reshape(bp * tp, d), mod_p, spec_p, g_norm2[l], g_final, wo, wfi, wfo,
                   tm_post, tf, final).reshape(bp, tp, d)
        new_p.append((knat.reshape(bp, tp, N_KV, HEAD_DIM), vnat.reshape(bp, tp, N_KV, HEAD_DIM),
                      jnp.swapaxes(ikT, 1, 2), tail[:, 8 - (CONV_W - 1):, :], ssm_p))

        (qkvn_s, z_s, vnat_s, knat_s, slab_s, raw_s, misc_s, aq_s, iq_s) = _inproj_sample(
            xs, mod_s, g_norm1[l], w_nat, w_t, w_conv[l], prow, pcol, state_conv[l], past)
        og_s, ssm_s = _gdn_sample(qkvn_s, z_s, misc_s, g_gdn_norm[l], state_ssm[l], _pick(db, (8,)))
        ik_s = slab_s[:, 0:IDX_DIM]
        iw_s = slab_s[:, IDX_DIM:IDX_DIM + IDX_HEADS]
        scores = _sample_index_scores(iq_s, iw_s, ik_s, jnp.swapaxes(cache_idx_k[l], 1, 2), page_table)
        keep = _sample_select(scores, past + ts)
        oa_s = _sample_attention(aq_s, keep, knat_s, vnat_s, cache_k[l], cache_v[l], page_table)
        spec_s = lambda k: pl.BlockSpec((db, d), lambda i, j: (i, k))
        xs = _post(og_s, oa_s, xs, mod_s, spec_s, g_norm2[l], g_final, wo, wfi, wfo, db, tf, final)
        new_conv_s = jnp.concatenate([state_conv[l][:, 1:, :], raw_s[:, None, :]], axis=1)
        new_s.append((knat_s.reshape(db, ts, N_KV, HEAD_DIM), vnat_s.reshape(db, ts, N_KV, HEAD_DIM),
                      ik_s.reshape(db, ts, IDX_DIM), new_conv_s, ssm_s))

    stack = lambda states, n: jnp.stack([s[n] for s in states], axis=0)
    return (xp, xs.reshape(db, ts, d),
            stack(new_p, 0), stack(new_p, 1), stack(new_p, 2), stack(new_p, 3), stack(new_p, 4),
            stack(new_s, 0), stack(new_s, 1), stack(new_s, 2), stack(new_s, 3), stack(new_s, 4))
```

```python
import functools
import math

import numpy as np
import jax
import jax.numpy as jnp
from jax import lax
from jax.experimental import pallas as pl
from jax.experimental.pallas import tpu as pltpu

F32 = jnp.float32
BF = jnp.bfloat16
I32 = jnp.int32

HEAD_DIM = 128
GDN_HEADS = 4
ATT_HEADS = 4
N_KV = 2
IDX_HEADS = 8
IDX_DIM = 64
CONV_W = 4
TOPK_MAX = 256
QBLOCK = 128
ROPE_THETA = 500000.0
EPS = 1e-6
GDN_W = GDN_HEADS * HEAD_DIM
ATT_W = ATT_HEADS * HEAD_DIM
KV_W = N_KV * HEAD_DIM
IDX_W = IDX_HEADS * IDX_DIM
QKV_W = 3 * GDN_W

GDN_CHUNK = 128
ATT_TK = 256
VT_ROWS = HEAD_DIM + 16
IDX_TK = 512
ATT_BLK = 1024
ATT_TQ = 256
LANES = 128
NAT_W = QKV_W + GDN_W + KV_W + LANES
T_AQ, T_AK, T_IQ, T_IK, T_MISC, T_AV = 0, 512, 768, 1280, 1344, 1360
T_ROWS = T_AV + KV_W
VMEM_LIMIT = 56 * 1024 * 1024
INT_MIN = np.int32(-2 ** 31)
NEG_INF = float("-inf")


def _params(sem):
    return pltpu.CompilerParams(dimension_semantics=sem, vmem_limit_bytes=VMEM_LIMIT)


def _dot(a, b):
    return jnp.dot(a, b, preferred_element_type=F32)


def _dot_nt(a, b):
    return lax.dot_general(a, b, (((1,), (1,)), ((), ())), preferred_element_type=F32)


def _split3(x):
    hi = x.astype(BF)
    r1 = x - hi.astype(F32)
    mid = r1.astype(BF)
    lo = (r1 - mid.astype(F32)).astype(BF)
    return hi, mid, lo


def _mm_hi(a, b):
    ah = a.astype(BF)
    al = (a - ah.astype(F32)).astype(BF)
    bh = b.astype(BF)
    bl = (b - bh.astype(F32)).astype(BF)
    m = a.shape[0]
    hi = _dot(jnp.concatenate([ah, al], axis=0), bh)
    return hi[0:m] + (hi[m:] + _dot(ah, bl))


def _sigmoid(x):
    return 1.0 / (1.0 + jnp.exp(-x))


def _silu(x):
    return x * _sigmoid(x)


def _softplus(x):
    return jnp.maximum(x, 0.0) + jnp.log(1.0 + jnp.exp(-jnp.abs(x)))


def _rmsnorm(x, g):
    return x * lax.rsqrt(jnp.mean(x * x, axis=-1, keepdims=True) + EPS) * g


def _mod_kernel(c_ref, w_ref, b_ref, o_ref):
    s = _silu(c_ref[...]).astype(BF)
    o_ref[...] = _dot(s, w_ref[...].astype(BF)) + b_ref[...]


def _modulation(c_all, w_ada, b_ada):
    n, d = c_all.shape
    cols = w_ada.shape[1]
    tn = d
    return pl.pallas_call(
        _mod_kernel,
        grid=(cols // tn,),
        in_specs=[pl.BlockSpec((n, d), lambda j: (0, 0)),
                  pl.BlockSpec((d, tn), lambda j: (0, j)),
                  pl.BlockSpec((1, tn), lambda j: (0, j))],
        out_specs=pl.BlockSpec((n, tn), lambda j: (0, j)),
        out_shape=jax.ShapeDtypeStruct((n, cols), F32),
        compiler_params=_params(("arbitrary",)),
        name="modulation",
    )(c_all, w_ada, b_ada.reshape(1, cols))


def _rope_rows(rt_ref, base, half, cos, sin):
    x1 = rt_ref[base:base + half, :]
    x2 = rt_ref[base + half:base + 2 * half, :]
    rt_ref[base:base + half, :] = x1 * cos - x2 * sin
    rt_ref[base + half:base + 2 * half, :] = x2 * cos + x1 * sin


def _project(x_ref, sc_ref, sh_ref, g1_ref, wn_ref, wt_ref, cs128_ref, cs64_ref, pcol_ref, rt_ref):
    h = _rmsnorm(x_ref[...], g1_ref[...]) * (1.0 + sc_ref[...]) + sh_ref[...]
    hb = h.astype(BF)
    nat = _dot(hb, wn_ref[...])
    rt_ref[...] = _dot_nt(wt_ref[...], hb)
    half = HEAD_DIM // 8
    cos, sin = cs128_ref[0:half, :], cs128_ref[half:2 * half, :]
    for hd in range(ATT_HEADS):
        _rope_rows(rt_ref, T_AQ + hd * HEAD_DIM, half, cos, sin)
    for hd in range(N_KV):
        _rope_rows(rt_ref, T_AK + hd * HEAD_DIM, half, cos, sin)
    half = IDX_DIM // 8
    cos, sin = cs64_ref[0:half, :], cs64_ref[half:2 * half, :]
    for hd in range(IDX_HEADS):
        _rope_rows(rt_ref, T_IQ + hd * IDX_DIM, half, cos, sin)
    _rope_rows(rt_ref, T_IK, half, cos, sin)
    mt = rt_ref[T_MISC:T_MISC + 16, :]
    r = lax.broadcasted_iota(I32, mt.shape, 0)
    a_log, dt_b = pcol_ref[:, 0:1], pcol_ref[:, 1:2]
    gate = -jnp.exp(a_log) * _softplus(mt + dt_b)
    mt = jnp.where(r < IDX_HEADS, mt * IDX_HEADS ** -0.5, jnp.where(r < IDX_HEADS + GDN_HEADS, _sigmoid(mt), gate))
    rt_ref[T_MISC:T_MISC + 16, :] = mt
    return nat, mt


def _misc_natural(m, prow_ref):
    lane = lax.broadcasted_iota(I32, m.shape, 1)
    gate = -jnp.exp(prow_ref[0:1, :]) * _softplus(m + prow_ref[1:2, :])
    return jnp.where(lane < GDN_HEADS, _sigmoid(m), jnp.where(lane < 3 * GDN_HEADS, gate, 0.0))


def _qkv_post(conv, qkvn_ref):
    c = _silu(conv)
    for hb in range(2 * GDN_HEADS):
        xh = c[:, hb * HEAD_DIM:(hb + 1) * HEAD_DIM]
        n = xh * lax.rsqrt(jnp.sum(xh * xh, axis=-1, keepdims=True) + EPS)
        if hb < GDN_HEADS:
            n = n * HEAD_DIM ** -0.5
        qkvn_ref[:, hb * HEAD_DIM:(hb + 1) * HEAD_DIM] = n
    qkvn_ref[:, 2 * GDN_W:] = c[:, 2 * GDN_W:]


def _inproj_prompt_kernel(tm, parts, x_ref, sc_ref, sh_ref, g1_ref, wn_ref, wt_ref, wconv_ref, cs128_ref, cs64_ref,
                          prow_ref, pcol_ref, tril_ref,
                          qkvn_ref, z_ref, vnat_ref, knat_ref, ikT_ref, tail_ref, misc_ref, gct_ref,
                          qT_ref, iqT_ref, iwT_ref, kbf_ref, vT_ref, ikbf_ref,
                          xs_ref, rt_ref):
    i = pl.program_id(1)

    @pl.when(i == 0)
    def _():
        xs_ref[0:8, :] = jnp.zeros((8, QKV_W), F32)

    @pl.when(i > 0)
    def _():
        xs_ref[0:8, :] = xs_ref[tm:tm + 8, :]

    th = tm // parts
    for n in range(parts):
        rows, cols = pl.ds(n * th, th), pl.ds(n * th, th)
        _inproj_prompt_rows(
            th, x_ref.at[rows, :], sc_ref, sh_ref, g1_ref, wn_ref, wt_ref, wconv_ref,
            cs128_ref.at[:, cols], cs64_ref.at[:, cols], prow_ref, pcol_ref, tril_ref.at[0:th, 0:th],
            qkvn_ref.at[rows, :], z_ref.at[rows, :], vnat_ref.at[pl.ds(n * th * N_KV, th * N_KV), :],
            knat_ref.at[pl.ds(n * th * N_KV, th * N_KV), :], ikT_ref.at[:, cols], tail_ref, misc_ref.at[rows, :],
            gct_ref.at[pl.ds(n * (th // GDN_CHUNK), th // GDN_CHUNK)],
            qT_ref.at[:, cols], iqT_ref.at[:, cols], iwT_ref.at[:, cols], kbf_ref.at[rows, :],
            vT_ref.at[pl.ds(n * (th // ATT_TK), th // ATT_TK)], ikbf_ref.at[rows, :],
            xs_ref.at[pl.ds(n * th, th + 8), :], rt_ref.at[:, cols])


def _inproj_prompt_rows(tm, x_ref, sc_ref, sh_ref, g1_ref, wn_ref, wt_ref, wconv_ref, cs128_ref, cs64_ref,
                        prow_ref, pcol_ref, tril_ref,
                        qkvn_ref, z_ref, vnat_ref, knat_ref, ikT_ref, tail_ref, misc_ref, gct_ref,
                        qT_ref, iqT_ref, iwT_ref, kbf_ref, vT_ref, ikbf_ref,
                        xs_ref, rt_ref):
    nat, mt = _project(x_ref, sc_ref, sh_ref, g1_ref, wn_ref, wt_ref, cs128_ref, cs64_ref, pcol_ref, rt_ref)
    xs_ref[8:tm + 8, :] = nat[:, 0:QKV_W]
    conv = wconv_ref[0:1, :] * xs_ref[5:tm + 5, :]
    for t in range(1, CONV_W):
        conv = conv + wconv_ref[t:t + 1, :] * xs_ref[5 + t:tm + 5 + t, :]
    tail_ref[...] = xs_ref[tm:tm + 8, :]
    _qkv_post(conv, qkvn_ref)
    z_ref[...] = nat[:, QKV_W:QKV_W + GDN_W]
    v0 = QKV_W + GDN_W
    for hd in range(N_KV):
        vnat_ref[pl.ds(hd, tm, stride=N_KV), :] = nat[:, v0 + hd * HEAD_DIM:v0 + (hd + 1) * HEAD_DIM]

    gm = _misc_natural(nat[:, NAT_W - LANES:], prow_ref)
    tril = tril_ref[...]
    hi, mid, lo = _split3(gm)
    gc = _dot(tril, hi) + (_dot(tril, mid) + _dot(tril, lo))
    lane = lax.broadcasted_iota(I32, gm.shape, 1)
    misc_ref[...] = jnp.where(lane < 2 * GDN_HEADS, gm, gc)
    hi, mid, lo = _split3(mt)
    gct = _dot_nt(hi, tril) + (_dot_nt(mid, tril) + _dot_nt(lo, tril))
    r = lax.broadcasted_iota(I32, mt.shape, 0)
    bg = jnp.where(r < IDX_HEADS + GDN_HEADS, mt, gct)[8:16, :]
    for cc in range(tm // GDN_CHUNK):
        gct_ref[cc] = bg[:, cc * GDN_CHUNK:(cc + 1) * GDN_CHUNK]

    iwT_ref[...] = mt[0:IDX_HEADS, :]
    qT_ref[...] = rt_ref[T_AQ:T_AQ + ATT_W, :].astype(BF)
    iqT_ref[...] = rt_ref[T_IQ:T_IQ + IDX_W, :].astype(BF)
    kn = rt_ref[T_AK:T_AK + KV_W, :].T
    for hd in range(N_KV):
        knat_ref[pl.ds(hd, tm, stride=N_KV), :] = kn[:, hd * HEAD_DIM:(hd + 1) * HEAD_DIM]
    kbf_ref[...] = kn.astype(BF)
    ikT_ref[...] = rt_ref[T_IK:T_IK + IDX_DIM, :]
    ikbf_ref[...] = rt_ref[T_IK:T_IK + LANES, :].T[:, 0:IDX_DIM].astype(BF)
    for cc in range(tm // ATT_TK):
        for hd in range(N_KV):
            lo = hd * VT_ROWS
            vT_ref[cc, lo:lo + HEAD_DIM, :] = rt_ref[T_AV + hd * HEAD_DIM:T_AV + (hd + 1) * HEAD_DIM,
                                                     cc * ATT_TK:(cc + 1) * ATT_TK].astype(BF)
            vT_ref[cc, lo + HEAD_DIM:lo + VT_ROWS, :] = jnp.ones((VT_ROWS - HEAD_DIM, ATT_TK), BF)


def _inproj_sample_kernel(x_ref, sc_ref, sh_ref, g1_ref, wn_ref, wt_ref, wconv_ref, cs128_ref, cs64_ref,
                          prow_ref, pcol_ref, s0_ref, s1_ref, s2_ref,
                          qkvn_ref, z_ref, vnat_ref, knat_ref, slab_ref, raw_ref, misc_ref, aq_ref, iq_ref,
                          rt_ref):
    nat, _ = _project(x_ref, sc_ref, sh_ref, g1_ref, wn_ref, wt_ref, cs128_ref, cs64_ref, pcol_ref, rt_ref)
    raw = nat[:, 0:QKV_W]
    raw_ref[...] = raw
    conv = wconv_ref[0:1, :] * s0_ref[...]
    conv = conv + wconv_ref[1:2, :] * s1_ref[...]
    conv = conv + wconv_ref[2:3, :] * s2_ref[...]
    conv = conv + wconv_ref[3:4, :] * raw
    _qkv_post(conv, qkvn_ref)
    z_ref[...] = nat[:, QKV_W:QKV_W + GDN_W]
    vnat_ref[...] = nat[:, QKV_W + GDN_W:QKV_W + GDN_W + KV_W]
    misc_ref[...] = _misc_natural(nat[:, NAT_W - LANES:], prow_ref)
    aq_ref[...] = rt_ref[T_AQ:T_AQ + ATT_W, :].T
    iq_ref[...] = rt_ref[T_IQ:T_IQ + IDX_W, :].T
    knat_ref[...] = rt_ref[T_AK:T_AK + KV_W, :].T
    slab_ref[...] = rt_ref[T_IK:T_IK + LANES, :].T


def _inproj_weights(w_in, a_log, dt_bias):
    offs = np.cumsum([0, QKV_W, GDN_W, GDN_HEADS, GDN_HEADS, ATT_W, KV_W, KV_W, IDX_W, IDX_DIM, IDX_HEADS])
    qkv, z, beta, a, aq, ak, av, iq, ik, iw = [w_in[:, offs[n]:offs[n + 1]] for n in range(10)]
    d = w_in.shape[0]
    pad = jnp.zeros((d, LANES - 3 * GDN_HEADS), w_in.dtype)
    w_nat = jnp.concatenate([qkv, z, av, beta, a, a, pad], axis=1).astype(BF)
    w_t = jnp.concatenate([aq, ak, iq, ik, iw, beta, a, av], axis=1).T.astype(BF)
    zrow = jnp.zeros((LANES - 3 * GDN_HEADS,), F32)
    prow = jnp.stack([jnp.concatenate([jnp.zeros((GDN_HEADS,), F32), a_log, a_log, zrow]),
                      jnp.concatenate([jnp.zeros((GDN_HEADS,), F32), dt_bias, dt_bias, zrow])])
    z12 = jnp.zeros((IDX_HEADS + GDN_HEADS,), F32)
    pcol = jnp.stack([jnp.concatenate([z12, a_log]), jnp.concatenate([z12, dt_bias])], axis=1)
    return w_nat, w_t, prow, pcol


def _rope_tables(pos, dim):
    half = dim // 8
    inv = ROPE_THETA ** (-jnp.arange(half, dtype=F32) / half)
    ang = pos.astype(F32)[:, None] * inv[None, :]
    return jnp.concatenate([jnp.cos(ang).T, jnp.sin(ang).T], axis=0)


def _inproj_prompt(x, mod, g1, w_nat, w_t, w_conv, prow, pcol, tm):
    bp, tp, d = x.shape
    nt = tp // tm
    rows = bp * tp
    pos = jnp.arange(tp, dtype=I32)
    cs128, cs64 = _rope_tables(pos, HEAD_DIM), _rope_tables(pos, IDX_DIM)
    ri = np.arange(tm)
    tril = jnp.asarray((ri[:, None] // GDN_CHUNK == ri[None, :] // GDN_CHUNK) & (ri[None, :] <= ri[:, None]), BF)
    const = lambda shape: pl.BlockSpec(shape, lambda b, i: (0,) * len(shape))
    rowblk = lambda w: pl.BlockSpec((tm, w), lambda b, i: (b * nt + i, 0))
    colblk = lambda h: pl.BlockSpec((h, tm), lambda b, i: (0, b * nt + i))
    in_specs = [
        pl.BlockSpec((None, tm, d), lambda b, i: (b, i, 0)),
        pl.BlockSpec((None, 1, d), lambda b, i: (b, 0, 1)),
        pl.BlockSpec((None, 1, d), lambda b, i: (b, 0, 0)),
        const((1, d)), const((d, NAT_W)), const((T_ROWS, d)), const((CONV_W, QKV_W)),
        pl.BlockSpec((HEAD_DIM // 4, tm), lambda b, i: (0, i)),
        pl.BlockSpec((IDX_DIM // 4, tm), lambda b, i: (0, i)),
        const((2, LANES)), const((16, 2)), const((tm, tm)),
    ]
    out_shape = [
        jax.ShapeDtypeStruct((rows, QKV_W), F32), jax.ShapeDtypeStruct((rows, GDN_W), F32),
        jax.ShapeDtypeStruct((rows * N_KV, HEAD_DIM), F32), jax.ShapeDtypeStruct((rows * N_KV, HEAD_DIM), F32),
        jax.ShapeDtypeStruct((bp, IDX_DIM, tp), F32), jax.ShapeDtypeStruct((bp, 8, QKV_W), F32),
        jax.ShapeDtypeStruct((rows, LANES), F32), jax.ShapeDtypeStruct((rows // GDN_CHUNK, 8, GDN_CHUNK), F32),
        jax.ShapeDtypeStruct((ATT_W, rows), BF), jax.ShapeDtypeStruct((IDX_W, rows), BF),
        jax.ShapeDtypeStruct((IDX_HEADS, rows), F32), jax.ShapeDtypeStruct((rows, KV_W), BF),
        jax.ShapeDtypeStruct((rows // ATT_TK, N_KV * VT_ROWS, ATT_TK), BF), jax.ShapeDtypeStruct((rows, IDX_DIM), BF),
    ]
    out_specs = [
        rowblk(QKV_W), rowblk(GDN_W),
        pl.BlockSpec((tm * N_KV, HEAD_DIM), lambda b, i: (b * nt + i, 0)),
        pl.BlockSpec((tm * N_KV, HEAD_DIM), lambda b, i: (b * nt + i, 0)),
        pl.BlockSpec((None, IDX_DIM, tm), lambda b, i: (b, 0, i)),
        pl.BlockSpec((None, 8, QKV_W), lambda b, i: (b, 0, 0)),
        rowblk(LANES),
        pl.BlockSpec((tm // GDN_CHUNK, 8, GDN_CHUNK), lambda b, i: (b * nt + i, 0, 0)),
        colblk(ATT_W), colblk(IDX_W), colblk(IDX_HEADS), rowblk(KV_W),
        pl.BlockSpec((tm // ATT_TK, N_KV * VT_ROWS, ATT_TK), lambda b, i: (b * nt + i, 0, 0)),
        rowblk(IDX_DIM),
    ]
    return pl.pallas_call(
        functools.partial(_inproj_prompt_kernel, tm, tm // _pick(tm, (ATT_TK,))),
        grid=(bp, nt), in_specs=in_specs, out_specs=out_specs, out_shape=out_shape,
        scratch_shapes=[pltpu.VMEM((tm + 8, QKV_W), F32), pltpu.VMEM((T_ROWS, tm), F32)],
        compiler_params=_params(("arbitrary", "arbitrary")),
        name="inproj_prompt",
    )(x, mod, mod, g1.reshape(1, d), w_nat, w_t, w_conv, cs128, cs64, prow, pcol, tril)


def _inproj_sample(x, mod, g1, w_nat, w_t, w_conv, prow, pcol, conv_state, past):
    db, d = x.shape
    pos = jnp.full((db,), past, I32)
    cs128, cs64 = _rope_tables(pos, HEAD_DIM), _rope_tables(pos, IDX_DIM)
    full = lambda shape: pl.BlockSpec(shape, lambda i: (0,) * len(shape))
    in_specs = [
        full((db, d)),
        pl.BlockSpec((db, d), lambda i: (0, 1)), pl.BlockSpec((db, d), lambda i: (0, 0)),
        full((1, d)), full((d, NAT_W)), full((T_ROWS, d)), full((CONV_W, QKV_W)),
        full((HEAD_DIM // 4, db)), full((IDX_DIM // 4, db)), full((2, LANES)), full((16, 2)),
        full((db, QKV_W)), full((db, QKV_W)), full((db, QKV_W)),
    ]
    widths = [QKV_W, GDN_W, KV_W, KV_W, LANES, QKV_W, LANES, ATT_W, IDX_W]
    return pl.pallas_call(
        _inproj_sample_kernel,
        grid=(1,), in_specs=in_specs,
        out_specs=[full((db, w)) for w in widths],
        out_shape=[jax.ShapeDtypeStruct((db, w), F32) for w in widths],
        scratch_shapes=[pltpu.VMEM((T_ROWS, db), F32)],
        compiler_params=_params(("arbitrary",)),
        name="inproj_sample",
    )(x, mod, mod, g1.reshape(1, d), w_nat, w_t, w_conv, cs128, cs64, prow, pcol,
      conv_state[:, 0], conv_state[:, 1], conv_state[:, 2])


def _unit_lower_inverses(lows):
    c = lows[0].shape[0]
    ii = lax.broadcasted_iota(I32, (c, c), 0)
    jj = lax.broadcasted_iota(I32, (c, c), 1)
    eye = jnp.where(ii == jj, 1.0, 0.0)
    levels = int(math.log2(c)) - 1
    ts = [eye - low for low in lows]
    ps = [_mm_hi(low, low) for low in lows]
    for lvl in range(levels):
        if lvl == levels - 1:
            ts = [t + _mm_hi(t, p) for t, p in zip(ts, ps)]
        else:
            both = [_mm_hi(jnp.concatenate([t, p], axis=0), p) for t, p in zip(ts, ps)]
            ts = [t + b[0:c] for t, b in zip(ts, both)]
            ps = [b[c:] for b in both]
    return ts


def _gdn_prompt_kernel(tg, qkvn_ref, z_ref, misc_ref, gct_ref, gn_ref, o_ref, ssm_ref, s_ref):
    i = pl.program_id(1)

    @pl.when(i == 0)
    def _():
        s_ref[...] = jnp.zeros(s_ref.shape, F32)

    c = GDN_CHUNK
    ii = lax.broadcasted_iota(I32, (c, c), 0)
    jj = lax.broadcasted_iota(I32, (c, c), 1)
    pairs = [(cc, hd) for cc in range(tg // c) for hd in range(GDN_HEADS)]

    qs, ks, gcs, rhs, lows, intras = {}, {}, {}, {}, [], {}
    for cc, hd in pairs:
        r0, lo = cc * c, hd * HEAD_DIM
        q = qkvn_ref[r0:r0 + c, lo:lo + HEAD_DIM]
        k = qkvn_ref[r0:r0 + c, GDN_W + lo:GDN_W + lo + HEAD_DIM]
        v = qkvn_ref[r0:r0 + c, 2 * GDN_W + lo:2 * GDN_W + lo + HEAD_DIM]
        beta_c = misc_ref[r0:r0 + c, hd:hd + 1]
        gc_c = misc_ref[r0:r0 + c, 2 * GDN_HEADS + hd:2 * GDN_HEADS + hd + 1]
        gc_r = gct_ref[cc, GDN_HEADS + hd:GDN_HEADS + hd + 1, :]
        decay = jnp.where(ii >= jj, jnp.exp(jnp.where(ii >= jj, gc_c - gc_r, 0.0)), 0.0)
        kb = k * beta_c
        both = _dot_nt(jnp.concatenate([kb, q], axis=0).astype(BF), k.astype(BF))
        lows.append(jnp.where(ii > jj, both[0:c] * decay, 0.0))
        intras[cc, hd] = (both[c:] * decay).astype(BF)
        rhs[cc, hd] = jnp.concatenate([v * beta_c, kb * jnp.exp(gc_c)], axis=1).astype(BF)
        qs[cc, hd], ks[cc, hd], gcs[cc, hd] = q, k, gc_c
    ts = _unit_lower_inverses(lows)
    uws = {p: _dot(t.astype(BF), rhs[p]) for p, t in zip(pairs, ts)}

    for cc, hd in pairs:
        r0, lo = cc * c, hd * HEAD_DIM
        q, k, gc_c, uw = qs[cc, hd], ks[cc, hd], gcs[cc, hd], uws[cc, hd]
        s = s_ref[hd]
        ws = _dot(jnp.concatenate([uw[:, HEAD_DIM:], q * jnp.exp(gc_c)], axis=0).astype(BF), s.astype(BF))
        v_new = (uw[:, 0:HEAD_DIM] - ws[0:c]).astype(BF)
        o = ws[c:] + _dot(intras[cc, hd], v_new)
        g_last = gc_c[c - 1:c, :]
        kd = k * jnp.exp(g_last - gc_c)
        s_ref[hd] = s * jnp.exp(g_last) + _dot(kd.T.astype(BF), v_new)
        o = _rmsnorm(o, gn_ref[...]) * _silu(z_ref[r0:r0 + c, lo:lo + HEAD_DIM])
        o_ref[r0:r0 + c, lo:lo + HEAD_DIM] = o.astype(BF)
    ssm_ref[...] = s_ref[...]


def _gdn_prompt(qkvn, z, misc, gct, g_norm, bp, tp, tg):
    nt = tp // tg
    rows = bp * tp
    rowblk = lambda w: pl.BlockSpec((tg, w), lambda b, i: (b * nt + i, 0))
    return pl.pallas_call(
        functools.partial(_gdn_prompt_kernel, tg),
        grid=(bp, nt),
        in_specs=[rowblk(QKV_W), rowblk(GDN_W), rowblk(LANES),
                  pl.BlockSpec((tg // GDN_CHUNK, 8, GDN_CHUNK), lambda b, i: (b * nt + i, 0, 0)),
                  pl.BlockSpec((1, HEAD_DIM), lambda b, i: (0, 0))],
        out_specs=[rowblk(GDN_W),
                   pl.BlockSpec((None, GDN_HEADS, HEAD_DIM, HEAD_DIM), lambda b, i: (b, 0, 0, 0))],
        out_shape=[jax.ShapeDtypeStruct((rows, GDN_W), BF),
                   jax.ShapeDtypeStruct((bp, GDN_HEADS, HEAD_DIM, HEAD_DIM), F32)],
        scratch_shapes=[pltpu.VMEM((GDN_HEADS, HEAD_DIM, HEAD_DIM), F32)],
        compiler_params=_params(("arbitrary", "arbitrary")),
        name="gdn_prompt",
    )(qkvn, z, misc, gct, g_norm.reshape(1, HEAD_DIM))


def _gdn_sample_kernel(nb, qkvn_ref, z_ref, misc_ref, gn_ref, s_ref, o_ref, ssm_ref):
    for bi in range(nb):
        for hd in range(GDN_HEADS):
            lo = hd * HEAD_DIM
            q = qkvn_ref[bi:bi + 1, lo:lo + HEAD_DIM]
            k = qkvn_ref[bi:bi + 1, GDN_W + lo:GDN_W + lo + HEAD_DIM]
            v = qkvn_ref[bi:bi + 1, 2 * GDN_W + lo:2 * GDN_W + lo + HEAD_DIM]
            beta = misc_ref[bi:bi + 1, hd:hd + 1]
            g = misc_ref[bi:bi + 1, GDN_HEADS + hd:GDN_HEADS + hd + 1]
            kcol = jnp.broadcast_to(k, (HEAD_DIM, HEAD_DIM)).T
            qcol = jnp.broadcast_to(q, (HEAD_DIM, HEAD_DIM)).T
            s = s_ref[bi, hd] * jnp.exp(g)
            ks = jnp.sum(kcol * s, axis=0, keepdims=True)
            delta = (v - ks) * beta
            s = s + kcol * delta
            ssm_ref[bi, hd] = s
            o = jnp.sum(qcol * s, axis=0, keepdims=True)
            o = _rmsnorm(o, gn_ref[...]) * _silu(z_ref[bi:bi + 1, lo:lo + HEAD_DIM])
            o_ref[bi:bi + 1, lo:lo + HEAD_DIM] = o.astype(BF)


def _gdn_sample(qkvn, z, misc, g_norm, state, nb):
    db = qkvn.shape[0]
    rowblk = lambda w: pl.BlockSpec((nb, w), lambda i: (i, 0))
    sblk = pl.BlockSpec((nb, GDN_HEADS, HEAD_DIM, HEAD_DIM), lambda i: (i, 0, 0, 0))
    return pl.pallas_call(
        functools.partial(_gdn_sample_kernel, nb),
        grid=(db // nb,),
        in_specs=[rowblk(QKV_W), rowblk(GDN_W), rowblk(LANES), pl.BlockSpec((1, HEAD_DIM), lambda i: (0, 0)), sblk],
        out_specs=[rowblk(GDN_W), sblk],
        out_shape=[jax.ShapeDtypeStruct((db, GDN_W), BF), jax.ShapeDtypeStruct(state.shape, F32)],
        compiler_params=_params(("arbitrary",)),
        name="gdn_sample",
    )(qkvn, z, misc, g_norm.reshape(1, HEAD_DIM), state)


def _ordered_word_to_float(u):
    s = u ^ INT_MIN
    return lax.bitcast_convert_type(s ^ ((s >> 31) & np.int32(0x7FFFFFFF)), F32)


def _select_topk_bias(i_ref, t_ref, nch, tk, k_sel, idx_bits, taken=0.0, dropped=NEG_INF):
    nl = i_ref.shape[1]
    row = lax.broadcasted_iota(I32, (tk, 1), 0)

    def chunk(c):
        return pl.ds(pl.multiple_of(c * tk, tk), tk)

    def count(ref, pred):
        def body(c, acc):
            m = pred(ref[chunk(c), :])
            ones = jnp.where(m, 1, 0).astype(I32).reshape(tk // 8, 8, nl)
            parts = [ones[n] for n in range(tk // 8)]
            while len(parts) > 1:
                parts = [a + b for a, b in zip(parts[0::2], parts[1::2])] + parts[len(parts) & ~1:]
            return acc + parts[0]
        acc = lax.fori_loop(0, nch, body, jnp.zeros((8, nl), I32))
        return jnp.sum(acc, axis=0, keepdims=True)

    def bit_body(i, carry):
        t_u, n_ge = carry
        cand_u = t_u | jnp.left_shift(jnp.int32(1), 31 - i)
        cand = _ordered_word_to_float(cand_u)
        cnt = count(i_ref, lambda sc: sc >= cand)
        take = cnt >= k_sel
        return jnp.where(take, cand_u, t_u), jnp.where(take, cnt, n_ge)

    t_u, n_ge = lax.fori_loop(0, 32, bit_body, (jnp.zeros((1, nl), I32), jnp.full((1, nl), -1, I32)))
    thr = jnp.where((t_u >> 23) == 0, NEG_INF, _ordered_word_to_float(t_u))

    def tie_break(_):
        need = k_sel - count(i_ref, lambda sc: sc > thr)

        def mark(c, carry):
            t_ref[chunk(c), :] = jnp.where(i_ref[chunk(c), :] == thr, c * tk + row, 2 ** idx_bits)
            return carry

        lax.fori_loop(0, nch, mark, 0)

        def idx_body(i, x):
            cand = x | jnp.left_shift(jnp.int32(1), idx_bits - 1 - i)
            cnt = count(t_ref, lambda idx: idx < cand)
            return jnp.where(cnt < need, cand, x)

        return lax.fori_loop(0, idx_bits, idx_body, jnp.zeros((1, nl), I32))

    x = lax.cond(jnp.max(n_ge) > k_sel, tie_break, lambda _: jnp.full((1, nl), 2 ** idx_bits - 1, I32), 0)

    def write(c, carry):
        sc = i_ref[chunk(c), :]
        sel = (sc > thr) | ((sc == thr) & (c * tk + row <= x))
        i_ref[chunk(c), :] = jnp.where(sel & (sc > NEG_INF), taken, dropped)
        return carry

    lax.fori_loop(0, nch, write, 0)


def _attn_prompt_kernel(tq, tk, blk, k_sel, idx_bits, qT_ref, iqT_ref, iwT_ref, kbf_ref, vT_ref, ikbf_ref, o_ref,
                        i_ref, t_ref):
    t0 = pl.program_id(1) * tq
    nch = (t0 + tq + tk - 1) // tk
    nblk = (t0 + tq + blk - 1) // blk
    qpos = t0 + lax.broadcasted_iota(I32, (1, tq), 1)
    row = lax.broadcasted_iota(I32, (tk, 1), 0)
    w = iwT_ref[...]
    pairs = IDX_HEADS // 2
    rhs = [jnp.concatenate([iqT_ref[(2 * p) * IDX_DIM:(2 * p + 1) * IDX_DIM, :],
                            iqT_ref[(2 * p + 1) * IDX_DIM:(2 * p + 2) * IDX_DIM, :]], axis=1) for p in range(pairs)]

    def chunk(c):
        return pl.ds(pl.multiple_of(c * tk, tk), tk)

    def idx_body(c, carry):
        ikc = ikbf_ref[chunk(c), :]
        acc = jnp.zeros((tk, tq), F32)
        for p in range(pairs):
            d = jnp.maximum(_dot(ikc, rhs[p]), 0.0)
            acc = acc + d[:, 0:tq] * w[2 * p:2 * p + 1, :] + d[:, tq:] * w[2 * p + 1:2 * p + 2, :]
        i_ref[chunk(c), :] = jnp.where(c * tk + row <= qpos, acc, NEG_INF)
        return carry

    lax.fori_loop(0, nch, idx_body, 0)

    def fill_body(c, carry):
        i_ref[chunk(c), :] = jnp.full((tk, tq), NEG_INF, F32)
        return carry

    lax.fori_loop(nch, nblk * (blk // tk), fill_body, 0)
    tks = max(tk * LANES // tq, 8)
    _select_topk_bias(i_ref, t_ref, (t0 + tq + tks - 1) // tks, tks, k_sel, idx_bits)

    scale2 = HEAD_DIM ** -0.5 * math.log2(math.e)
    rep = ATT_HEADS // N_KV
    sub = ATT_TK
    nsub = blk // sub
    qg = [jnp.concatenate([qT_ref[(rep * g + r) * HEAD_DIM:(rep * g + r + 1) * HEAD_DIM, :] for r in range(rep)],
                          axis=1) for g in range(N_KV)]

    def body(c, carry):
        out = []
        for g in range(N_KV):
            m, acc = carry[g]
            ss = []
            for i in range(nsub):
                rows = pl.ds(pl.multiple_of(c * blk + i * sub, sub), sub)
                bias = i_ref[rows, :]
                ss.append(_dot(kbf_ref[rows, g * HEAD_DIM:(g + 1) * HEAD_DIM], qg[g]) * scale2
                          + jnp.concatenate([bias] * rep, axis=1))
            m_new = m
            for s in ss:
                m_new = jnp.maximum(m_new, jnp.max(s, axis=0, keepdims=True))
            m_safe = jnp.where(m_new == NEG_INF, 0.0, m_new)
            acc = acc * jnp.exp2(m - m_safe)
            for i, s in enumerate(ss):
                p = jnp.exp2(s - m_safe).astype(BF)
                acc = acc + _dot(vT_ref[c * nsub + i, g * VT_ROWS:(g + 1) * VT_ROWS, :], p)
            out.append((m_new, acc))
        return tuple(out)

    init = tuple((jnp.full((1, rep * tq), NEG_INF, F32), jnp.zeros((VT_ROWS, rep * tq), F32)) for _ in range(N_KV))
    res = lax.fori_loop(0, nblk, body, init)
    for g in range(N_KV):
        acc = res[g][1]
        o_t = acc[0:HEAD_DIM] / acc[HEAD_DIM:HEAD_DIM + 1]
        for r in range(rep):
            hd = rep * g + r
            o_ref[:, hd * HEAD_DIM:(hd + 1) * HEAD_DIM] = o_t[:, r * tq:(r + 1) * tq].T.astype(BF)


def _attn_prompt(qT, iqT, iwT, kbf, vT, ikbf, bp, tp):
    tq = _pick(tp, (ATT_TQ, QBLOCK))
    tk = _pick(tp, (IDX_TK, ATT_TK))
    blk = _pick(tp, (ATT_BLK, IDX_TK, ATT_TK))
    assert blk % tk == 0 and blk % ATT_TK == 0
    nq = tp // tq
    rows = bp * tp
    k_sel = min(TOPK_MAX, tp // 4)
    idx_bits = max(1, int(tp - 1).bit_length())
    colblk = lambda h: pl.BlockSpec((h, tq), lambda b, j: (0, b * nq + j))
    return pl.pallas_call(
        functools.partial(_attn_prompt_kernel, tq, tk, blk, k_sel, idx_bits),
        grid=(bp, nq),
        in_specs=[colblk(ATT_W), colblk(IDX_W), colblk(IDX_HEADS),
                  pl.BlockSpec((tp, KV_W), lambda b, j: (b, 0)),
                  pl.BlockSpec((tp // ATT_TK, N_KV * VT_ROWS, ATT_TK), lambda b, j: (b, 0, 0)),
                  pl.BlockSpec((tp, IDX_DIM), lambda b, j: (b, 0))],
        out_specs=pl.BlockSpec((tq, ATT_W), lambda b, j: (b * nq + j, 0)),
        out_shape=jax.ShapeDtypeStruct((rows, ATT_W), BF),
        scratch_shapes=[pltpu.VMEM((tp, tq), F32), pltpu.VMEM((tp, tq), I32)],
        compiler_params=_params(("arbitrary", "arbitrary")),
        name="attn_prompt",
    )(qT, iqT, iwT, kbf, vT, ikbf)


def _sidx_kernel(npg, pt_ref, iq_ref, iw_ref, ikn_ref, *refs):
    pages, out_ref = refs[:npg], refs[npg]
    iq = iq_ref[...].astype(BF)
    w = iw_ref[...]
    for p in range(npg):
        d = jnp.maximum(_dot(iq, pages[p][...].astype(BF)), 0.0)
        out_ref[:, p * LANES:(p + 1) * LANES] = jnp.sum(d * w, axis=0, keepdims=True)
    dn = jnp.sum(iq.astype(F32) * ikn_ref[...].astype(BF).astype(F32), axis=1, keepdims=True)
    sn = jnp.sum(jnp.maximum(dn, 0.0) * w, axis=0, keepdims=True)
    lane = lax.broadcasted_iota(I32, (1, LANES), 1)
    out_ref[:, npg * LANES:(npg + 1) * LANES] = jnp.where(lane == 0, sn, NEG_INF)


def _sample_index_scores(iq, iw, ik_new, cache_ik_t, page_table):
    db, npg = page_table.shape
    page = cache_ik_t.shape[2]
    assert page == LANES
    width = (npg + 1) * LANES
    page_specs = [pl.BlockSpec((None, IDX_DIM, page), functools.partial(lambda p, b, pt: (pt[b, p], 0, 0), p))
                  for p in range(npg)]
    grid_spec = pltpu.PrefetchScalarGridSpec(
        num_scalar_prefetch=1, grid=(db,),
        in_specs=[pl.BlockSpec((None, IDX_HEADS, IDX_DIM), lambda b, pt: (b, 0, 0)),
                  pl.BlockSpec((None, IDX_HEADS, 1), lambda b, pt: (b, 0, 0)),
                  pl.BlockSpec((None, 1, IDX_DIM), lambda b, pt: (b, 0, 0))] + page_specs,
        out_specs=pl.BlockSpec((None, 1, width), lambda b, pt: (b, 0, 0)))
    out = pl.pallas_call(
        functools.partial(_sidx_kernel, npg), grid_spec=grid_spec,
        out_shape=jax.ShapeDtypeStruct((db, 1, width), F32),
        compiler_params=_params(("arbitrary",)),
        name="sample_index_scores",
    )(page_table, iq.reshape(db, IDX_HEADS, IDX_DIM), iw.reshape(db, IDX_HEADS, 1), ik_new.reshape(db, 1, IDX_DIM),
      *([cache_ik_t] * npg))
    return out.reshape(db, width)


def _ssel_kernel(nch, k_sel, idx_bits, s_ref, spread_ref, keep_ref, i_ref, t_ref):
    for p in range(nch):
        i_ref[p * LANES:(p + 1) * LANES, :] = s_ref[:, p * LANES:(p + 1) * LANES].T
    _select_topk_bias(i_ref, t_ref, nch, LANES, k_sel, idx_bits, taken=1.0, dropped=0.0)
    rows = N_KV * LANES
    for p in range(nch - 1):
        k2 = _dot(spread_ref[...], i_ref[p * LANES:(p + 1) * LANES, :].astype(BF))
        for h in range(N_KV):
            keep_ref[:, p * rows + h * LANES:p * rows + (h + 1) * LANES] = k2[h * LANES:(h + 1) * LANES, :].T
    keep_ref[:, (nch - 1) * rows:(nch - 1) * rows + LANES] = i_ref[(nch - 1) * LANES:nch * LANES, :].T


def _sample_select(scores, n_keys):
    db, width = scores.shape
    assert db == LANES
    nch = width // LANES
    k_sel = min(TOPK_MAX, n_keys // 4)
    idx_bits = max(1, int(width - 1).bit_length())
    rows = N_KV * LANES
    out_w = (nch - 1) * rows + LANES
    spread = jnp.asarray(np.arange(rows)[:, None] // N_KV == np.arange(LANES)[None, :], BF)
    return pl.pallas_call(
        functools.partial(_ssel_kernel, nch, k_sel, idx_bits),
        grid=(1,),
        in_specs=[pl.BlockSpec((db, width), lambda i: (0, 0)), pl.BlockSpec((rows, LANES), lambda i: (0, 0))],
        out_specs=pl.BlockSpec((db, out_w), lambda i: (0, 0)),
        out_shape=jax.ShapeDtypeStruct((db, out_w), F32),
        scratch_shapes=[pltpu.VMEM((width, db), F32), pltpu.VMEM((width, db), I32)],
        compiler_params=_params(("arbitrary",)),
        name="sample_select",
    )(scores, spread)


def _sattn_kernel(npg, pt_ref, q_ref, keep_ref, kn_ref, vn_ref, *refs):
    kpages, vpages, o_ref = refs[:npg], refs[npg:2 * npg], refs[2 * npg]
    rep = ATT_HEADS // N_KV
    rows = N_KV * LANES
    scale = HEAD_DIM ** -0.5
    q = q_ref[...]
    q8 = jnp.concatenate([q, jnp.zeros((8 - ATT_HEADS, HEAD_DIM), F32)], axis=0).astype(BF)
    hrow = lax.broadcasted_iota(I32, (8, 1), 0)
    lane = lax.broadcasted_iota(I32, (1, LANES), 1)
    own_kv = lax.broadcasted_iota(I32, (1, rows), 1) % N_KV == hrow // rep
    parts = []
    for p in range(npg):
        s2 = _dot_nt(q8, kpages[p][...].astype(BF))
        keep2 = keep_ref[:, p * rows:(p + 1) * rows]
        parts.append(jnp.where(own_kv & (keep2 > 0.5), s2 * scale, NEG_INF))
    kn = kn_ref[...].astype(BF).astype(F32)
    kn8 = jnp.where(hrow // rep == 0, kn[0:1, :], kn[1:2, :])
    s_new = jnp.sum(q8.astype(F32) * kn8, axis=1, keepdims=True)
    keep_new = keep_ref[:, npg * rows:npg * rows + 1]
    parts.append(jnp.where((lane == 0) & (keep_new > 0.5), s_new * scale, NEG_INF))
    s = jnp.concatenate(parts, axis=1)
    m = jnp.max(s, axis=1, keepdims=True)
    e = jnp.exp(s - m)
    pr = (e / jnp.sum(e, axis=1, keepdims=True)).astype(BF)
    o8 = jnp.zeros((8, HEAD_DIM), F32)
    for p in range(npg):
        o8 = o8 + _dot(pr[:, p * rows:(p + 1) * rows], vpages[p][...].astype(BF))
    vn = vn_ref[...].astype(BF).astype(F32)
    vn8 = jnp.where(hrow // rep == 0, vn[0:1, :], vn[1:2, :])
    o8 = o8 + pr[:, npg * rows:npg * rows + 1].astype(F32) * vn8
    for hd in range(ATT_HEADS):
        o_ref[:, hd * HEAD_DIM:(hd + 1) * HEAD_DIM] = o8[hd:hd + 1, :].astype(BF)


def _sample_attention(q, keep, k_new, v_new, cache_k, cache_v, page_table):
    db, npg = page_table.shape
    n_phys, page = cache_k.shape[0], cache_k.shape[1]
    assert page == LANES and N_KV == 2
    rows = page * N_KV
    width = npg * rows + LANES
    ck = cache_k.reshape(n_phys, rows, HEAD_DIM)
    cv = cache_v.reshape(n_phys, rows, HEAD_DIM)
    page_specs = [pl.BlockSpec((None, rows, HEAD_DIM), functools.partial(lambda p, b, pt: (pt[b, p], 0, 0), p))
                  for p in range(npg)]
    grid_spec = pltpu.PrefetchScalarGridSpec(
        num_scalar_prefetch=1, grid=(db,),
        in_specs=[pl.BlockSpec((None, ATT_HEADS, HEAD_DIM), lambda b, pt: (b, 0, 0)),
                  pl.BlockSpec((None, 1, width), lambda b, pt: (b, 0, 0)),
                  pl.BlockSpec((None, N_KV, HEAD_DIM), lambda b, pt: (b, 0, 0)),
                  pl.BlockSpec((None, N_KV, HEAD_DIM), lambda b, pt: (b, 0, 0))] + page_specs + page_specs,
        out_specs=pl.BlockSpec((None, 1, ATT_W), lambda b, pt: (b, 0, 0)))
    out = pl.pallas_call(
        functools.partial(_sattn_kernel, npg), grid_spec=grid_spec,
        out_shape=jax.ShapeDtypeStruct((db, 1, ATT_W), BF),
        compiler_params=_params(("arbitrary",)),
        name="sample_attention",
    )(page_table, q.reshape(db, ATT_HEADS, HEAD_DIM), keep.reshape(db, 1, width),
      k_new.reshape(db, N_KV, HEAD_DIM), v_new.reshape(db, N_KV, HEAD_DIM), *([ck] * npg), *([cv] * npg))
    return out.reshape(db, ATT_W)


def _post_kernel(final, nff, og_ref, oa_ref, x_ref, ga1_ref, sc2_ref, sh2_ref, ga2_ref, g2_ref, gf_ref,
                 wo_ref, wg_ref, wu_ref, wd_ref, y_ref, x1_ref, h2_ref, acc_ref):
    jf = pl.program_id(1)

    @pl.when(jf == 0)
    def _():
        mixed = _dot(og_ref[...], wo_ref[0:GDN_W, :]) + _dot(oa_ref[...], wo_ref[GDN_W:, :])
        x1 = x_ref[...] + ga1_ref[...] * mixed
        x1_ref[...] = x1
        h2_ref[...] = (_rmsnorm(x1, g2_ref[...]) * (1.0 + sc2_ref[...]) + sh2_ref[...]).astype(BF)
        acc_ref[...] = jnp.zeros(acc_ref.shape, F32)

    h2 = h2_ref[...]
    act = (_silu(_dot(h2, wg_ref[...])) * _dot(h2, wu_ref[...])).astype(BF)
    acc_ref[...] += _dot(act, wd_ref[...])

    @pl.when(jf == nff - 1)
    def _():
        x2 = x1_ref[...] + ga2_ref[...] * acc_ref[...]
        y_ref[...] = _rmsnorm(x2, gf_ref[...]) if final else x2


def _post(og, oa, x2d, mod, mod_spec, g2, gf, wo, wfi, wfo, tm, tf, final):
    rows, d = x2d.shape
    dff = wfo.shape[0]
    nff = dff // tf
    rowblk = lambda w: pl.BlockSpec((tm, w), lambda i, j: (i, 0))
    const = lambda shape: pl.BlockSpec(shape, lambda i, j: (0,) * len(shape))
    in_specs = [rowblk(GDN_W), rowblk(ATT_W), rowblk(d),
                mod_spec(2), mod_spec(4), mod_spec(3), mod_spec(5),
                const((1, d)), const((1, d)), const((d, d)),
                pl.BlockSpec((d, tf), lambda i, j: (0, j)),
                pl.BlockSpec((d, tf), lambda i, j: (0, nff + j)),
                pl.BlockSpec((tf, d), lambda i, j: (j, 0))]
    return pl.pallas_call(
        functools.partial(_post_kernel, final, nff),
        grid=(rows // tm, nff), in_specs=in_specs, out_specs=rowblk(d),
        out_shape=jax.ShapeDtypeStruct((rows, d), F32),
        scratch_shapes=[pltpu.VMEM((tm, d), F32), pltpu.VMEM((tm, d), BF), pltpu.VMEM((tm, d), F32)],
        compiler_params=_params(("arbitrary", "arbitrary")),
        name="post",
    )(og, oa, x2d, mod, mod, mod, mod, g2.reshape(1, d), gf.reshape(1, d), wo, wfi, wfi, wfo)


def _pick(n, prefs):
    for p in prefs:
        if n % p == 0:
            return p
    return n


def kernel(x_prompt, x_sample, c_prompt, c_sample, cache_k, cache_v, cache_idx_k, page_table, state_conv, state_ssm,
           w_ada, b_ada, g_norm1, w_in, w_conv, a_log, dt_bias, g_gdn_norm, w_out, g_norm2, w_ffn_in, w_ffn_out,
           g_final):
    bp, tp, d = x_prompt.shape
    db, ts, _ = x_sample.shape
    assert ts == 1 and d == GDN_W + ATT_W and tp % max(ATT_TK, GDN_CHUNK) == 0
    depth = w_in.shape[0]
    npg, page = page_table.shape[1], cache_k.shape[2]
    past = npg * page
    tm_in = _pick(tp, (512, 256))
    tg = _pick(tp, (512, 256))
    tm_post = _pick(tp, (1024, 512, 256))
    dff = w_ffn_out.shape[1]
    tf = _pick(dff, (256, 128))

    xp = x_prompt
    xs = x_sample.reshape(db, d)
    npad = (-(bp + db)) % 8
    c_all = jnp.concatenate([c_prompt, c_sample, jnp.zeros((npad, d), F32)], axis=0)
    new_p, new_s = [], []
    for l in range(depth):
        final = l == depth - 1
        mod = _modulation(c_all, w_ada[l], b_ada[l])
        mod_p = mod[:bp].reshape(bp, 1, 6 * d)
        mod_s = mod[bp:bp + db]
        w_nat, w_t, prow, pcol = _inproj_weights(w_in[l], a_log[l], dt_bias[l])
        wo = w_out[l].astype(BF)
        wfi = w_ffn_in[l].astype(BF)
        wfo = w_ffn_out[l].astype(BF)

        (qkvn, z, vnat, knat, ikT, tail, misc, gct, qT, iqT, iwT, kbf, vT, ikbf) = _inproj_prompt(
            xp, mod_p, g_norm1[l], w_nat, w_t, w_conv[l], prow, pcol, tm_in)
        og, ssm_p = _gdn_prompt(qkvn, z, misc, gct, g_gdn_norm[l], bp, tp, tg)
        oa = _attn_prompt(qT, iqT, iwT, kbf, vT, ikbf, bp, tp)
        tiles_b = tp // tm_post
        spec_p = lambda k: pl.BlockSpec((None, 1, d), lambda i, j: (i // tiles_b, 0, k))
        xp = _post(og, oa, xp.reshape(bp * tp, d), mod_p, spec_p, g_norm2[l], g_final, wo, wfi, wfo,
                   tm_post, tf, final).reshape(bp, tp, d)
        new_p.append((knat.reshape(bp, tp, N_KV, HEAD_DIM), vnat.reshape(bp, tp, N_KV, HEAD_DIM),
                      jnp.swapaxes(ikT, 1, 2), tail[:, 8 - (CONV_W - 1):, :], ssm_p))

        (qkvn_s, z_s, vnat_s, knat_s, slab_s, raw_s, misc_s, aq_s, iq_s) = _inproj_sample(
            xs, mod_s, g_norm1[l], w_nat, w_t, w_conv[l], prow, pcol, state_conv[l], past)
        og_s, ssm_s = _gdn_sample(qkvn_s, z_s, misc_s, g_gdn_norm[l], state_ssm[l], _pick(db, (8,)))
        ik_s = slab_s[:, 0:IDX_DIM]
        iw_s = slab_s[:, IDX_DIM:IDX_DIM + IDX_HEADS]
        scores = _sample_index_scores(iq_s, iw_s, ik_s, jnp.swapaxes(cache_idx_k[l], 1, 2), page_table)
        keep = _sample_select(scores, past + ts)
        oa_s = _sample_attention(aq_s, keep, knat_s, vnat_s, cache_k[l], cache_v[l], page_table)
        spec_s = lambda k: pl.BlockSpec((db, d), lambda i, j: (i, k))
        xs = _post(og_s, oa_s, xs, mod_s, spec_s, g_norm2[l], g_final, wo, wfi, wfo, db, tf, final)
        new_conv_s = jnp.concatenate([state_conv[l][:, 1:, :], raw_s[:, None, :]], axis=1)
        new_s.append((knat_s.reshape(db, ts, N_KV, HEAD_DIM), vnat_s.reshape(db, ts, N_KV, HEAD_DIM),
                      ik_s.reshape(db, ts, IDX_DIM), new_conv_s, ssm_s))

    stack = lambda states, n: jnp.stack([s[n] for s in states], axis=0)
    return (xp, xs.reshape(db, ts, d),
            stack(new_p, 0), stack(new_p, 1), stack(new_p, 2), stack(new_p, 3), stack(new_p, 4),
            stack(new_s, 0), stack(new_s, 1), stack(new_s, 2), stack(new_s, 3), stack(new_s, 4))
```

```python
import functools
import math

import numpy as np
import jax
import jax.numpy as jnp
from jax import lax
from jax.experimental import pallas as pl
from jax.experimental.pallas import tpu as pltpu

F32 = jnp.float32
BF = jnp.bfloat16
I32 = jnp.int32

HEAD_DIM = 128
GDN_HEADS = 4
ATT_HEADS = 4
N_KV = 2
IDX_HEADS = 8
IDX_DIM = 64
CONV_W = 4
TOPK_MAX = 256
QBLOCK = 128
ROPE_THETA = 500000.0
EPS = 1e-6
GDN_W = GDN_HEADS * HEAD_DIM
ATT_W = ATT_HEADS * HEAD_DIM
KV_W = N_KV * HEAD_DIM
IDX_W = IDX_HEADS * IDX_DIM
QKV_W = 3 * GDN_W

GDN_CHUNK = 128
ATT_TK = 256
VT_ROWS = HEAD_DIM + 16
IDX_TK = 512
ATT_BLK = 1024
ATT_TQ = 256
LANES = 128
NAT_W = QKV_W + GDN_W + KV_W + LANES
T_AQ, T_AK, T_IQ, T_IK, T_MISC, T_AV = 0, 512, 768, 1280, 1344, 1360
T_ROWS = T_AV + KV_W
VMEM_LIMIT = 56 * 1024 * 1024
INT_MIN = np.int32(-2 ** 31)
NEG_INF = float("-inf")


def _params(sem):
    return pltpu.CompilerParams(dimension_semantics=sem, vmem_limit_bytes=VMEM_LIMIT)


def _dot(a, b):
    return jnp.dot(a, b, preferred_element_type=F32)


def _dot_nt(a, b):
    return lax.dot_general(a, b, (((1,), (1,)), ((), ())), preferred_element_type=F32)


def _split3(x):
    hi = x.astype(BF)
    r1 = x - hi.astype(F32)
    mid = r1.astype(BF)
    lo = (r1 - mid.astype(F32)).astype(BF)
    return hi, mid, lo


def _mm_hi(a, b):
    ah = a.astype(BF)
    al = (a - ah.astype(F32)).astype(BF)
    bh = b.astype(BF)
    bl = (b - bh.astype(F32)).astype(BF)
    m = a.shape[0]
    hi = _dot(jnp.concatenate([ah, al], axis=0), bh)
    return hi[0:m] + (hi[m:] + _dot(ah, bl))


def _sigmoid(x):
    return 1.0 / (1.0 + jnp.exp(-x))


def _silu(x):
    return x * _sigmoid(x)


def _softplus(x):
    return jnp.maximum(x, 0.0) + jnp.log(1.0 + jnp.exp(-jnp.abs(x)))


def _rmsnorm(x, g):
    return x * lax.rsqrt(jnp.mean(x * x, axis=-1, keepdims=True) + EPS) * g


def _mod_kernel(c_ref, w_ref, b_ref, o_ref):
    s = _silu(c_ref[...]).astype(BF)
    o_ref[...] = _dot(s, w_ref[...].astype(BF)) + b_ref[...]


def _modulation(c_all, w_ada, b_ada):
    n, d = c_all.shape
    cols = w_ada.shape[1]
    tn = d
    return pl.pallas_call(
        _mod_kernel,
        grid=(cols // tn,),
        in_specs=[pl.BlockSpec((n, d), lambda j: (0, 0)),
                  pl.BlockSpec((d, tn), lambda j: (0, j)),
                  pl.BlockSpec((1, tn), lambda j: (0, j))],
        out_specs=pl.BlockSpec((n, tn), lambda j: (0, j)),
        out_shape=jax.ShapeDtypeStruct((n, cols), F32),
        compiler_params=_params(("arbitrary",)),
        name="modulation",
    )(c_all, w_ada, b_ada.reshape(1, cols))


def _rope_rows(rt_ref, base, half, cos, sin):
    x1 = rt_ref[base:base + half, :]
    x2 = rt_ref[base + half:base + 2 * half, :]
    rt_ref[base:base + half, :] = x1 * cos - x2 * sin
    rt_ref[base + half:base + 2 * half, :] = x2 * cos + x1 * sin


def _project(x_ref, sc_ref, sh_ref, g1_ref, wn_ref, wt_ref, cs128_ref, cs64_ref, pcol_ref, rt_ref):
    h = _rmsnorm(x_ref[...], g1_ref[...]) * (1.0 + sc_ref[...]) + sh_ref[...]
    hb = h.astype(BF)
    nat = _dot(hb, wn_ref[...])
    rt_ref[...] = _dot_nt(wt_ref[...], hb)
    half = HEAD_DIM // 8
    cos, sin = cs128_ref[0:half, :], cs128_ref[half:2 * half, :]
    for hd in range(ATT_HEADS):
        _rope_rows(rt_ref, T_AQ + hd * HEAD_DIM, half, cos, sin)
    for hd in range(N_KV):
        _rope_rows(rt_ref, T_AK + hd * HEAD_DIM, half, cos, sin)
    half = IDX_DIM // 8
    cos, sin = cs64_ref[0:half, :], cs64_ref[half:2 * half, :]
    for hd in range(IDX_HEADS):
        _rope_rows(rt_ref, T_IQ + hd * IDX_DIM, half, cos, sin)
    _rope_rows(rt_ref, T_IK, half, cos, sin)
    mt = rt_ref[T_MISC:T_MISC + 16, :]
    r = lax.broadcasted_iota(I32, mt.shape, 0)
    a_log, dt_b = pcol_ref[:, 0:1], pcol_ref[:, 1:2]
    gate = -jnp.exp(a_log) * _softplus(mt + dt_b)
    mt = jnp.where(r < IDX_HEADS, mt * IDX_HEADS ** -0.5, jnp.where(r < IDX_HEADS + GDN_HEADS, _sigmoid(mt), gate))
    rt_ref[T_MISC:T_MISC + 16, :] = mt
    return nat, mt


def _misc_natural(m, prow_ref):
    lane = lax.broadcasted_iota(I32, m.shape, 1)
    gate = -jnp.exp(prow_ref[0:1, :]) * _softplus(m + prow_ref[1:2, :])
    return jnp.where(lane < GDN_HEADS, _sigmoid(m), jnp.where(lane < 3 * GDN_HEADS, gate, 0.0))


def _qkv_post(conv, qkvn_ref):
    c = _silu(conv)
    for hb in range(2 * GDN_HEADS):
        xh = c[:, hb * HEAD_DIM:(hb + 1) * HEAD_DIM]
        n = xh * lax.rsqrt(jnp.sum(xh * xh, axis=-1, keepdims=True) + EPS)
        if hb < GDN_HEADS:
            n = n * HEAD_DIM ** -0.5
        qkvn_ref[:, hb * HEAD_DIM:(hb + 1) * HEAD_DIM] = n
    qkvn_ref[:, 2 * GDN_W:] = c[:, 2 * GDN_W:]


def _inproj_prompt_kernel(tm, parts, x_ref, sc_ref, sh_ref, g1_ref, wn_ref, wt_ref, wconv_ref, cs128_ref, cs64_ref,
                          prow_ref, pcol_ref, tril_ref,
                          qkvn_ref, z_ref, vnat_ref, knat_ref, ikT_ref, tail_ref, misc_ref, gct_ref,
                          qT_ref, iqT_ref, iwT_ref, kbf_ref, vT_ref, ikbf_ref,
                          xs_ref, rt_ref):
    i = pl.program_id(1)

    @pl.when(i == 0)
    def _():
        xs_ref[0:8, :] = jnp.zeros((8, QKV_W), F32)

    @pl.when(i > 0)
    def _():
        xs_ref[0:8, :] = xs_ref[tm:tm + 8, :]

    th = tm // parts
    for n in range(parts):
        rows, cols = pl.ds(n * th, th), pl.ds(n * th, th)
        _inproj_prompt_rows(
            th, x_ref.at[rows, :], sc_ref, sh_ref, g1_ref, wn_ref, wt_ref, wconv_ref,
            cs128_ref.at[:, cols], cs64_ref.at[:, cols], prow_ref, pcol_ref, tril_ref.at[0:th, 0:th],
            qkvn_ref.at[rows, :], z_ref.at[rows, :], vnat_ref.at[pl.ds(n * th * N_KV, th * N_KV), :],
            knat_ref.at[pl.ds(n * th * N_KV, th * N_KV), :], ikT_ref.at[:, cols], tail_ref, misc_ref.at[rows, :],
            gct_ref.at[pl.ds(n * (th // GDN_CHUNK), th // GDN_CHUNK)],
            qT_ref.at[:, cols], iqT_ref.at[:, cols], iwT_ref.at[:, cols], kbf_ref.at[rows, :],
            vT_ref.at[pl.ds(n * (th // ATT_TK), th // ATT_TK)], ikbf_ref.at[rows, :],
            xs_ref.at[pl.ds(n * th, th + 8), :], rt_ref.at[:, cols])


def _inproj_prompt_rows(tm, x_ref, sc_ref, sh_ref, g1_ref, wn_ref, wt_ref, wconv_ref, cs128_ref, cs64_ref,
                        prow_ref, pcol_ref, tril_ref,
                        qkvn_ref, z_ref, vnat_ref, knat_ref, ikT_ref, tail_ref, misc_ref, gct_ref,
                        qT_ref, iqT_ref, iwT_ref, kbf_ref, vT_ref, ikbf_ref,
                        xs_ref, rt_ref):
    nat, mt = _project(x_ref, sc_ref, sh_ref, g1_ref, wn_ref, wt_ref, cs128_ref, cs64_ref, pcol_ref, rt_ref)
    xs_ref[8:tm + 8, :] = nat[:, 0:QKV_W]
    conv = wconv_ref[0:1, :] * xs_ref[5:tm + 5, :]
    for t in range(1, CONV_W):
        conv = conv + wconv_ref[t:t + 1, :] * xs_ref[5 + t:tm + 5 + t, :]
    tail_ref[...] = xs_ref[tm:tm + 8, :]
    _qkv_post(conv, qkvn_ref)
    z_ref[...] = nat[:, QKV_W:QKV_W + GDN_W]
    v0 = QKV_W + GDN_W
    for hd in range(N_KV):
        vnat_ref[pl.ds(hd, tm, stride=N_KV), :] = nat[:, v0 + hd * HEAD_DIM:v0 + (hd + 1) * HEAD_DIM]

    gm = _misc_natural(nat[:, NAT_W - LANES:], prow_ref)
    tril = tril_ref[...]
    hi, mid, lo = _split3(gm)
    gc = _dot(tril, hi) + (_dot(tril, mid) + _dot(tril, lo))
    lane = lax.broadcasted_iota(I32, gm.shape, 1)
    misc_ref[...] = jnp.where(lane < 2 * GDN_HEADS, gm, gc)
    hi, mid, lo = _split3(mt)
    gct = _dot_nt(hi, tril) + (_dot_nt(mid, tril) + _dot_nt(lo, tril))
    r = lax.broadcasted_iota(I32, mt.shape, 0)
    bg = jnp.where(r < IDX_HEADS + GDN_HEADS, mt, gct)[8:16, :]
    for cc in range(tm // GDN_CHUNK):
        gct_ref[cc] = bg[:, cc * GDN_CHUNK:(cc + 1) * GDN_CHUNK]

    iwT_ref[...] = mt[0:IDX_HEADS, :]
    qT_ref[...] = rt_ref[T_AQ:T_AQ + ATT_W, :].astype(BF)
    iqT_ref[...] = rt_ref[T_IQ:T_IQ + IDX_W, :].astype(BF)
    kn = rt_ref[T_AK:T_AK + KV_W, :].T
    for hd in range(N_KV):
        knat_ref[pl.ds(hd, tm, stride=N_KV), :] = kn[:, hd * HEAD_DIM:(hd + 1) * HEAD_DIM]
    kbf_ref[...] = kn.astype(BF)
    ikT_ref[...] = rt_ref[T_IK:T_IK + IDX_DIM, :]
    ikbf_ref[...] = rt_ref[T_IK:T_IK + LANES, :].T[:, 0:IDX_DIM].astype(BF)
    for cc in range(tm // ATT_TK):
        for hd in range(N_KV):
            lo = hd * VT_ROWS
            vT_ref[cc, lo:lo + HEAD_DIM, :] = rt_ref[T_AV + hd * HEAD_DIM:T_AV + (hd + 1) * HEAD_DIM,
                                                     cc * ATT_TK:(cc + 1) * ATT_TK].astype(BF)
            vT_ref[cc, lo + HEAD_DIM:lo + VT_ROWS, :] = jnp.ones((VT_ROWS - HEAD_DIM, ATT_TK), BF)


def _inproj_sample_kernel(x_ref, sc_ref, sh_ref, g1_ref, wn_ref, wt_ref, wconv_ref, cs128_ref, cs64_ref,
                          prow_ref, pcol_ref, s0_ref, s1_ref, s2_ref,
                          qkvn_ref, z_ref, vnat_ref, knat_ref, slab_ref, raw_ref, misc_ref, aq_ref, iq_ref,
                          rt_ref):
    nat, _ = _project(x_ref, sc_ref, sh_ref, g1_ref, wn_ref, wt_ref, cs128_ref, cs64_ref, pcol_ref, rt_ref)
    raw = nat[:, 0:QKV_W]
    raw_ref[...] = raw
    conv = wconv_ref[0:1, :] * s0_ref[...]
    conv = conv + wconv_ref[1:2, :] * s1_ref[...]
    conv = conv + wconv_ref[2:3, :] * s2_ref[...]
    conv = conv + wconv_ref[3:4, :] * raw
    _qkv_post(conv, qkvn_ref)
    z_ref[...] = nat[:, QKV_W:QKV_W + GDN_W]
    vnat_ref[...] = nat[:, QKV_W + GDN_W:QKV_W + GDN_W + KV_W]
    misc_ref[...] = _misc_natural(nat[:, NAT_W - LANES:], prow_ref)
    aq_ref[...] = rt_ref[T_AQ:T_AQ + ATT_W, :].T
    iq_ref[...] = rt_ref[T_IQ:T_IQ + IDX_W, :].T
    knat_ref[...] = rt_ref[T_AK:T_AK + KV_W, :].T
    slab_ref[...] = rt_ref[T_IK:T_IK + LANES, :].T


def _inproj_weights(w_in, a_log, dt_bias):
    offs = np.cumsum([0, QKV_W, GDN_W, GDN_HEADS, GDN_HEADS, ATT_W, KV_W, KV_W, IDX_W, IDX_DIM, IDX_HEADS])
    qkv, z, beta, a, aq, ak, av, iq, ik, iw = [w_in[:, offs[n]:offs[n + 1]] for n in range(10)]
    d = w_in.shape[0]
    pad = jnp.zeros((d, LANES - 3 * GDN_HEADS), w_in.dtype)
    w_nat = jnp.concatenate([qkv, z, av, beta, a, a, pad], axis=1).astype(BF)
    w_t = jnp.concatenate([aq, ak, iq, ik, iw, beta, a, av], axis=1).T.astype(BF)
    zrow = jnp.zeros((LANES - 3 * GDN_HEADS,), F32)
    prow = jnp.stack([jnp.concatenate([jnp.zeros((GDN_HEADS,), F32), a_log, a_log, zrow]),
                      jnp.concatenate([jnp.zeros((GDN_HEADS,), F32), dt_bias, dt_bias, zrow])])
    z12 = jnp.zeros((IDX_HEADS + GDN_HEADS,), F32)
    pcol = jnp.stack([jnp.concatenate([z12, a_log]), jnp.concatenate([z12, dt_bias])], axis=1)
    return w_nat, w_t, prow, pcol


def _rope_tables(pos, dim):
    half = dim // 8
    inv = ROPE_THETA ** (-jnp.arange(half, dtype=F32) / half)
    ang = pos.astype(F32)[:, None] * inv[None, :]
    return jnp.concatenate([jnp.cos(ang).T, jnp.sin(ang).T], axis=0)


def _inproj_prompt(x, mod, g1, w_nat, w_t, w_conv, prow, pcol, tm):
    bp, tp, d = x.shape
    nt = tp // tm
    rows = bp * tp
    pos = jnp.arange(tp, dtype=I32)
    cs128, cs64 = _rope_tables(pos, HEAD_DIM), _rope_tables(pos, IDX_DIM)
    ri = np.arange(tm)
    tril = jnp.asarray((ri[:, None] // GDN_CHUNK == ri[None, :] // GDN_CHUNK) & (ri[None, :] <= ri[:, None]), BF)
    const = lambda shape: pl.BlockSpec(shape, lambda b, i: (0,) * len(shape))
    rowblk = lambda w: pl.BlockSpec((tm, w), lambda b, i: (b * nt + i, 0))
    colblk = lambda h: pl.BlockSpec((h, tm), lambda b, i: (0, b * nt + i))
    in_specs = [
        pl.BlockSpec((None, tm, d), lambda b, i: (b, i, 0)),
        pl.BlockSpec((None, 1, d), lambda b, i: (b, 0, 1)),
        pl.BlockSpec((None, 1, d), lambda b, i: (b, 0, 0)),
        const((1, d)), const((d, NAT_W)), const((T_ROWS, d)), const((CONV_W, QKV_W)),
        pl.BlockSpec((HEAD_DIM // 4, tm), lambda b, i: (0, i)),
        pl.BlockSpec((IDX_DIM // 4, tm), lambda b, i: (0, i)),
        const((2, LANES)), const((16, 2)), const((tm, tm)),
    ]
    out_shape = [
        jax.ShapeDtypeStruct((rows, QKV_W), F32), jax.ShapeDtypeStruct((rows, GDN_W), F32),
        jax.ShapeDtypeStruct((rows * N_KV, HEAD_DIM), F32), jax.ShapeDtypeStruct((rows * N_KV, HEAD_DIM), F32),
        jax.ShapeDtypeStruct((bp, IDX_DIM, tp), F32), jax.ShapeDtypeStruct((bp, 8, QKV_W), F32),
        jax.ShapeDtypeStruct((rows, LANES), F32), jax.ShapeDtypeStruct((rows // GDN_CHUNK, 8, GDN_CHUNK), F32),
        jax.ShapeDtypeStruct((ATT_W, rows), BF), jax.ShapeDtypeStruct((IDX_W, rows), BF),
        jax.ShapeDtypeStruct((IDX_HEADS, rows), F32), jax.ShapeDtypeStruct((rows, KV_W), BF),
        jax.ShapeDtypeStruct((rows // ATT_TK, N_KV * VT_ROWS, ATT_TK), BF), jax.ShapeDtypeStruct((rows, IDX_DIM), BF),
    ]
    out_specs = [
        rowblk(QKV_W), rowblk(GDN_W),
        pl.BlockSpec((tm * N_KV, HEAD_DIM), lambda b, i: (b * nt + i, 0)),
        pl.BlockSpec((tm * N_KV, HEAD_DIM), lambda b, i: (b * nt + i, 0)),
        pl.BlockSpec((None, IDX_DIM, tm), lambda b, i: (b, 0, i)),
        pl.BlockSpec((None, 8, QKV_W), lambda b, i: (b, 0, 0)),
        rowblk(LANES),
        pl.BlockSpec((tm // GDN_CHUNK, 8, GDN_CHUNK), lambda b, i: (b * nt + i, 0, 0)),
        colblk(ATT_W), colblk(IDX_W), colblk(IDX_HEADS), rowblk(KV_W),
        pl.BlockSpec((tm // ATT_TK, N_KV * VT_ROWS, ATT_TK), lambda b, i: (b * nt + i, 0, 0)),
        rowblk(IDX_DIM),
    ]
    return pl.pallas_call(
        functools.partial(_inproj_prompt_kernel, tm, tm // _pick(tm, (ATT_TK,))),
        grid=(bp, nt), in_specs=in_specs, out_specs=out_specs, out_shape=out_shape,
        scratch_shapes=[pltpu.VMEM((tm + 8, QKV_W), F32), pltpu.VMEM((T_ROWS, tm), F32)],
        compiler_params=_params(("arbitrary", "arbitrary")),
        name="inproj_prompt",
    )(x, mod, mod, g1.reshape(1, d), w_nat, w_t, w_conv, cs128, cs64, prow, pcol, tril)


def _inproj_sample(x, mod, g1, w_nat, w_t, w_conv, prow, pcol, conv_state, past):
    db, d = x.shape
    pos = jnp.full((db,), past, I32)
    cs128, cs64 = _rope_tables(pos, HEAD_DIM), _rope_tables(pos, IDX_DIM)
    full = lambda shape: pl.BlockSpec(shape, lambda i: (0,) * len(shape))
    in_specs = [
        full((db, d)),
        pl.BlockSpec((db, d), lambda i: (0, 1)), pl.BlockSpec((db, d), lambda i: (0, 0)),
        full((1, d)), full((d, NAT_W)), full((T_ROWS, d)), full((CONV_W, QKV_W)),
        full((HEAD_DIM // 4, db)), full((IDX_DIM // 4, db)), full((2, LANES)), full((16, 2)),
        full((db, QKV_W)), full((db, QKV_W)), full((db, QKV_W)),
    ]
    widths = [QKV_W, GDN_W, KV_W, KV_W, LANES, QKV_W, LANES, ATT_W, IDX_W]
    return pl.pallas_call(
        _inproj_sample_kernel,
        grid=(1,), in_specs=in_specs,
        out_specs=[full((db, w)) for w in widths],
        out_shape=[jax.ShapeDtypeStruct((db, w), F32) for w in widths],
        scratch_shapes=[pltpu.VMEM((T_ROWS, db), F32)],
        compiler_params=_params(("arbitrary",)),
        name="inproj_sample",
    )(x, mod, mod, g1.reshape(1, d), w_nat, w_t, w_conv, cs128, cs64, prow, pcol,
      conv_state[:, 0], conv_state[:, 1], conv_state[:, 2])


def _unit_lower_inverses(lows):
    c = lows[0].shape[0]
    ii = lax.broadcasted_iota(I32, (c, c), 0)
    jj = lax.broadcasted_iota(I32, (c, c), 1)
    eye = jnp.where(ii == jj, 1.0, 0.0)
    levels = int(math.log2(c)) - 1
    ts = [eye - low for low in lows]
    ps = [_mm_hi(low, low) for low in lows]
    for lvl in range(levels):
        if lvl == levels - 1:
            ts = [t + _mm_hi(t, p) for t, p in zip(ts, ps)]
        else:
            both = [_mm_hi(jnp.concatenate([t, p], axis=0), p) for t, p in zip(ts, ps)]
            ts = [t + b[0:c] for t, b in zip(ts, both)]
            ps = [b[c:] for b in both]
    return ts


def _gdn_prompt_kernel(tg, qkvn_ref, z_ref, misc_ref, gct_ref, gn_ref, o_ref, ssm_ref, s_ref):
    i = pl.program_id(1)

    @pl.when(i == 0)
    def _():
        s_ref[...] = jnp.zeros(s_ref.shape, F32)

    c = GDN_CHUNK
    ii = lax.broadcasted_iota(I32, (c, c), 0)
    jj = lax.broadcasted_iota(I32, (c, c), 1)
    pairs = [(cc, hd) for cc in range(tg // c) for hd in range(GDN_HEADS)]

    qs, ks, gcs, rhs, lows, intras = {}, {}, {}, {}, [], {}
    for cc, hd in pairs:
        r0, lo = cc * c, hd * HEAD_DIM
        q = qkvn_ref[r0:r0 + c, lo:lo + HEAD_DIM]
        k = qkvn_ref[r0:r0 + c, GDN_W + lo:GDN_W + lo + HEAD_DIM]
        v = qkvn_ref[r0:r0 + c, 2 * GDN_W + lo:2 * GDN_W + lo + HEAD_DIM]
        beta_c = misc_ref[r0:r0 + c, hd:hd + 1]
        gc_c = misc_ref[r0:r0 + c, 2 * GDN_HEADS + hd:2 * GDN_HEADS + hd + 1]
        gc_r = gct_ref[cc, GDN_HEADS + hd:GDN_HEADS + hd + 1, :]
        decay = jnp.where(ii >= jj, jnp.exp(jnp.where(ii >= jj, gc_c - gc_r, 0.0)), 0.0)
        kb = k * beta_c
        both = _dot_nt(jnp.concatenate([kb, q], axis=0).astype(BF), k.astype(BF))
        lows.append(jnp.where(ii > jj, both[0:c] * decay, 0.0))
        intras[cc, hd] = (both[c:] * decay).astype(BF)
        rhs[cc, hd] = jnp.concatenate([v * beta_c, kb * jnp.exp(gc_c)], axis=1).astype(BF)
        qs[cc, hd], ks[cc, hd], gcs[cc, hd] = q, k, gc_c
    ts = _unit_lower_inverses(lows)
    uws = {p: _dot(t.astype(BF), rhs[p]) for p, t in zip(pairs, ts)}

    for cc, hd in pairs:
        r0, lo = cc * c, hd * HEAD_DIM
        q, k, gc_c, uw = qs[cc, hd], ks[cc, hd], gcs[cc, hd], uws[cc, hd]
        s = s_ref[hd]
        ws = _dot(jnp.concatenate([uw[:, HEAD_DIM:], q * jnp.exp(gc_c)], axis=0).astype(BF), s.astype(BF))
        v_new = (uw[:, 0:HEAD_DIM] - ws[0:c]).astype(BF)
        o = ws[c:] + _dot(intras[cc, hd], v_new)
        g_last = gc_c[c - 1:c, :]
        kd = k * jnp.exp(g_last - gc_c)
        s_ref[hd] = s * jnp.exp(g_last) + _dot(kd.T.astype(BF), v_new)
        o = _rmsnorm(o, gn_ref[...]) * _silu(z_ref[r0:r0 + c, lo:lo + HEAD_DIM])
        o_ref[r0:r0 + c, lo:lo + HEAD_DIM] = o.astype(BF)
    ssm_ref[...] = s_ref[...]


def _gdn_prompt(qkvn, z, misc, gct, g_norm, bp, tp, tg):
    nt = tp // tg
    rows = bp * tp
    rowblk = lambda w: pl.BlockSpec((tg, w), lambda b, i: (b * nt + i, 0))
    return pl.pallas_call(
        functools.partial(_gdn_prompt_kernel, tg),
        grid=(bp, nt),
        in_specs=[rowblk(QKV_W), rowblk(GDN_W), rowblk(LANES),
                  pl.BlockSpec((tg // GDN_CHUNK, 8, GDN_CHUNK), lambda b, i: (b * nt + i, 0, 0)),
                  pl.BlockSpec((1, HEAD_DIM), lambda b, i: (0, 0))],
        out_specs=[rowblk(GDN_W),
                   pl.BlockSpec((None, GDN_HEADS, HEAD_DIM, HEAD_DIM), lambda b, i: (b, 0, 0, 0))],
        out_shape=[jax.ShapeDtypeStruct((rows, GDN_W), BF),
                   jax.ShapeDtypeStruct((bp, GDN_HEADS, HEAD_DIM, HEAD_DIM), F32)],
        scratch_shapes=[pltpu.VMEM((GDN_HEADS, HEAD_DIM, HEAD_DIM), F32)],
        compiler_params=_params(("arbitrary", "arbitrary")),
        name="gdn_prompt",
    )(qkvn, z, misc, gct, g_norm.reshape(1, HEAD_DIM))


def _gdn_sample_kernel(nb, qkvn_ref, z_ref, misc_ref, gn_ref, s_ref, o_ref, ssm_ref):
    for bi in range(nb):
        for hd in range(GDN_HEADS):
            lo = hd * HEAD_DIM
            q = qkvn_ref[bi:bi + 1, lo:lo + HEAD_DIM]
            k = qkvn_ref[bi:bi + 1, GDN_W + lo:GDN_W + lo + HEAD_DIM]
            v = qkvn_ref[bi:bi + 1, 2 * GDN_W + lo:2 * GDN_W + lo + HEAD_DIM]
            beta = misc_ref[bi:bi + 1, hd:hd + 1]
            g = misc_ref[bi:bi + 1, GDN_HEADS + hd:GDN_HEADS + hd + 1]
            kcol = jnp.broadcast_to(k, (HEAD_DIM, HEAD_DIM)).T
            qcol = jnp.broadcast_to(q, (HEAD_DIM, HEAD_DIM)).T
            s = s_ref[bi, hd] * jnp.exp(g)
            ks = jnp.sum(kcol * s, axis=0, keepdims=True)
            delta = (v - ks) * beta
            s = s + kcol * delta
            ssm_ref[bi, hd] = s
            o = jnp.sum(qcol * s, axis=0, keepdims=True)
            o = _rmsnorm(o, gn_ref[...]) * _silu(z_ref[bi:bi + 1, lo:lo + HEAD_DIM])
            o_ref[bi:bi + 1, lo:lo + HEAD_DIM] = o.astype(BF)


def _gdn_sample(qkvn, z, misc, g_norm, state, nb):
    db = qkvn.shape[0]
    rowblk = lambda w: pl.BlockSpec((nb, w), lambda i: (i, 0))
    sblk = pl.BlockSpec((nb, GDN_HEADS, HEAD_DIM, HEAD_DIM), lambda i: (i, 0, 0, 0))
    return pl.pallas_call(
        functools.partial(_gdn_sample_kernel, nb),
        grid=(db // nb,),
        in_specs=[rowblk(QKV_W), rowblk(GDN_W), rowblk(LANES), pl.BlockSpec((1, HEAD_DIM), lambda i: (0, 0)), sblk],
        out_specs=[rowblk(GDN_W), sblk],
        out_shape=[jax.ShapeDtypeStruct((db, GDN_W), BF), jax.ShapeDtypeStruct(state.shape, F32)],
        compiler_params=_params(("arbitrary",)),
        name="gdn_sample",
    )(qkvn, z, misc, g_norm.reshape(1, HEAD_DIM), state)


def _ordered_word_to_float(u):
    s = u ^ INT_MIN
    return lax.bitcast_convert_type(s ^ ((s >> 31) & np.int32(0x7FFFFFFF)), F32)


def _select_topk_bias(i_ref, t_ref, nch, tk, k_sel, idx_bits, taken=0.0, dropped=NEG_INF):
    nl = i_ref.shape[1]
    row = lax.broadcasted_iota(I32, (tk, 1), 0)

    def chunk(c):
        return pl.ds(pl.multiple_of(c * tk, tk), tk)

    def count(ref, pred):
        def body(c, acc):
            m = pred(ref[chunk(c), :])
            ones = jnp.where(m, 1, 0).astype(I32).reshape(tk // 8, 8, nl)
            parts = [ones[n] for n in range(tk // 8)]
            while len(parts) > 1:
                parts = [a + b for a, b in zip(parts[0::2], parts[1::2])] + parts[len(parts) & ~1:]
            return acc + parts[0]
        acc = lax.fori_loop(0, nch, body, jnp.zeros((8, nl), I32))
        return jnp.sum(acc, axis=0, keepdims=True)

    def bit_body(i, carry):
        t_u, n_ge = carry
        cand_u = t_u | jnp.left_shift(jnp.int32(1), 31 - i)
        cand = _ordered_word_to_float(cand_u)
        cnt = count(i_ref, lambda sc: sc >= cand)
        take = cnt >= k_sel
        return jnp.where(take, cand_u, t_u), jnp.where(take, cnt, n_ge)

    t_u, n_ge = lax.fori_loop(0, 32, bit_body, (jnp.zeros((1, nl), I32), jnp.full((1, nl), -1, I32)))
    thr = jnp.where((t_u >> 23) == 0, NEG_INF, _ordered_word_to_float(t_u))

    def tie_break(_):
        need = k_sel - count(i_ref, lambda sc: sc > thr)

        def mark(c, carry):
            t_ref[chunk(c), :] = jnp.where(i_ref[chunk(c), :] == thr, c * tk + row, 2 ** idx_bits)
            return carry

        lax.fori_loop(0, nch, mark, 0)

        def idx_body(i, x):
            cand = x | jnp.left_shift(jnp.int32(1), idx_bits - 1 - i)
            cnt = count(t_ref, lambda idx: idx < cand)
            return jnp.where(cnt < need, cand, x)

        return lax.fori_loop(0, idx_bits, idx_body, jnp.zeros((1, nl), I32))

    x = lax.cond(jnp.max(n_ge) > k_sel, tie_break, lambda _: jnp.full((1, nl), 2 ** idx_bits - 1, I32), 0)

    def write(c, carry):
        sc = i_ref[chunk(c), :]
        sel = (sc > thr) | ((sc == thr) & (c * tk + row <= x))
        i_ref[chunk(c), :] = jnp.where(sel & (sc > NEG_INF), taken, dropped)
        return carry

    lax.fori_loop(0, nch, write, 0)


def _attn_prompt_kernel(tq, tk, blk, k_sel, idx_bits, qT_ref, iqT_ref, iwT_ref, kbf_ref, vT_ref, ikbf_ref, o_ref,
                        i_ref, t_ref):
    t0 = pl.program_id(1) * tq
    nch = (t0 + tq + tk - 1) // tk
    nblk = (t0 + tq + blk - 1) // blk
    qpos = t0 + lax.broadcasted_iota(I32, (1, tq), 1)
    row = lax.broadcasted_iota(I32, (tk, 1), 0)
    w = iwT_ref[...]
    pairs = IDX_HEADS // 2
    rhs = [jnp.concatenate([iqT_ref[(2 * p) * IDX_DIM:(2 * p + 1) * IDX_DIM, :],
                            iqT_ref[(2 * p + 1) * IDX_DIM:(2 * p + 2) * IDX_DIM, :]], axis=1) for p in range(pairs)]

    def chunk(c):
        return pl.ds(pl.multiple_of(c * tk, tk), tk)

    def idx_body(c, carry):
        ikc = ikbf_ref[chunk(c), :]
        acc = jnp.zeros((tk, tq), F32)
        for p in range(pairs):
            d = jnp.maximum(_dot(ikc, rhs[p]), 0.0)
            acc = acc + d[:, 0:tq] * w[2 * p:2 * p + 1, :] + d[:, tq:] * w[2 * p + 1:2 * p + 2, :]
        i_ref[chunk(c), :] = jnp.where(c * tk + row <= qpos, acc, NEG_INF)
        return carry

    lax.fori_loop(0, nch, idx_body, 0)

    def fill_body(c, carry):
        i_ref[chunk(c), :] = jnp.full((tk, tq), NEG_INF, F32)
        return carry

    lax.fori_loop(nch, nblk * (blk // tk), fill_body, 0)
    tks = max(tk * LANES // tq, 8)
    _select_topk_bias(i_ref, t_ref, (t0 + tq + tks - 1) // tks, tks, k_sel, idx_bits)

    scale2 = HEAD_DIM ** -0.5 * math.log2(math.e)
    rep = ATT_HEADS // N_KV
    sub = ATT_TK
    nsub = blk // sub
    qg = [jnp.concatenate([qT_ref[(rep * g + r) * HEAD_DIM:(rep * g + r + 1) * HEAD_DIM, :] for r in range(rep)],
                          axis=1) for g in range(N_KV)]

    def body(c, carry):
        out = []
        for g in range(N_KV):
            m, acc = carry[g]
            ss = []
            for i in range(nsub):
                rows = pl.ds(pl.multiple_of(c * blk + i * sub, sub), sub)
                bias = i_ref[rows, :]
                ss.append(_dot(kbf_ref[rows, g * HEAD_DIM:(g + 1) * HEAD_DIM], qg[g]) * scale2
                          + jnp.concatenate([bias] * rep, axis=1))
            m_new = m
            for s in ss:
                m_new = jnp.maximum(m_new, jnp.max(s, axis=0, keepdims=True))
            m_safe = jnp.where(m_new == NEG_INF, 0.0, m_new)
            acc = acc * jnp.exp2(m - m_safe)
            for i, s in enumerate(ss):
                p = jnp.exp2(s - m_safe).astype(BF)
                acc = acc + _dot(vT_ref[c * nsub + i, g * VT_ROWS:(g + 1) * VT_ROWS, :], p)
            out.append((m_new, acc))
        return tuple(out)

    init = tuple((jnp.full((1, rep * tq), NEG_INF, F32), jnp.zeros((VT_ROWS, rep * tq), F32)) for _ in range(N_KV))
    res = lax.fori_loop(0, nblk, body, init)
    for g in range(N_KV):
        acc = res[g][1]
        o_t = acc[0:HEAD_DIM] / acc[HEAD_DIM:HEAD_DIM + 1]
        for r in range(rep):
            hd = rep * g + r
            o_ref[:, hd * HEAD_DIM:(hd + 1) * HEAD_DIM] = o_t[:, r * tq:(r + 1) * tq].T.astype(BF)


def _attn_prompt(qT, iqT, iwT, kbf, vT, ikbf, bp, tp):
    tq = _pick(tp, (ATT_TQ, QBLOCK))
    tk = _pick(tp, (IDX_TK, ATT_TK))
    blk = _pick(tp, (ATT_BLK, IDX_TK, ATT_TK))
    assert blk % tk == 0 and blk % ATT_TK == 0
    nq = tp // tq
    rows = bp * tp
    k_sel = min(TOPK_MAX, tp // 4)
    idx_bits = max(1, int(tp - 1).bit_length())
    colblk = lambda h: pl.BlockSpec((h, tq), lambda b, j: (0, b * nq + j))
    return pl.pallas_call(
        functools.partial(_attn_prompt_kernel, tq, tk, blk, k_sel, idx_bits),
        grid=(bp, nq),
        in_specs=[colblk(ATT_W), colblk(IDX_W), colblk(IDX_HEADS),
                  pl.BlockSpec((tp, KV_W), lambda b, j: (b, 0)),
                  pl.BlockSpec((tp // ATT_TK, N_KV * VT_ROWS, ATT_TK), lambda b, j: (b, 0, 0)),
                  pl.BlockSpec((tp, IDX_DIM), lambda b, j: (b, 0))],
        out_specs=pl.BlockSpec((tq, ATT_W), lambda b, j: (b * nq + j, 0)),
        out_shape=jax.ShapeDtypeStruct((rows, ATT_W), BF),
        scratch_shapes=[pltpu.VMEM((tp, tq), F32), pltpu.VMEM((tp, tq), I32)],
        compiler_params=_params(("arbitrary", "arbitrary")),
        name="attn_prompt",
    )(qT, iqT, iwT, kbf, vT, ikbf)


def _page_prefetch(npg, pt_ref, srcs, bufs, sem_ref):
    b = pl.program_id(0)
    slot = lax.rem(b, 2)

    def copy(n, step, sl, p):
        return pltpu.make_async_copy(srcs[n].at[pt_ref[step, p]], bufs[n].at[sl, p], sem_ref.at[n, sl, p])

    def start_all(step, sl):
        for n in range(len(srcs)):
            for p in range(npg):
                copy(n, step, sl, p).start()

    @pl.when(b == 0)
    def _():
        start_all(0, 0)

    @pl.when(b + 1 < pl.num_programs(0))
    def _():
        start_all(b + 1, 1 - slot)

    for n in range(len(srcs)):
        for p in range(npg):
            copy(n, b, slot, p).wait()
    return slot


def _sidx_kernel(npg, pt_ref, iq_ref, iw_ref, ikn_ref, cache_ref, out_ref, buf_ref, sem_ref):
    slot = _page_prefetch(npg, pt_ref, [cache_ref], [buf_ref], sem_ref)
    iq = iq_ref[...].astype(BF)
    w = iw_ref[...]
    for p in range(npg):
        d = jnp.maximum(_dot(iq, buf_ref[slot, p].astype(BF)), 0.0)
        out_ref[:, p * LANES:(p + 1) * LANES] = jnp.sum(d * w, axis=0, keepdims=True)
    dn = jnp.sum(iq.astype(F32) * ikn_ref[...].astype(BF).astype(F32), axis=1, keepdims=True)
    sn = jnp.sum(jnp.maximum(dn, 0.0) * w, axis=0, keepdims=True)
    lane = lax.broadcasted_iota(I32, (1, LANES), 1)
    out_ref[:, npg * LANES:(npg + 1) * LANES] = jnp.where(lane == 0, sn, NEG_INF)


def _sample_index_scores(iq, iw, ik_new, cache_ik_t, page_table):
    db, npg = page_table.shape
    page = cache_ik_t.shape[2]
    assert page == LANES
    width = (npg + 1) * LANES
    grid_spec = pltpu.PrefetchScalarGridSpec(
        num_scalar_prefetch=1, grid=(db,),
        in_specs=[pl.BlockSpec((None, IDX_HEADS, IDX_DIM), lambda b, pt: (b, 0, 0)),
                  pl.BlockSpec((None, IDX_HEADS, 1), lambda b, pt: (b, 0, 0)),
                  pl.BlockSpec((None, 1, IDX_DIM), lambda b, pt: (b, 0, 0)),
                  pl.BlockSpec(memory_space=pl.ANY)],
        out_specs=pl.BlockSpec((None, 1, width), lambda b, pt: (b, 0, 0)),
        scratch_shapes=[pltpu.VMEM((2, npg, IDX_DIM, page), F32), pltpu.SemaphoreType.DMA((1, 2, npg))])
    out = pl.pallas_call(
        functools.partial(_sidx_kernel, npg), grid_spec=grid_spec,
        out_shape=jax.ShapeDtypeStruct((db, 1, width), F32),
        compiler_params=_params(("arbitrary",)),
        name="sample_index_scores",
    )(page_table, iq.reshape(db, IDX_HEADS, IDX_DIM), iw.reshape(db, IDX_HEADS, 1), ik_new.reshape(db, 1, IDX_DIM),
      cache_ik_t)
    return out.reshape(db, width)


def _ssel_kernel(nch, k_sel, idx_bits, s_ref, spread_ref, keep_ref, i_ref, t_ref):
    for p in range(nch):
        i_ref[p * LANES:(p + 1) * LANES, :] = s_ref[:, p * LANES:(p + 1) * LANES].T
    _select_topk_bias(i_ref, t_ref, nch, LANES, k_sel, idx_bits, taken=1.0, dropped=0.0)
    rows = N_KV * LANES
    for p in range(nch - 1):
        k2 = _dot(spread_ref[...], i_ref[p * LANES:(p + 1) * LANES, :].astype(BF))
        for h in range(N_KV):
            keep_ref[:, p * rows + h * LANES:p * rows + (h + 1) * LANES] = k2[h * LANES:(h + 1) * LANES, :].T
    keep_ref[:, (nch - 1) * rows:(nch - 1) * rows + LANES] = i_ref[(nch - 1) * LANES:nch * LANES, :].T


def _sample_select(scores, n_keys):
    db, width = scores.shape
    assert db == LANES
    nch = width // LANES
    k_sel = min(TOPK_MAX, n_keys // 4)
    idx_bits = max(1, int(width - 1).bit_length())
    rows = N_KV * LANES
    out_w = (nch - 1) * rows + LANES
    spread = jnp.asarray(np.arange(rows)[:, None] // N_KV == np.arange(LANES)[None, :], BF)
    return pl.pallas_call(
        functools.partial(_ssel_kernel, nch, k_sel, idx_bits),
        grid=(1,),
        in_specs=[pl.BlockSpec((db, width), lambda i: (0, 0)), pl.BlockSpec((rows, LANES), lambda i: (0, 0))],
        out_specs=pl.BlockSpec((db, out_w), lambda i: (0, 0)),
        out_shape=jax.ShapeDtypeStruct((db, out_w), F32),
        scratch_shapes=[pltpu.VMEM((width, db), F32), pltpu.VMEM((width, db), I32)],
        compiler_params=_params(("arbitrary",)),
        name="sample_select",
    )(scores, spread)


def _sattn_kernel(npg, pt_ref, q_ref, keep_ref, kn_ref, vn_ref, ck_ref, cv_ref, o_ref, kbuf_ref, vbuf_ref, sem_ref):
    slot = _page_prefetch(npg, pt_ref, [ck_ref, cv_ref], [kbuf_ref, vbuf_ref], sem_ref)
    rep = ATT_HEADS // N_KV
    rows = N_KV * LANES
    scale = HEAD_DIM ** -0.5
    q = q_ref[...]
    q8 = jnp.concatenate([q, jnp.zeros((8 - ATT_HEADS, HEAD_DIM), F32)], axis=0).astype(BF)
    hrow = lax.broadcasted_iota(I32, (8, 1), 0)
    lane = lax.broadcasted_iota(I32, (1, LANES), 1)
    own_kv = lax.broadcasted_iota(I32, (1, rows), 1) % N_KV == hrow // rep
    parts = []
    for p in range(npg):
        s2 = _dot_nt(q8, kbuf_ref[slot, p].astype(BF))
        keep2 = keep_ref[:, p * rows:(p + 1) * rows]
        parts.append(jnp.where(own_kv & (keep2 > 0.5), s2 * scale, NEG_INF))
    kn = kn_ref[...].astype(BF).astype(F32)
    kn8 = jnp.where(hrow // rep == 0, kn[0:1, :], kn[1:2, :])
    s_new = jnp.sum(q8.astype(F32) * kn8, axis=1, keepdims=True)
    keep_new = keep_ref[:, npg * rows:npg * rows + 1]
    parts.append(jnp.where((lane == 0) & (keep_new > 0.5), s_new * scale, NEG_INF))
    s = jnp.concatenate(parts, axis=1)
    m = jnp.max(s, axis=1, keepdims=True)
    e = jnp.exp(s - m)
    pr = (e / jnp.sum(e, axis=1, keepdims=True)).astype(BF)
    o8 = jnp.zeros((8, HEAD_DIM), F32)
    for p in range(npg):
        o8 = o8 + _dot(pr[:, p * rows:(p + 1) * rows], vbuf_ref[slot, p].astype(BF))
    vn = vn_ref[...].astype(BF).astype(F32)
    vn8 = jnp.where(hrow // rep == 0, vn[0:1, :], vn[1:2, :])
    o8 = o8 + pr[:, npg * rows:npg * rows + 1].astype(F32) * vn8
    for hd in range(ATT_HEADS):
        o_ref[:, hd * HEAD_DIM:(hd + 1) * HEAD_DIM] = o8[hd:hd + 1, :].astype(BF)


def _sample_attention(q, keep, k_new, v_new, cache_k, cache_v, page_table):
    db, npg = page_table.shape
    n_phys, page = cache_k.shape[0], cache_k.shape[1]
    assert page == LANES and N_KV == 2
    rows = page * N_KV
    width = npg * rows + LANES
    ck = cache_k.reshape(n_phys, rows, HEAD_DIM)
    cv = cache_v.reshape(n_phys, rows, HEAD_DIM)
    grid_spec = pltpu.PrefetchScalarGridSpec(
        num_scalar_prefetch=1, grid=(db,),
        in_specs=[pl.BlockSpec((None, ATT_HEADS, HEAD_DIM), lambda b, pt: (b, 0, 0)),
                  pl.BlockSpec((None, 1, width), lambda b, pt: (b, 0, 0)),
                  pl.BlockSpec((None, N_KV, HEAD_DIM), lambda b, pt: (b, 0, 0)),
                  pl.BlockSpec((None, N_KV, HEAD_DIM), lambda b, pt: (b, 0, 0)),
                  pl.BlockSpec(memory_space=pl.ANY), pl.BlockSpec(memory_space=pl.ANY)],
        out_specs=pl.BlockSpec((None, 1, ATT_W), lambda b, pt: (b, 0, 0)),
        scratch_shapes=[pltpu.VMEM((2, npg, rows, HEAD_DIM), F32), pltpu.VMEM((2, npg, rows, HEAD_DIM), F32),
                        pltpu.SemaphoreType.DMA((2, 2, npg))])
    out = pl.pallas_call(
        functools.partial(_sattn_kernel, npg), grid_spec=grid_spec,
        out_shape=jax.ShapeDtypeStruct((db, 1, ATT_W), BF),
        compiler_params=_params(("arbitrary",)),
        name="sample_attention",
    )(page_table, q.reshape(db, ATT_HEADS, HEAD_DIM), keep.reshape(db, 1, width),
      k_new.reshape(db, N_KV, HEAD_DIM), v_new.reshape(db, N_KV, HEAD_DIM), ck, cv)
    return out.reshape(db, ATT_W)


def _post_kernel(final, nff, og_ref, oa_ref, x_ref, ga1_ref, sc2_ref, sh2_ref, ga2_ref, g2_ref, gf_ref,
                 wo_ref, wg_ref, wu_ref, wd_ref, y_ref, x1_ref, h2_ref, acc_ref):
    jf = pl.program_id(1)

    @pl.when(jf == 0)
    def _():
        mixed = _dot(og_ref[...], wo_ref[0:GDN_W, :]) + _dot(oa_ref[...], wo_ref[GDN_W:, :])
        x1 = x_ref[...] + ga1_ref[...] * mixed
        x1_ref[...] = x1
        h2_ref[...] = (_rmsnorm(x1, g2_ref[...]) * (1.0 + sc2_ref[...]) + sh2_ref[...]).astype(BF)
        acc_ref[...] = jnp.zeros(acc_ref.shape, F32)

    h2 = h2_ref[...]
    act = (_silu(_dot(h2, wg_ref[...])) * _dot(h2, wu_ref[...])).astype(BF)
    acc_ref[...] += _dot(act, wd_ref[...])

    @pl.when(jf == nff - 1)
    def _():
        x2 = x1_ref[...] + ga2_ref[...] * acc_ref[...]
        y_ref[...] = _rmsnorm(x2, gf_ref[...]) if final else x2


def _post(og, oa, x2d, mod, mod_spec, g2, gf, wo, wfi, wfo, tm, tf, final):
    rows, d = x2d.shape
    dff = wfo.shape[0]
    nff = dff // tf
    rowblk = lambda w: pl.BlockSpec((tm, w), lambda i, j: (i, 0))
    const = lambda shape: pl.BlockSpec(shape, lambda i, j: (0,) * len(shape))
    in_specs = [rowblk(GDN_W), rowblk(ATT_W), rowblk(d),
                mod_spec(2), mod_spec(4), mod_spec(3), mod_spec(5),
                const((1, d)), const((1, d)), const((d, d)),
                pl.BlockSpec((d, tf), lambda i, j: (0, j)),
                pl.BlockSpec((d, tf), lambda i, j: (0, nff + j)),
                pl.BlockSpec((tf, d), lambda i, j: (j, 0))]
    return pl.pallas_call(
        functools.partial(_post_kernel, final, nff),
        grid=(rows // tm, nff), in_specs=in_specs, out_specs=rowblk(d),
        out_shape=jax.ShapeDtypeStruct((rows, d), F32),
        scratch_shapes=[pltpu.VMEM((tm, d), F32), pltpu.VMEM((tm, d), BF), pltpu.VMEM((tm, d), F32)],
        compiler_params=_params(("arbitrary", "arbitrary")),
        name="post",
    )(og, oa, x2d, mod, mod, mod, mod, g2.reshape(1, d), gf.reshape(1, d), wo, wfi, wfi, wfo)


def _pick(n, prefs):
    for p in prefs:
        if n % p == 0:
            return p
    return n


def kernel(x_prompt, x_sample, c_prompt, c_sample, cache_k, cache_v, cache_idx_k, page_table, state_conv, state_ssm,
           w_ada, b_ada, g_norm1, w_in, w_conv, a_log, dt_bias, g_gdn_norm, w_out, g_norm2, w_ffn_in, w_ffn_out,
           g_final):
    bp, tp, d = x_prompt.shape
    db, ts, _ = x_sample.shape
    assert ts == 1 and d == GDN_W + ATT_W and tp % max(ATT_TK, GDN_CHUNK) == 0
    depth = w_in.shape[0]
    npg, page = page_table.shape[1], cache_k.shape[2]
    past = npg * page
    tm_in = _pick(tp, (512, 256))
    tg = _pick(tp, (512, 256))
    tm_post = _pick(tp, (1024, 512, 256))
    dff = w_ffn_out.shape[1]
    tf = _pick(dff, (256, 128))

    xp = x_prompt
    xs = x_sample.reshape(db, d)
    npad = (-(bp + db)) % 8
    c_all = jnp.concatenate([c_prompt, c_sample, jnp.zeros((npad, d), F32)], axis=0)
    new_p, new_s = [], []
    for l in range(depth):
        final = l == depth - 1
        mod = _modulation(c_all, w_ada[l], b_ada[l])
        mod_p = mod[:bp].reshape(bp, 1, 6 * d)
        mod_s = mod[bp:bp + db]
        w_nat, w_t, prow, pcol = _inproj_weights(w_in[l], a_log[l], dt_bias[l])
        wo = w_out[l].astype(BF)
        wfi = w_ffn_in[l].astype(BF)
        wfo = w_ffn_out[l].astype(BF)

        (qkvn, z, vnat, knat, ikT, tail, misc, gct, qT, iqT, iwT, kbf, vT, ikbf) = _inproj_prompt(
            xp, mod_p, g_norm1[l], w_nat, w_t, w_conv[l], prow, pcol, tm_in)
        og, ssm_p = _gdn_prompt(qkvn, z, misc, gct, g_gdn_norm[l], bp, tp, tg)
        oa = _attn_prompt(qT, iqT, iwT, kbf, vT, ikbf, bp, tp)
        tiles_b = tp // tm_post
        spec_p = lambda k: pl.BlockSpec((None, 1, d), lambda i, j: (i // tiles_b, 0, k))
        xp = _post(og, oa, xp.reshape(bp * tp, d), mod_p, spec_p, g_norm2[l], g_final, wo, wfi, wfo,
                   tm_post, tf, final).reshape(bp, tp, d)
        new_p.append((knat.reshape(bp, tp, N_KV, HEAD_DIM), vnat.reshape(bp, tp, N_KV, HEAD_DIM),
                      jnp.swapaxes(ikT, 1, 2), tail[:, 8 - (CONV_W - 1):, :], ssm_p))

        (qkvn_s, z_s, vnat_s, knat_s, slab_s, raw_s, misc_s, aq_s, iq_s) = _inproj_sample(
            xs, mod_s, g_norm1[l], w_nat, w_t, w_conv[l], prow, pcol, state_conv[l], past)
        og_s, ssm_s = _gdn_sample(qkvn_s, z_s, misc_s, g_gdn_norm[l], state_ssm[l], _pick(db, (8,)))
        ik_s = slab_s[:, 0:IDX_DIM]
        iw_s = slab_s[:, IDX_DIM:IDX_DIM + IDX_HEADS]
        scores = _sample_index_scores(iq_s, iw_s, ik_s, jnp.swapaxes(cache_idx_k[l], 1, 2), page_table)
        keep = _sample_select(scores, past + ts)
        oa_s = _sample_attention(aq_s, keep, knat_s, vnat_s, cache_k[l], cache_v[l], page_table)
        spec_s = lambda k: pl.BlockSpec((db, d), lambda i, j: (i, k))
        xs = _post(og_s, oa_s, xs, mod_s, spec_s, g_norm2[l], g_final, wo, wfi, wfo, db, tf, final)
        new_conv_s = jnp.concatenate([state_conv[l][:, 1:, :], raw_s[:, None, :]], axis=1)
        new_s.append((knat_s.reshape(db, ts, N_KV, HEAD_DIM), vnat_s.reshape(db, ts, N_KV, HEAD_DIM),
                      ik_s.reshape(db, ts, IDX_DIM), new_conv_s, ssm_s))

    stack = lambda states, n: jnp.stack([s[n] for s in states], axis=0)
    return (xp, xs.reshape(db, ts, d),
            stack(new_p, 0), stack(new_p, 1), stack(new_p, 2), stack(new_p, 3), stack(new_p, 4),
            stack(new_s, 0), stack(new_s, 1), stack(new_s, 2), stack(new_s, 3), stack(new_s, 4))
```

```python
import functools
import math

import numpy as np
import jax
import jax.numpy as jnp
from jax import lax
from jax.experimental import pallas as pl
from jax.experimental.pallas import tpu as pltpu

F32 = jnp.float32
BF = jnp.bfloat16
I32 = jnp.int32

HEAD_DIM = 128
GDN_HEADS = 4
ATT_HEADS = 4
N_KV = 2
IDX_HEADS = 8
IDX_DIM = 64
CONV_W = 4
TOPK_MAX = 256
QBLOCK = 128
ROPE_THETA = 500000.0
EPS = 1e-6
GDN_W = GDN_HEADS * HEAD_DIM
ATT_W = ATT_HEADS * HEAD_DIM
KV_W = N_KV * HEAD_DIM
IDX_W = IDX_HEADS * IDX_DIM
QKV_W = 3 * GDN_W

GDN_CHUNK = 128
ATT_TK = 256
VT_ROWS = HEAD_DIM + 16
IDX_TK = 512
ATT_BLK = 1024
ATT_TQ = 256
IDX_RING = 5
KV_RING = 3
LANES = 128
NAT_W = QKV_W + GDN_W + KV_W + LANES
T_AQ, T_AK, T_IQ, T_IK, T_MISC, T_AV = 0, 512, 768, 1280, 1344, 1360
T_ROWS = T_AV + KV_W
VMEM_LIMIT = 56 * 1024 * 1024
INT_MIN = np.int32(-2 ** 31)
NEG_INF = float("-inf")


def _params(sem):
    return pltpu.CompilerParams(dimension_semantics=sem, vmem_limit_bytes=VMEM_LIMIT)


def _dot(a, b):
    return jnp.dot(a, b, preferred_element_type=F32)


def _dot_nt(a, b):
    return lax.dot_general(a, b, (((1,), (1,)), ((), ())), preferred_element_type=F32)


def _split3(x):
    hi = x.astype(BF)
    r1 = x - hi.astype(F32)
    mid = r1.astype(BF)
    lo = (r1 - mid.astype(F32)).astype(BF)
    return hi, mid, lo


def _mm_hi(a, b):
    ah = a.astype(BF)
    al = (a - ah.astype(F32)).astype(BF)
    bh = b.astype(BF)
    bl = (b - bh.astype(F32)).astype(BF)
    m = a.shape[0]
    hi = _dot(jnp.concatenate([ah, al], axis=0), bh)
    return hi[0:m] + (hi[m:] + _dot(ah, bl))


def _sigmoid(x):
    return 1.0 / (1.0 + jnp.exp(-x))


def _silu(x):
    return x * _sigmoid(x)


def _softplus(x):
    return jnp.maximum(x, 0.0) + jnp.log(1.0 + jnp.exp(-jnp.abs(x)))


def _rmsnorm(x, g):
    return x * lax.rsqrt(jnp.mean(x * x, axis=-1, keepdims=True) + EPS) * g


def _mod_kernel(c_ref, w_ref, b_ref, o_ref):
    s = _silu(c_ref[...]).astype(BF)
    o_ref[...] = _dot(s, w_ref[...].astype(BF)) + b_ref[...]


def _modulation(c_all, w_ada, b_ada):
    n, d = c_all.shape
    cols = w_ada.shape[1]
    tn = d
    return pl.pallas_call(
        _mod_kernel,
        grid=(cols // tn,),
        in_specs=[pl.BlockSpec((n, d), lambda j: (0, 0)),
                  pl.BlockSpec((d, tn), lambda j: (0, j)),
                  pl.BlockSpec((1, tn), lambda j: (0, j))],
        out_specs=pl.BlockSpec((n, tn), lambda j: (0, j)),
        out_shape=jax.ShapeDtypeStruct((n, cols), F32),
        compiler_params=_params(("arbitrary",)),
        name="modulation",
    )(c_all, w_ada, b_ada.reshape(1, cols))


def _rope_rows(rt_ref, base, half, cos, sin):
    x1 = rt_ref[base:base + half, :]
    x2 = rt_ref[base + half:base + 2 * half, :]
    rt_ref[base:base + half, :] = x1 * cos - x2 * sin
    rt_ref[base + half:base + 2 * half, :] = x2 * cos + x1 * sin


def _project(x_ref, sc_ref, sh_ref, g1_ref, wn_ref, wt_ref, cs128_ref, cs64_ref, pcol_ref, rt_ref):
    h = _rmsnorm(x_ref[...], g1_ref[...]) * (1.0 + sc_ref[...]) + sh_ref[...]
    hb = h.astype(BF)
    nat = _dot(hb, wn_ref[...])
    rt_ref[...] = _dot_nt(wt_ref[...], hb)
    half = HEAD_DIM // 8
    cos, sin = cs128_ref[0:half, :], cs128_ref[half:2 * half, :]
    for hd in range(ATT_HEADS):
        _rope_rows(rt_ref, T_AQ + hd * HEAD_DIM, half, cos, sin)
    for hd in range(N_KV):
        _rope_rows(rt_ref, T_AK + hd * HEAD_DIM, half, cos, sin)
    half = IDX_DIM // 8
    cos, sin = cs64_ref[0:half, :], cs64_ref[half:2 * half, :]
    for hd in range(IDX_HEADS):
        _rope_rows(rt_ref, T_IQ + hd * IDX_DIM, half, cos, sin)
    _rope_rows(rt_ref, T_IK, half, cos, sin)
    mt = rt_ref[T_MISC:T_MISC + 16, :]
    r = lax.broadcasted_iota(I32, mt.shape, 0)
    a_log, dt_b = pcol_ref[:, 0:1], pcol_ref[:, 1:2]
    gate = -jnp.exp(a_log) * _softplus(mt + dt_b)
    mt = jnp.where(r < IDX_HEADS, mt * IDX_HEADS ** -0.5, jnp.where(r < IDX_HEADS + GDN_HEADS, _sigmoid(mt), gate))
    rt_ref[T_MISC:T_MISC + 16, :] = mt
    return nat, mt


def _misc_natural(m, prow_ref):
    lane = lax.broadcasted_iota(I32, m.shape, 1)
    gate = -jnp.exp(prow_ref[0:1, :]) * _softplus(m + prow_ref[1:2, :])
    return jnp.where(lane < GDN_HEADS, _sigmoid(m), jnp.where(lane < 3 * GDN_HEADS, gate, 0.0))


def _qkv_post(conv, qkvn_ref):
    c = _silu(conv)
    for hb in range(2 * GDN_HEADS):
        xh = c[:, hb * HEAD_DIM:(hb + 1) * HEAD_DIM]
        n = xh * lax.rsqrt(jnp.sum(xh * xh, axis=-1, keepdims=True) + EPS)
        if hb < GDN_HEADS:
            n = n * HEAD_DIM ** -0.5
        qkvn_ref[:, hb * HEAD_DIM:(hb + 1) * HEAD_DIM] = n
    qkvn_ref[:, 2 * GDN_W:] = c[:, 2 * GDN_W:]


def _inproj_prompt_kernel(tm, parts, x_ref, sc_ref, sh_ref, g1_ref, wn_ref, wt_ref, wconv_ref, cs128_ref, cs64_ref,
                          prow_ref, pcol_ref, tril_ref,
                          qkvn_ref, z_ref, vnat_ref, knat_ref, ikT_ref, tail_ref, misc_ref, gct_ref,
                          qT_ref, iqT_ref, iwT_ref, kbf_ref, vT_ref, ikbf_ref,
                          xs_ref, rt_ref):
    i = pl.program_id(1)

    @pl.when(i == 0)
    def _():
        xs_ref[0:8, :] = jnp.zeros((8, QKV_W), F32)

    @pl.when(i > 0)
    def _():
        xs_ref[0:8, :] = xs_ref[tm:tm + 8, :]

    th = tm // parts
    for n in range(parts):
        rows, cols = pl.ds(n * th, th), pl.ds(n * th, th)
        _inproj_prompt_rows(
            th, x_ref.at[rows, :], sc_ref, sh_ref, g1_ref, wn_ref, wt_ref, wconv_ref,
            cs128_ref.at[:, cols], cs64_ref.at[:, cols], prow_ref, pcol_ref, tril_ref.at[0:th, 0:th],
            qkvn_ref.at[rows, :], z_ref.at[rows, :], vnat_ref.at[pl.ds(n * th * N_KV, th * N_KV), :],
            knat_ref.at[pl.ds(n * th * N_KV, th * N_KV), :], ikT_ref.at[:, cols], tail_ref, misc_ref.at[rows, :],
            gct_ref.at[pl.ds(n * (th // GDN_CHUNK), th // GDN_CHUNK)],
            qT_ref.at[:, cols], iqT_ref.at[:, cols], iwT_ref.at[:, cols], kbf_ref.at[rows, :],
            vT_ref.at[pl.ds(n * (th // ATT_TK), th // ATT_TK)], ikbf_ref.at[rows, :],
            xs_ref.at[pl.ds(n * th, th + 8), :], rt_ref.at[:, cols])


def _inproj_prompt_rows(tm, x_ref, sc_ref, sh_ref, g1_ref, wn_ref, wt_ref, wconv_ref, cs128_ref, cs64_ref,
                        prow_ref, pcol_ref, tril_ref,
                        qkvn_ref, z_ref, vnat_ref, knat_ref, ikT_ref, tail_ref, misc_ref, gct_ref,
                        qT_ref, iqT_ref, iwT_ref, kbf_ref, vT_ref, ikbf_ref,
                        xs_ref, rt_ref):
    nat, mt = _project(x_ref, sc_ref, sh_ref, g1_ref, wn_ref, wt_ref, cs128_ref, cs64_ref, pcol_ref, rt_ref)
    xs_ref[8:tm + 8, :] = nat[:, 0:QKV_W]
    conv = wconv_ref[0:1, :] * xs_ref[5:tm + 5, :]
    for t in range(1, CONV_W):
        conv = conv + wconv_ref[t:t + 1, :] * xs_ref[5 + t:tm + 5 + t, :]
    tail_ref[...] = xs_ref[tm:tm + 8, :]
    _qkv_post(conv, qkvn_ref)
    z_ref[...] = nat[:, QKV_W:QKV_W + GDN_W]
    v0 = QKV_W + GDN_W
    for hd in range(N_KV):
        vnat_ref[pl.ds(hd, tm, stride=N_KV), :] = nat[:, v0 + hd * HEAD_DIM:v0 + (hd + 1) * HEAD_DIM]

    gm = _misc_natural(nat[:, NAT_W - LANES:], prow_ref)
    tril = tril_ref[...]
    hi, mid, lo = _split3(gm)
    gc = _dot(tril, hi) + (_dot(tril, mid) + _dot(tril, lo))
    lane = lax.broadcasted_iota(I32, gm.shape, 1)
    misc_ref[...] = jnp.where(lane < 2 * GDN_HEADS, gm, gc)
    hi, mid, lo = _split3(mt)
    gct = _dot_nt(hi, tril) + (_dot_nt(mid, tril) + _dot_nt(lo, tril))
    r = lax.broadcasted_iota(I32, mt.shape, 0)
    bg = jnp.where(r < IDX_HEADS + GDN_HEADS, mt, gct)[8:16, :]
    for cc in range(tm // GDN_CHUNK):
        gct_ref[cc] = bg[:, cc * GDN_CHUNK:(cc + 1) * GDN_CHUNK]

    iwT_ref[...] = mt[0:IDX_HEADS, :]
    qT_ref[...] = rt_ref[T_AQ:T_AQ + ATT_W, :].astype(BF)
    iqT_ref[...] = rt_ref[T_IQ:T_IQ + IDX_W, :].astype(BF)
    kn = rt_ref[T_AK:T_AK + KV_W, :].T
    for hd in range(N_KV):
        knat_ref[pl.ds(hd, tm, stride=N_KV), :] = kn[:, hd * HEAD_DIM:(hd + 1) * HEAD_DIM]
    kbf_ref[...] = kn.astype(BF)
    ikT_ref[...] = rt_ref[T_IK:T_IK + IDX_DIM, :]
    ikbf_ref[...] = rt_ref[T_IK:T_IK + LANES, :].T[:, 0:IDX_DIM].astype(BF)
    for cc in range(tm // ATT_TK):
        for hd in range(N_KV):
            lo = hd * VT_ROWS
            vT_ref[cc, lo:lo + HEAD_DIM, :] = rt_ref[T_AV + hd * HEAD_DIM:T_AV + (hd + 1) * HEAD_DIM,
                                                     cc * ATT_TK:(cc + 1) * ATT_TK].astype(BF)
            vT_ref[cc, lo + HEAD_DIM:lo + VT_ROWS, :] = jnp.ones((VT_ROWS - HEAD_DIM, ATT_TK), BF)


def _inproj_sample_kernel(x_ref, sc_ref, sh_ref, g1_ref, wn_ref, wt_ref, wconv_ref, cs128_ref, cs64_ref,
                          prow_ref, pcol_ref, s0_ref, s1_ref, s2_ref,
                          qkvn_ref, z_ref, vnat_ref, knat_ref, slab_ref, raw_ref, misc_ref, aq_ref, iq_ref,
                          rt_ref):
    nat, _ = _project(x_ref, sc_ref, sh_ref, g1_ref, wn_ref, wt_ref, cs128_ref, cs64_ref, pcol_ref, rt_ref)
    raw = nat[:, 0:QKV_W]
    raw_ref[...] = raw
    conv = wconv_ref[0:1, :] * s0_ref[...]
    conv = conv + wconv_ref[1:2, :] * s1_ref[...]
    conv = conv + wconv_ref[2:3, :] * s2_ref[...]
    conv = conv + wconv_ref[3:4, :] * raw
    _qkv_post(conv, qkvn_ref)
    z_ref[...] = nat[:, QKV_W:QKV_W + GDN_W]
    vnat_ref[...] = nat[:, QKV_W + GDN_W:QKV_W + GDN_W + KV_W]
    misc_ref[...] = _misc_natural(nat[:, NAT_W - LANES:], prow_ref)
    aq_ref[...] = rt_ref[T_AQ:T_AQ + ATT_W, :].T
    iq_ref[...] = rt_ref[T_IQ:T_IQ + IDX_W, :].T
    knat_ref[...] = rt_ref[T_AK:T_AK + KV_W, :].T
    slab_ref[...] = rt_ref[T_IK:T_IK + LANES, :].T


def _inproj_weights(w_in, a_log, dt_bias):
    offs = np.cumsum([0, QKV_W, GDN_W, GDN_HEADS, GDN_HEADS, ATT_W, KV_W, KV_W, IDX_W, IDX_DIM, IDX_HEADS])
    qkv, z, beta, a, aq, ak, av, iq, ik, iw = [w_in[:, offs[n]:offs[n + 1]] for n in range(10)]
    d = w_in.shape[0]
    pad = jnp.zeros((d, LANES - 3 * GDN_HEADS), w_in.dtype)
    w_nat = jnp.concatenate([qkv, z, av, beta, a, a, pad], axis=1).astype(BF)
    w_t = jnp.concatenate([aq, ak, iq, ik, iw, beta, a, av], axis=1).T.astype(BF)
    zrow = jnp.zeros((LANES - 3 * GDN_HEADS,), F32)
    prow = jnp.stack([jnp.concatenate([jnp.zeros((GDN_HEADS,), F32), a_log, a_log, zrow]),
                      jnp.concatenate([jnp.zeros((GDN_HEADS,), F32), dt_bias, dt_bias, zrow])])
    z12 = jnp.zeros((IDX_HEADS + GDN_HEADS,), F32)
    pcol = jnp.stack([jnp.concatenate([z12, a_log]), jnp.concatenate([z12, dt_bias])], axis=1)
    return w_nat, w_t, prow, pcol


def _rope_tables(pos, dim):
    half = dim // 8
    inv = ROPE_THETA ** (-jnp.arange(half, dtype=F32) / half)
    ang = pos.astype(F32)[:, None] * inv[None, :]
    return jnp.concatenate([jnp.cos(ang).T, jnp.sin(ang).T], axis=0)


def _inproj_prompt(x, mod, g1, w_nat, w_t, w_conv, prow, pcol, tm):
    bp, tp, d = x.shape
    nt = tp // tm
    rows = bp * tp
    pos = jnp.arange(tp, dtype=I32)
    cs128, cs64 = _rope_tables(pos, HEAD_DIM), _rope_tables(pos, IDX_DIM)
    ri = np.arange(tm)
    tril = jnp.asarray((ri[:, None] // GDN_CHUNK == ri[None, :] // GDN_CHUNK) & (ri[None, :] <= ri[:, None]), BF)
    const = lambda shape: pl.BlockSpec(shape, lambda b, i: (0,) * len(shape))
    rowblk = lambda w: pl.BlockSpec((tm, w), lambda b, i: (b * nt + i, 0))
    colblk = lambda h: pl.BlockSpec((h, tm), lambda b, i: (0, b * nt + i))
    in_specs = [
        pl.BlockSpec((None, tm, d), lambda b, i: (b, i, 0)),
        pl.BlockSpec((None, 1, d), lambda b, i: (b, 0, 1)),
        pl.BlockSpec((None, 1, d), lambda b, i: (b, 0, 0)),
        const((1, d)), const((d, NAT_W)), const((T_ROWS, d)), const((CONV_W, QKV_W)),
        pl.BlockSpec((HEAD_DIM // 4, tm), lambda b, i: (0, i)),
        pl.BlockSpec((IDX_DIM // 4, tm), lambda b, i: (0, i)),
        const((2, LANES)), const((16, 2)), const((tm, tm)),
    ]
    out_shape = [
        jax.ShapeDtypeStruct((rows, QKV_W), F32), jax.ShapeDtypeStruct((rows, GDN_W), F32),
        jax.ShapeDtypeStruct((rows * N_KV, HEAD_DIM), F32), jax.ShapeDtypeStruct((rows * N_KV, HEAD_DIM), F32),
        jax.ShapeDtypeStruct((bp, IDX_DIM, tp), F32), jax.ShapeDtypeStruct((bp, 8, QKV_W), F32),
        jax.ShapeDtypeStruct((rows, LANES), F32), jax.ShapeDtypeStruct((rows // GDN_CHUNK, 8, GDN_CHUNK), F32),
        jax.ShapeDtypeStruct((ATT_W, rows), BF), jax.ShapeDtypeStruct((IDX_W, rows), BF),
        jax.ShapeDtypeStruct((IDX_HEADS, rows), F32), jax.ShapeDtypeStruct((rows, KV_W), BF),
        jax.ShapeDtypeStruct((rows // ATT_TK, N_KV * VT_ROWS, ATT_TK), BF), jax.ShapeDtypeStruct((rows, IDX_DIM), BF),
    ]
    out_specs = [
        rowblk(QKV_W), rowblk(GDN_W),
        pl.BlockSpec((tm * N_KV, HEAD_DIM), lambda b, i: (b * nt + i, 0)),
        pl.BlockSpec((tm * N_KV, HEAD_DIM), lambda b, i: (b * nt + i, 0)),
        pl.BlockSpec((None, IDX_DIM, tm), lambda b, i: (b, 0, i)),
        pl.BlockSpec((None, 8, QKV_W), lambda b, i: (b, 0, 0)),
        rowblk(LANES),
        pl.BlockSpec((tm // GDN_CHUNK, 8, GDN_CHUNK), lambda b, i: (b * nt + i, 0, 0)),
        colblk(ATT_W), colblk(IDX_W), colblk(IDX_HEADS), rowblk(KV_W),
        pl.BlockSpec((tm // ATT_TK, N_KV * VT_ROWS, ATT_TK), lambda b, i: (b * nt + i, 0, 0)),
        rowblk(IDX_DIM),
    ]
    return pl.pallas_call(
        functools.partial(_inproj_prompt_kernel, tm, tm // _pick(tm, (ATT_TK,))),
        grid=(bp, nt), in_specs=in_specs, out_specs=out_specs, out_shape=out_shape,
        scratch_shapes=[pltpu.VMEM((tm + 8, QKV_W), F32), pltpu.VMEM((T_ROWS, tm), F32)],
        compiler_params=_params(("arbitrary", "arbitrary")),
        name="inproj_prompt",
    )(x, mod, mod, g1.reshape(1, d), w_nat, w_t, w_conv, cs128, cs64, prow, pcol, tril)


def _inproj_sample(x, mod, g1, w_nat, w_t, w_conv, prow, pcol, conv_state, past):
    db, d = x.shape
    pos = jnp.full((db,), past, I32)
    cs128, cs64 = _rope_tables(pos, HEAD_DIM), _rope_tables(pos, IDX_DIM)
    full = lambda shape: pl.BlockSpec(shape, lambda i: (0,) * len(shape))
    in_specs = [
        full((db, d)),
        pl.BlockSpec((db, d), lambda i: (0, 1)), pl.BlockSpec((db, d), lambda i: (0, 0)),
        full((1, d)), full((d, NAT_W)), full((T_ROWS, d)), full((CONV_W, QKV_W)),
        full((HEAD_DIM // 4, db)), full((IDX_DIM // 4, db)), full((2, LANES)), full((16, 2)),
        full((db, QKV_W)), full((db, QKV_W)), full((db, QKV_W)),
    ]
    widths = [QKV_W, GDN_W, KV_W, KV_W, LANES, QKV_W, LANES, ATT_W, IDX_W]
    return pl.pallas_call(
        _inproj_sample_kernel,
        grid=(1,), in_specs=in_specs,
        out_specs=[full((db, w)) for w in widths],
        out_shape=[jax.ShapeDtypeStruct((db, w), F32) for w in widths],
        scratch_shapes=[pltpu.VMEM((T_ROWS, db), F32)],
        compiler_params=_params(("arbitrary",)),
        name="inproj_sample",
    )(x, mod, mod, g1.reshape(1, d), w_nat, w_t, w_conv, cs128, cs64, prow, pcol,
      conv_state[:, 0], conv_state[:, 1], conv_state[:, 2])


def _unit_lower_inverses(lows):
    c = lows[0].shape[0]
    ii = lax.broadcasted_iota(I32, (c, c), 0)
    jj = lax.broadcasted_iota(I32, (c, c), 1)
    eye = jnp.where(ii == jj, 1.0, 0.0)
    levels = int(math.log2(c)) - 1
    ts = [eye - low for low in lows]
    ps = [_mm_hi(low, low) for low in lows]
    for lvl in range(levels):
        if lvl == levels - 1:
            ts = [t + _mm_hi(t, p) for t, p in zip(ts, ps)]
        else:
            both = [_mm_hi(jnp.concatenate([t, p], axis=0), p) for t, p in zip(ts, ps)]
            ts = [t + b[0:c] for t, b in zip(ts, both)]
            ps = [b[c:] for b in both]
    return ts


def _gdn_prompt_kernel(tg, qkvn_ref, z_ref, misc_ref, gct_ref, gn_ref, o_ref, ssm_ref, s_ref):
    i = pl.program_id(1)

    @pl.when(i == 0)
    def _():
        s_ref[...] = jnp.zeros(s_ref.shape, F32)

    c = GDN_CHUNK
    ii = lax.broadcasted_iota(I32, (c, c), 0)
    jj = lax.broadcasted_iota(I32, (c, c), 1)
    pairs = [(cc, hd) for cc in range(tg // c) for hd in range(GDN_HEADS)]

    qs, ks, gcs, rhs, lows, intras = {}, {}, {}, {}, [], {}
    for cc, hd in pairs:
        r0, lo = cc * c, hd * HEAD_DIM
        q = qkvn_ref[r0:r0 + c, lo:lo + HEAD_DIM]
        k = qkvn_ref[r0:r0 + c, GDN_W + lo:GDN_W + lo + HEAD_DIM]
        v = qkvn_ref[r0:r0 + c, 2 * GDN_W + lo:2 * GDN_W + lo + HEAD_DIM]
        beta_c = misc_ref[r0:r0 + c, hd:hd + 1]
        gc_c = misc_ref[r0:r0 + c, 2 * GDN_HEADS + hd:2 * GDN_HEADS + hd + 1]
        gc_r = gct_ref[cc, GDN_HEADS + hd:GDN_HEADS + hd + 1, :]
        decay = jnp.where(ii >= jj, jnp.exp(jnp.where(ii >= jj, gc_c - gc_r, 0.0)), 0.0)
        kb = k * beta_c
        both = _dot_nt(jnp.concatenate([kb, q], axis=0).astype(BF), k.astype(BF))
        lows.append(jnp.where(ii > jj, both[0:c] * decay, 0.0))
        intras[cc, hd] = (both[c:] * decay).astype(BF)
        rhs[cc, hd] = jnp.concatenate([v * beta_c, kb * jnp.exp(gc_c)], axis=1).astype(BF)
        qs[cc, hd], ks[cc, hd], gcs[cc, hd] = q, k, gc_c
    ts = _unit_lower_inverses(lows)
    uws = {p: _dot(t.astype(BF), rhs[p]) for p, t in zip(pairs, ts)}

    for cc, hd in pairs:
        r0, lo = cc * c, hd * HEAD_DIM
        q, k, gc_c, uw = qs[cc, hd], ks[cc, hd], gcs[cc, hd], uws[cc, hd]
        s = s_ref[hd]
        ws = _dot(jnp.concatenate([uw[:, HEAD_DIM:], q * jnp.exp(gc_c)], axis=0).astype(BF), s.astype(BF))
        v_new = (uw[:, 0:HEAD_DIM] - ws[0:c]).astype(BF)
        o = ws[c:] + _dot(intras[cc, hd], v_new)
        g_last = gc_c[c - 1:c, :]
        kd = k * jnp.exp(g_last - gc_c)
        s_ref[hd] = s * jnp.exp(g_last) + _dot(kd.T.astype(BF), v_new)
        o = _rmsnorm(o, gn_ref[...]) * _silu(z_ref[r0:r0 + c, lo:lo + HEAD_DIM])
        o_ref[r0:r0 + c, lo:lo + HEAD_DIM] = o.astype(BF)
    ssm_ref[...] = s_ref[...]


def _gdn_prompt(qkvn, z, misc, gct, g_norm, bp, tp, tg):
    nt = tp // tg
    rows = bp * tp
    rowblk = lambda w: pl.BlockSpec((tg, w), lambda b, i: (b * nt + i, 0))
    return pl.pallas_call(
        functools.partial(_gdn_prompt_kernel, tg),
        grid=(bp, nt),
        in_specs=[rowblk(QKV_W), rowblk(GDN_W), rowblk(LANES),
                  pl.BlockSpec((tg // GDN_CHUNK, 8, GDN_CHUNK), lambda b, i: (b * nt + i, 0, 0)),
                  pl.BlockSpec((1, HEAD_DIM), lambda b, i: (0, 0))],
        out_specs=[rowblk(GDN_W),
                   pl.BlockSpec((None, GDN_HEADS, HEAD_DIM, HEAD_DIM), lambda b, i: (b, 0, 0, 0))],
        out_shape=[jax.ShapeDtypeStruct((rows, GDN_W), BF),
                   jax.ShapeDtypeStruct((bp, GDN_HEADS, HEAD_DIM, HEAD_DIM), F32)],
        scratch_shapes=[pltpu.VMEM((GDN_HEADS, HEAD_DIM, HEAD_DIM), F32)],
        compiler_params=_params(("arbitrary", "arbitrary")),
        name="gdn_prompt",
    )(qkvn, z, misc, gct, g_norm.reshape(1, HEAD_DIM))


def _gdn_sample_kernel(nb, qkvn_ref, z_ref, misc_ref, gn_ref, s_ref, o_ref, ssm_ref):
    for bi in range(nb):
        for hd in range(GDN_HEADS):
            lo = hd * HEAD_DIM
            q = qkvn_ref[bi:bi + 1, lo:lo + HEAD_DIM]
            k = qkvn_ref[bi:bi + 1, GDN_W + lo:GDN_W + lo + HEAD_DIM]
            v = qkvn_ref[bi:bi + 1, 2 * GDN_W + lo:2 * GDN_W + lo + HEAD_DIM]
            beta = misc_ref[bi:bi + 1, hd:hd + 1]
            g = misc_ref[bi:bi + 1, GDN_HEADS + hd:GDN_HEADS + hd + 1]
            kcol = jnp.broadcast_to(k, (HEAD_DIM, HEAD_DIM)).T
            qcol = jnp.broadcast_to(q, (HEAD_DIM, HEAD_DIM)).T
            s = s_ref[bi, hd] * jnp.exp(g)
            ks = jnp.sum(kcol * s, axis=0, keepdims=True)
            delta = (v - ks) * beta
            s = s + kcol * delta
            ssm_ref[bi, hd] = s
            o = jnp.sum(qcol * s, axis=0, keepdims=True)
            o = _rmsnorm(o, gn_ref[...]) * _silu(z_ref[bi:bi + 1, lo:lo + HEAD_DIM])
            o_ref[bi:bi + 1, lo:lo + HEAD_DIM] = o.astype(BF)


def _gdn_sample(qkvn, z, misc, g_norm, state, nb):
    db = qkvn.shape[0]
    rowblk = lambda w: pl.BlockSpec((nb, w), lambda i: (i, 0))
    sblk = pl.BlockSpec((nb, GDN_HEADS, HEAD_DIM, HEAD_DIM), lambda i: (i, 0, 0, 0))
    return pl.pallas_call(
        functools.partial(_gdn_sample_kernel, nb),
        grid=(db // nb,),
        in_specs=[rowblk(QKV_W), rowblk(GDN_W), rowblk(LANES), pl.BlockSpec((1, HEAD_DIM), lambda i: (0, 0)), sblk],
        out_specs=[rowblk(GDN_W), sblk],
        out_shape=[jax.ShapeDtypeStruct((db, GDN_W), BF), jax.ShapeDtypeStruct(state.shape, F32)],
        compiler_params=_params(("arbitrary",)),
        name="gdn_sample",
    )(qkvn, z, misc, g_norm.reshape(1, HEAD_DIM), state)


def _ordered_word_to_float(u):
    s = u ^ INT_MIN
    return lax.bitcast_convert_type(s ^ ((s >> 31) & np.int32(0x7FFFFFFF)), F32)


def _select_topk_bias(i_ref, t_ref, nch, tk, k_sel, idx_bits, taken=0.0, dropped=NEG_INF):
    nl = i_ref.shape[1]
    row = lax.broadcasted_iota(I32, (tk, 1), 0)

    def chunk(c):
        return pl.ds(pl.multiple_of(c * tk, tk), tk)

    def count(ref, pred):
        def body(c, acc):
            m = pred(ref[chunk(c), :])
            ones = jnp.where(m, 1, 0).astype(I32).reshape(tk // 8, 8, nl)
            parts = [ones[n] for n in range(tk // 8)]
            while len(parts) > 1:
                parts = [a + b for a, b in zip(parts[0::2], parts[1::2])] + parts[len(parts) & ~1:]
            return acc + parts[0]
        acc = lax.fori_loop(0, nch, body, jnp.zeros((8, nl), I32))
        return jnp.sum(acc, axis=0, keepdims=True)

    def bit_body(i, carry):
        t_u, n_ge = carry
        cand_u = t_u | jnp.left_shift(jnp.int32(1), 31 - i)
        cand = _ordered_word_to_float(cand_u)
        cnt = count(i_ref, lambda sc: sc >= cand)
        take = cnt >= k_sel
        return jnp.where(take, cand_u, t_u), jnp.where(take, cnt, n_ge)

    t_u, n_ge = lax.fori_loop(0, 32, bit_body, (jnp.zeros((1, nl), I32), jnp.full((1, nl), -1, I32)))
    thr = jnp.where((t_u >> 23) == 0, NEG_INF, _ordered_word_to_float(t_u))

    def tie_break(_):
        need = k_sel - count(i_ref, lambda sc: sc > thr)

        def mark(c, carry):
            t_ref[chunk(c), :] = jnp.where(i_ref[chunk(c), :] == thr, c * tk + row, 2 ** idx_bits)
            return carry

        lax.fori_loop(0, nch, mark, 0)

        def idx_body(i, x):
            cand = x | jnp.left_shift(jnp.int32(1), idx_bits - 1 - i)
            cnt = count(t_ref, lambda idx: idx < cand)
            return jnp.where(cnt < need, cand, x)

        return lax.fori_loop(0, idx_bits, idx_body, jnp.zeros((1, nl), I32))

    x = lax.cond(jnp.max(n_ge) > k_sel, tie_break, lambda _: jnp.full((1, nl), 2 ** idx_bits - 1, I32), 0)

    def write(c, carry):
        sc = i_ref[chunk(c), :]
        sel = (sc > thr) | ((sc == thr) & (c * tk + row <= x))
        i_ref[chunk(c), :] = jnp.where(sel & (sc > NEG_INF), taken, dropped)
        return carry

    lax.fori_loop(0, nch, write, 0)


def _attn_prompt_kernel(tq, tk, blk, k_sel, idx_bits, qT_ref, iqT_ref, iwT_ref, kbf_ref, vT_ref, ikbf_ref, o_ref,
                        i_ref, t_ref):
    t0 = pl.program_id(1) * tq
    nch = (t0 + tq + tk - 1) // tk
    nblk = (t0 + tq + blk - 1) // blk
    qpos = t0 + lax.broadcasted_iota(I32, (1, tq), 1)
    row = lax.broadcasted_iota(I32, (tk, 1), 0)
    w = iwT_ref[...]
    pairs = IDX_HEADS // 2
    rhs = [jnp.concatenate([iqT_ref[(2 * p) * IDX_DIM:(2 * p + 1) * IDX_DIM, :],
                            iqT_ref[(2 * p + 1) * IDX_DIM:(2 * p + 2) * IDX_DIM, :]], axis=1) for p in range(pairs)]

    def chunk(c):
        return pl.ds(pl.multiple_of(c * tk, tk), tk)

    def idx_body(c, carry):
        ikc = ikbf_ref[chunk(c), :]
        acc = jnp.zeros((tk, tq), F32)
        for p in range(pairs):
            d = jnp.maximum(_dot(ikc, rhs[p]), 0.0)
            acc = acc + d[:, 0:tq] * w[2 * p:2 * p + 1, :] + d[:, tq:] * w[2 * p + 1:2 * p + 2, :]
        i_ref[chunk(c), :] = jnp.where(c * tk + row <= qpos, acc, NEG_INF)
        return carry

    lax.fori_loop(0, nch, idx_body, 0)

    def fill_body(c, carry):
        i_ref[chunk(c), :] = jnp.full((tk, tq), NEG_INF, F32)
        return carry

    lax.fori_loop(nch, nblk * (blk // tk), fill_body, 0)
    tks = max(tk * LANES // tq, 8)
    _select_topk_bias(i_ref, t_ref, (t0 + tq + tks - 1) // tks, tks, k_sel, idx_bits)

    scale2 = HEAD_DIM ** -0.5 * math.log2(math.e)
    rep = ATT_HEADS // N_KV
    sub = ATT_TK
    nsub = blk // sub
    qg = [jnp.concatenate([qT_ref[(rep * g + r) * HEAD_DIM:(rep * g + r + 1) * HEAD_DIM, :] for r in range(rep)],
                          axis=1) for g in range(N_KV)]

    def body(c, carry):
        out = []
        for g in range(N_KV):
            m, acc = carry[g]
            ss = []
            for i in range(nsub):
                rows = pl.ds(pl.multiple_of(c * blk + i * sub, sub), sub)
                bias = i_ref[rows, :]
                ss.append(_dot(kbf_ref[rows, g * HEAD_DIM:(g + 1) * HEAD_DIM], qg[g]) * scale2
                          + jnp.concatenate([bias] * rep, axis=1))
            m_new = m
            for s in ss:
                m_new = jnp.maximum(m_new, jnp.max(s, axis=0, keepdims=True))
            m_safe = jnp.where(m_new == NEG_INF, 0.0, m_new)
            acc = acc * jnp.exp2(m - m_safe)
            for i, s in enumerate(ss):
                p = jnp.exp2(s - m_safe).astype(BF)
                acc = acc + _dot(vT_ref[c * nsub + i, g * VT_ROWS:(g + 1) * VT_ROWS, :], p)
            out.append((m_new, acc))
        return tuple(out)

    init = tuple((jnp.full((1, rep * tq), NEG_INF, F32), jnp.zeros((VT_ROWS, rep * tq), F32)) for _ in range(N_KV))
    res = lax.fori_loop(0, nblk, body, init)
    for g in range(N_KV):
        acc = res[g][1]
        o_t = acc[0:HEAD_DIM] / acc[HEAD_DIM:HEAD_DIM + 1]
        for r in range(rep):
            hd = rep * g + r
            o_ref[:, hd * HEAD_DIM:(hd + 1) * HEAD_DIM] = o_t[:, r * tq:(r + 1) * tq].T.astype(BF)


def _attn_prompt(qT, iqT, iwT, kbf, vT, ikbf, bp, tp):
    tq = _pick(tp, (ATT_TQ, QBLOCK))
    tk = _pick(tp, (IDX_TK, ATT_TK))
    blk = _pick(tp, (ATT_BLK, IDX_TK, ATT_TK))
    assert blk % tk == 0 and blk % ATT_TK == 0
    nq = tp // tq
    rows = bp * tp
    k_sel = min(TOPK_MAX, tp // 4)
    idx_bits = max(1, int(tp - 1).bit_length())
    colblk = lambda h: pl.BlockSpec((h, tq), lambda b, j: (0, b * nq + j))
    return pl.pallas_call(
        functools.partial(_attn_prompt_kernel, tq, tk, blk, k_sel, idx_bits),
        grid=(bp, nq),
        in_specs=[colblk(ATT_W), colblk(IDX_W), colblk(IDX_HEADS),
                  pl.BlockSpec((tp, KV_W), lambda b, j: (b, 0)),
                  pl.BlockSpec((tp // ATT_TK, N_KV * VT_ROWS, ATT_TK), lambda b, j: (b, 0, 0)),
                  pl.BlockSpec((tp, IDX_DIM), lambda b, j: (b, 0))],
        out_specs=pl.BlockSpec((tq, ATT_W), lambda b, j: (b * nq + j, 0)),
        out_shape=jax.ShapeDtypeStruct((rows, ATT_W), BF),
        scratch_shapes=[pltpu.VMEM((tp, tq), F32), pltpu.VMEM((tp, tq), I32)],
        compiler_params=_params(("arbitrary", "arbitrary")),
        name="attn_prompt",
    )(qT, iqT, iwT, kbf, vT, ikbf)


def _page_prefetch(npg, pt_ref, srcs, bufs, sem_ref):
    b = pl.program_id(0)
    steps = pl.num_programs(0)
    ring = bufs[0].shape[0]
    ahead = ring - 1
    slot = lax.rem(b, ring)

    def copy(n, step, sl, p):
        return pltpu.make_async_copy(srcs[n].at[pt_ref[step, p]], bufs[n].at[sl, p], sem_ref.at[n, sl, p])

    def start_all(step, sl):
        for n in range(len(srcs)):
            for p in range(npg):
                copy(n, step, sl, p).start()

    @pl.when(b == 0)
    def _():
        for step in range(ahead):
            start_all(step, step)

    @pl.when(b + ahead < steps)
    def _():
        start_all(b + ahead, lax.rem(b + ahead, ring))

    for n in range(len(srcs)):
        for p in range(npg):
            copy(n, b, slot, p).wait()
    return slot


def _sidx_kernel(npg, pt_ref, iq_ref, iw_ref, ikn_ref, cache_ref, out_ref, buf_ref, sem_ref):
    slot = _page_prefetch(npg, pt_ref, [cache_ref], [buf_ref], sem_ref)
    iq = iq_ref[...].astype(BF)
    w = iw_ref[...]
    for p in range(npg):
        d = jnp.maximum(_dot(iq, buf_ref[slot, p].astype(BF)), 0.0)
        out_ref[:, p * LANES:(p + 1) * LANES] = jnp.sum(d * w, axis=0, keepdims=True)
    dn = jnp.sum(iq.astype(F32) * ikn_ref[...].astype(BF).astype(F32), axis=1, keepdims=True)
    sn = jnp.sum(jnp.maximum(dn, 0.0) * w, axis=0, keepdims=True)
    lane = lax.broadcasted_iota(I32, (1, LANES), 1)
    out_ref[:, npg * LANES:(npg + 1) * LANES] = jnp.where(lane == 0, sn, NEG_INF)


def _sample_index_scores(iq, iw, ik_new, cache_ik_t, page_table):
    db, npg = page_table.shape
    page = cache_ik_t.shape[2]
    assert page == LANES and db > IDX_RING
    width = (npg + 1) * LANES
    grid_spec = pltpu.PrefetchScalarGridSpec(
        num_scalar_prefetch=1, grid=(db,),
        in_specs=[pl.BlockSpec((None, IDX_HEADS, IDX_DIM), lambda b, pt: (b, 0, 0)),
                  pl.BlockSpec((None, IDX_HEADS, 1), lambda b, pt: (b, 0, 0)),
                  pl.BlockSpec((None, 1, IDX_DIM), lambda b, pt: (b, 0, 0)),
                  pl.BlockSpec(memory_space=pl.ANY)],
        out_specs=pl.BlockSpec((None, 1, width), lambda b, pt: (b, 0, 0)),
        scratch_shapes=[pltpu.VMEM((IDX_RING, npg, IDX_DIM, page), F32),
                        pltpu.SemaphoreType.DMA((1, IDX_RING, npg))])
    out = pl.pallas_call(
        functools.partial(_sidx_kernel, npg), grid_spec=grid_spec,
        out_shape=jax.ShapeDtypeStruct((db, 1, width), F32),
        compiler_params=_params(("arbitrary",)),
        name="sample_index_scores",
    )(page_table, iq.reshape(db, IDX_HEADS, IDX_DIM), iw.reshape(db, IDX_HEADS, 1), ik_new.reshape(db, 1, IDX_DIM),
      cache_ik_t)
    return out.reshape(db, width)


def _ssel_kernel(nch, k_sel, idx_bits, s_ref, spread_ref, keep_ref, i_ref, t_ref):
    for p in range(nch):
        i_ref[p * LANES:(p + 1) * LANES, :] = s_ref[:, p * LANES:(p + 1) * LANES].T
    _select_topk_bias(i_ref, t_ref, nch, LANES, k_sel, idx_bits, taken=1.0, dropped=0.0)
    rows = N_KV * LANES
    for p in range(nch - 1):
        k2 = _dot(spread_ref[...], i_ref[p * LANES:(p + 1) * LANES, :].astype(BF))
        for h in range(N_KV):
            keep_ref[:, p * rows + h * LANES:p * rows + (h + 1) * LANES] = k2[h * LANES:(h + 1) * LANES, :].T
    keep_ref[:, (nch - 1) * rows:(nch - 1) * rows + LANES] = i_ref[(nch - 1) * LANES:nch * LANES, :].T


def _sample_select(scores, n_keys):
    db, width = scores.shape
    assert db == LANES
    nch = width // LANES
    k_sel = min(TOPK_MAX, n_keys // 4)
    idx_bits = max(1, int(width - 1).bit_length())
    rows = N_KV * LANES
    out_w = (nch - 1) * rows + LANES
    spread = jnp.asarray(np.arange(rows)[:, None] // N_KV == np.arange(LANES)[None, :], BF)
    return pl.pallas_call(
        functools.partial(_ssel_kernel, nch, k_sel, idx_bits),
        grid=(1,),
        in_specs=[pl.BlockSpec((db, width), lambda i: (0, 0)), pl.BlockSpec((rows, LANES), lambda i: (0, 0))],
        out_specs=pl.BlockSpec((db, out_w), lambda i: (0, 0)),
        out_shape=jax.ShapeDtypeStruct((db, out_w), F32),
        scratch_shapes=[pltpu.VMEM((width, db), F32), pltpu.VMEM((width, db), I32)],
        compiler_params=_params(("arbitrary",)),
        name="sample_select",
    )(scores, spread)


def _sattn_kernel(npg, pt_ref, q_ref, keep_ref, kn_ref, vn_ref, ck_ref, cv_ref, o_ref, kbuf_ref, vbuf_ref, sem_ref):
    slot = _page_prefetch(npg, pt_ref, [ck_ref, cv_ref], [kbuf_ref, vbuf_ref], sem_ref)
    rep = ATT_HEADS // N_KV
    rows = N_KV * LANES
    scale = HEAD_DIM ** -0.5
    q = q_ref[...]
    q8 = jnp.concatenate([q, jnp.zeros((8 - ATT_HEADS, HEAD_DIM), F32)], axis=0).astype(BF)
    hrow = lax.broadcasted_iota(I32, (8, 1), 0)
    lane = lax.broadcasted_iota(I32, (1, LANES), 1)
    own_kv = lax.broadcasted_iota(I32, (1, rows), 1) % N_KV == hrow // rep
    parts = []
    for p in range(npg):
        s2 = _dot_nt(q8, kbuf_ref[slot, p].astype(BF))
        keep2 = keep_ref[:, p * rows:(p + 1) * rows]
        parts.append(jnp.where(own_kv & (keep2 > 0.5), s2 * scale, NEG_INF))
    kn = kn_ref[...].astype(BF).astype(F32)
    kn8 = jnp.where(hrow // rep == 0, kn[0:1, :], kn[1:2, :])
    s_new = jnp.sum(q8.astype(F32) * kn8, axis=1, keepdims=True)
    keep_new = keep_ref[:, npg * rows:npg * rows + 1]
    parts.append(jnp.where((lane == 0) & (keep_new > 0.5), s_new * scale, NEG_INF))
    s = jnp.concatenate(parts, axis=1)
    m = jnp.max(s, axis=1, keepdims=True)
    e = jnp.exp(s - m)
    pr = (e / jnp.sum(e, axis=1, keepdims=True)).astype(BF)
    o8 = jnp.zeros((8, HEAD_DIM), F32)
    for p in range(npg):
        o8 = o8 + _dot(pr[:, p * rows:(p + 1) * rows], vbuf_ref[slot, p].astype(BF))
    vn = vn_ref[...].astype(BF).astype(F32)
    vn8 = jnp.where(hrow // rep == 0, vn[0:1, :], vn[1:2, :])
    o8 = o8 + pr[:, npg * rows:npg * rows + 1].astype(F32) * vn8
    for hd in range(ATT_HEADS):
        o_ref[:, hd * HEAD_DIM:(hd + 1) * HEAD_DIM] = o8[hd:hd + 1, :].astype(BF)


def _sample_attention(q, keep, k_new, v_new, cache_k, cache_v, page_table):
    db, npg = page_table.shape
    n_phys, page = cache_k.shape[0], cache_k.shape[1]
    assert page == LANES and N_KV == 2 and db > KV_RING
    rows = page * N_KV
    width = npg * rows + LANES
    ck = cache_k.reshape(n_phys, rows, HEAD_DIM)
    cv = cache_v.reshape(n_phys, rows, HEAD_DIM)
    grid_spec = pltpu.PrefetchScalarGridSpec(
        num_scalar_prefetch=1, grid=(db,),
        in_specs=[pl.BlockSpec((None, ATT_HEADS, HEAD_DIM), lambda b, pt: (b, 0, 0)),
                  pl.BlockSpec((None, 1, width), lambda b, pt: (b, 0, 0)),
                  pl.BlockSpec((None, N_KV, HEAD_DIM), lambda b, pt: (b, 0, 0)),
                  pl.BlockSpec((None, N_KV, HEAD_DIM), lambda b, pt: (b, 0, 0)),
                  pl.BlockSpec(memory_space=pl.ANY), pl.BlockSpec(memory_space=pl.ANY)],
        out_specs=pl.BlockSpec((None, 1, ATT_W), lambda b, pt: (b, 0, 0)),
        scratch_shapes=[pltpu.VMEM((KV_RING, npg, rows, HEAD_DIM), F32), pltpu.VMEM((KV_RING, npg, rows, HEAD_DIM), F32),
                        pltpu.SemaphoreType.DMA((2, KV_RING, npg))])
    out = pl.pallas_call(
        functools.partial(_sattn_kernel, npg), grid_spec=grid_spec,
        out_shape=jax.ShapeDtypeStruct((db, 1, ATT_W), BF),
        compiler_params=_params(("arbitrary",)),
        name="sample_attention",
    )(page_table, q.reshape(db, ATT_HEADS, HEAD_DIM), keep.reshape(db, 1, width),
      k_new.reshape(db, N_KV, HEAD_DIM), v_new.reshape(db, N_KV, HEAD_DIM), ck, cv)
    return out.reshape(db, ATT_W)


def _post_kernel(final, nff, og_ref, oa_ref, x_ref, ga1_ref, sc2_ref, sh2_ref, ga2_ref, g2_ref, gf_ref,
                 wo_ref, wg_ref, wu_ref, wd_ref, y_ref, x1_ref, h2_ref, acc_ref):
    jf = pl.program_id(1)

    @pl.when(jf == 0)
    def _():
        mixed = _dot(og_ref[...], wo_ref[0:GDN_W, :]) + _dot(oa_ref[...], wo_ref[GDN_W:, :])
        x1 = x_ref[...] + ga1_ref[...] * mixed
        x1_ref[...] = x1
        h2_ref[...] = (_rmsnorm(x1, g2_ref[...]) * (1.0 + sc2_ref[...]) + sh2_ref[...]).astype(BF)
        acc_ref[...] = jnp.zeros(acc_ref.shape, F32)

    h2 = h2_ref[...]
    act = (_silu(_dot(h2, wg_ref[...])) * _dot(h2, wu_ref[...])).astype(BF)
    acc_ref[...] += _dot(act, wd_ref[...])

    @pl.when(jf == nff - 1)
    def _():
        x2 = x1_ref[...] + ga2_ref[...] * acc_ref[...]
        y_ref[...] = _rmsnorm(x2, gf_ref[...]) if final else x2


def _post(og, oa, x2d, mod, mod_spec, g2, gf, wo, wfi, wfo, tm, tf, final):
    rows, d = x2d.shape
    dff = wfo.shape[0]
    nff = dff // tf
    rowblk = lambda w: pl.BlockSpec((tm, w), lambda i, j: (i, 0))
    const = lambda shape: pl.BlockSpec(shape, lambda i, j: (0,) * len(shape))
    in_specs = [rowblk(GDN_W), rowblk(ATT_W), rowblk(d),
                mod_spec(2), mod_spec(4), mod_spec(3), mod_spec(5),
                const((1, d)), const((1, d)), const((d, d)),
                pl.BlockSpec((d, tf), lambda i, j: (0, j)),
                pl.BlockSpec((d, tf), lambda i, j: (0, nff + j)),
                pl.BlockSpec((tf, d), lambda i, j: (j, 0))]
    return pl.pallas_call(
        functools.partial(_post_kernel, final, nff),
        grid=(rows // tm, nff), in_specs=in_specs, out_specs=rowblk(d),
        out_shape=jax.ShapeDtypeStruct((rows, d), F32),
        scratch_shapes=[pltpu.VMEM((tm, d), F32), pltpu.VMEM((tm, d), BF), pltpu.VMEM((tm, d), F32)],
        compiler_params=_params(("arbitrary", "arbitrary")),
        name="post",
    )(og, oa, x2d, mod, mod, mod, mod, g2.reshape(1, d), gf.reshape(1, d), wo, wfi, wfi, wfo)


def _pick(n, prefs):
    for p in prefs:
        if n % p == 0:
            return p
    return n


def kernel(x_prompt, x_sample, c_prompt, c_sample, cache_k, cache_v, cache_idx_k, page_table, state_conv, state_ssm,
           w_ada, b_ada, g_norm1, w_in, w_conv, a_log, dt_bias, g_gdn_norm, w_out, g_norm2, w_ffn_in, w_ffn_out,
           g_final):
    bp, tp, d = x_prompt.shape
    db, ts, _ = x_sample.shape
    assert ts == 1 and d == GDN_W + ATT_W and tp % max(ATT_TK, GDN_CHUNK) == 0
    depth = w_in.shape[0]
    npg, page = page_table.shape[1], cache_k.shape[2]
    past = npg * page
    tm_in = _pick(tp, (512, 256))
    tg = _pick(tp, (512, 256))
    tm_post = _pick(tp, (1024, 512, 256))
    dff = w_ffn_out.shape[1]
    tf = _pick(dff, (256, 128))

    xp = x_prompt
    xs = x_sample.reshape(db, d)
    npad = (-(bp + db)) % 8
    c_all = jnp.concatenate([c_prompt, c_sample, jnp.zeros((npad, d), F32)], axis=0)
    new_p, new_s = [], []
    for l in range(depth):
        final = l == depth - 1
        mod = _modulation(c_all, w_ada[l], b_ada[l])
        mod_p = mod[:bp].reshape(bp, 1, 6 * d)
        mod_s = mod[bp:bp + db]
        w_nat, w_t, prow, pcol = _inproj_weights(w_in[l], a_log[l], dt_bias[l])
        wo = w_out[l].astype(BF)
        wfi = w_ffn_in[l].astype(BF)
        wfo = w_ffn_out[l].astype(BF)

        (qkvn, z, vnat, knat, ikT, tail, misc, gct, qT, iqT, iwT, kbf, vT, ikbf) = _inproj_prompt(
            xp, mod_p, g_norm1[l], w_nat, w_t, w_conv[l], prow, pcol, tm_in)
        og, ssm_p = _gdn_prompt(qkvn, z, misc, gct, g_gdn_norm[l], bp, tp, tg)
        oa = _attn_prompt(qT, iqT, iwT, kbf, vT, ikbf, bp, tp)
        tiles_b = tp // tm_post
        spec_p = lambda k: pl.BlockSpec((None, 1, d), lambda i, j: (i // tiles_b, 0, k))
        xp = _post(og, oa, xp.reshape(bp * tp, d), mod_p, spec_p, g_norm2[l], g_final, wo, wfi, wfo,
                   tm_post, tf, final).reshape(bp, tp, d)
        new_p.append((knat.reshape(bp, tp, N_KV, HEAD_DIM), vnat.reshape(bp, tp, N_KV, HEAD_DIM),
                      jnp.swapaxes(ikT, 1, 2), tail[:, 8 - (CONV_W - 1):, :], ssm_p))

        (qkvn_s, z_s, vnat_s, knat_s, slab_s, raw_s, misc_s, aq_s, iq_s) = _inproj_sample(
            xs, mod_s, g_norm1[l], w_nat, w_t, w_conv[l], prow, pcol, state_conv[l], past)
        og_s, ssm_s = _gdn_sample(qkvn_s, z_s, misc_s, g_gdn_norm[l], state_ssm[l], _pick(db, (8,)))
        ik_s = slab_s[:, 0:IDX_DIM]
        iw_s = slab_s[:, IDX_DIM:IDX_DIM + IDX_HEADS]
        scores = _sample_index_scores(iq_s, iw_s, ik_s, jnp.swapaxes(cache_idx_k[l], 1, 2), page_table)
        keep = _sample_select(scores, past + ts)
        oa_s = _sample_attention(aq_s, keep, knat_s, vnat_s, cache_k[l], cache_v[l], page_table)
        spec_s = lambda k: pl.BlockSpec((db, d), lambda i, j: (i, k))
        xs = _post(og_s, oa_s, xs, mod_s, spec_s, g_norm2[l], g_final, wo, wfi, wfo, db, tf, final)
        new_conv_s = jnp.concatenate([state_conv[l][:, 1:, :], raw_s[:, None, :]], axis=1)
        new_s.append((knat_s.reshape(db, ts, N_KV, HEAD_DIM), vnat_s.reshape(db, ts, N_KV, HEAD_DIM),
                      ik_s.reshape(db, ts, IDX_DIM), new_conv_s, ssm_s))

    stack = lambda states, n: jnp.stack([s[n] for s in states], axis=0)
    return (xp, xs.reshape(db, ts, d),
            stack(new_p, 0), stack(new_p, 1), stack(new_p, 2), stack(new_p, 3), stack(new_p, 4),
            stack(new_s, 0), stack(new_s, 1), stack(new_s, 2), stack(new_s, 3), stack(new_s, 4))
```

```python
import functools
import math

import numpy as np
import jax
import jax.numpy as jnp
from jax import lax
from jax.experimental import pallas as pl
from jax.experimental.pallas import tpu as pltpu

F32 = jnp.float32
BF = jnp.bfloat16
I32 = jnp.int32

HEAD_DIM = 128
GDN_HEADS = 4
ATT_HEADS = 4
N_KV = 2
IDX_HEADS = 8
IDX_DIM = 64
CONV_W = 4
TOPK_MAX = 256
QBLOCK = 128
ROPE_THETA = 500000.0
EPS = 1e-6
GDN_W = GDN_HEADS * HEAD_DIM
ATT_W = ATT_HEADS * HEAD_DIM
KV_W = N_KV * HEAD_DIM
IDX_W = IDX_HEADS * IDX_DIM
QKV_W = 3 * GDN_W

GDN_CHUNK = 128
ATT_TK = 256
VT_ROWS = HEAD_DIM + 16
IDX_TK = 512
ATT_BLK = 1024
ATT_TQ = 256
IDX_RING = 5
KV_RING = 3
LANES = 128
NAT_W = QKV_W + GDN_W + KV_W + LANES
T_AQ, T_AK, T_IQ, T_IK, T_MISC, T_AV = 0, 512, 768, 1280, 1344, 1360
T_ROWS = T_AV + KV_W
VMEM_LIMIT = 56 * 1024 * 1024
INT_MIN = np.int32(-2 ** 31)
NEG_INF = float("-inf")


def _params(sem):
    return pltpu.CompilerParams(dimension_semantics=sem, vmem_limit_bytes=VMEM_LIMIT)


def _dot(a, b):
    return jnp.dot(a, b, preferred_element_type=F32)


def _dot_nt(a, b):
    return lax.dot_general(a, b, (((1,), (1,)), ((), ())), preferred_element_type=F32)


def _split3(x):
    hi = x.astype(BF)
    r1 = x - hi.astype(F32)
    mid = r1.astype(BF)
    lo = (r1 - mid.astype(F32)).astype(BF)
    return hi, mid, lo


def _mm_hi(a, b):
    ah = a.astype(BF)
    al = (a - ah.astype(F32)).astype(BF)
    bh = b.astype(BF)
    bl = (b - bh.astype(F32)).astype(BF)
    m = a.shape[0]
    hi = _dot(jnp.concatenate([ah, al], axis=0), bh)
    return hi[0:m] + (hi[m:] + _dot(ah, bl))


def _sigmoid(x):
    return 1.0 / (1.0 + jnp.exp(-x))


def _silu(x):
    return x * _sigmoid(x)


def _softplus(x):
    return jnp.maximum(x, 0.0) + jnp.log(1.0 + jnp.exp(-jnp.abs(x)))


def _rmsnorm(x, g):
    return x * lax.rsqrt(jnp.mean(x * x, axis=-1, keepdims=True) + EPS) * g


def _mod_kernel(c_ref, w_ref, b_ref, o_ref):
    s = _silu(c_ref[...]).astype(BF)
    o_ref[...] = _dot(s, w_ref[...].astype(BF)) + b_ref[...]


def _modulation(c_all, w_ada, b_ada):
    n, d = c_all.shape
    cols = w_ada.shape[1]
    tn = d
    return pl.pallas_call(
        _mod_kernel,
        grid=(cols // tn,),
        in_specs=[pl.BlockSpec((n, d), lambda j: (0, 0)),
                  pl.BlockSpec((d, tn), lambda j: (0, j)),
                  pl.BlockSpec((1, tn), lambda j: (0, j))],
        out_specs=pl.BlockSpec((n, tn), lambda j: (0, j)),
        out_shape=jax.ShapeDtypeStruct((n, cols), F32),
        compiler_params=_params(("arbitrary",)),
        name="modulation",
    )(c_all, w_ada, b_ada.reshape(1, cols))


def _rope_rows(rt_ref, base, half, cos, sin):
    x1 = rt_ref[base:base + half, :]
    x2 = rt_ref[base + half:base + 2 * half, :]
    rt_ref[base:base + half, :] = x1 * cos - x2 * sin
    rt_ref[base + half:base + 2 * half, :] = x2 * cos + x1 * sin


def _project(x_ref, sc_ref, sh_ref, g1_ref, wn_ref, wt_ref, cs128_ref, cs64_ref, pcol_ref, rt_ref):
    h = _rmsnorm(x_ref[...], g1_ref[...]) * (1.0 + sc_ref[...]) + sh_ref[...]
    hb = h.astype(BF)
    nat = _dot(hb, wn_ref[...])
    rt_ref[...] = _dot_nt(wt_ref[...], hb)
    half = HEAD_DIM // 8
    cos, sin = cs128_ref[0:half, :], cs128_ref[half:2 * half, :]
    for hd in range(ATT_HEADS):
        _rope_rows(rt_ref, T_AQ + hd * HEAD_DIM, half, cos, sin)
    for hd in range(N_KV):
        _rope_rows(rt_ref, T_AK + hd * HEAD_DIM, half, cos, sin)
    half = IDX_DIM // 8
    cos, sin = cs64_ref[0:half, :], cs64_ref[half:2 * half, :]
    for hd in range(IDX_HEADS):
        _rope_rows(rt_ref, T_IQ + hd * IDX_DIM, half, cos, sin)
    _rope_rows(rt_ref, T_IK, half, cos, sin)
    mt = rt_ref[T_MISC:T_MISC + 16, :]
    r = lax.broadcasted_iota(I32, mt.shape, 0)
    a_log, dt_b = pcol_ref[:, 0:1], pcol_ref[:, 1:2]
    gate = -jnp.exp(a_log) * _softplus(mt + dt_b)
    mt = jnp.where(r < IDX_HEADS, mt * IDX_HEADS ** -0.5, jnp.where(r < IDX_HEADS + GDN_HEADS, _sigmoid(mt), gate))
    rt_ref[T_MISC:T_MISC + 16, :] = mt
    return nat, mt


def _misc_natural(m, prow_ref):
    lane = lax.broadcasted_iota(I32, m.shape, 1)
    gate = -jnp.exp(prow_ref[0:1, :]) * _softplus(m + prow_ref[1:2, :])
    return jnp.where(lane < GDN_HEADS, _sigmoid(m), jnp.where(lane < 3 * GDN_HEADS, gate, 0.0))


def _qkv_post(conv, qkvn_ref):
    c = _silu(conv)
    for hb in range(2 * GDN_HEADS):
        xh = c[:, hb * HEAD_DIM:(hb + 1) * HEAD_DIM]
        n = xh * lax.rsqrt(jnp.sum(xh * xh, axis=-1, keepdims=True) + EPS)
        if hb < GDN_HEADS:
            n = n * HEAD_DIM ** -0.5
        qkvn_ref[:, hb * HEAD_DIM:(hb + 1) * HEAD_DIM] = n
    qkvn_ref[:, 2 * GDN_W:] = c[:, 2 * GDN_W:]


def _inproj_prompt_kernel(tm, parts, x_ref, sc_ref, sh_ref, g1_ref, wn_ref, wt_ref, wconv_ref, cs128_ref, cs64_ref,
                          prow_ref, pcol_ref, tril_ref,
                          qkvn_ref, z_ref, vnat_ref, knat_ref, ikT_ref, tail_ref, misc_ref, gct_ref,
                          qT_ref, iqT_ref, iwT_ref, kbf_ref, vT_ref, ikbf_ref,
                          xs_ref, rt_ref):
    i = pl.program_id(1)

    @pl.when(i == 0)
    def _():
        xs_ref[0:8, :] = jnp.zeros((8, QKV_W), F32)

    @pl.when(i > 0)
    def _():
        xs_ref[0:8, :] = xs_ref[tm:tm + 8, :]

    th = tm // parts
    for n in range(parts):
        rows, cols = pl.ds(n * th, th), pl.ds(n * th, th)
        _inproj_prompt_rows(
            th, x_ref.at[rows, :], sc_ref, sh_ref, g1_ref, wn_ref, wt_ref, wconv_ref,
            cs128_ref.at[:, cols], cs64_ref.at[:, cols], prow_ref, pcol_ref, tril_ref.at[0:th, 0:th],
            qkvn_ref.at[rows, :], z_ref.at[rows, :], vnat_ref.at[pl.ds(n * th * N_KV, th * N_KV), :],
            knat_ref.at[pl.ds(n * th * N_KV, th * N_KV), :], ikT_ref.at[:, cols], tail_ref, misc_ref.at[rows, :],
            gct_ref.at[pl.ds(n * (th // GDN_CHUNK), th // GDN_CHUNK)],
            qT_ref.at[:, cols], iqT_ref.at[:, cols], iwT_ref.at[:, cols], kbf_ref.at[rows, :],
            vT_ref.at[pl.ds(n * (th // ATT_TK), th // ATT_TK)], ikbf_ref.at[rows, :],
            xs_ref.at[pl.ds(n * th, th + 8), :], rt_ref.at[:, cols])


def _inproj_prompt_rows(tm, x_ref, sc_ref, sh_ref, g1_ref, wn_ref, wt_ref, wconv_ref, cs128_ref, cs64_ref,
                        prow_ref, pcol_ref, tril_ref,
                        qkvn_ref, z_ref, vnat_ref, knat_ref, ikT_ref, tail_ref, misc_ref, gct_ref,
                        qT_ref, iqT_ref, iwT_ref, kbf_ref, vT_ref, ikbf_ref,
                        xs_ref, rt_ref):
    nat, mt = _project(x_ref, sc_ref, sh_ref, g1_ref, wn_ref, wt_ref, cs128_ref, cs64_ref, pcol_ref, rt_ref)
    xs_ref[8:tm + 8, :] = nat[:, 0:QKV_W]
    conv = wconv_ref[0:1, :] * xs_ref[5:tm + 5, :]
    for t in range(1, CONV_W):
        conv = conv + wconv_ref[t:t + 1, :] * xs_ref[5 + t:tm + 5 + t, :]
    tail_ref[...] = xs_ref[tm:tm + 8, :]
    _qkv_post(conv, qkvn_ref)
    z_ref[...] = nat[:, QKV_W:QKV_W + GDN_W]
    v0 = QKV_W + GDN_W
    for hd in range(N_KV):
        vnat_ref[pl.ds(hd, tm, stride=N_KV), :] = nat[:, v0 + hd * HEAD_DIM:v0 + (hd + 1) * HEAD_DIM]

    gm = _misc_natural(nat[:, NAT_W - LANES:], prow_ref)
    tril = tril_ref[...]
    hi, mid, lo = _split3(gm)
    gc = _dot(tril, hi) + (_dot(tril, mid) + _dot(tril, lo))
    lane = lax.broadcasted_iota(I32, gm.shape, 1)
    misc_ref[...] = jnp.where(lane < 2 * GDN_HEADS, gm, gc)
    hi, mid, lo = _split3(mt)
    gct = _dot_nt(hi, tril) + (_dot_nt(mid, tril) + _dot_nt(lo, tril))
    r = lax.broadcasted_iota(I32, mt.shape, 0)
    bg = jnp.where(r < IDX_HEADS + GDN_HEADS, mt, gct)[8:16, :]
    for cc in range(tm // GDN_CHUNK):
        gct_ref[cc] = bg[:, cc * GDN_CHUNK:(cc + 1) * GDN_CHUNK]

    iwT_ref[...] = mt[0:IDX_HEADS, :]
    qT_ref[...] = rt_ref[T_AQ:T_AQ + ATT_W, :].astype(BF)
    iqT_ref[...] = rt_ref[T_IQ:T_IQ + IDX_W, :].astype(BF)
    kn = rt_ref[T_AK:T_AK + KV_W, :].T
    for hd in range(N_KV):
        knat_ref[pl.ds(hd, tm, stride=N_KV), :] = kn[:, hd * HEAD_DIM:(hd + 1) * HEAD_DIM]
    kbf_ref[...] = kn.astype(BF)
    ikT_ref[...] = rt_ref[T_IK:T_IK + IDX_DIM, :]
    ikbf_ref[...] = rt_ref[T_IK:T_IK + LANES, :].T[:, 0:IDX_DIM].astype(BF)
    for cc in range(tm // ATT_TK):
        for hd in range(N_KV):
            lo = hd * VT_ROWS
            vT_ref[cc, lo:lo + HEAD_DIM, :] = rt_ref[T_AV + hd * HEAD_DIM:T_AV + (hd + 1) * HEAD_DIM,
                                                     cc * ATT_TK:(cc + 1) * ATT_TK].astype(BF)
            vT_ref[cc, lo + HEAD_DIM:lo + VT_ROWS, :] = jnp.ones((VT_ROWS - HEAD_DIM, ATT_TK), BF)


def _inproj_sample_kernel(x_ref, sc_ref, sh_ref, g1_ref, wn_ref, wt_ref, wconv_ref, cs128_ref, cs64_ref,
                          prow_ref, pcol_ref, s0_ref, s1_ref, s2_ref,
                          qkvn_ref, z_ref, vnat_ref, knat_ref, slab_ref, raw_ref, misc_ref, aq_ref, iq_ref,
                          rt_ref):
    nat, _ = _project(x_ref, sc_ref, sh_ref, g1_ref, wn_ref, wt_ref, cs128_ref, cs64_ref, pcol_ref, rt_ref)
    raw = nat[:, 0:QKV_W]
    raw_ref[...] = raw
    conv = wconv_ref[0:1, :] * s0_ref[...]
    conv = conv + wconv_ref[1:2, :] * s1_ref[...]
    conv = conv + wconv_ref[2:3, :] * s2_ref[...]
    conv = conv + wconv_ref[3:4, :] * raw
    _qkv_post(conv, qkvn_ref)
    z_ref[...] = nat[:, QKV_W:QKV_W + GDN_W]
    vnat_ref[...] = nat[:, QKV_W + GDN_W:QKV_W + GDN_W + KV_W]
    misc_ref[...] = _misc_natural(nat[:, NAT_W - LANES:], prow_ref)
    aq_ref[...] = rt_ref[T_AQ:T_AQ + ATT_W, :].T
    iq_ref[...] = rt_ref[T_IQ:T_IQ + IDX_W, :].T
    knat_ref[...] = rt_ref[T_AK:T_AK + KV_W, :].T
    slab_ref[...] = rt_ref[T_IK:T_IK + LANES, :].T


def _inproj_weights(w_in, a_log, dt_bias):
    offs = np.cumsum([0, QKV_W, GDN_W, GDN_HEADS, GDN_HEADS, ATT_W, KV_W, KV_W, IDX_W, IDX_DIM, IDX_HEADS])
    qkv, z, beta, a, aq, ak, av, iq, ik, iw = [w_in[:, offs[n]:offs[n + 1]] for n in range(10)]
    d = w_in.shape[0]
    pad = jnp.zeros((d, LANES - 3 * GDN_HEADS), w_in.dtype)
    w_nat = jnp.concatenate([qkv, z, av, beta, a, a, pad], axis=1).astype(BF)
    w_t = jnp.concatenate([aq, ak, iq, ik, iw, beta, a, av], axis=1).T.astype(BF)
    zrow = jnp.zeros((LANES - 3 * GDN_HEADS,), F32)
    prow = jnp.stack([jnp.concatenate([jnp.zeros((GDN_HEADS,), F32), a_log, a_log, zrow]),
                      jnp.concatenate([jnp.zeros((GDN_HEADS,), F32), dt_bias, dt_bias, zrow])])
    z12 = jnp.zeros((IDX_HEADS + GDN_HEADS,), F32)
    pcol = jnp.stack([jnp.concatenate([z12, a_log]), jnp.concatenate([z12, dt_bias])], axis=1)
    return w_nat, w_t, prow, pcol


def _rope_tables(pos, dim):
    half = dim // 8
    inv = ROPE_THETA ** (-jnp.arange(half, dtype=F32) / half)
    ang = pos.astype(F32)[:, None] * inv[None, :]
    return jnp.concatenate([jnp.cos(ang).T, jnp.sin(ang).T], axis=0)


def _inproj_prompt(x, mod, g1, w_nat, w_t, w_conv, prow, pcol, tm):
    bp, tp, d = x.shape
    nt = tp // tm
    rows = bp * tp
    pos = jnp.arange(tp, dtype=I32)
    cs128, cs64 = _rope_tables(pos, HEAD_DIM), _rope_tables(pos, IDX_DIM)
    ri = np.arange(tm)
    tril = jnp.asarray((ri[:, None] // GDN_CHUNK == ri[None, :] // GDN_CHUNK) & (ri[None, :] <= ri[:, None]), BF)
    const = lambda shape: pl.BlockSpec(shape, lambda b, i: (0,) * len(shape))
    rowblk = lambda w: pl.BlockSpec((tm, w), lambda b, i: (b * nt + i, 0))
    colblk = lambda h: pl.BlockSpec((h, tm), lambda b, i: (0, b * nt + i))
    in_specs = [
        pl.BlockSpec((None, tm, d), lambda b, i: (b, i, 0)),
        pl.BlockSpec((None, 1, d), lambda b, i: (b, 0, 1)),
        pl.BlockSpec((None, 1, d), lambda b, i: (b, 0, 0)),
        const((1, d)), const((d, NAT_W)), const((T_ROWS, d)), const((CONV_W, QKV_W)),
        pl.BlockSpec((HEAD_DIM // 4, tm), lambda b, i: (0, i)),
        pl.BlockSpec((IDX_DIM // 4, tm), lambda b, i: (0, i)),
        const((2, LANES)), const((16, 2)), const((tm, tm)),
    ]
    out_shape = [
        jax.ShapeDtypeStruct((rows, QKV_W), F32), jax.ShapeDtypeStruct((rows, GDN_W), F32),
        jax.ShapeDtypeStruct((rows * N_KV, HEAD_DIM), F32), jax.ShapeDtypeStruct((rows * N_KV, HEAD_DIM), F32),
        jax.ShapeDtypeStruct((bp, IDX_DIM, tp), F32), jax.ShapeDtypeStruct((bp, 8, QKV_W), F32),
        jax.ShapeDtypeStruct((rows, LANES), F32), jax.ShapeDtypeStruct((rows // GDN_CHUNK, 8, GDN_CHUNK), F32),
        jax.ShapeDtypeStruct((ATT_W, rows), BF), jax.ShapeDtypeStruct((IDX_W, rows), BF),
        jax.ShapeDtypeStruct((IDX_HEADS, rows), F32), jax.ShapeDtypeStruct((rows, KV_W), BF),
        jax.ShapeDtypeStruct((rows // ATT_TK, N_KV * VT_ROWS, ATT_TK), BF), jax.ShapeDtypeStruct((rows, IDX_DIM), BF),
    ]
    out_specs = [
        rowblk(QKV_W), rowblk(GDN_W),
        pl.BlockSpec((tm * N_KV, HEAD_DIM), lambda b, i: (b * nt + i, 0)),
        pl.BlockSpec((tm * N_KV, HEAD_DIM), lambda b, i: (b * nt + i, 0)),
        pl.BlockSpec((None, IDX_DIM, tm), lambda b, i: (b, 0, i)),
        pl.BlockSpec((None, 8, QKV_W), lambda b, i: (b, 0, 0)),
        rowblk(LANES),
        pl.BlockSpec((tm // GDN_CHUNK, 8, GDN_CHUNK), lambda b, i: (b * nt + i, 0, 0)),
        colblk(ATT_W), colblk(IDX_W), colblk(IDX_HEADS), rowblk(KV_W),
        pl.BlockSpec((tm // ATT_TK, N_KV * VT_ROWS, ATT_TK), lambda b, i: (b * nt + i, 0, 0)),
        rowblk(IDX_DIM),
    ]
    return pl.pallas_call(
        functools.partial(_inproj_prompt_kernel, tm, tm // _pick(tm, (ATT_TK,))),
        grid=(bp, nt), in_specs=in_specs, out_specs=out_specs, out_shape=out_shape,
        scratch_shapes=[pltpu.VMEM((tm + 8, QKV_W), F32), pltpu.VMEM((T_ROWS, tm), F32)],
        compiler_params=_params(("arbitrary", "arbitrary")),
        name="inproj_prompt",
    )(x, mod, mod, g1.reshape(1, d), w_nat, w_t, w_conv, cs128, cs64, prow, pcol, tril)


def _inproj_sample(x, mod, g1, w_nat, w_t, w_conv, prow, pcol, conv_state, past):
    db, d = x.shape
    pos = jnp.full((db,), past, I32)
    cs128, cs64 = _rope_tables(pos, HEAD_DIM), _rope_tables(pos, IDX_DIM)
    full = lambda shape: pl.BlockSpec(shape, lambda i: (0,) * len(shape))
    in_specs = [
        full((db, d)),
        pl.BlockSpec((db, d), lambda i: (0, 1)), pl.BlockSpec((db, d), lambda i: (0, 0)),
        full((1, d)), full((d, NAT_W)), full((T_ROWS, d)), full((CONV_W, QKV_W)),
        full((HEAD_DIM // 4, db)), full((IDX_DIM // 4, db)), full((2, LANES)), full((16, 2)),
        full((db, QKV_W)), full((db, QKV_W)), full((db, QKV_W)),
    ]
    widths = [QKV_W, GDN_W, KV_W, KV_W, LANES, QKV_W, LANES, ATT_W, IDX_W]
    return pl.pallas_call(
        _inproj_sample_kernel,
        grid=(1,), in_specs=in_specs,
        out_specs=[full((db, w)) for w in widths],
        out_shape=[jax.ShapeDtypeStruct((db, w), F32) for w in widths],
        scratch_shapes=[pltpu.VMEM((T_ROWS, db), F32)],
        compiler_params=_params(("arbitrary",)),
        name="inproj_sample",
    )(x, mod, mod, g1.reshape(1, d), w_nat, w_t, w_conv, cs128, cs64, prow, pcol,
      conv_state[:, 0], conv_state[:, 1], conv_state[:, 2])


def _unit_lower_inverses(lows):
    c = lows[0].shape[0]
    ii = lax.broadcasted_iota(I32, (c, c), 0)
    jj = lax.broadcasted_iota(I32, (c, c), 1)
    eye = jnp.where(ii == jj, 1.0, 0.0)
    levels = int(math.log2(c)) - 1
    ts = [eye - low for low in lows]
    ps = [_mm_hi(low, low) for low in lows]
    for lvl in range(levels):
        if lvl == levels - 1:
            ts = [t + _mm_hi(t, p) for t, p in zip(ts, ps)]
        else:
            both = [_mm_hi(jnp.concatenate([t, p], axis=0), p) for t, p in zip(ts, ps)]
            ts = [t + b[0:c] for t, b in zip(ts, both)]
            ps = [b[c:] for b in both]
    return ts


def _gdn_prompt_kernel(tg, qkvn_ref, z_ref, misc_ref, gct_ref, gn_ref, o_ref, ssm_ref, s_ref):
    i = pl.program_id(1)

    @pl.when(i == 0)
    def _():
        s_ref[...] = jnp.zeros(s_ref.shape, F32)

    c = GDN_CHUNK
    ii = lax.broadcasted_iota(I32, (c, c), 0)
    jj = lax.broadcasted_iota(I32, (c, c), 1)
    pairs = [(cc, hd) for cc in range(tg // c) for hd in range(GDN_HEADS)]

    qs, ks, gcs, rhs, lows, intras = {}, {}, {}, {}, [], {}
    for cc, hd in pairs:
        r0, lo = cc * c, hd * HEAD_DIM
        q = qkvn_ref[r0:r0 + c, lo:lo + HEAD_DIM]
        k = qkvn_ref[r0:r0 + c, GDN_W + lo:GDN_W + lo + HEAD_DIM]
        v = qkvn_ref[r0:r0 + c, 2 * GDN_W + lo:2 * GDN_W + lo + HEAD_DIM]
        beta_c = misc_ref[r0:r0 + c, hd:hd + 1]
        gc_c = misc_ref[r0:r0 + c, 2 * GDN_HEADS + hd:2 * GDN_HEADS + hd + 1]
        gc_r = gct_ref[cc, GDN_HEADS + hd:GDN_HEADS + hd + 1, :]
        decay = jnp.where(ii >= jj, jnp.exp(jnp.where(ii >= jj, gc_c - gc_r, 0.0)), 0.0)
        kb = k * beta_c
        both = _dot_nt(jnp.concatenate([kb, q], axis=0).astype(BF), k.astype(BF))
        lows.append(jnp.where(ii > jj, both[0:c] * decay, 0.0))
        intras[cc, hd] = (both[c:] * decay).astype(BF)
        rhs[cc, hd] = jnp.concatenate([v * beta_c, kb * jnp.exp(gc_c)], axis=1).astype(BF)
        qs[cc, hd], ks[cc, hd], gcs[cc, hd] = q, k, gc_c
    ts = _unit_lower_inverses(lows)
    uws = {p: _dot(t.astype(BF), rhs[p]) for p, t in zip(pairs, ts)}

    for cc, hd in pairs:
        r0, lo = cc * c, hd * HEAD_DIM
        q, k, gc_c, uw = qs[cc, hd], ks[cc, hd], gcs[cc, hd], uws[cc, hd]
        s = s_ref[hd]
        ws = _dot(jnp.concatenate([uw[:, HEAD_DIM:], q * jnp.exp(gc_c)], axis=0).astype(BF), s.astype(BF))
        v_new = (uw[:, 0:HEAD_DIM] - ws[0:c]).astype(BF)
        o = ws[c:] + _dot(intras[cc, hd], v_new)
        g_last = gc_c[c - 1:c, :]
        kd = k * jnp.exp(g_last - gc_c)
        s_ref[hd] = s * jnp.exp(g_last) + _dot(kd.T.astype(BF), v_new)
        o = _rmsnorm(o, gn_ref[...]) * _silu(z_ref[r0:r0 + c, lo:lo + HEAD_DIM])
        o_ref[r0:r0 + c, lo:lo + HEAD_DIM] = o.astype(BF)
    ssm_ref[...] = s_ref[...]


def _gdn_prompt(qkvn, z, misc, gct, g_norm, bp, tp, tg):
    nt = tp // tg
    rows = bp * tp
    rowblk = lambda w: pl.BlockSpec((tg, w), lambda b, i: (b * nt + i, 0))
    return pl.pallas_call(
        functools.partial(_gdn_prompt_kernel, tg),
        grid=(bp, nt),
        in_specs=[rowblk(QKV_W), rowblk(GDN_W), rowblk(LANES),
                  pl.BlockSpec((tg // GDN_CHUNK, 8, GDN_CHUNK), lambda b, i: (b * nt + i, 0, 0)),
                  pl.BlockSpec((1, HEAD_DIM), lambda b, i: (0, 0))],
        out_specs=[rowblk(GDN_W),
                   pl.BlockSpec((None, GDN_HEADS, HEAD_DIM, HEAD_DIM), lambda b, i: (b, 0, 0, 0))],
        out_shape=[jax.ShapeDtypeStruct((rows, GDN_W), BF),
                   jax.ShapeDtypeStruct((bp, GDN_HEADS, HEAD_DIM, HEAD_DIM), F32)],
        scratch_shapes=[pltpu.VMEM((GDN_HEADS, HEAD_DIM, HEAD_DIM), F32)],
        compiler_params=_params(("arbitrary", "arbitrary")),
        name="gdn_prompt",
    )(qkvn, z, misc, gct, g_norm.reshape(1, HEAD_DIM))


def _gdn_sample_kernel(nb, qkvn_ref, z_ref, misc_ref, gn_ref, s_ref, o_ref, ssm_ref):
    for bi in range(nb):
        for hd in range(GDN_HEADS):
            lo = hd * HEAD_DIM
            q = qkvn_ref[bi:bi + 1, lo:lo + HEAD_DIM]
            k = qkvn_ref[bi:bi + 1, GDN_W + lo:GDN_W + lo + HEAD_DIM]
            v = qkvn_ref[bi:bi + 1, 2 * GDN_W + lo:2 * GDN_W + lo + HEAD_DIM]
            beta = misc_ref[bi:bi + 1, hd:hd + 1]
            g = misc_ref[bi:bi + 1, GDN_HEADS + hd:GDN_HEADS + hd + 1]
            kcol = jnp.broadcast_to(k, (HEAD_DIM, HEAD_DIM)).T
            qcol = jnp.broadcast_to(q, (HEAD_DIM, HEAD_DIM)).T
            s = s_ref[bi, hd] * jnp.exp(g)
            ks = jnp.sum(kcol * s, axis=0, keepdims=True)
            delta = (v - ks) * beta
            s = s + kcol * delta
            ssm_ref[bi, hd] = s
            o = jnp.sum(qcol * s, axis=0, keepdims=True)
            o = _rmsnorm(o, gn_ref[...]) * _silu(z_ref[bi:bi + 1, lo:lo + HEAD_DIM])
            o_ref[bi:bi + 1, lo:lo + HEAD_DIM] = o.astype(BF)


def _gdn_sample(qkvn, z, misc, g_norm, state, nb):
    db = qkvn.shape[0]
    rowblk = lambda w: pl.BlockSpec((nb, w), lambda i: (i, 0))
    sblk = pl.BlockSpec((nb, GDN_HEADS, HEAD_DIM, HEAD_DIM), lambda i: (i, 0, 0, 0))
    return pl.pallas_call(
        functools.partial(_gdn_sample_kernel, nb),
        grid=(db // nb,),
        in_specs=[rowblk(QKV_W), rowblk(GDN_W), rowblk(LANES), pl.BlockSpec((1, HEAD_DIM), lambda i: (0, 0)), sblk],
        out_specs=[rowblk(GDN_W), sblk],
        out_shape=[jax.ShapeDtypeStruct((db, GDN_W), BF), jax.ShapeDtypeStruct(state.shape, F32)],
        compiler_params=_params(("arbitrary",)),
        name="gdn_sample",
    )(qkvn, z, misc, g_norm.reshape(1, HEAD_DIM), state)


def _ordered_word_to_float(u):
    s = u ^ INT_MIN
    return lax.bitcast_convert_type(s ^ ((s >> 31) & np.int32(0x7FFFFFFF)), F32)


def _select_topk_bias(i_ref, nch, tk, k_sel, taken=0.0, dropped=NEG_INF):
    nl = i_ref.shape[1]

    def chunk(c):
        return pl.ds(pl.multiple_of(c * tk, tk), tk)

    def count(ref, pred):
        def body(c, acc):
            m = pred(ref[chunk(c), :])
            ones = jnp.where(m, 1, 0).astype(I32).reshape(tk // 8, 8, nl)
            parts = [ones[n] for n in range(tk // 8)]
            while len(parts) > 1:
                parts = [a + b for a, b in zip(parts[0::2], parts[1::2])] + parts[len(parts) & ~1:]
            return acc + parts[0]
        acc = lax.fori_loop(0, nch, body, jnp.zeros((8, nl), I32))
        return jnp.sum(acc, axis=0, keepdims=True)

    def bit_body(i, carry):
        t_u, n_ge = carry
        cand_u = t_u | jnp.left_shift(jnp.int32(1), 31 - i)
        cand = _ordered_word_to_float(cand_u)
        cnt = count(i_ref, lambda sc: sc >= cand)
        take = cnt >= k_sel
        return jnp.where(take, cand_u, t_u), jnp.where(take, cnt, n_ge)

    t_u, n_ge = lax.fori_loop(0, 32, bit_body, (jnp.zeros((1, nl), I32), jnp.full((1, nl), -1, I32)))
    thr = jnp.where((t_u >> 23) == 0, NEG_INF, _ordered_word_to_float(t_u))

    excess = (n_ge - k_sel).astype(F32)
    ri = lax.broadcasted_iota(I32, (tk + 8, tk), 0)
    ci = lax.broadcasted_iota(I32, (tk + 8, tk), 1)
    above = jnp.where((ci > ri) | (ri >= tk), 1.0, 0.0).astype(BF)

    def write(n, later):
        c = nch - 1 - n
        sc = i_ref[chunk(c), :]
        tie = sc == thr
        cnt = _dot(above, jnp.where(tie, 1.0, 0.0).astype(BF))
        sel = (sc > thr) | (tie & (cnt[0:tk] + later >= excess))
        i_ref[chunk(c), :] = jnp.where(sel & (sc > NEG_INF), taken, dropped)
        return later + cnt[tk:tk + 1]

    lax.fori_loop(0, nch, write, jnp.zeros((1, nl), F32))


def _attn_prompt_kernel(tq, tk, blk, k_sel, qT_ref, iqT_ref, iwT_ref, kbf_ref, vT_ref, ikbf_ref, o_ref, i_ref):
    t0 = pl.program_id(1) * tq
    nch = (t0 + tq + tk - 1) // tk
    nblk = (t0 + tq + blk - 1) // blk
    qpos = t0 + lax.broadcasted_iota(I32, (1, tq), 1)
    row = lax.broadcasted_iota(I32, (tk, 1), 0)
    w = iwT_ref[...]
    pairs = IDX_HEADS // 2
    rhs = [jnp.concatenate([iqT_ref[(2 * p) * IDX_DIM:(2 * p + 1) * IDX_DIM, :],
                            iqT_ref[(2 * p + 1) * IDX_DIM:(2 * p + 2) * IDX_DIM, :]], axis=1) for p in range(pairs)]

    def chunk(c):
        return pl.ds(pl.multiple_of(c * tk, tk), tk)

    def idx_body(c, carry):
        ikc = ikbf_ref[chunk(c), :]
        acc = jnp.zeros((tk, tq), F32)
        for p in range(pairs):
            d = jnp.maximum(_dot(ikc, rhs[p]), 0.0)
            acc = acc + d[:, 0:tq] * w[2 * p:2 * p + 1, :] + d[:, tq:] * w[2 * p + 1:2 * p + 2, :]
        i_ref[chunk(c), :] = jnp.where(c * tk + row <= qpos, acc, NEG_INF)
        return carry

    lax.fori_loop(0, nch, idx_body, 0)

    def fill_body(c, carry):
        i_ref[chunk(c), :] = jnp.full((tk, tq), NEG_INF, F32)
        return carry

    lax.fori_loop(nch, nblk * (blk // tk), fill_body, 0)
    tks = max(tk * LANES // tq, 8)
    _select_topk_bias(i_ref, (t0 + tq + tks - 1) // tks, tks, k_sel)

    scale2 = HEAD_DIM ** -0.5 * math.log2(math.e)
    rep = ATT_HEADS // N_KV
    sub = ATT_TK
    nsub = blk // sub
    qg = [jnp.concatenate([qT_ref[(rep * g + r) * HEAD_DIM:(rep * g + r + 1) * HEAD_DIM, :] for r in range(rep)],
                          axis=1) for g in range(N_KV)]

    def body(c, carry):
        out = []
        for g in range(N_KV):
            m, acc = carry[g]
            ss = []
            for i in range(nsub):
                rows = pl.ds(pl.multiple_of(c * blk + i * sub, sub), sub)
                bias = i_ref[rows, :]
                ss.append(_dot(kbf_ref[rows, g * HEAD_DIM:(g + 1) * HEAD_DIM], qg[g]) * scale2
                          + jnp.concatenate([bias] * rep, axis=1))
            m_new = m
            for s in ss:
                m_new = jnp.maximum(m_new, jnp.max(s, axis=0, keepdims=True))
            m_safe = jnp.where(m_new == NEG_INF, 0.0, m_new)
            acc = acc * jnp.exp2(m - m_safe)
            for i, s in enumerate(ss):
                p = jnp.exp2(s - m_safe).astype(BF)
                acc = acc + _dot(vT_ref[c * nsub + i, g * VT_ROWS:(g + 1) * VT_ROWS, :], p)
            out.append((m_new, acc))
        return tuple(out)

    init = tuple((jnp.full((1, rep * tq), NEG_INF, F32), jnp.zeros((VT_ROWS, rep * tq), F32)) for _ in range(N_KV))
    res = lax.fori_loop(0, nblk, body, init)
    for g in range(N_KV):
        acc = res[g][1]
        o_t = acc[0:HEAD_DIM] / acc[HEAD_DIM:HEAD_DIM + 1]
        for r in range(rep):
            hd = rep * g + r
            o_ref[:, hd * HEAD_DIM:(hd + 1) * HEAD_DIM] = o_t[:, r * tq:(r + 1) * tq].T.astype(BF)


def _attn_prompt(qT, iqT, iwT, kbf, vT, ikbf, bp, tp):
    tq = _pick(tp, (ATT_TQ, QBLOCK))
    tk = _pick(tp, (IDX_TK, ATT_TK))
    blk = _pick(tp, (ATT_BLK, IDX_TK, ATT_TK))
    assert blk % tk == 0 and blk % ATT_TK == 0
    nq = tp // tq
    rows = bp * tp
    k_sel = min(TOPK_MAX, tp // 4)
    colblk = lambda h: pl.BlockSpec((h, tq), lambda b, j: (0, b * nq + j))
    return pl.pallas_call(
        functools.partial(_attn_prompt_kernel, tq, tk, blk, k_sel),
        grid=(bp, nq),
        in_specs=[colblk(ATT_W), colblk(IDX_W), colblk(IDX_HEADS),
                  pl.BlockSpec((tp, KV_W), lambda b, j: (b, 0)),
                  pl.BlockSpec((tp // ATT_TK, N_KV * VT_ROWS, ATT_TK), lambda b, j: (b, 0, 0)),
                  pl.BlockSpec((tp, IDX_DIM), lambda b, j: (b, 0))],
        out_specs=pl.BlockSpec((tq, ATT_W), lambda b, j: (b * nq + j, 0)),
        out_shape=jax.ShapeDtypeStruct((rows, ATT_W), BF),
        scratch_shapes=[pltpu.VMEM((tp, tq), F32)],
        compiler_params=_params(("arbitrary", "arbitrary")),
        name="attn_prompt",
    )(qT, iqT, iwT, kbf, vT, ikbf)


def _page_prefetch(npg, pt_ref, srcs, bufs, sem_ref):
    b = pl.program_id(0)
    steps = pl.num_programs(0)
    ring = bufs[0].shape[0]
    ahead = ring - 1
    slot = lax.rem(b, ring)

    def copy(n, step, sl, p):
        return pltpu.make_async_copy(srcs[n].at[pt_ref[step, p]], bufs[n].at[sl, p], sem_ref.at[n, sl, p])

    def start_all(step, sl):
        for n in range(len(srcs)):
            for p in range(npg):
                copy(n, step, sl, p).start()

    @pl.when(b == 0)
    def _():
        for step in range(ahead):
            start_all(step, step)

    @pl.when(b + ahead < steps)
    def _():
        start_all(b + ahead, lax.rem(b + ahead, ring))

    for n in range(len(srcs)):
        for p in range(npg):
            copy(n, b, slot, p).wait()
    return slot


def _sidx_kernel(npg, pt_ref, iq_ref, iw_ref, ikn_ref, cache_ref, out_ref, buf_ref, sem_ref):
    slot = _page_prefetch(npg, pt_ref, [cache_ref], [buf_ref], sem_ref)
    iq = iq_ref[...].astype(BF)
    w = iw_ref[...]
    for p in range(npg):
        d = jnp.maximum(_dot(iq, buf_ref[slot, p].astype(BF)), 0.0)
        out_ref[:, p * LANES:(p + 1) * LANES] = jnp.sum(d * w, axis=0, keepdims=True)
    dn = jnp.sum(iq.astype(F32) * ikn_ref[...].astype(BF).astype(F32), axis=1, keepdims=True)
    sn = jnp.sum(jnp.maximum(dn, 0.0) * w, axis=0, keepdims=True)
    lane = lax.broadcasted_iota(I32, (1, LANES), 1)
    out_ref[:, npg * LANES:(npg + 1) * LANES] = jnp.where(lane == 0, sn, NEG_INF)


def _sample_index_scores(iq, iw, ik_new, cache_ik_t, page_table):
    db, npg = page_table.shape
    page = cache_ik_t.shape[2]
    assert page == LANES and db > IDX_RING
    width = (npg + 1) * LANES
    grid_spec = pltpu.PrefetchScalarGridSpec(
        num_scalar_prefetch=1, grid=(db,),
        in_specs=[pl.BlockSpec((None, IDX_HEADS, IDX_DIM), lambda b, pt: (b, 0, 0)),
                  pl.BlockSpec((None, IDX_HEADS, 1), lambda b, pt: (b, 0, 0)),
                  pl.BlockSpec((None, 1, IDX_DIM), lambda b, pt: (b, 0, 0)),
                  pl.BlockSpec(memory_space=pl.ANY)],
        out_specs=pl.BlockSpec((None, 1, width), lambda b, pt: (b, 0, 0)),
        scratch_shapes=[pltpu.VMEM((IDX_RING, npg, IDX_DIM, page), F32),
                        pltpu.SemaphoreType.DMA((1, IDX_RING, npg))])
    out = pl.pallas_call(
        functools.partial(_sidx_kernel, npg), grid_spec=grid_spec,
        out_shape=jax.ShapeDtypeStruct((db, 1, width), F32),
        compiler_params=_params(("arbitrary",)),
        name="sample_index_scores",
    )(page_table, iq.reshape(db, IDX_HEADS, IDX_DIM), iw.reshape(db, IDX_HEADS, 1), ik_new.reshape(db, 1, IDX_DIM),
      cache_ik_t)
    return out.reshape(db, width)


def _ssel_kernel(nch, k_sel, s_ref, spread_ref, keep_ref, i_ref):
    for p in range(nch):
        i_ref[p * LANES:(p + 1) * LANES, :] = s_ref[:, p * LANES:(p + 1) * LANES].T
    _select_topk_bias(i_ref, nch, LANES, k_sel, taken=1.0, dropped=0.0)
    rows = N_KV * LANES
    for p in range(nch - 1):
        k2 = _dot(spread_ref[...], i_ref[p * LANES:(p + 1) * LANES, :].astype(BF))
        for h in range(N_KV):
            keep_ref[:, p * rows + h * LANES:p * rows + (h + 1) * LANES] = k2[h * LANES:(h + 1) * LANES, :].T
    keep_ref[:, (nch - 1) * rows:(nch - 1) * rows + LANES] = i_ref[(nch - 1) * LANES:nch * LANES, :].T


def _sample_select(scores, n_keys):
    db, width = scores.shape
    assert db == LANES
    nch = width // LANES
    k_sel = min(TOPK_MAX, n_keys // 4)
    rows = N_KV * LANES
    out_w = (nch - 1) * rows + LANES
    spread = jnp.asarray(np.arange(rows)[:, None] // N_KV == np.arange(LANES)[None, :], BF)
    return pl.pallas_call(
        functools.partial(_ssel_kernel, nch, k_sel),
        grid=(1,),
        in_specs=[pl.BlockSpec((db, width), lambda i: (0, 0)), pl.BlockSpec((rows, LANES), lambda i: (0, 0))],
        out_specs=pl.BlockSpec((db, out_w), lambda i: (0, 0)),
        out_shape=jax.ShapeDtypeStruct((db, out_w), F32),
        scratch_shapes=[pltpu.VMEM((width, db), F32)],
        compiler_params=_params(("arbitrary",)),
        name="sample_select",
    )(scores, spread)


def _sattn_kernel(npg, pt_ref, q_ref, keep_ref, kn_ref, vn_ref, ck_ref, cv_ref, o_ref, kbuf_ref, vbuf_ref, sem_ref):
    slot = _page_prefetch(npg, pt_ref, [ck_ref, cv_ref], [kbuf_ref, vbuf_ref], sem_ref)
    rep = ATT_HEADS // N_KV
    rows = N_KV * LANES
    scale = HEAD_DIM ** -0.5
    q = q_ref[...]
    q8 = jnp.concatenate([q, jnp.zeros((8 - ATT_HEADS, HEAD_DIM), F32)], axis=0).astype(BF)
    hrow = lax.broadcasted_iota(I32, (8, 1), 0)
    lane = lax.broadcasted_iota(I32, (1, LANES), 1)
    own_kv = lax.broadcasted_iota(I32, (1, rows), 1) % N_KV == hrow // rep
    parts = []
    for p in range(npg):
        s2 = _dot_nt(q8, kbuf_ref[slot, p].astype(BF))
        keep2 = keep_ref[:, p * rows:(p + 1) * rows]
        parts.append(jnp.where(own_kv & (keep2 > 0.5), s2 * scale, NEG_INF))
    kn = kn_ref[...].astype(BF).astype(F32)
    kn8 = jnp.where(hrow // rep == 0, kn[0:1, :], kn[1:2, :])
    s_new = jnp.sum(q8.astype(F32) * kn8, axis=1, keepdims=True)
    keep_new = keep_ref[:, npg * rows:npg * rows + 1]
    parts.append(jnp.where((lane == 0) & (keep_new > 0.5), s_new * scale, NEG_INF))
    s = jnp.concatenate(parts, axis=1)
    m = jnp.max(s, axis=1, keepdims=True)
    e = jnp.exp(s - m)
    pr = (e / jnp.sum(e, axis=1, keepdims=True)).astype(BF)
    o8 = jnp.zeros((8, HEAD_DIM), F32)
    for p in range(npg):
        o8 = o8 + _dot(pr[:, p * rows:(p + 1) * rows], vbuf_ref[slot, p].astype(BF))
    vn = vn_ref[...].astype(BF).astype(F32)
    vn8 = jnp.where(hrow // rep == 0, vn[0:1, :], vn[1:2, :])
    o8 = o8 + pr[:, npg * rows:npg * rows + 1].astype(F32) * vn8
    for hd in range(ATT_HEADS):
        o_ref[:, hd * HEAD_DIM:(hd + 1) * HEAD_DIM] = o8[hd:hd + 1, :].astype(BF)


def _sample_attention(q, keep, k_new, v_new, cache_k, cache_v, page_table):
    db, npg = page_table.shape
    n_phys, page = cache_k.shape[0], cache_k.shape[1]
    assert page == LANES and N_KV == 2 and db > KV_RING
    rows = page * N_KV
    width = npg * rows + LANES
    ck = cache_k.reshape(n_phys, rows, HEAD_DIM)
    cv = cache_v.reshape(n_phys, rows, HEAD_DIM)
    grid_spec = pltpu.PrefetchScalarGridSpec(
        num_scalar_prefetch=1, grid=(db,),
        in_specs=[pl.BlockSpec((None, ATT_HEADS, HEAD_DIM), lambda b, pt: (b, 0, 0)),
                  pl.BlockSpec((None, 1, width), lambda b, pt: (b, 0, 0)),
                  pl.BlockSpec((None, N_KV, HEAD_DIM), lambda b, pt: (b, 0, 0)),
                  pl.BlockSpec((None, N_KV, HEAD_DIM), lambda b, pt: (b, 0, 0)),
                  pl.BlockSpec(memory_space=pl.ANY), pl.BlockSpec(memory_space=pl.ANY)],
        out_specs=pl.BlockSpec((None, 1, ATT_W), lambda b, pt: (b, 0, 0)),
        scratch_shapes=[pltpu.VMEM((KV_RING, npg, rows, HEAD_DIM), F32), pltpu.VMEM((KV_RING, npg, rows, HEAD_DIM), F32),
                        pltpu.SemaphoreType.DMA((2, KV_RING, npg))])
    out = pl.pallas_call(
        functools.partial(_sattn_kernel, npg), grid_spec=grid_spec,
        out_shape=jax.ShapeDtypeStruct((db, 1, ATT_W), BF),
        compiler_params=_params(("arbitrary",)),
        name="sample_attention",
    )(page_table, q.reshape(db, ATT_HEADS, HEAD_DIM), keep.reshape(db, 1, width),
      k_new.reshape(db, N_KV, HEAD_DIM), v_new.reshape(db, N_KV, HEAD_DIM), ck, cv)
    return out.reshape(db, ATT_W)


def _post_kernel(final, nff, og_ref, oa_ref, x_ref, ga1_ref, sc2_ref, sh2_ref, ga2_ref, g2_ref, gf_ref,
                 wo_ref, wg_ref, wu_ref, wd_ref, y_ref, x1_ref, h2_ref, acc_ref):
    jf = pl.program_id(1)

    @pl.when(jf == 0)
    def _():
        mixed = _dot(og_ref[...], wo_ref[0:GDN_W, :]) + _dot(oa_ref[...], wo_ref[GDN_W:, :])
        x1 = x_ref[...] + ga1_ref[...] * mixed
        x1_ref[...] = x1
        h2_ref[...] = (_rmsnorm(x1, g2_ref[...]) * (1.0 + sc2_ref[...]) + sh2_ref[...]).astype(BF)
        acc_ref[...] = jnp.zeros(acc_ref.shape, F32)

    h2 = h2_ref[...]
    act = (_silu(_dot(h2, wg_ref[...])) * _dot(h2, wu_ref[...])).astype(BF)
    acc_ref[...] += _dot(act, wd_ref[...])

    @pl.when(jf == nff - 1)
    def _():
        x2 = x1_ref[...] + ga2_ref[...] * acc_ref[...]
        y_ref[...] = _rmsnorm(x2, gf_ref[...]) if final else x2


def _post(og, oa, x2d, mod, mod_spec, g2, gf, wo, wfi, wfo, tm, tf, final):
    rows, d = x2d.shape
    dff = wfo.shape[0]
    nff = dff // tf
    rowblk = lambda w: pl.BlockSpec((tm, w), lambda i, j: (i, 0))
    const = lambda shape: pl.BlockSpec(shape, lambda i, j: (0,) * len(shape))
    in_specs = [rowblk(GDN_W), rowblk(ATT_W), rowblk(d),
                mod_spec(2), mod_spec(4), mod_spec(3), mod_spec(5),
                const((1, d)), const((1, d)), const((d, d)),
                pl.BlockSpec((d, tf), lambda i, j: (0, j)),
                pl.BlockSpec((d, tf), lambda i, j: (0, nff + j)),
                pl.BlockSpec((tf, d), lambda i, j: (j, 0))]
    return pl.pallas_call(
        functools.partial(_post_kernel, final, nff),
        grid=(rows // tm, nff), in_specs=in_specs, out_specs=rowblk(d),
        out_shape=jax.ShapeDtypeStruct((rows, d), F32),
        scratch_shapes=[pltpu.VMEM((tm, d), F32), pltpu.VMEM((tm, d), BF), pltpu.VMEM((tm, d), F32)],
        compiler_params=_params(("arbitrary", "arbitrary")),
        name="post",
    )(og, oa, x2d, mod, mod, mod, mod, g2.reshape(1, d), gf.reshape(1, d), wo, wfi, wfi, wfo)


def _pick(n, prefs):
    for p in prefs:
        if n % p == 0:
            return p
    return n


def kernel(x_prompt, x_sample, c_prompt, c_sample, cache_k, cache_v, cache_idx_k, page_table, state_conv, state_ssm,
           w_ada, b_ada, g_norm1, w_in, w_conv, a_log, dt_bias, g_gdn_norm, w_out, g_norm2, w_ffn_in, w_ffn_out,
           g_final):
    bp, tp, d = x_prompt.shape
    db, ts, _ = x_sample.shape
    assert ts == 1 and d == GDN_W + ATT_W and tp % max(ATT_TK, GDN_CHUNK) == 0
    depth = w_in.shape[0]
    npg, page = page_table.shape[1], cache_k.shape[2]
    past = npg * page
    tm_in = _pick(tp, (512, 256))
    tg = _pick(tp, (512, 256))
    tm_post = _pick(tp, (1024, 512, 256))
    dff = w_ffn_out.shape[1]
    tf = _pick(dff, (256, 128))

    xp = x_prompt
    xs = x_sample.reshape(db, d)
    npad = (-(bp + db)) % 8
    c_all = jnp.concatenate([c_prompt, c_sample, jnp.zeros((npad, d), F32)], axis=0)
    new_p, new_s = [], []
    for l in range(depth):
        final = l == depth - 1
        mod = _modulation(c_all, w_ada[l], b_ada[l])
        mod_p = mod[:bp].reshape(bp, 1, 6 * d)
        mod_s = mod[bp:bp + db]
        w_nat, w_t, prow, pcol = _inproj_weights(w_in[l], a_log[l], dt_bias[l])
        wo = w_out[l].astype(BF)
        wfi = w_ffn_in[l].astype(BF)
        wfo = w_ffn_out[l].astype(BF)

        (qkvn, z, vnat, knat, ikT, tail, misc, gct, qT, iqT, iwT, kbf, vT, ikbf) = _inproj_prompt(
            xp, mod_p, g_norm1[l], w_nat, w_t, w_conv[l], prow, pcol, tm_in)
        og, ssm_p = _gdn_prompt(qkvn, z, misc, gct, g_gdn_norm[l], bp, tp, tg)
        oa = _attn_prompt(qT, iqT, iwT, kbf, vT, ikbf, bp, tp)
        tiles_b = tp // tm_post
        spec_p = lambda k: pl.BlockSpec((None, 1, d), lambda i, j: (i // tiles_b, 0, k))
        xp = _post(og, oa, xp.reshape(bp * tp, d), mod_p, spec_p, g_norm2[l], g_final, wo, wfi, wfo,
                   tm_post, tf, final).reshape(bp, tp, d)
        new_p.append((knat.reshape(bp, tp, N_KV, HEAD_DIM), vnat.reshape(bp, tp, N_KV, HEAD_DIM),
                      jnp.swapaxes(ikT, 1, 2), tail[:, 8 - (CONV_W - 1):, :], ssm_p))

        (qkvn_s, z_s, vnat_s, knat_s, slab_s, raw_s, misc_s, aq_s, iq_s) = _inproj_sample(
            xs, mod_s, g_norm1[l], w_nat, w_t, w_conv[l], prow, pcol, state_conv[l], past)
        og_s, ssm_s = _gdn_sample(qkvn_s, z_s, misc_s, g_gdn_norm[l], state_ssm[l], _pick(db, (8,)))
        ik_s = slab_s[:, 0:IDX_DIM]
        iw_s = slab_s[:, IDX_DIM:IDX_DIM + IDX_HEADS]
        scores = _sample_index_scores(iq_s, iw_s, ik_s, jnp.swapaxes(cache_idx_k[l], 1, 2), page_table)
        keep = _sample_select(scores, past + ts)
        oa_s = _sample_attention(aq_s, keep, knat_s, vnat_s, cache_k[l], cache_v[l], page_table)
        spec_s = lambda k: pl.BlockSpec((db, d), lambda i, j: (i, k))
        xs = _post(og_s, oa_s, xs, mod_s, spec_s, g_norm2[l], g_final, wo, wfi, wfo, db, tf, final)
        new_conv_s = jnp.concatenate([state_conv[l][:, 1:, :], raw_s[:, None, :]], axis=1)
        new_s.append((knat_s.reshape(db, ts, N_KV, HEAD_DIM), vnat_s.reshape(db, ts, N_KV, HEAD_DIM),
                      ik_s.reshape(db, ts, IDX_DIM), new_conv_s, ssm_s))

    stack = lambda states, n: jnp.stack([s[n] for s in states], axis=0)
    return (xp, xs.reshape(db, ts, d),
            stack(new_p, 0), stack(new_p, 1), stack(new_p, 2), stack(new_p, 3), stack(new_p, 4),
            stack(new_s, 0), stack(new_s, 1), stack(new_s, 2), stack(new_s, 3), stack(new_s, 4))
```

```python
import functools
import math

import numpy as np
import jax
import jax.numpy as jnp
from jax import lax
from jax.experimental import pallas as pl
from jax.experimental.pallas import tpu as pltpu

F32 = jnp.float32
BF = jnp.bfloat16
I32 = jnp.int32

HEAD_DIM = 128
GDN_HEADS = 4
ATT_HEADS = 4
N_KV = 2
IDX_HEADS = 8
IDX_DIM = 64
CONV_W = 4
TOPK_MAX = 256
QBLOCK = 128
ROPE_THETA = 500000.0
EPS = 1e-6
GDN_W = GDN_HEADS * HEAD_DIM
ATT_W = ATT_HEADS * HEAD_DIM
KV_W = N_KV * HEAD_DIM
IDX_W = IDX_HEADS * IDX_DIM
QKV_W = 3 * GDN_W

GDN_CHUNK = 128
ATT_TK = 256
VT_ROWS = HEAD_DIM + 16
IDX_TK = 512
ATT_BLK = 1024
ATT_TQ = 256
IDX_RING = 5
KV_RING = 3
LANES = 128
NAT_W = QKV_W + GDN_W + KV_W + LANES
T_AQ, T_AK, T_IQ, T_IK, T_MISC, T_AV = 0, 512, 768, 1280, 1344, 1360
T_ROWS = T_AV + KV_W
VMEM_LIMIT = 56 * 1024 * 1024
INT_MIN = np.int32(-2 ** 31)
NEG_INF = float("-inf")


def _params(sem):
    return pltpu.CompilerParams(dimension_semantics=sem, vmem_limit_bytes=VMEM_LIMIT)


def _dot(a, b):
    return jnp.dot(a, b, preferred_element_type=F32)


def _dot_nt(a, b):
    return lax.dot_general(a, b, (((1,), (1,)), ((), ())), preferred_element_type=F32)


def _split3(x):
    hi = x.astype(BF)
    r1 = x - hi.astype(F32)
    mid = r1.astype(BF)
    lo = (r1 - mid.astype(F32)).astype(BF)
    return hi, mid, lo


def _mm_hi(a, b):
    ah = a.astype(BF)
    al = (a - ah.astype(F32)).astype(BF)
    bh = b.astype(BF)
    bl = (b - bh.astype(F32)).astype(BF)
    m = a.shape[0]
    hi = _dot(jnp.concatenate([ah, al], axis=0), bh)
    return hi[0:m] + (hi[m:] + _dot(ah, bl))


def _sigmoid(x):
    return 1.0 / (1.0 + jnp.exp(-x))


def _silu(x):
    return x * _sigmoid(x)


def _softplus(x):
    return jnp.maximum(x, 0.0) + jnp.log(1.0 + jnp.exp(-jnp.abs(x)))


def _rmsnorm(x, g):
    return x * lax.rsqrt(jnp.mean(x * x, axis=-1, keepdims=True) + EPS) * g


def _mod_kernel(c_ref, w_ref, b_ref, o_ref):
    s = _silu(c_ref[...]).astype(BF)
    o_ref[...] = _dot(s, w_ref[...].astype(BF)) + b_ref[...]


def _modulation(c_all, w_ada, b_ada):
    n, d = c_all.shape
    cols = w_ada.shape[1]
    tn = d
    return pl.pallas_call(
        _mod_kernel,
        grid=(cols // tn,),
        in_specs=[pl.BlockSpec((n, d), lambda j: (0, 0)),
                  pl.BlockSpec((d, tn), lambda j: (0, j)),
                  pl.BlockSpec((1, tn), lambda j: (0, j))],
        out_specs=pl.BlockSpec((n, tn), lambda j: (0, j)),
        out_shape=jax.ShapeDtypeStruct((n, cols), F32),
        compiler_params=_params(("arbitrary",)),
        name="modulation",
    )(c_all, w_ada, b_ada.reshape(1, cols))


def _rope_rows(rt_ref, base, half, cos, sin):
    x1 = rt_ref[base:base + half, :]
    x2 = rt_ref[base + half:base + 2 * half, :]
    rt_ref[base:base + half, :] = x1 * cos - x2 * sin
    rt_ref[base + half:base + 2 * half, :] = x2 * cos + x1 * sin


def _project(x_ref, sc_ref, sh_ref, g1_ref, wn_ref, wt_ref, cs128_ref, cs64_ref, pcol_ref, rt_ref):
    h = _rmsnorm(x_ref[...], g1_ref[...]) * (1.0 + sc_ref[...]) + sh_ref[...]
    hb = h.astype(BF)
    nat = _dot(hb, wn_ref[...])
    rt_ref[...] = _dot_nt(wt_ref[...], hb)
    half = HEAD_DIM // 8
    cos, sin = cs128_ref[0:half, :], cs128_ref[half:2 * half, :]
    for hd in range(ATT_HEADS):
        _rope_rows(rt_ref, T_AQ + hd * HEAD_DIM, half, cos, sin)
    for hd in range(N_KV):
        _rope_rows(rt_ref, T_AK + hd * HEAD_DIM, half, cos, sin)
    half = IDX_DIM // 8
    cos, sin = cs64_ref[0:half, :], cs64_ref[half:2 * half, :]
    for hd in range(IDX_HEADS):
        _rope_rows(rt_ref, T_IQ + hd * IDX_DIM, half, cos, sin)
    _rope_rows(rt_ref, T_IK, half, cos, sin)
    mt = rt_ref[T_MISC:T_MISC + 16, :]
    r = lax.broadcasted_iota(I32, mt.shape, 0)
    a_log, dt_b = pcol_ref[:, 0:1], pcol_ref[:, 1:2]
    gate = -jnp.exp(a_log) * _softplus(mt + dt_b)
    mt = jnp.where(r < IDX_HEADS, mt * IDX_HEADS ** -0.5, jnp.where(r < IDX_HEADS + GDN_HEADS, _sigmoid(mt), gate))
    rt_ref[T_MISC:T_MISC + 16, :] = mt
    return nat, mt


def _misc_natural(m, prow_ref):
    lane = lax.broadcasted_iota(I32, m.shape, 1)
    gate = -jnp.exp(prow_ref[0:1, :]) * _softplus(m + prow_ref[1:2, :])
    return jnp.where(lane < GDN_HEADS, _sigmoid(m), jnp.where(lane < 3 * GDN_HEADS, gate, 0.0))


def _qkv_post(conv, qkvn_ref):
    c = _silu(conv)
    for hb in range(2 * GDN_HEADS):
        xh = c[:, hb * HEAD_DIM:(hb + 1) * HEAD_DIM]
        n = xh * lax.rsqrt(jnp.sum(xh * xh, axis=-1, keepdims=True) + EPS)
        if hb < GDN_HEADS:
            n = n * HEAD_DIM ** -0.5
        qkvn_ref[:, hb * HEAD_DIM:(hb + 1) * HEAD_DIM] = n
    qkvn_ref[:, 2 * GDN_W:] = c[:, 2 * GDN_W:]


def _inproj_prompt_kernel(tm, parts, x_ref, sc_ref, sh_ref, g1_ref, wn_ref, wt_ref, wconv_ref, cs128_ref, cs64_ref,
                          prow_ref, pcol_ref, tril_ref,
                          qkvn_ref, z_ref, vnat_ref, knat_ref, ikT_ref, tail_ref, misc_ref, gct_ref,
                          qT_ref, iqT_ref, iwT_ref, kbf_ref, vT_ref, ikbf_ref,
                          xs_ref, rt_ref):
    i = pl.program_id(1)

    @pl.when(i == 0)
    def _():
        xs_ref[0:8, :] = jnp.zeros((8, QKV_W), F32)

    @pl.when(i > 0)
    def _():
        xs_ref[0:8, :] = xs_ref[tm:tm + 8, :]

    th = tm // parts
    for n in range(parts):
        rows, cols = pl.ds(n * th, th), pl.ds(n * th, th)
        _inproj_prompt_rows(
            th, x_ref.at[rows, :], sc_ref, sh_ref, g1_ref, wn_ref, wt_ref, wconv_ref,
            cs128_ref.at[:, cols], cs64_ref.at[:, cols], prow_ref, pcol_ref, tril_ref.at[0:th, 0:th],
            qkvn_ref.at[rows, :], z_ref.at[rows, :], vnat_ref.at[pl.ds(n * th * N_KV, th * N_KV), :],
            knat_ref.at[pl.ds(n * th * N_KV, th * N_KV), :], ikT_ref.at[:, cols], tail_ref, misc_ref.at[rows, :],
            gct_ref.at[pl.ds(n * (th // GDN_CHUNK), th // GDN_CHUNK)],
            qT_ref.at[:, cols], iqT_ref.at[:, cols], iwT_ref.at[:, cols], kbf_ref.at[rows, :],
            vT_ref.at[pl.ds(n * (th // ATT_TK), th // ATT_TK)], ikbf_ref.at[rows, :],
            xs_ref.at[pl.ds(n * th, th + 8), :], rt_ref.at[:, cols])


def _inproj_prompt_rows(tm, x_ref, sc_ref, sh_ref, g1_ref, wn_ref, wt_ref, wconv_ref, cs128_ref, cs64_ref,
                        prow_ref, pcol_ref, tril_ref,
                        qkvn_ref, z_ref, vnat_ref, knat_ref, ikT_ref, tail_ref, misc_ref, gct_ref,
                        qT_ref, iqT_ref, iwT_ref, kbf_ref, vT_ref, ikbf_ref,
                        xs_ref, rt_ref):
    nat, mt = _project(x_ref, sc_ref, sh_ref, g1_ref, wn_ref, wt_ref, cs128_ref, cs64_ref, pcol_ref, rt_ref)
    xs_ref[8:tm + 8, :] = nat[:, 0:QKV_W]
    conv = wconv_ref[0:1, :] * xs_ref[5:tm + 5, :]
    for t in range(1, CONV_W):
        conv = conv + wconv_ref[t:t + 1, :] * xs_ref[5 + t:tm + 5 + t, :]
    tail_ref[...] = xs_ref[tm:tm + 8, :]
    _qkv_post(conv, qkvn_ref)
    z_ref[...] = nat[:, QKV_W:QKV_W + GDN_W]
    v0 = QKV_W + GDN_W
    for hd in range(N_KV):
        vnat_ref[pl.ds(hd, tm, stride=N_KV), :] = nat[:, v0 + hd * HEAD_DIM:v0 + (hd + 1) * HEAD_DIM]

    gm = _misc_natural(nat[:, NAT_W - LANES:], prow_ref)
    tril = tril_ref[...]
    hi, mid, lo = _split3(gm)
    gc = _dot(tril, hi) + (_dot(tril, mid) + _dot(tril, lo))
    lane = lax.broadcasted_iota(I32, gm.shape, 1)
    misc_ref[...] = jnp.where(lane < 2 * GDN_HEADS, gm, gc)
    hi, mid, lo = _split3(mt)
    gct = _dot_nt(hi, tril) + (_dot_nt(mid, tril) + _dot_nt(lo, tril))
    r = lax.broadcasted_iota(I32, mt.shape, 0)
    bg = jnp.where(r < IDX_HEADS + GDN_HEADS, mt, gct)[8:16, :]
    for cc in range(tm // GDN_CHUNK):
        gct_ref[cc] = bg[:, cc * GDN_CHUNK:(cc + 1) * GDN_CHUNK]

    iwT_ref[...] = mt[0:IDX_HEADS, :]
    qT_ref[...] = rt_ref[T_AQ:T_AQ + ATT_W, :].astype(BF)
    iqT_ref[...] = rt_ref[T_IQ:T_IQ + IDX_W, :].astype(BF)
    kn = rt_ref[T_AK:T_AK + KV_W, :].T
    for hd in range(N_KV):
        knat_ref[pl.ds(hd, tm, stride=N_KV), :] = kn[:, hd * HEAD_DIM:(hd + 1) * HEAD_DIM]
    kbf_ref[...] = kn.astype(BF)
    ikT_ref[...] = rt_ref[T_IK:T_IK + IDX_DIM, :]
    ikbf_ref[...] = rt_ref[T_IK:T_IK + LANES, :].T[:, 0:IDX_DIM].astype(BF)
    for cc in range(tm // ATT_TK):
        for hd in range(N_KV):
            lo = hd * VT_ROWS
            vT_ref[cc, lo:lo + HEAD_DIM, :] = rt_ref[T_AV + hd * HEAD_DIM:T_AV + (hd + 1) * HEAD_DIM,
                                                     cc * ATT_TK:(cc + 1) * ATT_TK].astype(BF)
            vT_ref[cc, lo + HEAD_DIM:lo + VT_ROWS, :] = jnp.ones((VT_ROWS - HEAD_DIM, ATT_TK), BF)


def _inproj_sample_kernel(x_ref, sc_ref, sh_ref, g1_ref, wn_ref, wt_ref, wconv_ref, cs128_ref, cs64_ref,
                          prow_ref, pcol_ref, s0_ref, s1_ref, s2_ref,
                          qkvn_ref, z_ref, vnat_ref, knat_ref, slab_ref, raw_ref, misc_ref, aq_ref, iq_ref,
                          rt_ref):
    nat, _ = _project(x_ref, sc_ref, sh_ref, g1_ref, wn_ref, wt_ref, cs128_ref, cs64_ref, pcol_ref, rt_ref)
    raw = nat[:, 0:QKV_W]
    raw_ref[...] = raw
    conv = wconv_ref[0:1, :] * s0_ref[...]
    conv = conv + wconv_ref[1:2, :] * s1_ref[...]
    conv = conv + wconv_ref[2:3, :] * s2_ref[...]
    conv = conv + wconv_ref[3:4, :] * raw
    _qkv_post(conv, qkvn_ref)
    z_ref[...] = nat[:, QKV_W:QKV_W + GDN_W]
    vnat_ref[...] = nat[:, QKV_W + GDN_W:QKV_W + GDN_W + KV_W]
    misc_ref[...] = _misc_natural(nat[:, NAT_W - LANES:], prow_ref)
    aq_ref[...] = rt_ref[T_AQ:T_AQ + ATT_W, :].T
    iq_ref[...] = rt_ref[T_IQ:T_IQ + IDX_W, :].T
    knat_ref[...] = rt_ref[T_AK:T_AK + KV_W, :].T
    slab_ref[...] = rt_ref[T_IK:T_IK + LANES, :].T


def _inproj_weights(w_in, a_log, dt_bias):
    offs = np.cumsum([0, QKV_W, GDN_W, GDN_HEADS, GDN_HEADS, ATT_W, KV_W, KV_W, IDX_W, IDX_DIM, IDX_HEADS])
    qkv, z, beta, a, aq, ak, av, iq, ik, iw = [w_in[:, offs[n]:offs[n + 1]] for n in range(10)]
    d = w_in.shape[0]
    pad = jnp.zeros((d, LANES - 3 * GDN_HEADS), w_in.dtype)
    w_nat = jnp.concatenate([qkv, z, av, beta, a, a, pad], axis=1).astype(BF)
    w_t = jnp.concatenate([aq, ak, iq, ik, iw, beta, a, av], axis=1).T.astype(BF)
    zrow = jnp.zeros((LANES - 3 * GDN_HEADS,), F32)
    prow = jnp.stack([jnp.concatenate([jnp.zeros((GDN_HEADS,), F32), a_log, a_log, zrow]),
                      jnp.concatenate([jnp.zeros((GDN_HEADS,), F32), dt_bias, dt_bias, zrow])])
    z12 = jnp.zeros((IDX_HEADS + GDN_HEADS,), F32)
    pcol = jnp.stack([jnp.concatenate([z12, a_log]), jnp.concatenate([z12, dt_bias])], axis=1)
    return w_nat, w_t, prow, pcol


def _rope_tables(pos, dim):
    half = dim // 8
    inv = ROPE_THETA ** (-jnp.arange(half, dtype=F32) / half)
    ang = pos.astype(F32)[:, None] * inv[None, :]
    return jnp.concatenate([jnp.cos(ang).T, jnp.sin(ang).T], axis=0)


def _inproj_prompt(x, mod, g1, w_nat, w_t, w_conv, prow, pcol, tm):
    bp, tp, d = x.shape
    nt = tp // tm
    rows = bp * tp
    pos = jnp.arange(tp, dtype=I32)
    cs128, cs64 = _rope_tables(pos, HEAD_DIM), _rope_tables(pos, IDX_DIM)
    ri = np.arange(tm)
    tril = jnp.asarray((ri[:, None] // GDN_CHUNK == ri[None, :] // GDN_CHUNK) & (ri[None, :] <= ri[:, None]), BF)
    const = lambda shape: pl.BlockSpec(shape, lambda b, i: (0,) * len(shape))
    rowblk = lambda w: pl.BlockSpec((tm, w), lambda b, i: (b * nt + i, 0))
    colblk = lambda h: pl.BlockSpec((h, tm), lambda b, i: (0, b * nt + i))
    in_specs = [
        pl.BlockSpec((None, tm, d), lambda b, i: (b, i, 0)),
        pl.BlockSpec((None, 1, d), lambda b, i: (b, 0, 1)),
        pl.BlockSpec((None, 1, d), lambda b, i: (b, 0, 0)),
        const((1, d)), const((d, NAT_W)), const((T_ROWS, d)), const((CONV_W, QKV_W)),
        pl.BlockSpec((HEAD_DIM // 4, tm), lambda b, i: (0, i)),
        pl.BlockSpec((IDX_DIM // 4, tm), lambda b, i: (0, i)),
        const((2, LANES)), const((16, 2)), const((tm, tm)),
    ]
    out_shape = [
        jax.ShapeDtypeStruct((rows, QKV_W), F32), jax.ShapeDtypeStruct((rows, GDN_W), F32),
        jax.ShapeDtypeStruct((rows * N_KV, HEAD_DIM), F32), jax.ShapeDtypeStruct((rows * N_KV, HEAD_DIM), F32),
        jax.ShapeDtypeStruct((bp, IDX_DIM, tp), F32), jax.ShapeDtypeStruct((bp, 8, QKV_W), F32),
        jax.ShapeDtypeStruct((rows, LANES), F32), jax.ShapeDtypeStruct((rows // GDN_CHUNK, 8, GDN_CHUNK), F32),
        jax.ShapeDtypeStruct((ATT_W, rows), BF), jax.ShapeDtypeStruct((IDX_W, rows), BF),
        jax.ShapeDtypeStruct((IDX_HEADS, rows), F32), jax.ShapeDtypeStruct((rows, KV_W), BF),
        jax.ShapeDtypeStruct((rows // ATT_TK, N_KV * VT_ROWS, ATT_TK), BF), jax.ShapeDtypeStruct((rows, IDX_DIM), BF),
    ]
    out_specs = [
        rowblk(QKV_W), rowblk(GDN_W),
        pl.BlockSpec((tm * N_KV, HEAD_DIM), lambda b, i: (b * nt + i, 0)),
        pl.BlockSpec((tm * N_KV, HEAD_DIM), lambda b, i: (b * nt + i, 0)),
        pl.BlockSpec((None, IDX_DIM, tm), lambda b, i: (b, 0, i)),
        pl.BlockSpec((None, 8, QKV_W), lambda b, i: (b, 0, 0)),
        rowblk(LANES),
        pl.BlockSpec((tm // GDN_CHUNK, 8, GDN_CHUNK), lambda b, i: (b * nt + i, 0, 0)),
        colblk(ATT_W), colblk(IDX_W), colblk(IDX_HEADS), rowblk(KV_W),
        pl.BlockSpec((tm // ATT_TK, N_KV * VT_ROWS, ATT_TK), lambda b, i: (b * nt + i, 0, 0)),
        rowblk(IDX_DIM),
    ]
    return pl.pallas_call(
        functools.partial(_inproj_prompt_kernel, tm, tm // _pick(tm, (ATT_TK,))),
        grid=(bp, nt), in_specs=in_specs, out_specs=out_specs, out_shape=out_shape,
        scratch_shapes=[pltpu.VMEM((tm + 8, QKV_W), F32), pltpu.VMEM((T_ROWS, tm), F32)],
        compiler_params=_params(("arbitrary", "arbitrary")),
        name="inproj_prompt",
    )(x, mod, mod, g1.reshape(1, d), w_nat, w_t, w_conv, cs128, cs64, prow, pcol, tril)


def _inproj_sample(x, mod, g1, w_nat, w_t, w_conv, prow, pcol, conv_state, past):
    db, d = x.shape
    pos = jnp.full((db,), past, I32)
    cs128, cs64 = _rope_tables(pos, HEAD_DIM), _rope_tables(pos, IDX_DIM)
    full = lambda shape: pl.BlockSpec(shape, lambda i: (0,) * len(shape))
    in_specs = [
        full((db, d)),
        pl.BlockSpec((db, d), lambda i: (0, 1)), pl.BlockSpec((db, d), lambda i: (0, 0)),
        full((1, d)), full((d, NAT_W)), full((T_ROWS, d)), full((CONV_W, QKV_W)),
        full((HEAD_DIM // 4, db)), full((IDX_DIM // 4, db)), full((2, LANES)), full((16, 2)),
        full((db, QKV_W)), full((db, QKV_W)), full((db, QKV_W)),
    ]
    widths = [QKV_W, GDN_W, KV_W, KV_W, LANES, QKV_W, LANES, ATT_W, IDX_W]
    return pl.pallas_call(
        _inproj_sample_kernel,
        grid=(1,), in_specs=in_specs,
        out_specs=[full((db, w)) for w in widths],
        out_shape=[jax.ShapeDtypeStruct((db, w), F32) for w in widths],
        scratch_shapes=[pltpu.VMEM((T_ROWS, db), F32)],
        compiler_params=_params(("arbitrary",)),
        name="inproj_sample",
    )(x, mod, mod, g1.reshape(1, d), w_nat, w_t, w_conv, cs128, cs64, prow, pcol,
      conv_state[:, 0], conv_state[:, 1], conv_state[:, 2])


def _unit_lower_inverses(lows):
    c = lows[0].shape[0]
    ii = lax.broadcasted_iota(I32, (c, c), 0)
    jj = lax.broadcasted_iota(I32, (c, c), 1)
    eye = jnp.where(ii == jj, 1.0, 0.0)
    levels = int(math.log2(c)) - 1
    ts = [eye - low for low in lows]
    ps = [_mm_hi(low, low) for low in lows]
    for lvl in range(levels):
        if lvl == levels - 1:
            ts = [t + _mm_hi(t, p) for t, p in zip(ts, ps)]
        else:
            both = [_mm_hi(jnp.concatenate([t, p], axis=0), p) for t, p in zip(ts, ps)]
            ts = [t + b[0:c] for t, b in zip(ts, both)]
            ps = [b[c:] for b in both]
    return ts


def _gdn_prompt_kernel(tg, qkvn_ref, z_ref, misc_ref, gct_ref, gn_ref, o_ref, ssm_ref, s_ref):
    i = pl.program_id(1)

    @pl.when(i == 0)
    def _():
        s_ref[...] = jnp.zeros(s_ref.shape, F32)

    c = GDN_CHUNK
    ii = lax.broadcasted_iota(I32, (c, c), 0)
    jj = lax.broadcasted_iota(I32, (c, c), 1)
    pairs = [(cc, hd) for cc in range(tg // c) for hd in range(GDN_HEADS)]

    qs, ks, gcs, rhs, lows, intras = {}, {}, {}, {}, [], {}
    for cc, hd in pairs:
        r0, lo = cc * c, hd * HEAD_DIM
        q = qkvn_ref[r0:r0 + c, lo:lo + HEAD_DIM]
        k = qkvn_ref[r0:r0 + c, GDN_W + lo:GDN_W + lo + HEAD_DIM]
        v = qkvn_ref[r0:r0 + c, 2 * GDN_W + lo:2 * GDN_W + lo + HEAD_DIM]
        beta_c = misc_ref[r0:r0 + c, hd:hd + 1]
        gc_c = misc_ref[r0:r0 + c, 2 * GDN_HEADS + hd:2 * GDN_HEADS + hd + 1]
        gc_r = gct_ref[cc, GDN_HEADS + hd:GDN_HEADS + hd + 1, :]
        decay = jnp.where(ii >= jj, jnp.exp(jnp.where(ii >= jj, gc_c - gc_r, 0.0)), 0.0)
        kb = k * beta_c
        both = _dot_nt(jnp.concatenate([kb, q], axis=0).astype(BF), k.astype(BF))
        lows.append(jnp.where(ii > jj, both[0:c] * decay, 0.0))
        intras[cc, hd] = (both[c:] * decay).astype(BF)
        rhs[cc, hd] = jnp.concatenate([v * beta_c, kb * jnp.exp(gc_c)], axis=1).astype(BF)
        qs[cc, hd], ks[cc, hd], gcs[cc, hd] = q, k, gc_c
    ts = _unit_lower_inverses(lows)
    uws = {p: _dot(t.astype(BF), rhs[p]) for p, t in zip(pairs, ts)}

    for cc, hd in pairs:
        r0, lo = cc * c, hd * HEAD_DIM
        q, k, gc_c, uw = qs[cc, hd], ks[cc, hd], gcs[cc, hd], uws[cc, hd]
        s = s_ref[hd]
        ws = _dot(jnp.concatenate([uw[:, HEAD_DIM:], q * jnp.exp(gc_c)], axis=0).astype(BF), s.astype(BF))
        v_new = (uw[:, 0:HEAD_DIM] - ws[0:c]).astype(BF)
        o = ws[c:] + _dot(intras[cc, hd], v_new)
        g_last = gc_c[c - 1:c, :]
        kd = k * jnp.exp(g_last - gc_c)
        s_ref[hd] = s * jnp.exp(g_last) + _dot(kd.T.astype(BF), v_new)
        o = _rmsnorm(o, gn_ref[...]) * _silu(z_ref[r0:r0 + c, lo:lo + HEAD_DIM])
        o_ref[r0:r0 + c, lo:lo + HEAD_DIM] = o.astype(BF)
    ssm_ref[...] = s_ref[...]


def _gdn_prompt(qkvn, z, misc, gct, g_norm, bp, tp, tg):
    nt = tp // tg
    rows = bp * tp
    rowblk = lambda w: pl.BlockSpec((tg, w), lambda b, i: (b * nt + i, 0))
    return pl.pallas_call(
        functools.partial(_gdn_prompt_kernel, tg),
        grid=(bp, nt),
        in_specs=[rowblk(QKV_W), rowblk(GDN_W), rowblk(LANES),
                  pl.BlockSpec((tg // GDN_CHUNK, 8, GDN_CHUNK), lambda b, i: (b * nt + i, 0, 0)),
                  pl.BlockSpec((1, HEAD_DIM), lambda b, i: (0, 0))],
        out_specs=[rowblk(GDN_W),
                   pl.BlockSpec((None, GDN_HEADS, HEAD_DIM, HEAD_DIM), lambda b, i: (b, 0, 0, 0))],
        out_shape=[jax.ShapeDtypeStruct((rows, GDN_W), BF),
                   jax.ShapeDtypeStruct((bp, GDN_HEADS, HEAD_DIM, HEAD_DIM), F32)],
        scratch_shapes=[pltpu.VMEM((GDN_HEADS, HEAD_DIM, HEAD_DIM), F32)],
        compiler_params=_params(("arbitrary", "arbitrary")),
        name="gdn_prompt",
    )(qkvn, z, misc, gct, g_norm.reshape(1, HEAD_DIM))


def _gdn_sample_kernel(nb, qkvn_ref, z_ref, misc_ref, gn_ref, s_ref, o_ref, ssm_ref):
    for bi in range(nb):
        for hd in range(GDN_HEADS):
            lo = hd * HEAD_DIM
            q = qkvn_ref[bi:bi + 1, lo:lo + HEAD_DIM]
            k = qkvn_ref[bi:bi + 1, GDN_W + lo:GDN_W + lo + HEAD_DIM]
            v = qkvn_ref[bi:bi + 1, 2 * GDN_W + lo:2 * GDN_W + lo + HEAD_DIM]
            beta = misc_ref[bi:bi + 1, hd:hd + 1]
            g = misc_ref[bi:bi + 1, GDN_HEADS + hd:GDN_HEADS + hd + 1]
            kcol = jnp.broadcast_to(k, (HEAD_DIM, HEAD_DIM)).T
            qcol = jnp.broadcast_to(q, (HEAD_DIM, HEAD_DIM)).T
            s = s_ref[bi, hd] * jnp.exp(g)
            ks = jnp.sum(kcol * s, axis=0, keepdims=True)
            delta = (v - ks) * beta
            s = s + kcol * delta
            ssm_ref[bi, hd] = s
            o = jnp.sum(qcol * s, axis=0, keepdims=True)
            o = _rmsnorm(o, gn_ref[...]) * _silu(z_ref[bi:bi + 1, lo:lo + HEAD_DIM])
            o_ref[bi:bi + 1, lo:lo + HEAD_DIM] = o.astype(BF)


def _gdn_sample(qkvn, z, misc, g_norm, state, nb):
    db = qkvn.shape[0]
    rowblk = lambda w: pl.BlockSpec((nb, w), lambda i: (i, 0))
    sblk = pl.BlockSpec((nb, GDN_HEADS, HEAD_DIM, HEAD_DIM), lambda i: (i, 0, 0, 0))
    return pl.pallas_call(
        functools.partial(_gdn_sample_kernel, nb),
        grid=(db // nb,),
        in_specs=[rowblk(QKV_W), rowblk(GDN_W), rowblk(LANES), pl.BlockSpec((1, HEAD_DIM), lambda i: (0, 0)), sblk],
        out_specs=[rowblk(GDN_W), sblk],
        out_shape=[jax.ShapeDtypeStruct((db, GDN_W), BF), jax.ShapeDtypeStruct(state.shape, F32)],
        compiler_params=_params(("arbitrary",)),
        name="gdn_sample",
    )(qkvn, z, misc, g_norm.reshape(1, HEAD_DIM), state)


def _ordered_word_to_float(u):
    s = u ^ INT_MIN
    return lax.bitcast_convert_type(s ^ ((s >> 31) & np.int32(0x7FFFFFFF)), F32)


def _select_topk_bias(i_ref, nch, tk, k_sel, taken=0.0, dropped=NEG_INF, group=1):
    nl = i_ref.shape[1]

    def chunk(c):
        return pl.ds(pl.multiple_of(c * tk, tk), tk)

    def count(ref, pred):
        def body(c, acc):
            m = pred(ref[chunk(c), :])
            ones = jnp.where(m, 1, 0).astype(I32).reshape(tk // 8, 8, nl)
            parts = [ones[n] for n in range(tk // 8)]
            while len(parts) > 1:
                parts = [a + b for a, b in zip(parts[0::2], parts[1::2])] + parts[len(parts) & ~1:]
            return acc + parts[0]
        acc = lax.fori_loop(0, nch, body, jnp.zeros((8, nl), I32))
        return jnp.sum(acc, axis=0, keepdims=True)

    def bit_body(i, carry):
        t_u, n_ge = carry
        cand_u = t_u | jnp.left_shift(jnp.int32(1), 31 - i)
        cand = _ordered_word_to_float(cand_u)
        cnt = count(i_ref, lambda sc: sc >= cand)
        take = cnt >= k_sel
        return jnp.where(take, cand_u, t_u), jnp.where(take, cnt, n_ge)

    t_u, n_ge = lax.fori_loop(0, 32, bit_body, (jnp.zeros((1, nl), I32), jnp.full((1, nl), -1, I32)))
    thr = jnp.where((t_u >> 23) == 0, NEG_INF, _ordered_word_to_float(t_u))

    excess = (n_ge - k_sel).astype(F32)
    ri = lax.broadcasted_iota(I32, (tk + 8, tk), 0)
    ci = lax.broadcasted_iota(I32, (tk + 8, tk), 1)
    above = jnp.where((ci > ri) | (ri >= tk), 1.0, 0.0).astype(BF)

    nsteps = (nch + group - 1) // group

    def write(n, later):
        scs = [i_ref[chunk((nsteps - 1 - n) * group + g), :] for g in reversed(range(group))]
        ties = [sc == thr for sc in scs]
        cnts = [_dot(above, jnp.where(t, 1.0, 0.0).astype(BF)) for t in ties]
        for g, sc, tie, cnt in zip(reversed(range(group)), scs, ties, cnts):
            sel = (sc > thr) | (tie & (cnt[0:tk] + later >= excess))
            i_ref[chunk((nsteps - 1 - n) * group + g), :] = jnp.where(sel & (sc > NEG_INF), taken, dropped)
            later = later + cnt[tk:tk + 1]
        return later

    lax.fori_loop(0, nsteps, write, jnp.zeros((1, nl), F32))


def _attn_prompt_kernel(tq, tk, blk, k_sel, qT_ref, iqT_ref, iwT_ref, kbf_ref, vT_ref, ikbf_ref, o_ref, i_ref):
    t0 = pl.program_id(1) * tq
    nch = (t0 + tq + tk - 1) // tk
    nblk = (t0 + tq + blk - 1) // blk
    qpos = t0 + lax.broadcasted_iota(I32, (1, tq), 1)
    row = lax.broadcasted_iota(I32, (tk, 1), 0)
    w = iwT_ref[...]
    pairs = IDX_HEADS // 2
    rhs = [jnp.concatenate([iqT_ref[(2 * p) * IDX_DIM:(2 * p + 1) * IDX_DIM, :],
                            iqT_ref[(2 * p + 1) * IDX_DIM:(2 * p + 2) * IDX_DIM, :]], axis=1) for p in range(pairs)]

    def chunk(c):
        return pl.ds(pl.multiple_of(c * tk, tk), tk)

    def idx_body(c, carry):
        ikc = ikbf_ref[chunk(c), :]
        acc = jnp.zeros((tk, tq), F32)
        for p in range(pairs):
            d = jnp.maximum(_dot(ikc, rhs[p]), 0.0)
            acc = acc + d[:, 0:tq] * w[2 * p:2 * p + 1, :] + d[:, tq:] * w[2 * p + 1:2 * p + 2, :]
        i_ref[chunk(c), :] = jnp.where(c * tk + row <= qpos, acc, NEG_INF)
        return carry

    lax.fori_loop(0, nch, idx_body, 0)

    def fill_body(c, carry):
        i_ref[chunk(c), :] = jnp.full((tk, tq), NEG_INF, F32)
        return carry

    lax.fori_loop(nch, nblk * (blk // tk), fill_body, 0)
    tks = max(tk * LANES // tq, 8)
    _select_topk_bias(i_ref, (t0 + tq + tks - 1) // tks, tks, k_sel, group=min(2, blk // tks))

    scale2 = HEAD_DIM ** -0.5 * math.log2(math.e)
    rep = ATT_HEADS // N_KV
    sub = ATT_TK
    nsub = blk // sub
    qg = [jnp.concatenate([qT_ref[(rep * g + r) * HEAD_DIM:(rep * g + r + 1) * HEAD_DIM, :] for r in range(rep)],
                          axis=1) for g in range(N_KV)]

    def body(c, carry):
        out = []
        for g in range(N_KV):
            m, acc = carry[g]
            ss = []
            for i in range(nsub):
                rows = pl.ds(pl.multiple_of(c * blk + i * sub, sub), sub)
                bias = i_ref[rows, :]
                ss.append(_dot(kbf_ref[rows, g * HEAD_DIM:(g + 1) * HEAD_DIM], qg[g]) * scale2
                          + jnp.concatenate([bias] * rep, axis=1))
            m_new = m
            for s in ss:
                m_new = jnp.maximum(m_new, jnp.max(s, axis=0, keepdims=True))
            m_safe = jnp.where(m_new == NEG_INF, 0.0, m_new)
            acc = acc * jnp.exp2(m - m_safe)
            for i, s in enumerate(ss):
                p = jnp.exp2(s - m_safe).astype(BF)
                acc = acc + _dot(vT_ref[c * nsub + i, g * VT_ROWS:(g + 1) * VT_ROWS, :], p)
            out.append((m_new, acc))
        return tuple(out)

    init = tuple((jnp.full((1, rep * tq), NEG_INF, F32), jnp.zeros((VT_ROWS, rep * tq), F32)) for _ in range(N_KV))
    res = lax.fori_loop(0, nblk, body, init)
    for g in range(N_KV):
        acc = res[g][1]
        o_t = acc[0:HEAD_DIM] / acc[HEAD_DIM:HEAD_DIM + 1]
        for r in range(rep):
            hd = rep * g + r
            o_ref[:, hd * HEAD_DIM:(hd + 1) * HEAD_DIM] = o_t[:, r * tq:(r + 1) * tq].T.astype(BF)


def _attn_prompt(qT, iqT, iwT, kbf, vT, ikbf, bp, tp):
    tq = _pick(tp, (ATT_TQ, QBLOCK))
    tk = _pick(tp, (IDX_TK, ATT_TK))
    blk = _pick(tp, (ATT_BLK, IDX_TK, ATT_TK))
    assert blk % tk == 0 and blk % ATT_TK == 0
    nq = tp // tq
    rows = bp * tp
    k_sel = min(TOPK_MAX, tp // 4)
    colblk = lambda h: pl.BlockSpec((h, tq), lambda b, j: (0, b * nq + j))
    return pl.pallas_call(
        functools.partial(_attn_prompt_kernel, tq, tk, blk, k_sel),
        grid=(bp, nq),
        in_specs=[colblk(ATT_W), colblk(IDX_W), colblk(IDX_HEADS),
                  pl.BlockSpec((tp, KV_W), lambda b, j: (b, 0)),
                  pl.BlockSpec((tp // ATT_TK, N_KV * VT_ROWS, ATT_TK), lambda b, j: (b, 0, 0)),
                  pl.BlockSpec((tp, IDX_DIM), lambda b, j: (b, 0))],
        out_specs=pl.BlockSpec((tq, ATT_W), lambda b, j: (b * nq + j, 0)),
        out_shape=jax.ShapeDtypeStruct((rows, ATT_W), BF),
        scratch_shapes=[pltpu.VMEM((tp, tq), F32)],
        compiler_params=_params(("arbitrary", "arbitrary")),
        name="attn_prompt",
    )(qT, iqT, iwT, kbf, vT, ikbf)


def _page_prefetch(npg, pt_ref, srcs, bufs, sem_ref):
    b = pl.program_id(0)
    steps = pl.num_programs(0)
    ring = bufs[0].shape[0]
    ahead = ring - 1
    slot = lax.rem(b, ring)

    def copy(n, step, sl, p):
        return pltpu.make_async_copy(srcs[n].at[pt_ref[step, p]], bufs[n].at[sl, p], sem_ref.at[n, sl, p])

    def start_all(step, sl):
        for n in range(len(srcs)):
            for p in range(npg):
                copy(n, step, sl, p).start()

    @pl.when(b == 0)
    def _():
        for step in range(ahead):
            start_all(step, step)

    @pl.when(b + ahead < steps)
    def _():
        start_all(b + ahead, lax.rem(b + ahead, ring))

    for n in range(len(srcs)):
        for p in range(npg):
            copy(n, b, slot, p).wait()
    return slot


def _sidx_kernel(npg, pt_ref, iq_ref, iw_ref, ikn_ref, cache_ref, out_ref, buf_ref, sem_ref):
    slot = _page_prefetch(npg, pt_ref, [cache_ref], [buf_ref], sem_ref)
    iq = iq_ref[...].astype(BF)
    w = iw_ref[...]
    for p in range(npg):
        d = jnp.maximum(_dot(iq, buf_ref[slot, p].astype(BF)), 0.0)
        out_ref[:, p * LANES:(p + 1) * LANES] = jnp.sum(d * w, axis=0, keepdims=True)
    dn = jnp.sum(iq.astype(F32) * ikn_ref[...].astype(BF).astype(F32), axis=1, keepdims=True)
    sn = jnp.sum(jnp.maximum(dn, 0.0) * w, axis=0, keepdims=True)
    lane = lax.broadcasted_iota(I32, (1, LANES), 1)
    out_ref[:, npg * LANES:(npg + 1) * LANES] = jnp.where(lane == 0, sn, NEG_INF)


def _sample_index_scores(iq, iw, ik_new, cache_ik_t, page_table):
    db, npg = page_table.shape
    page = cache_ik_t.shape[2]
    assert page == LANES and db > IDX_RING
    width = (npg + 1) * LANES
    grid_spec = pltpu.PrefetchScalarGridSpec(
        num_scalar_prefetch=1, grid=(db,),
        in_specs=[pl.BlockSpec((None, IDX_HEADS, IDX_DIM), lambda b, pt: (b, 0, 0)),
                  pl.BlockSpec((None, IDX_HEADS, 1), lambda b, pt: (b, 0, 0)),
                  pl.BlockSpec((None, 1, IDX_DIM), lambda b, pt: (b, 0, 0)),
                  pl.BlockSpec(memory_space=pl.ANY)],
        out_specs=pl.BlockSpec((None, 1, width), lambda b, pt: (b, 0, 0)),
        scratch_shapes=[pltpu.VMEM((IDX_RING, npg, IDX_DIM, page), F32),
                        pltpu.SemaphoreType.DMA((1, IDX_RING, npg))])
    out = pl.pallas_call(
        functools.partial(_sidx_kernel, npg), grid_spec=grid_spec,
        out_shape=jax.ShapeDtypeStruct((db, 1, width), F32),
        compiler_params=_params(("arbitrary",)),
        name="sample_index_scores",
    )(page_table, iq.reshape(db, IDX_HEADS, IDX_DIM), iw.reshape(db, IDX_HEADS, 1), ik_new.reshape(db, 1, IDX_DIM),
      cache_ik_t)
    return out.reshape(db, width)


def _ssel_kernel(nch, k_sel, s_ref, spread_ref, keep_ref, i_ref):
    for p in range(nch):
        i_ref[p * LANES:(p + 1) * LANES, :] = s_ref[:, p * LANES:(p + 1) * LANES].T
    _select_topk_bias(i_ref, nch, LANES, k_sel, taken=1.0, dropped=0.0)
    rows = N_KV * LANES
    for p in range(nch - 1):
        k2 = _dot(spread_ref[...], i_ref[p * LANES:(p + 1) * LANES, :].astype(BF))
        for h in range(N_KV):
            keep_ref[:, p * rows + h * LANES:p * rows + (h + 1) * LANES] = k2[h * LANES:(h + 1) * LANES, :].T
    keep_ref[:, (nch - 1) * rows:(nch - 1) * rows + LANES] = i_ref[(nch - 1) * LANES:nch * LANES, :].T


def _sample_select(scores, n_keys):
    db, width = scores.shape
    assert db == LANES
    nch = width // LANES
    k_sel = min(TOPK_MAX, n_keys // 4)
    rows = N_KV * LANES
    out_w = (nch - 1) * rows + LANES
    spread = jnp.asarray(np.arange(rows)[:, None] // N_KV == np.arange(LANES)[None, :], BF)
    return pl.pallas_call(
        functools.partial(_ssel_kernel, nch, k_sel),
        grid=(1,),
        in_specs=[pl.BlockSpec((db, width), lambda i: (0, 0)), pl.BlockSpec((rows, LANES), lambda i: (0, 0))],
        out_specs=pl.BlockSpec((db, out_w), lambda i: (0, 0)),
        out_shape=jax.ShapeDtypeStruct((db, out_w), F32),
        scratch_shapes=[pltpu.VMEM((width, db), F32)],
        compiler_params=_params(("arbitrary",)),
        name="sample_select",
    )(scores, spread)


def _sattn_kernel(npg, pt_ref, q_ref, keep_ref, kn_ref, vn_ref, ck_ref, cv_ref, o_ref, kbuf_ref, vbuf_ref, sem_ref):
    slot = _page_prefetch(npg, pt_ref, [ck_ref, cv_ref], [kbuf_ref, vbuf_ref], sem_ref)
    rep = ATT_HEADS // N_KV
    rows = N_KV * LANES
    scale = HEAD_DIM ** -0.5
    q = q_ref[...]
    q8 = jnp.concatenate([q, jnp.zeros((8 - ATT_HEADS, HEAD_DIM), F32)], axis=0).astype(BF)
    hrow = lax.broadcasted_iota(I32, (8, 1), 0)
    lane = lax.broadcasted_iota(I32, (1, LANES), 1)
    own_kv = lax.broadcasted_iota(I32, (1, rows), 1) % N_KV == hrow // rep
    parts = []
    for p in range(npg):
        s2 = _dot_nt(q8, kbuf_ref[slot, p].astype(BF))
        keep2 = keep_ref[:, p * rows:(p + 1) * rows]
        parts.append(jnp.where(own_kv & (keep2 > 0.5), s2 * scale, NEG_INF))
    kn = kn_ref[...].astype(BF).astype(F32)
    kn8 = jnp.where(hrow // rep == 0, kn[0:1, :], kn[1:2, :])
    s_new = jnp.sum(q8.astype(F32) * kn8, axis=1, keepdims=True)
    keep_new = keep_ref[:, npg * rows:npg * rows + 1]
    parts.append(jnp.where((lane == 0) & (keep_new > 0.5), s_new * scale, NEG_INF))
    s = jnp.concatenate(parts, axis=1)
    m = jnp.max(s, axis=1, keepdims=True)
    e = jnp.exp(s - m)
    pr = (e / jnp.sum(e, axis=1, keepdims=True)).astype(BF)
    o8 = jnp.zeros((8, HEAD_DIM), F32)
    for p in range(npg):
        o8 = o8 + _dot(pr[:, p * rows:(p + 1) * rows], vbuf_ref[slot, p].astype(BF))
    vn = vn_ref[...].astype(BF).astype(F32)
    vn8 = jnp.where(hrow // rep == 0, vn[0:1, :], vn[1:2, :])
    o8 = o8 + pr[:, npg * rows:npg * rows + 1].astype(F32) * vn8
    for hd in range(ATT_HEADS):
        o_ref[:, hd * HEAD_DIM:(hd + 1) * HEAD_DIM] = o8[hd:hd + 1, :].astype(BF)


def _sample_attention(q, keep, k_new, v_new, cache_k, cache_v, page_table):
    db, npg = page_table.shape
    n_phys, page = cache_k.shape[0], cache_k.shape[1]
    assert page == LANES and N_KV == 2 and db > KV_RING
    rows = page * N_KV
    width = npg * rows + LANES
    ck = cache_k.reshape(n_phys, rows, HEAD_DIM)
    cv = cache_v.reshape(n_phys, rows, HEAD_DIM)
    grid_spec = pltpu.PrefetchScalarGridSpec(
        num_scalar_prefetch=1, grid=(db,),
        in_specs=[pl.BlockSpec((None, ATT_HEADS, HEAD_DIM), lambda b, pt: (b, 0, 0)),
                  pl.BlockSpec((None, 1, width), lambda b, pt: (b, 0, 0)),
                  pl.BlockSpec((None, N_KV, HEAD_DIM), lambda b, pt: (b, 0, 0)),
                  pl.BlockSpec((None, N_KV, HEAD_DIM), lambda b, pt: (b, 0, 0)),
                  pl.BlockSpec(memory_space=pl.ANY), pl.BlockSpec(memory_space=pl.ANY)],
        out_specs=pl.BlockSpec((None, 1, ATT_W), lambda b, pt: (b, 0, 0)),
        scratch_shapes=[pltpu.VMEM((KV_RING, npg, rows, HEAD_DIM), F32), pltpu.VMEM((KV_RING, npg, rows, HEAD_DIM), F32),
                        pltpu.SemaphoreType.DMA((2, KV_RING, npg))])
    out = pl.pallas_call(
        functools.partial(_sattn_kernel, npg), grid_spec=grid_spec,
        out_shape=jax.ShapeDtypeStruct((db, 1, ATT_W), BF),
        compiler_params=_params(("arbitrary",)),
        name="sample_attention",
    )(page_table, q.reshape(db, ATT_HEADS, HEAD_DIM), keep.reshape(db, 1, width),
      k_new.reshape(db, N_KV, HEAD_DIM), v_new.reshape(db, N_KV, HEAD_DIM), ck, cv)
    return out.reshape(db, ATT_W)


def _post_kernel(final, nff, og_ref, oa_ref, x_ref, ga1_ref, sc2_ref, sh2_ref, ga2_ref, g2_ref, gf_ref,
                 wo_ref, wg_ref, wu_ref, wd_ref, y_ref, x1_ref, h2_ref, acc_ref):
    jf = pl.program_id(1)

    @pl.when(jf == 0)
    def _():
        mixed = _dot(og_ref[...], wo_ref[0:GDN_W, :]) + _dot(oa_ref[...], wo_ref[GDN_W:, :])
        x1 = x_ref[...] + ga1_ref[...] * mixed
        x1_ref[...] = x1
        h2_ref[...] = (_rmsnorm(x1, g2_ref[...]) * (1.0 + sc2_ref[...]) + sh2_ref[...]).astype(BF)
        acc_ref[...] = jnp.zeros(acc_ref.shape, F32)

    h2 = h2_ref[...]
    act = (_silu(_dot(h2, wg_ref[...])) * _dot(h2, wu_ref[...])).astype(BF)
    acc_ref[...] += _dot(act, wd_ref[...])

    @pl.when(jf == nff - 1)
    def _():
        x2 = x1_ref[...] + ga2_ref[...] * acc_ref[...]
        y_ref[...] = _rmsnorm(x2, gf_ref[...]) if final else x2


def _post(og, oa, x2d, mod, mod_spec, g2, gf, wo, wfi, wfo, tm, tf, final):
    rows, d = x2d.shape
    dff = wfo.shape[0]
    nff = dff // tf
    rowblk = lambda w: pl.BlockSpec((tm, w), lambda i, j: (i, 0))
    const = lambda shape: pl.BlockSpec(shape, lambda i, j: (0,) * len(shape))
    in_specs = [rowblk(GDN_W), rowblk(ATT_W), rowblk(d),
                mod_spec(2), mod_spec(4), mod_spec(3), mod_spec(5),
                const((1, d)), const((1, d)), const((d, d)),
                pl.BlockSpec((d, tf), lambda i, j: (0, j)),
                pl.BlockSpec((d, tf), lambda i, j: (0, nff + j)),
                pl.BlockSpec((tf, d), lambda i, j: (j, 0))]
    return pl.pallas_call(
        functools.partial(_post_kernel, final, nff),
        grid=(rows // tm, nff), in_specs=in_specs, out_specs=rowblk(d),
        out_shape=jax.ShapeDtypeStruct((rows, d), F32),
        scratch_shapes=[pltpu.VMEM((tm, d), F32), pltpu.VMEM((tm, d), BF), pltpu.VMEM((tm, d), F32)],
        compiler_params=_params(("arbitrary", "arbitrary")),
        name="post",
    )(og, oa, x2d, mod, mod, mod, mod, g2.reshape(1, d), gf.reshape(1, d), wo, wfi, wfi, wfo)


def _pick(n, prefs):
    for p in prefs:
        if n % p == 0:
            return p
    return n


def kernel(x_prompt, x_sample, c_prompt, c_sample, cache_k, cache_v, cache_idx_k, page_table, state_conv, state_ssm,
           w_ada, b_ada, g_norm1, w_in, w_conv, a_log, dt_bias, g_gdn_norm, w_out, g_norm2, w_ffn_in, w_ffn_out,
           g_final):
    bp, tp, d = x_prompt.shape
    db, ts, _ = x_sample.shape
    assert ts == 1 and d == GDN_W + ATT_W and tp % max(ATT_TK, GDN_CHUNK) == 0
    depth = w_in.shape[0]
    npg, page = page_table.shape[1], cache_k.shape[2]
    past = npg * page
    tm_in = _pick(tp, (512, 256))
    tg = _pick(tp, (512, 256))
    tm_post = _pick(tp, (1024, 512, 256))
    dff = w_ffn_out.shape[1]
    tf = _pick(dff, (256, 128))

    xp = x_prompt
    xs = x_sample.reshape(db, d)
    npad = (-(bp + db)) % 8
    c_all = jnp.concatenate([c_prompt, c_sample, jnp.zeros((npad, d), F32)], axis=0)
    new_p, new_s = [], []
    for l in range(depth):
        final = l == depth - 1
        mod = _modulation(c_all, w_ada[l], b_ada[l])
        mod_p = mod[:bp].reshape(bp, 1, 6 * d)
        mod_s = mod[bp:bp + db]
        w_nat, w_t, prow, pcol = _inproj_weights(w_in[l], a_log[l], dt_bias[l])
        wo = w_out[l].astype(BF)
        wfi = w_ffn_in[l].astype(BF)
        wfo = w_ffn_out[l].astype(BF)

        (qkvn, z, vnat, knat, ikT, tail, misc, gct, qT, iqT, iwT, kbf, vT, ikbf) = _inproj_prompt(
            xp, mod_p, g_norm1[l], w_nat, w_t, w_conv[l], prow, pcol, tm_in)
        og, ssm_p = _gdn_prompt(qkvn, z, misc, gct, g_gdn_norm[l], bp, tp, tg)
        oa = _attn_prompt(qT, iqT, iwT, kbf, vT, ikbf, bp, tp)
        tiles_b = tp // tm_post
        spec_p = lambda k: pl.BlockSpec((None, 1, d), lambda i, j: (i // tiles_b, 0, k))
        xp = _post(og, oa, xp.reshape(bp * tp, d), mod_p, spec_p, g_norm2[l], g_final, wo, wfi, wfo,
                   tm_post, tf, final).reshape(bp, tp, d)
        new_p.append((knat.reshape(bp, tp, N_KV, HEAD_DIM), vnat.reshape(bp, tp, N_KV, HEAD_DIM),
                      jnp.swapaxes(ikT, 1, 2), tail[:, 8 - (CONV_W - 1):, :], ssm_p))

        (qkvn_s, z_s, vnat_s, knat_s, slab_s, raw_s, misc_s, aq_s, iq_s) = _inproj_sample(
            xs, mod_s, g_norm1[l], w_nat, w_t, w_conv[l], prow, pcol, state_conv[l], past)
        og_s, ssm_s = _gdn_sample(qkvn_s, z_s, misc_s, g_gdn_norm[l], state_ssm[l], _pick(db, (8,)))
        ik_s = slab_s[:, 0:IDX_DIM]
        iw_s = slab_s[:, IDX_DIM:IDX_DIM + IDX_HEADS]
        scores = _sample_index_scores(iq_s, iw_s, ik_s, jnp.swapaxes(cache_idx_k[l], 1, 2), page_table)
        keep = _sample_select(scores, past + ts)
        oa_s = _sample_attention(aq_s, keep, knat_s, vnat_s, cache_k[l], cache_v[l], page_table)
        spec_s = lambda k: pl.BlockSpec((db, d), lambda i, j: (i, k))
        xs = _post(og_s, oa_s, xs, mod_s, spec_s, g_norm2[l], g_final, wo, wfi, wfo, db, tf, final)
        new_conv_s = jnp.concatenate([state_conv[l][:, 1:, :], raw_s[:, None, :]], axis=1)
        new_s.append((knat_s.reshape(db, ts, N_KV, HEAD_DIM), vnat_s.reshape(db, ts, N_KV, HEAD_DIM),
                      ik_s.reshape(db, ts, IDX_DIM), new_conv_s, ssm_s))

    stack = lambda states, n: jnp.stack([s[n] for s in states], axis=0)
    return (xp, xs.reshape(db, ts, d),
            stack(new_p, 0), stack(new_p, 1), stack(new_p, 2), stack(new_p, 3), stack(new_p, 4),
            stack(new_s, 0), stack(new_s, 1), stack(new_s, 2), stack(new_s, 3), stack(new_s, 4))
```

```python
import functools
import math

import numpy as np
import jax
import jax.numpy as jnp
from jax import lax
from jax.experimental import pallas as pl
from jax.experimental.pallas import tpu as pltpu

F32 = jnp.float32
BF = jnp.bfloat16
I32 = jnp.int32

HEAD_DIM = 128
GDN_HEADS = 4
ATT_HEADS = 4
N_KV = 2
IDX_HEADS = 8
IDX_DIM = 64
CONV_W = 4
TOPK_MAX = 256
QBLOCK = 128
ROPE_THETA = 500000.0
EPS = 1e-6
GDN_W = GDN_HEADS * HEAD_DIM
ATT_W = ATT_HEADS * HEAD_DIM
KV_W = N_KV * HEAD_DIM
IDX_W = IDX_HEADS * IDX_DIM
QKV_W = 3 * GDN_W

GDN_CHUNK = 128
ATT_TK = 256
VT_ROWS = HEAD_DIM + 16
IDX_TK = 512
ATT_BLK = 1024
ATT_TQ = 256
IDX_RING = 9
KV_RING = 4
LANES = 128
NAT_W = QKV_W + GDN_W + KV_W + LANES
T_AQ, T_AK, T_IQ, T_IK, T_MISC, T_AV = 0, 512, 768, 1280, 1344, 1360
T_ROWS = T_AV + KV_W
VMEM_LIMIT = 56 * 1024 * 1024
INT_MIN = np.int32(-2 ** 31)
NEG_INF = float("-inf")


def _params(sem):
    return pltpu.CompilerParams(dimension_semantics=sem, vmem_limit_bytes=VMEM_LIMIT)


def _dot(a, b):
    return jnp.dot(a, b, preferred_element_type=F32)


def _dot_nt(a, b):
    return lax.dot_general(a, b, (((1,), (1,)), ((), ())), preferred_element_type=F32)


def _split3(x):
    hi = x.astype(BF)
    r1 = x - hi.astype(F32)
    mid = r1.astype(BF)
    lo = (r1 - mid.astype(F32)).astype(BF)
    return hi, mid, lo


def _mm_hi(a, b):
    ah = a.astype(BF)
    al = (a - ah.astype(F32)).astype(BF)
    bh = b.astype(BF)
    bl = (b - bh.astype(F32)).astype(BF)
    m = a.shape[0]
    hi = _dot(jnp.concatenate([ah, al], axis=0), bh)
    return hi[0:m] + (hi[m:] + _dot(ah, bl))


def _sigmoid(x):
    return 1.0 / (1.0 + jnp.exp(-x))


def _silu(x):
    return x * _sigmoid(x)


def _softplus(x):
    return jnp.maximum(x, 0.0) + jnp.log(1.0 + jnp.exp(-jnp.abs(x)))


def _rmsnorm(x, g):
    return x * lax.rsqrt(jnp.mean(x * x, axis=-1, keepdims=True) + EPS) * g


def _mod_kernel(c_ref, w_ref, b_ref, o_ref):
    s = _silu(c_ref[...]).astype(BF)
    o_ref[...] = _dot(s, w_ref[...].astype(BF)) + b_ref[...]


def _modulation(c_all, w_ada, b_ada):
    n, d = c_all.shape
    cols = w_ada.shape[1]
    tn = d
    return pl.pallas_call(
        _mod_kernel,
        grid=(cols // tn,),
        in_specs=[pl.BlockSpec((n, d), lambda j: (0, 0)),
                  pl.BlockSpec((d, tn), lambda j: (0, j)),
                  pl.BlockSpec((1, tn), lambda j: (0, j))],
        out_specs=pl.BlockSpec((n, tn), lambda j: (0, j)),
        out_shape=jax.ShapeDtypeStruct((n, cols), F32),
        compiler_params=_params(("arbitrary",)),
        name="modulation",
    )(c_all, w_ada, b_ada.reshape(1, cols))


def _rope_rows(rt_ref, base, half, cos, sin):
    x1 = rt_ref[base:base + half, :]
    x2 = rt_ref[base + half:base + 2 * half, :]
    rt_ref[base:base + half, :] = x1 * cos - x2 * sin
    rt_ref[base + half:base + 2 * half, :] = x2 * cos + x1 * sin


def _project(x_ref, sc_ref, sh_ref, g1_ref, wn_ref, wt_ref, cs128_ref, cs64_ref, pcol_ref, rt_ref):
    h = _rmsnorm(x_ref[...], g1_ref[...]) * (1.0 + sc_ref[...]) + sh_ref[...]
    hb = h.astype(BF)
    nat = _dot(hb, wn_ref[...])
    rt_ref[...] = _dot_nt(wt_ref[...], hb)
    half = HEAD_DIM // 8
    cos, sin = cs128_ref[0:half, :], cs128_ref[half:2 * half, :]
    for hd in range(ATT_HEADS):
        _rope_rows(rt_ref, T_AQ + hd * HEAD_DIM, half, cos, sin)
    for hd in range(N_KV):
        _rope_rows(rt_ref, T_AK + hd * HEAD_DIM, half, cos, sin)
    half = IDX_DIM // 8
    cos, sin = cs64_ref[0:half, :], cs64_ref[half:2 * half, :]
    for hd in range(IDX_HEADS):
        _rope_rows(rt_ref, T_IQ + hd * IDX_DIM, half, cos, sin)
    _rope_rows(rt_ref, T_IK, half, cos, sin)
    mt = rt_ref[T_MISC:T_MISC + 16, :]
    r = lax.broadcasted_iota(I32, mt.shape, 0)
    a_log, dt_b = pcol_ref[:, 0:1], pcol_ref[:, 1:2]
    gate = -jnp.exp(a_log) * _softplus(mt + dt_b)
    mt = jnp.where(r < IDX_HEADS, mt * IDX_HEADS ** -0.5, jnp.where(r < IDX_HEADS + GDN_HEADS, _sigmoid(mt), gate))
    rt_ref[T_MISC:T_MISC + 16, :] = mt
    return nat, mt


def _misc_natural(m, prow_ref):
    lane = lax.broadcasted_iota(I32, m.shape, 1)
    gate = -jnp.exp(prow_ref[0:1, :]) * _softplus(m + prow_ref[1:2, :])
    return jnp.where(lane < GDN_HEADS, _sigmoid(m), jnp.where(lane < 3 * GDN_HEADS, gate, 0.0))


def _qkv_post(conv, qkvn_ref):
    c = _silu(conv)
    for hb in range(2 * GDN_HEADS):
        xh = c[:, hb * HEAD_DIM:(hb + 1) * HEAD_DIM]
        n = xh * lax.rsqrt(jnp.sum(xh * xh, axis=-1, keepdims=True) + EPS)
        if hb < GDN_HEADS:
            n = n * HEAD_DIM ** -0.5
        qkvn_ref[:, hb * HEAD_DIM:(hb + 1) * HEAD_DIM] = n
    qkvn_ref[:, 2 * GDN_W:] = c[:, 2 * GDN_W:]


def _inproj_prompt_kernel(tm, parts, x_ref, sc_ref, sh_ref, g1_ref, wn_ref, wt_ref, wconv_ref, cs128_ref, cs64_ref,
                          prow_ref, pcol_ref, tril_ref,
                          qkvn_ref, z_ref, vnat_ref, knat_ref, ikT_ref, tail_ref, misc_ref, gct_ref,
                          qT_ref, iqT_ref, iwT_ref, kbf_ref, vT_ref, ikbf_ref,
                          xs_ref, rt_ref):
    i = pl.program_id(1)

    @pl.when(i == 0)
    def _():
        xs_ref[0:8, :] = jnp.zeros((8, QKV_W), F32)

    @pl.when(i > 0)
    def _():
        xs_ref[0:8, :] = xs_ref[tm:tm + 8, :]

    th = tm // parts
    for n in range(parts):
        rows, cols = pl.ds(n * th, th), pl.ds(n * th, th)
        _inproj_prompt_rows(
            th, x_ref.at[rows, :], sc_ref, sh_ref, g1_ref, wn_ref, wt_ref, wconv_ref,
            cs128_ref.at[:, cols], cs64_ref.at[:, cols], prow_ref, pcol_ref, tril_ref.at[0:th, 0:th],
            qkvn_ref.at[rows, :], z_ref.at[rows, :], vnat_ref.at[pl.ds(n * th * N_KV, th * N_KV), :],
            knat_ref.at[pl.ds(n * th * N_KV, th * N_KV), :], ikT_ref.at[:, cols], tail_ref, misc_ref.at[rows, :],
            gct_ref.at[pl.ds(n * (th // GDN_CHUNK), th // GDN_CHUNK)],
            qT_ref.at[:, cols], iqT_ref.at[:, cols], iwT_ref.at[:, cols], kbf_ref.at[rows, :],
            vT_ref.at[pl.ds(n * (th // ATT_TK), th // ATT_TK)], ikbf_ref.at[rows, :],
            xs_ref.at[pl.ds(n * th, th + 8), :], rt_ref.at[:, cols])


def _inproj_prompt_rows(tm, x_ref, sc_ref, sh_ref, g1_ref, wn_ref, wt_ref, wconv_ref, cs128_ref, cs64_ref,
                        prow_ref, pcol_ref, tril_ref,
                        qkvn_ref, z_ref, vnat_ref, knat_ref, ikT_ref, tail_ref, misc_ref, gct_ref,
                        qT_ref, iqT_ref, iwT_ref, kbf_ref, vT_ref, ikbf_ref,
                        xs_ref, rt_ref):
    nat, mt = _project(x_ref, sc_ref, sh_ref, g1_ref, wn_ref, wt_ref, cs128_ref, cs64_ref, pcol_ref, rt_ref)
    xs_ref[8:tm + 8, :] = nat[:, 0:QKV_W]
    conv = wconv_ref[0:1, :] * xs_ref[5:tm + 5, :]
    for t in range(1, CONV_W):
        conv = conv + wconv_ref[t:t + 1, :] * xs_ref[5 + t:tm + 5 + t, :]
    tail_ref[...] = xs_ref[tm:tm + 8, :]
    _qkv_post(conv, qkvn_ref)
    z_ref[...] = nat[:, QKV_W:QKV_W + GDN_W]
    v0 = QKV_W + GDN_W
    for hd in range(N_KV):
        vnat_ref[pl.ds(hd, tm, stride=N_KV), :] = nat[:, v0 + hd * HEAD_DIM:v0 + (hd + 1) * HEAD_DIM]

    gm = _misc_natural(nat[:, NAT_W - LANES:], prow_ref)
    tril = tril_ref[...]
    hi, mid, lo = _split3(gm)
    gc = _dot(tril, hi) + (_dot(tril, mid) + _dot(tril, lo))
    lane = lax.broadcasted_iota(I32, gm.shape, 1)
    misc_ref[...] = jnp.where(lane < 2 * GDN_HEADS, gm, gc)
    hi, mid, lo = _split3(mt)
    gct = _dot_nt(hi, tril) + (_dot_nt(mid, tril) + _dot_nt(lo, tril))
    r = lax.broadcasted_iota(I32, mt.shape, 0)
    bg = jnp.where(r < IDX_HEADS + GDN_HEADS, mt, gct)[8:16, :]
    for cc in range(tm // GDN_CHUNK):
        gct_ref[cc] = bg[:, cc * GDN_CHUNK:(cc + 1) * GDN_CHUNK]

    iwT_ref[...] = mt[0:IDX_HEADS, :]
    qT_ref[...] = rt_ref[T_AQ:T_AQ + ATT_W, :].astype(BF)
    iqT_ref[...] = rt_ref[T_IQ:T_IQ + IDX_W, :].astype(BF)
    kn = rt_ref[T_AK:T_AK + KV_W, :].T
    for hd in range(N_KV):
        knat_ref[pl.ds(hd, tm, stride=N_KV), :] = kn[:, hd * HEAD_DIM:(hd + 1) * HEAD_DIM]
    kbf_ref[...] = kn.astype(BF)
    ikT_ref[...] = rt_ref[T_IK:T_IK + IDX_DIM, :]
    ikbf_ref[...] = rt_ref[T_IK:T_IK + LANES, :].T[:, 0:IDX_DIM].astype(BF)
    for cc in range(tm // ATT_TK):
        for hd in range(N_KV):
            lo = hd * VT_ROWS
            vT_ref[cc, lo:lo + HEAD_DIM, :] = rt_ref[T_AV + hd * HEAD_DIM:T_AV + (hd + 1) * HEAD_DIM,
                                                     cc * ATT_TK:(cc + 1) * ATT_TK].astype(BF)
            vT_ref[cc, lo + HEAD_DIM:lo + VT_ROWS, :] = jnp.ones((VT_ROWS - HEAD_DIM, ATT_TK), BF)


def _inproj_sample_kernel(x_ref, sc_ref, sh_ref, g1_ref, wn_ref, wt_ref, wconv_ref, cs128_ref, cs64_ref,
                          prow_ref, pcol_ref, s0_ref, s1_ref, s2_ref,
                          qkvn_ref, z_ref, vnat_ref, knat_ref, slab_ref, raw_ref, misc_ref, aq_ref, iq_ref,
                          rt_ref):
    nat, _ = _project(x_ref, sc_ref, sh_ref, g1_ref, wn_ref, wt_ref, cs128_ref, cs64_ref, pcol_ref, rt_ref)
    raw = nat[:, 0:QKV_W]
    raw_ref[...] = raw
    conv = wconv_ref[0:1, :] * s0_ref[...]
    conv = conv + wconv_ref[1:2, :] * s1_ref[...]
    conv = conv + wconv_ref[2:3, :] * s2_ref[...]
    conv = conv + wconv_ref[3:4, :] * raw
    _qkv_post(conv, qkvn_ref)
    z_ref[...] = nat[:, QKV_W:QKV_W + GDN_W]
    vnat_ref[...] = nat[:, QKV_W + GDN_W:QKV_W + GDN_W + KV_W]
    misc_ref[...] = _misc_natural(nat[:, NAT_W - LANES:], prow_ref)
    aq_ref[...] = rt_ref[T_AQ:T_AQ + ATT_W, :].T
    iq_ref[...] = rt_ref[T_IQ:T_IQ + IDX_W, :].T
    knat_ref[...] = rt_ref[T_AK:T_AK + KV_W, :].T
    slab_ref[...] = rt_ref[T_IK:T_IK + LANES, :].T


def _inproj_weights(w_in, a_log, dt_bias):
    offs = np.cumsum([0, QKV_W, GDN_W, GDN_HEADS, GDN_HEADS, ATT_W, KV_W, KV_W, IDX_W, IDX_DIM, IDX_HEADS])
    qkv, z, beta, a, aq, ak, av, iq, ik, iw = [w_in[:, offs[n]:offs[n + 1]] for n in range(10)]
    d = w_in.shape[0]
    pad = jnp.zeros((d, LANES - 3 * GDN_HEADS), w_in.dtype)
    w_nat = jnp.concatenate([qkv, z, av, beta, a, a, pad], axis=1).astype(BF)
    w_t = jnp.concatenate([aq, ak, iq, ik, iw, beta, a, av], axis=1).T.astype(BF)
    zrow = jnp.zeros((LANES - 3 * GDN_HEADS,), F32)
    prow = jnp.stack([jnp.concatenate([jnp.zeros((GDN_HEADS,), F32), a_log, a_log, zrow]),
                      jnp.concatenate([jnp.zeros((GDN_HEADS,), F32), dt_bias, dt_bias, zrow])])
    z12 = jnp.zeros((IDX_HEADS + GDN_HEADS,), F32)
    pcol = jnp.stack([jnp.concatenate([z12, a_log]), jnp.concatenate([z12, dt_bias])], axis=1)
    return w_nat, w_t, prow, pcol


def _rope_tables(pos, dim):
    half = dim // 8
    inv = ROPE_THETA ** (-jnp.arange(half, dtype=F32) / half)
    ang = pos.astype(F32)[:, None] * inv[None, :]
    return jnp.concatenate([jnp.cos(ang).T, jnp.sin(ang).T], axis=0)


def _inproj_prompt(x, mod, g1, w_nat, w_t, w_conv, prow, pcol, tm):
    bp, tp, d = x.shape
    nt = tp // tm
    rows = bp * tp
    pos = jnp.arange(tp, dtype=I32)
    cs128, cs64 = _rope_tables(pos, HEAD_DIM), _rope_tables(pos, IDX_DIM)
    ri = np.arange(tm)
    tril = jnp.asarray((ri[:, None] // GDN_CHUNK == ri[None, :] // GDN_CHUNK) & (ri[None, :] <= ri[:, None]), BF)
    const = lambda shape: pl.BlockSpec(shape, lambda b, i: (0,) * len(shape))
    rowblk = lambda w: pl.BlockSpec((tm, w), lambda b, i: (b * nt + i, 0))
    colblk = lambda h: pl.BlockSpec((h, tm), lambda b, i: (0, b * nt + i))
    in_specs = [
        pl.BlockSpec((None, tm, d), lambda b, i: (b, i, 0)),
        pl.BlockSpec((None, 1, d), lambda b, i: (b, 0, 1)),
        pl.BlockSpec((None, 1, d), lambda b, i: (b, 0, 0)),
        const((1, d)), const((d, NAT_W)), const((T_ROWS, d)), const((CONV_W, QKV_W)),
        pl.BlockSpec((HEAD_DIM // 4, tm), lambda b, i: (0, i)),
        pl.BlockSpec((IDX_DIM // 4, tm), lambda b, i: (0, i)),
        const((2, LANES)), const((16, 2)), const((tm, tm)),
    ]
    out_shape = [
        jax.ShapeDtypeStruct((rows, QKV_W), F32), jax.ShapeDtypeStruct((rows, GDN_W), F32),
        jax.ShapeDtypeStruct((rows * N_KV, HEAD_DIM), F32), jax.ShapeDtypeStruct((rows * N_KV, HEAD_DIM), F32),
        jax.ShapeDtypeStruct((bp, IDX_DIM, tp), F32), jax.ShapeDtypeStruct((bp, 8, QKV_W), F32),
        jax.ShapeDtypeStruct((rows, LANES), F32), jax.ShapeDtypeStruct((rows // GDN_CHUNK, 8, GDN_CHUNK), F32),
        jax.ShapeDtypeStruct((ATT_W, rows), BF), jax.ShapeDtypeStruct((IDX_W, rows), BF),
        jax.ShapeDtypeStruct((IDX_HEADS, rows), F32), jax.ShapeDtypeStruct((rows, KV_W), BF),
        jax.ShapeDtypeStruct((rows // ATT_TK, N_KV * VT_ROWS, ATT_TK), BF), jax.ShapeDtypeStruct((rows, IDX_DIM), BF),
    ]
    out_specs = [
        rowblk(QKV_W), rowblk(GDN_W),
        pl.BlockSpec((tm * N_KV, HEAD_DIM), lambda b, i: (b * nt + i, 0)),
        pl.BlockSpec((tm * N_KV, HEAD_DIM), lambda b, i: (b * nt + i, 0)),
        pl.BlockSpec((None, IDX_DIM, tm), lambda b, i: (b, 0, i)),
        pl.BlockSpec((None, 8, QKV_W), lambda b, i: (b, 0, 0)),
        rowblk(LANES),
        pl.BlockSpec((tm // GDN_CHUNK, 8, GDN_CHUNK), lambda b, i: (b * nt + i, 0, 0)),
        colblk(ATT_W), colblk(IDX_W), colblk(IDX_HEADS), rowblk(KV_W),
        pl.BlockSpec((tm // ATT_TK, N_KV * VT_ROWS, ATT_TK), lambda b, i: (b * nt + i, 0, 0)),
        rowblk(IDX_DIM),
    ]
    return pl.pallas_call(
        functools.partial(_inproj_prompt_kernel, tm, tm // _pick(tm, (ATT_TK,))),
        grid=(bp, nt), in_specs=in_specs, out_specs=out_specs, out_shape=out_shape,
        scratch_shapes=[pltpu.VMEM((tm + 8, QKV_W), F32), pltpu.VMEM((T_ROWS, tm), F32)],
        compiler_params=_params(("arbitrary", "arbitrary")),
        name="inproj_prompt",
    )(x, mod, mod, g1.reshape(1, d), w_nat, w_t, w_conv, cs128, cs64, prow, pcol, tril)


def _inproj_sample(x, mod, g1, w_nat, w_t, w_conv, prow, pcol, conv_state, past):
    db, d = x.shape
    pos = jnp.full((db,), past, I32)
    cs128, cs64 = _rope_tables(pos, HEAD_DIM), _rope_tables(pos, IDX_DIM)
    full = lambda shape: pl.BlockSpec(shape, lambda i: (0,) * len(shape))
    in_specs = [
        full((db, d)),
        pl.BlockSpec((db, d), lambda i: (0, 1)), pl.BlockSpec((db, d), lambda i: (0, 0)),
        full((1, d)), full((d, NAT_W)), full((T_ROWS, d)), full((CONV_W, QKV_W)),
        full((HEAD_DIM // 4, db)), full((IDX_DIM // 4, db)), full((2, LANES)), full((16, 2)),
        full((db, QKV_W)), full((db, QKV_W)), full((db, QKV_W)),
    ]
    widths = [QKV_W, GDN_W, KV_W, KV_W, LANES, QKV_W, LANES, ATT_W, IDX_W]
    return pl.pallas_call(
        _inproj_sample_kernel,
        grid=(1,), in_specs=in_specs,
        out_specs=[full((db, w)) for w in widths],
        out_shape=[jax.ShapeDtypeStruct((db, w), F32) for w in widths],
        scratch_shapes=[pltpu.VMEM((T_ROWS, db), F32)],
        compiler_params=_params(("arbitrary",)),
        name="inproj_sample",
    )(x, mod, mod, g1.reshape(1, d), w_nat, w_t, w_conv, cs128, cs64, prow, pcol,
      conv_state[:, 0], conv_state[:, 1], conv_state[:, 2])


def _unit_lower_inverses(lows):
    c = lows[0].shape[0]
    ii = lax.broadcasted_iota(I32, (c, c), 0)
    jj = lax.broadcasted_iota(I32, (c, c), 1)
    eye = jnp.where(ii == jj, 1.0, 0.0)
    levels = int(math.log2(c)) - 1
    ts = [eye - low for low in lows]
    ps = [_mm_hi(low, low) for low in lows]
    for lvl in range(levels):
        if lvl == levels - 1:
            ts = [t + _mm_hi(t, p) for t, p in zip(ts, ps)]
        else:
            both = [_mm_hi(jnp.concatenate([t, p], axis=0), p) for t, p in zip(ts, ps)]
            ts = [t + b[0:c] for t, b in zip(ts, both)]
            ps = [b[c:] for b in both]
    return ts


def _gdn_prompt_kernel(tg, qkvn_ref, z_ref, misc_ref, gct_ref, gn_ref, o_ref, ssm_ref, s_ref):
    i = pl.program_id(1)

    @pl.when(i == 0)
    def _():
        s_ref[...] = jnp.zeros(s_ref.shape, F32)

    c = GDN_CHUNK
    ii = lax.broadcasted_iota(I32, (c, c), 0)
    jj = lax.broadcasted_iota(I32, (c, c), 1)
    pairs = [(cc, hd) for cc in range(tg // c) for hd in range(GDN_HEADS)]

    qs, ks, gcs, rhs, lows, intras = {}, {}, {}, {}, [], {}
    for cc, hd in pairs:
        r0, lo = cc * c, hd * HEAD_DIM
        q = qkvn_ref[r0:r0 + c, lo:lo + HEAD_DIM]
        k = qkvn_ref[r0:r0 + c, GDN_W + lo:GDN_W + lo + HEAD_DIM]
        v = qkvn_ref[r0:r0 + c, 2 * GDN_W + lo:2 * GDN_W + lo + HEAD_DIM]
        beta_c = misc_ref[r0:r0 + c, hd:hd + 1]
        gc_c = misc_ref[r0:r0 + c, 2 * GDN_HEADS + hd:2 * GDN_HEADS + hd + 1]
        gc_r = gct_ref[cc, GDN_HEADS + hd:GDN_HEADS + hd + 1, :]
        decay = jnp.where(ii >= jj, jnp.exp(jnp.where(ii >= jj, gc_c - gc_r, 0.0)), 0.0)
        kb = k * beta_c
        both = _dot_nt(jnp.concatenate([kb, q], axis=0).astype(BF), k.astype(BF))
        lows.append(jnp.where(ii > jj, both[0:c] * decay, 0.0))
        intras[cc, hd] = (both[c:] * decay).astype(BF)
        rhs[cc, hd] = jnp.concatenate([v * beta_c, kb * jnp.exp(gc_c)], axis=1).astype(BF)
        qs[cc, hd], ks[cc, hd], gcs[cc, hd] = q, k, gc_c
    ts = _unit_lower_inverses(lows)
    uws = {p: _dot(t.astype(BF), rhs[p]) for p, t in zip(pairs, ts)}

    for cc, hd in pairs:
        r0, lo = cc * c, hd * HEAD_DIM
        q, k, gc_c, uw = qs[cc, hd], ks[cc, hd], gcs[cc, hd], uws[cc, hd]
        s = s_ref[hd]
        ws = _dot(jnp.concatenate([uw[:, HEAD_DIM:], q * jnp.exp(gc_c)], axis=0).astype(BF), s.astype(BF))
        v_new = (uw[:, 0:HEAD_DIM] - ws[0:c]).astype(BF)
        o = ws[c:] + _dot(intras[cc, hd], v_new)
        g_last = gc_c[c - 1:c, :]
        kd = k * jnp.exp(g_last - gc_c)
        s_ref[hd] = s * jnp.exp(g_last) + _dot(kd.T.astype(BF), v_new)
        o = _rmsnorm(o, gn_ref[...]) * _silu(z_ref[r0:r0 + c, lo:lo + HEAD_DIM])
        o_ref[r0:r0 + c, lo:lo + HEAD_DIM] = o.astype(BF)
    ssm_ref[...] = s_ref[...]


def _gdn_prompt(qkvn, z, misc, gct, g_norm, bp, tp, tg):
    nt = tp // tg
    rows = bp * tp
    rowblk = lambda w: pl.BlockSpec((tg, w), lambda b, i: (b * nt + i, 0))
    return pl.pallas_call(
        functools.partial(_gdn_prompt_kernel, tg),
        grid=(bp, nt),
        in_specs=[rowblk(QKV_W), rowblk(GDN_W), rowblk(LANES),
                  pl.BlockSpec((tg // GDN_CHUNK, 8, GDN_CHUNK), lambda b, i: (b * nt + i, 0, 0)),
                  pl.BlockSpec((1, HEAD_DIM), lambda b, i: (0, 0))],
        out_specs=[rowblk(GDN_W),
                   pl.BlockSpec((None, GDN_HEADS, HEAD_DIM, HEAD_DIM), lambda b, i: (b, 0, 0, 0))],
        out_shape=[jax.ShapeDtypeStruct((rows, GDN_W), BF),
                   jax.ShapeDtypeStruct((bp, GDN_HEADS, HEAD_DIM, HEAD_DIM), F32)],
        scratch_shapes=[pltpu.VMEM((GDN_HEADS, HEAD_DIM, HEAD_DIM), F32)],
        compiler_params=_params(("arbitrary", "arbitrary")),
        name="gdn_prompt",
    )(qkvn, z, misc, gct, g_norm.reshape(1, HEAD_DIM))


def _gdn_sample_kernel(nb, qkvn_ref, z_ref, misc_ref, gn_ref, s_ref, o_ref, ssm_ref):
    for bi in range(nb):
        for hd in range(GDN_HEADS):
            lo = hd * HEAD_DIM
            q = qkvn_ref[bi:bi + 1, lo:lo + HEAD_DIM]
            k = qkvn_ref[bi:bi + 1, GDN_W + lo:GDN_W + lo + HEAD_DIM]
            v = qkvn_ref[bi:bi + 1, 2 * GDN_W + lo:2 * GDN_W + lo + HEAD_DIM]
            beta = misc_ref[bi:bi + 1, hd:hd + 1]
            g = misc_ref[bi:bi + 1, GDN_HEADS + hd:GDN_HEADS + hd + 1]
            kcol = jnp.broadcast_to(k, (HEAD_DIM, HEAD_DIM)).T
            qcol = jnp.broadcast_to(q, (HEAD_DIM, HEAD_DIM)).T
            s = s_ref[bi, hd] * jnp.exp(g)
            ks = jnp.sum(kcol * s, axis=0, keepdims=True)
            delta = (v - ks) * beta
            s = s + kcol * delta
            ssm_ref[bi, hd] = s
            o = jnp.sum(qcol * s, axis=0, keepdims=True)
            o = _rmsnorm(o, gn_ref[...]) * _silu(z_ref[bi:bi + 1, lo:lo + HEAD_DIM])
            o_ref[bi:bi + 1, lo:lo + HEAD_DIM] = o.astype(BF)


def _gdn_sample(qkvn, z, misc, g_norm, state, nb):
    db = qkvn.shape[0]
    rowblk = lambda w: pl.BlockSpec((nb, w), lambda i: (i, 0))
    sblk = pl.BlockSpec((nb, GDN_HEADS, HEAD_DIM, HEAD_DIM), lambda i: (i, 0, 0, 0))
    return pl.pallas_call(
        functools.partial(_gdn_sample_kernel, nb),
        grid=(db // nb,),
        in_specs=[rowblk(QKV_W), rowblk(GDN_W), rowblk(LANES), pl.BlockSpec((1, HEAD_DIM), lambda i: (0, 0)), sblk],
        out_specs=[rowblk(GDN_W), sblk],
        out_shape=[jax.ShapeDtypeStruct((db, GDN_W), BF), jax.ShapeDtypeStruct(state.shape, F32)],
        compiler_params=_params(("arbitrary",)),
        name="gdn_sample",
    )(qkvn, z, misc, g_norm.reshape(1, HEAD_DIM), state)


def _ordered_word_to_float(u):
    s = u ^ INT_MIN
    return lax.bitcast_convert_type(s ^ ((s >> 31) & np.int32(0x7FFFFFFF)), F32)


def _select_topk_bias(i_ref, nch, tk, k_sel, taken=0.0, dropped=NEG_INF, group=1):
    nl = i_ref.shape[1]

    def chunk(c):
        return pl.ds(pl.multiple_of(c * tk, tk), tk)

    def count(ref, pred):
        def body(c, acc):
            m = pred(ref[chunk(c), :])
            ones = jnp.where(m, 1, 0).astype(I32).reshape(tk // 8, 8, nl)
            parts = [ones[n] for n in range(tk // 8)]
            while len(parts) > 1:
                parts = [a + b for a, b in zip(parts[0::2], parts[1::2])] + parts[len(parts) & ~1:]
            return acc + parts[0]
        acc = lax.fori_loop(0, nch, body, jnp.zeros((8, nl), I32))
        return jnp.sum(acc, axis=0, keepdims=True)

    def bit_body(i, carry):
        t_u, n_ge = carry
        cand_u = t_u | jnp.left_shift(jnp.int32(1), 31 - i)
        cand = _ordered_word_to_float(cand_u)
        cnt = count(i_ref, lambda sc: sc >= cand)
        take = cnt >= k_sel
        return jnp.where(take, cand_u, t_u), jnp.where(take, cnt, n_ge)

    t_u, n_ge = lax.fori_loop(0, 32, bit_body, (jnp.zeros((1, nl), I32), jnp.full((1, nl), -1, I32)))
    thr = jnp.where((t_u >> 23) == 0, NEG_INF, _ordered_word_to_float(t_u))

    excess = (n_ge - k_sel).astype(F32)
    ri = lax.broadcasted_iota(I32, (tk + 8, tk), 0)
    ci = lax.broadcasted_iota(I32, (tk + 8, tk), 1)
    above = jnp.where((ci > ri) | (ri >= tk), 1.0, 0.0).astype(BF)

    nsteps = (nch + group - 1) // group

    def write(n, later):
        scs = [i_ref[chunk((nsteps - 1 - n) * group + g), :] for g in reversed(range(group))]
        ties = [sc == thr for sc in scs]
        cnts = [_dot(above, jnp.where(t, 1.0, 0.0).astype(BF)) for t in ties]
        for g, sc, tie, cnt in zip(reversed(range(group)), scs, ties, cnts):
            sel = (sc > thr) | (tie & (cnt[0:tk] + later >= excess))
            i_ref[chunk((nsteps - 1 - n) * group + g), :] = jnp.where(sel & (sc > NEG_INF), taken, dropped)
            later = later + cnt[tk:tk + 1]
        return later

    lax.fori_loop(0, nsteps, write, jnp.zeros((1, nl), F32))


def _attn_prompt_kernel(tq, tk, blk, k_sel, qT_ref, iqT_ref, iwT_ref, kbf_ref, vT_ref, ikbf_ref, o_ref, i_ref):
    t0 = pl.program_id(1) * tq
    nch = (t0 + tq + tk - 1) // tk
    nblk = (t0 + tq + blk - 1) // blk
    qpos = t0 + lax.broadcasted_iota(I32, (1, tq), 1)
    row = lax.broadcasted_iota(I32, (tk, 1), 0)
    w = iwT_ref[...]
    pairs = IDX_HEADS // 2
    rhs = [jnp.concatenate([iqT_ref[(2 * p) * IDX_DIM:(2 * p + 1) * IDX_DIM, :],
                            iqT_ref[(2 * p + 1) * IDX_DIM:(2 * p + 2) * IDX_DIM, :]], axis=1) for p in range(pairs)]

    def chunk(c):
        return pl.ds(pl.multiple_of(c * tk, tk), tk)

    def idx_body(c, carry):
        ikc = ikbf_ref[chunk(c), :]
        acc = jnp.zeros((tk, tq), F32)
        for p in range(pairs):
            d = jnp.maximum(_dot(ikc, rhs[p]), 0.0)
            acc = acc + d[:, 0:tq] * w[2 * p:2 * p + 1, :] + d[:, tq:] * w[2 * p + 1:2 * p + 2, :]
        i_ref[chunk(c), :] = jnp.where(c * tk + row <= qpos, acc, NEG_INF)
        return carry

    lax.fori_loop(0, nch, idx_body, 0)

    def fill_body(c, carry):
        i_ref[chunk(c), :] = jnp.full((tk, tq), NEG_INF, F32)
        return carry

    lax.fori_loop(nch, nblk * (blk // tk), fill_body, 0)
    tks = max(tk * LANES // tq, 8)
    _select_topk_bias(i_ref, (t0 + tq + tks - 1) // tks, tks, k_sel, group=min(2, blk // tks))

    scale2 = HEAD_DIM ** -0.5 * math.log2(math.e)
    rep = ATT_HEADS // N_KV
    sub = ATT_TK
    nsub = blk // sub
    qg = [jnp.concatenate([qT_ref[(rep * g + r) * HEAD_DIM:(rep * g + r + 1) * HEAD_DIM, :] for r in range(rep)],
                          axis=1) for g in range(N_KV)]

    def body(c, carry):
        out = []
        for g in range(N_KV):
            m, acc = carry[g]
            ss = []
            for i in range(nsub):
                rows = pl.ds(pl.multiple_of(c * blk + i * sub, sub), sub)
                bias = i_ref[rows, :]
                ss.append(_dot(kbf_ref[rows, g * HEAD_DIM:(g + 1) * HEAD_DIM], qg[g]) * scale2
                          + jnp.concatenate([bias] * rep, axis=1))
            m_new = m
            for s in ss:
                m_new = jnp.maximum(m_new, jnp.max(s, axis=0, keepdims=True))
            m_safe = jnp.where(m_new == NEG_INF, 0.0, m_new)
            acc = acc * jnp.exp2(m - m_safe)
            for i, s in enumerate(ss):
                p = jnp.exp2(s - m_safe).astype(BF)
                acc = acc + _dot(vT_ref[c * nsub + i, g * VT_ROWS:(g + 1) * VT_ROWS, :], p)
            out.append((m_new, acc))
        return tuple(out)

    init = tuple((jnp.full((1, rep * tq), NEG_INF, F32), jnp.zeros((VT_ROWS, rep * tq), F32)) for _ in range(N_KV))
    res = lax.fori_loop(0, nblk, body, init)
    for g in range(N_KV):
        acc = res[g][1]
        o_t = acc[0:HEAD_DIM] / acc[HEAD_DIM:HEAD_DIM + 1]
        for r in range(rep):
            hd = rep * g + r
            o_ref[:, hd * HEAD_DIM:(hd + 1) * HEAD_DIM] = o_t[:, r * tq:(r + 1) * tq].T.astype(BF)


def _attn_prompt(qT, iqT, iwT, kbf, vT, ikbf, bp, tp):
    tq = _pick(tp, (ATT_TQ, QBLOCK))
    tk = _pick(tp, (IDX_TK, ATT_TK))
    blk = _pick(tp, (ATT_BLK, IDX_TK, ATT_TK))
    assert blk % tk == 0 and blk % ATT_TK == 0
    nq = tp // tq
    rows = bp * tp
    k_sel = min(TOPK_MAX, tp // 4)
    colblk = lambda h: pl.BlockSpec((h, tq), lambda b, j: (0, b * nq + j))
    return pl.pallas_call(
        functools.partial(_attn_prompt_kernel, tq, tk, blk, k_sel),
        grid=(bp, nq),
        in_specs=[colblk(ATT_W), colblk(IDX_W), colblk(IDX_HEADS),
                  pl.BlockSpec((tp, KV_W), lambda b, j: (b, 0)),
                  pl.BlockSpec((tp // ATT_TK, N_KV * VT_ROWS, ATT_TK), lambda b, j: (b, 0, 0)),
                  pl.BlockSpec((tp, IDX_DIM), lambda b, j: (b, 0))],
        out_specs=pl.BlockSpec((tq, ATT_W), lambda b, j: (b * nq + j, 0)),
        out_shape=jax.ShapeDtypeStruct((rows, ATT_W), BF),
        scratch_shapes=[pltpu.VMEM((tp, tq), F32)],
        compiler_params=_params(("arbitrary", "arbitrary")),
        name="attn_prompt",
    )(qT, iqT, iwT, kbf, vT, ikbf)


def _page_prefetch(npg, pt_ref, srcs, bufs, sem_ref):
    b = pl.program_id(0)
    steps = pl.num_programs(0)
    ring = bufs[0].shape[0]
    ahead = ring - 1
    slot = lax.rem(b, ring)

    def copy(n, step, sl, p):
        return pltpu.make_async_copy(srcs[n].at[pt_ref[step, p]], bufs[n].at[sl, p], sem_ref.at[n, sl, p])

    def start_all(step, sl):
        for n in range(len(srcs)):
            for p in range(npg):
                copy(n, step, sl, p).start()

    @pl.when(b == 0)
    def _():
        for step in range(ahead):
            start_all(step, step)

    @pl.when(b + ahead < steps)
    def _():
        start_all(b + ahead, lax.rem(b + ahead, ring))

    for n in range(len(srcs)):
        for p in range(npg):
            copy(n, b, slot, p).wait()
    return slot


def _sidx_kernel(npg, pt_ref, iq_ref, iw_ref, ikn_ref, cache_ref, out_ref, buf_ref, sem_ref):
    slot = _page_prefetch(npg, pt_ref, [cache_ref], [buf_ref], sem_ref)
    iq = iq_ref[...].astype(BF)
    w = iw_ref[...]
    for p in range(npg):
        d = jnp.maximum(_dot(iq, buf_ref[slot, p].astype(BF)), 0.0)
        out_ref[:, p * LANES:(p + 1) * LANES] = jnp.sum(d * w, axis=0, keepdims=True)
    dn = jnp.sum(iq.astype(F32) * ikn_ref[...].astype(BF).astype(F32), axis=1, keepdims=True)
    sn = jnp.sum(jnp.maximum(dn, 0.0) * w, axis=0, keepdims=True)
    lane = lax.broadcasted_iota(I32, (1, LANES), 1)
    out_ref[:, npg * LANES:(npg + 1) * LANES] = jnp.where(lane == 0, sn, NEG_INF)


def _sample_index_scores(iq, iw, ik_new, cache_ik_t, page_table):
    db, npg = page_table.shape
    page = cache_ik_t.shape[2]
    assert page == LANES and db > IDX_RING
    width = (npg + 1) * LANES
    grid_spec = pltpu.PrefetchScalarGridSpec(
        num_scalar_prefetch=1, grid=(db,),
        in_specs=[pl.BlockSpec((None, IDX_HEADS, IDX_DIM), lambda b, pt: (b, 0, 0)),
                  pl.BlockSpec((None, IDX_HEADS, 1), lambda b, pt: (b, 0, 0)),
                  pl.BlockSpec((None, 1, IDX_DIM), lambda b, pt: (b, 0, 0)),
                  pl.BlockSpec(memory_space=pl.ANY)],
        out_specs=pl.BlockSpec((None, 1, width), lambda b, pt: (b, 0, 0)),
        scratch_shapes=[pltpu.VMEM((IDX_RING, npg, IDX_DIM, page), F32),
                        pltpu.SemaphoreType.DMA((1, IDX_RING, npg))])
    out = pl.pallas_call(
        functools.partial(_sidx_kernel, npg), grid_spec=grid_spec,
        out_shape=jax.ShapeDtypeStruct((db, 1, width), F32),
        compiler_params=_params(("arbitrary",)),
        name="sample_index_scores",
    )(page_table, iq.reshape(db, IDX_HEADS, IDX_DIM), iw.reshape(db, IDX_HEADS, 1), ik_new.reshape(db, 1, IDX_DIM),
      cache_ik_t)
    return out.reshape(db, width)


def _ssel_kernel(nch, k_sel, s_ref, spread_ref, keep_ref, i_ref):
    for p in range(nch):
        i_ref[p * LANES:(p + 1) * LANES, :] = s_ref[:, p * LANES:(p + 1) * LANES].T
    _select_topk_bias(i_ref, nch, LANES, k_sel, taken=1.0, dropped=0.0)
    rows = N_KV * LANES
    for p in range(nch - 1):
        k2 = _dot(spread_ref[...], i_ref[p * LANES:(p + 1) * LANES, :].astype(BF))
        for h in range(N_KV):
            keep_ref[:, p * rows + h * LANES:p * rows + (h + 1) * LANES] = k2[h * LANES:(h + 1) * LANES, :].T
    keep_ref[:, (nch - 1) * rows:(nch - 1) * rows + LANES] = i_ref[(nch - 1) * LANES:nch * LANES, :].T


def _sample_select(scores, n_keys):
    db, width = scores.shape
    assert db == LANES
    nch = width // LANES
    k_sel = min(TOPK_MAX, n_keys // 4)
    rows = N_KV * LANES
    out_w = (nch - 1) * rows + LANES
    spread = jnp.asarray(np.arange(rows)[:, None] // N_KV == np.arange(LANES)[None, :], BF)
    return pl.pallas_call(
        functools.partial(_ssel_kernel, nch, k_sel),
        grid=(1,),
        in_specs=[pl.BlockSpec((db, width), lambda i: (0, 0)), pl.BlockSpec((rows, LANES), lambda i: (0, 0))],
        out_specs=pl.BlockSpec((db, out_w), lambda i: (0, 0)),
        out_shape=jax.ShapeDtypeStruct((db, out_w), F32),
        scratch_shapes=[pltpu.VMEM((width, db), F32)],
        compiler_params=_params(("arbitrary",)),
        name="sample_select",
    )(scores, spread)


def _sattn_kernel(npg, pt_ref, q_ref, keep_ref, kn_ref, vn_ref, ck_ref, cv_ref, o_ref, kbuf_ref, vbuf_ref, sem_ref):
    slot = _page_prefetch(npg, pt_ref, [ck_ref, cv_ref], [kbuf_ref, vbuf_ref], sem_ref)
    rep = ATT_HEADS // N_KV
    rows = N_KV * LANES
    scale = HEAD_DIM ** -0.5
    q = q_ref[...]
    q8 = jnp.concatenate([q, jnp.zeros((8 - ATT_HEADS, HEAD_DIM), F32)], axis=0).astype(BF)
    hrow = lax.broadcasted_iota(I32, (8, 1), 0)
    lane = lax.broadcasted_iota(I32, (1, LANES), 1)
    own_kv = lax.broadcasted_iota(I32, (1, rows), 1) % N_KV == hrow // rep
    parts = []
    for p in range(npg):
        s2 = _dot_nt(q8, kbuf_ref[slot, p].astype(BF))
        keep2 = keep_ref[:, p * rows:(p + 1) * rows]
        parts.append(jnp.where(own_kv & (keep2 > 0.5), s2 * scale, NEG_INF))
    kn = kn_ref[...].astype(BF).astype(F32)
    kn8 = jnp.where(hrow // rep == 0, kn[0:1, :], kn[1:2, :])
    s_new = jnp.sum(q8.astype(F32) * kn8, axis=1, keepdims=True)
    keep_new = keep_ref[:, npg * rows:npg * rows + 1]
    parts.append(jnp.where((lane == 0) & (keep_new > 0.5), s_new * scale, NEG_INF))
    s = jnp.concatenate(parts, axis=1)
    m = jnp.max(s, axis=1, keepdims=True)
    e = jnp.exp(s - m)
    pr = (e / jnp.sum(e, axis=1, keepdims=True)).astype(BF)
    o8 = jnp.zeros((8, HEAD_DIM), F32)
    for p in range(npg):
        o8 = o8 + _dot(pr[:, p * rows:(p + 1) * rows], vbuf_ref[slot, p].astype(BF))
    vn = vn_ref[...].astype(BF).astype(F32)
    vn8 = jnp.where(hrow // rep == 0, vn[0:1, :], vn[1:2, :])
    o8 = o8 + pr[:, npg * rows:npg * rows + 1].astype(F32) * vn8
    for hd in range(ATT_HEADS):
        o_ref[:, hd * HEAD_DIM:(hd + 1) * HEAD_DIM] = o8[hd:hd + 1, :].astype(BF)


def _sample_attention(q, keep, k_new, v_new, cache_k, cache_v, page_table):
    db, npg = page_table.shape
    n_phys, page = cache_k.shape[0], cache_k.shape[1]
    assert page == LANES and N_KV == 2 and db > KV_RING
    rows = page * N_KV
    width = npg * rows + LANES
    ck = cache_k.reshape(n_phys, rows, HEAD_DIM)
    cv = cache_v.reshape(n_phys, rows, HEAD_DIM)
    grid_spec = pltpu.PrefetchScalarGridSpec(
        num_scalar_prefetch=1, grid=(db,),
        in_specs=[pl.BlockSpec((None, ATT_HEADS, HEAD_DIM), lambda b, pt: (b, 0, 0)),
                  pl.BlockSpec((None, 1, width), lambda b, pt: (b, 0, 0)),
                  pl.BlockSpec((None, N_KV, HEAD_DIM), lambda b, pt: (b, 0, 0)),
                  pl.BlockSpec((None, N_KV, HEAD_DIM), lambda b, pt: (b, 0, 0)),
                  pl.BlockSpec(memory_space=pl.ANY), pl.BlockSpec(memory_space=pl.ANY)],
        out_specs=pl.BlockSpec((None, 1, ATT_W), lambda b, pt: (b, 0, 0)),
        scratch_shapes=[pltpu.VMEM((KV_RING, npg, rows, HEAD_DIM), F32), pltpu.VMEM((KV_RING, npg, rows, HEAD_DIM), F32),
                        pltpu.SemaphoreType.DMA((2, KV_RING, npg))])
    out = pl.pallas_call(
        functools.partial(_sattn_kernel, npg), grid_spec=grid_spec,
        out_shape=jax.ShapeDtypeStruct((db, 1, ATT_W), BF),
        compiler_params=_params(("arbitrary",)),
        name="sample_attention",
    )(page_table, q.reshape(db, ATT_HEADS, HEAD_DIM), keep.reshape(db, 1, width),
      k_new.reshape(db, N_KV, HEAD_DIM), v_new.reshape(db, N_KV, HEAD_DIM), ck, cv)
    return out.reshape(db, ATT_W)


def _post_kernel(final, nff, og_ref, oa_ref, x_ref, ga1_ref, sc2_ref, sh2_ref, ga2_ref, g2_ref, gf_ref,
                 wo_ref, wg_ref, wu_ref, wd_ref, y_ref, x1_ref, h2_ref, acc_ref):
    jf = pl.program_id(1)

    @pl.when(jf == 0)
    def _():
        mixed = _dot(og_ref[...], wo_ref[0:GDN_W, :]) + _dot(oa_ref[...], wo_ref[GDN_W:, :])
        x1 = x_ref[...] + ga1_ref[...] * mixed
        x1_ref[...] = x1
        h2_ref[...] = (_rmsnorm(x1, g2_ref[...]) * (1.0 + sc2_ref[...]) + sh2_ref[...]).astype(BF)
        acc_ref[...] = jnp.zeros(acc_ref.shape, F32)

    h2 = h2_ref[...]
    act = (_silu(_dot(h2, wg_ref[...])) * _dot(h2, wu_ref[...])).astype(BF)
    acc_ref[...] += _dot(act, wd_ref[...])

    @pl.when(jf == nff - 1)
    def _():
        x2 = x1_ref[...] + ga2_ref[...] * acc_ref[...]
        y_ref[...] = _rmsnorm(x2, gf_ref[...]) if final else x2


def _post(og, oa, x2d, mod, mod_spec, g2, gf, wo, wfi, wfo, tm, tf, final):
    rows, d = x2d.shape
    dff = wfo.shape[0]
    nff = dff // tf
    rowblk = lambda w: pl.BlockSpec((tm, w), lambda i, j: (i, 0))
    const = lambda shape: pl.BlockSpec(shape, lambda i, j: (0,) * len(shape))
    in_specs = [rowblk(GDN_W), rowblk(ATT_W), rowblk(d),
                mod_spec(2), mod_spec(4), mod_spec(3), mod_spec(5),
                const((1, d)), const((1, d)), const((d, d)),
                pl.BlockSpec((d, tf), lambda i, j: (0, j)),
                pl.BlockSpec((d, tf), lambda i, j: (0, nff + j)),
                pl.BlockSpec((tf, d), lambda i, j: (j, 0))]
    return pl.pallas_call(
        functools.partial(_post_kernel, final, nff),
        grid=(rows // tm, nff), in_specs=in_specs, out_specs=rowblk(d),
        out_shape=jax.ShapeDtypeStruct((rows, d), F32),
        scratch_shapes=[pltpu.VMEM((tm, d), F32), pltpu.VMEM((tm, d), BF), pltpu.VMEM((tm, d), F32)],
        compiler_params=_params(("arbitrary", "arbitrary")),
        name="post",
    )(og, oa, x2d, mod, mod, mod, mod, g2.reshape(1, d), gf.reshape(1, d), wo, wfi, wfi, wfo)


def _pick(n, prefs):
    for p in prefs:
        if n % p == 0:
            return p
    return n


def kernel(x_prompt, x_sample, c_prompt, c_sample, cache_k, cache_v, cache_idx_k, page_table, state_conv, state_ssm,
           w_ada, b_ada, g_norm1, w_in, w_conv, a_log, dt_bias, g_gdn_norm, w_out, g_norm2, w_ffn_in, w_ffn_out,
           g_final):
    bp, tp, d = x_prompt.shape
    db, ts, _ = x_sample.shape
    assert ts == 1 and d == GDN_W + ATT_W and tp % max(ATT_TK, GDN_CHUNK) == 0
    depth = w_in.shape[0]
    npg, page = page_table.shape[1], cache_k.shape[2]
    past = npg * page
    tm_in = _pick(tp, (512, 256))
    tg = _pick(tp, (512, 256))
    tm_post = _pick(tp, (1024, 512, 256))
    dff = w_ffn_out.shape[1]
    tf = _pick(dff, (256, 128))

    xp = x_prompt
    xs = x_sample.reshape(db, d)
    npad = (-(bp + db)) % 8
    c_all = jnp.concatenate([c_prompt, c_sample, jnp.zeros((npad, d), F32)], axis=0)
    new_p, new_s = [], []
    for l in range(depth):
        final = l == depth - 1
        mod = _modulation(c_all, w_ada[l], b_ada[l])
        mod_p = mod[:bp].reshape(bp, 1, 6 * d)
        mod_s = mod[bp:bp + db]
        w_nat, w_t, prow, pcol = _inproj_weights(w_in[l], a_log[l], dt_bias[l])
        wo = w_out[l].astype(BF)
        wfi = w_ffn_in[l].astype(BF)
        wfo = w_ffn_out[l].astype(BF)

        (qkvn, z, vnat, knat, ikT, tail, misc, gct, qT, iqT, iwT, kbf, vT, ikbf) = _inproj_prompt(
            xp, mod_p, g_norm1[l], w_nat, w_t, w_conv[l], prow, pcol, tm_in)
        og, ssm_p = _gdn_prompt(qkvn, z, misc, gct, g_gdn_norm[l], bp, tp, tg)
        oa = _attn_prompt(qT, iqT, iwT, kbf, vT, ikbf, bp, tp)
        tiles_b = tp // tm_post
        spec_p = lambda k: pl.BlockSpec((None, 1, d), lambda i, j: (i // tiles_b, 0, k))
        xp = _post(og, oa, xp.reshape(bp * tp, d), mod_p, spec_p, g_norm2[l], g_final, wo, wfi, wfo,
                   tm_post, tf, final).reshape(bp, tp, d)
        new_p.append((knat.reshape(bp, tp, N_KV, HEAD_DIM), vnat.reshape(bp, tp, N_KV, HEAD_DIM),
                      jnp.swapaxes(ikT, 1, 2), tail[:, 8 - (CONV_W - 1):, :], ssm_p))

        (qkvn_s, z_s, vnat_s, knat_s, slab_s, raw_s, misc_s, aq_s, iq_s) = _inproj_sample(
            xs, mod_s, g_norm1[l], w_nat, w_t, w_conv[l], prow, pcol, state_conv[l], past)
        og_s, ssm_s = _gdn_sample(qkvn_s, z_s, misc_s, g_gdn_norm[l], state_ssm[l], _pick(db, (8,)))
        ik_s = slab_s[:, 0:IDX_DIM]
        iw_s = slab_s[:, IDX_DIM:IDX_DIM + IDX_HEADS]
        scores = _sample_index_scores(iq_s, iw_s, ik_s, jnp.swapaxes(cache_idx_k[l], 1, 2), page_table)
        keep = _sample_select(scores, past + ts)
        oa_s = _sample_attention(aq_s, keep, knat_s, vnat_s, cache_k[l], cache_v[l], page_table)
        spec_s = lambda k: pl.BlockSpec((db, d), lambda i, j: (i, k))
        xs = _post(og_s, oa_s, xs, mod_s, spec_s, g_norm2[l], g_final, wo, wfi, wfo, db, tf, final)
        new_conv_s = jnp.concatenate([state_conv[l][:, 1:, :], raw_s[:, None, :]], axis=1)
        new_s.append((knat_s.reshape(db, ts, N_KV, HEAD_DIM), vnat_s.reshape(db, ts, N_KV, HEAD_DIM),
                      ik_s.reshape(db, ts, IDX_DIM), new_conv_s, ssm_s))

    stack = lambda states, n: jnp.stack([s[n] for s in states], axis=0)
    return (xp, xs.reshape(db, ts, d),
            stack(new_p, 0), stack(new_p, 1), stack(new_p, 2), stack(new_p, 3), stack(new_p, 4),
            stack(new_s, 0), stack(new_s, 1), stack(new_s, 2), stack(new_s, 3), stack(new_s, 4))
```

```python
import functools
import math

import numpy as np
import jax
import jax.numpy as jnp
from jax import lax
from jax.experimental import pallas as pl
from jax.experimental.pallas import tpu as pltpu

F32 = jnp.float32
BF = jnp.bfloat16
I32 = jnp.int32

HEAD_DIM = 128
GDN_HEADS = 4
ATT_HEADS = 4
N_KV = 2
IDX_HEADS = 8
IDX_DIM = 64
CONV_W = 4
TOPK_MAX = 256
QBLOCK = 128
ROPE_THETA = 500000.0
EPS = 1e-6
GDN_W = GDN_HEADS * HEAD_DIM
ATT_W = ATT_HEADS * HEAD_DIM
KV_W = N_KV * HEAD_DIM
IDX_W = IDX_HEADS * IDX_DIM
QKV_W = 3 * GDN_W

LANES = 128
BF16_ROWS = 16
GDN_CHUNK = 128
ATT_TK = 256
VT_ROWS = HEAD_DIM + BF16_ROWS
IDX_TK = 512
ATT_BLK = 1024
ATT_TQ = 256
IDX_RING = 5
KV_RING = 3
NAT_W = QKV_W + GDN_W + KV_W + LANES
MISC_ROWS = IDX_HEADS + 2 * GDN_HEADS
T_AQ = 0
T_AK = T_AQ + ATT_W
T_IQ = T_AK + KV_W
T_IK = T_IQ + IDX_W
T_MISC = T_IK + IDX_DIM
T_AV = T_MISC + MISC_ROWS
T_ROWS = T_AV + KV_W
VMEM_LIMIT = 56 * 1024 * 1024
INT_MIN = np.int32(-2 ** 31)
NEG_INF = float("-inf")


def _params(sem):
    return pltpu.CompilerParams(dimension_semantics=sem, vmem_limit_bytes=VMEM_LIMIT)


def _dot(a, b):
    return jnp.dot(a, b, preferred_element_type=F32)


def _dot_nt(a, b):
    return lax.dot_general(a, b, (((1,), (1,)), ((), ())), preferred_element_type=F32)


def _split3(x):
    hi = x.astype(BF)
    r1 = x - hi.astype(F32)
    mid = r1.astype(BF)
    lo = (r1 - mid.astype(F32)).astype(BF)
    return hi, mid, lo


def _mm_hi(a, b):
    ah = a.astype(BF)
    al = (a - ah.astype(F32)).astype(BF)
    bh = b.astype(BF)
    bl = (b - bh.astype(F32)).astype(BF)
    m = a.shape[0]
    hi = _dot(jnp.concatenate([ah, al], axis=0), bh)
    return hi[0:m] + (hi[m:] + _dot(ah, bl))


def _sigmoid(x):
    return 1.0 / (1.0 + jnp.exp(-x))


def _silu(x):
    return x * _sigmoid(x)


def _softplus(x):
    return jnp.maximum(x, 0.0) + jnp.log(1.0 + jnp.exp(-jnp.abs(x)))


def _rmsnorm(x, g):
    return x * lax.rsqrt(jnp.mean(x * x, axis=-1, keepdims=True) + EPS) * g


def _mod_kernel(c_ref, w_ref, b_ref, o_ref):
    s = _silu(c_ref[...]).astype(BF)
    o_ref[...] = _dot(s, w_ref[...].astype(BF)) + b_ref[...]


def _modulation(c_all, w_ada, b_ada):
    n, d = c_all.shape
    cols = w_ada.shape[1]
    tn = d
    return pl.pallas_call(
        _mod_kernel,
        grid=(cols // tn,),
        in_specs=[pl.BlockSpec((n, d), lambda j: (0, 0)),
                  pl.BlockSpec((d, tn), lambda j: (0, j)),
                  pl.BlockSpec((1, tn), lambda j: (0, j))],
        out_specs=pl.BlockSpec((n, tn), lambda j: (0, j)),
        out_shape=jax.ShapeDtypeStruct((n, cols), F32),
        compiler_params=_params(("arbitrary",)),
        name="modulation",
    )(c_all, w_ada, b_ada.reshape(1, cols))


def _rope_rows(rt_ref, base, half, cos, sin):
    x1 = rt_ref[base:base + half, :]
    x2 = rt_ref[base + half:base + 2 * half, :]
    rt_ref[base:base + half, :] = x1 * cos - x2 * sin
    rt_ref[base + half:base + 2 * half, :] = x2 * cos + x1 * sin


def _project(x_ref, sc_ref, sh_ref, g1_ref, wn_ref, wt_ref, cs128_ref, cs64_ref, pcol_ref, rt_ref):
    h = _rmsnorm(x_ref[...], g1_ref[...]) * (1.0 + sc_ref[...]) + sh_ref[...]
    hb = h.astype(BF)
    nat = _dot(hb, wn_ref[...])
    rt_ref[...] = _dot_nt(wt_ref[...], hb)
    half = HEAD_DIM // 8
    cos, sin = cs128_ref[0:half, :], cs128_ref[half:2 * half, :]
    for hd in range(ATT_HEADS):
        _rope_rows(rt_ref, T_AQ + hd * HEAD_DIM, half, cos, sin)
    for hd in range(N_KV):
        _rope_rows(rt_ref, T_AK + hd * HEAD_DIM, half, cos, sin)
    half = IDX_DIM // 8
    cos, sin = cs64_ref[0:half, :], cs64_ref[half:2 * half, :]
    for hd in range(IDX_HEADS):
        _rope_rows(rt_ref, T_IQ + hd * IDX_DIM, half, cos, sin)
    _rope_rows(rt_ref, T_IK, half, cos, sin)
    mt = rt_ref[T_MISC:T_MISC + MISC_ROWS, :]
    r = lax.broadcasted_iota(I32, mt.shape, 0)
    a_log, dt_b = pcol_ref[:, 0:1], pcol_ref[:, 1:2]
    gate = -jnp.exp(a_log) * _softplus(mt + dt_b)
    mt = jnp.where(r < IDX_HEADS, mt * IDX_HEADS ** -0.5, jnp.where(r < IDX_HEADS + GDN_HEADS, _sigmoid(mt), gate))
    rt_ref[T_MISC:T_MISC + MISC_ROWS, :] = mt
    return nat, mt


def _misc_natural(m, prow_ref):
    lane = lax.broadcasted_iota(I32, m.shape, 1)
    gate = -jnp.exp(prow_ref[0:1, :]) * _softplus(m + prow_ref[1:2, :])
    return jnp.where(lane < GDN_HEADS, _sigmoid(m), jnp.where(lane < 3 * GDN_HEADS, gate, 0.0))


def _qkv_post(conv, qkvn_ref):
    c = _silu(conv)
    for hb in range(2 * GDN_HEADS):
        xh = c[:, hb * HEAD_DIM:(hb + 1) * HEAD_DIM]
        n = xh * lax.rsqrt(jnp.sum(xh * xh, axis=-1, keepdims=True) + EPS)
        if hb < GDN_HEADS:
            n = n * HEAD_DIM ** -0.5
        qkvn_ref[:, hb * HEAD_DIM:(hb + 1) * HEAD_DIM] = n
    qkvn_ref[:, 2 * GDN_W:] = c[:, 2 * GDN_W:]


def _inproj_prompt_kernel(tm, parts, x_ref, sc_ref, sh_ref, g1_ref, wn_ref, wt_ref, wconv_ref, cs128_ref, cs64_ref,
                          prow_ref, pcol_ref, tril_ref,
                          qkvn_ref, z_ref, vnat_ref, knat_ref, ikT_ref, tail_ref, misc_ref, gct_ref,
                          qT_ref, iqT_ref, iwT_ref, kbf_ref, vT_ref, ikbf_ref,
                          xs_ref, rt_ref):
    i = pl.program_id(1)

    @pl.when(i == 0)
    def _():
        xs_ref[0:8, :] = jnp.zeros((8, QKV_W), F32)

    @pl.when(i > 0)
    def _():
        xs_ref[0:8, :] = xs_ref[tm:tm + 8, :]

    th = tm // parts
    for n in range(parts):
        rows, cols = pl.ds(n * th, th), pl.ds(n * th, th)
        _inproj_prompt_rows(
            th, x_ref.at[rows, :], sc_ref, sh_ref, g1_ref, wn_ref, wt_ref, wconv_ref,
            cs128_ref.at[:, cols], cs64_ref.at[:, cols], prow_ref, pcol_ref, tril_ref.at[0:th, 0:th],
            qkvn_ref.at[rows, :], z_ref.at[rows, :], vnat_ref.at[pl.ds(n * th * N_KV, th * N_KV), :],
            knat_ref.at[pl.ds(n * th * N_KV, th * N_KV), :], ikT_ref.at[:, cols], tail_ref, misc_ref.at[rows, :],
            gct_ref.at[pl.ds(n * (th // GDN_CHUNK), th // GDN_CHUNK)],
            qT_ref.at[:, cols], iqT_ref.at[:, cols], iwT_ref.at[:, cols], kbf_ref.at[rows, :],
            vT_ref.at[pl.ds(n * (th // ATT_TK), th // ATT_TK)], ikbf_ref.at[rows, :],
            xs_ref.at[pl.ds(n * th, th + 8), :], rt_ref.at[:, cols])


def _inproj_prompt_rows(tm, x_ref, sc_ref, sh_ref, g1_ref, wn_ref, wt_ref, wconv_ref, cs128_ref, cs64_ref,
                        prow_ref, pcol_ref, tril_ref,
                        qkvn_ref, z_ref, vnat_ref, knat_ref, ikT_ref, tail_ref, misc_ref, gct_ref,
                        qT_ref, iqT_ref, iwT_ref, kbf_ref, vT_ref, ikbf_ref,
                        xs_ref, rt_ref):
    nat, mt = _project(x_ref, sc_ref, sh_ref, g1_ref, wn_ref, wt_ref, cs128_ref, cs64_ref, pcol_ref, rt_ref)
    xs_ref[8:tm + 8, :] = nat[:, 0:QKV_W]
    conv = wconv_ref[0:1, :] * xs_ref[5:tm + 5, :]
    for t in range(1, CONV_W):
        conv = conv + wconv_ref[t:t + 1, :] * xs_ref[5 + t:tm + 5 + t, :]
    tail_ref[...] = xs_ref[tm:tm + 8, :]
    _qkv_post(conv, qkvn_ref)
    z_ref[...] = nat[:, QKV_W:QKV_W + GDN_W]
    v0 = QKV_W + GDN_W
    for hd in range(N_KV):
        vnat_ref[pl.ds(hd, tm, stride=N_KV), :] = nat[:, v0 + hd * HEAD_DIM:v0 + (hd + 1) * HEAD_DIM]

    gm = _misc_natural(nat[:, NAT_W - LANES:], prow_ref)
    tril = tril_ref[...]
    hi, mid, lo = _split3(gm)
    gc = _dot(tril, hi) + (_dot(tril, mid) + _dot(tril, lo))
    lane = lax.broadcasted_iota(I32, gm.shape, 1)
    misc_ref[...] = jnp.where(lane < 2 * GDN_HEADS, gm, gc)
    hi, mid, lo = _split3(mt)
    gct = _dot_nt(hi, tril) + (_dot_nt(mid, tril) + _dot_nt(lo, tril))
    r = lax.broadcasted_iota(I32, mt.shape, 0)
    bg = jnp.where(r < IDX_HEADS + GDN_HEADS, mt, gct)[IDX_HEADS:MISC_ROWS, :]
    for cc in range(tm // GDN_CHUNK):
        gct_ref[cc] = bg[:, cc * GDN_CHUNK:(cc + 1) * GDN_CHUNK]

    iwT_ref[...] = mt[0:IDX_HEADS, :]
    qT_ref[...] = rt_ref[T_AQ:T_AQ + ATT_W, :].astype(BF)
    iqT_ref[...] = rt_ref[T_IQ:T_IQ + IDX_W, :].astype(BF)
    kn = rt_ref[T_AK:T_AK + KV_W, :].T
    for hd in range(N_KV):
        knat_ref[pl.ds(hd, tm, stride=N_KV), :] = kn[:, hd * HEAD_DIM:(hd + 1) * HEAD_DIM]
    kbf_ref[...] = kn.astype(BF)
    ikT_ref[...] = rt_ref[T_IK:T_IK + IDX_DIM, :]
    ikbf_ref[...] = rt_ref[T_IK:T_IK + LANES, :].T[:, 0:IDX_DIM].astype(BF)
    for cc in range(tm // ATT_TK):
        for hd in range(N_KV):
            lo = hd * VT_ROWS
            vT_ref[cc, lo:lo + HEAD_DIM, :] = rt_ref[T_AV + hd * HEAD_DIM:T_AV + (hd + 1) * HEAD_DIM,
                                                     cc * ATT_TK:(cc + 1) * ATT_TK].astype(BF)
            vT_ref[cc, lo + HEAD_DIM:lo + VT_ROWS, :] = jnp.ones((VT_ROWS - HEAD_DIM, ATT_TK), BF)


def _inproj_sample_kernel(x_ref, sc_ref, sh_ref, g1_ref, wn_ref, wt_ref, wconv_ref, cs128_ref, cs64_ref,
                          prow_ref, pcol_ref, s0_ref, s1_ref, s2_ref,
                          qkvn_ref, z_ref, vnat_ref, knat_ref, slab_ref, raw_ref, misc_ref, aq_ref, iq_ref,
                          rt_ref):
    nat, _ = _project(x_ref, sc_ref, sh_ref, g1_ref, wn_ref, wt_ref, cs128_ref, cs64_ref, pcol_ref, rt_ref)
    raw = nat[:, 0:QKV_W]
    raw_ref[...] = raw
    conv = wconv_ref[0:1, :] * s0_ref[...]
    conv = conv + wconv_ref[1:2, :] * s1_ref[...]
    conv = conv + wconv_ref[2:3, :] * s2_ref[...]
    conv = conv + wconv_ref[3:4, :] * raw
    _qkv_post(conv, qkvn_ref)
    z_ref[...] = nat[:, QKV_W:QKV_W + GDN_W]
    vnat_ref[...] = nat[:, QKV_W + GDN_W:QKV_W + GDN_W + KV_W]
    misc_ref[...] = _misc_natural(nat[:, NAT_W - LANES:], prow_ref)
    aq_ref[...] = rt_ref[T_AQ:T_AQ + ATT_W, :].T
    iq_ref[...] = rt_ref[T_IQ:T_IQ + IDX_W, :].T
    knat_ref[...] = rt_ref[T_AK:T_AK + KV_W, :].T
    slab_ref[...] = rt_ref[T_IK:T_IK + LANES, :].T


def _inproj_weights(w_in, a_log, dt_bias):
    offs = np.cumsum([0, QKV_W, GDN_W, GDN_HEADS, GDN_HEADS, ATT_W, KV_W, KV_W, IDX_W, IDX_DIM, IDX_HEADS])
    qkv, z, beta, a, aq, ak, av, iq, ik, iw = [w_in[:, offs[n]:offs[n + 1]] for n in range(10)]
    d = w_in.shape[0]
    pad = jnp.zeros((d, LANES - 3 * GDN_HEADS), w_in.dtype)
    w_nat = jnp.concatenate([qkv, z, av, beta, a, a, pad], axis=1).astype(BF)
    w_t = jnp.concatenate([aq, ak, iq, ik, iw, beta, a, av], axis=1).T.astype(BF)
    zrow = jnp.zeros((LANES - 3 * GDN_HEADS,), F32)
    prow = jnp.stack([jnp.concatenate([jnp.zeros((GDN_HEADS,), F32), a_log, a_log, zrow]),
                      jnp.concatenate([jnp.zeros((GDN_HEADS,), F32), dt_bias, dt_bias, zrow])])
    z12 = jnp.zeros((IDX_HEADS + GDN_HEADS,), F32)
    pcol = jnp.stack([jnp.concatenate([z12, a_log]), jnp.concatenate([z12, dt_bias])], axis=1)
    return w_nat, w_t, prow, pcol


def _rope_tables(pos, dim):
    half = dim // 8
    inv = ROPE_THETA ** (-jnp.arange(half, dtype=F32) / half)
    ang = pos.astype(F32)[:, None] * inv[None, :]
    return jnp.concatenate([jnp.cos(ang).T, jnp.sin(ang).T], axis=0)


def _inproj_prompt(x, mod, g1, w_nat, w_t, w_conv, prow, pcol, tm):
    bp, tp, d = x.shape
    nt = tp // tm
    rows = bp * tp
    pos = jnp.arange(tp, dtype=I32)
    cs128, cs64 = _rope_tables(pos, HEAD_DIM), _rope_tables(pos, IDX_DIM)
    ri = np.arange(tm)
    tril = jnp.asarray((ri[:, None] // GDN_CHUNK == ri[None, :] // GDN_CHUNK) & (ri[None, :] <= ri[:, None]), BF)
    const = lambda shape: pl.BlockSpec(shape, lambda b, i: (0,) * len(shape))
    rowblk = lambda w: pl.BlockSpec((tm, w), lambda b, i: (b * nt + i, 0))
    colblk = lambda h: pl.BlockSpec((h, tm), lambda b, i: (0, b * nt + i))
    in_specs = [
        pl.BlockSpec((None, tm, d), lambda b, i: (b, i, 0)),
        pl.BlockSpec((None, 1, d), lambda b, i: (b, 0, 1)),
        pl.BlockSpec((None, 1, d), lambda b, i: (b, 0, 0)),
        const((1, d)), const((d, NAT_W)), const((T_ROWS, d)), const((CONV_W, QKV_W)),
        pl.BlockSpec((HEAD_DIM // 4, tm), lambda b, i: (0, i)),
        pl.BlockSpec((IDX_DIM // 4, tm), lambda b, i: (0, i)),
        const((2, LANES)), const((MISC_ROWS, 2)), const((tm, tm)),
    ]
    out_shape = [
        jax.ShapeDtypeStruct((rows, QKV_W), F32), jax.ShapeDtypeStruct((rows, GDN_W), F32),
        jax.ShapeDtypeStruct((rows * N_KV, HEAD_DIM), F32), jax.ShapeDtypeStruct((rows * N_KV, HEAD_DIM), F32),
        jax.ShapeDtypeStruct((bp, IDX_DIM, tp), F32), jax.ShapeDtypeStruct((bp, 8, QKV_W), F32),
        jax.ShapeDtypeStruct((rows, LANES), F32), jax.ShapeDtypeStruct((rows // GDN_CHUNK, 8, GDN_CHUNK), F32),
        jax.ShapeDtypeStruct((ATT_W, rows), BF), jax.ShapeDtypeStruct((IDX_W, rows), BF),
        jax.ShapeDtypeStruct((IDX_HEADS, rows), F32), jax.ShapeDtypeStruct((rows, KV_W), BF),
        jax.ShapeDtypeStruct((rows // ATT_TK, N_KV * VT_ROWS, ATT_TK), BF), jax.ShapeDtypeStruct((rows, IDX_DIM), BF),
    ]
    out_specs = [
        rowblk(QKV_W), rowblk(GDN_W),
        pl.BlockSpec((tm * N_KV, HEAD_DIM), lambda b, i: (b * nt + i, 0)),
        pl.BlockSpec((tm * N_KV, HEAD_DIM), lambda b, i: (b * nt + i, 0)),
        pl.BlockSpec((None, IDX_DIM, tm), lambda b, i: (b, 0, i)),
        pl.BlockSpec((None, 8, QKV_W), lambda b, i: (b, 0, 0)),
        rowblk(LANES),
        pl.BlockSpec((tm // GDN_CHUNK, 8, GDN_CHUNK), lambda b, i: (b * nt + i, 0, 0)),
        colblk(ATT_W), colblk(IDX_W), colblk(IDX_HEADS), rowblk(KV_W),
        pl.BlockSpec((tm // ATT_TK, N_KV * VT_ROWS, ATT_TK), lambda b, i: (b * nt + i, 0, 0)),
        rowblk(IDX_DIM),
    ]
    return pl.pallas_call(
        functools.partial(_inproj_prompt_kernel, tm, tm // _pick(tm, (ATT_TK,))),
        grid=(bp, nt), in_specs=in_specs, out_specs=out_specs, out_shape=out_shape,
        scratch_shapes=[pltpu.VMEM((tm + 8, QKV_W), F32), pltpu.VMEM((T_ROWS, tm), F32)],
        compiler_params=_params(("arbitrary", "arbitrary")),
        name="inproj_prompt",
    )(x, mod, mod, g1.reshape(1, d), w_nat, w_t, w_conv, cs128, cs64, prow, pcol, tril)


def _inproj_sample(x, mod, g1, w_nat, w_t, w_conv, prow, pcol, conv_state, past):
    db, d = x.shape
    pos = jnp.full((db,), past, I32)
    cs128, cs64 = _rope_tables(pos, HEAD_DIM), _rope_tables(pos, IDX_DIM)
    full = lambda shape: pl.BlockSpec(shape, lambda i: (0,) * len(shape))
    in_specs = [
        full((db, d)),
        pl.BlockSpec((db, d), lambda i: (0, 1)), pl.BlockSpec((db, d), lambda i: (0, 0)),
        full((1, d)), full((d, NAT_W)), full((T_ROWS, d)), full((CONV_W, QKV_W)),
        full((HEAD_DIM // 4, db)), full((IDX_DIM // 4, db)), full((2, LANES)), full((MISC_ROWS, 2)),
        full((db, QKV_W)), full((db, QKV_W)), full((db, QKV_W)),
    ]
    widths = [QKV_W, GDN_W, KV_W, KV_W, LANES, QKV_W, LANES, ATT_W, IDX_W]
    return pl.pallas_call(
        _inproj_sample_kernel,
        grid=(1,), in_specs=in_specs,
        out_specs=[full((db, w)) for w in widths],
        out_shape=[jax.ShapeDtypeStruct((db, w), F32) for w in widths],
        scratch_shapes=[pltpu.VMEM((T_ROWS, db), F32)],
        compiler_params=_params(("arbitrary",)),
        name="inproj_sample",
    )(x, mod, mod, g1.reshape(1, d), w_nat, w_t, w_conv, cs128, cs64, prow, pcol,
      conv_state[:, 0], conv_state[:, 1], conv_state[:, 2])


def _unit_lower_inverses(lows):
    c = lows[0].shape[0]
    ii = lax.broadcasted_iota(I32, (c, c), 0)
    jj = lax.broadcasted_iota(I32, (c, c), 1)
    eye = jnp.where(ii == jj, 1.0, 0.0)
    levels = int(math.log2(c)) - 1
    ts = [eye - low for low in lows]
    ps = [_mm_hi(low, low) for low in lows]
    for lvl in range(levels):
        if lvl == levels - 1:
            ts = [t + _mm_hi(t, p) for t, p in zip(ts, ps)]
        else:
            both = [_mm_hi(jnp.concatenate([t, p], axis=0), p) for t, p in zip(ts, ps)]
            ts = [t + b[0:c] for t, b in zip(ts, both)]
            ps = [b[c:] for b in both]
    return ts


def _gdn_prompt_kernel(tg, qkvn_ref, z_ref, misc_ref, gct_ref, gn_ref, o_ref, ssm_ref, s_ref):
    i = pl.program_id(1)

    @pl.when(i == 0)
    def _():
        s_ref[...] = jnp.zeros(s_ref.shape, F32)

    c = GDN_CHUNK
    ii = lax.broadcasted_iota(I32, (c, c), 0)
    jj = lax.broadcasted_iota(I32, (c, c), 1)
    pairs = [(cc, hd) for cc in range(tg // c) for hd in range(GDN_HEADS)]

    qs, ks, gcs, rhs, lows, intras = {}, {}, {}, {}, [], {}
    for cc, hd in pairs:
        r0, lo = cc * c, hd * HEAD_DIM
        q = qkvn_ref[r0:r0 + c, lo:lo + HEAD_DIM]
        k = qkvn_ref[r0:r0 + c, GDN_W + lo:GDN_W + lo + HEAD_DIM]
        v = qkvn_ref[r0:r0 + c, 2 * GDN_W + lo:2 * GDN_W + lo + HEAD_DIM]
        beta_c = misc_ref[r0:r0 + c, hd:hd + 1]
        gc_c = misc_ref[r0:r0 + c, 2 * GDN_HEADS + hd:2 * GDN_HEADS + hd + 1]
        gc_r = gct_ref[cc, GDN_HEADS + hd:GDN_HEADS + hd + 1, :]
        decay = jnp.where(ii >= jj, jnp.exp(jnp.where(ii >= jj, gc_c - gc_r, 0.0)), 0.0)
        kb = k * beta_c
        both = _dot_nt(jnp.concatenate([kb, q], axis=0).astype(BF), k.astype(BF))
        lows.append(jnp.where(ii > jj, both[0:c] * decay, 0.0))
        intras[cc, hd] = (both[c:] * decay).astype(BF)
        rhs[cc, hd] = jnp.concatenate([v * beta_c, kb * jnp.exp(gc_c)], axis=1).astype(BF)
        qs[cc, hd], ks[cc, hd], gcs[cc, hd] = q, k, gc_c
    ts = _unit_lower_inverses(lows)
    uws = {p: _dot(t.astype(BF), rhs[p]) for p, t in zip(pairs, ts)}

    for cc, hd in pairs:
        r0, lo = cc * c, hd * HEAD_DIM
        q, k, gc_c, uw = qs[cc, hd], ks[cc, hd], gcs[cc, hd], uws[cc, hd]
        s = s_ref[hd]
        ws = _dot(jnp.concatenate([uw[:, HEAD_DIM:], q * jnp.exp(gc_c)], axis=0).astype(BF), s.astype(BF))
        v_new = (uw[:, 0:HEAD_DIM] - ws[0:c]).astype(BF)
        o = ws[c:] + _dot(intras[cc, hd], v_new)
        g_last = gc_c[c - 1:c, :]
        kd = k * jnp.exp(g_last - gc_c)
        s_ref[hd] = s * jnp.exp(g_last) + _dot(kd.T.astype(BF), v_new)
        o = _rmsnorm(o, gn_ref[...]) * _silu(z_ref[r0:r0 + c, lo:lo + HEAD_DIM])
        o_ref[r0:r0 + c, lo:lo + HEAD_DIM] = o.astype(BF)
    ssm_ref[...] = s_ref[...]


def _gdn_prompt(qkvn, z, misc, gct, g_norm, bp, tp, tg):
    nt = tp // tg
    rows = bp * tp
    rowblk = lambda w: pl.BlockSpec((tg, w), lambda b, i: (b * nt + i, 0))
    return pl.pallas_call(
        functools.partial(_gdn_prompt_kernel, tg),
        grid=(bp, nt),
        in_specs=[rowblk(QKV_W), rowblk(GDN_W), rowblk(LANES),
                  pl.BlockSpec((tg // GDN_CHUNK, 8, GDN_CHUNK), lambda b, i: (b * nt + i, 0, 0)),
                  pl.BlockSpec((1, HEAD_DIM), lambda b, i: (0, 0))],
        out_specs=[rowblk(GDN_W),
                   pl.BlockSpec((None, GDN_HEADS, HEAD_DIM, HEAD_DIM), lambda b, i: (b, 0, 0, 0))],
        out_shape=[jax.ShapeDtypeStruct((rows, GDN_W), BF),
                   jax.ShapeDtypeStruct((bp, GDN_HEADS, HEAD_DIM, HEAD_DIM), F32)],
        scratch_shapes=[pltpu.VMEM((GDN_HEADS, HEAD_DIM, HEAD_DIM), F32)],
        compiler_params=_params(("arbitrary", "arbitrary")),
        name="gdn_prompt",
    )(qkvn, z, misc, gct, g_norm.reshape(1, HEAD_DIM))


def _gdn_sample_kernel(nb, qkvn_ref, z_ref, misc_ref, gn_ref, s_ref, o_ref, ssm_ref):
    for bi in range(nb):
        for hd in range(GDN_HEADS):
            lo = hd * HEAD_DIM
            q = qkvn_ref[bi:bi + 1, lo:lo + HEAD_DIM]
            k = qkvn_ref[bi:bi + 1, GDN_W + lo:GDN_W + lo + HEAD_DIM]
            v = qkvn_ref[bi:bi + 1, 2 * GDN_W + lo:2 * GDN_W + lo + HEAD_DIM]
            beta = misc_ref[bi:bi + 1, hd:hd + 1]
            g = misc_ref[bi:bi + 1, GDN_HEADS + hd:GDN_HEADS + hd + 1]
            kcol = jnp.broadcast_to(k, (HEAD_DIM, HEAD_DIM)).T
            qcol = jnp.broadcast_to(q, (HEAD_DIM, HEAD_DIM)).T
            s = s_ref[bi, hd] * jnp.exp(g)
            ks = jnp.sum(kcol * s, axis=0, keepdims=True)
            delta = (v - ks) * beta
            s = s + kcol * delta
            ssm_ref[bi, hd] = s
            o = jnp.sum(qcol * s, axis=0, keepdims=True)
            o = _rmsnorm(o, gn_ref[...]) * _silu(z_ref[bi:bi + 1, lo:lo + HEAD_DIM])
            o_ref[bi:bi + 1, lo:lo + HEAD_DIM] = o.astype(BF)


def _gdn_sample(qkvn, z, misc, g_norm, state, nb):
    db = qkvn.shape[0]
    rowblk = lambda w: pl.BlockSpec((nb, w), lambda i: (i, 0))
    sblk = pl.BlockSpec((nb, GDN_HEADS, HEAD_DIM, HEAD_DIM), lambda i: (i, 0, 0, 0))
    return pl.pallas_call(
        functools.partial(_gdn_sample_kernel, nb),
        grid=(db // nb,),
        in_specs=[rowblk(QKV_W), rowblk(GDN_W), rowblk(LANES), pl.BlockSpec((1, HEAD_DIM), lambda i: (0, 0)), sblk],
        out_specs=[rowblk(GDN_W), sblk],
        out_shape=[jax.ShapeDtypeStruct((db, GDN_W), BF), jax.ShapeDtypeStruct(state.shape, F32)],
        compiler_params=_params(("arbitrary",)),
        name="gdn_sample",
    )(qkvn, z, misc, g_norm.reshape(1, HEAD_DIM), state)


def _ordered_word_to_float(u):
    s = u ^ INT_MIN
    return lax.bitcast_convert_type(s ^ ((s >> 31) & np.int32(0x7FFFFFFF)), F32)


def _select_topk_bias(i_ref, nch, tk, k_sel, taken=0.0, dropped=NEG_INF, group=1):
    nl = i_ref.shape[1]

    def chunk(c):
        return pl.ds(pl.multiple_of(c * tk, tk), tk)

    def count(ref, pred):
        def body(c, acc):
            m = pred(ref[chunk(c), :])
            ones = jnp.where(m, 1, 0).astype(I32).reshape(tk // 8, 8, nl)
            parts = [ones[n] for n in range(tk // 8)]
            while len(parts) > 1:
                parts = [a + b for a, b in zip(parts[0::2], parts[1::2])] + parts[len(parts) & ~1:]
            return acc + parts[0]
        acc = lax.fori_loop(0, nch, body, jnp.zeros((8, nl), I32))
        return jnp.sum(acc, axis=0, keepdims=True)

    def bit_body(i, carry):
        t_u, n_ge = carry
        cand_u = t_u | jnp.left_shift(jnp.int32(1), 31 - i)
        cand = _ordered_word_to_float(cand_u)
        cnt = count(i_ref, lambda sc: sc >= cand)
        take = cnt >= k_sel
        return jnp.where(take, cand_u, t_u), jnp.where(take, cnt, n_ge)

    t_u, n_ge = lax.fori_loop(0, 32, bit_body, (jnp.zeros((1, nl), I32), jnp.full((1, nl), -1, I32)))
    thr = jnp.where((t_u >> 23) == 0, NEG_INF, _ordered_word_to_float(t_u))

    excess = (n_ge - k_sel).astype(F32)
    ri = lax.broadcasted_iota(I32, (tk + 8, tk), 0)
    ci = lax.broadcasted_iota(I32, (tk + 8, tk), 1)
    above = jnp.where((ci > ri) | (ri >= tk), 1.0, 0.0).astype(BF)

    nsteps = (nch + group - 1) // group

    def write(n, later):
        scs = [i_ref[chunk((nsteps - 1 - n) * group + g), :] for g in reversed(range(group))]
        ties = [sc == thr for sc in scs]
        cnts = [_dot(above, jnp.where(t, 1.0, 0.0).astype(BF)) for t in ties]
        for g, sc, tie, cnt in zip(reversed(range(group)), scs, ties, cnts):
            sel = (sc > thr) | (tie & (cnt[0:tk] + later >= excess))
            i_ref[chunk((nsteps - 1 - n) * group + g), :] = jnp.where(sel & (sc > NEG_INF), taken, dropped)
            later = later + cnt[tk:tk + 1]
        return later

    lax.fori_loop(0, nsteps, write, jnp.zeros((1, nl), F32))


def _attn_prompt_kernel(tq, tk, blk, k_sel, qT_ref, iqT_ref, iwT_ref, kbf_ref, vT_ref, ikbf_ref, o_ref, i_ref):
    t0 = pl.program_id(1) * tq
    nch = (t0 + tq + tk - 1) // tk
    nblk = (t0 + tq + blk - 1) // blk
    qpos = t0 + lax.broadcasted_iota(I32, (1, tq), 1)
    row = lax.broadcasted_iota(I32, (tk, 1), 0)
    w = iwT_ref[...]
    pairs = IDX_HEADS // 2
    rhs = [jnp.concatenate([iqT_ref[(2 * p) * IDX_DIM:(2 * p + 1) * IDX_DIM, :],
                            iqT_ref[(2 * p + 1) * IDX_DIM:(2 * p + 2) * IDX_DIM, :]], axis=1) for p in range(pairs)]

    def chunk(c):
        return pl.ds(pl.multiple_of(c * tk, tk), tk)

    def idx_body(c, carry):
        ikc = ikbf_ref[chunk(c), :]
        acc = jnp.zeros((tk, tq), F32)
        for p in range(pairs):
            d = jnp.maximum(_dot(ikc, rhs[p]), 0.0)
            acc = acc + d[:, 0:tq] * w[2 * p:2 * p + 1, :] + d[:, tq:] * w[2 * p + 1:2 * p + 2, :]
        i_ref[chunk(c), :] = jnp.where(c * tk + row <= qpos, acc, NEG_INF)
        return carry

    lax.fori_loop(0, nch, idx_body, 0)

    def fill_body(c, carry):
        i_ref[chunk(c), :] = jnp.full((tk, tq), NEG_INF, F32)
        return carry

    lax.fori_loop(nch, nblk * (blk // tk), fill_body, 0)
    tks = max(tk * LANES // tq, 8)
    _select_topk_bias(i_ref, (t0 + tq + tks - 1) // tks, tks, k_sel, group=min(2, blk // tks))

    scale2 = HEAD_DIM ** -0.5 * math.log2(math.e)
    rep = ATT_HEADS // N_KV
    sub = ATT_TK
    nsub = blk // sub
    qg = [jnp.concatenate([qT_ref[(rep * g + r) * HEAD_DIM:(rep * g + r + 1) * HEAD_DIM, :] for r in range(rep)],
                          axis=1) for g in range(N_KV)]

    def body(c, carry):
        out = []
        for g in range(N_KV):
            m, acc = carry[g]
            ss = []
            for i in range(nsub):
                rows = pl.ds(pl.multiple_of(c * blk + i * sub, sub), sub)
                bias = i_ref[rows, :]
                ss.append(_dot(kbf_ref[rows, g * HEAD_DIM:(g + 1) * HEAD_DIM], qg[g]) * scale2
                          + jnp.concatenate([bias] * rep, axis=1))
            m_new = m
            for s in ss:
                m_new = jnp.maximum(m_new, jnp.max(s, axis=0, keepdims=True))
            m_safe = jnp.where(m_new == NEG_INF, 0.0, m_new)
            acc = acc * jnp.exp2(m - m_safe)
            for i, s in enumerate(ss):
                p = jnp.exp2(s - m_safe).astype(BF)
                acc = acc + _dot(vT_ref[c * nsub + i, g * VT_ROWS:(g + 1) * VT_ROWS, :], p)
            out.append((m_new, acc))
        return tuple(out)

    init = tuple((jnp.full((1, rep * tq), NEG_INF, F32), jnp.zeros((VT_ROWS, rep * tq), F32)) for _ in range(N_KV))
    res = lax.fori_loop(0, nblk, body, init)
    for g in range(N_KV):
        acc = res[g][1]
        o_t = acc[0:HEAD_DIM] / acc[HEAD_DIM:HEAD_DIM + 1]
        for r in range(rep):
            hd = rep * g + r
            o_ref[:, hd * HEAD_DIM:(hd + 1) * HEAD_DIM] = o_t[:, r * tq:(r + 1) * tq].T.astype(BF)


def _attn_prompt(qT, iqT, iwT, kbf, vT, ikbf, bp, tp):
    tq = _pick(tp, (ATT_TQ, QBLOCK))
    tk = _pick(tp, (IDX_TK, ATT_TK))
    blk = _pick(tp, (ATT_BLK, IDX_TK, ATT_TK))
    assert blk % tk == 0 and blk % ATT_TK == 0
    nq = tp // tq
    rows = bp * tp
    k_sel = min(TOPK_MAX, tp // 4)
    colblk = lambda h: pl.BlockSpec((h, tq), lambda b, j: (0, b * nq + j))
    return pl.pallas_call(
        functools.partial(_attn_prompt_kernel, tq, tk, blk, k_sel),
        grid=(bp, nq),
        in_specs=[colblk(ATT_W), colblk(IDX_W), colblk(IDX_HEADS),
                  pl.BlockSpec((tp, KV_W), lambda b, j: (b, 0)),
                  pl.BlockSpec((tp // ATT_TK, N_KV * VT_ROWS, ATT_TK), lambda b, j: (b, 0, 0)),
                  pl.BlockSpec((tp, IDX_DIM), lambda b, j: (b, 0))],
        out_specs=pl.BlockSpec((tq, ATT_W), lambda b, j: (b * nq + j, 0)),
        out_shape=jax.ShapeDtypeStruct((rows, ATT_W), BF),
        scratch_shapes=[pltpu.VMEM((tp, tq), F32)],
        compiler_params=_params(("arbitrary", "arbitrary")),
        name="attn_prompt",
    )(qT, iqT, iwT, kbf, vT, ikbf)


def _page_prefetch(npg, pt_ref, srcs, bufs, sem_ref):
    b = pl.program_id(0)
    steps = pl.num_programs(0)
    ring = bufs[0].shape[0]
    ahead = ring - 1
    slot = lax.rem(b, ring)

    def copy(n, step, sl, p):
        return pltpu.make_async_copy(srcs[n].at[pt_ref[step, p]], bufs[n].at[sl, p], sem_ref.at[n, sl, p])

    def start_all(step, sl):
        for n in range(len(srcs)):
            for p in range(npg):
                copy(n, step, sl, p).start()

    @pl.when(b == 0)
    def _():
        for step in range(ahead):
            start_all(step, step)

    @pl.when(b + ahead < steps)
    def _():
        start_all(b + ahead, lax.rem(b + ahead, ring))

    for n in range(len(srcs)):
        for p in range(npg):
            copy(n, b, slot, p).wait()
    return slot


def _sidx_kernel(npg, pt_ref, iq_ref, iw_ref, ikn_ref, cache_ref, out_ref, buf_ref, sem_ref):
    slot = _page_prefetch(npg, pt_ref, [cache_ref], [buf_ref], sem_ref)
    iq = iq_ref[...].astype(BF)
    w = iw_ref[...]
    for p in range(npg):
        d = jnp.maximum(_dot(iq, buf_ref[slot, p].astype(BF)), 0.0)
        out_ref[:, p * LANES:(p + 1) * LANES] = jnp.sum(d * w, axis=0, keepdims=True)
    dn = jnp.sum(iq.astype(F32) * ikn_ref[...].astype(BF).astype(F32), axis=1, keepdims=True)
    sn = jnp.sum(jnp.maximum(dn, 0.0) * w, axis=0, keepdims=True)
    lane = lax.broadcasted_iota(I32, (1, LANES), 1)
    out_ref[:, npg * LANES:(npg + 1) * LANES] = jnp.where(lane == 0, sn, NEG_INF)


def _sample_index_scores(iq, iw, ik_new, cache_ik_t, page_table):
    db, npg = page_table.shape
    page = cache_ik_t.shape[2]
    assert page == LANES and db > IDX_RING
    width = (npg + 1) * LANES
    grid_spec = pltpu.PrefetchScalarGridSpec(
        num_scalar_prefetch=1, grid=(db,),
        in_specs=[pl.BlockSpec((None, IDX_HEADS, IDX_DIM), lambda b, pt: (b, 0, 0)),
                  pl.BlockSpec((None, IDX_HEADS, 1), lambda b, pt: (b, 0, 0)),
                  pl.BlockSpec((None, 1, IDX_DIM), lambda b, pt: (b, 0, 0)),
                  pl.BlockSpec(memory_space=pl.ANY)],
        out_specs=pl.BlockSpec((None, 1, width), lambda b, pt: (b, 0, 0)),
        scratch_shapes=[pltpu.VMEM((IDX_RING, npg, IDX_DIM, page), F32),
                        pltpu.SemaphoreType.DMA((1, IDX_RING, npg))])
    out = pl.pallas_call(
        functools.partial(_sidx_kernel, npg), grid_spec=grid_spec,
        out_shape=jax.ShapeDtypeStruct((db, 1, width), F32),
        compiler_params=_params(("arbitrary",)),
        name="sample_index_scores",
    )(page_table, iq.reshape(db, IDX_HEADS, IDX_DIM), iw.reshape(db, IDX_HEADS, 1), ik_new.reshape(db, 1, IDX_DIM),
      cache_ik_t)
    return out.reshape(db, width)


def _ssel_kernel(nch, k_sel, s_ref, spread_ref, keep_ref, i_ref):
    for p in range(nch):
        i_ref[p * LANES:(p + 1) * LANES, :] = s_ref[:, p * LANES:(p + 1) * LANES].T
    _select_topk_bias(i_ref, nch, LANES, k_sel, taken=1.0, dropped=0.0)
    rows = N_KV * LANES
    for p in range(nch - 1):
        k2 = _dot(spread_ref[...], i_ref[p * LANES:(p + 1) * LANES, :].astype(BF))
        for h in range(N_KV):
            keep_ref[:, p * rows + h * LANES:p * rows + (h + 1) * LANES] = k2[h * LANES:(h + 1) * LANES, :].T
    keep_ref[:, (nch - 1) * rows:(nch - 1) * rows + LANES] = i_ref[(nch - 1) * LANES:nch * LANES, :].T


def _sample_select(scores, n_keys):
    db, width = scores.shape
    assert db == LANES
    nch = width // LANES
    k_sel = min(TOPK_MAX, n_keys // 4)
    rows = N_KV * LANES
    out_w = (nch - 1) * rows + LANES
    spread = jnp.asarray(np.arange(rows)[:, None] // N_KV == np.arange(LANES)[None, :], BF)
    return pl.pallas_call(
        functools.partial(_ssel_kernel, nch, k_sel),
        grid=(1,),
        in_specs=[pl.BlockSpec((db, width), lambda i: (0, 0)), pl.BlockSpec((rows, LANES), lambda i: (0, 0))],
        out_specs=pl.BlockSpec((db, out_w), lambda i: (0, 0)),
        out_shape=jax.ShapeDtypeStruct((db, out_w), F32),
        scratch_shapes=[pltpu.VMEM((width, db), F32)],
        compiler_params=_params(("arbitrary",)),
        name="sample_select",
    )(scores, spread)


def _sattn_kernel(npg, pt_ref, q_ref, keep_ref, kn_ref, vn_ref, ck_ref, cv_ref, o_ref, kbuf_ref, vbuf_ref, sem_ref):
    slot = _page_prefetch(npg, pt_ref, [ck_ref, cv_ref], [kbuf_ref, vbuf_ref], sem_ref)
    rep = ATT_HEADS // N_KV
    rows = N_KV * LANES
    scale = HEAD_DIM ** -0.5
    q = q_ref[...]
    q8 = jnp.concatenate([q, jnp.zeros((8 - ATT_HEADS, HEAD_DIM), F32)], axis=0).astype(BF)
    hrow = lax.broadcasted_iota(I32, (8, 1), 0)
    lane = lax.broadcasted_iota(I32, (1, LANES), 1)
    own_kv = lax.broadcasted_iota(I32, (1, rows), 1) % N_KV == hrow // rep
    parts = []
    for p in range(npg):
        s2 = _dot_nt(q8, kbuf_ref[slot, p].astype(BF))
        keep2 = keep_ref[:, p * rows:(p + 1) * rows]
        parts.append(jnp.where(own_kv & (keep2 > 0.5), s2 * scale, NEG_INF))
    kn = kn_ref[...].astype(BF).astype(F32)
    kn8 = jnp.where(hrow // rep == 0, kn[0:1, :], kn[1:2, :])
    s_new = jnp.sum(q8.astype(F32) * kn8, axis=1, keepdims=True)
    keep_new = keep_ref[:, npg * rows:npg * rows + 1]
    parts.append(jnp.where((lane == 0) & (keep_new > 0.5), s_new * scale, NEG_INF))
    s = jnp.concatenate(parts, axis=1)
    m = jnp.max(s, axis=1, keepdims=True)
    e = jnp.exp(s - m)
    pr = (e / jnp.sum(e, axis=1, keepdims=True)).astype(BF)
    o8 = jnp.zeros((8, HEAD_DIM), F32)
    for p in range(npg):
        o8 = o8 + _dot(pr[:, p * rows:(p + 1) * rows], vbuf_ref[slot, p].astype(BF))
    vn = vn_ref[...].astype(BF).astype(F32)
    vn8 = jnp.where(hrow // rep == 0, vn[0:1, :], vn[1:2, :])
    o8 = o8 + pr[:, npg * rows:npg * rows + 1].astype(F32) * vn8
    for hd in range(ATT_HEADS):
        o_ref[:, hd * HEAD_DIM:(hd + 1) * HEAD_DIM] = o8[hd:hd + 1, :].astype(BF)


def _sample_attention(q, keep, k_new, v_new, cache_k, cache_v, page_table):
    db, npg = page_table.shape
    n_phys, page = cache_k.shape[0], cache_k.shape[1]
    assert page == LANES and N_KV == 2 and db > KV_RING
    rows = page * N_KV
    width = npg * rows + LANES
    ck = cache_k.reshape(n_phys, rows, HEAD_DIM)
    cv = cache_v.reshape(n_phys, rows, HEAD_DIM)
    grid_spec = pltpu.PrefetchScalarGridSpec(
        num_scalar_prefetch=1, grid=(db,),
        in_specs=[pl.BlockSpec((None, ATT_HEADS, HEAD_DIM), lambda b, pt: (b, 0, 0)),
                  pl.BlockSpec((None, 1, width), lambda b, pt: (b, 0, 0)),
                  pl.BlockSpec((None, N_KV, HEAD_DIM), lambda b, pt: (b, 0, 0)),
                  pl.BlockSpec((None, N_KV, HEAD_DIM), lambda b, pt: (b, 0, 0)),
                  pl.BlockSpec(memory_space=pl.ANY), pl.BlockSpec(memory_space=pl.ANY)],
        out_specs=pl.BlockSpec((None, 1, ATT_W), lambda b, pt: (b, 0, 0)),
        scratch_shapes=[pltpu.VMEM((KV_RING, npg, rows, HEAD_DIM), F32), pltpu.VMEM((KV_RING, npg, rows, HEAD_DIM), F32),
                        pltpu.SemaphoreType.DMA((2, KV_RING, npg))])
    out = pl.pallas_call(
        functools.partial(_sattn_kernel, npg), grid_spec=grid_spec,
        out_shape=jax.ShapeDtypeStruct((db, 1, ATT_W), BF),
        compiler_params=_params(("arbitrary",)),
        name="sample_attention",
    )(page_table, q.reshape(db, ATT_HEADS, HEAD_DIM), keep.reshape(db, 1, width),
      k_new.reshape(db, N_KV, HEAD_DIM), v_new.reshape(db, N_KV, HEAD_DIM), ck, cv)
    return out.reshape(db, ATT_W)


def _post_kernel(final, nff, og_ref, oa_ref, x_ref, ga1_ref, sc2_ref, sh2_ref, ga2_ref, g2_ref, gf_ref,
                 wo_ref, wg_ref, wu_ref, wd_ref, y_ref, x1_ref, h2_ref, acc_ref):
    jf = pl.program_id(1)

    @pl.when(jf == 0)
    def _():
        mixed = _dot(og_ref[...], wo_ref[0:GDN_W, :]) + _dot(oa_ref[...], wo_ref[GDN_W:, :])
        x1 = x_ref[...] + ga1_ref[...] * mixed
        x1_ref[...] = x1
        h2_ref[...] = (_rmsnorm(x1, g2_ref[...]) * (1.0 + sc2_ref[...]) + sh2_ref[...]).astype(BF)
        acc_ref[...] = jnp.zeros(acc_ref.shape, F32)

    h2 = h2_ref[...]
    act = (_silu(_dot(h2, wg_ref[...])) * _dot(h2, wu_ref[...])).astype(BF)
    acc_ref[...] += _dot(act, wd_ref[...])

    @pl.when(jf == nff - 1)
    def _():
        x2 = x1_ref[...] + ga2_ref[...] * acc_ref[...]
        y_ref[...] = _rmsnorm(x2, gf_ref[...]) if final else x2


def _post(og, oa, x2d, mod, mod_spec, g2, gf, wo, wfi, wfo, tm, tf, final):
    rows, d = x2d.shape
    dff = wfo.shape[0]
    nff = dff // tf
    rowblk = lambda w: pl.BlockSpec((tm, w), lambda i, j: (i, 0))
    const = lambda shape: pl.BlockSpec(shape, lambda i, j: (0,) * len(shape))
    in_specs = [rowblk(GDN_W), rowblk(ATT_W), rowblk(d),
                mod_spec(2), mod_spec(4), mod_spec(3), mod_spec(5),
                const((1, d)), const((1, d)), const((d, d)),
                pl.BlockSpec((d, tf), lambda i, j: (0, j)),
                pl.BlockSpec((d, tf), lambda i, j: (0, nff + j)),
                pl.BlockSpec((tf, d), lambda i, j: (j, 0))]
    return pl.pallas_call(
        functools.partial(_post_kernel, final, nff),
        grid=(rows // tm, nff), in_specs=in_specs, out_specs=rowblk(d),
        out_shape=jax.ShapeDtypeStruct((rows, d), F32),
        scratch_shapes=[pltpu.VMEM((tm, d), F32), pltpu.VMEM((tm, d), BF), pltpu.VMEM((tm, d), F32)],
        compiler_params=_params(("arbitrary", "arbitrary")),
        name="post",
    )(og, oa, x2d, mod, mod, mod, mod, g2.reshape(1, d), gf.reshape(1, d), wo, wfi, wfi, wfo)


def _pick(n, prefs):
    for p in prefs:
        if n % p == 0:
            return p
    return n


def kernel(x_prompt, x_sample, c_prompt, c_sample, cache_k, cache_v, cache_idx_k, page_table, state_conv, state_ssm,
           w_ada, b_ada, g_norm1, w_in, w_conv, a_log, dt_bias, g_gdn_norm, w_out, g_norm2, w_ffn_in, w_ffn_out,
           g_final):
    bp, tp, d = x_prompt.shape
    db, ts, _ = x_sample.shape
    assert ts == 1 and d == GDN_W + ATT_W and tp % max(ATT_TK, GDN_CHUNK) == 0
    depth = w_in.shape[0]
    npg, page = page_table.shape[1], cache_k.shape[2]
    past = npg * page
    tm_in = _pick(tp, (512, 256))
    tg = _pick(tp, (512, 256))
    tm_post = _pick(tp, (1024, 512, 256))
    dff = w_ffn_out.shape[1]
    tf = _pick(dff, (256, 128))

    xp = x_prompt
    xs = x_sample.reshape(db, d)
    npad = (-(bp + db)) % 8
    c_all = jnp.concatenate([c_prompt, c_sample, jnp.zeros((npad, d), F32)], axis=0)
    new_p, new_s = [], []
    for l in range(depth):
        final = l == depth - 1
        mod = _modulation(c_all, w_ada[l], b_ada[l])
        mod_p = mod[:bp].reshape(bp, 1, 6 * d)
        mod_s = mod[bp:bp + db]
        w_nat, w_t, prow, pcol = _inproj_weights(w_in[l], a_log[l], dt_bias[l])
        wo = w_out[l].astype(BF)
        wfi = w_ffn_in[l].astype(BF)
        wfo = w_ffn_out[l].astype(BF)

        (qkvn, z, vnat, knat, ikT, tail, misc, gct, qT, iqT, iwT, kbf, vT, ikbf) = _inproj_prompt(
            xp, mod_p, g_norm1[l], w_nat, w_t, w_conv[l], prow, pcol, tm_in)
        og, ssm_p = _gdn_prompt(qkvn, z, misc, gct, g_gdn_norm[l], bp, tp, tg)
        oa = _attn_prompt(qT, iqT, iwT, kbf, vT, ikbf, bp, tp)
        tiles_b = tp // tm_post
        spec_p = lambda k: pl.BlockSpec((None, 1, d), lambda i, j: (i // tiles_b, 0, k))
        xp = _post(og, oa, xp.reshape(bp * tp, d), mod_p, spec_p, g_norm2[l], g_final, wo, wfi, wfo,
                   tm_post, tf, final).reshape(bp, tp, d)
        new_p.append((knat.reshape(bp, tp, N_KV, HEAD_DIM), vnat.reshape(bp, tp, N_KV, HEAD_DIM),
                      jnp.swapaxes(ikT, 1, 2), tail[:, 8 - (CONV_W - 1):, :], ssm_p))

        (qkvn_s, z_s, vnat_s, knat_s, slab_s, raw_s, misc_s, aq_s, iq_s) = _inproj_sample(
            xs, mod_s, g_norm1[l], w_nat, w_t, w_conv[l], prow, pcol, state_conv[l], past)
        og_s, ssm_s = _gdn_sample(qkvn_s, z_s, misc_s, g_gdn_norm[l], state_ssm[l], _pick(db, (8,)))
        ik_s = slab_s[:, 0:IDX_DIM]
        iw_s = slab_s[:, IDX_DIM:IDX_DIM + IDX_HEADS]
        scores = _sample_index_scores(iq_s, iw_s, ik_s, jnp.swapaxes(cache_idx_k[l], 1, 2), page_table)
        keep = _sample_select(scores, past + ts)
        oa_s = _sample_attention(aq_s, keep, knat_s, vnat_s, cache_k[l], cache_v[l], page_table)
        spec_s = lambda k: pl.BlockSpec((db, d), lambda i, j: (i, k))
        xs = _post(og_s, oa_s, xs, mod_s, spec_s, g_norm2[l], g_final, wo, wfi, wfo, db, tf, final)
        new_conv_s = jnp.concatenate([state_conv[l][:, 1:, :], raw_s[:, None, :]], axis=1)
        new_s.append((knat_s.reshape(db, ts, N_KV, HEAD_DIM), vnat_s.reshape(db, ts, N_KV, HEAD_DIM),
                      ik_s.reshape(db, ts, IDX_DIM), new_conv_s, ssm_s))

    stack = lambda states, n: jnp.stack([s[n] for s in states], axis=0)
    return (xp, xs.reshape(db, ts, d),
            stack(new_p, 0), stack(new_p, 1), stack(new_p, 2), stack(new_p, 3), stack(new_p, 4),
            stack(new_s, 0), stack(new_s, 1), stack(new_s, 2), stack(new_s, 3), stack(new_s, 4))
```

```python
import functools
import math

import numpy as np
import jax
import jax.numpy as jnp
from jax import lax
from jax.experimental import pallas as pl
from jax.experimental.pallas import tpu as pltpu

F32 = jnp.float32
BF = jnp.bfloat16
I32 = jnp.int32

HEAD_DIM = 128
GDN_HEADS = 4
ATT_HEADS = 4
N_KV = 2
IDX_HEADS = 8
IDX_DIM = 64
CONV_W = 4
TOPK_MAX = 256
QBLOCK = 128
ROPE_THETA = 500000.0
EPS = 1e-6
GDN_W = GDN_HEADS * HEAD_DIM
ATT_W = ATT_HEADS * HEAD_DIM
KV_W = N_KV * HEAD_DIM
IDX_W = IDX_HEADS * IDX_DIM
QKV_W = 3 * GDN_W

LANES = 128
BF16_ROWS = 16
GDN_CHUNK = 128
ATT_TK = 256
VT_ROWS = HEAD_DIM + BF16_ROWS
IDX_TK = 512
ATT_BLK = 1024
ATT_TQ = 256
IDX_RING = 5
KV_RING = 3
NAT_W = QKV_W + GDN_W + KV_W + LANES
MISC_ROWS = IDX_HEADS + 2 * GDN_HEADS
T_AQ = 0
T_AK = T_AQ + ATT_W
T_IQ = T_AK + KV_W
T_IK = T_IQ + IDX_W
T_MISC = T_IK + IDX_DIM
T_AV = T_MISC + MISC_ROWS
T_ROWS = T_AV + KV_W
VMEM_LIMIT = 56 * 1024 * 1024
INT_MIN = np.int32(-2 ** 31)
NEG_INF = float("-inf")


def _params(sem):
    return pltpu.CompilerParams(dimension_semantics=sem, vmem_limit_bytes=VMEM_LIMIT)


def _dot(a, b):
    return jnp.dot(a, b, preferred_element_type=F32)


def _dot_nt(a, b):
    return lax.dot_general(a, b, (((1,), (1,)), ((), ())), preferred_element_type=F32)


def _split3(x):
    hi = x.astype(BF)
    r1 = x - hi.astype(F32)
    mid = r1.astype(BF)
    lo = (r1 - mid.astype(F32)).astype(BF)
    return hi, mid, lo


def _mm_hi(a, b):
    ah = a.astype(BF)
    al = (a - ah.astype(F32)).astype(BF)
    bh = b.astype(BF)
    bl = (b - bh.astype(F32)).astype(BF)
    m = a.shape[0]
    hi = _dot(jnp.concatenate([ah, al], axis=0), bh)
    return hi[0:m] + (hi[m:] + _dot(ah, bl))


def _sigmoid(x):
    return 1.0 / (1.0 + jnp.exp(-x))


def _silu(x):
    return x * _sigmoid(x)


def _softplus(x):
    return jnp.maximum(x, 0.0) + jnp.log(1.0 + jnp.exp(-jnp.abs(x)))


def _rmsnorm(x, g):
    return x * lax.rsqrt(jnp.mean(x * x, axis=-1, keepdims=True) + EPS) * g


def _mod_kernel(c_ref, w_ref, b_ref, o_ref):
    s = _silu(c_ref[...]).astype(BF)
    o_ref[...] = _dot(s, w_ref[...].astype(BF)) + b_ref[...]


def _modulation(c_all, w_ada, b_ada):
    n, d = c_all.shape
    cols = w_ada.shape[1]
    tn = d
    return pl.pallas_call(
        _mod_kernel,
        grid=(cols // tn,),
        in_specs=[pl.BlockSpec((n, d), lambda j: (0, 0)),
                  pl.BlockSpec((d, tn), lambda j: (0, j)),
                  pl.BlockSpec((1, tn), lambda j: (0, j))],
        out_specs=pl.BlockSpec((n, tn), lambda j: (0, j)),
        out_shape=jax.ShapeDtypeStruct((n, cols), F32),
        compiler_params=_params(("arbitrary",)),
        name="modulation",
    )(c_all, w_ada, b_ada.reshape(1, cols))


def _rope_rows(rt_ref, base, half, cos, sin):
    x1 = rt_ref[base:base + half, :]
    x2 = rt_ref[base + half:base + 2 * half, :]
    rt_ref[base:base + half, :] = x1 * cos - x2 * sin
    rt_ref[base + half:base + 2 * half, :] = x2 * cos + x1 * sin


def _project(x_ref, sc_ref, sh_ref, g1_ref, wn_ref, wt_ref, cs128_ref, cs64_ref, pcol_ref, rt_ref):
    h = _rmsnorm(x_ref[...], g1_ref[...]) * (1.0 + sc_ref[...]) + sh_ref[...]
    hb = h.astype(BF)
    nat = _dot(hb, wn_ref[...])
    rt_ref[...] = _dot_nt(wt_ref[...], hb)
    half = HEAD_DIM // 8
    cos, sin = cs128_ref[0:half, :], cs128_ref[half:2 * half, :]
    for hd in range(ATT_HEADS):
        _rope_rows(rt_ref, T_AQ + hd * HEAD_DIM, half, cos, sin)
    for hd in range(N_KV):
        _rope_rows(rt_ref, T_AK + hd * HEAD_DIM, half, cos, sin)
    half = IDX_DIM // 8
    cos, sin = cs64_ref[0:half, :], cs64_ref[half:2 * half, :]
    for hd in range(IDX_HEADS):
        _rope_rows(rt_ref, T_IQ + hd * IDX_DIM, half, cos, sin)
    _rope_rows(rt_ref, T_IK, half, cos, sin)
    mt = rt_ref[T_MISC:T_MISC + MISC_ROWS, :]
    r = lax.broadcasted_iota(I32, mt.shape, 0)
    a_log, dt_b = pcol_ref[:, 0:1], pcol_ref[:, 1:2]
    gate = -jnp.exp(a_log) * _softplus(mt + dt_b)
    mt = jnp.where(r < IDX_HEADS, mt * IDX_HEADS ** -0.5, jnp.where(r < IDX_HEADS + GDN_HEADS, _sigmoid(mt), gate))
    rt_ref[T_MISC:T_MISC + MISC_ROWS, :] = mt
    return nat, mt


def _misc_natural(m, prow_ref):
    lane = lax.broadcasted_iota(I32, m.shape, 1)
    gate = -jnp.exp(prow_ref[0:1, :]) * _softplus(m + prow_ref[1:2, :])
    return jnp.where(lane < GDN_HEADS, _sigmoid(m), jnp.where(lane < 3 * GDN_HEADS, gate, 0.0))


def _qkv_post(conv, qkvn_ref):
    c = _silu(conv)
    for hb in range(2 * GDN_HEADS):
        xh = c[:, hb * HEAD_DIM:(hb + 1) * HEAD_DIM]
        n = xh * lax.rsqrt(jnp.sum(xh * xh, axis=-1, keepdims=True) + EPS)
        if hb < GDN_HEADS:
            n = n * HEAD_DIM ** -0.5
        qkvn_ref[:, hb * HEAD_DIM:(hb + 1) * HEAD_DIM] = n
    qkvn_ref[:, 2 * GDN_W:] = c[:, 2 * GDN_W:]


def _inproj_prompt_kernel(tm, parts, x_ref, sc_ref, sh_ref, g1_ref, wn_ref, wt_ref, wconv_ref, cs128_ref, cs64_ref,
                          prow_ref, pcol_ref, tril_ref,
                          qkvn_ref, z_ref, vnat_ref, knat_ref, ikT_ref, tail_ref, misc_ref, gct_ref,
                          qT_ref, iqT_ref, iwT_ref, kbf_ref, vT_ref, ikbf_ref,
                          xs_ref, rt_ref):
    i = pl.program_id(1)

    @pl.when(i == 0)
    def _():
        xs_ref[0:8, :] = jnp.zeros((8, QKV_W), F32)

    @pl.when(i > 0)
    def _():
        xs_ref[0:8, :] = xs_ref[tm:tm + 8, :]

    th = tm // parts
    for n in range(parts):
        rows, cols = pl.ds(n * th, th), pl.ds(n * th, th)
        _inproj_prompt_rows(
            th, x_ref.at[rows, :], sc_ref, sh_ref, g1_ref, wn_ref, wt_ref, wconv_ref,
            cs128_ref.at[:, cols], cs64_ref.at[:, cols], prow_ref, pcol_ref, tril_ref.at[0:th, 0:th],
            qkvn_ref.at[rows, :], z_ref.at[rows, :], vnat_ref.at[pl.ds(n * th * N_KV, th * N_KV), :],
            knat_ref.at[pl.ds(n * th * N_KV, th * N_KV), :], ikT_ref.at[:, cols], tail_ref, misc_ref.at[rows, :],
            gct_ref.at[pl.ds(n * (th // GDN_CHUNK), th // GDN_CHUNK)],
            qT_ref.at[:, cols], iqT_ref.at[:, cols], iwT_ref.at[:, cols], kbf_ref.at[rows, :],
            vT_ref.at[pl.ds(n * (th // ATT_TK), th // ATT_TK)], ikbf_ref.at[rows, :],
            xs_ref.at[pl.ds(n * th, th + 8), :], rt_ref.at[:, cols])


def _inproj_prompt_rows(tm, x_ref, sc_ref, sh_ref, g1_ref, wn_ref, wt_ref, wconv_ref, cs128_ref, cs64_ref,
                        prow_ref, pcol_ref, tril_ref,
                        qkvn_ref, z_ref, vnat_ref, knat_ref, ikT_ref, tail_ref, misc_ref, gct_ref,
                        qT_ref, iqT_ref, iwT_ref, kbf_ref, vT_ref, ikbf_ref,
                        xs_ref, rt_ref):
    nat, mt = _project(x_ref, sc_ref, sh_ref, g1_ref, wn_ref, wt_ref, cs128_ref, cs64_ref, pcol_ref, rt_ref)
    xs_ref[8:tm + 8, :] = nat[:, 0:QKV_W]
    conv = wconv_ref[0:1, :] * xs_ref[5:tm + 5, :]
    for t in range(1, CONV_W):
        conv = conv + wconv_ref[t:t + 1, :] * xs_ref[5 + t:tm + 5 + t, :]
    tail_ref[...] = xs_ref[tm:tm + 8, :]
    _qkv_post(conv, qkvn_ref)
    z_ref[...] = nat[:, QKV_W:QKV_W + GDN_W]
    v0 = QKV_W + GDN_W
    for hd in range(N_KV):
        vnat_ref[pl.ds(hd, tm, stride=N_KV), :] = nat[:, v0 + hd * HEAD_DIM:v0 + (hd + 1) * HEAD_DIM]

    gm = _misc_natural(nat[:, NAT_W - LANES:], prow_ref)
    tril = tril_ref[...]
    hi, mid, lo = _split3(gm)
    gc = _dot(tril, hi) + (_dot(tril, mid) + _dot(tril, lo))
    lane = lax.broadcasted_iota(I32, gm.shape, 1)
    misc_ref[...] = jnp.where(lane < 2 * GDN_HEADS, gm, gc)
    hi, mid, lo = _split3(mt)
    gct = _dot_nt(hi, tril) + (_dot_nt(mid, tril) + _dot_nt(lo, tril))
    r = lax.broadcasted_iota(I32, mt.shape, 0)
    bg = jnp.where(r < IDX_HEADS + GDN_HEADS, mt, gct)[IDX_HEADS:MISC_ROWS, :]
    for cc in range(tm // GDN_CHUNK):
        gct_ref[cc] = bg[:, cc * GDN_CHUNK:(cc + 1) * GDN_CHUNK]

    iwT_ref[...] = mt[0:IDX_HEADS, :]
    qT_ref[...] = rt_ref[T_AQ:T_AQ + ATT_W, :].astype(BF)
    iqT_ref[...] = rt_ref[T_IQ:T_IQ + IDX_W, :].astype(BF)
    kn = rt_ref[T_AK:T_AK + KV_W, :].T
    for hd in range(N_KV):
        knat_ref[pl.ds(hd, tm, stride=N_KV), :] = kn[:, hd * HEAD_DIM:(hd + 1) * HEAD_DIM]
    kbf_ref[...] = kn.astype(BF)
    ikT_ref[...] = rt_ref[T_IK:T_IK + IDX_DIM, :]
    ikbf_ref[...] = rt_ref[T_IK:T_IK + LANES, :].T[:, 0:IDX_DIM].astype(BF)
    for cc in range(tm // ATT_TK):
        for hd in range(N_KV):
            lo = hd * VT_ROWS
            vT_ref[cc, lo:lo + HEAD_DIM, :] = rt_ref[T_AV + hd * HEAD_DIM:T_AV + (hd + 1) * HEAD_DIM,
                                                     cc * ATT_TK:(cc + 1) * ATT_TK].astype(BF)
            vT_ref[cc, lo + HEAD_DIM:lo + VT_ROWS, :] = jnp.ones((VT_ROWS - HEAD_DIM, ATT_TK), BF)


def _inproj_sample_kernel(x_ref, sc_ref, sh_ref, g1_ref, wn_ref, wt_ref, wconv_ref, cs128_ref, cs64_ref,
                          prow_ref, pcol_ref, s0_ref, s1_ref, s2_ref,
                          qkvn_ref, z_ref, vnat_ref, knat_ref, slab_ref, raw_ref, misc_ref, aq_ref, iq_ref,
                          rt_ref):
    nat, _ = _project(x_ref, sc_ref, sh_ref, g1_ref, wn_ref, wt_ref, cs128_ref, cs64_ref, pcol_ref, rt_ref)
    raw = nat[:, 0:QKV_W]
    raw_ref[...] = raw
    conv = wconv_ref[0:1, :] * s0_ref[...]
    conv = conv + wconv_ref[1:2, :] * s1_ref[...]
    conv = conv + wconv_ref[2:3, :] * s2_ref[...]
    conv = conv + wconv_ref[3:4, :] * raw
    _qkv_post(conv, qkvn_ref)
    z_ref[...] = nat[:, QKV_W:QKV_W + GDN_W]
    vnat_ref[...] = nat[:, QKV_W + GDN_W:QKV_W + GDN_W + KV_W]
    misc_ref[...] = _misc_natural(nat[:, NAT_W - LANES:], prow_ref)
    aq_ref[...] = rt_ref[T_AQ:T_AQ + ATT_W, :].T
    iq_ref[...] = rt_ref[T_IQ:T_IQ + IDX_W, :].T
    knat_ref[...] = rt_ref[T_AK:T_AK + KV_W, :].T
    slab_ref[...] = rt_ref[T_IK:T_IK + LANES, :].T


def _inproj_weights(w_in, a_log, dt_bias):
    offs = np.cumsum([0, QKV_W, GDN_W, GDN_HEADS, GDN_HEADS, ATT_W, KV_W, KV_W, IDX_W, IDX_DIM, IDX_HEADS])
    qkv, z, beta, a, aq, ak, av, iq, ik, iw = [w_in[:, offs[n]:offs[n + 1]] for n in range(10)]
    d = w_in.shape[0]
    pad = jnp.zeros((d, LANES - 3 * GDN_HEADS), w_in.dtype)
    w_nat = jnp.concatenate([qkv, z, av, beta, a, a, pad], axis=1).astype(BF)
    w_t = jnp.concatenate([aq, ak, iq, ik, iw, beta, a, av], axis=1).T.astype(BF)
    zrow = jnp.zeros((LANES - 3 * GDN_HEADS,), F32)
    prow = jnp.stack([jnp.concatenate([jnp.zeros((GDN_HEADS,), F32), a_log, a_log, zrow]),
                      jnp.concatenate([jnp.zeros((GDN_HEADS,), F32), dt_bias, dt_bias, zrow])])
    z12 = jnp.zeros((IDX_HEADS + GDN_HEADS,), F32)
    pcol = jnp.stack([jnp.concatenate([z12, a_log]), jnp.concatenate([z12, dt_bias])], axis=1)
    return w_nat, w_t, prow, pcol


def _rope_tables(pos, dim):
    half = dim // 8
    inv = ROPE_THETA ** (-jnp.arange(half, dtype=F32) / half)
    ang = pos.astype(F32)[:, None] * inv[None, :]
    return jnp.concatenate([jnp.cos(ang).T, jnp.sin(ang).T], axis=0)


def _inproj_prompt(x, mod, g1, w_nat, w_t, w_conv, prow, pcol, tm):
    bp, tp, d = x.shape
    nt = tp // tm
    rows = bp * tp
    pos = jnp.arange(tp, dtype=I32)
    cs128, cs64 = _rope_tables(pos, HEAD_DIM), _rope_tables(pos, IDX_DIM)
    ri = np.arange(tm)
    tril = jnp.asarray((ri[:, None] // GDN_CHUNK == ri[None, :] // GDN_CHUNK) & (ri[None, :] <= ri[:, None]), BF)
    const = lambda shape: pl.BlockSpec(shape, lambda b, i: (0,) * len(shape))
    rowblk = lambda w: pl.BlockSpec((tm, w), lambda b, i: (b * nt + i, 0))
    colblk = lambda h: pl.BlockSpec((h, tm), lambda b, i: (0, b * nt + i))
    in_specs = [
        pl.BlockSpec((None, tm, d), lambda b, i: (b, i, 0)),
        pl.BlockSpec((None, 1, d), lambda b, i: (b, 0, 1)),
        pl.BlockSpec((None, 1, d), lambda b, i: (b, 0, 0)),
        const((1, d)), const((d, NAT_W)), const((T_ROWS, d)), const((CONV_W, QKV_W)),
        pl.BlockSpec((HEAD_DIM // 4, tm), lambda b, i: (0, i)),
        pl.BlockSpec((IDX_DIM // 4, tm), lambda b, i: (0, i)),
        const((2, LANES)), const((MISC_ROWS, 2)), const((tm, tm)),
    ]
    out_shape = [
        jax.ShapeDtypeStruct((rows, QKV_W), F32), jax.ShapeDtypeStruct((rows, GDN_W), F32),
        jax.ShapeDtypeStruct((rows * N_KV, HEAD_DIM), F32), jax.ShapeDtypeStruct((rows * N_KV, HEAD_DIM), F32),
        jax.ShapeDtypeStruct((bp, IDX_DIM, tp), F32), jax.ShapeDtypeStruct((bp, 8, QKV_W), F32),
        jax.ShapeDtypeStruct((rows, LANES), F32), jax.ShapeDtypeStruct((rows // GDN_CHUNK, 8, GDN_CHUNK), F32),
        jax.ShapeDtypeStruct((ATT_W, rows), BF), jax.ShapeDtypeStruct((IDX_W, rows), BF),
        jax.ShapeDtypeStruct((IDX_HEADS, rows), F32), jax.ShapeDtypeStruct((rows, KV_W), BF),
        jax.ShapeDtypeStruct((rows // ATT_TK, N_KV * VT_ROWS, ATT_TK), BF), jax.ShapeDtypeStruct((rows, IDX_DIM), BF),
    ]
    out_specs = [
        rowblk(QKV_W), rowblk(GDN_W),
        pl.BlockSpec((tm * N_KV, HEAD_DIM), lambda b, i: (b * nt + i, 0)),
        pl.BlockSpec((tm * N_KV, HEAD_DIM), lambda b, i: (b * nt + i, 0)),
        pl.BlockSpec((None, IDX_DIM, tm), lambda b, i: (b, 0, i)),
        pl.BlockSpec((None, 8, QKV_W), lambda b, i: (b, 0, 0)),
        rowblk(LANES),
        pl.BlockSpec((tm // GDN_CHUNK, 8, GDN_CHUNK), lambda b, i: (b * nt + i, 0, 0)),
        colblk(ATT_W), colblk(IDX_W), colblk(IDX_HEADS), rowblk(KV_W),
        pl.BlockSpec((tm // ATT_TK, N_KV * VT_ROWS, ATT_TK), lambda b, i: (b * nt + i, 0, 0)),
        rowblk(IDX_DIM),
    ]
    return pl.pallas_call(
        functools.partial(_inproj_prompt_kernel, tm, tm // _pick(tm, (ATT_TK,))),
        grid=(bp, nt), in_specs=in_specs, out_specs=out_specs, out_shape=out_shape,
        scratch_shapes=[pltpu.VMEM((tm + 8, QKV_W), F32), pltpu.VMEM((T_ROWS, tm), F32)],
        compiler_params=_params(("arbitrary", "arbitrary")),
        name="inproj_prompt",
    )(x, mod, mod, g1.reshape(1, d), w_nat, w_t, w_conv, cs128, cs64, prow, pcol, tril)


def _inproj_sample(x, mod, g1, w_nat, w_t, w_conv, prow, pcol, conv_state, past):
    db, d = x.shape
    pos = jnp.full((db,), past, I32)
    cs128, cs64 = _rope_tables(pos, HEAD_DIM), _rope_tables(pos, IDX_DIM)
    full = lambda shape: pl.BlockSpec(shape, lambda i: (0,) * len(shape))
    in_specs = [
        full((db, d)),
        pl.BlockSpec((db, d), lambda i: (0, 1)), pl.BlockSpec((db, d), lambda i: (0, 0)),
        full((1, d)), full((d, NAT_W)), full((T_ROWS, d)), full((CONV_W, QKV_W)),
        full((HEAD_DIM // 4, db)), full((IDX_DIM // 4, db)), full((2, LANES)), full((MISC_ROWS, 2)),
        full((db, QKV_W)), full((db, QKV_W)), full((db, QKV_W)),
    ]
    widths = [QKV_W, GDN_W, KV_W, KV_W, LANES, QKV_W, LANES, ATT_W, IDX_W]
    return pl.pallas_call(
        _inproj_sample_kernel,
        grid=(1,), in_specs=in_specs,
        out_specs=[full((db, w)) for w in widths],
        out_shape=[jax.ShapeDtypeStruct((db, w), F32) for w in widths],
        scratch_shapes=[pltpu.VMEM((T_ROWS, db), F32)],
        compiler_params=_params(("arbitrary",)),
        name="inproj_sample",
    )(x, mod, mod, g1.reshape(1, d), w_nat, w_t, w_conv, cs128, cs64, prow, pcol,
      conv_state[:, 0], conv_state[:, 1], conv_state[:, 2])


def _unit_lower_inverses(lows):
    c = lows[0].shape[0]
    ii = lax.broadcasted_iota(I32, (c, c), 0)
    jj = lax.broadcasted_iota(I32, (c, c), 1)
    eye = jnp.where(ii == jj, 1.0, 0.0)
    levels = int(math.log2(c)) - 1
    ts = [eye - low for low in lows]
    ps = [_mm_hi(low, low) for low in lows]
    for lvl in range(levels):
        if lvl == levels - 1:
            ts = [t + _mm_hi(t, p) for t, p in zip(ts, ps)]
        else:
            both = [_mm_hi(jnp.concatenate([t, p], axis=0), p) for t, p in zip(ts, ps)]
            ts = [t + b[0:c] for t, b in zip(ts, both)]
            ps = [b[c:] for b in both]
    return ts


def _gdn_prompt_kernel(tg, qkvn_ref, z_ref, misc_ref, gct_ref, gn_ref, o_ref, ssm_ref, s_ref):
    i = pl.program_id(1)

    @pl.when(i == 0)
    def _():
        s_ref[...] = jnp.zeros(s_ref.shape, F32)

    c = GDN_CHUNK
    ii = lax.broadcasted_iota(I32, (c, c), 0)
    jj = lax.broadcasted_iota(I32, (c, c), 1)
    pairs = [(cc, hd) for cc in range(tg // c) for hd in range(GDN_HEADS)]

    qs, ks, gcs, rhs, lows, intras = {}, {}, {}, {}, [], {}
    for cc, hd in pairs:
        r0, lo = cc * c, hd * HEAD_DIM
        q = qkvn_ref[r0:r0 + c, lo:lo + HEAD_DIM]
        k = qkvn_ref[r0:r0 + c, GDN_W + lo:GDN_W + lo + HEAD_DIM]
        v = qkvn_ref[r0:r0 + c, 2 * GDN_W + lo:2 * GDN_W + lo + HEAD_DIM]
        beta_c = misc_ref[r0:r0 + c, hd:hd + 1]
        gc_c = misc_ref[r0:r0 + c, 2 * GDN_HEADS + hd:2 * GDN_HEADS + hd + 1]
        gc_r = gct_ref[cc, GDN_HEADS + hd:GDN_HEADS + hd + 1, :]
        decay = jnp.where(ii >= jj, jnp.exp(jnp.where(ii >= jj, gc_c - gc_r, 0.0)), 0.0)
        kb = k * beta_c
        both = _dot_nt(jnp.concatenate([kb, q], axis=0).astype(BF), k.astype(BF))
        lows.append(jnp.where(ii > jj, both[0:c] * decay, 0.0))
        intras[cc, hd] = (both[c:] * decay).astype(BF)
        rhs[cc, hd] = jnp.concatenate([v * beta_c, kb * jnp.exp(gc_c)], axis=1).astype(BF)
        qs[cc, hd], ks[cc, hd], gcs[cc, hd] = q, k, gc_c
    ts = _unit_lower_inverses(lows)
    uws = {p: _dot(t.astype(BF), rhs[p]) for p, t in zip(pairs, ts)}

    for cc, hd in pairs:
        r0, lo = cc * c, hd * HEAD_DIM
        q, k, gc_c, uw = qs[cc, hd], ks[cc, hd], gcs[cc, hd], uws[cc, hd]
        s = s_ref[hd]
        ws = _dot(jnp.concatenate([uw[:, HEAD_DIM:], q * jnp.exp(gc_c)], axis=0).astype(BF), s.astype(BF))
        v_new = (uw[:, 0:HEAD_DIM] - ws[0:c]).astype(BF)
        o = ws[c:] + _dot(intras[cc, hd], v_new)
        g_last = gc_c[c - 1:c, :]
        kd = k * jnp.exp(g_last - gc_c)
        s_ref[hd] = s * jnp.exp(g_last) + _dot(kd.T.astype(BF), v_new)
        o = _rmsnorm(o, gn_ref[...]) * _silu(z_ref[r0:r0 + c, lo:lo + HEAD_DIM])
        o_ref[r0:r0 + c, lo:lo + HEAD_DIM] = o.astype(BF)
    ssm_ref[...] = s_ref[...]


def _gdn_prompt(qkvn, z, misc, gct, g_norm, bp, tp, tg):
    nt = tp // tg
    rows = bp * tp
    rowblk = lambda w: pl.BlockSpec((tg, w), lambda b, i: (b * nt + i, 0))
    return pl.pallas_call(
        functools.partial(_gdn_prompt_kernel, tg),
        grid=(bp, nt),
        in_specs=[rowblk(QKV_W), rowblk(GDN_W), rowblk(LANES),
                  pl.BlockSpec((tg // GDN_CHUNK, 8, GDN_CHUNK), lambda b, i: (b * nt + i, 0, 0)),
                  pl.BlockSpec((1, HEAD_DIM), lambda b, i: (0, 0))],
        out_specs=[rowblk(GDN_W),
                   pl.BlockSpec((None, GDN_HEADS, HEAD_DIM, HEAD_DIM), lambda b, i: (b, 0, 0, 0))],
        out_shape=[jax.ShapeDtypeStruct((rows, GDN_W), BF),
                   jax.ShapeDtypeStruct((bp, GDN_HEADS, HEAD_DIM, HEAD_DIM), F32)],
        scratch_shapes=[pltpu.VMEM((GDN_HEADS, HEAD_DIM, HEAD_DIM), F32)],
        compiler_params=_params(("arbitrary", "arbitrary")),
        name="gdn_prompt",
    )(qkvn, z, misc, gct, g_norm.reshape(1, HEAD_DIM))


def _gdn_sample_kernel(nb, qkvn_ref, z_ref, misc_ref, gn_ref, s_ref, o_ref, ssm_ref):
    for bi in range(nb):
        for hd in range(GDN_HEADS):
            lo = hd * HEAD_DIM
            q = qkvn_ref[bi:bi + 1, lo:lo + HEAD_DIM]
            k = qkvn_ref[bi:bi + 1, GDN_W + lo:GDN_W + lo + HEAD_DIM]
            v = qkvn_ref[bi:bi + 1, 2 * GDN_W + lo:2 * GDN_W + lo + HEAD_DIM]
            beta = misc_ref[bi:bi + 1, hd:hd + 1]
            g = misc_ref[bi:bi + 1, GDN_HEADS + hd:GDN_HEADS + hd + 1]
            kcol = jnp.broadcast_to(k, (HEAD_DIM, HEAD_DIM)).T
            qcol = jnp.broadcast_to(q, (HEAD_DIM, HEAD_DIM)).T
            s = s_ref[bi, hd] * jnp.exp(g)
            ks = jnp.sum(kcol * s, axis=0, keepdims=True)
            delta = (v - ks) * beta
            s = s + kcol * delta
            ssm_ref[bi, hd] = s
            o = jnp.sum(qcol * s, axis=0, keepdims=True)
            o = _rmsnorm(o, gn_ref[...]) * _silu(z_ref[bi:bi + 1, lo:lo + HEAD_DIM])
            o_ref[bi:bi + 1, lo:lo + HEAD_DIM] = o.astype(BF)


def _gdn_sample(qkvn, z, misc, g_norm, state, nb):
    db = qkvn.shape[0]
    rowblk = lambda w: pl.BlockSpec((nb, w), lambda i: (i, 0))
    sblk = pl.BlockSpec((nb, GDN_HEADS, HEAD_DIM, HEAD_DIM), lambda i: (i, 0, 0, 0))
    return pl.pallas_call(
        functools.partial(_gdn_sample_kernel, nb),
        grid=(db // nb,),
        in_specs=[rowblk(QKV_W), rowblk(GDN_W), rowblk(LANES), pl.BlockSpec((1, HEAD_DIM), lambda i: (0, 0)), sblk],
        out_specs=[rowblk(GDN_W), sblk],
        out_shape=[jax.ShapeDtypeStruct((db, GDN_W), BF), jax.ShapeDtypeStruct(state.shape, F32)],
        compiler_params=_params(("arbitrary",)),
        name="gdn_sample",
    )(qkvn, z, misc, g_norm.reshape(1, HEAD_DIM), state)


def _ordered_word_to_float(u):
    s = u ^ INT_MIN
    return lax.bitcast_convert_type(s ^ ((s >> 31) & np.int32(0x7FFFFFFF)), F32)


def _select_topk_bias(i_ref, nch, tk, k_sel, taken=0.0, dropped=NEG_INF, group=1):
    nl = i_ref.shape[1]

    def chunk(c):
        return pl.ds(pl.multiple_of(c * tk, tk), tk)

    def count(ref, pred):
        def body(c, acc):
            m = pred(ref[chunk(c), :])
            ones = jnp.where(m, 1, 0).astype(I32).reshape(tk // 8, 8, nl)
            parts = [ones[n] for n in range(tk // 8)]
            while len(parts) > 1:
                parts = [a + b for a, b in zip(parts[0::2], parts[1::2])] + parts[len(parts) & ~1:]
            return acc + parts[0]
        acc = lax.fori_loop(0, nch, body, jnp.zeros((8, nl), I32))
        return jnp.sum(acc, axis=0, keepdims=True)

    def bit_body(i, carry):
        t_u, n_ge = carry
        cand_u = t_u | jnp.left_shift(jnp.int32(1), 31 - i)
        cand = _ordered_word_to_float(cand_u)
        cnt = count(i_ref, lambda sc: sc >= cand)
        take = cnt >= k_sel
        return jnp.where(take, cand_u, t_u), jnp.where(take, cnt, n_ge)

    t_u, n_ge = lax.fori_loop(0, 32, bit_body, (jnp.zeros((1, nl), I32), jnp.full((1, nl), -1, I32)))
    thr = jnp.where((t_u >> 23) == 0, NEG_INF, _ordered_word_to_float(t_u))

    excess = (n_ge - k_sel).astype(F32)
    ri = lax.broadcasted_iota(I32, (tk + 8, tk), 0)
    ci = lax.broadcasted_iota(I32, (tk + 8, tk), 1)
    above = jnp.where((ci > ri) | (ri >= tk), 1.0, 0.0).astype(BF)

    nsteps = (nch + group - 1) // group

    def write(n, later):
        scs = [i_ref[chunk((nsteps - 1 - n) * group + g), :] for g in reversed(range(group))]
        ties = [sc == thr for sc in scs]
        cnts = [_dot(above, jnp.where(t, 1.0, 0.0).astype(BF)) for t in ties]
        for g, sc, tie, cnt in zip(reversed(range(group)), scs, ties, cnts):
            sel = (sc > thr) | (tie & (cnt[0:tk] + later >= excess))
            i_ref[chunk((nsteps - 1 - n) * group + g), :] = jnp.where(sel & (sc > NEG_INF), taken, dropped)
            later = later + cnt[tk:tk + 1]
        return later

    lax.fori_loop(0, nsteps, write, jnp.zeros((1, nl), F32))


def _attn_prompt_kernel(tq, tk, blk, k_sel, qT_ref, iqT_ref, iwT_ref, kbf_ref, vT_ref, ikbf_ref, o_ref, i_ref):
    t0 = pl.program_id(1) * tq
    nch = (t0 + tq + tk - 1) // tk
    nblk = (t0 + tq + blk - 1) // blk
    qpos = t0 + lax.broadcasted_iota(I32, (1, tq), 1)
    row = lax.broadcasted_iota(I32, (tk, 1), 0)
    w = iwT_ref[...]
    pairs = IDX_HEADS // 2
    rhs = [jnp.concatenate([iqT_ref[(2 * p) * IDX_DIM:(2 * p + 1) * IDX_DIM, :],
                            iqT_ref[(2 * p + 1) * IDX_DIM:(2 * p + 2) * IDX_DIM, :]], axis=1) for p in range(pairs)]

    def chunk(c):
        return pl.ds(pl.multiple_of(c * tk, tk), tk)

    def idx_body(c, carry):
        ikc = ikbf_ref[chunk(c), :]
        acc = jnp.zeros((tk, tq), F32)
        for p in range(pairs):
            d = jnp.maximum(_dot(ikc, rhs[p]), 0.0)
            acc = acc + d[:, 0:tq] * w[2 * p:2 * p + 1, :] + d[:, tq:] * w[2 * p + 1:2 * p + 2, :]
        i_ref[chunk(c), :] = jnp.where(c * tk + row <= qpos, acc, NEG_INF)
        return carry

    lax.fori_loop(0, nch, idx_body, 0)

    def fill_body(c, carry):
        i_ref[chunk(c), :] = jnp.full((tk, tq), NEG_INF, F32)
        return carry

    lax.fori_loop(nch, nblk * (blk // tk), fill_body, 0)
    tks = max(tk * LANES // tq, 8)
    _select_topk_bias(i_ref, (t0 + tq + tks - 1) // tks, tks, k_sel, group=min(4, blk // tks))

    scale2 = HEAD_DIM ** -0.5 * math.log2(math.e)
    rep = ATT_HEADS // N_KV
    sub = ATT_TK
    nsub = blk // sub
    qg = [jnp.concatenate([qT_ref[(rep * g + r) * HEAD_DIM:(rep * g + r + 1) * HEAD_DIM, :] for r in range(rep)],
                          axis=1) for g in range(N_KV)]

    def body(c, carry):
        out = []
        for g in range(N_KV):
            m, acc = carry[g]
            ss = []
            for i in range(nsub):
                rows = pl.ds(pl.multiple_of(c * blk + i * sub, sub), sub)
                bias = i_ref[rows, :]
                ss.append(_dot(kbf_ref[rows, g * HEAD_DIM:(g + 1) * HEAD_DIM], qg[g]) * scale2
                          + jnp.concatenate([bias] * rep, axis=1))
            m_new = m
            for s in ss:
                m_new = jnp.maximum(m_new, jnp.max(s, axis=0, keepdims=True))
            m_safe = jnp.where(m_new == NEG_INF, 0.0, m_new)
            acc = acc * jnp.exp2(m - m_safe)
            for i, s in enumerate(ss):
                p = jnp.exp2(s - m_safe).astype(BF)
                acc = acc + _dot(vT_ref[c * nsub + i, g * VT_ROWS:(g + 1) * VT_ROWS, :], p)
            out.append((m_new, acc))
        return tuple(out)

    init = tuple((jnp.full((1, rep * tq), NEG_INF, F32), jnp.zeros((VT_ROWS, rep * tq), F32)) for _ in range(N_KV))
    res = lax.fori_loop(0, nblk, body, init)
    for g in range(N_KV):
        acc = res[g][1]
        o_t = acc[0:HEAD_DIM] / acc[HEAD_DIM:HEAD_DIM + 1]
        for r in range(rep):
            hd = rep * g + r
            o_ref[:, hd * HEAD_DIM:(hd + 1) * HEAD_DIM] = o_t[:, r * tq:(r + 1) * tq].T.astype(BF)


def _attn_prompt(qT, iqT, iwT, kbf, vT, ikbf, bp, tp):
    tq = _pick(tp, (ATT_TQ, QBLOCK))
    tk = _pick(tp, (IDX_TK, ATT_TK))
    blk = _pick(tp, (ATT_BLK, IDX_TK, ATT_TK))
    assert blk % tk == 0 and blk % ATT_TK == 0
    nq = tp // tq
    rows = bp * tp
    k_sel = min(TOPK_MAX, tp // 4)
    colblk = lambda h: pl.BlockSpec((h, tq), lambda b, j: (0, b * nq + j))
    return pl.pallas_call(
        functools.partial(_attn_prompt_kernel, tq, tk, blk, k_sel),
        grid=(bp, nq),
        in_specs=[colblk(ATT_W), colblk(IDX_W), colblk(IDX_HEADS),
                  pl.BlockSpec((tp, KV_W), lambda b, j: (b, 0)),
                  pl.BlockSpec((tp // ATT_TK, N_KV * VT_ROWS, ATT_TK), lambda b, j: (b, 0, 0)),
                  pl.BlockSpec((tp, IDX_DIM), lambda b, j: (b, 0))],
        out_specs=pl.BlockSpec((tq, ATT_W), lambda b, j: (b * nq + j, 0)),
        out_shape=jax.ShapeDtypeStruct((rows, ATT_W), BF),
        scratch_shapes=[pltpu.VMEM((tp, tq), F32)],
        compiler_params=_params(("arbitrary", "arbitrary")),
        name="attn_prompt",
    )(qT, iqT, iwT, kbf, vT, ikbf)


def _page_prefetch(npg, pt_ref, srcs, bufs, sem_ref):
    b = pl.program_id(0)
    steps = pl.num_programs(0)
    ring = bufs[0].shape[0]
    ahead = ring - 1
    slot = lax.rem(b, ring)

    def copy(n, step, sl, p):
        return pltpu.make_async_copy(srcs[n].at[pt_ref[step, p]], bufs[n].at[sl, p], sem_ref.at[n, sl, p])

    def start_all(step, sl):
        for n in range(len(srcs)):
            for p in range(npg):
                copy(n, step, sl, p).start()

    @pl.when(b == 0)
    def _():
        for step in range(ahead):
            start_all(step, step)

    @pl.when(b + ahead < steps)
    def _():
        start_all(b + ahead, lax.rem(b + ahead, ring))

    for n in range(len(srcs)):
        for p in range(npg):
            copy(n, b, slot, p).wait()
    return slot


def _sidx_kernel(npg, pt_ref, iq_ref, iw_ref, ikn_ref, cache_ref, out_ref, buf_ref, sem_ref):
    slot = _page_prefetch(npg, pt_ref, [cache_ref], [buf_ref], sem_ref)
    iq = iq_ref[...].astype(BF)
    w = iw_ref[...]
    for p in range(npg):
        d = jnp.maximum(_dot(iq, buf_ref[slot, p].astype(BF)), 0.0)
        out_ref[:, p * LANES:(p + 1) * LANES] = jnp.sum(d * w, axis=0, keepdims=True)
    dn = jnp.sum(iq.astype(F32) * ikn_ref[...].astype(BF).astype(F32), axis=1, keepdims=True)
    sn = jnp.sum(jnp.maximum(dn, 0.0) * w, axis=0, keepdims=True)
    lane = lax.broadcasted_iota(I32, (1, LANES), 1)
    out_ref[:, npg * LANES:(npg + 1) * LANES] = jnp.where(lane == 0, sn, NEG_INF)


def _sample_index_scores(iq, iw, ik_new, cache_ik_t, page_table):
    db, npg = page_table.shape
    page = cache_ik_t.shape[2]
    assert page == LANES and db > IDX_RING
    width = (npg + 1) * LANES
    grid_spec = pltpu.PrefetchScalarGridSpec(
        num_scalar_prefetch=1, grid=(db,),
        in_specs=[pl.BlockSpec((None, IDX_HEADS, IDX_DIM), lambda b, pt: (b, 0, 0)),
                  pl.BlockSpec((None, IDX_HEADS, 1), lambda b, pt: (b, 0, 0)),
                  pl.BlockSpec((None, 1, IDX_DIM), lambda b, pt: (b, 0, 0)),
                  pl.BlockSpec(memory_space=pl.ANY)],
        out_specs=pl.BlockSpec((None, 1, width), lambda b, pt: (b, 0, 0)),
        scratch_shapes=[pltpu.VMEM((IDX_RING, npg, IDX_DIM, page), F32),
                        pltpu.SemaphoreType.DMA((1, IDX_RING, npg))])
    out = pl.pallas_call(
        functools.partial(_sidx_kernel, npg), grid_spec=grid_spec,
        out_shape=jax.ShapeDtypeStruct((db, 1, width), F32),
        compiler_params=_params(("arbitrary",)),
        name="sample_index_scores",
    )(page_table, iq.reshape(db, IDX_HEADS, IDX_DIM), iw.reshape(db, IDX_HEADS, 1), ik_new.reshape(db, 1, IDX_DIM),
      cache_ik_t)
    return out.reshape(db, width)


def _ssel_kernel(nch, k_sel, s_ref, spread_ref, keep_ref, i_ref):
    for p in range(nch):
        i_ref[p * LANES:(p + 1) * LANES, :] = s_ref[:, p * LANES:(p + 1) * LANES].T
    _select_topk_bias(i_ref, nch, LANES, k_sel, taken=1.0, dropped=0.0)
    rows = N_KV * LANES
    for p in range(nch - 1):
        k2 = _dot(spread_ref[...], i_ref[p * LANES:(p + 1) * LANES, :].astype(BF))
        for h in range(N_KV):
            keep_ref[:, p * rows + h * LANES:p * rows + (h + 1) * LANES] = k2[h * LANES:(h + 1) * LANES, :].T
    keep_ref[:, (nch - 1) * rows:(nch - 1) * rows + LANES] = i_ref[(nch - 1) * LANES:nch * LANES, :].T


def _sample_select(scores, n_keys):
    db, width = scores.shape
    assert db == LANES
    nch = width // LANES
    k_sel = min(TOPK_MAX, n_keys // 4)
    rows = N_KV * LANES
    out_w = (nch - 1) * rows + LANES
    spread = jnp.asarray(np.arange(rows)[:, None] // N_KV == np.arange(LANES)[None, :], BF)
    return pl.pallas_call(
        functools.partial(_ssel_kernel, nch, k_sel),
        grid=(1,),
        in_specs=[pl.BlockSpec((db, width), lambda i: (0, 0)), pl.BlockSpec((rows, LANES), lambda i: (0, 0))],
        out_specs=pl.BlockSpec((db, out_w), lambda i: (0, 0)),
        out_shape=jax.ShapeDtypeStruct((db, out_w), F32),
        scratch_shapes=[pltpu.VMEM((width, db), F32)],
        compiler_params=_params(("arbitrary",)),
        name="sample_select",
    )(scores, spread)


def _sattn_kernel(npg, pt_ref, q_ref, keep_ref, kn_ref, vn_ref, ck_ref, cv_ref, o_ref, kbuf_ref, vbuf_ref, sem_ref):
    slot = _page_prefetch(npg, pt_ref, [ck_ref, cv_ref], [kbuf_ref, vbuf_ref], sem_ref)
    rep = ATT_HEADS // N_KV
    rows = N_KV * LANES
    scale = HEAD_DIM ** -0.5
    q = q_ref[...]
    q8 = jnp.concatenate([q, jnp.zeros((8 - ATT_HEADS, HEAD_DIM), F32)], axis=0).astype(BF)
    hrow = lax.broadcasted_iota(I32, (8, 1), 0)
    lane = lax.broadcasted_iota(I32, (1, LANES), 1)
    own_kv = lax.broadcasted_iota(I32, (1, rows), 1) % N_KV == hrow // rep
    parts = []
    for p in range(npg):
        s2 = _dot_nt(q8, kbuf_ref[slot, p].astype(BF))
        keep2 = keep_ref[:, p * rows:(p + 1) * rows]
        parts.append(jnp.where(own_kv & (keep2 > 0.5), s2 * scale, NEG_INF))
    kn = kn_ref[...].astype(BF).astype(F32)
    kn8 = jnp.where(hrow // rep == 0, kn[0:1, :], kn[1:2, :])
    s_new = jnp.sum(q8.astype(F32) * kn8, axis=1, keepdims=True)
    keep_new = keep_ref[:, npg * rows:npg * rows + 1]
    parts.append(jnp.where((lane == 0) & (keep_new > 0.5), s_new * scale, NEG_INF))
    s = jnp.concatenate(parts, axis=1)
    m = jnp.max(s, axis=1, keepdims=True)
    e = jnp.exp(s - m)
    pr = (e / jnp.sum(e, axis=1, keepdims=True)).astype(BF)
    o8 = jnp.zeros((8, HEAD_DIM), F32)
    for p in range(npg):
        o8 = o8 + _dot(pr[:, p * rows:(p + 1) * rows], vbuf_ref[slot, p].astype(BF))
    vn = vn_ref[...].astype(BF).astype(F32)
    vn8 = jnp.where(hrow // rep == 0, vn[0:1, :], vn[1:2, :])
    o8 = o8 + pr[:, npg * rows:npg * rows + 1].astype(F32) * vn8
    for hd in range(ATT_HEADS):
        o_ref[:, hd * HEAD_DIM:(hd + 1) * HEAD_DIM] = o8[hd:hd + 1, :].astype(BF)


def _sample_attention(q, keep, k_new, v_new, cache_k, cache_v, page_table):
    db, npg = page_table.shape
    n_phys, page = cache_k.shape[0], cache_k.shape[1]
    assert page == LANES and N_KV == 2 and db > KV_RING
    rows = page * N_KV
    width = npg * rows + LANES
    ck = cache_k.reshape(n_phys, rows, HEAD_DIM)
    cv = cache_v.reshape(n_phys, rows, HEAD_DIM)
    grid_spec = pltpu.PrefetchScalarGridSpec(
        num_scalar_prefetch=1, grid=(db,),
        in_specs=[pl.BlockSpec((None, ATT_HEADS, HEAD_DIM), lambda b, pt: (b, 0, 0)),
                  pl.BlockSpec((None, 1, width), lambda b, pt: (b, 0, 0)),
                  pl.BlockSpec((None, N_KV, HEAD_DIM), lambda b, pt: (b, 0, 0)),
                  pl.BlockSpec((None, N_KV, HEAD_DIM), lambda b, pt: (b, 0, 0)),
                  pl.BlockSpec(memory_space=pl.ANY), pl.BlockSpec(memory_space=pl.ANY)],
        out_specs=pl.BlockSpec((None, 1, ATT_W), lambda b, pt: (b, 0, 0)),
        scratch_shapes=[pltpu.VMEM((KV_RING, npg, rows, HEAD_DIM), F32), pltpu.VMEM((KV_RING, npg, rows, HEAD_DIM), F32),
                        pltpu.SemaphoreType.DMA((2, KV_RING, npg))])
    out = pl.pallas_call(
        functools.partial(_sattn_kernel, npg), grid_spec=grid_spec,
        out_shape=jax.ShapeDtypeStruct((db, 1, ATT_W), BF),
        compiler_params=_params(("arbitrary",)),
        name="sample_attention",
    )(page_table, q.reshape(db, ATT_HEADS, HEAD_DIM), keep.reshape(db, 1, width),
      k_new.reshape(db, N_KV, HEAD_DIM), v_new.reshape(db, N_KV, HEAD_DIM), ck, cv)
    return out.reshape(db, ATT_W)


def _post_kernel(final, nff, og_ref, oa_ref, x_ref, ga1_ref, sc2_ref, sh2_ref, ga2_ref, g2_ref, gf_ref,
                 wo_ref, wg_ref, wu_ref, wd_ref, y_ref, x1_ref, h2_ref, acc_ref):
    jf = pl.program_id(1)

    @pl.when(jf == 0)
    def _():
        tm = x_ref.shape[0]
        th = _pick(tm, (256,))

        def rows_of(ref, r):
            return ref[...] if ref.shape[0] == 1 else ref[r, :]

        for n in range(tm // th):
            r = pl.ds(n * th, th)
            mixed = _dot(og_ref[r, :], wo_ref[0:GDN_W, :]) + _dot(oa_ref[r, :], wo_ref[GDN_W:, :])
            x1 = x_ref[r, :] + rows_of(ga1_ref, r) * mixed
            x1_ref[r, :] = x1
            h2_ref[r, :] = (_rmsnorm(x1, g2_ref[...]) * (1.0 + rows_of(sc2_ref, r)) + rows_of(sh2_ref, r)).astype(BF)
        acc_ref[...] = jnp.zeros(acc_ref.shape, F32)

    h2 = h2_ref[...]
    act = (_silu(_dot(h2, wg_ref[...])) * _dot(h2, wu_ref[...])).astype(BF)
    acc_ref[...] += _dot(act, wd_ref[...])

    @pl.when(jf == nff - 1)
    def _():
        x2 = x1_ref[...] + ga2_ref[...] * acc_ref[...]
        y_ref[...] = _rmsnorm(x2, gf_ref[...]) if final else x2


def _post(og, oa, x2d, mod, mod_spec, g2, gf, wo, wfi, wfo, tm, tf, final):
    rows, d = x2d.shape
    dff = wfo.shape[0]
    nff = dff // tf
    rowblk = lambda w: pl.BlockSpec((tm, w), lambda i, j: (i, 0))
    const = lambda shape: pl.BlockSpec(shape, lambda i, j: (0,) * len(shape))
    in_specs = [rowblk(GDN_W), rowblk(ATT_W), rowblk(d),
                mod_spec(2), mod_spec(4), mod_spec(3), mod_spec(5),
                const((1, d)), const((1, d)), const((d, d)),
                pl.BlockSpec((d, tf), lambda i, j: (0, j)),
                pl.BlockSpec((d, tf), lambda i, j: (0, nff + j)),
                pl.BlockSpec((tf, d), lambda i, j: (j, 0))]
    return pl.pallas_call(
        functools.partial(_post_kernel, final, nff),
        grid=(rows // tm, nff), in_specs=in_specs, out_specs=rowblk(d),
        out_shape=jax.ShapeDtypeStruct((rows, d), F32),
        scratch_shapes=[pltpu.VMEM((tm, d), F32), pltpu.VMEM((tm, d), BF), pltpu.VMEM((tm, d), F32)],
        compiler_params=_params(("arbitrary", "arbitrary")),
        name="post",
    )(og, oa, x2d, mod, mod, mod, mod, g2.reshape(1, d), gf.reshape(1, d), wo, wfi, wfi, wfo)


def _pick(n, prefs):
    for p in prefs:
        if n % p == 0:
            return p
    return n


def kernel(x_prompt, x_sample, c_prompt, c_sample, cache_k, cache_v, cache_idx_k, page_table, state_conv, state_ssm,
           w_ada, b_ada, g_norm1, w_in, w_conv, a_log, dt_bias, g_gdn_norm, w_out, g_norm2, w_ffn_in, w_ffn_out,
           g_final):
    bp, tp, d = x_prompt.shape
    db, ts, _ = x_sample.shape
    assert ts == 1 and d == GDN_W + ATT_W and tp % max(ATT_TK, GDN_CHUNK) == 0
    depth = w_in.shape[0]
    npg, page = page_table.shape[1], cache_k.shape[2]
    past = npg * page
    tm_in = _pick(tp, (512, 256))
    tg = _pick(tp, (512, 256))
    tm_post = _pick(tp, (1024, 512, 256))
    dff = w_ffn_out.shape[1]
    tf = _pick(dff, (256, 128))

    xp = x_prompt
    xs = x_sample.reshape(db, d)
    npad = (-(bp + db)) % 8
    c_all = jnp.concatenate([c_prompt, c_sample, jnp.zeros((npad, d), F32)], axis=0)
    new_p, new_s = [], []
    for l in range(depth):
        final = l == depth - 1
        mod = _modulation(c_all, w_ada[l], b_ada[l])
        mod_p = mod[:bp].reshape(bp, 1, 6 * d)
        mod_s = mod[bp:bp + db]
        w_nat, w_t, prow, pcol = _inproj_weights(w_in[l], a_log[l], dt_bias[l])
        wo = w_out[l].astype(BF)
        wfi = w_ffn_in[l].astype(BF)
        wfo = w_ffn_out[l].astype(BF)

        (qkvn, z, vnat, knat, ikT, tail, misc, gct, qT, iqT, iwT, kbf, vT, ikbf) = _inproj_prompt(
            xp, mod_p, g_norm1[l], w_nat, w_t, w_conv[l], prow, pcol, tm_in)
        og, ssm_p = _gdn_prompt(qkvn, z, misc, gct, g_gdn_norm[l], bp, tp, tg)
        oa = _attn_prompt(qT, iqT, iwT, kbf, vT, ikbf, bp, tp)
        tiles_b = tp // tm_post
        spec_p = lambda k: pl.BlockSpec((None, 1, d), lambda i, j: (i // tiles_b, 0, k))
        xp = _post(og, oa, xp.reshape(bp * tp, d), mod_p, spec_p, g_norm2[l], g_final, wo, wfi, wfo,
                   tm_post, tf, final).reshape(bp, tp, d)
        new_p.append((knat.reshape(bp, tp, N_KV, HEAD_DIM), vnat.reshape(bp, tp, N_KV, HEAD_DIM),
                      jnp.swapaxes(ikT, 1, 2), tail[:, 8 - (CONV_W - 1):, :], ssm_p))

        (qkvn_s, z_s, vnat_s, knat_s, slab_s, raw_s, misc_s, aq_s, iq_s) = _inproj_sample(
            xs, mod_s, g_norm1[l], w_nat, w_t, w_conv[l], prow, pcol, state_conv[l], past)
        og_s, ssm_s = _gdn_sample(qkvn_s, z_s, misc_s, g_gdn_norm[l], state_ssm[l], _pick(db, (8,)))
        ik_s = slab_s[:, 0:IDX_DIM]
        iw_s = slab_s[:, IDX_DIM:IDX_DIM + IDX_HEADS]
        scores = _sample_index_scores(iq_s, iw_s, ik_s, jnp.swapaxes(cache_idx_k[l], 1, 2), page_table)
        keep = _sample_select(scores, past + ts)
        oa_s = _sample_attention(aq_s, keep, knat_s, vnat_s, cache_k[l], cache_v[l], page_table)
        spec_s = lambda k: pl.BlockSpec((db, d), lambda i, j: (i, k))
        xs = _post(og_s, oa_s, xs, mod_s, spec_s, g_norm2[l], g_final, wo, wfi, wfo, db, tf, final)
        new_conv_s = jnp.concatenate([state_conv[l][:, 1:, :], raw_s[:, None, :]], axis=1)
        new_s.append((knat_s.reshape(db, ts, N_KV, HEAD_DIM), vnat_s.reshape(db, ts, N_KV, HEAD_DIM),
                      ik_s.reshape(db, ts, IDX_DIM), new_conv_s, ssm_s))

    stack = lambda states, n: jnp.stack([s[n] for s in states], axis=0)
    return (xp, xs.reshape(db, ts, d),
            stack(new_p, 0), stack(new_p, 1), stack(new_p, 2), stack(new_p, 3), stack(new_p, 4),
            stack(new_s, 0), stack(new_s, 1), stack(new_s, 2), stack(new_s, 3), stack(new_s, 4))
```

```python
import functools
import math

import numpy as np
import jax
import jax.numpy as jnp
from jax import lax
from jax.experimental import pallas as pl
from jax.experimental.pallas import tpu as pltpu

F32 = jnp.float32
BF = jnp.bfloat16
I32 = jnp.int32

HEAD_DIM = 128
GDN_HEADS = 4
ATT_HEADS = 4
N_KV = 2
IDX_HEADS = 8
IDX_DIM = 64
CONV_W = 4
TOPK_MAX = 256
QBLOCK = 128
ROPE_THETA = 500000.0
EPS = 1e-6
GDN_W = GDN_HEADS * HEAD_DIM
ATT_W = ATT_HEADS * HEAD_DIM
KV_W = N_KV * HEAD_DIM
IDX_W = IDX_HEADS * IDX_DIM
QKV_W = 3 * GDN_W

LANES = 128
BF16_ROWS = 16
GDN_CHUNK = 128
ATT_TK = 256
VT_ROWS = HEAD_DIM + BF16_ROWS
IDX_TK = 512
ATT_BLK = 1024
ATT_TQ = 256
IDX_RING = 5
KV_RING = 3
NAT_W = QKV_W + GDN_W + KV_W + LANES
MISC_ROWS = IDX_HEADS + 2 * GDN_HEADS
T_AQ = 0
T_AK = T_AQ + ATT_W
T_IQ = T_AK + KV_W
T_IK = T_IQ + IDX_W
T_MISC = T_IK + IDX_DIM
T_AV = T_MISC + MISC_ROWS
T_ROWS = T_AV + KV_W
VMEM_LIMIT = 56 * 1024 * 1024
INT_MIN = np.int32(-2 ** 31)
NEG_INF = float("-inf")


def _params(sem):
    return pltpu.CompilerParams(dimension_semantics=sem, vmem_limit_bytes=VMEM_LIMIT)


def _dot(a, b):
    return jnp.dot(a, b, preferred_element_type=F32)


def _dot_nt(a, b):
    return lax.dot_general(a, b, (((1,), (1,)), ((), ())), preferred_element_type=F32)


def _split3(x):
    hi = x.astype(BF)
    r1 = x - hi.astype(F32)
    mid = r1.astype(BF)
    lo = (r1 - mid.astype(F32)).astype(BF)
    return hi, mid, lo


def _mm_hi(a, b):
    ah = a.astype(BF)
    al = (a - ah.astype(F32)).astype(BF)
    bh = b.astype(BF)
    bl = (b - bh.astype(F32)).astype(BF)
    m = a.shape[0]
    hi = _dot(jnp.concatenate([ah, al], axis=0), bh)
    return hi[0:m] + (hi[m:] + _dot(ah, bl))


def _sigmoid(x):
    return 1.0 / (1.0 + jnp.exp(-x))


def _silu(x):
    return x * _sigmoid(x)


def _softplus(x):
    return jnp.maximum(x, 0.0) + jnp.log(1.0 + jnp.exp(-jnp.abs(x)))


def _rmsnorm(x, g):
    return x * lax.rsqrt(jnp.mean(x * x, axis=-1, keepdims=True) + EPS) * g


def _mod_kernel(c_ref, w_ref, b_ref, o_ref):
    s = _silu(c_ref[...]).astype(BF)
    o_ref[...] = _dot(s, w_ref[...].astype(BF)) + b_ref[...]


def _modulation(c_all, w_ada, b_ada):
    n, d = c_all.shape
    cols = w_ada.shape[1]
    tn = d
    return pl.pallas_call(
        _mod_kernel,
        grid=(cols // tn,),
        in_specs=[pl.BlockSpec((n, d), lambda j: (0, 0)),
                  pl.BlockSpec((d, tn), lambda j: (0, j)),
                  pl.BlockSpec((1, tn), lambda j: (0, j))],
        out_specs=pl.BlockSpec((n, tn), lambda j: (0, j)),
        out_shape=jax.ShapeDtypeStruct((n, cols), F32),
        compiler_params=_params(("arbitrary",)),
        name="modulation",
    )(c_all, w_ada, b_ada.reshape(1, cols))


def _rope_rows(rt_ref, base, half, cos, sin):
    x1 = rt_ref[base:base + half, :]
    x2 = rt_ref[base + half:base + 2 * half, :]
    rt_ref[base:base + half, :] = x1 * cos - x2 * sin
    rt_ref[base + half:base + 2 * half, :] = x2 * cos + x1 * sin


def _project(x_ref, sc_ref, sh_ref, g1_ref, wn_ref, wt_ref, cs128_ref, cs64_ref, pcol_ref, rt_ref):
    h = _rmsnorm(x_ref[...], g1_ref[...]) * (1.0 + sc_ref[...]) + sh_ref[...]
    hb = h.astype(BF)
    nat = _dot(hb, wn_ref[...])
    rt_ref[...] = _dot_nt(wt_ref[...], hb)
    half = HEAD_DIM // 8
    cos, sin = cs128_ref[0:half, :], cs128_ref[half:2 * half, :]
    for hd in range(ATT_HEADS):
        _rope_rows(rt_ref, T_AQ + hd * HEAD_DIM, half, cos, sin)
    for hd in range(N_KV):
        _rope_rows(rt_ref, T_AK + hd * HEAD_DIM, half, cos, sin)
    half = IDX_DIM // 8
    cos, sin = cs64_ref[0:half, :], cs64_ref[half:2 * half, :]
    for hd in range(IDX_HEADS):
        _rope_rows(rt_ref, T_IQ + hd * IDX_DIM, half, cos, sin)
    _rope_rows(rt_ref, T_IK, half, cos, sin)
    mt = rt_ref[T_MISC:T_MISC + MISC_ROWS, :]
    r = lax.broadcasted_iota(I32, mt.shape, 0)
    a_log, dt_b = pcol_ref[:, 0:1], pcol_ref[:, 1:2]
    gate = -jnp.exp(a_log) * _softplus(mt + dt_b)
    mt = jnp.where(r < IDX_HEADS, mt * IDX_HEADS ** -0.5, jnp.where(r < IDX_HEADS + GDN_HEADS, _sigmoid(mt), gate))
    rt_ref[T_MISC:T_MISC + MISC_ROWS, :] = mt
    return nat, mt


def _misc_natural(m, prow_ref):
    lane = lax.broadcasted_iota(I32, m.shape, 1)
    gate = -jnp.exp(prow_ref[0:1, :]) * _softplus(m + prow_ref[1:2, :])
    return jnp.where(lane < GDN_HEADS, _sigmoid(m), jnp.where(lane < 3 * GDN_HEADS, gate, 0.0))


def _qkv_post(conv, qkvn_ref):
    c = _silu(conv)
    for hb in range(2 * GDN_HEADS):
        xh = c[:, hb * HEAD_DIM:(hb + 1) * HEAD_DIM]
        n = xh * lax.rsqrt(jnp.sum(xh * xh, axis=-1, keepdims=True) + EPS)
        if hb < GDN_HEADS:
            n = n * HEAD_DIM ** -0.5
        qkvn_ref[:, hb * HEAD_DIM:(hb + 1) * HEAD_DIM] = n
    qkvn_ref[:, 2 * GDN_W:] = c[:, 2 * GDN_W:]


def _inproj_prompt_kernel(tm, parts, x_ref, sc_ref, sh_ref, g1_ref, wn_ref, wt_ref, wconv_ref, cs128_ref, cs64_ref,
                          prow_ref, pcol_ref, tril_ref,
                          qkvn_ref, z_ref, vnat_ref, knat_ref, ikT_ref, tail_ref, misc_ref, gct_ref,
                          qT_ref, iqT_ref, iwT_ref, kbf_ref, vT_ref, ikbf_ref,
                          xs_ref, rt_ref):
    i = pl.program_id(1)

    @pl.when(i == 0)
    def _():
        xs_ref[0:8, :] = jnp.zeros((8, QKV_W), F32)

    @pl.when(i > 0)
    def _():
        xs_ref[0:8, :] = xs_ref[tm:tm + 8, :]

    th = tm // parts
    for n in range(parts):
        rows, cols = pl.ds(n * th, th), pl.ds(n * th, th)
        _inproj_prompt_rows(
            th, x_ref.at[rows, :], sc_ref, sh_ref, g1_ref, wn_ref, wt_ref, wconv_ref,
            cs128_ref.at[:, cols], cs64_ref.at[:, cols], prow_ref, pcol_ref, tril_ref.at[0:th, 0:th],
            qkvn_ref.at[rows, :], z_ref.at[rows, :], vnat_ref.at[pl.ds(n * th * N_KV, th * N_KV), :],
            knat_ref.at[pl.ds(n * th * N_KV, th * N_KV), :], ikT_ref.at[:, cols], tail_ref, misc_ref.at[rows, :],
            gct_ref.at[pl.ds(n * (th // GDN_CHUNK), th // GDN_CHUNK)],
            qT_ref.at[:, cols], iqT_ref.at[:, cols], iwT_ref.at[:, cols], kbf_ref.at[rows, :],
            vT_ref.at[pl.ds(n * (th // ATT_TK), th // ATT_TK)], ikbf_ref.at[rows, :],
            xs_ref.at[pl.ds(n * th, th + 8), :], rt_ref.at[:, cols])


def _inproj_prompt_rows(tm, x_ref, sc_ref, sh_ref, g1_ref, wn_ref, wt_ref, wconv_ref, cs128_ref, cs64_ref,
                        prow_ref, pcol_ref, tril_ref,
                        qkvn_ref, z_ref, vnat_ref, knat_ref, ikT_ref, tail_ref, misc_ref, gct_ref,
                        qT_ref, iqT_ref, iwT_ref, kbf_ref, vT_ref, ikbf_ref,
                        xs_ref, rt_ref):
    nat, mt = _project(x_ref, sc_ref, sh_ref, g1_ref, wn_ref, wt_ref, cs128_ref, cs64_ref, pcol_ref, rt_ref)
    xs_ref[8:tm + 8, :] = nat[:, 0:QKV_W]
    conv = wconv_ref[0:1, :] * xs_ref[5:tm + 5, :]
    for t in range(1, CONV_W):
        conv = conv + wconv_ref[t:t + 1, :] * xs_ref[5 + t:tm + 5 + t, :]
    tail_ref[...] = xs_ref[tm:tm + 8, :]
    _qkv_post(conv, qkvn_ref)
    z_ref[...] = nat[:, QKV_W:QKV_W + GDN_W]
    v0 = QKV_W + GDN_W
    for hd in range(N_KV):
        vnat_ref[pl.ds(hd, tm, stride=N_KV), :] = nat[:, v0 + hd * HEAD_DIM:v0 + (hd + 1) * HEAD_DIM]

    gm = _misc_natural(nat[:, NAT_W - LANES:], prow_ref)
    tril = tril_ref[...]
    hi, mid, lo = _split3(gm)
    gc = _dot(tril, hi) + (_dot(tril, mid) + _dot(tril, lo))
    lane = lax.broadcasted_iota(I32, gm.shape, 1)
    misc_ref[...] = jnp.where(lane < 2 * GDN_HEADS, gm, gc)
    hi, mid, lo = _split3(mt)
    gct = _dot_nt(hi, tril) + (_dot_nt(mid, tril) + _dot_nt(lo, tril))
    r = lax.broadcasted_iota(I32, mt.shape, 0)
    bg = jnp.where(r < IDX_HEADS + GDN_HEADS, mt, gct)[IDX_HEADS:MISC_ROWS, :]
    for cc in range(tm // GDN_CHUNK):
        gct_ref[cc] = bg[:, cc * GDN_CHUNK:(cc + 1) * GDN_CHUNK]

    iwT_ref[...] = mt[0:IDX_HEADS, :]
    qT_ref[...] = rt_ref[T_AQ:T_AQ + ATT_W, :].astype(BF)
    iqT_ref[...] = rt_ref[T_IQ:T_IQ + IDX_W, :].astype(BF)
    kn = rt_ref[T_AK:T_AK + KV_W, :].T
    for hd in range(N_KV):
        knat_ref[pl.ds(hd, tm, stride=N_KV), :] = kn[:, hd * HEAD_DIM:(hd + 1) * HEAD_DIM]
    kbf_ref[...] = kn.astype(BF)
    ikT_ref[...] = rt_ref[T_IK:T_IK + IDX_DIM, :]
    ikbf_ref[...] = rt_ref[T_IK:T_IK + LANES, :].T[:, 0:IDX_DIM].astype(BF)
    for cc in range(tm // ATT_TK):
        for hd in range(N_KV):
            lo = hd * VT_ROWS
            vT_ref[cc, lo:lo + HEAD_DIM, :] = rt_ref[T_AV + hd * HEAD_DIM:T_AV + (hd + 1) * HEAD_DIM,
                                                     cc * ATT_TK:(cc + 1) * ATT_TK].astype(BF)
            vT_ref[cc, lo + HEAD_DIM:lo + VT_ROWS, :] = jnp.ones((VT_ROWS - HEAD_DIM, ATT_TK), BF)


def _inproj_sample_kernel(x_ref, sc_ref, sh_ref, g1_ref, wn_ref, wt_ref, wconv_ref, cs128_ref, cs64_ref,
                          prow_ref, pcol_ref, s0_ref, s1_ref, s2_ref,
                          qkvn_ref, z_ref, vnat_ref, knat_ref, slab_ref, raw_ref, misc_ref, aq_ref, iq_ref,
                          rt_ref):
    nat, _ = _project(x_ref, sc_ref, sh_ref, g1_ref, wn_ref, wt_ref, cs128_ref, cs64_ref, pcol_ref, rt_ref)
    raw = nat[:, 0:QKV_W]
    raw_ref[...] = raw
    conv = wconv_ref[0:1, :] * s0_ref[...]
    conv = conv + wconv_ref[1:2, :] * s1_ref[...]
    conv = conv + wconv_ref[2:3, :] * s2_ref[...]
    conv = conv + wconv_ref[3:4, :] * raw
    _qkv_post(conv, qkvn_ref)
    z_ref[...] = nat[:, QKV_W:QKV_W + GDN_W]
    vnat_ref[...] = nat[:, QKV_W + GDN_W:QKV_W + GDN_W + KV_W]
    misc_ref[...] = _misc_natural(nat[:, NAT_W - LANES:], prow_ref)
    aq_ref[...] = rt_ref[T_AQ:T_AQ + ATT_W, :].T
    iq_ref[...] = rt_ref[T_IQ:T_IQ + IDX_W, :].T
    knat_ref[...] = rt_ref[T_AK:T_AK + KV_W, :].T
    slab_ref[...] = rt_ref[T_IK:T_IK + LANES, :].T


def _inproj_weights(w_in, a_log, dt_bias):
    offs = np.cumsum([0, QKV_W, GDN_W, GDN_HEADS, GDN_HEADS, ATT_W, KV_W, KV_W, IDX_W, IDX_DIM, IDX_HEADS])
    qkv, z, beta, a, aq, ak, av, iq, ik, iw = [w_in[:, offs[n]:offs[n + 1]] for n in range(10)]
    d = w_in.shape[0]
    pad = jnp.zeros((d, LANES - 3 * GDN_HEADS), w_in.dtype)
    w_nat = jnp.concatenate([qkv, z, av, beta, a, a, pad], axis=1).astype(BF)
    w_t = jnp.concatenate([aq, ak, iq, ik, iw, beta, a, av], axis=1).T.astype(BF)
    zrow = jnp.zeros((LANES - 3 * GDN_HEADS,), F32)
    prow = jnp.stack([jnp.concatenate([jnp.zeros((GDN_HEADS,), F32), a_log, a_log, zrow]),
                      jnp.concatenate([jnp.zeros((GDN_HEADS,), F32), dt_bias, dt_bias, zrow])])
    z12 = jnp.zeros((IDX_HEADS + GDN_HEADS,), F32)
    pcol = jnp.stack([jnp.concatenate([z12, a_log]), jnp.concatenate([z12, dt_bias])], axis=1)
    return w_nat, w_t, prow, pcol


def _rope_tables(pos, dim):
    half = dim // 8
    inv = ROPE_THETA ** (-jnp.arange(half, dtype=F32) / half)
    ang = pos.astype(F32)[:, None] * inv[None, :]
    return jnp.concatenate([jnp.cos(ang).T, jnp.sin(ang).T], axis=0)


def _inproj_prompt(x, mod, g1, w_nat, w_t, w_conv, prow, pcol, tm):
    bp, tp, d = x.shape
    nt = tp // tm
    rows = bp * tp
    pos = jnp.arange(tp, dtype=I32)
    cs128, cs64 = _rope_tables(pos, HEAD_DIM), _rope_tables(pos, IDX_DIM)
    ri = np.arange(tm)
    tril = jnp.asarray((ri[:, None] // GDN_CHUNK == ri[None, :] // GDN_CHUNK) & (ri[None, :] <= ri[:, None]), BF)
    const = lambda shape: pl.BlockSpec(shape, lambda b, i: (0,) * len(shape))
    rowblk = lambda w: pl.BlockSpec((tm, w), lambda b, i: (b * nt + i, 0))
    colblk = lambda h: pl.BlockSpec((h, tm), lambda b, i: (0, b * nt + i))
    in_specs = [
        pl.BlockSpec((None, tm, d), lambda b, i: (b, i, 0)),
        pl.BlockSpec((None, 1, d), lambda b, i: (b, 0, 1)),
        pl.BlockSpec((None, 1, d), lambda b, i: (b, 0, 0)),
        const((1, d)), const((d, NAT_W)), const((T_ROWS, d)), const((CONV_W, QKV_W)),
        pl.BlockSpec((HEAD_DIM // 4, tm), lambda b, i: (0, i)),
        pl.BlockSpec((IDX_DIM // 4, tm), lambda b, i: (0, i)),
        const((2, LANES)), const((MISC_ROWS, 2)), const((tm, tm)),
    ]
    out_shape = [
        jax.ShapeDtypeStruct((rows, QKV_W), F32), jax.ShapeDtypeStruct((rows, GDN_W), F32),
        jax.ShapeDtypeStruct((rows * N_KV, HEAD_DIM), F32), jax.ShapeDtypeStruct((rows * N_KV, HEAD_DIM), F32),
        jax.ShapeDtypeStruct((bp, IDX_DIM, tp), F32), jax.ShapeDtypeStruct((bp, 8, QKV_W), F32),
        jax.ShapeDtypeStruct((rows, LANES), F32), jax.ShapeDtypeStruct((rows // GDN_CHUNK, 8, GDN_CHUNK), F32),
        jax.ShapeDtypeStruct((ATT_W, rows), BF), jax.ShapeDtypeStruct((IDX_W, rows), BF),
        jax.ShapeDtypeStruct((IDX_HEADS, rows), F32), jax.ShapeDtypeStruct((rows, KV_W), BF),
        jax.ShapeDtypeStruct((rows // ATT_TK, N_KV * VT_ROWS, ATT_TK), BF), jax.ShapeDtypeStruct((rows, IDX_DIM), BF),
    ]
    out_specs = [
        rowblk(QKV_W), rowblk(GDN_W),
        pl.BlockSpec((tm * N_KV, HEAD_DIM), lambda b, i: (b * nt + i, 0)),
        pl.BlockSpec((tm * N_KV, HEAD_DIM), lambda b, i: (b * nt + i, 0)),
        pl.BlockSpec((None, IDX_DIM, tm), lambda b, i: (b, 0, i)),
        pl.BlockSpec((None, 8, QKV_W), lambda b, i: (b, 0, 0)),
        rowblk(LANES),
        pl.BlockSpec((tm // GDN_CHUNK, 8, GDN_CHUNK), lambda b, i: (b * nt + i, 0, 0)),
        colblk(ATT_W), colblk(IDX_W), colblk(IDX_HEADS), rowblk(KV_W),
        pl.BlockSpec((tm // ATT_TK, N_KV * VT_ROWS, ATT_TK), lambda b, i: (b * nt + i, 0, 0)),
        rowblk(IDX_DIM),
    ]
    return pl.pallas_call(
        functools.partial(_inproj_prompt_kernel, tm, tm // _pick(tm, (ATT_TK,))),
        grid=(bp, nt), in_specs=in_specs, out_specs=out_specs, out_shape=out_shape,
        scratch_shapes=[pltpu.VMEM((tm + 8, QKV_W), F32), pltpu.VMEM((T_ROWS, tm), F32)],
        compiler_params=_params(("arbitrary", "arbitrary")),
        name="inproj_prompt",
    )(x, mod, mod, g1.reshape(1, d), w_nat, w_t, w_conv, cs128, cs64, prow, pcol, tril)


def _inproj_sample(x, mod, g1, w_nat, w_t, w_conv, prow, pcol, conv_state, past):
    db, d = x.shape
    pos = jnp.full((db,), past, I32)
    cs128, cs64 = _rope_tables(pos, HEAD_DIM), _rope_tables(pos, IDX_DIM)
    full = lambda shape: pl.BlockSpec(shape, lambda i: (0,) * len(shape))
    in_specs = [
        full((db, d)),
        pl.BlockSpec((db, d), lambda i: (0, 1)), pl.BlockSpec((db, d), lambda i: (0, 0)),
        full((1, d)), full((d, NAT_W)), full((T_ROWS, d)), full((CONV_W, QKV_W)),
        full((HEAD_DIM // 4, db)), full((IDX_DIM // 4, db)), full((2, LANES)), full((MISC_ROWS, 2)),
        full((db, QKV_W)), full((db, QKV_W)), full((db, QKV_W)),
    ]
    widths = [QKV_W, GDN_W, KV_W, KV_W, LANES, QKV_W, LANES, ATT_W, IDX_W]
    return pl.pallas_call(
        _inproj_sample_kernel,
        grid=(1,), in_specs=in_specs,
        out_specs=[full((db, w)) for w in widths],
        out_shape=[jax.ShapeDtypeStruct((db, w), F32) for w in widths],
        scratch_shapes=[pltpu.VMEM((T_ROWS, db), F32)],
        compiler_params=_params(("arbitrary",)),
        name="inproj_sample",
    )(x, mod, mod, g1.reshape(1, d), w_nat, w_t, w_conv, cs128, cs64, prow, pcol,
      conv_state[:, 0], conv_state[:, 1], conv_state[:, 2])


def _unit_lower_inverses(lows):
    c = lows[0].shape[0]
    ii = lax.broadcasted_iota(I32, (c, c), 0)
    jj = lax.broadcasted_iota(I32, (c, c), 1)
    eye = jnp.where(ii == jj, 1.0, 0.0)
    levels = int(math.log2(c)) - 1
    ts = [eye - low for low in lows]
    ps = [_mm_hi(low, low) for low in lows]
    for lvl in range(levels):
        if lvl == levels - 1:
            ts = [t + _mm_hi(t, p) for t, p in zip(ts, ps)]
        else:
            both = [_mm_hi(jnp.concatenate([t, p], axis=0), p) for t, p in zip(ts, ps)]
            ts = [t + b[0:c] for t, b in zip(ts, both)]
            ps = [b[c:] for b in both]
    return ts


def _gdn_prompt_kernel(tg, qkvn_ref, z_ref, misc_ref, gct_ref, gn_ref, o_ref, ssm_ref, s_ref):
    i = pl.program_id(1)

    @pl.when(i == 0)
    def _():
        s_ref[...] = jnp.zeros(s_ref.shape, F32)

    c = GDN_CHUNK
    ii = lax.broadcasted_iota(I32, (c, c), 0)
    jj = lax.broadcasted_iota(I32, (c, c), 1)
    pairs = [(cc, hd) for cc in range(tg // c) for hd in range(GDN_HEADS)]

    qs, ks, gcs, rhs, lows, intras = {}, {}, {}, {}, [], {}
    for cc, hd in pairs:
        r0, lo = cc * c, hd * HEAD_DIM
        q = qkvn_ref[r0:r0 + c, lo:lo + HEAD_DIM]
        k = qkvn_ref[r0:r0 + c, GDN_W + lo:GDN_W + lo + HEAD_DIM]
        v = qkvn_ref[r0:r0 + c, 2 * GDN_W + lo:2 * GDN_W + lo + HEAD_DIM]
        beta_c = misc_ref[r0:r0 + c, hd:hd + 1]
        gc_c = misc_ref[r0:r0 + c, 2 * GDN_HEADS + hd:2 * GDN_HEADS + hd + 1]
        gc_r = gct_ref[cc, GDN_HEADS + hd:GDN_HEADS + hd + 1, :]
        decay = jnp.where(ii >= jj, jnp.exp(jnp.where(ii >= jj, gc_c - gc_r, 0.0)), 0.0)
        kb = k * beta_c
        both = _dot_nt(jnp.concatenate([kb, q], axis=0).astype(BF), k.astype(BF))
        lows.append(jnp.where(ii > jj, both[0:c] * decay, 0.0))
        intras[cc, hd] = (both[c:] * decay).astype(BF)
        rhs[cc, hd] = jnp.concatenate([v * beta_c, kb * jnp.exp(gc_c)], axis=1).astype(BF)
        qs[cc, hd], ks[cc, hd], gcs[cc, hd] = q, k, gc_c
    ts = _unit_lower_inverses(lows)
    uws = {p: _dot(t.astype(BF), rhs[p]) for p, t in zip(pairs, ts)}

    for cc, hd in pairs:
        r0, lo = cc * c, hd * HEAD_DIM
        q, k, gc_c, uw = qs[cc, hd], ks[cc, hd], gcs[cc, hd], uws[cc, hd]
        s = s_ref[hd]
        ws = _dot(jnp.concatenate([uw[:, HEAD_DIM:], q * jnp.exp(gc_c)], axis=0).astype(BF), s.astype(BF))
        v_new = (uw[:, 0:HEAD_DIM] - ws[0:c]).astype(BF)
        o = ws[c:] + _dot(intras[cc, hd], v_new)
        g_last = gc_c[c - 1:c, :]
        kd = k * jnp.exp(g_last - gc_c)
        s_ref[hd] = s * jnp.exp(g_last) + _dot(kd.T.astype(BF), v_new)
        o = _rmsnorm(o, gn_ref[...]) * _silu(z_ref[r0:r0 + c, lo:lo + HEAD_DIM])
        o_ref[r0:r0 + c, lo:lo + HEAD_DIM] = o.astype(BF)
    ssm_ref[...] = s_ref[...]


def _gdn_prompt(qkvn, z, misc, gct, g_norm, bp, tp, tg):
    nt = tp // tg
    rows = bp * tp
    rowblk = lambda w: pl.BlockSpec((tg, w), lambda b, i: (b * nt + i, 0))
    return pl.pallas_call(
        functools.partial(_gdn_prompt_kernel, tg),
        grid=(bp, nt),
        in_specs=[rowblk(QKV_W), rowblk(GDN_W), rowblk(LANES),
                  pl.BlockSpec((tg // GDN_CHUNK, 8, GDN_CHUNK), lambda b, i: (b * nt + i, 0, 0)),
                  pl.BlockSpec((1, HEAD_DIM), lambda b, i: (0, 0))],
        out_specs=[rowblk(GDN_W),
                   pl.BlockSpec((None, GDN_HEADS, HEAD_DIM, HEAD_DIM), lambda b, i: (b, 0, 0, 0))],
        out_shape=[jax.ShapeDtypeStruct((rows, GDN_W), BF),
                   jax.ShapeDtypeStruct((bp, GDN_HEADS, HEAD_DIM, HEAD_DIM), F32)],
        scratch_shapes=[pltpu.VMEM((GDN_HEADS, HEAD_DIM, HEAD_DIM), F32)],
        compiler_params=_params(("arbitrary", "arbitrary")),
        name="gdn_prompt",
    )(qkvn, z, misc, gct, g_norm.reshape(1, HEAD_DIM))


def _gdn_sample_kernel(nb, qkvn_ref, z_ref, misc_ref, gn_ref, s_ref, o_ref, ssm_ref):
    for bi in range(nb):
        for hd in range(GDN_HEADS):
            lo = hd * HEAD_DIM
            q = qkvn_ref[bi:bi + 1, lo:lo + HEAD_DIM]
            k = qkvn_ref[bi:bi + 1, GDN_W + lo:GDN_W + lo + HEAD_DIM]
            v = qkvn_ref[bi:bi + 1, 2 * GDN_W + lo:2 * GDN_W + lo + HEAD_DIM]
            beta = misc_ref[bi:bi + 1, hd:hd + 1]
            g = misc_ref[bi:bi + 1, GDN_HEADS + hd:GDN_HEADS + hd + 1]
            kcol = jnp.broadcast_to(k, (HEAD_DIM, HEAD_DIM)).T
            qcol = jnp.broadcast_to(q, (HEAD_DIM, HEAD_DIM)).T
            s = s_ref[bi, hd] * jnp.exp(g)
            ks = jnp.sum(kcol * s, axis=0, keepdims=True)
            delta = (v - ks) * beta
            s = s + kcol * delta
            ssm_ref[bi, hd] = s
            o = jnp.sum(qcol * s, axis=0, keepdims=True)
            o = _rmsnorm(o, gn_ref[...]) * _silu(z_ref[bi:bi + 1, lo:lo + HEAD_DIM])
            o_ref[bi:bi + 1, lo:lo + HEAD_DIM] = o.astype(BF)


def _gdn_sample(qkvn, z, misc, g_norm, state, nb):
    db = qkvn.shape[0]
    rowblk = lambda w: pl.BlockSpec((nb, w), lambda i: (i, 0))
    sblk = pl.BlockSpec((nb, GDN_HEADS, HEAD_DIM, HEAD_DIM), lambda i: (i, 0, 0, 0))
    return pl.pallas_call(
        functools.partial(_gdn_sample_kernel, nb),
        grid=(db // nb,),
        in_specs=[rowblk(QKV_W), rowblk(GDN_W), rowblk(LANES), pl.BlockSpec((1, HEAD_DIM), lambda i: (0, 0)), sblk],
        out_specs=[rowblk(GDN_W), sblk],
        out_shape=[jax.ShapeDtypeStruct((db, GDN_W), BF), jax.ShapeDtypeStruct(state.shape, F32)],
        compiler_params=_params(("arbitrary",)),
        name="gdn_sample",
    )(qkvn, z, misc, g_norm.reshape(1, HEAD_DIM), state)


def _ordered_word_to_float(u):
    s = u ^ INT_MIN
    return lax.bitcast_convert_type(s ^ ((s >> 31) & np.int32(0x7FFFFFFF)), F32)


def _select_topk_bias(i_ref, nch, tk, k_sel, taken=0.0, dropped=NEG_INF, group=1):
    nl = i_ref.shape[1]

    def chunk(c):
        return pl.ds(pl.multiple_of(c * tk, tk), tk)

    def count(ref, pred):
        def body(c, acc):
            m = pred(ref[chunk(c), :])
            ones = jnp.where(m, 1, 0).astype(I32).reshape(tk // 8, 8, nl)
            parts = [ones[n] for n in range(tk // 8)]
            while len(parts) > 1:
                parts = [a + b for a, b in zip(parts[0::2], parts[1::2])] + parts[len(parts) & ~1:]
            return acc + parts[0]
        acc = lax.fori_loop(0, nch, body, jnp.zeros((8, nl), I32))
        return jnp.sum(acc, axis=0, keepdims=True)

    def bit_body(i, carry):
        t_u, n_ge = carry
        cand_u = t_u | jnp.left_shift(jnp.int32(1), 31 - i)
        cand = _ordered_word_to_float(cand_u)
        cnt = count(i_ref, lambda sc: sc >= cand)
        take = cnt >= k_sel
        return jnp.where(take, cand_u, t_u), jnp.where(take, cnt, n_ge)

    t_u, n_ge = lax.fori_loop(0, 32, bit_body, (jnp.zeros((1, nl), I32), jnp.full((1, nl), -1, I32)))
    thr = jnp.where((t_u >> 23) == 0, NEG_INF, _ordered_word_to_float(t_u))

    excess = (n_ge - k_sel).astype(F32)
    ri = lax.broadcasted_iota(I32, (tk + 8, tk), 0)
    ci = lax.broadcasted_iota(I32, (tk + 8, tk), 1)
    above = jnp.where((ci > ri) | (ri >= tk), 1.0, 0.0).astype(BF)

    nsteps = (nch + group - 1) // group

    def write(n, later):
        scs = [i_ref[chunk((nsteps - 1 - n) * group + g), :] for g in reversed(range(group))]
        ties = [sc == thr for sc in scs]
        cnts = [_dot(above, jnp.where(t, 1.0, 0.0).astype(BF)) for t in ties]
        for g, sc, tie, cnt in zip(reversed(range(group)), scs, ties, cnts):
            sel = (sc > thr) | (tie & (cnt[0:tk] + later >= excess))
            i_ref[chunk((nsteps - 1 - n) * group + g), :] = jnp.where(sel & (sc > NEG_INF), taken, dropped)
            later = later + cnt[tk:tk + 1]
        return later

    lax.fori_loop(0, nsteps, write, jnp.zeros((1, nl), F32))


def _attn_prompt_kernel(tq, tk, blk, k_sel, qT_ref, iqT_ref, iwT_ref, kbf_ref, vT_ref, ikbf_ref, o_ref, i_ref):
    t0 = pl.program_id(1) * tq
    nch = (t0 + tq + tk - 1) // tk
    nblk = (t0 + tq + blk - 1) // blk
    qpos = t0 + lax.broadcasted_iota(I32, (1, tq), 1)
    row = lax.broadcasted_iota(I32, (tk, 1), 0)
    w = iwT_ref[...]
    pairs = IDX_HEADS // 2
    rhs = [jnp.concatenate([iqT_ref[(2 * p) * IDX_DIM:(2 * p + 1) * IDX_DIM, :],
                            iqT_ref[(2 * p + 1) * IDX_DIM:(2 * p + 2) * IDX_DIM, :]], axis=1) for p in range(pairs)]

    def chunk(c):
        return pl.ds(pl.multiple_of(c * tk, tk), tk)

    def idx_body(c, carry):
        ikc = ikbf_ref[chunk(c), :]
        acc = jnp.zeros((tk, tq), F32)
        for p in range(pairs):
            d = jnp.maximum(_dot(ikc, rhs[p]), 0.0)
            acc = acc + d[:, 0:tq] * w[2 * p:2 * p + 1, :] + d[:, tq:] * w[2 * p + 1:2 * p + 2, :]
        i_ref[chunk(c), :] = jnp.where(c * tk + row <= qpos, acc, NEG_INF)
        return carry

    lax.fori_loop(0, nch, idx_body, 0)

    def fill_body(c, carry):
        i_ref[chunk(c), :] = jnp.full((tk, tq), NEG_INF, F32)
        return carry

    lax.fori_loop(nch, nblk * (blk // tk), fill_body, 0)
    tks = max(tk * LANES // tq, 8)
    _select_topk_bias(i_ref, (t0 + tq + tks - 1) // tks, tks, k_sel, group=min(4, blk // tks))

    scale2 = HEAD_DIM ** -0.5 * math.log2(math.e)
    rep = ATT_HEADS // N_KV
    sub = ATT_TK
    nsub = blk // sub
    qg = [jnp.concatenate([qT_ref[(rep * g + r) * HEAD_DIM:(rep * g + r + 1) * HEAD_DIM, :] for r in range(rep)],
                          axis=1) for g in range(N_KV)]

    def body(c, carry):
        out = []
        for g in range(N_KV):
            m, acc = carry[g]
            ss = []
            for i in range(nsub):
                rows = pl.ds(pl.multiple_of(c * blk + i * sub, sub), sub)
                bias = i_ref[rows, :]
                ss.append(_dot(kbf_ref[rows, g * HEAD_DIM:(g + 1) * HEAD_DIM], qg[g]) * scale2
                          + jnp.concatenate([bias] * rep, axis=1))
            m_new = m
            for s in ss:
                m_new = jnp.maximum(m_new, jnp.max(s, axis=0, keepdims=True))
            m_safe = jnp.where(m_new == NEG_INF, 0.0, m_new)
            acc = acc * jnp.exp2(m - m_safe)
            for i, s in enumerate(ss):
                p = jnp.exp2(s - m_safe).astype(BF)
                acc = acc + _dot(vT_ref[c * nsub + i, g * VT_ROWS:(g + 1) * VT_ROWS, :], p)
            out.append((m_new, acc))
        return tuple(out)

    init = tuple((jnp.full((1, rep * tq), NEG_INF, F32), jnp.zeros((VT_ROWS, rep * tq), F32)) for _ in range(N_KV))
    res = lax.fori_loop(0, nblk, body, init)
    for g in range(N_KV):
        acc = res[g][1]
        o_t = acc[0:HEAD_DIM] / acc[HEAD_DIM:HEAD_DIM + 1]
        for r in range(rep):
            hd = rep * g + r
            o_ref[:, hd * HEAD_DIM:(hd + 1) * HEAD_DIM] = o_t[:, r * tq:(r + 1) * tq].T.astype(BF)


def _attn_prompt(qT, iqT, iwT, kbf, vT, ikbf, bp, tp):
    tq = _pick(tp, (ATT_TQ, QBLOCK))
    tk = _pick(tp, (IDX_TK, ATT_TK))
    blk = _pick(tp, (ATT_BLK, IDX_TK, ATT_TK))
    assert blk % tk == 0 and blk % ATT_TK == 0
    nq = tp // tq
    rows = bp * tp
    k_sel = min(TOPK_MAX, tp // 4)
    colblk = lambda h: pl.BlockSpec((h, tq), lambda b, j: (0, b * nq + j))
    return pl.pallas_call(
        functools.partial(_attn_prompt_kernel, tq, tk, blk, k_sel),
        grid=(bp, nq),
        in_specs=[colblk(ATT_W), colblk(IDX_W), colblk(IDX_HEADS),
                  pl.BlockSpec((tp, KV_W), lambda b, j: (b, 0)),
                  pl.BlockSpec((tp // ATT_TK, N_KV * VT_ROWS, ATT_TK), lambda b, j: (b, 0, 0)),
                  pl.BlockSpec((tp, IDX_DIM), lambda b, j: (b, 0))],
        out_specs=pl.BlockSpec((tq, ATT_W), lambda b, j: (b * nq + j, 0)),
        out_shape=jax.ShapeDtypeStruct((rows, ATT_W), BF),
        scratch_shapes=[pltpu.VMEM((tp, tq), F32)],
        compiler_params=_params(("arbitrary", "arbitrary")),
        name="attn_prompt",
    )(qT, iqT, iwT, kbf, vT, ikbf)


def _page_prefetch(npg, pt_ref, srcs, bufs, sem_ref):
    b = pl.program_id(0)
    steps = pl.num_programs(0)
    ring = bufs[0].shape[0]
    ahead = ring - 1
    slot = lax.rem(b, ring)

    def copy(n, step, sl, p):
        return pltpu.make_async_copy(srcs[n].at[pt_ref[step, p]], bufs[n].at[sl, p], sem_ref.at[n, sl, p])

    def start_all(step, sl):
        for n in range(len(srcs)):
            for p in range(npg):
                copy(n, step, sl, p).start()

    @pl.when(b == 0)
    def _():
        for step in range(ahead):
            start_all(step, step)

    @pl.when(b + ahead < steps)
    def _():
        start_all(b + ahead, lax.rem(b + ahead, ring))

    for n in range(len(srcs)):
        for p in range(npg):
            copy(n, b, slot, p).wait()
    return slot


def _sidx_kernel(npg, pt_ref, iq_ref, iw_ref, ikn_ref, cache_ref, out_ref, buf_ref, sem_ref):
    slot = _page_prefetch(npg, pt_ref, [cache_ref], [buf_ref], sem_ref)
    iq = iq_ref[...].astype(BF)
    w = iw_ref[...]
    for p in range(npg):
        d = jnp.maximum(_dot(iq, buf_ref[slot, p].astype(BF)), 0.0)
        out_ref[:, p * LANES:(p + 1) * LANES] = jnp.sum(d * w, axis=0, keepdims=True)
    dn = jnp.sum(iq.astype(F32) * ikn_ref[...].astype(BF).astype(F32), axis=1, keepdims=True)
    sn = jnp.sum(jnp.maximum(dn, 0.0) * w, axis=0, keepdims=True)
    lane = lax.broadcasted_iota(I32, (1, LANES), 1)
    out_ref[:, npg * LANES:(npg + 1) * LANES] = jnp.where(lane == 0, sn, NEG_INF)


def _sample_index_scores(iq, iw, ik_new, cache_ik_t, page_table):
    db, npg = page_table.shape
    page = cache_ik_t.shape[2]
    assert page == LANES and db > IDX_RING
    width = (npg + 1) * LANES
    grid_spec = pltpu.PrefetchScalarGridSpec(
        num_scalar_prefetch=1, grid=(db,),
        in_specs=[pl.BlockSpec((None, IDX_HEADS, IDX_DIM), lambda b, pt: (b, 0, 0)),
                  pl.BlockSpec((None, IDX_HEADS, 1), lambda b, pt: (b, 0, 0)),
                  pl.BlockSpec((None, 1, IDX_DIM), lambda b, pt: (b, 0, 0)),
                  pl.BlockSpec(memory_space=pl.ANY)],
        out_specs=pl.BlockSpec((None, 1, width), lambda b, pt: (b, 0, 0)),
        scratch_shapes=[pltpu.VMEM((IDX_RING, npg, IDX_DIM, page), F32),
                        pltpu.SemaphoreType.DMA((1, IDX_RING, npg))])
    out = pl.pallas_call(
        functools.partial(_sidx_kernel, npg), grid_spec=grid_spec,
        out_shape=jax.ShapeDtypeStruct((db, 1, width), F32),
        compiler_params=_params(("arbitrary",)),
        name="sample_index_scores",
    )(page_table, iq.reshape(db, IDX_HEADS, IDX_DIM), iw.reshape(db, IDX_HEADS, 1), ik_new.reshape(db, 1, IDX_DIM),
      cache_ik_t)
    return out.reshape(db, width)


def _ssel_kernel(nch, k_sel, s_ref, spread_ref, keep_ref, i_ref):
    for p in range(nch):
        i_ref[p * LANES:(p + 1) * LANES, :] = s_ref[:, p * LANES:(p + 1) * LANES].T
    _select_topk_bias(i_ref, nch, LANES, k_sel, taken=1.0, dropped=0.0)
    rows = N_KV * LANES
    for p in range(nch - 1):
        k2 = _dot(spread_ref[...], i_ref[p * LANES:(p + 1) * LANES, :].astype(BF))
        for h in range(N_KV):
            keep_ref[:, p * rows + h * LANES:p * rows + (h + 1) * LANES] = k2[h * LANES:(h + 1) * LANES, :].T
    keep_ref[:, (nch - 1) * rows:(nch - 1) * rows + LANES] = i_ref[(nch - 1) * LANES:nch * LANES, :].T


def _sample_select(scores, n_keys):
    db, width = scores.shape
    assert db == LANES
    nch = width // LANES
    k_sel = min(TOPK_MAX, n_keys // 4)
    rows = N_KV * LANES
    out_w = (nch - 1) * rows + LANES
    spread = jnp.asarray(np.arange(rows)[:, None] // N_KV == np.arange(LANES)[None, :], BF)
    return pl.pallas_call(
        functools.partial(_ssel_kernel, nch, k_sel),
        grid=(1,),
        in_specs=[pl.BlockSpec((db, width), lambda i: (0, 0)), pl.BlockSpec((rows, LANES), lambda i: (0, 0))],
        out_specs=pl.BlockSpec((db, out_w), lambda i: (0, 0)),
        out_shape=jax.ShapeDtypeStruct((db, out_w), F32),
        scratch_shapes=[pltpu.VMEM((width, db), F32)],
        compiler_params=_params(("arbitrary",)),
        name="sample_select",
    )(scores, spread)


def _sattn_kernel(npg, pt_ref, q_ref, keep_ref, kn_ref, vn_ref, ck_ref, cv_ref, o_ref, kbuf_ref, vbuf_ref, sem_ref):
    slot = _page_prefetch(npg, pt_ref, [ck_ref, cv_ref], [kbuf_ref, vbuf_ref], sem_ref)
    rep = ATT_HEADS // N_KV
    rows = N_KV * LANES
    scale = HEAD_DIM ** -0.5
    q = q_ref[...]
    q8 = jnp.concatenate([q, jnp.zeros((8 - ATT_HEADS, HEAD_DIM), F32)], axis=0).astype(BF)
    hrow = lax.broadcasted_iota(I32, (8, 1), 0)
    lane = lax.broadcasted_iota(I32, (1, LANES), 1)
    own_kv = lax.broadcasted_iota(I32, (1, rows), 1) % N_KV == hrow // rep
    parts = []
    for p in range(npg):
        s2 = _dot_nt(q8, kbuf_ref[slot, p].astype(BF))
        keep2 = keep_ref[:, p * rows:(p + 1) * rows]
        parts.append(jnp.where(own_kv & (keep2 > 0.5), s2 * scale, NEG_INF))
    kn = kn_ref[...].astype(BF).astype(F32)
    kn8 = jnp.where(hrow // rep == 0, kn[0:1, :], kn[1:2, :])
    s_new = jnp.sum(q8.astype(F32) * kn8, axis=1, keepdims=True)
    keep_new = keep_ref[:, npg * rows:npg * rows + 1]
    parts.append(jnp.where((lane == 0) & (keep_new > 0.5), s_new * scale, NEG_INF))
    s = jnp.concatenate(parts, axis=1)
    m = jnp.max(s, axis=1, keepdims=True)
    e = jnp.exp(s - m)
    pr = (e / jnp.sum(e, axis=1, keepdims=True)).astype(BF)
    o8 = jnp.zeros((8, HEAD_DIM), F32)
    for p in range(npg):
        o8 = o8 + _dot(pr[:, p * rows:(p + 1) * rows], vbuf_ref[slot, p].astype(BF))
    vn = vn_ref[...].astype(BF).astype(F32)
    vn8 = jnp.where(hrow // rep == 0, vn[0:1, :], vn[1:2, :])
    o8 = o8 + pr[:, npg * rows:npg * rows + 1].astype(F32) * vn8
    for hd in range(ATT_HEADS):
        o_ref[:, hd * HEAD_DIM:(hd + 1) * HEAD_DIM] = o8[hd:hd + 1, :].astype(BF)


def _sample_attention(q, keep, k_new, v_new, cache_k, cache_v, page_table):
    db, npg = page_table.shape
    n_phys, page = cache_k.shape[0], cache_k.shape[1]
    assert page == LANES and N_KV == 2 and db > KV_RING
    rows = page * N_KV
    width = npg * rows + LANES
    ck = cache_k.reshape(n_phys, rows, HEAD_DIM)
    cv = cache_v.reshape(n_phys, rows, HEAD_DIM)
    grid_spec = pltpu.PrefetchScalarGridSpec(
        num_scalar_prefetch=1, grid=(db,),
        in_specs=[pl.BlockSpec((None, ATT_HEADS, HEAD_DIM), lambda b, pt: (b, 0, 0)),
                  pl.BlockSpec((None, 1, width), lambda b, pt: (b, 0, 0)),
                  pl.BlockSpec((None, N_KV, HEAD_DIM), lambda b, pt: (b, 0, 0)),
                  pl.BlockSpec((None, N_KV, HEAD_DIM), lambda b, pt: (b, 0, 0)),
                  pl.BlockSpec(memory_space=pl.ANY), pl.BlockSpec(memory_space=pl.ANY)],
        out_specs=pl.BlockSpec((None, 1, ATT_W), lambda b, pt: (b, 0, 0)),
        scratch_shapes=[pltpu.VMEM((KV_RING, npg, rows, HEAD_DIM), F32), pltpu.VMEM((KV_RING, npg, rows, HEAD_DIM), F32),
                        pltpu.SemaphoreType.DMA((2, KV_RING, npg))])
    out = pl.pallas_call(
        functools.partial(_sattn_kernel, npg), grid_spec=grid_spec,
        out_shape=jax.ShapeDtypeStruct((db, 1, ATT_W), BF),
        compiler_params=_params(("arbitrary",)),
        name="sample_attention",
    )(page_table, q.reshape(db, ATT_HEADS, HEAD_DIM), keep.reshape(db, 1, width),
      k_new.reshape(db, N_KV, HEAD_DIM), v_new.reshape(db, N_KV, HEAD_DIM), ck, cv)
    return out.reshape(db, ATT_W)


def _post_kernel(final, nff, og_ref, oa_ref, x_ref, ga1_ref, sc2_ref, sh2_ref, ga2_ref, g2_ref, gf_ref,
                 wo_ref, wg_ref, wu_ref, wd_ref, y_ref, x1_ref, h2_ref, acc_ref):
    jf = pl.program_id(1)

    @pl.when(jf == 0)
    def _():
        mixed = _dot(og_ref[...], wo_ref[0:GDN_W, :]) + _dot(oa_ref[...], wo_ref[GDN_W:, :])
        x1 = x_ref[...] + ga1_ref[...] * mixed
        x1_ref[...] = x1
        h2_ref[...] = (_rmsnorm(x1, g2_ref[...]) * (1.0 + sc2_ref[...]) + sh2_ref[...]).astype(BF)
        acc_ref[...] = jnp.zeros(acc_ref.shape, F32)

    h2 = h2_ref[...]
    act = (_silu(_dot(h2, wg_ref[...])) * _dot(h2, wu_ref[...])).astype(BF)
    acc_ref[...] += _dot(act, wd_ref[...])

    @pl.when(jf == nff - 1)
    def _():
        x2 = x1_ref[...] + ga2_ref[...] * acc_ref[...]
        y_ref[...] = _rmsnorm(x2, gf_ref[...]) if final else x2


def _post(og, oa, x2d, mod, mod_spec, g2, gf, wo, wfi, wfo, tm, tf, final):
    rows, d = x2d.shape
    dff = wfo.shape[0]
    nff = dff // tf
    rowblk = lambda w: pl.BlockSpec((tm, w), lambda i, j: (i, 0))
    const = lambda shape: pl.BlockSpec(shape, lambda i, j: (0,) * len(shape))
    in_specs = [rowblk(GDN_W), rowblk(ATT_W), rowblk(d),
                mod_spec(2), mod_spec(4), mod_spec(3), mod_spec(5),
                const((1, d)), const((1, d)), const((d, d)),
                pl.BlockSpec((d, tf), lambda i, j: (0, j)),
                pl.BlockSpec((d, tf), lambda i, j: (0, nff + j)),
                pl.BlockSpec((tf, d), lambda i, j: (j, 0))]
    return pl.pallas_call(
        functools.partial(_post_kernel, final, nff),
        grid=(rows // tm, nff), in_specs=in_specs, out_specs=rowblk(d),
        out_shape=jax.ShapeDtypeStruct((rows, d), F32),
        scratch_shapes=[pltpu.VMEM((tm, d), F32), pltpu.VMEM((tm, d), BF), pltpu.VMEM((tm, d), F32)],
        compiler_params=_params(("arbitrary", "arbitrary")),
        name="post",
    )(og, oa, x2d, mod, mod, mod, mod, g2.reshape(1, d), gf.reshape(1, d), wo, wfi, wfi, wfo)


def _pick(n, prefs):
    for p in prefs:
        if n % p == 0:
            return p
    return n


def kernel(x_prompt, x_sample, c_prompt, c_sample, cache_k, cache_v, cache_idx_k, page_table, state_conv, state_ssm,
           w_ada, b_ada, g_norm1, w_in, w_conv, a_log, dt_bias, g_gdn_norm, w_out, g_norm2, w_ffn_in, w_ffn_out,
           g_final):
    bp, tp, d = x_prompt.shape
    db, ts, _ = x_sample.shape
    assert ts == 1 and d == GDN_W + ATT_W and tp % max(ATT_TK, GDN_CHUNK) == 0
    depth = w_in.shape[0]
    npg, page = page_table.shape[1], cache_k.shape[2]
    past = npg * page
    tm_in = _pick(tp, (512, 256))
    tg = _pick(tp, (512, 256))
    tm_post = _pick(tp, (1024, 512, 256))
    dff = w_ffn_out.shape[1]
    tf = _pick(dff, (256, 128))

    xp = x_prompt
    xs = x_sample.reshape(db, d)
    npad = (-(bp + db)) % 8
    c_all = jnp.concatenate([c_prompt, c_sample, jnp.zeros((npad, d), F32)], axis=0)
    new_p, new_s = [], []
    for l in range(depth):
        final = l == depth - 1
        mod = _modulation(c_all, w_ada[l], b_ada[l])
        mod_p = mod[:bp].reshape(bp, 1, 6 * d)
        mod_s = mod[bp:bp + db]
        w_nat, w_t, prow, pcol = _inproj_weights(w_in[l], a_log[l], dt_bias[l])
        wo = w_out[l].astype(BF)
        wfi = w_ffn_in[l].astype(BF)
        wfo = w_ffn_out[l].astype(BF)

        (qkvn, z, vnat, knat, ikT, tail, misc, gct, qT, iqT, iwT, kbf, vT, ikbf) = _inproj_prompt(
            xp, mod_p, g_norm1[l], w_nat, w_t, w_conv[l], prow, pcol, tm_in)
        og, ssm_p = _gdn_prompt(qkvn, z, misc, gct, g_gdn_norm[l], bp, tp, tg)
        oa = _attn_prompt(qT, iqT, iwT, kbf, vT, ikbf, bp, tp)
        tiles_b = tp // tm_post
        spec_p = lambda k: pl.BlockSpec((None, 1, d), lambda i, j: (i // tiles_b, 0, k))
        xp = _post(og, oa, xp.reshape(bp * tp, d), mod_p, spec_p, g_norm2[l], g_final, wo, wfi, wfo,
                   tm_post, tf, final).reshape(bp, tp, d)
        new_p.append((knat.reshape(bp, tp, N_KV, HEAD_DIM), vnat.reshape(bp, tp, N_KV, HEAD_DIM),
                      jnp.swapaxes(ikT, 1, 2), tail[:, 8 - (CONV_W - 1):, :], ssm_p))

        (qkvn_s, z_s, vnat_s, knat_s, slab_s, raw_s, misc_s, aq_s, iq_s) = _inproj_sample(
            xs, mod_s, g_norm1[l], w_nat, w_t, w_conv[l], prow, pcol, state_conv[l], past)
        og_s, ssm_s = _gdn_sample(qkvn_s, z_s, misc_s, g_gdn_norm[l], state_ssm[l], _pick(db, (8,)))
        ik_s = slab_s[:, 0:IDX_DIM]
        iw_s = slab_s[:, IDX_DIM:IDX_DIM + IDX_HEADS]
        scores = _sample_index_scores(iq_s, iw_s, ik_s, jnp.swapaxes(cache_idx_k[l], 1, 2), page_table)
        keep = _sample_select(scores, past + ts)
        oa_s = _sample_attention(aq_s, keep, knat_s, vnat_s, cache_k[l], cache_v[l], page_table)
        spec_s = lambda k: pl.BlockSpec((db, d), lambda i, j: (i, k))
        xs = _post(og_s, oa_s, xs, mod_s, spec_s, g_norm2[l], g_final, wo, wfi, wfo, db, tf, final)
        new_conv_s = jnp.concatenate([state_conv[l][:, 1:, :], raw_s[:, None, :]], axis=1)
        new_s.append((knat_s.reshape(db, ts, N_KV, HEAD_DIM), vnat_s.reshape(db, ts, N_KV, HEAD_DIM),
                      ik_s.reshape(db, ts, IDX_DIM), new_conv_s, ssm_s))

    stack = lambda states, n: jnp.stack([s[n] for s in states], axis=0)
    return (xp, xs.reshape(db, ts, d),
            stack(new_p, 0), stack(new_p, 1), stack(new_p, 2), stack(new_p, 3), stack(new_p, 4),
            stack(new_s, 0), stack(new_s, 1), stack(new_s, 2), stack(new_s, 3), stack(new_s, 4))
```

```python
import functools
import math

import numpy as np
import jax
import jax.numpy as jnp
from jax import lax
from jax.experimental import pallas as pl
from jax.experimental.pallas import tpu as pltpu

F32 = jnp.float32
BF = jnp.bfloat16
I32 = jnp.int32

HEAD_DIM = 128
GDN_HEADS = 4
ATT_HEADS = 4
N_KV = 2
IDX_HEADS = 8
IDX_DIM = 64
CONV_W = 4
TOPK_MAX = 256
QBLOCK = 128
ROPE_THETA = 500000.0
EPS = 1e-6
GDN_W = GDN_HEADS * HEAD_DIM
ATT_W = ATT_HEADS * HEAD_DIM
KV_W = N_KV * HEAD_DIM
IDX_W = IDX_HEADS * IDX_DIM
QKV_W = 3 * GDN_W

LANES = 128
BF16_ROWS = 16
GDN_CHUNK = 128
ATT_TK = 256
VT_ROWS = HEAD_DIM + BF16_ROWS
IDX_TK = 512
ATT_BLK = 1024
ATT_TQ = 256
IDX_RING = 5
KV_RING = 3
NAT_W = QKV_W + GDN_W + KV_W + LANES
MISC_ROWS = IDX_HEADS + 2 * GDN_HEADS
T_AQ = 0
T_AK = T_AQ + ATT_W
T_IQ = T_AK + KV_W
T_IK = T_IQ + IDX_W
T_MISC = T_IK + IDX_DIM
T_AV = T_MISC + MISC_ROWS
T_ROWS = T_AV + KV_W
VMEM_LIMIT = 56 * 1024 * 1024
INT_MIN = np.int32(-2 ** 31)
NEG_INF = float("-inf")


def _params(sem):
    return pltpu.CompilerParams(dimension_semantics=sem, vmem_limit_bytes=VMEM_LIMIT)


def _dot(a, b):
    return jnp.dot(a, b, preferred_element_type=F32)


def _dot_nt(a, b):
    return lax.dot_general(a, b, (((1,), (1,)), ((), ())), preferred_element_type=F32)


def _split3(x):
    hi = x.astype(BF)
    r1 = x - hi.astype(F32)
    mid = r1.astype(BF)
    lo = (r1 - mid.astype(F32)).astype(BF)
    return hi, mid, lo


def _mm_hi(a, b):
    ah = a.astype(BF)
    al = (a - ah.astype(F32)).astype(BF)
    bh = b.astype(BF)
    bl = (b - bh.astype(F32)).astype(BF)
    m = a.shape[0]
    hi = _dot(jnp.concatenate([ah, al], axis=0), bh)
    return hi[0:m] + (hi[m:] + _dot(ah, bl))


def _sigmoid(x):
    return 1.0 / (1.0 + jnp.exp(-x))


def _silu(x):
    return x * _sigmoid(x)


def _softplus(x):
    return jnp.maximum(x, 0.0) + jnp.log(1.0 + jnp.exp(-jnp.abs(x)))


def _rmsnorm(x, g):
    return x * lax.rsqrt(jnp.mean(x * x, axis=-1, keepdims=True) + EPS) * g


def _mod_kernel(c_ref, w_ref, b_ref, o_ref):
    s = _silu(c_ref[...]).astype(BF)
    o_ref[...] = _dot(s, w_ref[...].astype(BF)) + b_ref[...]


def _modulation(c_all, w_ada, b_ada):
    n, d = c_all.shape
    cols = w_ada.shape[1]
    tn = d
    return pl.pallas_call(
        _mod_kernel,
        grid=(cols // tn,),
        in_specs=[pl.BlockSpec((n, d), lambda j: (0, 0)),
                  pl.BlockSpec((d, tn), lambda j: (0, j)),
                  pl.BlockSpec((1, tn), lambda j: (0, j))],
        out_specs=pl.BlockSpec((n, tn), lambda j: (0, j)),
        out_shape=jax.ShapeDtypeStruct((n, cols), F32),
        compiler_params=_params(("arbitrary",)),
        name="modulation",
    )(c_all, w_ada, b_ada.reshape(1, cols))


def _rope_rows(rt_ref, base, half, cos, sin):
    x1 = rt_ref[base:base + half, :]
    x2 = rt_ref[base + half:base + 2 * half, :]
    rt_ref[base:base + half, :] = x1 * cos - x2 * sin
    rt_ref[base + half:base + 2 * half, :] = x2 * cos + x1 * sin


def _project(x_ref, sc_ref, sh_ref, g1_ref, wn_ref, wt_ref, cs128_ref, cs64_ref, pcol_ref, rt_ref):
    h = _rmsnorm(x_ref[...], g1_ref[...]) * (1.0 + sc_ref[...]) + sh_ref[...]
    hb = h.astype(BF)
    nat = _dot(hb, wn_ref[...])
    rt_ref[...] = _dot_nt(wt_ref[...], hb)
    half = HEAD_DIM // 8
    cos, sin = cs128_ref[0:half, :], cs128_ref[half:2 * half, :]
    for hd in range(ATT_HEADS):
        _rope_rows(rt_ref, T_AQ + hd * HEAD_DIM, half, cos, sin)
    for hd in range(N_KV):
        _rope_rows(rt_ref, T_AK + hd * HEAD_DIM, half, cos, sin)
    half = IDX_DIM // 8
    cos, sin = cs64_ref[0:half, :], cs64_ref[half:2 * half, :]
    for hd in range(IDX_HEADS):
        _rope_rows(rt_ref, T_IQ + hd * IDX_DIM, half, cos, sin)
    _rope_rows(rt_ref, T_IK, half, cos, sin)
    mt = rt_ref[T_MISC:T_MISC + MISC_ROWS, :]
    r = lax.broadcasted_iota(I32, mt.shape, 0)
    a_log, dt_b = pcol_ref[:, 0:1], pcol_ref[:, 1:2]
    gate = -jnp.exp(a_log) * _softplus(mt + dt_b)
    mt = jnp.where(r < IDX_HEADS, mt * IDX_HEADS ** -0.5, jnp.where(r < IDX_HEADS + GDN_HEADS, _sigmoid(mt), gate))
    rt_ref[T_MISC:T_MISC + MISC_ROWS, :] = mt
    return nat, mt


def _misc_natural(m, prow_ref):
    lane = lax.broadcasted_iota(I32, m.shape, 1)
    gate = -jnp.exp(prow_ref[0:1, :]) * _softplus(m + prow_ref[1:2, :])
    return jnp.where(lane < GDN_HEADS, _sigmoid(m), jnp.where(lane < 3 * GDN_HEADS, gate, 0.0))


def _qkv_post(conv, qkvn_ref):
    c = _silu(conv)
    for hb in range(2 * GDN_HEADS):
        xh = c[:, hb * HEAD_DIM:(hb + 1) * HEAD_DIM]
        n = xh * lax.rsqrt(jnp.sum(xh * xh, axis=-1, keepdims=True) + EPS)
        if hb < GDN_HEADS:
            n = n * HEAD_DIM ** -0.5
        qkvn_ref[:, hb * HEAD_DIM:(hb + 1) * HEAD_DIM] = n
    qkvn_ref[:, 2 * GDN_W:] = c[:, 2 * GDN_W:]


def _inproj_prompt_kernel(tm, parts, x_ref, sc_ref, sh_ref, g1_ref, wn_ref, wt_ref, wconv_ref, cs128_ref, cs64_ref,
                          prow_ref, pcol_ref, tril_ref,
                          qkvn_ref, z_ref, vnat_ref, knat_ref, ikT_ref, tail_ref, misc_ref, gct_ref,
                          qT_ref, iqT_ref, iwT_ref, kbf_ref, vT_ref, ikbf_ref,
                          xs_ref, rt_ref):
    i = pl.program_id(1)

    @pl.when(i == 0)
    def _():
        xs_ref[0:8, :] = jnp.zeros((8, QKV_W), F32)

    @pl.when(i > 0)
    def _():
        xs_ref[0:8, :] = xs_ref[tm:tm + 8, :]

    th = tm // parts
    for n in range(parts):
        rows, cols = pl.ds(n * th, th), pl.ds(n * th, th)
        _inproj_prompt_rows(
            th, x_ref.at[rows, :], sc_ref, sh_ref, g1_ref, wn_ref, wt_ref, wconv_ref,
            cs128_ref.at[:, cols], cs64_ref.at[:, cols], prow_ref, pcol_ref, tril_ref.at[0:th, 0:th],
            qkvn_ref.at[rows, :], z_ref.at[rows, :], vnat_ref.at[pl.ds(n * th * N_KV, th * N_KV), :],
            knat_ref.at[pl.ds(n * th * N_KV, th * N_KV), :], ikT_ref.at[:, cols], tail_ref, misc_ref.at[rows, :],
            gct_ref.at[pl.ds(n * (th // GDN_CHUNK), th // GDN_CHUNK)],
            qT_ref.at[:, cols], iqT_ref.at[:, cols], iwT_ref.at[:, cols], kbf_ref.at[rows, :],
            vT_ref.at[pl.ds(n * (th // ATT_TK), th // ATT_TK)], ikbf_ref.at[rows, :],
            xs_ref.at[pl.ds(n * th, th + 8), :], rt_ref.at[:, cols])


def _inproj_prompt_rows(tm, x_ref, sc_ref, sh_ref, g1_ref, wn_ref, wt_ref, wconv_ref, cs128_ref, cs64_ref,
                        prow_ref, pcol_ref, tril_ref,
                        qkvn_ref, z_ref, vnat_ref, knat_ref, ikT_ref, tail_ref, misc_ref, gct_ref,
                        qT_ref, iqT_ref, iwT_ref, kbf_ref, vT_ref, ikbf_ref,
                        xs_ref, rt_ref):
    nat, mt = _project(x_ref, sc_ref, sh_ref, g1_ref, wn_ref, wt_ref, cs128_ref, cs64_ref, pcol_ref, rt_ref)
    xs_ref[8:tm + 8, :] = nat[:, 0:QKV_W]
    conv = wconv_ref[0:1, :] * xs_ref[5:tm + 5, :]
    for t in range(1, CONV_W):
        conv = conv + wconv_ref[t:t + 1, :] * xs_ref[5 + t:tm + 5 + t, :]
    tail_ref[...] = xs_ref[tm:tm + 8, :]
    _qkv_post(conv, qkvn_ref)
    z_ref[...] = nat[:, QKV_W:QKV_W + GDN_W]
    v0 = QKV_W + GDN_W
    for hd in range(N_KV):
        vnat_ref[pl.ds(hd, tm, stride=N_KV), :] = nat[:, v0 + hd * HEAD_DIM:v0 + (hd + 1) * HEAD_DIM]

    gm = _misc_natural(nat[:, NAT_W - LANES:], prow_ref)
    tril = tril_ref[...]
    hi, mid, lo = _split3(gm)
    gc = _dot(tril, hi) + (_dot(tril, mid) + _dot(tril, lo))
    lane = lax.broadcasted_iota(I32, gm.shape, 1)
    misc_ref[...] = jnp.where(lane < 2 * GDN_HEADS, gm, gc)
    hi, mid, lo = _split3(mt)
    gct = _dot_nt(hi, tril) + (_dot_nt(mid, tril) + _dot_nt(lo, tril))
    r = lax.broadcasted_iota(I32, mt.shape, 0)
    bg = jnp.where(r < IDX_HEADS + GDN_HEADS, mt, gct)[IDX_HEADS:MISC_ROWS, :]
    for cc in range(tm // GDN_CHUNK):
        gct_ref[cc] = bg[:, cc * GDN_CHUNK:(cc + 1) * GDN_CHUNK]

    iwT_ref[...] = mt[0:IDX_HEADS, :]
    qT_ref[...] = rt_ref[T_AQ:T_AQ + ATT_W, :].astype(BF)
    iqT_ref[...] = rt_ref[T_IQ:T_IQ + IDX_W, :].astype(BF)
    kn = rt_ref[T_AK:T_AK + KV_W, :].T
    for hd in range(N_KV):
        knat_ref[pl.ds(hd, tm, stride=N_KV), :] = kn[:, hd * HEAD_DIM:(hd + 1) * HEAD_DIM]
    kbf_ref[...] = kn.astype(BF)
    ikT_ref[...] = rt_ref[T_IK:T_IK + IDX_DIM, :]
    ikbf_ref[...] = rt_ref[T_IK:T_IK + LANES, :].T[:, 0:IDX_DIM].astype(BF)
    for cc in range(tm // ATT_TK):
        for hd in range(N_KV):
            lo = hd * VT_ROWS
            vT_ref[cc, lo:lo + HEAD_DIM, :] = rt_ref[T_AV + hd * HEAD_DIM:T_AV + (hd + 1) * HEAD_DIM,
                                                     cc * ATT_TK:(cc + 1) * ATT_TK].astype(BF)
            vT_ref[cc, lo + HEAD_DIM:lo + VT_ROWS, :] = jnp.ones((VT_ROWS - HEAD_DIM, ATT_TK), BF)


def _inproj_sample_kernel(x_ref, sc_ref, sh_ref, g1_ref, wn_ref, wt_ref, wconv_ref, cs128_ref, cs64_ref,
                          prow_ref, pcol_ref, s0_ref, s1_ref, s2_ref,
                          qkvn_ref, z_ref, vnat_ref, knat_ref, slab_ref, raw_ref, misc_ref, aq_ref, iq_ref,
                          rt_ref):
    nat, _ = _project(x_ref, sc_ref, sh_ref, g1_ref, wn_ref, wt_ref, cs128_ref, cs64_ref, pcol_ref, rt_ref)
    raw = nat[:, 0:QKV_W]
    raw_ref[...] = raw
    conv = wconv_ref[0:1, :] * s0_ref[...]
    conv = conv + wconv_ref[1:2, :] * s1_ref[...]
    conv = conv + wconv_ref[2:3, :] * s2_ref[...]
    conv = conv + wconv_ref[3:4, :] * raw
    _qkv_post(conv, qkvn_ref)
    z_ref[...] = nat[:, QKV_W:QKV_W + GDN_W]
    vnat_ref[...] = nat[:, QKV_W + GDN_W:QKV_W + GDN_W + KV_W]
    misc_ref[...] = _misc_natural(nat[:, NAT_W - LANES:], prow_ref)
    aq_ref[...] = rt_ref[T_AQ:T_AQ + ATT_W, :].T
    iq_ref[...] = rt_ref[T_IQ:T_IQ + IDX_W, :].T
    knat_ref[...] = rt_ref[T_AK:T_AK + KV_W, :].T
    slab_ref[...] = rt_ref[T_IK:T_IK + LANES, :].T


def _inproj_weights(w_in, a_log, dt_bias):
    offs = np.cumsum([0, QKV_W, GDN_W, GDN_HEADS, GDN_HEADS, ATT_W, KV_W, KV_W, IDX_W, IDX_DIM, IDX_HEADS])
    qkv, z, beta, a, aq, ak, av, iq, ik, iw = [w_in[:, offs[n]:offs[n + 1]] for n in range(10)]
    d = w_in.shape[0]
    pad = jnp.zeros((d, LANES - 3 * GDN_HEADS), w_in.dtype)
    w_nat = jnp.concatenate([qkv, z, av, beta, a, a, pad], axis=1).astype(BF)
    w_t = jnp.concatenate([aq, ak, iq, ik, iw, beta, a, av], axis=1).T.astype(BF)
    zrow = jnp.zeros((LANES - 3 * GDN_HEADS,), F32)
    prow = jnp.stack([jnp.concatenate([jnp.zeros((GDN_HEADS,), F32), a_log, a_log, zrow]),
                      jnp.concatenate([jnp.zeros((GDN_HEADS,), F32), dt_bias, dt_bias, zrow])])
    z12 = jnp.zeros((IDX_HEADS + GDN_HEADS,), F32)
    pcol = jnp.stack([jnp.concatenate([z12, a_log]), jnp.concatenate([z12, dt_bias])], axis=1)
    return w_nat, w_t, prow, pcol


def _rope_tables(pos, dim):
    half = dim // 8
    inv = ROPE_THETA ** (-jnp.arange(half, dtype=F32) / half)
    ang = pos.astype(F32)[:, None] * inv[None, :]
    return jnp.concatenate([jnp.cos(ang).T, jnp.sin(ang).T], axis=0)


def _inproj_prompt(x, mod, g1, w_nat, w_t, w_conv, prow, pcol, tm):
    bp, tp, d = x.shape
    nt = tp // tm
    rows = bp * tp
    pos = jnp.arange(tp, dtype=I32)
    cs128, cs64 = _rope_tables(pos, HEAD_DIM), _rope_tables(pos, IDX_DIM)
    ri = np.arange(tm)
    tril = jnp.asarray((ri[:, None] // GDN_CHUNK == ri[None, :] // GDN_CHUNK) & (ri[None, :] <= ri[:, None]), BF)
    const = lambda shape: pl.BlockSpec(shape, lambda b, i: (0,) * len(shape))
    rowblk = lambda w: pl.BlockSpec((tm, w), lambda b, i: (b * nt + i, 0))
    colblk = lambda h: pl.BlockSpec((h, tm), lambda b, i: (0, b * nt + i))
    in_specs = [
        pl.BlockSpec((None, tm, d), lambda b, i: (b, i, 0)),
        pl.BlockSpec((None, 1, d), lambda b, i: (b, 0, 1)),
        pl.BlockSpec((None, 1, d), lambda b, i: (b, 0, 0)),
        const((1, d)), const((d, NAT_W)), const((T_ROWS, d)), const((CONV_W, QKV_W)),
        pl.BlockSpec((HEAD_DIM // 4, tm), lambda b, i: (0, i)),
        pl.BlockSpec((IDX_DIM // 4, tm), lambda b, i: (0, i)),
        const((2, LANES)), const((MISC_ROWS, 2)), const((tm, tm)),
    ]
    out_shape = [
        jax.ShapeDtypeStruct((rows, QKV_W), F32), jax.ShapeDtypeStruct((rows, GDN_W), F32),
        jax.ShapeDtypeStruct((rows * N_KV, HEAD_DIM), F32), jax.ShapeDtypeStruct((rows * N_KV, HEAD_DIM), F32),
        jax.ShapeDtypeStruct((bp, IDX_DIM, tp), F32), jax.ShapeDtypeStruct((bp, 8, QKV_W), F32),
        jax.ShapeDtypeStruct((rows, LANES), F32), jax.ShapeDtypeStruct((rows // GDN_CHUNK, 8, GDN_CHUNK), F32),
        jax.ShapeDtypeStruct((ATT_W, rows), BF), jax.ShapeDtypeStruct((IDX_W, rows), BF),
        jax.ShapeDtypeStruct((IDX_HEADS, rows), F32), jax.ShapeDtypeStruct((rows, KV_W), BF),
        jax.ShapeDtypeStruct((rows // ATT_TK, N_KV * VT_ROWS, ATT_TK), BF), jax.ShapeDtypeStruct((rows, IDX_DIM), BF),
    ]
    out_specs = [
        rowblk(QKV_W), rowblk(GDN_W),
        pl.BlockSpec((tm * N_KV, HEAD_DIM), lambda b, i: (b * nt + i, 0)),
        pl.BlockSpec((tm * N_KV, HEAD_DIM), lambda b, i: (b * nt + i, 0)),
        pl.BlockSpec((None, IDX_DIM, tm), lambda b, i: (b, 0, i)),
        pl.BlockSpec((None, 8, QKV_W), lambda b, i: (b, 0, 0)),
        rowblk(LANES),
        pl.BlockSpec((tm // GDN_CHUNK, 8, GDN_CHUNK), lambda b, i: (b * nt + i, 0, 0)),
        colblk(ATT_W), colblk(IDX_W), colblk(IDX_HEADS), rowblk(KV_W),
        pl.BlockSpec((tm // ATT_TK, N_KV * VT_ROWS, ATT_TK), lambda b, i: (b * nt + i, 0, 0)),
        rowblk(IDX_DIM),
    ]
    return pl.pallas_call(
        functools.partial(_inproj_prompt_kernel, tm, tm // _pick(tm, (ATT_TK,))),
        grid=(bp, nt), in_specs=in_specs, out_specs=out_specs, out_shape=out_shape,
        scratch_shapes=[pltpu.VMEM((tm + 8, QKV_W), F32), pltpu.VMEM((T_ROWS, tm), F32)],
        compiler_params=_params(("arbitrary", "arbitrary")),
        name="inproj_prompt",
    )(x, mod, mod, g1.reshape(1, d), w_nat, w_t, w_conv, cs128, cs64, prow, pcol, tril)


def _inproj_sample(x, mod, g1, w_nat, w_t, w_conv, prow, pcol, conv_state, past):
    db, d = x.shape
    pos = jnp.full((db,), past, I32)
    cs128, cs64 = _rope_tables(pos, HEAD_DIM), _rope_tables(pos, IDX_DIM)
    full = lambda shape: pl.BlockSpec(shape, lambda i: (0,) * len(shape))
    in_specs = [
        full((db, d)),
        pl.BlockSpec((db, d), lambda i: (0, 1)), pl.BlockSpec((db, d), lambda i: (0, 0)),
        full((1, d)), full((d, NAT_W)), full((T_ROWS, d)), full((CONV_W, QKV_W)),
        full((HEAD_DIM // 4, db)), full((IDX_DIM // 4, db)), full((2, LANES)), full((MISC_ROWS, 2)),
        full((db, QKV_W)), full((db, QKV_W)), full((db, QKV_W)),
    ]
    widths = [QKV_W, GDN_W, KV_W, KV_W, LANES, QKV_W, LANES, ATT_W, IDX_W]
    return pl.pallas_call(
        _inproj_sample_kernel,
        grid=(1,), in_specs=in_specs,
        out_specs=[full((db, w)) for w in widths],
        out_shape=[jax.ShapeDtypeStruct((db, w), F32) for w in widths],
        scratch_shapes=[pltpu.VMEM((T_ROWS, db), F32)],
        compiler_params=_params(("arbitrary",)),
        name="inproj_sample",
    )(x, mod, mod, g1.reshape(1, d), w_nat, w_t, w_conv, cs128, cs64, prow, pcol,
      conv_state[:, 0], conv_state[:, 1], conv_state[:, 2])


def _unit_lower_inverses(lows):
    c = lows[0].shape[0]
    ii = lax.broadcasted_iota(I32, (c, c), 0)
    jj = lax.broadcasted_iota(I32, (c, c), 1)
    eye = jnp.where(ii == jj, 1.0, 0.0)
    levels = int(math.log2(c)) - 1
    ts = [eye - low for low in lows]
    ps = [_mm_hi(low, low) for low in lows]
    for lvl in range(levels):
        if lvl == levels - 1:
            ts = [t + _mm_hi(t, p) for t, p in zip(ts, ps)]
        else:
            both = [_mm_hi(jnp.concatenate([t, p], axis=0), p) for t, p in zip(ts, ps)]
            ts = [t + b[0:c] for t, b in zip(ts, both)]
            ps = [b[c:] for b in both]
    return ts


def _gdn_prompt_kernel(tg, qkvn_ref, z_ref, misc_ref, gct_ref, gn_ref, o_ref, ssm_ref, s_ref):
    i = pl.program_id(1)

    @pl.when(i == 0)
    def _():
        s_ref[...] = jnp.zeros(s_ref.shape, F32)

    c = GDN_CHUNK
    ii = lax.broadcasted_iota(I32, (c, c), 0)
    jj = lax.broadcasted_iota(I32, (c, c), 1)
    pairs = [(cc, hd) for cc in range(tg // c) for hd in range(GDN_HEADS)]

    qs, ks, gcs, rhs, lows, intras = {}, {}, {}, {}, [], {}
    for cc, hd in pairs:
        r0, lo = cc * c, hd * HEAD_DIM
        q = qkvn_ref[r0:r0 + c, lo:lo + HEAD_DIM]
        k = qkvn_ref[r0:r0 + c, GDN_W + lo:GDN_W + lo + HEAD_DIM]
        v = qkvn_ref[r0:r0 + c, 2 * GDN_W + lo:2 * GDN_W + lo + HEAD_DIM]
        beta_c = misc_ref[r0:r0 + c, hd:hd + 1]
        gc_c = misc_ref[r0:r0 + c, 2 * GDN_HEADS + hd:2 * GDN_HEADS + hd + 1]
        gc_r = gct_ref[cc, GDN_HEADS + hd:GDN_HEADS + hd + 1, :]
        decay = jnp.where(ii >= jj, jnp.exp(jnp.where(ii >= jj, gc_c - gc_r, 0.0)), 0.0)
        kb = k * beta_c
        both = _dot_nt(jnp.concatenate([kb, q], axis=0).astype(BF), k.astype(BF))
        lows.append(jnp.where(ii > jj, both[0:c] * decay, 0.0))
        intras[cc, hd] = (both[c:] * decay).astype(BF)
        rhs[cc, hd] = jnp.concatenate([v * beta_c, kb * jnp.exp(gc_c)], axis=1).astype(BF)
        qs[cc, hd], ks[cc, hd], gcs[cc, hd] = q, k, gc_c
    ts = _unit_lower_inverses(lows)
    uws = {p: _dot(t.astype(BF), rhs[p]) for p, t in zip(pairs, ts)}

    for cc, hd in pairs:
        r0, lo = cc * c, hd * HEAD_DIM
        q, k, gc_c, uw = qs[cc, hd], ks[cc, hd], gcs[cc, hd], uws[cc, hd]
        s = s_ref[hd]
        ws = _dot(jnp.concatenate([uw[:, HEAD_DIM:], q * jnp.exp(gc_c)], axis=0).astype(BF), s.astype(BF))
        v_new = (uw[:, 0:HEAD_DIM] - ws[0:c]).astype(BF)
        o = ws[c:] + _dot(intras[cc, hd], v_new)
        g_last = gc_c[c - 1:c, :]
        kd = k * jnp.exp(g_last - gc_c)
        s_ref[hd] = s * jnp.exp(g_last) + _dot(kd.T.astype(BF), v_new)
        o = _rmsnorm(o, gn_ref[...]) * _silu(z_ref[r0:r0 + c, lo:lo + HEAD_DIM])
        o_ref[r0:r0 + c, lo:lo + HEAD_DIM] = o.astype(BF)
    ssm_ref[...] = s_ref[...]


def _gdn_prompt(qkvn, z, misc, gct, g_norm, bp, tp, tg):
    nt = tp // tg
    rows = bp * tp
    rowblk = lambda w: pl.BlockSpec((tg, w), lambda b, i: (b * nt + i, 0))
    return pl.pallas_call(
        functools.partial(_gdn_prompt_kernel, tg),
        grid=(bp, nt),
        in_specs=[rowblk(QKV_W), rowblk(GDN_W), rowblk(LANES),
                  pl.BlockSpec((tg // GDN_CHUNK, 8, GDN_CHUNK), lambda b, i: (b * nt + i, 0, 0)),
                  pl.BlockSpec((1, HEAD_DIM), lambda b, i: (0, 0))],
        out_specs=[rowblk(GDN_W),
                   pl.BlockSpec((None, GDN_HEADS, HEAD_DIM, HEAD_DIM), lambda b, i: (b, 0, 0, 0))],
        out_shape=[jax.ShapeDtypeStruct((rows, GDN_W), BF),
                   jax.ShapeDtypeStruct((bp, GDN_HEADS, HEAD_DIM, HEAD_DIM), F32)],
        scratch_shapes=[pltpu.VMEM((GDN_HEADS, HEAD_DIM, HEAD_DIM), F32)],
        compiler_params=_params(("arbitrary", "arbitrary")),
        name="gdn_prompt",
    )(qkvn, z, misc, gct, g_norm.reshape(1, HEAD_DIM))


def _gdn_sample_kernel(nb, qkvn_ref, z_ref, misc_ref, gn_ref, s_ref, o_ref, ssm_ref):
    for bi in range(nb):
        for hd in range(GDN_HEADS):
            lo = hd * HEAD_DIM
            q = qkvn_ref[bi:bi + 1, lo:lo + HEAD_DIM]
            k = qkvn_ref[bi:bi + 1, GDN_W + lo:GDN_W + lo + HEAD_DIM]
            v = qkvn_ref[bi:bi + 1, 2 * GDN_W + lo:2 * GDN_W + lo + HEAD_DIM]
            beta = misc_ref[bi:bi + 1, hd:hd + 1]
            g = misc_ref[bi:bi + 1, GDN_HEADS + hd:GDN_HEADS + hd + 1]
            kcol = jnp.broadcast_to(k, (HEAD_DIM, HEAD_DIM)).T
            qcol = jnp.broadcast_to(q, (HEAD_DIM, HEAD_DIM)).T
            s = s_ref[bi, hd] * jnp.exp(g)
            ks = jnp.sum(kcol * s, axis=0, keepdims=True)
            delta = (v - ks) * beta
            s = s + kcol * delta
            ssm_ref[bi, hd] = s
            o = jnp.sum(qcol * s, axis=0, keepdims=True)
            o = _rmsnorm(o, gn_ref[...]) * _silu(z_ref[bi:bi + 1, lo:lo + HEAD_DIM])
            o_ref[bi:bi + 1, lo:lo + HEAD_DIM] = o.astype(BF)


def _gdn_sample(qkvn, z, misc, g_norm, state, nb):
    db = qkvn.shape[0]
    rowblk = lambda w: pl.BlockSpec((nb, w), lambda i: (i, 0))
    sblk = pl.BlockSpec((nb, GDN_HEADS, HEAD_DIM, HEAD_DIM), lambda i: (i, 0, 0, 0))
    return pl.pallas_call(
        functools.partial(_gdn_sample_kernel, nb),
        grid=(db // nb,),
        in_specs=[rowblk(QKV_W), rowblk(GDN_W), rowblk(LANES), pl.BlockSpec((1, HEAD_DIM), lambda i: (0, 0)), sblk],
        out_specs=[rowblk(GDN_W), sblk],
        out_shape=[jax.ShapeDtypeStruct((db, GDN_W), BF), jax.ShapeDtypeStruct(state.shape, F32)],
        compiler_params=_params(("arbitrary",)),
        name="gdn_sample",
    )(qkvn, z, misc, g_norm.reshape(1, HEAD_DIM), state)


def _ordered_word_to_float(u):
    s = u ^ INT_MIN
    return lax.bitcast_convert_type(s ^ ((s >> 31) & np.int32(0x7FFFFFFF)), F32)


def _select_topk_bias(i_ref, nch, tk, k_sel, taken=0.0, dropped=NEG_INF, group=1):
    nl = i_ref.shape[1]

    def chunk(c):
        return pl.ds(pl.multiple_of(c * tk, tk), tk)

    def count(ref, pred):
        def body(c, acc):
            m = pred(ref[chunk(c), :])
            ones = jnp.where(m, 1, 0).astype(I32).reshape(tk // 8, 8, nl)
            parts = [ones[n] for n in range(tk // 8)]
            while len(parts) > 1:
                parts = [a + b for a, b in zip(parts[0::2], parts[1::2])] + parts[len(parts) & ~1:]
            return acc + parts[0]
        acc = lax.fori_loop(0, nch, body, jnp.zeros((8, nl), I32))
        return jnp.sum(acc, axis=0, keepdims=True)

    def bit_body(i, carry):
        t_u, n_ge = carry
        cand_u = t_u | jnp.left_shift(jnp.int32(1), 31 - i)
        cand = _ordered_word_to_float(cand_u)
        cnt = count(i_ref, lambda sc: sc >= cand)
        take = cnt >= k_sel
        return jnp.where(take, cand_u, t_u), jnp.where(take, cnt, n_ge)

    t_u, n_ge = lax.fori_loop(0, 32, bit_body, (jnp.zeros((1, nl), I32), jnp.full((1, nl), -1, I32)))
    thr = jnp.where((t_u >> 23) == 0, NEG_INF, _ordered_word_to_float(t_u))

    excess = (n_ge - k_sel).astype(F32)
    ri = lax.broadcasted_iota(I32, (tk + 8, tk), 0)
    ci = lax.broadcasted_iota(I32, (tk + 8, tk), 1)
    above = jnp.where((ci > ri) | (ri >= tk), 1.0, 0.0).astype(BF)

    nsteps = (nch + group - 1) // group

    def write(n, later):
        scs = [i_ref[chunk((nsteps - 1 - n) * group + g), :] for g in reversed(range(group))]
        ties = [sc == thr for sc in scs]
        cnts = [_dot(above, jnp.where(t, 1.0, 0.0).astype(BF)) for t in ties]
        for g, sc, tie, cnt in zip(reversed(range(group)), scs, ties, cnts):
            sel = (sc > thr) | (tie & (cnt[0:tk] + later >= excess))
            i_ref[chunk((nsteps - 1 - n) * group + g), :] = jnp.where(sel & (sc > NEG_INF), taken, dropped)
            later = later + cnt[tk:tk + 1]
        return later

    lax.fori_loop(0, nsteps, write, jnp.zeros((1, nl), F32))


def _attn_prompt_kernel(tq, tk, blk, k_sel, qT_ref, iqT_ref, iwT_ref, kbf_ref, vT_ref, ikbf_ref, o_ref, i_ref):
    t0 = pl.program_id(1) * tq
    nch = (t0 + tq + tk - 1) // tk
    nblk = (t0 + tq + blk - 1) // blk
    qpos = t0 + lax.broadcasted_iota(I32, (1, tq), 1)
    row = lax.broadcasted_iota(I32, (tk, 1), 0)
    w = iwT_ref[...]
    pairs = IDX_HEADS // 2
    rhs = [jnp.concatenate([iqT_ref[(2 * p) * IDX_DIM:(2 * p + 1) * IDX_DIM, :],
                            iqT_ref[(2 * p + 1) * IDX_DIM:(2 * p + 2) * IDX_DIM, :]], axis=1) for p in range(pairs)]

    def chunk(c):
        return pl.ds(pl.multiple_of(c * tk, tk), tk)

    def idx_body(c, carry):
        ikc = ikbf_ref[chunk(c), :]
        acc = jnp.zeros((tk, tq), F32)
        for p in range(pairs):
            d = jnp.maximum(_dot(ikc, rhs[p]), 0.0)
            acc = acc + d[:, 0:tq] * w[2 * p:2 * p + 1, :] + d[:, tq:] * w[2 * p + 1:2 * p + 2, :]
        i_ref[chunk(c), :] = jnp.where(c * tk + row <= qpos, acc, NEG_INF)
        return carry

    lax.fori_loop(0, nch, idx_body, 0)

    def fill_body(c, carry):
        i_ref[chunk(c), :] = jnp.full((tk, tq), NEG_INF, F32)
        return carry

    lax.fori_loop(nch, nblk * (blk // tk), fill_body, 0)
    tks = max(tk * LANES // tq, 8)
    _select_topk_bias(i_ref, (t0 + tq + tks - 1) // tks, tks, k_sel, group=min(4, blk // tks))

    scale2 = HEAD_DIM ** -0.5 * math.log2(math.e)
    rep = ATT_HEADS // N_KV
    sub = ATT_TK
    nsub = blk // sub
    qg = [jnp.concatenate([qT_ref[(rep * g + r) * HEAD_DIM:(rep * g + r + 1) * HEAD_DIM, :] for r in range(rep)],
                          axis=1) for g in range(N_KV)]

    def body(c, carry):
        out = []
        for g in range(N_KV):
            m, acc = carry[g]
            ss = []
            for i in range(nsub):
                rows = pl.ds(pl.multiple_of(c * blk + i * sub, sub), sub)
                bias = i_ref[rows, :]
                ss.append(_dot(kbf_ref[rows, g * HEAD_DIM:(g + 1) * HEAD_DIM], qg[g]) * scale2
                          + jnp.concatenate([bias] * rep, axis=1))
            m_new = m
            for s in ss:
                m_new = jnp.maximum(m_new, jnp.max(s, axis=0, keepdims=True))
            m_safe = jnp.where(m_new == NEG_INF, 0.0, m_new)
            acc = acc * jnp.exp2(m - m_safe)
            for i, s in enumerate(ss):
                p = jnp.exp2(s - m_safe).astype(BF)
                acc = acc + _dot(vT_ref[c * nsub + i, g * VT_ROWS:(g + 1) * VT_ROWS, :], p)
            out.append((m_new, acc))
        return tuple(out)

    init = tuple((jnp.full((1, rep * tq), NEG_INF, F32), jnp.zeros((VT_ROWS, rep * tq), F32)) for _ in range(N_KV))
    res = lax.fori_loop(0, nblk, body, init)
    for g in range(N_KV):
        acc = res[g][1]
        o_t = acc[0:HEAD_DIM] / acc[HEAD_DIM:HEAD_DIM + 1]
        for r in range(rep):
            hd = rep * g + r
            o_ref[:, hd * HEAD_DIM:(hd + 1) * HEAD_DIM] = o_t[:, r * tq:(r + 1) * tq].T.astype(BF)


def _attn_prompt(qT, iqT, iwT, kbf, vT, ikbf, bp, tp):
    tq = _pick(tp, (ATT_TQ, QBLOCK))
    tk = _pick(tp, (IDX_TK, ATT_TK))
    blk = _pick(tp, (ATT_BLK, IDX_TK, ATT_TK))
    assert blk % tk == 0 and blk % ATT_TK == 0
    nq = tp // tq
    rows = bp * tp
    k_sel = min(TOPK_MAX, tp // 4)
    colblk = lambda h: pl.BlockSpec((h, tq), lambda b, j: (0, b * nq + j))
    return pl.pallas_call(
        functools.partial(_attn_prompt_kernel, tq, tk, blk, k_sel),
        grid=(bp, nq),
        in_specs=[colblk(ATT_W), colblk(IDX_W), colblk(IDX_HEADS),
                  pl.BlockSpec((tp, KV_W), lambda b, j: (b, 0)),
                  pl.BlockSpec((tp // ATT_TK, N_KV * VT_ROWS, ATT_TK), lambda b, j: (b, 0, 0)),
                  pl.BlockSpec((tp, IDX_DIM), lambda b, j: (b, 0))],
        out_specs=pl.BlockSpec((tq, ATT_W), lambda b, j: (b * nq + j, 0)),
        out_shape=jax.ShapeDtypeStruct((rows, ATT_W), BF),
        scratch_shapes=[pltpu.VMEM((tp, tq), F32)],
        compiler_params=_params(("arbitrary", "arbitrary")),
        name="attn_prompt",
    )(qT, iqT, iwT, kbf, vT, ikbf)


def _page_prefetch(npg, pt_ref, srcs, bufs, sem_ref):
    b = pl.program_id(0)
    steps = pl.num_programs(0)
    ring = bufs[0].shape[0]
    ahead = ring - 1
    slot = lax.rem(b, ring)

    def copy(n, step, sl, p):
        return pltpu.make_async_copy(srcs[n].at[pt_ref[step, p]], bufs[n].at[sl, p], sem_ref.at[n, sl, p])

    def start_all(step, sl):
        for n in range(len(srcs)):
            for p in range(npg):
                copy(n, step, sl, p).start(priority=p % 2)

    @pl.when(b == 0)
    def _():
        for step in range(ahead):
            start_all(step, step)

    @pl.when(b + ahead < steps)
    def _():
        start_all(b + ahead, lax.rem(b + ahead, ring))

    for n in range(len(srcs)):
        for p in range(npg):
            copy(n, b, slot, p).wait()
    return slot


def _sidx_kernel(npg, pt_ref, iq_ref, iw_ref, ikn_ref, cache_ref, out_ref, buf_ref, sem_ref):
    slot = _page_prefetch(npg, pt_ref, [cache_ref], [buf_ref], sem_ref)
    iq = iq_ref[...].astype(BF)
    w = iw_ref[...]
    for p in range(npg):
        d = jnp.maximum(_dot(iq, buf_ref[slot, p].astype(BF)), 0.0)
        out_ref[:, p * LANES:(p + 1) * LANES] = jnp.sum(d * w, axis=0, keepdims=True)
    dn = jnp.sum(iq.astype(F32) * ikn_ref[...].astype(BF).astype(F32), axis=1, keepdims=True)
    sn = jnp.sum(jnp.maximum(dn, 0.0) * w, axis=0, keepdims=True)
    lane = lax.broadcasted_iota(I32, (1, LANES), 1)
    out_ref[:, npg * LANES:(npg + 1) * LANES] = jnp.where(lane == 0, sn, NEG_INF)


def _sample_index_scores(iq, iw, ik_new, cache_ik_t, page_table):
    db, npg = page_table.shape
    page = cache_ik_t.shape[2]
    assert page == LANES and db > IDX_RING
    width = (npg + 1) * LANES
    grid_spec = pltpu.PrefetchScalarGridSpec(
        num_scalar_prefetch=1, grid=(db,),
        in_specs=[pl.BlockSpec((None, IDX_HEADS, IDX_DIM), lambda b, pt: (b, 0, 0)),
                  pl.BlockSpec((None, IDX_HEADS, 1), lambda b, pt: (b, 0, 0)),
                  pl.BlockSpec((None, 1, IDX_DIM), lambda b, pt: (b, 0, 0)),
                  pl.BlockSpec(memory_space=pl.ANY)],
        out_specs=pl.BlockSpec((None, 1, width), lambda b, pt: (b, 0, 0)),
        scratch_shapes=[pltpu.VMEM((IDX_RING, npg, IDX_DIM, page), F32),
                        pltpu.SemaphoreType.DMA((1, IDX_RING, npg))])
    out = pl.pallas_call(
        functools.partial(_sidx_kernel, npg), grid_spec=grid_spec,
        out_shape=jax.ShapeDtypeStruct((db, 1, width), F32),
        compiler_params=_params(("arbitrary",)),
        name="sample_index_scores",
    )(page_table, iq.reshape(db, IDX_HEADS, IDX_DIM), iw.reshape(db, IDX_HEADS, 1), ik_new.reshape(db, 1, IDX_DIM),
      cache_ik_t)
    return out.reshape(db, width)


def _ssel_kernel(nch, k_sel, s_ref, spread_ref, keep_ref, i_ref):
    for p in range(nch):
        i_ref[p * LANES:(p + 1) * LANES, :] = s_ref[:, p * LANES:(p + 1) * LANES].T
    _select_topk_bias(i_ref, nch, LANES, k_sel, taken=1.0, dropped=0.0)
    rows = N_KV * LANES
    for p in range(nch - 1):
        k2 = _dot(spread_ref[...], i_ref[p * LANES:(p + 1) * LANES, :].astype(BF))
        for h in range(N_KV):
            keep_ref[:, p * rows + h * LANES:p * rows + (h + 1) * LANES] = k2[h * LANES:(h + 1) * LANES, :].T
    keep_ref[:, (nch - 1) * rows:(nch - 1) * rows + LANES] = i_ref[(nch - 1) * LANES:nch * LANES, :].T


def _sample_select(scores, n_keys):
    db, width = scores.shape
    assert db == LANES
    nch = width // LANES
    k_sel = min(TOPK_MAX, n_keys // 4)
    rows = N_KV * LANES
    out_w = (nch - 1) * rows + LANES
    spread = jnp.asarray(np.arange(rows)[:, None] // N_KV == np.arange(LANES)[None, :], BF)
    return pl.pallas_call(
        functools.partial(_ssel_kernel, nch, k_sel),
        grid=(1,),
        in_specs=[pl.BlockSpec((db, width), lambda i: (0, 0)), pl.BlockSpec((rows, LANES), lambda i: (0, 0))],
        out_specs=pl.BlockSpec((db, out_w), lambda i: (0, 0)),
        out_shape=jax.ShapeDtypeStruct((db, out_w), F32),
        scratch_shapes=[pltpu.VMEM((width, db), F32)],
        compiler_params=_params(("arbitrary",)),
        name="sample_select",
    )(scores, spread)


def _sattn_kernel(npg, pt_ref, q_ref, keep_ref, kn_ref, vn_ref, ck_ref, cv_ref, o_ref, kbuf_ref, vbuf_ref, sem_ref):
    slot = _page_prefetch(npg, pt_ref, [ck_ref, cv_ref], [kbuf_ref, vbuf_ref], sem_ref)
    rep = ATT_HEADS // N_KV
    rows = N_KV * LANES
    scale = HEAD_DIM ** -0.5
    q = q_ref[...]
    q8 = jnp.concatenate([q, jnp.zeros((8 - ATT_HEADS, HEAD_DIM), F32)], axis=0).astype(BF)
    hrow = lax.broadcasted_iota(I32, (8, 1), 0)
    lane = lax.broadcasted_iota(I32, (1, LANES), 1)
    own_kv = lax.broadcasted_iota(I32, (1, rows), 1) % N_KV == hrow // rep
    parts = []
    for p in range(npg):
        s2 = _dot_nt(q8, kbuf_ref[slot, p].astype(BF))
        keep2 = keep_ref[:, p * rows:(p + 1) * rows]
        parts.append(jnp.where(own_kv & (keep2 > 0.5), s2 * scale, NEG_INF))
    kn = kn_ref[...].astype(BF).astype(F32)
    kn8 = jnp.where(hrow // rep == 0, kn[0:1, :], kn[1:2, :])
    s_new = jnp.sum(q8.astype(F32) * kn8, axis=1, keepdims=True)
    keep_new = keep_ref[:, npg * rows:npg * rows + 1]
    parts.append(jnp.where((lane == 0) & (keep_new > 0.5), s_new * scale, NEG_INF))
    s = jnp.concatenate(parts, axis=1)
    m = jnp.max(s, axis=1, keepdims=True)
    e = jnp.exp(s - m)
    pr = (e / jnp.sum(e, axis=1, keepdims=True)).astype(BF)
    o8 = jnp.zeros((8, HEAD_DIM), F32)
    for p in range(npg):
        o8 = o8 + _dot(pr[:, p * rows:(p + 1) * rows], vbuf_ref[slot, p].astype(BF))
    vn = vn_ref[...].astype(BF).astype(F32)
    vn8 = jnp.where(hrow // rep == 0, vn[0:1, :], vn[1:2, :])
    o8 = o8 + pr[:, npg * rows:npg * rows + 1].astype(F32) * vn8
    for hd in range(ATT_HEADS):
        o_ref[:, hd * HEAD_DIM:(hd + 1) * HEAD_DIM] = o8[hd:hd + 1, :].astype(BF)


def _sample_attention(q, keep, k_new, v_new, cache_k, cache_v, page_table):
    db, npg = page_table.shape
    n_phys, page = cache_k.shape[0], cache_k.shape[1]
    assert page == LANES and N_KV == 2 and db > KV_RING
    rows = page * N_KV
    width = npg * rows + LANES
    ck = cache_k.reshape(n_phys, rows, HEAD_DIM)
    cv = cache_v.reshape(n_phys, rows, HEAD_DIM)
    grid_spec = pltpu.PrefetchScalarGridSpec(
        num_scalar_prefetch=1, grid=(db,),
        in_specs=[pl.BlockSpec((None, ATT_HEADS, HEAD_DIM), lambda b, pt: (b, 0, 0)),
                  pl.BlockSpec((None, 1, width), lambda b, pt: (b, 0, 0)),
                  pl.BlockSpec((None, N_KV, HEAD_DIM), lambda b, pt: (b, 0, 0)),
                  pl.BlockSpec((None, N_KV, HEAD_DIM), lambda b, pt: (b, 0, 0)),
                  pl.BlockSpec(memory_space=pl.ANY), pl.BlockSpec(memory_space=pl.ANY)],
        out_specs=pl.BlockSpec((None, 1, ATT_W), lambda b, pt: (b, 0, 0)),
        scratch_shapes=[pltpu.VMEM((KV_RING, npg, rows, HEAD_DIM), F32), pltpu.VMEM((KV_RING, npg, rows, HEAD_DIM), F32),
                        pltpu.SemaphoreType.DMA((2, KV_RING, npg))])
    out = pl.pallas_call(
        functools.partial(_sattn_kernel, npg), grid_spec=grid_spec,
        out_shape=jax.ShapeDtypeStruct((db, 1, ATT_W), BF),
        compiler_params=_params(("arbitrary",)),
        name="sample_attention",
    )(page_table, q.reshape(db, ATT_HEADS, HEAD_DIM), keep.reshape(db, 1, width),
      k_new.reshape(db, N_KV, HEAD_DIM), v_new.reshape(db, N_KV, HEAD_DIM), ck, cv)
    return out.reshape(db, ATT_W)


def _post_kernel(final, nff, og_ref, oa_ref, x_ref, ga1_ref, sc2_ref, sh2_ref, ga2_ref, g2_ref, gf_ref,
                 wo_ref, wg_ref, wu_ref, wd_ref, y_ref, x1_ref, h2_ref, acc_ref):
    jf = pl.program_id(1)

    @pl.when(jf == 0)
    def _():
        mixed = _dot(og_ref[...], wo_ref[0:GDN_W, :]) + _dot(oa_ref[...], wo_ref[GDN_W:, :])
        x1 = x_ref[...] + ga1_ref[...] * mixed
        x1_ref[...] = x1
        h2_ref[...] = (_rmsnorm(x1, g2_ref[...]) * (1.0 + sc2_ref[...]) + sh2_ref[...]).astype(BF)
        acc_ref[...] = jnp.zeros(acc_ref.shape, F32)

    h2 = h2_ref[...]
    act = (_silu(_dot(h2, wg_ref[...])) * _dot(h2, wu_ref[...])).astype(BF)
    acc_ref[...] += _dot(act, wd_ref[...])

    @pl.when(jf == nff - 1)
    def _():
        x2 = x1_ref[...] + ga2_ref[...] * acc_ref[...]
        y_ref[...] = _rmsnorm(x2, gf_ref[...]) if final else x2


def _post(og, oa, x2d, mod, mod_spec, g2, gf, wo, wfi, wfo, tm, tf, final):
    rows, d = x2d.shape
    dff = wfo.shape[0]
    nff = dff // tf
    rowblk = lambda w: pl.BlockSpec((tm, w), lambda i, j: (i, 0))
    const = lambda shape: pl.BlockSpec(shape, lambda i, j: (0,) * len(shape))
    in_specs = [rowblk(GDN_W), rowblk(ATT_W), rowblk(d),
                mod_spec(2), mod_spec(4), mod_spec(3), mod_spec(5),
                const((1, d)), const((1, d)), const((d, d)),
                pl.BlockSpec((d, tf), lambda i, j: (0, j)),
                pl.BlockSpec((d, tf), lambda i, j: (0, nff + j)),
                pl.BlockSpec((tf, d), lambda i, j: (j, 0))]
    return pl.pallas_call(
        functools.partial(_post_kernel, final, nff),
        grid=(rows // tm, nff), in_specs=in_specs, out_specs=rowblk(d),
        out_shape=jax.ShapeDtypeStruct((rows, d), F32),
        scratch_shapes=[pltpu.VMEM((tm, d), F32), pltpu.VMEM((tm, d), BF), pltpu.VMEM((tm, d), F32)],
        compiler_params=_params(("arbitrary", "arbitrary")),
        name="post",
    )(og, oa, x2d, mod, mod, mod, mod, g2.reshape(1, d), gf.reshape(1, d), wo, wfi, wfi, wfo)


def _pick(n, prefs):
    for p in prefs:
        if n % p == 0:
            return p
    return n


def kernel(x_prompt, x_sample, c_prompt, c_sample, cache_k, cache_v, cache_idx_k, page_table, state_conv, state_ssm,
           w_ada, b_ada, g_norm1, w_in, w_conv, a_log, dt_bias, g_gdn_norm, w_out, g_norm2, w_ffn_in, w_ffn_out,
           g_final):
    bp, tp, d = x_prompt.shape
    db, ts, _ = x_sample.shape
    assert ts == 1 and d == GDN_W + ATT_W and tp % max(ATT_TK, GDN_CHUNK) == 0
    depth = w_in.shape[0]
    npg, page = page_table.shape[1], cache_k.shape[2]
    past = npg * page
    tm_in = _pick(tp, (512, 256))
    tg = _pick(tp, (512, 256))
    tm_post = _pick(tp, (1024, 512, 256))
    dff = w_ffn_out.shape[1]
    tf = _pick(dff, (256, 128))

    xp = x_prompt
    xs = x_sample.reshape(db, d)
    npad = (-(bp + db)) % 8
    c_all = jnp.concatenate([c_prompt, c_sample, jnp.zeros((npad, d), F32)], axis=0)
    new_p, new_s = [], []
    for l in range(depth):
        final = l == depth - 1
        mod = _modulation(c_all, w_ada[l], b_ada[l])
        mod_p = mod[:bp].reshape(bp, 1, 6 * d)
        mod_s = mod[bp:bp + db]
        w_nat, w_t, prow, pcol = _inproj_weights(w_in[l], a_log[l], dt_bias[l])
        wo = w_out[l].astype(BF)
        wfi = w_ffn_in[l].astype(BF)
        wfo = w_ffn_out[l].astype(BF)

        (qkvn, z, vnat, knat, ikT, tail, misc, gct, qT, iqT, iwT, kbf, vT, ikbf) = _inproj_prompt(
            xp, mod_p, g_norm1[l], w_nat, w_t, w_conv[l], prow, pcol, tm_in)
        og, ssm_p = _gdn_prompt(qkvn, z, misc, gct, g_gdn_norm[l], bp, tp, tg)
        oa = _attn_prompt(qT, iqT, iwT, kbf, vT, ikbf, bp, tp)
        tiles_b = tp // tm_post
        spec_p = lambda k: pl.BlockSpec((None, 1, d), lambda i, j: (i // tiles_b, 0, k))
        xp = _post(og, oa, xp.reshape(bp * tp, d), mod_p, spec_p, g_norm2[l], g_final, wo, wfi, wfo,
                   tm_post, tf, final).reshape(bp, tp, d)
        new_p.append((knat.reshape(bp, tp, N_KV, HEAD_DIM), vnat.reshape(bp, tp, N_KV, HEAD_DIM),
                      jnp.swapaxes(ikT, 1, 2), tail[:, 8 - (CONV_W - 1):, :], ssm_p))

        (qkvn_s, z_s, vnat_s, knat_s, slab_s, raw_s, misc_s, aq_s, iq_s) = _inproj_sample(
            xs, mod_s, g_norm1[l], w_nat, w_t, w_conv[l], prow, pcol, state_conv[l], past)
        og_s, ssm_s = _gdn_sample(qkvn_s, z_s, misc_s, g_gdn_norm[l], state_ssm[l], _pick(db, (8,)))
        ik_s = slab_s[:, 0:IDX_DIM]
        iw_s = slab_s[:, IDX_DIM:IDX_DIM + IDX_HEADS]
        scores = _sample_index_scores(iq_s, iw_s, ik_s, jnp.swapaxes(cache_idx_k[l], 1, 2), page_table)
        keep = _sample_select(scores, past + ts)
        oa_s = _sample_attention(aq_s, keep, knat_s, vnat_s, cache_k[l], cache_v[l], page_table)
        spec_s = lambda k: pl.BlockSpec((db, d), lambda i, j: (i, k))
        xs = _post(og_s, oa_s, xs, mod_s, spec_s, g_norm2[l], g_final, wo, wfi, wfo, db, tf, final)
        new_conv_s = jnp.concatenate([state_conv[l][:, 1:, :], raw_s[:, None, :]], axis=1)
        new_s.append((knat_s.reshape(db, ts, N_KV, HEAD_DIM), vnat_s.reshape(db, ts, N_KV, HEAD_DIM),
                      ik_s.reshape(db, ts, IDX_DIM), new_conv_s, ssm_s))

    stack = lambda states, n: jnp.stack([s[n] for s in states], axis=0)
    return (xp, xs.reshape(db, ts, d),
            stack(new_p, 0), stack(new_p, 1), stack(new_p, 2), stack(new_p, 3), stack(new_p, 4),
            stack(new_s, 0), stack(new_s, 1), stack(new_s, 2), stack(new_s, 3), stack(new_s, 4))
```
